```python
import math
import jax, jax.numpy as jnp
from jax import lax
import numpy as np

D_MODEL = 1024
BATCH = 32
SEQ = 2048
DEPTH = 4

N_MIXERS = 2
N_CONV_LAYERS = (DEPTH + 1) // 2
N_LRU_LAYERS = DEPTH // 2
SC_WIDTH = 3
LRU_WIDTH = 1280
LRU_HEADS = 10
LRU_BLOCK = LRU_WIDTH // LRU_HEADS
LRU_CONV_WIDTH = 4
LRU_C = 8.0
FFN_HIDDEN = 2816
FFN_CONV_WIDTH = 3
LN_EPS = 1e-5
DEEPNORM_ALPHA = (2.0 * DEPTH) ** 0.25
DEEPNORM_BETA = (8.0 * DEPTH) ** -0.25

kernel_name = "hybrid_shortconv_rglru_convffn_deepnorm"


def causal_dwconv(x, w, b):
    k_width = w.shape[0]
    s = x.shape[1]
    xp = jnp.pad(x, ((0, 0), (k_width - 1, 0), (0, 0)))
    y = xp[:, 0:s] * w[0] + b
    for k in range(1, k_width):
        y = y + xp[:, k:k + s] * w[k]
    return y


def layer_norm(x, g, b):
    xf = x.astype(jnp.float32)
    mu = jnp.mean(xf, axis=-1, keepdims=True)
    var = jnp.mean(jnp.square(xf - mu), axis=-1, keepdims=True)
    y = (xf - mu) * lax.rsqrt(var + LN_EPS)
    return y.astype(x.dtype) * g + b


def short_conv_mixer(x, w_in, conv_w, conv_b, w_out):
    h = jnp.einsum('bsd,de->bse', x, w_in)
    gate_b, gate_c, v = jnp.split(h, 3, axis=-1)
    u = causal_dwconv(gate_c * v, conv_w, conv_b)
    return jnp.einsum('bsd,de->bse', gate_b * u, w_out)


def _lru_combine(left, right):
    a_l, b_l = left
    a_r, b_r = right
    return a_l * a_r, a_r * b_l + b_r


def rglru_block(x, w_in, b_in, conv_w, conv_b, w_gate, b_gate, lam, w_out):
    bsz, s, _ = x.shape
    h = jnp.einsum('bsd,de->bse', x, w_in) + b_in
    g_branch, r_branch = jnp.split(h, 2, axis=-1)
    xr = causal_dwconv(r_branch, conv_w, conv_b)
    xh = xr.reshape(bsz, s, LRU_HEADS, LRU_BLOCK)
    gates = jnp.einsum('bshi,hio->bsho', xh, w_gate) + b_gate
    r_gate, i_gate = jnp.split(gates.astype(jnp.float32), 2, axis=-1)
    r_gate = jax.nn.sigmoid(r_gate).reshape(bsz, s, LRU_WIDTH)
    i_gate = jax.nn.sigmoid(i_gate).reshape(bsz, s, LRU_WIDTH)
    log_a = -LRU_C * r_gate * jax.nn.softplus(-lam.astype(jnp.float32))
    a = jnp.exp(log_a)
    mult = jnp.sqrt(-jnp.expm1(2.0 * log_a))
    b = mult * (i_gate * xr.astype(jnp.float32))
    _, hs = lax.associative_scan(_lru_combine, (a, b), axis=1)
    y = hs.astype(x.dtype) * jax.nn.gelu(g_branch, approximate=True)
    return jnp.einsum('bsr,rd->bsd', y, w_out)


def conv_ffn(x, w_up, conv_w, conv_b, w_down):
    h = jnp.einsum('bsd,df->bsf', x, w_up)
    h = causal_dwconv(h, conv_w, conv_b)
    g, v = jnp.split(h, 2, axis=-1)
    return jnp.einsum('bsf,fd->bsd', jax.nn.silu(g) * v, w_down)


def _fwd_setup_inputs(seed: int = 0) -> dict:
    key = jax.random.key(seed)
    ks = jax.random.split(key, 24)
    d, r, f = D_MODEL, LRU_WIDTH, FFN_HIDDEN
    nA, nB, L = N_CONV_LAYERS, N_LRU_LAYERS, DEPTH
    nrm = jax.random.normal
    x = nrm(ks[0], (BATCH, SEQ, d), jnp.float32)
    sc_w_in = nrm(ks[1], (nA, d, 3 * d), jnp.float32) * d ** -0.5
    sc_conv_w = nrm(ks[2], (nA, SC_WIDTH, d), jnp.float32) * SC_WIDTH ** -0.5
    sc_conv_b = nrm(ks[3], (nA, d), jnp.float32) * 0.01
    sc_w_out = nrm(ks[4], (nA, d, d), jnp.float32) * d ** -0.5 * DEEPNORM_BETA
    lru_w_in = nrm(ks[5], (nB, d, 2 * r), jnp.float32) * d ** -0.5
    lru_b_in = nrm(ks[6], (nB, 2 * r), jnp.float32) * 0.01
    lru_conv_w = nrm(ks[7], (nB, LRU_CONV_WIDTH, r), jnp.float32) * LRU_CONV_WIDTH ** -0.5
    lru_conv_b = nrm(ks[8], (nB, r), jnp.float32) * 0.01
    lru_w_gate = nrm(ks[9], (nB, LRU_HEADS, LRU_BLOCK, 2 * LRU_BLOCK), jnp.float32) * LRU_BLOCK ** -0.5
    lru_b_gate = nrm(ks[10], (nB, LRU_HEADS, 2 * LRU_BLOCK), jnp.float32) * 0.01
    u = jax.random.uniform(ks[11], (nB, r), jnp.float32, 0.9, 0.999)
    p = u ** (1.0 / LRU_C)
    lru_lambda = jnp.log(p) - jnp.log1p(-p)
    lru_w_out = nrm(ks[12], (nB, r, d), jnp.float32) * r ** -0.5 * DEEPNORM_BETA
    ffn_w_up = nrm(ks[13], (L, d, 2 * f), jnp.float32) * d ** -0.5
    ffn_conv_w = nrm(ks[14], (L, FFN_CONV_WIDTH, 2 * f), jnp.float32) * FFN_CONV_WIDTH ** -0.5
    ffn_conv_b = nrm(ks[15], (L, 2 * f), jnp.float32) * 0.01
    ffn_w_down = nrm(ks[16], (L, f, d), jnp.float32) * f ** -0.5 * DEEPNORM_BETA
    ln_g = 1.0 + 0.02 * nrm(ks[17], (L, 2, d), jnp.float32)
    ln_b = 0.02 * nrm(ks[18], (L, 2, d), jnp.float32)
    return {"x": x,
            "sc_w_in": sc_w_in, "sc_conv_w": sc_conv_w, "sc_conv_b": sc_conv_b, "sc_w_out": sc_w_out,
            "lru_w_in": lru_w_in, "lru_b_in": lru_b_in, "lru_conv_w": lru_conv_w, "lru_conv_b": lru_conv_b,
            "lru_w_gate": lru_w_gate, "lru_b_gate": lru_b_gate, "lru_lambda": lru_lambda, "lru_w_out": lru_w_out,
            "ffn_w_up": ffn_w_up, "ffn_conv_w": ffn_conv_w, "ffn_conv_b": ffn_conv_b, "ffn_w_down": ffn_w_down,
            "ln_g": ln_g, "ln_b": ln_b}


def _fwd_reference(x, sc_w_in, sc_conv_w, sc_conv_b, sc_w_out,
              lru_w_in, lru_b_in, lru_conv_w, lru_conv_b, lru_w_gate, lru_b_gate, lru_lambda, lru_w_out,
              ffn_w_up, ffn_conv_w, ffn_conv_b, ffn_w_down, ln_g, ln_b):
    for i in range(DEPTH):
        j = i // N_MIXERS
        if i % N_MIXERS == 0:
            y = short_conv_mixer(x, sc_w_in[j], sc_conv_w[j], sc_conv_b[j], sc_w_out[j])
        else:
            y = rglru_block(x, lru_w_in[j], lru_b_in[j], lru_conv_w[j], lru_conv_b[j],
                            lru_w_gate[j], lru_b_gate[j], lru_lambda[j], lru_w_out[j])
        x = layer_norm(DEEPNORM_ALPHA * x + y, ln_g[i, 0], ln_b[i, 0])
        y = conv_ffn(x, ffn_w_up[i], ffn_conv_w[i], ffn_conv_b[i], ffn_w_down[i])
        x = layer_norm(DEEPNORM_ALPHA * x + y, ln_g[i, 1], ln_b[i, 1])
    return x


import jax as _jax
import jax.numpy as _jnp

TWIN_FORMAT = 'train_step'
FWD_PARAMS = ['x', 'sc_w_in', 'sc_conv_w', 'sc_conv_b', 'sc_w_out', 'lru_w_in', 'lru_b_in', 'lru_conv_w', 'lru_conv_b', 'lru_w_gate', 'lru_b_gate', 'lru_lambda', 'lru_w_out', 'ffn_w_up', 'ffn_conv_w', 'ffn_conv_b', 'ffn_w_down', 'ln_g', 'ln_b']
TWIN_WEIGHTS = ['sc_w_in', 'sc_conv_w', 'sc_conv_b', 'sc_w_out', 'lru_w_in', 'lru_b_in', 'lru_conv_w', 'lru_conv_b', 'lru_w_gate', 'lru_b_gate', 'lru_lambda', 'lru_w_out', 'ffn_w_up', 'ffn_conv_w', 'ffn_conv_b', 'ffn_w_down', 'ln_g', 'ln_b']
TWIN_DIFF_INPUT = 'x'
TWIN_INPUTS = ['x', 'sc_w_in', 'sc_conv_w', 'sc_conv_b', 'sc_w_out', 'lru_w_in', 'lru_b_in', 'lru_conv_w', 'lru_conv_b', 'lru_w_gate', 'lru_b_gate', 'lru_lambda', 'lru_w_out', 'ffn_w_up', 'ffn_conv_w', 'ffn_conv_b', 'ffn_w_down', 'ln_g', 'ln_b', 'loss_target', 'm_sc_w_in', 'm_sc_conv_w', 'm_sc_conv_b', 'm_sc_w_out', 'm_lru_w_in', 'm_lru_b_in', 'm_lru_conv_w', 'm_lru_conv_b', 'm_lru_w_gate', 'm_lru_b_gate', 'm_lru_lambda', 'm_lru_w_out', 'm_ffn_w_up', 'm_ffn_conv_w', 'm_ffn_conv_b', 'm_ffn_w_down', 'm_ln_g', 'm_ln_b', 'v_sc_w_in', 'v_sc_conv_w', 'v_sc_conv_b', 'v_sc_w_out', 'v_lru_w_in', 'v_lru_b_in', 'v_lru_conv_w', 'v_lru_conv_b', 'v_lru_w_gate', 'v_lru_b_gate', 'v_lru_lambda', 'v_lru_w_out', 'v_ffn_w_up', 'v_ffn_conv_w', 'v_ffn_conv_b', 'v_ffn_w_down', 'v_ln_g', 'v_ln_b']
TWIN_OUTPUTS = ['loss', 'grad_x', 'grad_sc_w_in', 'grad_sc_conv_w', 'grad_sc_conv_b', 'grad_sc_w_out', 'grad_lru_w_in', 'grad_lru_b_in', 'grad_lru_conv_w', 'grad_lru_conv_b', 'grad_lru_w_gate', 'grad_lru_b_gate', 'grad_lru_lambda', 'grad_lru_w_out', 'grad_ffn_w_up', 'grad_ffn_conv_w', 'grad_ffn_conv_b', 'grad_ffn_w_down', 'grad_ln_g', 'grad_ln_b', 'delta_sc_w_in', 'delta_sc_conv_w', 'delta_sc_conv_b', 'delta_sc_w_out', 'delta_lru_w_in', 'delta_lru_b_in', 'delta_lru_conv_w', 'delta_lru_conv_b', 'delta_lru_w_gate', 'delta_lru_b_gate', 'delta_lru_lambda', 'delta_lru_w_out', 'delta_ffn_w_up', 'delta_ffn_conv_w', 'delta_ffn_conv_b', 'delta_ffn_w_down', 'delta_ln_g', 'delta_ln_b', 'new_m_sc_w_in', 'new_m_sc_conv_w', 'new_m_sc_conv_b', 'new_m_sc_w_out', 'new_m_lru_w_in', 'new_m_lru_b_in', 'new_m_lru_conv_w', 'new_m_lru_conv_b', 'new_m_lru_w_gate', 'new_m_lru_b_gate', 'new_m_lru_lambda', 'new_m_lru_w_out', 'new_m_ffn_w_up', 'new_m_ffn_conv_w', 'new_m_ffn_conv_b', 'new_m_ffn_w_down', 'new_m_ln_g', 'new_m_ln_b', 'new_v_sc_w_in', 'new_v_sc_conv_w', 'new_v_sc_conv_b', 'new_v_sc_w_out', 'new_v_lru_w_in', 'new_v_lru_b_in', 'new_v_lru_conv_w', 'new_v_lru_conv_b', 'new_v_lru_w_gate', 'new_v_lru_b_gate', 'new_v_lru_lambda', 'new_v_lru_w_out', 'new_v_ffn_w_up', 'new_v_ffn_conv_w', 'new_v_ffn_conv_b', 'new_v_ffn_w_down', 'new_v_ln_g', 'new_v_ln_b']
TWIN_LEAF_KINDS = {'loss': 'loss', 'grad_x': 'grad_x', 'grad_sc_w_in': 'grad_w', 'grad_sc_conv_w': 'grad_w', 'grad_sc_conv_b': 'grad_w', 'grad_sc_w_out': 'grad_w', 'grad_lru_w_in': 'grad_w', 'grad_lru_b_in': 'grad_w', 'grad_lru_conv_w': 'grad_w', 'grad_lru_conv_b': 'grad_w', 'grad_lru_w_gate': 'grad_w', 'grad_lru_b_gate': 'grad_w', 'grad_lru_lambda': 'grad_w', 'grad_lru_w_out': 'grad_w', 'grad_ffn_w_up': 'grad_w', 'grad_ffn_conv_w': 'grad_w', 'grad_ffn_conv_b': 'grad_w', 'grad_ffn_w_down': 'grad_w', 'grad_ln_g': 'grad_w', 'grad_ln_b': 'grad_w', 'delta_sc_w_in': 'delta_w', 'delta_sc_conv_w': 'delta_w', 'delta_sc_conv_b': 'delta_w', 'delta_sc_w_out': 'delta_w', 'delta_lru_w_in': 'delta_w', 'delta_lru_b_in': 'delta_w', 'delta_lru_conv_w': 'delta_w', 'delta_lru_conv_b': 'delta_w', 'delta_lru_w_gate': 'delta_w', 'delta_lru_b_gate': 'delta_w', 'delta_lru_lambda': 'delta_w', 'delta_lru_w_out': 'delta_w', 'delta_ffn_w_up': 'delta_w', 'delta_ffn_conv_w': 'delta_w', 'delta_ffn_conv_b': 'delta_w', 'delta_ffn_w_down': 'delta_w', 'delta_ln_g': 'delta_w', 'delta_ln_b': 'delta_w', 'new_m_sc_w_in': 'new_m', 'new_m_sc_conv_w': 'new_m', 'new_m_sc_conv_b': 'new_m', 'new_m_sc_w_out': 'new_m', 'new_m_lru_w_in': 'new_m', 'new_m_lru_b_in': 'new_m', 'new_m_lru_conv_w': 'new_m', 'new_m_lru_conv_b': 'new_m', 'new_m_lru_w_gate': 'new_m', 'new_m_lru_b_gate': 'new_m', 'new_m_lru_lambda': 'new_m', 'new_m_lru_w_out': 'new_m', 'new_m_ffn_w_up': 'new_m', 'new_m_ffn_conv_w': 'new_m', 'new_m_ffn_conv_b': 'new_m', 'new_m_ffn_w_down': 'new_m', 'new_m_ln_g': 'new_m', 'new_m_ln_b': 'new_m', 'new_v_sc_w_in': 'new_v', 'new_v_sc_conv_w': 'new_v', 'new_v_sc_conv_b': 'new_v', 'new_v_sc_w_out': 'new_v', 'new_v_lru_w_in': 'new_v', 'new_v_lru_b_in': 'new_v', 'new_v_lru_conv_w': 'new_v', 'new_v_lru_conv_b': 'new_v', 'new_v_lru_w_gate': 'new_v', 'new_v_lru_b_gate': 'new_v', 'new_v_lru_lambda': 'new_v', 'new_v_lru_w_out': 'new_v', 'new_v_ffn_w_up': 'new_v', 'new_v_ffn_conv_w': 'new_v', 'new_v_ffn_conv_b': 'new_v', 'new_v_ffn_w_down': 'new_v', 'new_v_ln_g': 'new_v', 'new_v_ln_b': 'new_v'}


def _forward(args):
    return _fwd_reference(*[args[k] for k in FWD_PARAMS])


def _output_shape():
    out = _jax.eval_shape(lambda: _forward(_fwd_setup_inputs(0)))
    return out.shape, out.dtype

N_MICROBATCH = 1
ADAM_LR = 0.001
ADAM_B1 = 0.9
ADAM_B2 = 0.999
ADAM_EPS = 1e-08
ADAM_WD = 0.01
ADAM_STEP = 10
PER_EXAMPLE_BATCH_AXIS = {'x': 0, 'loss_target': 0}
SHARED_INPUTS = []
_WEIGHT_DTYPES = {'sc_w_in': _jnp.float32, 'sc_conv_w': _jnp.float32, 'sc_conv_b': _jnp.float32, 'sc_w_out': _jnp.float32, 'lru_w_in': _jnp.float32, 'lru_b_in': _jnp.float32, 'lru_conv_w': _jnp.float32, 'lru_conv_b': _jnp.float32, 'lru_w_gate': _jnp.float32, 'lru_b_gate': _jnp.float32, 'lru_lambda': _jnp.float32, 'lru_w_out': _jnp.float32, 'ffn_w_up': _jnp.float32, 'ffn_conv_w': _jnp.float32, 'ffn_conv_b': _jnp.float32, 'ffn_w_down': _jnp.float32, 'ln_g': _jnp.float32, 'ln_b': _jnp.float32}
MOMENT_SCALE = {'sc_w_in': 6.601351e-02, 'sc_conv_w': 6.648357e-02, 'sc_conv_b': 7.029321e-02, 'sc_w_out': 1.570499e-01, 'lru_w_in': 3.101806e-02, 'lru_b_in': 3.993949e-01, 'lru_conv_w': 3.552181e-02, 'lru_conv_b': 5.169463e-01, 'lru_w_gate': 1.741352e-02, 'lru_b_gate': 1.120752e-02, 'lru_lambda': 1.938120e-02, 'lru_w_out': 9.099412e-02, 'ffn_w_up': 2.433454e-02, 'ffn_conv_w': 2.475954e-02, 'ffn_conv_b': 2.898979e-02, 'ffn_w_down': 9.446507e-02, 'ln_g': 2.267112e+01, 'ln_b': 1.392146e+00}


def _to_microbatches(a, axis):
    t = _jnp.moveaxis(a, axis, 0)
    t = t.reshape((N_MICROBATCH, t.shape[0] // N_MICROBATCH) + t.shape[1:])
    return _jnp.moveaxis(t, 1, axis + 1)


def setup_inputs(seed: int = 0) -> dict:
    inp = _fwd_setup_inputs(seed)
    key = _jax.random.fold_in(_jax.random.key(seed), 7919)
    shape, _ = _output_shape()
    out = dict(inp)
    out["loss_target"] = _jax.random.normal(_jax.random.fold_in(key, 0), shape, _jnp.float32)
    for i, name in enumerate(TWIN_WEIGHTS):
        w = inp[name].astype(_jnp.float32)
        if MOMENT_SCALE is None:
            s = _jnp.sqrt(_jnp.mean(_jnp.square(w)) + 1e-30)
        else:
            s = MOMENT_SCALE[name]
        km, kv = _jax.random.split(_jax.random.fold_in(key, i + 1))
        out[name] = w
        out["m_" + name] = s * _jax.random.normal(km, w.shape, _jnp.float32)
        out["v_" + name] = (s * s) * _jax.random.uniform(kv, w.shape, _jnp.float32, 0.5, 1.5)
    if N_MICROBATCH > 1:
        for name, axis in PER_EXAMPLE_BATCH_AXIS.items():
            out[name] = _to_microbatches(out[name], axis)
    return {'x': out['x'], 'sc_w_in': out['sc_w_in'], 'sc_conv_w': out['sc_conv_w'], 'sc_conv_b': out['sc_conv_b'], 'sc_w_out': out['sc_w_out'], 'lru_w_in': out['lru_w_in'], 'lru_b_in': out['lru_b_in'], 'lru_conv_w': out['lru_conv_w'], 'lru_conv_b': out['lru_conv_b'], 'lru_w_gate': out['lru_w_gate'], 'lru_b_gate': out['lru_b_gate'], 'lru_lambda': out['lru_lambda'], 'lru_w_out': out['lru_w_out'], 'ffn_w_up': out['ffn_w_up'], 'ffn_conv_w': out['ffn_conv_w'], 'ffn_conv_b': out['ffn_conv_b'], 'ffn_w_down': out['ffn_w_down'], 'ln_g': out['ln_g'], 'ln_b': out['ln_b'], 'loss_target': out['loss_target'], 'm_sc_w_in': out['m_sc_w_in'], 'm_sc_conv_w': out['m_sc_conv_w'], 'm_sc_conv_b': out['m_sc_conv_b'], 'm_sc_w_out': out['m_sc_w_out'], 'm_lru_w_in': out['m_lru_w_in'], 'm_lru_b_in': out['m_lru_b_in'], 'm_lru_conv_w': out['m_lru_conv_w'], 'm_lru_conv_b': out['m_lru_conv_b'], 'm_lru_w_gate': out['m_lru_w_gate'], 'm_lru_b_gate': out['m_lru_b_gate'], 'm_lru_lambda': out['m_lru_lambda'], 'm_lru_w_out': out['m_lru_w_out'], 'm_ffn_w_up': out['m_ffn_w_up'], 'm_ffn_conv_w': out['m_ffn_conv_w'], 'm_ffn_conv_b': out['m_ffn_conv_b'], 'm_ffn_w_down': out['m_ffn_w_down'], 'm_ln_g': out['m_ln_g'], 'm_ln_b': out['m_ln_b'], 'v_sc_w_in': out['v_sc_w_in'], 'v_sc_conv_w': out['v_sc_conv_w'], 'v_sc_conv_b': out['v_sc_conv_b'], 'v_sc_w_out': out['v_sc_w_out'], 'v_lru_w_in': out['v_lru_w_in'], 'v_lru_b_in': out['v_lru_b_in'], 'v_lru_conv_w': out['v_lru_conv_w'], 'v_lru_conv_b': out['v_lru_conv_b'], 'v_lru_w_gate': out['v_lru_w_gate'], 'v_lru_b_gate': out['v_lru_b_gate'], 'v_lru_lambda': out['v_lru_lambda'], 'v_lru_w_out': out['v_lru_w_out'], 'v_ffn_w_up': out['v_ffn_w_up'], 'v_ffn_conv_w': out['v_ffn_conv_w'], 'v_ffn_conv_b': out['v_ffn_conv_b'], 'v_ffn_w_down': out['v_ffn_w_down'], 'v_ln_g': out['v_ln_g'], 'v_ln_b': out['v_ln_b']}


def _loss(weights, diff, rest, loss_target):
    with _jax.named_scope("forward"):
        args = {**rest, TWIN_DIFF_INPUT: diff, **{k: w.astype(_WEIGHT_DTYPES[k]) for k, w in weights.items()}}
        y = _forward(args)
    with _jax.named_scope("loss_head"):
        err = _jnp.square(y.astype(_jnp.float32) - loss_target)
        return 0.5 * _jnp.sum(_jnp.mean(err, axis=-1)) if err.ndim else 0.5 * err


def _adamw(w, g, m, v):
    m = ADAM_B1 * m + (1.0 - ADAM_B1) * g
    v = ADAM_B2 * v + (1.0 - ADAM_B2) * _jnp.square(g)
    m_hat = m / (1.0 - ADAM_B1 ** ADAM_STEP)
    v_hat = v / (1.0 - ADAM_B2 ** ADAM_STEP)
    delta = -ADAM_LR * (m_hat / (_jnp.sqrt(v_hat) + ADAM_EPS) + ADAM_WD * w)
    return delta, m, v


def reference(x, sc_w_in, sc_conv_w, sc_conv_b, sc_w_out, lru_w_in, lru_b_in, lru_conv_w, lru_conv_b, lru_w_gate, lru_b_gate, lru_lambda, lru_w_out, ffn_w_up, ffn_conv_w, ffn_conv_b, ffn_w_down, ln_g, ln_b, loss_target, m_sc_w_in, m_sc_conv_w, m_sc_conv_b, m_sc_w_out, m_lru_w_in, m_lru_b_in, m_lru_conv_w, m_lru_conv_b, m_lru_w_gate, m_lru_b_gate, m_lru_lambda, m_lru_w_out, m_ffn_w_up, m_ffn_conv_w, m_ffn_conv_b, m_ffn_w_down, m_ln_g, m_ln_b, v_sc_w_in, v_sc_conv_w, v_sc_conv_b, v_sc_w_out, v_lru_w_in, v_lru_b_in, v_lru_conv_w, v_lru_conv_b, v_lru_w_gate, v_lru_b_gate, v_lru_lambda, v_lru_w_out, v_ffn_w_up, v_ffn_conv_w, v_ffn_conv_b, v_ffn_w_down, v_ln_g, v_ln_b):
    given = dict(x=x, sc_w_in=sc_w_in, sc_conv_w=sc_conv_w, sc_conv_b=sc_conv_b, sc_w_out=sc_w_out, lru_w_in=lru_w_in, lru_b_in=lru_b_in, lru_conv_w=lru_conv_w, lru_conv_b=lru_conv_b, lru_w_gate=lru_w_gate, lru_b_gate=lru_b_gate, lru_lambda=lru_lambda, lru_w_out=lru_w_out, ffn_w_up=ffn_w_up, ffn_conv_w=ffn_conv_w, ffn_conv_b=ffn_conv_b, ffn_w_down=ffn_w_down, ln_g=ln_g, ln_b=ln_b, loss_target=loss_target, m_sc_w_in=m_sc_w_in, m_sc_conv_w=m_sc_conv_w, m_sc_conv_b=m_sc_conv_b, m_sc_w_out=m_sc_w_out, m_lru_w_in=m_lru_w_in, m_lru_b_in=m_lru_b_in, m_lru_conv_w=m_lru_conv_w, m_lru_conv_b=m_lru_conv_b, m_lru_w_gate=m_lru_w_gate, m_lru_b_gate=m_lru_b_gate, m_lru_lambda=m_lru_lambda, m_lru_w_out=m_lru_w_out, m_ffn_w_up=m_ffn_w_up, m_ffn_conv_w=m_ffn_conv_w, m_ffn_conv_b=m_ffn_conv_b, m_ffn_w_down=m_ffn_w_down, m_ln_g=m_ln_g, m_ln_b=m_ln_b, v_sc_w_in=v_sc_w_in, v_sc_conv_w=v_sc_conv_w, v_sc_conv_b=v_sc_conv_b, v_sc_w_out=v_sc_w_out, v_lru_w_in=v_lru_w_in, v_lru_b_in=v_lru_b_in, v_lru_conv_w=v_lru_conv_w, v_lru_conv_b=v_lru_conv_b, v_lru_w_gate=v_lru_w_gate, v_lru_b_gate=v_lru_b_gate, v_lru_lambda=v_lru_lambda, v_lru_w_out=v_lru_w_out, v_ffn_w_up=v_ffn_w_up, v_ffn_conv_w=v_ffn_conv_w, v_ffn_conv_b=v_ffn_conv_b, v_ffn_w_down=v_ffn_w_down, v_ln_g=v_ln_g, v_ln_b=v_ln_b)
    weights = {n: given[n] for n in TWIN_WEIGHTS}
    shared = {n: given[n] for n in SHARED_INPUTS}
    per_example = {n: given[n] for n in ['x']}
    grad_fn = _jax.value_and_grad(_loss, argnums=(0, 1))

    def one_microbatch(ex, loss_target):
        ex = dict(ex)
        diff = ex.pop(TWIN_DIFF_INPUT)
        return grad_fn(weights, diff, {**shared, **ex}, loss_target)

    if N_MICROBATCH == 1:
        loss, (grad_w, grad_x) = one_microbatch(per_example, given["loss_target"])
    else:
        def body(carry, xs):
            loss_sum, grad_sum = carry
            l_k, (gw_k, gx_k) = one_microbatch(xs[0], xs[1])
            with _jax.named_scope("update"):
                return (loss_sum + l_k, _jax.tree.map(_jnp.add, grad_sum, gw_k)), gx_k

        init = (_jnp.zeros((), _jnp.float32), _jax.tree.map(_jnp.zeros_like, weights))
        (loss, grad_w), grad_x = _jax.lax.scan(body, init, (per_example, given["loss_target"]))
    with _jax.named_scope("update"):
        delta_w, new_m, new_v = {}, {}, {}
        for n in TWIN_WEIGHTS:
            delta_w[n], new_m[n], new_v[n] = _adamw(weights[n], grad_w[n], given["m_" + n], given["v_" + n])
    return (loss, grad_x, *[grad_w[n] for n in TWIN_WEIGHTS], *[delta_w[n] for n in TWIN_WEIGHTS],
            *[new_m[n] for n in TWIN_WEIGHTS], *[new_v[n] for n in TWIN_WEIGHTS])
```

```python
import functools
import math

import jax
import jax.numpy as jnp
from jax import lax
from jax.experimental import pallas as pl
from jax.experimental.pallas import tpu as pltpu

F32 = jnp.float32
BF16 = jnp.bfloat16

N_DEV = 8
MESH_AXES = ("x", "y", "c")
LANES = 128
SUBLANES = 8
VMEM_LIMIT = 56 * 1024 * 1024
MM_LHS_ELEMS = 2 * 1024 * 1024

LRU_C = 8.0
LN_EPS = 1e-5
ADAM_LR = 0.001
ADAM_B1 = 0.9
ADAM_B2 = 0.999
ADAM_EPS = 1e-08
ADAM_WD = 0.01
ADAM_STEP = 10
GELU_K = math.sqrt(2.0 / math.pi)
GELU_C = 0.044715


def _tile(n, target, align):
    if n <= target:
        return n
    t = (target // align) * align
    while t >= align:
        if n % t == 0:
            return t
        t -= align
    return n


def _params(sem):
    return pltpu.CompilerParams(dimension_semantics=sem, vmem_limit_bytes=VMEM_LIMIT)


def _rows(x):
    return lax.broadcasted_iota(jnp.int32, x.shape, 0)


def _shift_dn(x, k, fill=0.0):
    if k == 0:
        return x
    return jnp.where(_rows(x) >= k, pltpu.roll(x, k, 0), fill)


def _shift_up(x, k, fill=0.0):
    if k == 0:
        return x
    s = x.shape[0]
    return jnp.where(_rows(x) < s - k, pltpu.roll(x, s - k, 0), fill)


def _conv_fwd(x, w, b):
    kw = w.shape[0]
    y = _shift_dn(x, kw - 1) * w[0:1, :] + b
    for k in range(1, kw):
        y = y + _shift_dn(x, kw - 1 - k) * w[k:k + 1, :]
    return y


def _conv_bwd_x(dy, w):
    kw = w.shape[0]
    dx = _shift_up(dy, kw - 1) * w[0:1, :]
    for k in range(1, kw):
        dx = dx + _shift_up(dy, kw - 1 - k) * w[k:k + 1, :]
    return dx


def _conv_bwd_w(dy, x, kw):
    return [jnp.sum(dy * _shift_dn(x, kw - 1 - k), axis=0, keepdims=True) for k in range(kw)]


def _accumulate(first, items):
    flat = []
    for ref, val in items:
        if isinstance(val, list):
            flat += [(ref, (slice(k, k + 1), slice(None)), row) for k, row in enumerate(val)]
        else:
            flat.append((ref, Ellipsis, val))

    @pl.when(first)
    def _():
        for ref, idx, val in flat:
            ref[idx] = val

    @pl.when(jnp.logical_not(first))
    def _():
        for ref, idx, val in flat:
            ref[idx] += val


def _colsum(x):
    return jnp.sum(x, axis=0, keepdims=True)


def _sigmoid(x):
    return 1.0 / (1.0 + jnp.exp(-x))


def _log1p(x):
    u = 1.0 + x
    return jnp.where(u == 1.0, x, jnp.log(u) * (x / (u - 1.0)))


def _softplus(x):
    return jnp.maximum(x, 0.0) + _log1p(jnp.exp(-jnp.abs(x)))


def _expm1(x, ex):
    poly = x * (1.0 + x * (0.5 + x * (1.0 / 6.0 + x * (1.0 / 24.0 + x * (1.0 / 120.0 + x * (1.0 / 720.0))))))
    return jnp.where(jnp.abs(x) < 0.25, poly, ex - 1.0)


def _gelu(x):
    t = jnp.tanh(GELU_K * (x + GELU_C * x * x * x))
    return 0.5 * x * (1.0 + t)


def _gelu_and_grad(x):
    x2 = x * x
    t = jnp.tanh(GELU_K * (x + GELU_C * x * x2))
    g = 0.5 * x * (1.0 + t)
    dg = 0.5 * (1.0 + t) + 0.5 * x * (1.0 - t * t) * (GELU_K * (1.0 + 3.0 * GELU_C * x2))
    return g, dg


def _scan_fwd(a, b):
    s = a.shape[0]
    k = 1
    while k < s:
        b = a * _shift_dn(b, k) + b
        if 2 * k < s:
            a = a * _shift_dn(a, k, 1.0)
        k *= 2
    return b


def _scan_rev(c, v):
    s = c.shape[0]
    k = 1
    while k < s:
        v = c * _shift_up(v, k) + v
        if 2 * k < s:
            c = c * _shift_up(c, k, 1.0)
        k *= 2
    return v


def _mm(a, w, *, name, trans_w=False, bias=None, resid=None, resid_scale=1.0, tn=512):
    m, k = a.shape
    n = w.shape[0] if trans_w else w.shape[1]
    tm = _tile(m, min(1024, max(256, MM_LHS_ELEMS // k)), SUBLANES)
    tn = _tile(n, tn, LANES)
    has_bias = bias is not None
    has_resid = resid is not None

    def body(*refs):
        a_ref, w_ref = refs[0], refs[1]
        pos = 2
        b_ref = r_ref = None
        if has_bias:
            b_ref = refs[pos]
            pos += 1
        if has_resid:
            r_ref = refs[pos]
            pos += 1
        o_ref, abf = refs[pos], refs[pos + 1]

        @pl.when(pl.program_id(1) == 0)
        def _():
            abf[...] = a_ref[...].astype(BF16)

        if trans_w:
            acc = lax.dot_general(abf[...], w_ref[...], (((1,), (1,)), ((), ())), preferred_element_type=F32)
        else:
            acc = jnp.dot(abf[...], w_ref[...], preferred_element_type=F32)
        if has_bias:
            acc = acc + b_ref[...]
        if has_resid:
            acc = acc + resid_scale * r_ref[...]
        o_ref[...] = acc

    in_specs = [pl.BlockSpec((tm, k), lambda i, j: (i, 0))]
    if trans_w:
        in_specs.append(pl.BlockSpec((tn, k), lambda i, j: (j, 0)))
    else:
        in_specs.append(pl.BlockSpec((k, tn), lambda i, j: (0, j)))
    args = [a, w]
    if has_bias:
        in_specs.append(pl.BlockSpec((1, tn), lambda i, j: (0, j)))
        args.append(bias)
    if has_resid:
        in_specs.append(pl.BlockSpec((tm, tn), lambda i, j: (i, j)))
        args.append(resid)
    return pl.pallas_call(
        body, name=name, grid=(m // tm, n // tn), in_specs=in_specs,
        out_specs=pl.BlockSpec((tm, tn), lambda i, j: (i, j)),
        out_shape=jax.ShapeDtypeStruct((m, n), F32),
        scratch_shapes=[pltpu.VMEM((tm, k), BF16)],
        compiler_params=_params(("parallel", "arbitrary")),
    )(*args)


def _mm_ln(a, w, resid, alpha, g, b, *, name, tm=512):
    m, k = a.shape
    d = w.shape[1]
    tm = _tile(m, tm, SUBLANES)

    def body(a_ref, w_ref, r_ref, g_ref, b_ref, z_ref, o_ref):
        y = jnp.dot(a_ref[...].astype(BF16), w_ref[...], preferred_element_type=F32)
        z = alpha * r_ref[...] + y
        z_ref[...] = z
        mu = jnp.mean(z, axis=-1, keepdims=True)
        zc = z - mu
        var = jnp.mean(zc * zc, axis=-1, keepdims=True)
        o_ref[...] = zc * lax.rsqrt(var + LN_EPS) * g_ref[...] + b_ref[...]

    row = pl.BlockSpec((tm, d), lambda i: (i, 0))
    vec = pl.BlockSpec((1, d), lambda i: (0, 0))
    return pl.pallas_call(
        body, name=name, grid=(m // tm,),
        in_specs=[pl.BlockSpec((tm, k), lambda i: (i, 0)), pl.BlockSpec((k, d), lambda i: (0, 0)), row, vec, vec],
        out_specs=[row, row],
        out_shape=[jax.ShapeDtypeStruct((m, d), F32), jax.ShapeDtypeStruct((m, d), F32)],
        compiler_params=_params(("parallel",)),
    )(a, w, resid, g, b)


def _mm_tn(a, b, *, name, tm=1024, tn=512, tk=512):
    t, m = a.shape
    n = b.shape[1]
    tm = _tile(m, tm, LANES)
    tn = _tile(n, tn, LANES)
    tk = _tile(t, tk, SUBLANES)

    def body(a_ref, b_ref, o_ref):
        @pl.when(pl.program_id(2) == 0)
        def _():
            o_ref[...] = jnp.zeros_like(o_ref)

        o_ref[...] += lax.dot_general(a_ref[...].astype(BF16), b_ref[...].astype(BF16),
                                      (((0,), (0,)), ((), ())), preferred_element_type=F32)

    return pl.pallas_call(
        body, name=name, grid=(m // tm, n // tn, t // tk),
        in_specs=[pl.BlockSpec((tk, tm), lambda i, j, l: (l, i)), pl.BlockSpec((tk, tn), lambda i, j, l: (l, j))],
        out_specs=pl.BlockSpec((tm, tn), lambda i, j, l: (i, j)),
        out_shape=jax.ShapeDtypeStruct((m, n), F32),
        compiler_params=_params(("parallel", "parallel", "arbitrary")),
    )(a, b)


def _ln_bwd(dout, z, g, *, name, tm=512):
    t, d = z.shape
    tm = _tile(t, tm, SUBLANES)

    def body(do_ref, z_ref, g_ref, dz_ref, dg_ref, db_ref):
        @pl.when(pl.program_id(0) == 0)
        def _():
            dg_ref[...] = jnp.zeros_like(dg_ref)
            db_ref[...] = jnp.zeros_like(db_ref)

        z = z_ref[...]
        do = do_ref[...]
        mu = jnp.mean(z, axis=-1, keepdims=True)
        zc = z - mu
        var = jnp.mean(zc * zc, axis=-1, keepdims=True)
        rstd = lax.rsqrt(var + LN_EPS)
        xhat = zc * rstd
        dxh = do * g_ref[...]
        m1 = jnp.mean(dxh, axis=-1, keepdims=True)
        m2 = jnp.mean(dxh * xhat, axis=-1, keepdims=True)
        dz_ref[...] = rstd * (dxh - m1 - xhat * m2)
        dg_ref[...] += _colsum(do * xhat)
        db_ref[...] += _colsum(do)

    row = pl.BlockSpec((tm, d), lambda i: (i, 0))
    vec = pl.BlockSpec((1, d), lambda i: (0, 0))
    return pl.pallas_call(
        body, name=name, grid=(t // tm,), in_specs=[row, row, vec], out_specs=[row, vec, vec],
        out_shape=[jax.ShapeDtypeStruct((t, d), F32), jax.ShapeDtypeStruct((1, d), F32), jax.ShapeDtypeStruct((1, d), F32)],
        compiler_params=_params(("arbitrary",)),
    )(dout, z, g)


def _loss_head(y, target, *, name, tm=512):
    t, d = y.shape
    tm = _tile(t, tm, SUBLANES)

    def body(y_ref, t_ref, s_ref, dy_ref):
        @pl.when(pl.program_id(0) == 0)
        def _():
            s_ref[...] = jnp.zeros_like(s_ref)

        e = y_ref[...] - t_ref[...]
        dy_ref[...] = e * (1.0 / d)
        s_ref[...] += jnp.sum(_colsum(e * e), axis=-1, keepdims=True)

    row = pl.BlockSpec((tm, d), lambda i: (i, 0))
    return pl.pallas_call(
        body, name=name, grid=(t // tm,), in_specs=[row, row],
        out_specs=[pl.BlockSpec((1, LANES), lambda i: (0, 0)), row],
        out_shape=[jax.ShapeDtypeStruct((1, LANES), F32), jax.ShapeDtypeStruct((t, d), F32)],
        compiler_params=_params(("arbitrary",)),
    )(y, target)


def _strip(s, tc, off):
    return pl.BlockSpec((None, s, tc), lambda c, b, *_: (b, 0, off + c))


def _cvec(kw, tc, off):
    return pl.BlockSpec((kw, tc), lambda c, b, *_: (0, off + c))


def _acc(kw, tc):
    return pl.BlockSpec((kw, tc), lambda c, b, *_: (0, c))


def _sc_fwd(h, cw, cb, *, name, tc=256):
    bsz, s, d3 = h.shape
    d = d3 // 3
    tc = _tile(d, tc, LANES)
    nc = d // tc

    def body(gb_ref, gc_ref, v_ref, w_ref, b_ref, q_ref):
        u = _conv_fwd(gc_ref[...] * v_ref[...], w_ref[...], b_ref[...])
        q_ref[...] = gb_ref[...] * u

    return pl.pallas_call(
        body, name=name, grid=(nc, bsz),
        in_specs=[_strip(s, tc, 0), _strip(s, tc, nc), _strip(s, tc, 2 * nc), _cvec(cw.shape[0], tc, 0), _cvec(1, tc, 0)],
        out_specs=_strip(s, tc, 0),
        out_shape=jax.ShapeDtypeStruct((bsz, s, d), F32),
        compiler_params=_params(("parallel", "parallel")),
    )(h, h, h, cw, cb)


def _sc_bwd(h, dq, cw, cb, *, name, tc=256):
    bsz, s, d3 = h.shape
    d = d3 // 3
    kw = cw.shape[0]
    tc = _tile(d, tc, LANES)
    nc = d // tc

    def body(gb_ref, gc_ref, v_ref, dq_ref, w_ref, b_ref, dh_ref, dw_ref, db_ref, parts):
        b_id, part = pl.program_id(1), pl.program_id(2)

        @pl.when(part == 0)
        def _():
            gb, gc, v, dq_, w = gb_ref[...], gc_ref[...], v_ref[...], dq_ref[...], w_ref[...]
            p = gc * v
            u = _conv_fwd(p, w, b_ref[...])
            du = dq_ * gb
            dp = _conv_bwd_x(du, w)
            parts[0] = dq_ * u
            parts[1] = dp * v
            parts[2] = dp * gc
            _accumulate(b_id == 0, [(dw_ref, _conv_bwd_w(du, p, kw)), (db_ref, _colsum(du))])

        dh_ref[...] = parts[part]

    return pl.pallas_call(
        body, name=name, grid=(nc, bsz, 3),
        in_specs=[_strip(s, tc, 0), _strip(s, tc, nc), _strip(s, tc, 2 * nc), _strip(s, tc, 0),
                  _cvec(kw, tc, 0), _cvec(1, tc, 0)],
        out_specs=[pl.BlockSpec((None, s, tc), lambda c, b, p: (b, 0, p * nc + c)), _acc(kw, tc), _acc(1, tc)],
        out_shape=[jax.ShapeDtypeStruct((bsz, s, d3), F32), jax.ShapeDtypeStruct((kw, d), F32),
                   jax.ShapeDtypeStruct((1, d), F32)],
        scratch_shapes=[pltpu.VMEM((3, s, tc), F32)],
        compiler_params=_params(("parallel", "arbitrary", "arbitrary")),
    )(h, h, h, dq, cw, cb)


def _ffn_fwd(h, cw, cb, *, name, tc=256):
    bsz, s, f2 = h.shape
    f = f2 // 2
    tc = _tile(f, tc, LANES)
    nc = f // tc

    def body(hg_ref, hv_ref, wg_ref, wv_ref, bg_ref, bv_ref, a_ref):
        g = _conv_fwd(hg_ref[...], wg_ref[...], bg_ref[...])
        v = _conv_fwd(hv_ref[...], wv_ref[...], bv_ref[...])
        a_ref[...] = g * _sigmoid(g) * v

    kw = cw.shape[0]
    return pl.pallas_call(
        body, name=name, grid=(nc, bsz),
        in_specs=[_strip(s, tc, 0), _strip(s, tc, nc), _cvec(kw, tc, 0), _cvec(kw, tc, nc), _cvec(1, tc, 0), _cvec(1, tc, nc)],
        out_specs=_strip(s, tc, 0),
        out_shape=jax.ShapeDtypeStruct((bsz, s, f), F32),
        compiler_params=_params(("parallel", "parallel")),
    )(h, h, cw, cw, cb, cb)


def _ffn_bwd(h, da, cw, cb, *, name, tc=256):
    bsz, s, f2 = h.shape
    f = f2 // 2
    kw = cw.shape[0]
    tc = _tile(f, tc, LANES)
    nc = f // tc

    def body(hg_ref, hv_ref, da_ref, wg_ref, wv_ref, bg_ref, bv_ref,
             dh_ref, dwg_ref, dwv_ref, dbg_ref, dbv_ref, parts):
        b_id, part = pl.program_id(1), pl.program_id(2)

        @pl.when(part == 0)
        def _():
            hg, hv, da_ = hg_ref[...], hv_ref[...], da_ref[...]
            wg, wv = wg_ref[...], wv_ref[...]
            g = _conv_fwd(hg, wg, bg_ref[...])
            v = _conv_fwd(hv, wv, bv_ref[...])
            sg = _sigmoid(g)
            dv = da_ * (g * sg)
            dg = da_ * v * (sg * (1.0 + g * (1.0 - sg)))
            parts[0] = _conv_bwd_x(dg, wg)
            parts[1] = _conv_bwd_x(dv, wv)
            _accumulate(b_id == 0, [(dwg_ref, _conv_bwd_w(dg, hg, kw)), (dwv_ref, _conv_bwd_w(dv, hv, kw)),
                                    (dbg_ref, _colsum(dg)), (dbv_ref, _colsum(dv))])

        dh_ref[...] = parts[part]

    return pl.pallas_call(
        body, name=name, grid=(nc, bsz, 2),
        in_specs=[_strip(s, tc, 0), _strip(s, tc, nc), _strip(s, tc, 0), _cvec(kw, tc, 0), _cvec(kw, tc, nc),
                  _cvec(1, tc, 0), _cvec(1, tc, nc)],
        out_specs=[pl.BlockSpec((None, s, tc), lambda c, b, p: (b, 0, p * nc + c)),
                   _acc(kw, tc), _acc(kw, tc), _acc(1, tc), _acc(1, tc)],
        out_shape=[jax.ShapeDtypeStruct((bsz, s, f2), F32), jax.ShapeDtypeStruct((kw, f), F32),
                   jax.ShapeDtypeStruct((kw, f), F32), jax.ShapeDtypeStruct((1, f), F32), jax.ShapeDtypeStruct((1, f), F32)],
        scratch_shapes=[pltpu.VMEM((2, s, tc), F32)],
        compiler_params=_params(("parallel", "arbitrary", "arbitrary")),
    )(h, h, da, cw, cw, cb, cb)


def _lru_gates(r, cw, cb, wg, bg, lam):
    blk = r.shape[1]
    xr = _conv_fwd(r, cw, cb)
    gates = jnp.dot(xr.astype(BF16), wg, preferred_element_type=F32) + bg
    rg = _sigmoid(gates[:, :blk])
    ig = _sigmoid(gates[:, blk:])
    sp = _softplus(-lam)
    la = (-LRU_C * sp) * rg
    a = jnp.exp(la)
    mult = jnp.sqrt(-_expm1(2.0 * la, a * a))
    return xr, rg, ig, sp, a, mult


def _lru_specs(s, blk, heads, kw):
    return [pl.BlockSpec((None, s, blk), lambda h, b, *_: (b, 0, h)),
            pl.BlockSpec((None, s, blk), lambda h, b, *_: (b, 0, heads + h)),
            pl.BlockSpec((kw, blk), lambda h, b, *_: (0, h)),
            pl.BlockSpec((1, blk), lambda h, b, *_: (0, h)),
            pl.BlockSpec((None, blk, 2 * blk), lambda h, b, *_: (h, 0, 0)),
            pl.BlockSpec((None, 1, 2 * blk), lambda h, b, *_: (h, 0, 0)),
            pl.BlockSpec((1, blk), lambda h, b, *_: (0, h))]


def _lru_fwd(h, cw, cb, wg, bg, lam, *, name):
    bsz, s, r2 = h.shape
    heads, blk = wg.shape[0], wg.shape[1]
    kw = cw.shape[0]

    def body(g_ref, r_ref, cw_ref, cb_ref, wg_ref, bg_ref, lam_ref, y_ref, hs_ref):
        xr, _, ig, _, a, mult = _lru_gates(r_ref[...], cw_ref[...], cb_ref[...], wg_ref[...], bg_ref[...], lam_ref[...])
        hs = _scan_fwd(a, mult * (ig * xr))
        hs_ref[...] = hs
        y_ref[...] = hs * _gelu(g_ref[...])

    out = pl.BlockSpec((None, s, blk), lambda hd, b: (b, 0, hd))
    return pl.pallas_call(
        body, name=name, grid=(heads, bsz), in_specs=_lru_specs(s, blk, heads, kw), out_specs=[out, out],
        out_shape=[jax.ShapeDtypeStruct((bsz, s, r2 // 2), F32), jax.ShapeDtypeStruct((bsz, s, r2 // 2), F32)],
        compiler_params=_params(("parallel", "parallel")),
    )(h, h, cw, cb, wg, bg, lam)


def _lru_bwd(h, hs, dy, cw, cb, wg, bg, lam, *, name):
    bsz, s, r2 = h.shape
    rw = r2 // 2
    heads, blk = wg.shape[0], wg.shape[1]
    kw = cw.shape[0]

    def body(g_ref, r_ref, cw_ref, cb_ref, wg_ref, bg_ref, lam_ref, hs_ref, dy_ref,
             dh_ref, dcw_ref, dcb_ref, dwg_ref, dbg_ref, dlam_ref, sg_ref, sr_ref, parts):
        b_id, part = pl.program_id(1), pl.program_id(2)

        @pl.when(part == 0)
        def _():
            r, cw_, wg_, lam_ = r_ref[...], cw_ref[...], wg_ref[...], lam_ref[...]
            xr, rg, ig, sp, a, mult = _lru_gates(r, cw_, cb_ref[...], wg_, bg_ref[...], lam_)
            hs_, dy_ = hs_ref[...], dy_ref[...]
            gel, dgel = _gelu_and_grad(g_ref[...])
            dg = dy_ * hs_ * dgel
            lmb = _scan_rev(_shift_up(a, 1, 1.0), dy_ * gel)
            da = lmb * _shift_dn(hs_, 1)
            dmult = lmb * (ig * xr)
            dig = lmb * (mult * xr)
            dxr = lmb * (mult * ig)
            dla = da * a - dmult * (a * a / mult)
            drg = dla * (-LRU_C * sp)
            dsp = _colsum(dla * rg) * (-LRU_C)
            dlam = -dsp * _sigmoid(-lam_)
            dgates = jnp.concatenate([drg * (rg * (1.0 - rg)), dig * (ig * (1.0 - ig))], axis=1)
            dgates_bf = dgates.astype(BF16)
            dwg = lax.dot_general(xr.astype(BF16), dgates_bf, (((0,), (0,)), ((), ())), preferred_element_type=F32)
            dxr = dxr + lax.dot_general(dgates_bf, wg_, (((1,), (1,)), ((), ())), preferred_element_type=F32)
            dr = _conv_bwd_x(dxr, cw_)
            parts[0] = dg
            parts[1] = dr
            _accumulate(b_id == 0, [(dcw_ref, _conv_bwd_w(dxr, r, kw)), (dcb_ref, _colsum(dxr)), (dwg_ref, dwg),
                                    (dbg_ref, _colsum(dgates)), (dlam_ref, dlam), (sg_ref, _colsum(dg)),
                                    (sr_ref, _colsum(dr))])

        dh_ref[...] = parts[part]

    strip = pl.BlockSpec((None, s, blk), lambda hd, b, p: (b, 0, hd))
    vec = pl.BlockSpec((1, blk), lambda hd, b, p: (0, hd))
    return pl.pallas_call(
        body, name=name, grid=(heads, bsz, 2),
        in_specs=_lru_specs(s, blk, heads, kw) + [strip, strip],
        out_specs=[pl.BlockSpec((None, s, blk), lambda hd, b, p: (b, 0, p * heads + hd)),
                   pl.BlockSpec((kw, blk), lambda hd, b, p: (0, hd)), vec,
                   pl.BlockSpec((None, blk, 2 * blk), lambda hd, b, p: (hd, 0, 0)),
                   pl.BlockSpec((None, 1, 2 * blk), lambda hd, b, p: (hd, 0, 0)), vec, vec, vec],
        out_shape=[jax.ShapeDtypeStruct((bsz, s, r2), F32), jax.ShapeDtypeStruct((kw, rw), F32),
                   jax.ShapeDtypeStruct((1, rw), F32), jax.ShapeDtypeStruct((heads, blk, 2 * blk), F32),
                   jax.ShapeDtypeStruct((heads, 1, 2 * blk), F32), jax.ShapeDtypeStruct((1, rw), F32),
                   jax.ShapeDtypeStruct((1, rw), F32), jax.ShapeDtypeStruct((1, rw), F32)],
        scratch_shapes=[pltpu.VMEM((2, s, blk), F32)],
        compiler_params=_params(("parallel", "arbitrary", "arbitrary")),
    )(h, h, cw, cb, wg, bg, lam, hs, dy)


def _exchange(arrs, gather, *, name):
    n = len(arrs)
    slab = [a.shape if g else a.shape[1:] for a, g in zip(arrs, gather)]

    def body(*refs):
        ins, outs = refs[:n], refs[n:2 * n]
        send_sem, recv_sem, own_sem = refs[2 * n:]
        x, y, c = (lax.axis_index(ax) for ax in MESH_AXES)
        me = 4 * x + 2 * y + c
        copies = []
        for i in range(n):
            def src(dev, i=i):
                return ins[i] if gather[i] else ins[i].at[dev]
            own = pltpu.make_async_copy(src(me), outs[i].at[me], own_sem.at[i])
            own.start()
            copies.append(own)
            for d in range(1, N_DEV):
                px = 1 - x if d & 4 else x
                py = 1 - y if d & 2 else y
                pc = 1 - c if d & 1 else c
                cp = pltpu.make_async_remote_copy(
                    src_ref=src(4 * px + 2 * py + pc), dst_ref=outs[i].at[me],
                    send_sem=send_sem.at[i, d - 1], recv_sem=recv_sem.at[i, d - 1],
                    device_id=(px, py, pc), device_id_type=pl.DeviceIdType.MESH)
                cp.start()
                copies.append(cp)
        for cp in copies:
            cp.wait()

    hbm = pl.BlockSpec(memory_space=pl.ANY)
    return pl.pallas_call(
        body, name=name, in_specs=[hbm] * n, out_specs=[hbm] * n,
        out_shape=[jax.ShapeDtypeStruct((N_DEV,) + tuple(s), a.dtype) for s, a in zip(slab, arrs)],
        scratch_shapes=[pltpu.SemaphoreType.DMA((n, N_DEV - 1)), pltpu.SemaphoreType.DMA((n, N_DEV - 1)),
                        pltpu.SemaphoreType.DMA((n,))],
    )(*arrs)


def _adamw(parts, w, m, v, *, name, tr=256):
    r, c = w.shape
    tr = _tile(r, tr, SUBLANES)
    bc1 = 1.0 / (1.0 - ADAM_B1 ** ADAM_STEP)
    bc2 = 1.0 / (1.0 - ADAM_B2 ** ADAM_STEP)

    def body(p_ref, w_ref, m_ref, v_ref, g_ref, d_ref, mo_ref, vo_ref):
        g = p_ref[0]
        for s in range(1, N_DEV):
            g = g + p_ref[s]
        m_new = ADAM_B1 * m_ref[...] + (1.0 - ADAM_B1) * g
        v_new = ADAM_B2 * v_ref[...] + (1.0 - ADAM_B2) * (g * g)
        g_ref[...] = g
        mo_ref[...] = m_new
        vo_ref[...] = v_new
        d_ref[...] = -ADAM_LR * ((m_new * bc1) / (jnp.sqrt(v_new * bc2) + ADAM_EPS) + ADAM_WD * w_ref[...])

    blk = pl.BlockSpec((tr, c), lambda i: (i, 0))
    return pl.pallas_call(
        body, name=name, grid=(r // tr,),
        in_specs=[pl.BlockSpec((N_DEV, tr, c), lambda i: (0, i, 0)), blk, blk, blk],
        out_specs=[blk] * 4, out_shape=[jax.ShapeDtypeStruct((r, c), F32)] * 4,
        compiler_params=_params(("parallel",)),
    )(parts, w, m, v)


def _whole(slabs, axis):
    x = jnp.moveaxis(slabs, 0, axis)
    shp = x.shape
    return x.reshape(shp[:axis] + (shp[axis] * shp[axis + 1],) + shp[axis + 2:])


def _slabs(whole, axis):
    shp = whole.shape
    x = whole.reshape(shp[:axis] + (N_DEV, shp[axis] // N_DEV) + shp[axis + 1:])
    return jnp.moveaxis(x, axis, 0)


def _pack(vecs, rows):
    flat = jnp.concatenate(vecs, axis=-1)
    pad = rows * LANES - flat.shape[-1]
    flat = jnp.pad(flat, [(0, 0)] * (flat.ndim - 1) + [(0, pad)])
    return flat.reshape(flat.shape[:-1] + (rows, LANES))


def _unpack(packed, sizes):
    flat = packed.reshape(packed.shape[:-2] + (-1,))
    out, pos = [], 0
    for n in sizes:
        out.append(flat[..., pos:pos + n])
        pos += n
    return out


def _pack_rows(sizes):
    total = sum(sizes)
    return -(-total // (LANES * SUBLANES)) * SUBLANES


BIG = {"sc_w_in": 2, "sc_w_out": 1, "lru_w_in": 2, "lru_w_gate": 3, "lru_w_out": 1, "ffn_w_up": 2, "ffn_w_down": 1}
SMALL = ["sc_conv_w", "lru_b_in", "lru_conv_w", "lru_conv_b", "lru_b_gate", "lru_lambda", "ffn_conv_w", "ln_g", "ln_b"]
REPL = ["sc_conv_b", "ffn_conv_b"]
WEIGHTS = ["sc_w_in", "sc_conv_w", "sc_conv_b", "sc_w_out", "lru_w_in", "lru_b_in", "lru_conv_w", "lru_conv_b",
           "lru_w_gate", "lru_b_gate", "lru_lambda", "lru_w_out", "ffn_w_up", "ffn_conv_w", "ffn_conv_b", "ffn_w_down",
           "ln_g", "ln_b"]


def _step(x, loss_target, w, m, v):
    bsz, s, d = x.shape
    t = bsz * s
    depth = w["ffn_w_up"].shape[0]
    alpha = (2.0 * depth) ** 0.25
    heads = w["lru_w_gate"].shape[1]

    small_sizes = [w[k].size for k in SMALL]
    small_rows = _pack_rows(small_sizes)
    small_local = _pack([w[k].reshape(1, -1) for k in SMALL], small_rows)[0]
    big_names = list(BIG)
    got = _exchange([w[k].astype(BF16) for k in big_names] + [small_local], [True] * (len(big_names) + 1),
                    name="gather_weights")
    full = {k: _whole(g, BIG[k]) for k, g in zip(big_names, got[:-1])}
    for k, seg in zip(SMALL, _unpack(got[-1], small_sizes)):
        full[k] = _whole(seg.reshape((N_DEV,) + w[k].shape), w[k].ndim - 1)
    full["sc_conv_b"] = w["sc_conv_b"]
    full["ffn_conv_b"] = w["ffn_conv_b"]

    xt = x.reshape(t, d)
    saved = []
    for i in range(depth):
        j = i // 2
        lng, lnb = full["ln_g"][i], full["ln_b"][i]
        sv = {"x0": xt}
        if i % 2 == 0:
            hm = _mm(xt, full["sc_w_in"][j], name="sc_in")
            q = _sc_fwd(hm.reshape(bsz, s, -1), full["sc_conv_w"][j], full["sc_conv_b"][j:j + 1], name="sc_mix")
            w_out = full["sc_w_out"][j]
        else:
            hm = _mm(xt, full["lru_w_in"][j], bias=full["lru_b_in"][j:j + 1], name="lru_in")
            q, hs = _lru_fwd(hm.reshape(bsz, s, -1), full["lru_conv_w"][j], full["lru_conv_b"][j:j + 1],
                             full["lru_w_gate"][j], full["lru_b_gate"][j].reshape(heads, 1, -1),
                             full["lru_lambda"][j:j + 1], name="lru_mix")
            sv["hs"] = hs
            w_out = full["lru_w_out"][j]
        q = q.reshape(t, -1)
        z1, x1 = _mm_ln(q, w_out, xt, alpha, lng[0:1], lnb[0:1], name="mix_out_ln")
        hf = _mm(x1, full["ffn_w_up"][i], name="ffn_up")
        a = _ffn_fwd(hf.reshape(bsz, s, -1), full["ffn_conv_w"][i], full["ffn_conv_b"][i:i + 1], name="ffn_act")
        a = a.reshape(t, -1)
        z2, x2 = _mm_ln(a, full["ffn_w_down"][i], x1, alpha, lng[1:2], lnb[1:2], name="ffn_down_ln")
        sv.update(hm=hm, q=q, z1=z1, x1=x1, hf=hf, a=a, z2=z2)
        saved.append(sv)
        xt = x2

    sq, dx = _loss_head(xt, loss_target.reshape(t, d), name="loss_head")
    loss = lax.psum((0.5 / d) * sq[0, 0], MESH_AXES)

    grads = {k: [None] * w[k].shape[0] for k in WEIGHTS}
    for i in reversed(range(depth)):
        j = i // 2
        sv = saved[i]
        lng = full["ln_g"][i]
        dz2, dg2, db2 = _ln_bwd(dx, sv["z2"], lng[1:2], name="ln_bwd")
        da = _mm(dz2, full["ffn_w_down"][i], trans_w=True, name="ffn_down_dx")
        grads["ffn_w_down"][i] = _mm_tn(sv["a"], dz2, name="ffn_down_dw")
        dhf, dwg, dwv, dbg, dbv = _ffn_bwd(sv["hf"].reshape(bsz, s, -1), da.reshape(bsz, s, -1), full["ffn_conv_w"][i],
                                           full["ffn_conv_b"][i:i + 1], name="ffn_act_bwd")
        dhf = dhf.reshape(t, -1)
        grads["ffn_conv_w"][i] = jnp.concatenate([dwg, dwv], axis=1)
        grads["ffn_conv_b"][i] = jnp.concatenate([dbg, dbv], axis=1)[0]
        grads["ffn_w_up"][i] = _mm_tn(sv["x1"], dhf, name="ffn_up_dw")
        dx1 = _mm(dhf, full["ffn_w_up"][i], trans_w=True, resid=dz2, resid_scale=alpha, name="ffn_up_dx")
        dz1, dg1, db1 = _ln_bwd(dx1, sv["z1"], lng[0:1], name="ln_bwd")
        grads["ln_g"][i] = jnp.concatenate([dg1, dg2], axis=0)
        grads["ln_b"][i] = jnp.concatenate([db1, db2], axis=0)
        if i % 2 == 0:
            dq = _mm(dz1, full["sc_w_out"][j], trans_w=True, name="sc_out_dx")
            grads["sc_w_out"][j] = _mm_tn(sv["q"], dz1, name="sc_out_dw")
            dhm, dcw, dcb = _sc_bwd(sv["hm"].reshape(bsz, s, -1), dq.reshape(bsz, s, -1), full["sc_conv_w"][j],
                                    full["sc_conv_b"][j:j + 1], name="sc_mix_bwd")
            dhm = dhm.reshape(t, -1)
            grads["sc_conv_w"][j] = dcw
            grads["sc_conv_b"][j] = dcb[0]
            grads["sc_w_in"][j] = _mm_tn(sv["x0"], dhm, name="sc_in_dw")
            dx = _mm(dhm, full["sc_w_in"][j], trans_w=True, resid=dz1, resid_scale=alpha, name="sc_in_dx")
        else:
            dq = _mm(dz1, full["lru_w_out"][j], trans_w=True, name="lru_out_dx")
            grads["lru_w_out"][j] = _mm_tn(sv["q"], dz1, name="lru_out_dw")
            dhm, dcw, dcb, dwgt, dbgt, dlam, sgb, srb = _lru_bwd(
                sv["hm"].reshape(bsz, s, -1), sv["hs"], dq.reshape(bsz, s, -1), full["lru_conv_w"][j],
                full["lru_conv_b"][j:j + 1], full["lru_w_gate"][j], full["lru_b_gate"][j].reshape(heads, 1, -1),
                full["lru_lambda"][j:j + 1], name="lru_mix_bwd")
            dhm = dhm.reshape(t, -1)
            grads["lru_conv_w"][j] = dcw
            grads["lru_conv_b"][j] = dcb[0]
            grads["lru_w_gate"][j] = dwgt
            grads["lru_b_gate"][j] = dbgt[:, 0, :]
            grads["lru_lambda"][j] = dlam[0]
            grads["lru_b_in"][j] = jnp.concatenate([sgb, srb], axis=1)[0]
            grads["lru_w_in"][j] = _mm_tn(sv["x0"], dhm, name="lru_in_dw")
            dx = _mm(dhm, full["lru_w_in"][j], trans_w=True, resid=dz1, resid_scale=alpha, name="lru_in_dx")
    grad_x = dx.reshape(bsz, s, d)
    grads = {k: jnp.stack(g) for k, g in grads.items()}

    send = [_slabs(grads[k], BIG[k]) for k in big_names]
    small_send = _pack([_slabs(grads[k], grads[k].ndim - 1).reshape(N_DEV, -1) for k in SMALL], small_rows)
    repl_sizes = [w[k].size for k in REPL]
    repl_rows = _pack_rows(repl_sizes)
    repl_send = _pack([grads[k].reshape(1, -1) for k in REPL], repl_rows)[0]
    got = _exchange(send + [small_send, repl_send], [False] * (len(big_names) + 1) + [True], name="scatter_grads")

    out = {}

    def update(key, parts, wk, mk, vk):
        g, dl, mn, vn = _adamw(parts, wk, mk, vk, name="adamw_" + key)
        return g, dl, mn, vn

    for k, parts in zip(big_names, got[:len(big_names)]):
        shp = w[k].shape
        c2 = shp[-1]
        res = update(k, parts.reshape(N_DEV, -1, c2), w[k].reshape(-1, c2), m[k].reshape(-1, c2), v[k].reshape(-1, c2))
        out[k] = [r.reshape(shp) for r in res]
    pk = lambda src, names, rows: _pack([src[k].reshape(1, -1) for k in names], rows)[0]
    res = update("small", got[-2], small_local, pk(m, SMALL, small_rows), pk(v, SMALL, small_rows))
    for r_i, r in enumerate(res):
        for k, seg in zip(SMALL, _unpack(r, small_sizes)):
            out.setdefault(k, [None] * 4)[r_i] = seg.reshape(w[k].shape)
    res = update("repl", got[-1], pk(w, REPL, repl_rows), pk(m, REPL, repl_rows), pk(v, REPL, repl_rows))
    for r_i, r in enumerate(res):
        for k, seg in zip(REPL, _unpack(r, repl_sizes)):
            out.setdefault(k, [None] * 4)[r_i] = seg.reshape(w[k].shape)

    return (loss, grad_x, *[out[k][0] for k in WEIGHTS], *[out[k][1] for k in WEIGHTS],
            *[out[k][2] for k in WEIGHTS], *[out[k][3] for k in WEIGHTS])


def kernel(x, sc_w_in, sc_conv_w, sc_conv_b, sc_w_out, lru_w_in, lru_b_in, lru_conv_w, lru_conv_b, lru_w_gate, lru_b_gate, lru_lambda, lru_w_out, ffn_w_up, ffn_conv_w, ffn_conv_b, ffn_w_down, ln_g, ln_b, loss_target, m_sc_w_in, m_sc_conv_w, m_sc_conv_b, m_sc_w_out, m_lru_w_in, m_lru_b_in, m_lru_conv_w, m_lru_conv_b, m_lru_w_gate, m_lru_b_gate, m_lru_lambda, m_lru_w_out, m_ffn_w_up, m_ffn_conv_w, m_ffn_conv_b, m_ffn_w_down, m_ln_g, m_ln_b, v_sc_w_in, v_sc_conv_w, v_sc_conv_b, v_sc_w_out, v_lru_w_in, v_lru_b_in, v_lru_conv_w, v_lru_conv_b, v_lru_w_gate, v_lru_b_gate, v_lru_lambda, v_lru_w_out, v_ffn_w_up, v_ffn_conv_w, v_ffn_conv_b, v_ffn_w_down, v_ln_g, v_ln_b):
    w = dict(sc_w_in=sc_w_in, sc_conv_w=sc_conv_w, sc_conv_b=sc_conv_b, sc_w_out=sc_w_out, lru_w_in=lru_w_in,
             lru_b_in=lru_b_in, lru_conv_w=lru_conv_w, lru_conv_b=lru_conv_b, lru_w_gate=lru_w_gate,
             lru_b_gate=lru_b_gate, lru_lambda=lru_lambda, lru_w_out=lru_w_out, ffn_w_up=ffn_w_up,
             ffn_conv_w=ffn_conv_w, ffn_conv_b=ffn_conv_b, ffn_w_down=ffn_w_down, ln_g=ln_g, ln_b=ln_b)
    m = dict(sc_w_in=m_sc_w_in, sc_conv_w=m_sc_conv_w, sc_conv_b=m_sc_conv_b, sc_w_out=m_sc_w_out, lru_w_in=m_lru_w_in,
             lru_b_in=m_lru_b_in, lru_conv_w=m_lru_conv_w, lru_conv_b=m_lru_conv_b, lru_w_gate=m_lru_w_gate,
             lru_b_gate=m_lru_b_gate, lru_lambda=m_lru_lambda, lru_w_out=m_lru_w_out, ffn_w_up=m_ffn_w_up,
             ffn_conv_w=m_ffn_conv_w, ffn_conv_b=m_ffn_conv_b, ffn_w_down=m_ffn_w_down, ln_g=m_ln_g, ln_b=m_ln_b)
    v = dict(sc_w_in=v_sc_w_in, sc_conv_w=v_sc_conv_w, sc_conv_b=v_sc_conv_b, sc_w_out=v_sc_w_out, lru_w_in=v_lru_w_in,
             lru_b_in=v_lru_b_in, lru_conv_w=v_lru_conv_w, lru_conv_b=v_lru_conv_b, lru_w_gate=v_lru_w_gate,
             lru_b_gate=v_lru_b_gate, lru_lambda=v_lru_lambda, lru_w_out=v_lru_w_out, ffn_w_up=v_ffn_w_up,
             ffn_conv_w=v_ffn_conv_w, ffn_conv_b=v_ffn_conv_b, ffn_w_down=v_ffn_w_down, ln_g=v_ln_g, ln_b=v_ln_b)
    return _step(x, loss_target, w, m, v)
```

```python
import functools
import math

import jax
import jax.numpy as jnp
from jax import lax
from jax.experimental import pallas as pl
from jax.experimental.pallas import tpu as pltpu

F32 = jnp.float32
BF16 = jnp.bfloat16

N_DEV = 8
MESH_AXES = ("x", "y", "c")
LANES = 128
SUBLANES = 8
VMEM_LIMIT = 56 * 1024 * 1024
MM_LHS_ELEMS = 6 * 1024 * 1024

LRU_C = 8.0
LN_EPS = 1e-5
ADAM_LR = 0.001
ADAM_B1 = 0.9
ADAM_B2 = 0.999
ADAM_EPS = 1e-08
ADAM_WD = 0.01
ADAM_STEP = 10
GELU_K = math.sqrt(2.0 / math.pi)
GELU_C = 0.044715


def _tile(n, target, align):
    if n <= target:
        return n
    t = (target // align) * align
    while t >= align:
        if n % t == 0:
            return t
        t -= align
    return n


def _params(sem):
    return pltpu.CompilerParams(dimension_semantics=sem, vmem_limit_bytes=VMEM_LIMIT)


def _rows(x):
    return lax.broadcasted_iota(jnp.int32, x.shape, 0)


def _shift_dn(x, k, fill=0.0):
    if k == 0:
        return x
    return jnp.where(_rows(x) >= k, pltpu.roll(x, k, 0), fill)


def _shift_up(x, k, fill=0.0):
    if k == 0:
        return x
    s = x.shape[0]
    return jnp.where(_rows(x) < s - k, pltpu.roll(x, s - k, 0), fill)


def _conv_fwd(x, w, b):
    kw = w.shape[0]
    y = _shift_dn(x, kw - 1) * w[0:1, :] + b
    for k in range(1, kw):
        y = y + _shift_dn(x, kw - 1 - k) * w[k:k + 1, :]
    return y


def _conv_bwd_x(dy, w):
    kw = w.shape[0]
    dx = _shift_up(dy, kw - 1) * w[0:1, :]
    for k in range(1, kw):
        dx = dx + _shift_up(dy, kw - 1 - k) * w[k:k + 1, :]
    return dx


def _conv_bwd_w(dy, x, kw):
    return [jnp.sum(dy * _shift_dn(x, kw - 1 - k), axis=0, keepdims=True) for k in range(kw)]


def _accumulate(first, items):
    flat = []
    for ref, val in items:
        if isinstance(val, list):
            flat += [(ref, (slice(k, k + 1), slice(None)), row) for k, row in enumerate(val)]
        else:
            flat.append((ref, Ellipsis, val))

    @pl.when(first)
    def _():
        for ref, idx, val in flat:
            ref[idx] = val

    @pl.when(jnp.logical_not(first))
    def _():
        for ref, idx, val in flat:
            ref[idx] += val


def _colsum(x):
    return jnp.sum(x, axis=0, keepdims=True)


def _sigmoid(x):
    return 1.0 / (1.0 + jnp.exp(-x))


def _log1p(x):
    u = 1.0 + x
    return jnp.where(u == 1.0, x, jnp.log(u) * (x / (u - 1.0)))


def _softplus(x):
    return jnp.maximum(x, 0.0) + _log1p(jnp.exp(-jnp.abs(x)))


def _expm1(x, ex):
    poly = x * (1.0 + x * (0.5 + x * (1.0 / 6.0 + x * (1.0 / 24.0 + x * (1.0 / 120.0 + x * (1.0 / 720.0))))))
    return jnp.where(jnp.abs(x) < 0.25, poly, ex - 1.0)


def _gelu(x):
    t = jnp.tanh(GELU_K * (x + GELU_C * x * x * x))
    return 0.5 * x * (1.0 + t)


def _gelu_and_grad(x):
    x2 = x * x
    t = jnp.tanh(GELU_K * (x + GELU_C * x * x2))
    g = 0.5 * x * (1.0 + t)
    dg = 0.5 * (1.0 + t) + 0.5 * x * (1.0 - t * t) * (GELU_K * (1.0 + 3.0 * GELU_C * x2))
    return g, dg


def _scan_fwd(a, b):
    s = a.shape[0]
    k = 1
    while k < s:
        b = a * _shift_dn(b, k) + b
        if 2 * k < s:
            a = a * _shift_dn(a, k, 1.0)
        k *= 2
    return b


def _scan_rev(c, v):
    s = c.shape[0]
    k = 1
    while k < s:
        v = c * _shift_up(v, k) + v
        if 2 * k < s:
            c = c * _shift_up(c, k, 1.0)
        k *= 2
    return v


def _mm(a, w, *, name, trans_w=False, bias=None, resid=None, resid_scale=1.0, tn=512):
    m, k = a.shape
    n = w.shape[0] if trans_w else w.shape[1]
    tm = _tile(m, min(1024, max(256, MM_LHS_ELEMS // k)), SUBLANES)
    tn = _tile(n, tn, LANES)
    has_bias = bias is not None
    has_resid = resid is not None

    def body(*refs):
        a_ref, w_ref = refs[0], refs[1]
        pos = 2
        b_ref = r_ref = None
        if has_bias:
            b_ref = refs[pos]
            pos += 1
        if has_resid:
            r_ref = refs[pos]
            pos += 1
        o_ref = refs[pos]

        if trans_w:
            acc = lax.dot_general(a_ref[...], w_ref[...], (((1,), (1,)), ((), ())), preferred_element_type=F32)
        else:
            acc = jnp.dot(a_ref[...], w_ref[...], preferred_element_type=F32)
        if has_bias:
            acc = acc + b_ref[...]
        if has_resid:
            acc = acc + resid_scale * r_ref[...]
        o_ref[...] = acc

    in_specs = [pl.BlockSpec((tm, k), lambda i, j: (i, 0))]
    if trans_w:
        in_specs.append(pl.BlockSpec((tn, k), lambda i, j: (j, 0)))
    else:
        in_specs.append(pl.BlockSpec((k, tn), lambda i, j: (0, j)))
    args = [a, w]
    if has_bias:
        in_specs.append(pl.BlockSpec((1, tn), lambda i, j: (0, j)))
        args.append(bias)
    if has_resid:
        in_specs.append(pl.BlockSpec((tm, tn), lambda i, j: (i, j)))
        args.append(resid)
    return pl.pallas_call(
        body, name=name, grid=(m // tm, n // tn), in_specs=in_specs,
        out_specs=pl.BlockSpec((tm, tn), lambda i, j: (i, j)),
        out_shape=jax.ShapeDtypeStruct((m, n), F32),
        compiler_params=_params(("parallel", "arbitrary")),
    )(*args)


def _mm_ln(a, w, resid, alpha, g, b, *, name, tm=512):
    m, k = a.shape
    d = w.shape[1]
    tm = _tile(m, tm, SUBLANES)

    def body(a_ref, w_ref, r_ref, g_ref, b_ref, z_ref, o_ref, obf_ref):
        y = jnp.dot(a_ref[...], w_ref[...], preferred_element_type=F32)
        z = alpha * r_ref[...] + y
        z_ref[...] = z
        mu = jnp.mean(z, axis=-1, keepdims=True)
        zc = z - mu
        var = jnp.mean(zc * zc, axis=-1, keepdims=True)
        o = zc * lax.rsqrt(var + LN_EPS) * g_ref[...] + b_ref[...]
        o_ref[...] = o
        obf_ref[...] = o.astype(BF16)

    row = pl.BlockSpec((tm, d), lambda i: (i, 0))
    vec = pl.BlockSpec((1, d), lambda i: (0, 0))
    return pl.pallas_call(
        body, name=name, grid=(m // tm,),
        in_specs=[pl.BlockSpec((tm, k), lambda i: (i, 0)), pl.BlockSpec((k, d), lambda i: (0, 0)), row, vec, vec],
        out_specs=[row, row, row],
        out_shape=[jax.ShapeDtypeStruct((m, d), F32), jax.ShapeDtypeStruct((m, d), F32),
                   jax.ShapeDtypeStruct((m, d), BF16)],
        compiler_params=_params(("parallel",)),
    )(a, w, resid, g, b)


def _mm_tn(a, b, *, name, tm=1408, tn=1536, tk=1024):
    t, m = a.shape
    n = b.shape[1]
    tm = _tile(m, tm, LANES)
    tn = _tile(n, tn, LANES)
    tk = _tile(t, tk, SUBLANES)

    def body(a_ref, b_ref, o_ref):
        @pl.when(pl.program_id(2) == 0)
        def _():
            o_ref[...] = jnp.zeros_like(o_ref)

        o_ref[...] += lax.dot_general(a_ref[...], b_ref[...], (((0,), (0,)), ((), ())), preferred_element_type=F32)

    return pl.pallas_call(
        body, name=name, grid=(m // tm, n // tn, t // tk),
        in_specs=[pl.BlockSpec((tk, tm), lambda i, j, l: (l, i)), pl.BlockSpec((tk, tn), lambda i, j, l: (l, j))],
        out_specs=pl.BlockSpec((tm, tn), lambda i, j, l: (i, j)),
        out_shape=jax.ShapeDtypeStruct((m, n), F32),
        compiler_params=_params(("parallel", "parallel", "arbitrary")),
    )(a, b)


def _ln_bwd(dout, z, g, *, name, tm=512):
    t, d = z.shape
    tm = _tile(t, tm, SUBLANES)

    def body(do_ref, z_ref, g_ref, dz_ref, dzbf_ref, dg_ref, db_ref):
        @pl.when(pl.program_id(0) == 0)
        def _():
            dg_ref[...] = jnp.zeros_like(dg_ref)
            db_ref[...] = jnp.zeros_like(db_ref)

        z = z_ref[...]
        do = do_ref[...]
        mu = jnp.mean(z, axis=-1, keepdims=True)
        zc = z - mu
        var = jnp.mean(zc * zc, axis=-1, keepdims=True)
        rstd = lax.rsqrt(var + LN_EPS)
        xhat = zc * rstd
        dxh = do * g_ref[...]
        m1 = jnp.mean(dxh, axis=-1, keepdims=True)
        m2 = jnp.mean(dxh * xhat, axis=-1, keepdims=True)
        dz = rstd * (dxh - m1 - xhat * m2)
        dz_ref[...] = dz
        dzbf_ref[...] = dz.astype(BF16)
        dg_ref[...] += _colsum(do * xhat)
        db_ref[...] += _colsum(do)

    row = pl.BlockSpec((tm, d), lambda i: (i, 0))
    vec = pl.BlockSpec((1, d), lambda i: (0, 0))
    return pl.pallas_call(
        body, name=name, grid=(t // tm,), in_specs=[row, row, vec], out_specs=[row, row, vec, vec],
        out_shape=[jax.ShapeDtypeStruct((t, d), F32), jax.ShapeDtypeStruct((t, d), BF16),
                   jax.ShapeDtypeStruct((1, d), F32), jax.ShapeDtypeStruct((1, d), F32)],
        compiler_params=_params(("arbitrary",)),
    )(dout, z, g)


def _loss_head(y, target, *, name, tm=512):
    t, d = y.shape
    tm = _tile(t, tm, SUBLANES)

    def body(y_ref, t_ref, s_ref, dy_ref):
        @pl.when(pl.program_id(0) == 0)
        def _():
            s_ref[...] = jnp.zeros_like(s_ref)

        e = y_ref[...] - t_ref[...]
        dy_ref[...] = e * (1.0 / d)
        s_ref[...] += jnp.sum(_colsum(e * e), axis=-1, keepdims=True)

    row = pl.BlockSpec((tm, d), lambda i: (i, 0))
    return pl.pallas_call(
        body, name=name, grid=(t // tm,), in_specs=[row, row],
        out_specs=[pl.BlockSpec((1, LANES), lambda i: (0, 0)), row],
        out_shape=[jax.ShapeDtypeStruct((1, LANES), F32), jax.ShapeDtypeStruct((t, d), F32)],
        compiler_params=_params(("arbitrary",)),
    )(y, target)


def _strip(s, tc, off):
    return pl.BlockSpec((None, s, tc), lambda c, b, *_: (b, 0, off + c))


def _cvec(kw, tc, off):
    return pl.BlockSpec((kw, tc), lambda c, b, *_: (0, off + c))


def _acc(kw, tc):
    return pl.BlockSpec((kw, tc), lambda c, b, *_: (0, c))


def _sc_fwd(h, cw, cb, *, name, tc=256):
    bsz, s, d3 = h.shape
    d = d3 // 3
    tc = _tile(d, tc, LANES)
    nc = d // tc

    def body(gb_ref, gc_ref, v_ref, w_ref, b_ref, q_ref):
        u = _conv_fwd(gc_ref[...] * v_ref[...], w_ref[...], b_ref[...])
        q_ref[...] = (gb_ref[...] * u).astype(BF16)

    return pl.pallas_call(
        body, name=name, grid=(nc, bsz),
        in_specs=[_strip(s, tc, 0), _strip(s, tc, nc), _strip(s, tc, 2 * nc), _cvec(cw.shape[0], tc, 0), _cvec(1, tc, 0)],
        out_specs=_strip(s, tc, 0),
        out_shape=jax.ShapeDtypeStruct((bsz, s, d), BF16),
        compiler_params=_params(("parallel", "parallel")),
    )(h, h, h, cw, cb)


def _sc_bwd(h, dq, cw, cb, *, name, tc=256):
    bsz, s, d3 = h.shape
    d = d3 // 3
    kw = cw.shape[0]
    tc = _tile(d, tc, LANES)
    nc = d // tc

    def body(gb_ref, gc_ref, v_ref, dq_ref, w_ref, b_ref, dh_ref, dw_ref, db_ref, parts):
        b_id, part = pl.program_id(1), pl.program_id(2)

        @pl.when(part == 0)
        def _():
            gb, gc, v, dq_, w = gb_ref[...], gc_ref[...], v_ref[...], dq_ref[...], w_ref[...]
            p = gc * v
            u = _conv_fwd(p, w, b_ref[...])
            du = dq_ * gb
            dp = _conv_bwd_x(du, w)
            parts[0] = (dq_ * u).astype(BF16)
            parts[1] = (dp * v).astype(BF16)
            parts[2] = (dp * gc).astype(BF16)
            _accumulate(b_id == 0, [(dw_ref, _conv_bwd_w(du, p, kw)), (db_ref, _colsum(du))])

        dh_ref[...] = parts[part]

    return pl.pallas_call(
        body, name=name, grid=(nc, bsz, 3),
        in_specs=[_strip(s, tc, 0), _strip(s, tc, nc), _strip(s, tc, 2 * nc), _strip(s, tc, 0),
                  _cvec(kw, tc, 0), _cvec(1, tc, 0)],
        out_specs=[pl.BlockSpec((None, s, tc), lambda c, b, p: (b, 0, p * nc + c)), _acc(kw, tc), _acc(1, tc)],
        out_shape=[jax.ShapeDtypeStruct((bsz, s, d3), BF16), jax.ShapeDtypeStruct((kw, d), F32),
                   jax.ShapeDtypeStruct((1, d), F32)],
        scratch_shapes=[pltpu.VMEM((3, s, tc), BF16)],
        compiler_params=_params(("parallel", "arbitrary", "arbitrary")),
    )(h, h, h, dq, cw, cb)


def _ffn_fwd(h, cw, cb, *, name, tc=256):
    bsz, s, f2 = h.shape
    f = f2 // 2
    tc = _tile(f, tc, LANES)
    nc = f // tc

    def body(hg_ref, hv_ref, wg_ref, wv_ref, bg_ref, bv_ref, a_ref):
        g = _conv_fwd(hg_ref[...], wg_ref[...], bg_ref[...])
        v = _conv_fwd(hv_ref[...], wv_ref[...], bv_ref[...])
        a_ref[...] = (g * _sigmoid(g) * v).astype(BF16)

    kw = cw.shape[0]
    return pl.pallas_call(
        body, name=name, grid=(nc, bsz),
        in_specs=[_strip(s, tc, 0), _strip(s, tc, nc), _cvec(kw, tc, 0), _cvec(kw, tc, nc), _cvec(1, tc, 0), _cvec(1, tc, nc)],
        out_specs=_strip(s, tc, 0),
        out_shape=jax.ShapeDtypeStruct((bsz, s, f), BF16),
        compiler_params=_params(("parallel", "parallel")),
    )(h, h, cw, cw, cb, cb)


def _ffn_bwd(h, da, cw, cb, *, name, tc=256):
    bsz, s, f2 = h.shape
    f = f2 // 2
    kw = cw.shape[0]
    tc = _tile(f, tc, LANES)
    nc = f // tc

    def body(hg_ref, hv_ref, da_ref, wg_ref, wv_ref, bg_ref, bv_ref,
             dh_ref, dwg_ref, dwv_ref, dbg_ref, dbv_ref, parts):
        b_id, part = pl.program_id(1), pl.program_id(2)

        @pl.when(part == 0)
        def _():
            hg, hv, da_ = hg_ref[...], hv_ref[...], da_ref[...]
            wg, wv = wg_ref[...], wv_ref[...]
            g = _conv_fwd(hg, wg, bg_ref[...])
            v = _conv_fwd(hv, wv, bv_ref[...])
            sg = _sigmoid(g)
            dv = da_ * (g * sg)
            dg = da_ * v * (sg * (1.0 + g * (1.0 - sg)))
            parts[0] = _conv_bwd_x(dg, wg).astype(BF16)
            parts[1] = _conv_bwd_x(dv, wv).astype(BF16)
            _accumulate(b_id == 0, [(dwg_ref, _conv_bwd_w(dg, hg, kw)), (dwv_ref, _conv_bwd_w(dv, hv, kw)),
                                    (dbg_ref, _colsum(dg)), (dbv_ref, _colsum(dv))])

        dh_ref[...] = parts[part]

    return pl.pallas_call(
        body, name=name, grid=(nc, bsz, 2),
        in_specs=[_strip(s, tc, 0), _strip(s, tc, nc), _strip(s, tc, 0), _cvec(kw, tc, 0), _cvec(kw, tc, nc),
                  _cvec(1, tc, 0), _cvec(1, tc, nc)],
        out_specs=[pl.BlockSpec((None, s, tc), lambda c, b, p: (b, 0, p * nc + c)),
                   _acc(kw, tc), _acc(kw, tc), _acc(1, tc), _acc(1, tc)],
        out_shape=[jax.ShapeDtypeStruct((bsz, s, f2), BF16), jax.ShapeDtypeStruct((kw, f), F32),
                   jax.ShapeDtypeStruct((kw, f), F32), jax.ShapeDtypeStruct((1, f), F32), jax.ShapeDtypeStruct((1, f), F32)],
        scratch_shapes=[pltpu.VMEM((2, s, tc), BF16)],
        compiler_params=_params(("parallel", "arbitrary", "arbitrary")),
    )(h, h, da, cw, cw, cb, cb)


def _lru_gates(r, cw, cb, wg, bg, lam):
    blk = r.shape[1]
    xr = _conv_fwd(r, cw, cb)
    gates = jnp.dot(xr.astype(BF16), wg, preferred_element_type=F32) + bg
    rg = _sigmoid(gates[:, :blk])
    ig = _sigmoid(gates[:, blk:])
    sp = _softplus(-lam)
    la = (-LRU_C * sp) * rg
    a = jnp.exp(la)
    mult = jnp.sqrt(-_expm1(2.0 * la, a * a))
    return xr, rg, ig, sp, a, mult


def _lru_specs(s, blk, heads, kw):
    return [pl.BlockSpec((None, s, blk), lambda h, b, *_: (b, 0, h)),
            pl.BlockSpec((None, s, blk), lambda h, b, *_: (b, 0, heads + h)),
            pl.BlockSpec((kw, blk), lambda h, b, *_: (0, h)),
            pl.BlockSpec((1, blk), lambda h, b, *_: (0, h)),
            pl.BlockSpec((None, blk, 2 * blk), lambda h, b, *_: (h, 0, 0)),
            pl.BlockSpec((None, 1, 2 * blk), lambda h, b, *_: (h, 0, 0)),
            pl.BlockSpec((1, blk), lambda h, b, *_: (0, h))]


def _lru_fwd(h, cw, cb, wg, bg, lam, *, name):
    bsz, s, r2 = h.shape
    heads, blk = wg.shape[0], wg.shape[1]
    kw = cw.shape[0]

    def body(g_ref, r_ref, cw_ref, cb_ref, wg_ref, bg_ref, lam_ref, y_ref, hs_ref):
        xr, _, ig, _, a, mult = _lru_gates(r_ref[...], cw_ref[...], cb_ref[...], wg_ref[...], bg_ref[...], lam_ref[...])
        hs = _scan_fwd(a, mult * (ig * xr))
        hs_ref[...] = hs
        y_ref[...] = (hs * _gelu(g_ref[...])).astype(BF16)

    out = pl.BlockSpec((None, s, blk), lambda hd, b: (b, 0, hd))
    return pl.pallas_call(
        body, name=name, grid=(heads, bsz), in_specs=_lru_specs(s, blk, heads, kw), out_specs=[out, out],
        out_shape=[jax.ShapeDtypeStruct((bsz, s, r2 // 2), BF16), jax.ShapeDtypeStruct((bsz, s, r2 // 2), F32)],
        compiler_params=_params(("parallel", "parallel")),
    )(h, h, cw, cb, wg, bg, lam)


def _lru_bwd(h, hs, dy, cw, cb, wg, bg, lam, *, name):
    bsz, s, r2 = h.shape
    rw = r2 // 2
    heads, blk = wg.shape[0], wg.shape[1]
    kw = cw.shape[0]

    def body(g_ref, r_ref, cw_ref, cb_ref, wg_ref, bg_ref, lam_ref, hs_ref, dy_ref,
             dh_ref, dcw_ref, dcb_ref, dwg_ref, dbg_ref, dlam_ref, sg_ref, sr_ref, parts):
        b_id, part = pl.program_id(1), pl.program_id(2)

        @pl.when(part == 0)
        def _():
            r, cw_, wg_, lam_ = r_ref[...], cw_ref[...], wg_ref[...], lam_ref[...]
            xr, rg, ig, sp, a, mult = _lru_gates(r, cw_, cb_ref[...], wg_, bg_ref[...], lam_)
            hs_, dy_ = hs_ref[...], dy_ref[...]
            gel, dgel = _gelu_and_grad(g_ref[...])
            dg = dy_ * hs_ * dgel
            lmb = _scan_rev(_shift_up(a, 1, 1.0), dy_ * gel)
            da = lmb * _shift_dn(hs_, 1)
            dmult = lmb * (ig * xr)
            dig = lmb * (mult * xr)
            dxr = lmb * (mult * ig)
            dla = da * a - dmult * (a * a / mult)
            drg = dla * (-LRU_C * sp)
            dsp = _colsum(dla * rg) * (-LRU_C)
            dlam = -dsp * _sigmoid(-lam_)
            dgates = jnp.concatenate([drg * (rg * (1.0 - rg)), dig * (ig * (1.0 - ig))], axis=1)
            dgates_bf = dgates.astype(BF16)
            dwg = lax.dot_general(xr.astype(BF16), dgates_bf, (((0,), (0,)), ((), ())), preferred_element_type=F32)
            dxr = dxr + lax.dot_general(dgates_bf, wg_, (((1,), (1,)), ((), ())), preferred_element_type=F32)
            dr = _conv_bwd_x(dxr, cw_)
            parts[0] = dg.astype(BF16)
            parts[1] = dr.astype(BF16)
            _accumulate(b_id == 0, [(dcw_ref, _conv_bwd_w(dxr, r, kw)), (dcb_ref, _colsum(dxr)), (dwg_ref, dwg),
                                    (dbg_ref, _colsum(dgates)), (dlam_ref, dlam), (sg_ref, _colsum(dg)),
                                    (sr_ref, _colsum(dr))])

        dh_ref[...] = parts[part]

    strip = pl.BlockSpec((None, s, blk), lambda hd, b, p: (b, 0, hd))
    vec = pl.BlockSpec((1, blk), lambda hd, b, p: (0, hd))
    return pl.pallas_call(
        body, name=name, grid=(heads, bsz, 2),
        in_specs=_lru_specs(s, blk, heads, kw) + [strip, strip],
        out_specs=[pl.BlockSpec((None, s, blk), lambda hd, b, p: (b, 0, p * heads + hd)),
                   pl.BlockSpec((kw, blk), lambda hd, b, p: (0, hd)), vec,
                   pl.BlockSpec((None, blk, 2 * blk), lambda hd, b, p: (hd, 0, 0)),
                   pl.BlockSpec((None, 1, 2 * blk), lambda hd, b, p: (hd, 0, 0)), vec, vec, vec],
        out_shape=[jax.ShapeDtypeStruct((bsz, s, r2), BF16), jax.ShapeDtypeStruct((kw, rw), F32),
                   jax.ShapeDtypeStruct((1, rw), F32), jax.ShapeDtypeStruct((heads, blk, 2 * blk), F32),
                   jax.ShapeDtypeStruct((heads, 1, 2 * blk), F32), jax.ShapeDtypeStruct((1, rw), F32),
                   jax.ShapeDtypeStruct((1, rw), F32), jax.ShapeDtypeStruct((1, rw), F32)],
        scratch_shapes=[pltpu.VMEM((2, s, blk), BF16)],
        compiler_params=_params(("parallel", "arbitrary", "arbitrary")),
    )(h, h, cw, cb, wg, bg, lam, hs, dy)


def _exchange(arrs, gather, *, name):
    n = len(arrs)
    slab = [a.shape if g else a.shape[1:] for a, g in zip(arrs, gather)]

    def body(*refs):
        ins, outs = refs[:n], refs[n:2 * n]
        send_sem, recv_sem, own_sem = refs[2 * n:]
        x, y, c = (lax.axis_index(ax) for ax in MESH_AXES)
        me = 4 * x + 2 * y + c
        copies = []
        for i in range(n):
            def src(dev, i=i):
                return ins[i] if gather[i] else ins[i].at[dev]
            own = pltpu.make_async_copy(src(me), outs[i].at[me], own_sem.at[i])
            own.start()
            copies.append(own)
            for d in range(1, N_DEV):
                px = 1 - x if d & 4 else x
                py = 1 - y if d & 2 else y
                pc = 1 - c if d & 1 else c
                cp = pltpu.make_async_remote_copy(
                    src_ref=src(4 * px + 2 * py + pc), dst_ref=outs[i].at[me],
                    send_sem=send_sem.at[i, d - 1], recv_sem=recv_sem.at[i, d - 1],
                    device_id=(px, py, pc), device_id_type=pl.DeviceIdType.MESH)
                cp.start()
                copies.append(cp)
        for cp in copies:
            cp.wait()

    hbm = pl.BlockSpec(memory_space=pl.ANY)
    return pl.pallas_call(
        body, name=name, in_specs=[hbm] * n, out_specs=[hbm] * n,
        out_shape=[jax.ShapeDtypeStruct((N_DEV,) + tuple(s), a.dtype) for s, a in zip(slab, arrs)],
        scratch_shapes=[pltpu.SemaphoreType.DMA((n, N_DEV - 1)), pltpu.SemaphoreType.DMA((n, N_DEV - 1)),
                        pltpu.SemaphoreType.DMA((n,))],
    )(*arrs)


def _adamw(parts, w, m, v, *, name, tr=256):
    r, c = w.shape
    tr = _tile(r, tr, SUBLANES)
    bc1 = 1.0 / (1.0 - ADAM_B1 ** ADAM_STEP)
    bc2 = 1.0 / (1.0 - ADAM_B2 ** ADAM_STEP)

    def body(p_ref, w_ref, m_ref, v_ref, g_ref, d_ref, mo_ref, vo_ref):
        g = p_ref[0]
        for s in range(1, N_DEV):
            g = g + p_ref[s]
        m_new = ADAM_B1 * m_ref[...] + (1.0 - ADAM_B1) * g
        v_new = ADAM_B2 * v_ref[...] + (1.0 - ADAM_B2) * (g * g)
        g_ref[...] = g
        mo_ref[...] = m_new
        vo_ref[...] = v_new
        d_ref[...] = -ADAM_LR * ((m_new * bc1) / (jnp.sqrt(v_new * bc2) + ADAM_EPS) + ADAM_WD * w_ref[...])

    blk = pl.BlockSpec((tr, c), lambda i: (i, 0))
    return pl.pallas_call(
        body, name=name, grid=(r // tr,),
        in_specs=[pl.BlockSpec((N_DEV, tr, c), lambda i: (0, i, 0)), blk, blk, blk],
        out_specs=[blk] * 4, out_shape=[jax.ShapeDtypeStruct((r, c), F32)] * 4,
        compiler_params=_params(("parallel",)),
    )(parts, w, m, v)


def _whole(slabs, axis):
    x = jnp.moveaxis(slabs, 0, axis)
    shp = x.shape
    return x.reshape(shp[:axis] + (shp[axis] * shp[axis + 1],) + shp[axis + 2:])


def _slabs(whole, axis):
    shp = whole.shape
    x = whole.reshape(shp[:axis] + (N_DEV, shp[axis] // N_DEV) + shp[axis + 1:])
    return jnp.moveaxis(x, axis, 0)


def _pack(vecs, rows):
    flat = jnp.concatenate(vecs, axis=-1)
    pad = rows * LANES - flat.shape[-1]
    flat = jnp.pad(flat, [(0, 0)] * (flat.ndim - 1) + [(0, pad)])
    return flat.reshape(flat.shape[:-1] + (rows, LANES))


def _unpack(packed, sizes):
    flat = packed.reshape(packed.shape[:-2] + (-1,))
    out, pos = [], 0
    for n in sizes:
        out.append(flat[..., pos:pos + n])
        pos += n
    return out


def _pack_rows(sizes):
    total = sum(sizes)
    return -(-total // (LANES * SUBLANES)) * SUBLANES


BIG = {"sc_w_in": 2, "sc_w_out": 1, "lru_w_in": 2, "lru_w_gate": 3, "lru_w_out": 1, "ffn_w_up": 2, "ffn_w_down": 1}
SMALL = ["sc_conv_w", "lru_b_in", "lru_conv_w", "lru_conv_b", "lru_b_gate", "lru_lambda", "ffn_conv_w", "ln_g", "ln_b"]
REPL = ["sc_conv_b", "ffn_conv_b"]
WEIGHTS = ["sc_w_in", "sc_conv_w", "sc_conv_b", "sc_w_out", "lru_w_in", "lru_b_in", "lru_conv_w", "lru_conv_b",
           "lru_w_gate", "lru_b_gate", "lru_lambda", "lru_w_out", "ffn_w_up", "ffn_conv_w", "ffn_conv_b", "ffn_w_down",
           "ln_g", "ln_b"]


def _step(x, loss_target, w, m, v):
    bsz, s, d = x.shape
    t = bsz * s
    depth = w["ffn_w_up"].shape[0]
    alpha = (2.0 * depth) ** 0.25
    heads = w["lru_w_gate"].shape[1]

    small_sizes = [w[k].size for k in SMALL]
    small_rows = _pack_rows(small_sizes)
    small_local = _pack([w[k].reshape(1, -1) for k in SMALL], small_rows)[0]
    big_names = list(BIG)
    got = _exchange([w[k].astype(BF16) for k in big_names] + [small_local], [True] * (len(big_names) + 1),
                    name="gather_weights")
    full = {k: _whole(g, BIG[k]) for k, g in zip(big_names, got[:-1])}
    for k, seg in zip(SMALL, _unpack(got[-1], small_sizes)):
        full[k] = _whole(seg.reshape((N_DEV,) + w[k].shape), w[k].ndim - 1)
    full["sc_conv_b"] = w["sc_conv_b"]
    full["ffn_conv_b"] = w["ffn_conv_b"]

    xt = x.reshape(t, d)
    xb = xt.astype(BF16)
    saved = []
    for i in range(depth):
        j = i // 2
        lng, lnb = full["ln_g"][i], full["ln_b"][i]
        sv = {"x0": xb}
        if i % 2 == 0:
            hm = _mm(xb, full["sc_w_in"][j], name="sc_in")
            q = _sc_fwd(hm.reshape(bsz, s, -1), full["sc_conv_w"][j], full["sc_conv_b"][j:j + 1], name="sc_mix")
            w_out = full["sc_w_out"][j]
        else:
            hm = _mm(xb, full["lru_w_in"][j], bias=full["lru_b_in"][j:j + 1], name="lru_in")
            q, hs = _lru_fwd(hm.reshape(bsz, s, -1), full["lru_conv_w"][j], full["lru_conv_b"][j:j + 1],
                             full["lru_w_gate"][j], full["lru_b_gate"][j].reshape(heads, 1, -1),
                             full["lru_lambda"][j:j + 1], name="lru_mix")
            sv["hs"] = hs
            w_out = full["lru_w_out"][j]
        q = q.reshape(t, -1)
        z1, x1, x1b = _mm_ln(q, w_out, xt, alpha, lng[0:1], lnb[0:1], name="mix_out_ln")
        hf = _mm(x1b, full["ffn_w_up"][i], name="ffn_up")
        a = _ffn_fwd(hf.reshape(bsz, s, -1), full["ffn_conv_w"][i], full["ffn_conv_b"][i:i + 1], name="ffn_act")
        a = a.reshape(t, -1)
        z2, xt, xb = _mm_ln(a, full["ffn_w_down"][i], x1, alpha, lng[1:2], lnb[1:2], name="ffn_down_ln")
        sv.update(hm=hm, q=q, z1=z1, x1=x1b, hf=hf, a=a, z2=z2)
        saved.append(sv)

    sq, dx = _loss_head(xt, loss_target.reshape(t, d), name="loss_head")
    loss = lax.psum((0.5 / d) * sq[0, 0], MESH_AXES)

    grads = {k: [None] * w[k].shape[0] for k in WEIGHTS}
    for i in reversed(range(depth)):
        j = i // 2
        sv = saved[i]
        lng = full["ln_g"][i]
        dz2, dz2b, dg2, db2 = _ln_bwd(dx, sv["z2"], lng[1:2], name="ln_bwd")
        da = _mm(dz2b, full["ffn_w_down"][i], trans_w=True, name="ffn_down_dx")
        grads["ffn_w_down"][i] = _mm_tn(sv["a"], dz2b, name="ffn_down_dw")
        dhf, dwg, dwv, dbg, dbv = _ffn_bwd(sv["hf"].reshape(bsz, s, -1), da.reshape(bsz, s, -1), full["ffn_conv_w"][i],
                                           full["ffn_conv_b"][i:i + 1], name="ffn_act_bwd")
        dhf = dhf.reshape(t, -1)
        grads["ffn_conv_w"][i] = jnp.concatenate([dwg, dwv], axis=1)
        grads["ffn_conv_b"][i] = jnp.concatenate([dbg, dbv], axis=1)[0]
        grads["ffn_w_up"][i] = _mm_tn(sv["x1"], dhf, name="ffn_up_dw")
        dx1 = _mm(dhf, full["ffn_w_up"][i], trans_w=True, resid=dz2, resid_scale=alpha, name="ffn_up_dx")
        dz1, dz1b, dg1, db1 = _ln_bwd(dx1, sv["z1"], lng[0:1], name="ln_bwd")
        grads["ln_g"][i] = jnp.concatenate([dg1, dg2], axis=0)
        grads["ln_b"][i] = jnp.concatenate([db1, db2], axis=0)
        if i % 2 == 0:
            dq = _mm(dz1b, full["sc_w_out"][j], trans_w=True, name="sc_out_dx")
            grads["sc_w_out"][j] = _mm_tn(sv["q"], dz1b, name="sc_out_dw")
            dhm, dcw, dcb = _sc_bwd(sv["hm"].reshape(bsz, s, -1), dq.reshape(bsz, s, -1), full["sc_conv_w"][j],
                                    full["sc_conv_b"][j:j + 1], name="sc_mix_bwd")
            dhm = dhm.reshape(t, -1)
            grads["sc_conv_w"][j] = dcw
            grads["sc_conv_b"][j] = dcb[0]
            grads["sc_w_in"][j] = _mm_tn(sv["x0"], dhm, name="sc_in_dw")
            dx = _mm(dhm, full["sc_w_in"][j], trans_w=True, resid=dz1, resid_scale=alpha, name="sc_in_dx")
        else:
            dq = _mm(dz1b, full["lru_w_out"][j], trans_w=True, name="lru_out_dx")
            grads["lru_w_out"][j] = _mm_tn(sv["q"], dz1b, name="lru_out_dw")
            dhm, dcw, dcb, dwgt, dbgt, dlam, sgb, srb = _lru_bwd(
                sv["hm"].reshape(bsz, s, -1), sv["hs"], dq.reshape(bsz, s, -1), full["lru_conv_w"][j],
                full["lru_conv_b"][j:j + 1], full["lru_w_gate"][j], full["lru_b_gate"][j].reshape(heads, 1, -1),
                full["lru_lambda"][j:j + 1], name="lru_mix_bwd")
            dhm = dhm.reshape(t, -1)
            grads["lru_conv_w"][j] = dcw
            grads["lru_conv_b"][j] = dcb[0]
            grads["lru_w_gate"][j] = dwgt
            grads["lru_b_gate"][j] = dbgt[:, 0, :]
            grads["lru_lambda"][j] = dlam[0]
            grads["lru_b_in"][j] = jnp.concatenate([sgb, srb], axis=1)[0]
            grads["lru_w_in"][j] = _mm_tn(sv["x0"], dhm, name="lru_in_dw")
            dx = _mm(dhm, full["lru_w_in"][j], trans_w=True, resid=dz1, resid_scale=alpha, name="lru_in_dx")
    grad_x = dx.reshape(bsz, s, d)
    grads = {k: jnp.stack(g) for k, g in grads.items()}

    send = [_slabs(grads[k], BIG[k]) for k in big_names]
    small_send = _pack([_slabs(grads[k], grads[k].ndim - 1).reshape(N_DEV, -1) for k in SMALL], small_rows)
    repl_sizes = [w[k].size for k in REPL]
    repl_rows = _pack_rows(repl_sizes)
    repl_send = _pack([grads[k].reshape(1, -1) for k in REPL], repl_rows)[0]
    got = _exchange(send + [small_send, repl_send], [False] * (len(big_names) + 1) + [True], name="scatter_grads")

    out = {}

    def update(key, parts, wk, mk, vk):
        g, dl, mn, vn = _adamw(parts, wk, mk, vk, name="adamw_" + key)
        return g, dl, mn, vn

    for k, parts in zip(big_names, got[:len(big_names)]):
        shp = w[k].shape
        c2 = shp[-1]
        res = update(k, parts.reshape(N_DEV, -1, c2), w[k].reshape(-1, c2), m[k].reshape(-1, c2), v[k].reshape(-1, c2))
        out[k] = [r.reshape(shp) for r in res]
    pk = lambda src, names, rows: _pack([src[k].reshape(1, -1) for k in names], rows)[0]
    res = update("small", got[-2], small_local, pk(m, SMALL, small_rows), pk(v, SMALL, small_rows))
    for r_i, r in enumerate(res):
        for k, seg in zip(SMALL, _unpack(r, small_sizes)):
            out.setdefault(k, [None] * 4)[r_i] = seg.reshape(w[k].shape)
    res = update("repl", got[-1], pk(w, REPL, repl_rows), pk(m, REPL, repl_rows), pk(v, REPL, repl_rows))
    for r_i, r in enumerate(res):
        for k, seg in zip(REPL, _unpack(r, repl_sizes)):
            out.setdefault(k, [None] * 4)[r_i] = seg.reshape(w[k].shape)

    return (loss, grad_x, *[out[k][0] for k in WEIGHTS], *[out[k][1] for k in WEIGHTS],
            *[out[k][2] for k in WEIGHTS], *[out[k][3] for k in WEIGHTS])


def kernel(x, sc_w_in, sc_conv_w, sc_conv_b, sc_w_out, lru_w_in, lru_b_in, lru_conv_w, lru_conv_b, lru_w_gate, lru_b_gate, lru_lambda, lru_w_out, ffn_w_up, ffn_conv_w, ffn_conv_b, ffn_w_down, ln_g, ln_b, loss_target, m_sc_w_in, m_sc_conv_w, m_sc_conv_b, m_sc_w_out, m_lru_w_in, m_lru_b_in, m_lru_conv_w, m_lru_conv_b, m_lru_w_gate, m_lru_b_gate, m_lru_lambda, m_lru_w_out, m_ffn_w_up, m_ffn_conv_w, m_ffn_conv_b, m_ffn_w_down, m_ln_g, m_ln_b, v_sc_w_in, v_sc_conv_w, v_sc_conv_b, v_sc_w_out, v_lru_w_in, v_lru_b_in, v_lru_conv_w, v_lru_conv_b, v_lru_w_gate, v_lru_b_gate, v_lru_lambda, v_lru_w_out, v_ffn_w_up, v_ffn_conv_w, v_ffn_conv_b, v_ffn_w_down, v_ln_g, v_ln_b):
    w = dict(sc_w_in=sc_w_in, sc_conv_w=sc_conv_w, sc_conv_b=sc_conv_b, sc_w_out=sc_w_out, lru_w_in=lru_w_in,
             lru_b_in=lru_b_in, lru_conv_w=lru_conv_w, lru_conv_b=lru_conv_b, lru_w_gate=lru_w_gate,
             lru_b_gate=lru_b_gate, lru_lambda=lru_lambda, lru_w_out=lru_w_out, ffn_w_up=ffn_w_up,
             ffn_conv_w=ffn_conv_w, ffn_conv_b=ffn_conv_b, ffn_w_down=ffn_w_down, ln_g=ln_g, ln_b=ln_b)
    m = dict(sc_w_in=m_sc_w_in, sc_conv_w=m_sc_conv_w, sc_conv_b=m_sc_conv_b, sc_w_out=m_sc_w_out, lru_w_in=m_lru_w_in,
             lru_b_in=m_lru_b_in, lru_conv_w=m_lru_conv_w, lru_conv_b=m_lru_conv_b, lru_w_gate=m_lru_w_gate,
             lru_b_gate=m_lru_b_gate, lru_lambda=m_lru_lambda, lru_w_out=m_lru_w_out, ffn_w_up=m_ffn_w_up,
             ffn_conv_w=m_ffn_conv_w, ffn_conv_b=m_ffn_conv_b, ffn_w_down=m_ffn_w_down, ln_g=m_ln_g, ln_b=m_ln_b)
    v = dict(sc_w_in=v_sc_w_in, sc_conv_w=v_sc_conv_w, sc_conv_b=v_sc_conv_b, sc_w_out=v_sc_w_out, lru_w_in=v_lru_w_in,
             lru_b_in=v_lru_b_in, lru_conv_w=v_lru_conv_w, lru_conv_b=v_lru_conv_b, lru_w_gate=v_lru_w_gate,
             lru_b_gate=v_lru_b_gate, lru_lambda=v_lru_lambda, lru_w_out=v_lru_w_out, ffn_w_up=v_ffn_w_up,
             ffn_conv_w=v_ffn_conv_w, ffn_conv_b=v_ffn_conv_b, ffn_w_down=v_ffn_w_down, ln_g=v_ln_g, ln_b=v_ln_b)
    return _step(x, loss_target, w, m, v)
```

```python
import functools
import math

import jax
import jax.numpy as jnp
from jax import lax
from jax.experimental import pallas as pl
from jax.experimental.pallas import tpu as pltpu

F32 = jnp.float32
BF16 = jnp.bfloat16

N_DEV = 8
MESH_AXES = ("x", "y", "c")
LANES = 128
SUBLANES = 8
VMEM_LIMIT = 56 * 1024 * 1024
MM_LHS_ELEMS = 6 * 1024 * 1024

LRU_C = 8.0
LN_EPS = 1e-5
ADAM_LR = 0.001
ADAM_B1 = 0.9
ADAM_B2 = 0.999
ADAM_EPS = 1e-08
ADAM_WD = 0.01
ADAM_STEP = 10
GELU_K = math.sqrt(2.0 / math.pi)
GELU_C = 0.044715


def _tile(n, target, align):
    if n <= target:
        return n
    t = (target // align) * align
    while t >= align:
        if n % t == 0:
            return t
        t -= align
    return n


def _params(sem):
    return pltpu.CompilerParams(dimension_semantics=sem, vmem_limit_bytes=VMEM_LIMIT)


def _rows(x):
    return lax.broadcasted_iota(jnp.int32, x.shape, 0)


def _shift_dn(x, k, fill=0.0):
    if k == 0:
        return x
    return jnp.where(_rows(x) >= k, pltpu.roll(x, k, 0), fill)


def _shift_up(x, k, fill=0.0):
    if k == 0:
        return x
    s = x.shape[0]
    return jnp.where(_rows(x) < s - k, pltpu.roll(x, s - k, 0), fill)


def _conv_fwd(x, w, b):
    kw = w.shape[0]
    y = _shift_dn(x, kw - 1) * w[0:1, :] + b
    for k in range(1, kw):
        y = y + _shift_dn(x, kw - 1 - k) * w[k:k + 1, :]
    return y


def _conv_bwd_x(dy, w):
    kw = w.shape[0]
    dx = _shift_up(dy, kw - 1) * w[0:1, :]
    for k in range(1, kw):
        dx = dx + _shift_up(dy, kw - 1 - k) * w[k:k + 1, :]
    return dx


def _conv_bwd_w(dy, x, kw):
    return [jnp.sum(dy * _shift_dn(x, kw - 1 - k), axis=0, keepdims=True) for k in range(kw)]


def _accumulate(first, items):
    flat = []
    for ref, val in items:
        if isinstance(val, list):
            flat += [(ref, (slice(k, k + 1), slice(None)), row) for k, row in enumerate(val)]
        else:
            flat.append((ref, Ellipsis, val))

    @pl.when(first)
    def _():
        for ref, idx, val in flat:
            ref[idx] = val

    @pl.when(jnp.logical_not(first))
    def _():
        for ref, idx, val in flat:
            ref[idx] += val


def _colsum(x):
    return jnp.sum(x, axis=0, keepdims=True)


def _sigmoid(x):
    return 1.0 / (1.0 + jnp.exp(-x))


def _log1p(x):
    u = 1.0 + x
    return jnp.where(u == 1.0, x, jnp.log(u) * (x / (u - 1.0)))


def _softplus(x):
    return jnp.maximum(x, 0.0) + _log1p(jnp.exp(-jnp.abs(x)))


def _expm1(x, ex):
    poly = x * (1.0 + x * (0.5 + x * (1.0 / 6.0 + x * (1.0 / 24.0 + x * (1.0 / 120.0 + x * (1.0 / 720.0))))))
    return jnp.where(jnp.abs(x) < 0.25, poly, ex - 1.0)


def _gelu(x):
    t = jnp.tanh(GELU_K * (x + GELU_C * x * x * x))
    return 0.5 * x * (1.0 + t)


def _gelu_and_grad(x):
    x2 = x * x
    t = jnp.tanh(GELU_K * (x + GELU_C * x * x2))
    g = 0.5 * x * (1.0 + t)
    dg = 0.5 * (1.0 + t) + 0.5 * x * (1.0 - t * t) * (GELU_K * (1.0 + 3.0 * GELU_C * x2))
    return g, dg


def _scan_fwd(a, b):
    s = a.shape[0]
    k = 1
    while k < s:
        b = a * _shift_dn(b, k) + b
        if 2 * k < s:
            a = a * _shift_dn(a, k, 1.0)
        k *= 2
    return b


def _scan_rev(c, v):
    s = c.shape[0]
    k = 1
    while k < s:
        v = c * _shift_up(v, k) + v
        if 2 * k < s:
            c = c * _shift_up(c, k, 1.0)
        k *= 2
    return v


def _mm(a, w, *, name, trans_w=False, bias=None, resid=None, resid_scale=1.0, tn=512):
    m, k = a.shape
    n = w.shape[0] if trans_w else w.shape[1]
    tm = _tile(m, min(1024, max(256, MM_LHS_ELEMS // k)), SUBLANES)
    tn = _tile(n, tn, LANES)
    has_bias = bias is not None
    has_resid = resid is not None

    def body(*refs):
        a_ref, w_ref = refs[0], refs[1]
        pos = 2
        b_ref = r_ref = None
        if has_bias:
            b_ref = refs[pos]
            pos += 1
        if has_resid:
            r_ref = refs[pos]
            pos += 1
        o_ref = refs[pos]

        if trans_w:
            acc = lax.dot_general(a_ref[...], w_ref[...], (((1,), (1,)), ((), ())), preferred_element_type=F32)
        else:
            acc = jnp.dot(a_ref[...], w_ref[...], preferred_element_type=F32)
        if has_bias:
            acc = acc + b_ref[...]
        if has_resid:
            acc = acc + resid_scale * r_ref[...]
        o_ref[...] = acc

    in_specs = [pl.BlockSpec((tm, k), lambda i, j: (i, 0))]
    if trans_w:
        in_specs.append(pl.BlockSpec((tn, k), lambda i, j: (j, 0)))
    else:
        in_specs.append(pl.BlockSpec((k, tn), lambda i, j: (0, j)))
    args = [a, w]
    if has_bias:
        in_specs.append(pl.BlockSpec((1, tn), lambda i, j: (0, j)))
        args.append(bias)
    if has_resid:
        in_specs.append(pl.BlockSpec((tm, tn), lambda i, j: (i, j)))
        args.append(resid)
    return pl.pallas_call(
        body, name=name, grid=(m // tm, n // tn), in_specs=in_specs,
        out_specs=pl.BlockSpec((tm, tn), lambda i, j: (i, j)),
        out_shape=jax.ShapeDtypeStruct((m, n), F32),
        compiler_params=_params(("parallel", "arbitrary")),
    )(*args)


def _mm_ln(a, w, resid, alpha, g, b, *, name, tm=512):
    m, k = a.shape
    d = w.shape[1]
    tm = _tile(m, tm, SUBLANES)

    def body(a_ref, w_ref, r_ref, g_ref, b_ref, z_ref, o_ref, obf_ref):
        y = jnp.dot(a_ref[...], w_ref[...], preferred_element_type=F32)
        z = alpha * r_ref[...] + y
        z_ref[...] = z
        mu = jnp.mean(z, axis=-1, keepdims=True)
        zc = z - mu
        var = jnp.mean(zc * zc, axis=-1, keepdims=True)
        o = zc * lax.rsqrt(var + LN_EPS) * g_ref[...] + b_ref[...]
        o_ref[...] = o
        obf_ref[...] = o.astype(BF16)

    row = pl.BlockSpec((tm, d), lambda i: (i, 0))
    vec = pl.BlockSpec((1, d), lambda i: (0, 0))
    return pl.pallas_call(
        body, name=name, grid=(m // tm,),
        in_specs=[pl.BlockSpec((tm, k), lambda i: (i, 0)), pl.BlockSpec((k, d), lambda i: (0, 0)), row, vec, vec],
        out_specs=[row, row, row],
        out_shape=[jax.ShapeDtypeStruct((m, d), F32), jax.ShapeDtypeStruct((m, d), F32),
                   jax.ShapeDtypeStruct((m, d), BF16)],
        compiler_params=_params(("parallel",)),
    )(a, w, resid, g, b)


def _mm_tn(a, b, *, name, tm=1408, tn=1536, tk=1024):
    t, m = a.shape
    n = b.shape[1]
    tm = _tile(m, tm, LANES)
    tn = _tile(n, tn, LANES)
    tk = _tile(t, tk, SUBLANES)

    def body(a_ref, b_ref, o_ref):
        @pl.when(pl.program_id(2) == 0)
        def _():
            o_ref[...] = jnp.zeros_like(o_ref)

        o_ref[...] += lax.dot_general(a_ref[...], b_ref[...], (((0,), (0,)), ((), ())), preferred_element_type=F32)

    return pl.pallas_call(
        body, name=name, grid=(m // tm, n // tn, t // tk),
        in_specs=[pl.BlockSpec((tk, tm), lambda i, j, l: (l, i)), pl.BlockSpec((tk, tn), lambda i, j, l: (l, j))],
        out_specs=pl.BlockSpec((tm, tn), lambda i, j, l: (i, j)),
        out_shape=jax.ShapeDtypeStruct((m, n), F32),
        compiler_params=_params(("parallel", "parallel", "arbitrary")),
    )(a, b)


def _ln_bwd(dout, z, g, *, name, tm=512):
    t, d = z.shape
    tm = _tile(t, tm, SUBLANES)

    def body(do_ref, z_ref, g_ref, dz_ref, dzbf_ref, dg_ref, db_ref):
        @pl.when(pl.program_id(0) == 0)
        def _():
            dg_ref[...] = jnp.zeros_like(dg_ref)
            db_ref[...] = jnp.zeros_like(db_ref)

        z = z_ref[...]
        do = do_ref[...]
        mu = jnp.mean(z, axis=-1, keepdims=True)
        zc = z - mu
        var = jnp.mean(zc * zc, axis=-1, keepdims=True)
        rstd = lax.rsqrt(var + LN_EPS)
        xhat = zc * rstd
        dxh = do * g_ref[...]
        m1 = jnp.mean(dxh, axis=-1, keepdims=True)
        m2 = jnp.mean(dxh * xhat, axis=-1, keepdims=True)
        dz = rstd * (dxh - m1 - xhat * m2)
        dz_ref[...] = dz
        dzbf_ref[...] = dz.astype(BF16)
        dg_ref[...] += _colsum(do * xhat)
        db_ref[...] += _colsum(do)

    row = pl.BlockSpec((tm, d), lambda i: (i, 0))
    vec = pl.BlockSpec((1, d), lambda i: (0, 0))
    return pl.pallas_call(
        body, name=name, grid=(t // tm,), in_specs=[row, row, vec], out_specs=[row, row, vec, vec],
        out_shape=[jax.ShapeDtypeStruct((t, d), F32), jax.ShapeDtypeStruct((t, d), BF16),
                   jax.ShapeDtypeStruct((1, d), F32), jax.ShapeDtypeStruct((1, d), F32)],
        compiler_params=_params(("arbitrary",)),
    )(dout, z, g)


def _loss_head(y, target, *, name, tm=512):
    t, d = y.shape
    tm = _tile(t, tm, SUBLANES)

    def body(y_ref, t_ref, s_ref, dy_ref):
        @pl.when(pl.program_id(0) == 0)
        def _():
            s_ref[...] = jnp.zeros_like(s_ref)

        e = y_ref[...] - t_ref[...]
        dy_ref[...] = e * (1.0 / d)
        s_ref[...] += jnp.sum(_colsum(e * e), axis=-1, keepdims=True)

    row = pl.BlockSpec((tm, d), lambda i: (i, 0))
    return pl.pallas_call(
        body, name=name, grid=(t // tm,), in_specs=[row, row],
        out_specs=[pl.BlockSpec((1, LANES), lambda i: (0, 0)), row],
        out_shape=[jax.ShapeDtypeStruct((1, LANES), F32), jax.ShapeDtypeStruct((t, d), F32)],
        compiler_params=_params(("arbitrary",)),
    )(y, target)


def _strip(s, tc, off):
    return pl.BlockSpec((None, s, tc), lambda c, b, *_: (b, 0, off + c))


def _cvec(kw, tc, off):
    return pl.BlockSpec((kw, tc), lambda c, b, *_: (0, off + c))


def _acc(kw, tc):
    return pl.BlockSpec((kw, tc), lambda c, b, *_: (0, c))


def _sc_fwd(h, cw, cb, *, name, tc=256):
    bsz, s, d3 = h.shape
    d = d3 // 3
    tc = _tile(d, tc, LANES)
    nc = d // tc

    def body(gb_ref, gc_ref, v_ref, w_ref, b_ref, q_ref):
        u = _conv_fwd(gc_ref[...] * v_ref[...], w_ref[...], b_ref[...])
        q_ref[...] = (gb_ref[...] * u).astype(BF16)

    return pl.pallas_call(
        body, name=name, grid=(nc, bsz),
        in_specs=[_strip(s, tc, 0), _strip(s, tc, nc), _strip(s, tc, 2 * nc), _cvec(cw.shape[0], tc, 0), _cvec(1, tc, 0)],
        out_specs=_strip(s, tc, 0),
        out_shape=jax.ShapeDtypeStruct((bsz, s, d), BF16),
        compiler_params=_params(("parallel", "parallel")),
    )(h, h, h, cw, cb)


def _sc_bwd(h, dq, cw, cb, *, name, tc=256):
    bsz, s, d3 = h.shape
    d = d3 // 3
    kw = cw.shape[0]
    tc = _tile(d, tc, LANES)
    nc = d // tc

    def body(gb_ref, gc_ref, v_ref, dq_ref, w_ref, b_ref, dh_ref, dw_ref, db_ref, parts):
        b_id, part = pl.program_id(1), pl.program_id(2)

        @pl.when(part == 0)
        def _():
            gb, gc, v, dq_, w = gb_ref[...], gc_ref[...], v_ref[...], dq_ref[...], w_ref[...]
            p = gc * v
            u = _conv_fwd(p, w, b_ref[...])
            du = dq_ * gb
            dp = _conv_bwd_x(du, w)
            parts[0] = (dq_ * u).astype(BF16)
            parts[1] = (dp * v).astype(BF16)
            parts[2] = (dp * gc).astype(BF16)
            _accumulate(b_id == 0, [(dw_ref, _conv_bwd_w(du, p, kw)), (db_ref, _colsum(du))])

        dh_ref[...] = parts[part]

    return pl.pallas_call(
        body, name=name, grid=(nc, bsz, 3),
        in_specs=[_strip(s, tc, 0), _strip(s, tc, nc), _strip(s, tc, 2 * nc), _strip(s, tc, 0),
                  _cvec(kw, tc, 0), _cvec(1, tc, 0)],
        out_specs=[pl.BlockSpec((None, s, tc), lambda c, b, p: (b, 0, p * nc + c)), _acc(kw, tc), _acc(1, tc)],
        out_shape=[jax.ShapeDtypeStruct((bsz, s, d3), BF16), jax.ShapeDtypeStruct((kw, d), F32),
                   jax.ShapeDtypeStruct((1, d), F32)],
        scratch_shapes=[pltpu.VMEM((3, s, tc), BF16)],
        compiler_params=_params(("parallel", "arbitrary", "arbitrary")),
    )(h, h, h, dq, cw, cb)


def _ffn_fwd(h, cw, cb, *, name, tc=256):
    bsz, s, f2 = h.shape
    f = f2 // 2
    tc = _tile(f, tc, LANES)
    nc = f // tc

    def body(hg_ref, hv_ref, wg_ref, wv_ref, bg_ref, bv_ref, a_ref):
        g = _conv_fwd(hg_ref[...], wg_ref[...], bg_ref[...])
        v = _conv_fwd(hv_ref[...], wv_ref[...], bv_ref[...])
        a_ref[...] = (g * _sigmoid(g) * v).astype(BF16)

    kw = cw.shape[0]
    return pl.pallas_call(
        body, name=name, grid=(nc, bsz),
        in_specs=[_strip(s, tc, 0), _strip(s, tc, nc), _cvec(kw, tc, 0), _cvec(kw, tc, nc), _cvec(1, tc, 0), _cvec(1, tc, nc)],
        out_specs=_strip(s, tc, 0),
        out_shape=jax.ShapeDtypeStruct((bsz, s, f), BF16),
        compiler_params=_params(("parallel", "parallel")),
    )(h, h, cw, cw, cb, cb)


def _ffn_bwd(h, da, cw, cb, *, name, tc=256):
    bsz, s, f2 = h.shape
    f = f2 // 2
    kw = cw.shape[0]
    tc = _tile(f, tc, LANES)
    nc = f // tc

    def body(hg_ref, hv_ref, da_ref, wg_ref, wv_ref, bg_ref, bv_ref,
             dh_ref, dwg_ref, dwv_ref, dbg_ref, dbv_ref, parts):
        b_id, part = pl.program_id(1), pl.program_id(2)

        @pl.when(part == 0)
        def _():
            hg, hv, da_ = hg_ref[...], hv_ref[...], da_ref[...]
            wg, wv = wg_ref[...], wv_ref[...]
            g = _conv_fwd(hg, wg, bg_ref[...])
            v = _conv_fwd(hv, wv, bv_ref[...])
            sg = _sigmoid(g)
            dv = da_ * (g * sg)
            dg = da_ * v * (sg * (1.0 + g * (1.0 - sg)))
            parts[0] = _conv_bwd_x(dg, wg).astype(BF16)
            parts[1] = _conv_bwd_x(dv, wv).astype(BF16)
            _accumulate(b_id == 0, [(dwg_ref, _conv_bwd_w(dg, hg, kw)), (dwv_ref, _conv_bwd_w(dv, hv, kw)),
                                    (dbg_ref, _colsum(dg)), (dbv_ref, _colsum(dv))])

        dh_ref[...] = parts[part]

    return pl.pallas_call(
        body, name=name, grid=(nc, bsz, 2),
        in_specs=[_strip(s, tc, 0), _strip(s, tc, nc), _strip(s, tc, 0), _cvec(kw, tc, 0), _cvec(kw, tc, nc),
                  _cvec(1, tc, 0), _cvec(1, tc, nc)],
        out_specs=[pl.BlockSpec((None, s, tc), lambda c, b, p: (b, 0, p * nc + c)),
                   _acc(kw, tc), _acc(kw, tc), _acc(1, tc), _acc(1, tc)],
        out_shape=[jax.ShapeDtypeStruct((bsz, s, f2), BF16), jax.ShapeDtypeStruct((kw, f), F32),
                   jax.ShapeDtypeStruct((kw, f), F32), jax.ShapeDtypeStruct((1, f), F32), jax.ShapeDtypeStruct((1, f), F32)],
        scratch_shapes=[pltpu.VMEM((2, s, tc), BF16)],
        compiler_params=_params(("parallel", "arbitrary", "arbitrary")),
    )(h, h, da, cw, cw, cb, cb)


def _lru_gates(r, cw, cb, wg, bg, lam):
    blk = r.shape[1]
    xr = _conv_fwd(r, cw, cb)
    gates = jnp.dot(xr.astype(BF16), wg, preferred_element_type=F32) + bg
    rg = _sigmoid(gates[:, :blk])
    ig = _sigmoid(gates[:, blk:])
    sp = _softplus(-lam)
    la = (-LRU_C * sp) * rg
    a = jnp.exp(la)
    mult = jnp.sqrt(-_expm1(2.0 * la, a * a))
    return xr, rg, ig, sp, a, mult


def _lru_specs(s, blk, heads, kw):
    return [pl.BlockSpec((None, s, blk), lambda h, b, *_: (b, 0, h)),
            pl.BlockSpec((None, s, blk), lambda h, b, *_: (b, 0, heads + h)),
            pl.BlockSpec((kw, blk), lambda h, b, *_: (0, h)),
            pl.BlockSpec((1, blk), lambda h, b, *_: (0, h)),
            pl.BlockSpec((None, blk, 2 * blk), lambda h, b, *_: (h, 0, 0)),
            pl.BlockSpec((None, 1, 2 * blk), lambda h, b, *_: (h, 0, 0)),
            pl.BlockSpec((1, blk), lambda h, b, *_: (0, h))]


def _lru_fwd(h, cw, cb, wg, bg, lam, *, name):
    bsz, s, r2 = h.shape
    heads, blk = wg.shape[0], wg.shape[1]
    kw = cw.shape[0]

    def body(g_ref, r_ref, cw_ref, cb_ref, wg_ref, bg_ref, lam_ref, y_ref, hs_ref):
        xr, _, ig, _, a, mult = _lru_gates(r_ref[...], cw_ref[...], cb_ref[...], wg_ref[...], bg_ref[...], lam_ref[...])
        hs = _scan_fwd(a, mult * (ig * xr))
        hs_ref[...] = hs
        y_ref[...] = (hs * _gelu(g_ref[...])).astype(BF16)

    out = pl.BlockSpec((None, s, blk), lambda hd, b: (b, 0, hd))
    return pl.pallas_call(
        body, name=name, grid=(heads, bsz), in_specs=_lru_specs(s, blk, heads, kw), out_specs=[out, out],
        out_shape=[jax.ShapeDtypeStruct((bsz, s, r2 // 2), BF16), jax.ShapeDtypeStruct((bsz, s, r2 // 2), F32)],
        compiler_params=_params(("parallel", "parallel")),
    )(h, h, cw, cb, wg, bg, lam)


def _lru_bwd(h, hs, dy, cw, cb, wg, bg, lam, *, name):
    bsz, s, r2 = h.shape
    rw = r2 // 2
    heads, blk = wg.shape[0], wg.shape[1]
    kw = cw.shape[0]

    def body(g_ref, r_ref, cw_ref, cb_ref, wg_ref, bg_ref, lam_ref, hs_ref, dy_ref,
             dh_ref, dcw_ref, dcb_ref, dwg_ref, dbg_ref, dlam_ref, sg_ref, sr_ref, parts):
        b_id, part = pl.program_id(1), pl.program_id(2)

        @pl.when(part == 0)
        def _():
            r, cw_, wg_, lam_ = r_ref[...], cw_ref[...], wg_ref[...], lam_ref[...]
            xr, rg, ig, sp, a, mult = _lru_gates(r, cw_, cb_ref[...], wg_, bg_ref[...], lam_)
            hs_, dy_ = hs_ref[...], dy_ref[...]
            gel, dgel = _gelu_and_grad(g_ref[...])
            dg = dy_ * hs_ * dgel
            lmb = _scan_rev(_shift_up(a, 1, 1.0), dy_ * gel)
            da = lmb * _shift_dn(hs_, 1)
            dmult = lmb * (ig * xr)
            dig = lmb * (mult * xr)
            dxr = lmb * (mult * ig)
            dla = da * a - dmult * (a * a / mult)
            drg = dla * (-LRU_C * sp)
            dsp = _colsum(dla * rg) * (-LRU_C)
            dlam = -dsp * _sigmoid(-lam_)
            dgates = jnp.concatenate([drg * (rg * (1.0 - rg)), dig * (ig * (1.0 - ig))], axis=1)
            dgates_bf = dgates.astype(BF16)
            dwg = lax.dot_general(xr.astype(BF16), dgates_bf, (((0,), (0,)), ((), ())), preferred_element_type=F32)
            dxr = dxr + lax.dot_general(dgates_bf, wg_, (((1,), (1,)), ((), ())), preferred_element_type=F32)
            dr = _conv_bwd_x(dxr, cw_)
            parts[0] = dg.astype(BF16)
            parts[1] = dr.astype(BF16)
            _accumulate(b_id == 0, [(dcw_ref, _conv_bwd_w(dxr, r, kw)), (dcb_ref, _colsum(dxr)), (dwg_ref, dwg),
                                    (dbg_ref, _colsum(dgates)), (dlam_ref, dlam), (sg_ref, _colsum(dg)),
                                    (sr_ref, _colsum(dr))])

        dh_ref[...] = parts[part]

    strip = pl.BlockSpec((None, s, blk), lambda hd, b, p: (b, 0, hd))
    vec = pl.BlockSpec((1, blk), lambda hd, b, p: (0, hd))
    return pl.pallas_call(
        body, name=name, grid=(heads, bsz, 2),
        in_specs=_lru_specs(s, blk, heads, kw) + [strip, strip],
        out_specs=[pl.BlockSpec((None, s, blk), lambda hd, b, p: (b, 0, p * heads + hd)),
                   pl.BlockSpec((kw, blk), lambda hd, b, p: (0, hd)), vec,
                   pl.BlockSpec((None, blk, 2 * blk), lambda hd, b, p: (hd, 0, 0)),
                   pl.BlockSpec((None, 1, 2 * blk), lambda hd, b, p: (hd, 0, 0)), vec, vec, vec],
        out_shape=[jax.ShapeDtypeStruct((bsz, s, r2), BF16), jax.ShapeDtypeStruct((kw, rw), F32),
                   jax.ShapeDtypeStruct((1, rw), F32), jax.ShapeDtypeStruct((heads, blk, 2 * blk), F32),
                   jax.ShapeDtypeStruct((heads, 1, 2 * blk), F32), jax.ShapeDtypeStruct((1, rw), F32),
                   jax.ShapeDtypeStruct((1, rw), F32), jax.ShapeDtypeStruct((1, rw), F32)],
        scratch_shapes=[pltpu.VMEM((2, s, blk), BF16)],
        compiler_params=_params(("parallel", "arbitrary", "arbitrary")),
    )(h, h, cw, cb, wg, bg, lam, hs, dy)


HBM_SPEC = pl.BlockSpec(memory_space=pltpu.HBM)
SEM_SPEC = pl.BlockSpec(memory_space=pltpu.SEMAPHORE)
EFFECT = pltpu.SideEffectType.DATAFLOW_SIDE_EFFECTING


def _peer_copies(srcs, lands, gather, send_sem, recv_sem):
    x, y, c = (lax.axis_index(ax) for ax in MESH_AXES)
    me = 4 * x + 2 * y + c
    copies = []
    for i in range(len(srcs)):
        for d in range(1, N_DEV):
            px = 1 - x if d & 4 else x
            py = 1 - y if d & 2 else y
            pc = 1 - c if d & 1 else c
            src = srcs[i] if gather[i] else srcs[i].at[4 * px + 2 * py + pc]
            k = i * (N_DEV - 1) + d - 1
            copies.append(pltpu.make_async_remote_copy(
                src_ref=src, dst_ref=lands[i].at[me], send_sem=send_sem.at[k], recv_sem=recv_sem.at[k],
                device_id=(px, py, pc), device_id_type=pl.DeviceIdType.MESH))
    return copies


def _exchange_start(arrs, gather, *, name):
    n = len(arrs)
    lands = [lax.empty((N_DEV,) + tuple(a.shape if g else a.shape[1:]), a.dtype) for a, g in zip(arrs, gather)]

    def body(*refs):
        srcs, land_refs = refs[:n], refs[n:2 * n]
        send_sem, recv_sem = refs[2 * n], refs[2 * n + 1]
        token = refs[-1]
        for cp in _peer_copies(srcs, land_refs, gather, send_sem, recv_sem):
            cp.start()
        token[...] = jnp.zeros_like(token)

    sems = pltpu.SemaphoreType.DMA((n * (N_DEV - 1),))
    thru = [pltpu.HBM(a.shape, a.dtype) for a in arrs + lands]
    out = pl.pallas_call(
        body, name=name, in_specs=[HBM_SPEC] * (2 * n),
        out_shape=(sems, sems, *thru, jax.ShapeDtypeStruct((SUBLANES, LANES), F32)),
        out_specs=(SEM_SPEC, SEM_SPEC, *([HBM_SPEC] * (2 * n)), pl.BlockSpec(memory_space=pltpu.VMEM)),
        input_output_aliases={i: 2 + i for i in range(2 * n)},
        compiler_params=pltpu.CompilerParams(has_side_effects=EFFECT),
    )(*[pltpu.with_memory_space_constraint(a, pltpu.HBM) for a in arrs + lands])
    return {"send_sem": out[0], "recv_sem": out[1], "srcs": list(out[2:2 + n]), "lands": list(out[2 + n:2 + 2 * n]),
            "token": out[-1], "gather": list(gather)}


def _exchange_wait(handle, after, *, name):
    srcs, lands, gather = handle["srcs"], handle["lands"], handle["gather"]
    n = len(srcs)

    def body(*refs):
        src_refs, land_refs = refs[:n], refs[n:2 * n]
        send_sem, recv_sem = refs[2 * n], refs[2 * n + 1]
        for cp in _peer_copies(src_refs, land_refs, gather, send_sem, recv_sem):
            cp.wait_send()
            cp.wait_recv()

    out = pl.pallas_call(
        body, name=name,
        in_specs=[HBM_SPEC] * (2 * n) + [SEM_SPEC, SEM_SPEC, pl.BlockSpec(memory_space=pl.ANY)],
        out_shape=tuple(pltpu.HBM(a.shape, a.dtype) for a in srcs + lands), out_specs=tuple([HBM_SPEC] * (2 * n)),
        input_output_aliases={i: i for i in range(2 * n)},
        compiler_params=pltpu.CompilerParams(has_side_effects=EFFECT),
    )(*srcs, *lands, handle["send_sem"], handle["recv_sem"], after)
    return list(out[:n]), list(out[n:])


def _adamw(parts, w, m, v, *, name, tr=256):
    r, c = w.shape
    tr = _tile(r, tr, SUBLANES)
    bc1 = 1.0 / (1.0 - ADAM_B1 ** ADAM_STEP)
    bc2 = 1.0 / (1.0 - ADAM_B2 ** ADAM_STEP)

    def body(p_ref, w_ref, m_ref, v_ref, g_ref, d_ref, mo_ref, vo_ref):
        g = p_ref[0]
        for s in range(1, N_DEV):
            g = g + p_ref[s]
        m_new = ADAM_B1 * m_ref[...] + (1.0 - ADAM_B1) * g
        v_new = ADAM_B2 * v_ref[...] + (1.0 - ADAM_B2) * (g * g)
        g_ref[...] = g
        mo_ref[...] = m_new
        vo_ref[...] = v_new
        d_ref[...] = -ADAM_LR * ((m_new * bc1) / (jnp.sqrt(v_new * bc2) + ADAM_EPS) + ADAM_WD * w_ref[...])

    blk = pl.BlockSpec((tr, c), lambda i: (i, 0))
    return pl.pallas_call(
        body, name=name, grid=(r // tr,),
        in_specs=[pl.BlockSpec((N_DEV, tr, c), lambda i: (0, i, 0)), blk, blk, blk],
        out_specs=[blk] * 4, out_shape=[jax.ShapeDtypeStruct((r, c), F32)] * 4,
        compiler_params=_params(("parallel",)),
    )(parts, w, m, v)


def _whole(slabs, axis):
    x = jnp.moveaxis(slabs, 0, axis)
    shp = x.shape
    return x.reshape(shp[:axis] + (shp[axis] * shp[axis + 1],) + shp[axis + 2:])


def _slabs(whole, axis):
    shp = whole.shape
    x = whole.reshape(shp[:axis] + (N_DEV, shp[axis] // N_DEV) + shp[axis + 1:])
    return jnp.moveaxis(x, axis, 0)


def _pack(vecs, rows):
    flat = jnp.concatenate(vecs, axis=-1)
    pad = rows * LANES - flat.shape[-1]
    flat = jnp.pad(flat, [(0, 0)] * (flat.ndim - 1) + [(0, pad)])
    return flat.reshape(flat.shape[:-1] + (rows, LANES))


def _unpack(packed, sizes):
    flat = packed.reshape(packed.shape[:-2] + (-1,))
    out, pos = [], 0
    for n in sizes:
        out.append(flat[..., pos:pos + n])
        pos += n
    return out


def _pack_rows(sizes):
    total = sum(sizes)
    return -(-total // (LANES * SUBLANES)) * SUBLANES


BIG = {"sc_w_in": 2, "sc_w_out": 1, "lru_w_in": 2, "lru_w_gate": 3, "lru_w_out": 1, "ffn_w_up": 2, "ffn_w_down": 1}
SMALL = ["sc_conv_w", "lru_b_in", "lru_conv_w", "lru_conv_b", "lru_b_gate", "lru_lambda", "ffn_conv_w", "ln_g", "ln_b"]
REPL = ["sc_conv_b", "ffn_conv_b"]
WEIGHTS = ["sc_w_in", "sc_conv_w", "sc_conv_b", "sc_w_out", "lru_w_in", "lru_b_in", "lru_conv_w", "lru_conv_b",
           "lru_w_gate", "lru_b_gate", "lru_lambda", "lru_w_out", "ffn_w_up", "ffn_conv_w", "ffn_conv_b", "ffn_w_down",
           "ln_g", "ln_b"]


def _layer_big(i):
    j = i // 2
    mixer = [("sc_w_in", j), ("sc_w_out", j)] if i % 2 == 0 else [("lru_w_in", j), ("lru_w_gate", j), ("lru_w_out", j)]
    return mixer + [("ffn_w_up", i), ("ffn_w_down", i)]


def _step(x, loss_target, w, m, v):
    bsz, s, d = x.shape
    t = bsz * s
    depth = w["ffn_w_up"].shape[0]
    alpha = (2.0 * depth) ** 0.25
    heads = w["lru_w_gate"].shape[1]

    small_sizes = [w[k].size for k in SMALL]
    small_rows = _pack_rows(small_sizes)
    small_local = _pack([w[k].reshape(1, -1) for k in SMALL], small_rows)[0]
    me = 4 * lax.axis_index("x") + 2 * lax.axis_index("y") + lax.axis_index("c")

    def with_own(land, own):
        return lax.dynamic_update_slice_in_dim(land, own, me, axis=0)

    gathers, tok = [], None
    for i in range(depth):
        arrs = [w[k][l].astype(BF16) for k, l in _layer_big(i)]
        if i == 0:
            arrs.append(small_local)
        if tok is not None:
            arrs[0] = arrs[0] + tok.astype(BF16)
        gathers.append(_exchange_start(arrs, [True] * len(arrs), name=f"gather_start_{i}"))
        tok = gathers[-1]["token"][0, 0]
    full = {k: [None] * w[k].shape[0] for k in BIG}
    full["sc_conv_b"] = w["sc_conv_b"]
    full["ffn_conv_b"] = w["ffn_conv_b"]

    xt = x.reshape(t, d)
    xb = xt.astype(BF16)
    saved = []
    for i in range(depth):
        j = i // 2
        srcs, lands = _exchange_wait(gathers[i], gathers[-1]["token"] if i == 0 else xb, name=f"gather_wait_{i}")
        for (k, l), src, land in zip(_layer_big(i), srcs, lands):
            full[k][l] = _whole(with_own(land, src[None]), BIG[k] - 1)
        if i == 0:
            for k, seg in zip(SMALL, _unpack(with_own(lands[-1], srcs[-1][None]), small_sizes)):
                full[k] = _whole(seg.reshape((N_DEV,) + w[k].shape), w[k].ndim - 1)
        lng, lnb = full["ln_g"][i], full["ln_b"][i]
        sv = {"x0": xb}
        if i % 2 == 0:
            hm = _mm(xb, full["sc_w_in"][j], name="sc_in")
            q = _sc_fwd(hm.reshape(bsz, s, -1), full["sc_conv_w"][j], full["sc_conv_b"][j:j + 1], name="sc_mix")
            w_out = full["sc_w_out"][j]
        else:
            hm = _mm(xb, full["lru_w_in"][j], bias=full["lru_b_in"][j:j + 1], name="lru_in")
            q, hs = _lru_fwd(hm.reshape(bsz, s, -1), full["lru_conv_w"][j], full["lru_conv_b"][j:j + 1],
                             full["lru_w_gate"][j], full["lru_b_gate"][j].reshape(heads, 1, -1),
                             full["lru_lambda"][j:j + 1], name="lru_mix")
            sv["hs"] = hs
            w_out = full["lru_w_out"][j]
        q = q.reshape(t, -1)
        z1, x1, x1b = _mm_ln(q, w_out, xt, alpha, lng[0:1], lnb[0:1], name="mix_out_ln")
        hf = _mm(x1b, full["ffn_w_up"][i], name="ffn_up")
        a = _ffn_fwd(hf.reshape(bsz, s, -1), full["ffn_conv_w"][i], full["ffn_conv_b"][i:i + 1], name="ffn_act")
        a = a.reshape(t, -1)
        z2, xt, xb = _mm_ln(a, full["ffn_w_down"][i], x1, alpha, lng[1:2], lnb[1:2], name="ffn_down_ln")
        sv.update(hm=hm, q=q, z1=z1, x1=x1b, hf=hf, a=a, z2=z2)
        saved.append(sv)

    sq, dx = _loss_head(xt, loss_target.reshape(t, d), name="loss_head")
    loss = lax.psum((0.5 / d) * sq[0, 0], MESH_AXES)

    grads = {k: [None] * w[k].shape[0] for k in WEIGHTS}
    scatters, tok = [None] * depth, None
    for i in reversed(range(depth)):
        j = i // 2
        sv = saved[i]
        lng = full["ln_g"][i]
        g2 = lng[1:2] if tok is None else lng[1:2] + tok
        dz2, dz2b, dg2, db2 = _ln_bwd(dx, sv["z2"], g2, name="ln_bwd")
        da = _mm(dz2b, full["ffn_w_down"][i], trans_w=True, name="ffn_down_dx")
        grads["ffn_w_down"][i] = _mm_tn(sv["a"], dz2b, name="ffn_down_dw")
        dhf, dwg, dwv, dbg, dbv = _ffn_bwd(sv["hf"].reshape(bsz, s, -1), da.reshape(bsz, s, -1), full["ffn_conv_w"][i],
                                           full["ffn_conv_b"][i:i + 1], name="ffn_act_bwd")
        dhf = dhf.reshape(t, -1)
        grads["ffn_conv_w"][i] = jnp.concatenate([dwg, dwv], axis=1)
        grads["ffn_conv_b"][i] = jnp.concatenate([dbg, dbv], axis=1)[0]
        grads["ffn_w_up"][i] = _mm_tn(sv["x1"], dhf, name="ffn_up_dw")
        dx1 = _mm(dhf, full["ffn_w_up"][i], trans_w=True, resid=dz2, resid_scale=alpha, name="ffn_up_dx")
        dz1, dz1b, dg1, db1 = _ln_bwd(dx1, sv["z1"], lng[0:1], name="ln_bwd")
        grads["ln_g"][i] = jnp.concatenate([dg1, dg2], axis=0)
        grads["ln_b"][i] = jnp.concatenate([db1, db2], axis=0)
        if i % 2 == 0:
            dq = _mm(dz1b, full["sc_w_out"][j], trans_w=True, name="sc_out_dx")
            grads["sc_w_out"][j] = _mm_tn(sv["q"], dz1b, name="sc_out_dw")
            dhm, dcw, dcb = _sc_bwd(sv["hm"].reshape(bsz, s, -1), dq.reshape(bsz, s, -1), full["sc_conv_w"][j],
                                    full["sc_conv_b"][j:j + 1], name="sc_mix_bwd")
            dhm = dhm.reshape(t, -1)
            grads["sc_conv_w"][j] = dcw
            grads["sc_conv_b"][j] = dcb[0]
            grads["sc_w_in"][j] = _mm_tn(sv["x0"], dhm, name="sc_in_dw")
            dx = _mm(dhm, full["sc_w_in"][j], trans_w=True, resid=dz1, resid_scale=alpha, name="sc_in_dx")
        else:
            dq = _mm(dz1b, full["lru_w_out"][j], trans_w=True, name="lru_out_dx")
            grads["lru_w_out"][j] = _mm_tn(sv["q"], dz1b, name="lru_out_dw")
            dhm, dcw, dcb, dwgt, dbgt, dlam, sgb, srb = _lru_bwd(
                sv["hm"].reshape(bsz, s, -1), sv["hs"], dq.reshape(bsz, s, -1), full["lru_conv_w"][j],
                full["lru_conv_b"][j:j + 1], full["lru_w_gate"][j], full["lru_b_gate"][j].reshape(heads, 1, -1),
                full["lru_lambda"][j:j + 1], name="lru_mix_bwd")
            dhm = dhm.reshape(t, -1)
            grads["lru_conv_w"][j] = dcw
            grads["lru_conv_b"][j] = dcb[0]
            grads["lru_w_gate"][j] = dwgt
            grads["lru_b_gate"][j] = dbgt[:, 0, :]
            grads["lru_lambda"][j] = dlam[0]
            grads["lru_b_in"][j] = jnp.concatenate([sgb, srb], axis=1)[0]
            grads["lru_w_in"][j] = _mm_tn(sv["x0"], dhm, name="lru_in_dw")
            dx = _mm(dhm, full["lru_w_in"][j], trans_w=True, resid=dz1, resid_scale=alpha, name="lru_in_dx")
        send = [_slabs(grads[k][l], BIG[k] - 1) for k, l in _layer_big(i)]
        scatters[i] = _exchange_start(send, [False] * len(send), name=f"scatter_start_{i}")
        tok = scatters[i]["token"][0:1, 0:1]
    grad_x = dx.reshape(bsz, s, d)

    gsm = {k: jnp.stack(grads[k]) for k in SMALL + REPL}
    small_send = _pack([_slabs(gsm[k], gsm[k].ndim - 1).reshape(N_DEV, -1) for k in SMALL], small_rows)
    repl_sizes = [w[k].size for k in REPL]
    repl_rows = _pack_rows(repl_sizes)
    repl_send = _pack([gsm[k].reshape(1, -1) for k in REPL], repl_rows)[0]
    small_scatter = _exchange_start([small_send, repl_send], [False, True], name="scatter_start_small")

    out = {}

    def update(key, parts, wk, mk, vk):
        g, dl, mn, vn = _adamw(parts, wk, mk, vk, name="adamw_" + key)
        return g, dl, mn, vn

    def own_slab(src):
        return lax.dynamic_slice_in_dim(src, me, 1, axis=0)

    per_layer = {k: [None] * w[k].shape[0] for k in BIG}
    after = dx
    for i in reversed(range(depth)):
        srcs, lands = _exchange_wait(scatters[i], after, name=f"scatter_wait_{i}")
        for (k, l), src, land in zip(_layer_big(i), srcs, lands):
            shp = w[k].shape[1:]
            c2 = shp[-1]
            res = update(f"{k}_{l}", with_own(land, own_slab(src)).reshape(N_DEV, -1, c2), w[k][l].reshape(-1, c2),
                         m[k][l].reshape(-1, c2), v[k][l].reshape(-1, c2))
            per_layer[k][l] = [r.reshape(shp) for r in res]
            after = res[-1]
    for k in BIG:
        out[k] = [jnp.stack([per_layer[k][l][r_i] for l in range(w[k].shape[0])]) for r_i in range(4)]
    srcs, lands = _exchange_wait(small_scatter, after, name="scatter_wait_small")
    got_small = with_own(lands[0], own_slab(srcs[0]))
    got_repl = with_own(lands[1], srcs[1][None])
    pk = lambda src, names, rows: _pack([src[k].reshape(1, -1) for k in names], rows)[0]
    res = update("small", got_small, small_local, pk(m, SMALL, small_rows), pk(v, SMALL, small_rows))
    for r_i, r in enumerate(res):
        for k, seg in zip(SMALL, _unpack(r, small_sizes)):
            out.setdefault(k, [None] * 4)[r_i] = seg.reshape(w[k].shape)
    res = update("repl", got_repl, pk(w, REPL, repl_rows), pk(m, REPL, repl_rows), pk(v, REPL, repl_rows))
    for r_i, r in enumerate(res):
        for k, seg in zip(REPL, _unpack(r, repl_sizes)):
            out.setdefault(k, [None] * 4)[r_i] = seg.reshape(w[k].shape)

    return (loss, grad_x, *[out[k][0] for k in WEIGHTS], *[out[k][1] for k in WEIGHTS],
            *[out[k][2] for k in WEIGHTS], *[out[k][3] for k in WEIGHTS])


def kernel(x, sc_w_in, sc_conv_w, sc_conv_b, sc_w_out, lru_w_in, lru_b_in, lru_conv_w, lru_conv_b, lru_w_gate, lru_b_gate, lru_lambda, lru_w_out, ffn_w_up, ffn_conv_w, ffn_conv_b, ffn_w_down, ln_g, ln_b, loss_target, m_sc_w_in, m_sc_conv_w, m_sc_conv_b, m_sc_w_out, m_lru_w_in, m_lru_b_in, m_lru_conv_w, m_lru_conv_b, m_lru_w_gate, m_lru_b_gate, m_lru_lambda, m_lru_w_out, m_ffn_w_up, m_ffn_conv_w, m_ffn_conv_b, m_ffn_w_down, m_ln_g, m_ln_b, v_sc_w_in, v_sc_conv_w, v_sc_conv_b, v_sc_w_out, v_lru_w_in, v_lru_b_in, v_lru_conv_w, v_lru_conv_b, v_lru_w_gate, v_lru_b_gate, v_lru_lambda, v_lru_w_out, v_ffn_w_up, v_ffn_conv_w, v_ffn_conv_b, v_ffn_w_down, v_ln_g, v_ln_b):
    w = dict(sc_w_in=sc_w_in, sc_conv_w=sc_conv_w, sc_conv_b=sc_conv_b, sc_w_out=sc_w_out, lru_w_in=lru_w_in,
             lru_b_in=lru_b_in, lru_conv_w=lru_conv_w, lru_conv_b=lru_conv_b, lru_w_gate=lru_w_gate,
             lru_b_gate=lru_b_gate, lru_lambda=lru_lambda, lru_w_out=lru_w_out, ffn_w_up=ffn_w_up,
             ffn_conv_w=ffn_conv_w, ffn_conv_b=ffn_conv_b, ffn_w_down=ffn_w_down, ln_g=ln_g, ln_b=ln_b)
    m = dict(sc_w_in=m_sc_w_in, sc_conv_w=m_sc_conv_w, sc_conv_b=m_sc_conv_b, sc_w_out=m_sc_w_out, lru_w_in=m_lru_w_in,
             lru_b_in=m_lru_b_in, lru_conv_w=m_lru_conv_w, lru_conv_b=m_lru_conv_b, lru_w_gate=m_lru_w_gate,
             lru_b_gate=m_lru_b_gate, lru_lambda=m_lru_lambda, lru_w_out=m_lru_w_out, ffn_w_up=m_ffn_w_up,
             ffn_conv_w=m_ffn_conv_w, ffn_conv_b=m_ffn_conv_b, ffn_w_down=m_ffn_w_down, ln_g=m_ln_g, ln_b=m_ln_b)
    v = dict(sc_w_in=v_sc_w_in, sc_conv_w=v_sc_conv_w, sc_conv_b=v_sc_conv_b, sc_w_out=v_sc_w_out, lru_w_in=v_lru_w_in,
             lru_b_in=v_lru_b_in, lru_conv_w=v_lru_conv_w, lru_conv_b=v_lru_conv_b, lru_w_gate=v_lru_w_gate,
             lru_b_gate=v_lru_b_gate, lru_lambda=v_lru_lambda, lru_w_out=v_lru_w_out, ffn_w_up=v_ffn_w_up,
             ffn_conv_w=v_ffn_conv_w, ffn_conv_b=v_ffn_conv_b, ffn_w_down=v_ffn_w_down, ln_g=v_ln_g, ln_b=v_ln_b)
    return _step(x, loss_target, w, m, v)
```

```python
import functools
import math

import jax
import jax.numpy as jnp
from jax import lax
from jax.experimental import pallas as pl
from jax.experimental.pallas import tpu as pltpu

F32 = jnp.float32
BF16 = jnp.bfloat16

N_DEV = 8
MESH_AXES = ("x", "y", "c")
LANES = 128
SUBLANES = 8
VMEM_LIMIT = 56 * 1024 * 1024
MM_LHS_ELEMS = 6 * 1024 * 1024

LRU_C = 8.0
LN_EPS = 1e-5
ADAM_LR = 0.001
ADAM_B1 = 0.9
ADAM_B2 = 0.999
ADAM_EPS = 1e-08
ADAM_WD = 0.01
ADAM_STEP = 10
GELU_K = math.sqrt(2.0 / math.pi)
GELU_C = 0.044715


def _tile(n, target, align):
    if n <= target:
        return n
    t = (target // align) * align
    while t >= align:
        if n % t == 0:
            return t
        t -= align
    return n


def _params(sem):
    return pltpu.CompilerParams(dimension_semantics=sem, vmem_limit_bytes=VMEM_LIMIT)


def _rows(x):
    return lax.broadcasted_iota(jnp.int32, x.shape, 0)


def _shift_dn(x, k, fill=0.0):
    if k == 0:
        return x
    return jnp.where(_rows(x) >= k, pltpu.roll(x, k, 0), fill)


def _shift_up(x, k, fill=0.0):
    if k == 0:
        return x
    s = x.shape[0]
    return jnp.where(_rows(x) < s - k, pltpu.roll(x, s - k, 0), fill)


def _conv_fwd(x, w, b):
    kw = w.shape[0]
    y = _shift_dn(x, kw - 1) * w[0:1, :] + b
    for k in range(1, kw):
        y = y + _shift_dn(x, kw - 1 - k) * w[k:k + 1, :]
    return y


def _conv_bwd_x(dy, w):
    kw = w.shape[0]
    dx = _shift_up(dy, kw - 1) * w[0:1, :]
    for k in range(1, kw):
        dx = dx + _shift_up(dy, kw - 1 - k) * w[k:k + 1, :]
    return dx


def _conv_bwd_w(dy, x, kw):
    return [jnp.sum(dy * _shift_dn(x, kw - 1 - k), axis=0, keepdims=True) for k in range(kw)]


def _accumulate(first, items):
    flat = []
    for ref, val in items:
        if isinstance(val, list):
            flat += [(ref, (slice(k, k + 1), slice(None)), row) for k, row in enumerate(val)]
        else:
            flat.append((ref, Ellipsis, val))

    @pl.when(first)
    def _():
        for ref, idx, val in flat:
            ref[idx] = val

    @pl.when(jnp.logical_not(first))
    def _():
        for ref, idx, val in flat:
            ref[idx] += val


def _colsum(x):
    return jnp.sum(x, axis=0, keepdims=True)


def _sigmoid(x):
    return 1.0 / (1.0 + jnp.exp(-x))


def _log1p(x):
    u = 1.0 + x
    return jnp.where(u == 1.0, x, jnp.log(u) * (x / (u - 1.0)))


def _softplus(x):
    return jnp.maximum(x, 0.0) + _log1p(jnp.exp(-jnp.abs(x)))


def _expm1(x, ex):
    poly = x * (1.0 + x * (0.5 + x * (1.0 / 6.0 + x * (1.0 / 24.0 + x * (1.0 / 120.0 + x * (1.0 / 720.0))))))
    return jnp.where(jnp.abs(x) < 0.25, poly, ex - 1.0)


def _gelu(x):
    t = jnp.tanh(GELU_K * (x + GELU_C * x * x * x))
    return 0.5 * x * (1.0 + t)


def _gelu_and_grad(x):
    x2 = x * x
    t = jnp.tanh(GELU_K * (x + GELU_C * x * x2))
    g = 0.5 * x * (1.0 + t)
    dg = 0.5 * (1.0 + t) + 0.5 * x * (1.0 - t * t) * (GELU_K * (1.0 + 3.0 * GELU_C * x2))
    return g, dg


def _scan_fwd(a, b):
    s = a.shape[0]
    k = 1
    while k < s:
        b = a * _shift_dn(b, k) + b
        if 2 * k < s:
            a = a * _shift_dn(a, k, 1.0)
        k *= 2
    return b


def _scan_rev(c, v):
    s = c.shape[0]
    k = 1
    while k < s:
        v = c * _shift_up(v, k) + v
        if 2 * k < s:
            c = c * _shift_up(c, k, 1.0)
        k *= 2
    return v


def _mm(a, w, *, name, trans_w=False, bias=None, resid=None, resid_scale=1.0, tn=512):
    m, k = a.shape
    n = w.shape[0] if trans_w else w.shape[1]
    tm = _tile(m, min(1024, max(256, MM_LHS_ELEMS // k)), SUBLANES)
    tn = _tile(n, tn, LANES)
    has_bias = bias is not None
    has_resid = resid is not None

    def body(*refs):
        a_ref, w_ref = refs[0], refs[1]
        pos = 2
        b_ref = r_ref = None
        if has_bias:
            b_ref = refs[pos]
            pos += 1
        if has_resid:
            r_ref = refs[pos]
            pos += 1
        o_ref = refs[pos]

        if trans_w:
            acc = lax.dot_general(a_ref[...], w_ref[...], (((1,), (1,)), ((), ())), preferred_element_type=F32)
        else:
            acc = jnp.dot(a_ref[...], w_ref[...], preferred_element_type=F32)
        if has_bias:
            acc = acc + b_ref[...]
        if has_resid:
            acc = acc + resid_scale * r_ref[...]
        o_ref[...] = acc

    in_specs = [pl.BlockSpec((tm, k), lambda i, j: (i, 0))]
    if trans_w:
        in_specs.append(pl.BlockSpec((tn, k), lambda i, j: (j, 0)))
    else:
        in_specs.append(pl.BlockSpec((k, tn), lambda i, j: (0, j)))
    args = [a, w]
    if has_bias:
        in_specs.append(pl.BlockSpec((1, tn), lambda i, j: (0, j)))
        args.append(bias)
    if has_resid:
        in_specs.append(pl.BlockSpec((tm, tn), lambda i, j: (i, j)))
        args.append(resid)
    return pl.pallas_call(
        body, name=name, grid=(m // tm, n // tn), in_specs=in_specs,
        out_specs=pl.BlockSpec((tm, tn), lambda i, j: (i, j)),
        out_shape=jax.ShapeDtypeStruct((m, n), F32),
        compiler_params=_params(("parallel", "arbitrary")),
    )(*args)


def _mm_ln(a, w, resid, alpha, g, b, *, name, tm=512):
    m, k = a.shape
    d = w.shape[1]
    tm = _tile(m, tm, SUBLANES)

    def body(a_ref, w_ref, r_ref, g_ref, b_ref, z_ref, o_ref, obf_ref):
        y = jnp.dot(a_ref[...], w_ref[...], preferred_element_type=F32)
        z = alpha * r_ref[...] + y
        z_ref[...] = z
        mu = jnp.mean(z, axis=-1, keepdims=True)
        zc = z - mu
        var = jnp.mean(zc * zc, axis=-1, keepdims=True)
        o = zc * lax.rsqrt(var + LN_EPS) * g_ref[...] + b_ref[...]
        o_ref[...] = o
        obf_ref[...] = o.astype(BF16)

    row = pl.BlockSpec((tm, d), lambda i: (i, 0))
    vec = pl.BlockSpec((1, d), lambda i: (0, 0))
    return pl.pallas_call(
        body, name=name, grid=(m // tm,),
        in_specs=[pl.BlockSpec((tm, k), lambda i: (i, 0)), pl.BlockSpec((k, d), lambda i: (0, 0)), row, vec, vec],
        out_specs=[row, row, row],
        out_shape=[jax.ShapeDtypeStruct((m, d), F32), jax.ShapeDtypeStruct((m, d), F32),
                   jax.ShapeDtypeStruct((m, d), BF16)],
        compiler_params=_params(("parallel",)),
    )(a, w, resid, g, b)


def _mm_tn(a, b, *, name, tm=1408, tn=1536, tk=1024):
    t, m = a.shape
    n = b.shape[1]
    tm = _tile(m, tm, LANES)
    tn = _tile(n, tn, LANES)
    tk = _tile(t, tk, SUBLANES)

    def body(a_ref, b_ref, o_ref):
        @pl.when(pl.program_id(2) == 0)
        def _():
            o_ref[...] = jnp.zeros_like(o_ref)

        o_ref[...] += lax.dot_general(a_ref[...], b_ref[...], (((0,), (0,)), ((), ())), preferred_element_type=F32)

    return pl.pallas_call(
        body, name=name, grid=(m // tm, n // tn, t // tk),
        in_specs=[pl.BlockSpec((tk, tm), lambda i, j, l: (l, i)), pl.BlockSpec((tk, tn), lambda i, j, l: (l, j))],
        out_specs=pl.BlockSpec((tm, tn), lambda i, j, l: (i, j)),
        out_shape=jax.ShapeDtypeStruct((m, n), F32),
        compiler_params=_params(("parallel", "parallel", "arbitrary")),
    )(a, b)


def _ln_bwd(dout, z, g, *, name, tm=512):
    t, d = z.shape
    tm = _tile(t, tm, SUBLANES)

    def body(do_ref, z_ref, g_ref, dz_ref, dzbf_ref, dg_ref, db_ref):
        @pl.when(pl.program_id(0) == 0)
        def _():
            dg_ref[...] = jnp.zeros_like(dg_ref)
            db_ref[...] = jnp.zeros_like(db_ref)

        z = z_ref[...]
        do = do_ref[...]
        mu = jnp.mean(z, axis=-1, keepdims=True)
        zc = z - mu
        var = jnp.mean(zc * zc, axis=-1, keepdims=True)
        rstd = lax.rsqrt(var + LN_EPS)
        xhat = zc * rstd
        dxh = do * g_ref[...]
        m1 = jnp.mean(dxh, axis=-1, keepdims=True)
        m2 = jnp.mean(dxh * xhat, axis=-1, keepdims=True)
        dz = rstd * (dxh - m1 - xhat * m2)
        dz_ref[...] = dz
        dzbf_ref[...] = dz.astype(BF16)
        dg_ref[...] += _colsum(do * xhat)
        db_ref[...] += _colsum(do)

    row = pl.BlockSpec((tm, d), lambda i: (i, 0))
    vec = pl.BlockSpec((1, d), lambda i: (0, 0))
    return pl.pallas_call(
        body, name=name, grid=(t // tm,), in_specs=[row, row, vec], out_specs=[row, row, vec, vec],
        out_shape=[jax.ShapeDtypeStruct((t, d), F32), jax.ShapeDtypeStruct((t, d), BF16),
                   jax.ShapeDtypeStruct((1, d), F32), jax.ShapeDtypeStruct((1, d), F32)],
        compiler_params=_params(("arbitrary",)),
    )(dout, z, g)


def _loss_head(y, target, *, name, tm=512):
    t, d = y.shape
    tm = _tile(t, tm, SUBLANES)

    def body(y_ref, t_ref, s_ref, dy_ref):
        @pl.when(pl.program_id(0) == 0)
        def _():
            s_ref[...] = jnp.zeros_like(s_ref)

        e = y_ref[...] - t_ref[...]
        dy_ref[...] = e * (1.0 / d)
        s_ref[...] += jnp.sum(_colsum(e * e), axis=-1, keepdims=True)

    row = pl.BlockSpec((tm, d), lambda i: (i, 0))
    return pl.pallas_call(
        body, name=name, grid=(t // tm,), in_specs=[row, row],
        out_specs=[pl.BlockSpec((1, LANES), lambda i: (0, 0)), row],
        out_shape=[jax.ShapeDtypeStruct((1, LANES), F32), jax.ShapeDtypeStruct((t, d), F32)],
        compiler_params=_params(("arbitrary",)),
    )(y, target)


def _own(c, b, *_):
    return c, b


def _ahead(nc, bsz):
    def at(c, b, part):
        b2 = b + jnp.minimum(part, 1)
        return jnp.minimum(c + b2 // bsz, nc - 1), b2 % bsz
    return at


def _strip(s, tc, off, at=_own):
    def index(*ids):
        c, b = at(*ids)
        return b, 0, off + c
    return pl.BlockSpec((None, s, tc), index)


def _cvec(kw, tc, off, at=_own):
    def index(*ids):
        return 0, off + at(*ids)[0]
    return pl.BlockSpec((kw, tc), index)


def _acc(kw, tc):
    return pl.BlockSpec((kw, tc), lambda c, b, *_: (0, c))


def _sc_fwd(h, cw, cb, *, name, tc=256):
    bsz, s, d3 = h.shape
    d = d3 // 3
    tc = _tile(d, tc, LANES)
    nc = d // tc

    def body(gb_ref, gc_ref, v_ref, w_ref, b_ref, q_ref):
        u = _conv_fwd(gc_ref[...] * v_ref[...], w_ref[...], b_ref[...])
        q_ref[...] = (gb_ref[...] * u).astype(BF16)

    return pl.pallas_call(
        body, name=name, grid=(nc, bsz),
        in_specs=[_strip(s, tc, 0), _strip(s, tc, nc), _strip(s, tc, 2 * nc), _cvec(cw.shape[0], tc, 0), _cvec(1, tc, 0)],
        out_specs=_strip(s, tc, 0),
        out_shape=jax.ShapeDtypeStruct((bsz, s, d), BF16),
        compiler_params=_params(("parallel", "parallel")),
    )(h, h, h, cw, cb)


def _sc_bwd(h, dq, cw, cb, *, name, tc=256):
    bsz, s, d3 = h.shape
    d = d3 // 3
    kw = cw.shape[0]
    tc = _tile(d, tc, LANES)
    nc = d // tc

    def body(gb_ref, gc_ref, v_ref, dq_ref, w_ref, b_ref, dh_ref, dw_ref, db_ref, parts):
        b_id, part = pl.program_id(1), pl.program_id(2)

        @pl.when(part == 0)
        def _():
            gb, gc, v, dq_, w = gb_ref[...], gc_ref[...], v_ref[...], dq_ref[...], w_ref[...]
            p = gc * v
            u = _conv_fwd(p, w, b_ref[...])
            du = dq_ * gb
            dp = _conv_bwd_x(du, w)
            parts[0] = (dq_ * u).astype(BF16)
            parts[1] = (dp * v).astype(BF16)
            parts[2] = (dp * gc).astype(BF16)
            _accumulate(b_id == 0, [(dw_ref, _conv_bwd_w(du, p, kw)), (db_ref, _colsum(du))])

        dh_ref[...] = parts[part]

    at = _ahead(nc, bsz)
    return pl.pallas_call(
        body, name=name, grid=(nc, bsz, 3),
        in_specs=[_strip(s, tc, 0, at), _strip(s, tc, nc, at), _strip(s, tc, 2 * nc, at), _strip(s, tc, 0, at),
                  _cvec(kw, tc, 0, at), _cvec(1, tc, 0, at)],
        out_specs=[pl.BlockSpec((None, s, tc), lambda c, b, p: (b, 0, p * nc + c)), _acc(kw, tc), _acc(1, tc)],
        out_shape=[jax.ShapeDtypeStruct((bsz, s, d3), BF16), jax.ShapeDtypeStruct((kw, d), F32),
                   jax.ShapeDtypeStruct((1, d), F32)],
        scratch_shapes=[pltpu.VMEM((3, s, tc), BF16)],
        compiler_params=_params(("parallel", "arbitrary", "arbitrary")),
    )(h, h, h, dq, cw, cb)


def _ffn_fwd(h, cw, cb, *, name, tc=256):
    bsz, s, f2 = h.shape
    f = f2 // 2
    tc = _tile(f, tc, LANES)
    nc = f // tc

    def body(hg_ref, hv_ref, wg_ref, wv_ref, bg_ref, bv_ref, a_ref):
        g = _conv_fwd(hg_ref[...], wg_ref[...], bg_ref[...])
        v = _conv_fwd(hv_ref[...], wv_ref[...], bv_ref[...])
        a_ref[...] = (g * _sigmoid(g) * v).astype(BF16)

    kw = cw.shape[0]
    return pl.pallas_call(
        body, name=name, grid=(nc, bsz),
        in_specs=[_strip(s, tc, 0), _strip(s, tc, nc), _cvec(kw, tc, 0), _cvec(kw, tc, nc), _cvec(1, tc, 0), _cvec(1, tc, nc)],
        out_specs=_strip(s, tc, 0),
        out_shape=jax.ShapeDtypeStruct((bsz, s, f), BF16),
        compiler_params=_params(("parallel", "parallel")),
    )(h, h, cw, cw, cb, cb)


def _ffn_bwd(h, da, cw, cb, *, name, tc=256):
    bsz, s, f2 = h.shape
    f = f2 // 2
    kw = cw.shape[0]
    tc = _tile(f, tc, LANES)
    nc = f // tc

    def body(hg_ref, hv_ref, da_ref, wg_ref, wv_ref, bg_ref, bv_ref,
             dh_ref, dwg_ref, dwv_ref, dbg_ref, dbv_ref, parts):
        b_id, part = pl.program_id(1), pl.program_id(2)

        @pl.when(part == 0)
        def _():
            hg, hv, da_ = hg_ref[...], hv_ref[...], da_ref[...]
            wg, wv = wg_ref[...], wv_ref[...]
            g = _conv_fwd(hg, wg, bg_ref[...])
            v = _conv_fwd(hv, wv, bv_ref[...])
            sg = _sigmoid(g)
            dv = da_ * (g * sg)
            dg = da_ * v * (sg * (1.0 + g * (1.0 - sg)))
            parts[0] = _conv_bwd_x(dg, wg).astype(BF16)
            parts[1] = _conv_bwd_x(dv, wv).astype(BF16)
            _accumulate(b_id == 0, [(dwg_ref, _conv_bwd_w(dg, hg, kw)), (dwv_ref, _conv_bwd_w(dv, hv, kw)),
                                    (dbg_ref, _colsum(dg)), (dbv_ref, _colsum(dv))])

        dh_ref[...] = parts[part]

    at = _ahead(nc, bsz)
    return pl.pallas_call(
        body, name=name, grid=(nc, bsz, 2),
        in_specs=[_strip(s, tc, 0, at), _strip(s, tc, nc, at), _strip(s, tc, 0, at), _cvec(kw, tc, 0, at),
                  _cvec(kw, tc, nc, at), _cvec(1, tc, 0, at), _cvec(1, tc, nc, at)],
        out_specs=[pl.BlockSpec((None, s, tc), lambda c, b, p: (b, 0, p * nc + c)),
                   _acc(kw, tc), _acc(kw, tc), _acc(1, tc), _acc(1, tc)],
        out_shape=[jax.ShapeDtypeStruct((bsz, s, f2), BF16), jax.ShapeDtypeStruct((kw, f), F32),
                   jax.ShapeDtypeStruct((kw, f), F32), jax.ShapeDtypeStruct((1, f), F32), jax.ShapeDtypeStruct((1, f), F32)],
        scratch_shapes=[pltpu.VMEM((2, s, tc), BF16)],
        compiler_params=_params(("parallel", "arbitrary", "arbitrary")),
    )(h, h, da, cw, cw, cb, cb)


def _lru_gates(r, cw, cb, wg, bg, lam):
    blk = r.shape[1]
    xr = _conv_fwd(r, cw, cb)
    gates = jnp.dot(xr.astype(BF16), wg, preferred_element_type=F32) + bg
    rg = _sigmoid(gates[:, :blk])
    ig = _sigmoid(gates[:, blk:])
    sp = _softplus(-lam)
    la = (-LRU_C * sp) * rg
    a = jnp.exp(la)
    mult = jnp.sqrt(-_expm1(2.0 * la, a * a))
    return xr, rg, ig, sp, a, mult


def _lru_specs(s, blk, heads, kw, at=_own):
    def per_head(*ids):
        return at(*ids)[0], 0, 0
    return [_strip(s, blk, 0, at), _strip(s, blk, heads, at), _cvec(kw, blk, 0, at), _cvec(1, blk, 0, at),
            pl.BlockSpec((None, blk, 2 * blk), per_head), pl.BlockSpec((None, 1, 2 * blk), per_head),
            _cvec(1, blk, 0, at)]


def _lru_fwd(h, cw, cb, wg, bg, lam, *, name):
    bsz, s, r2 = h.shape
    heads, blk = wg.shape[0], wg.shape[1]
    kw = cw.shape[0]

    def body(g_ref, r_ref, cw_ref, cb_ref, wg_ref, bg_ref, lam_ref, y_ref, hs_ref):
        xr, _, ig, _, a, mult = _lru_gates(r_ref[...], cw_ref[...], cb_ref[...], wg_ref[...], bg_ref[...], lam_ref[...])
        hs = _scan_fwd(a, mult * (ig * xr))
        hs_ref[...] = hs
        y_ref[...] = (hs * _gelu(g_ref[...])).astype(BF16)

    out = pl.BlockSpec((None, s, blk), lambda hd, b: (b, 0, hd))
    return pl.pallas_call(
        body, name=name, grid=(heads, bsz), in_specs=_lru_specs(s, blk, heads, kw), out_specs=[out, out],
        out_shape=[jax.ShapeDtypeStruct((bsz, s, r2 // 2), BF16), jax.ShapeDtypeStruct((bsz, s, r2 // 2), F32)],
        compiler_params=_params(("parallel", "parallel")),
    )(h, h, cw, cb, wg, bg, lam)


def _lru_bwd(h, hs, dy, cw, cb, wg, bg, lam, *, name):
    bsz, s, r2 = h.shape
    rw = r2 // 2
    heads, blk = wg.shape[0], wg.shape[1]
    kw = cw.shape[0]

    def body(g_ref, r_ref, cw_ref, cb_ref, wg_ref, bg_ref, lam_ref, hs_ref, dy_ref,
             dh_ref, dcw_ref, dcb_ref, dwg_ref, dbg_ref, dlam_ref, sg_ref, sr_ref, parts):
        b_id, part = pl.program_id(1), pl.program_id(2)

        @pl.when(part == 0)
        def _():
            r, cw_, wg_, lam_ = r_ref[...], cw_ref[...], wg_ref[...], lam_ref[...]
            xr, rg, ig, sp, a, mult = _lru_gates(r, cw_, cb_ref[...], wg_, bg_ref[...], lam_)
            hs_, dy_ = hs_ref[...], dy_ref[...]
            gel, dgel = _gelu_and_grad(g_ref[...])
            dg = dy_ * hs_ * dgel
            lmb = _scan_rev(_shift_up(a, 1, 1.0), dy_ * gel)
            da = lmb * _shift_dn(hs_, 1)
            dmult = lmb * (ig * xr)
            dig = lmb * (mult * xr)
            dxr = lmb * (mult * ig)
            dla = da * a - dmult * (a * a / mult)
            drg = dla * (-LRU_C * sp)
            dsp = _colsum(dla * rg) * (-LRU_C)
            dlam = -dsp * _sigmoid(-lam_)
            dgates = jnp.concatenate([drg * (rg * (1.0 - rg)), dig * (ig * (1.0 - ig))], axis=1)
            dgates_bf = dgates.astype(BF16)
            dwg = lax.dot_general(xr.astype(BF16), dgates_bf, (((0,), (0,)), ((), ())), preferred_element_type=F32)
            dxr = dxr + lax.dot_general(dgates_bf, wg_, (((1,), (1,)), ((), ())), preferred_element_type=F32)
            dr = _conv_bwd_x(dxr, cw_)
            parts[0] = dg.astype(BF16)
            parts[1] = dr.astype(BF16)
            _accumulate(b_id == 0, [(dcw_ref, _conv_bwd_w(dxr, r, kw)), (dcb_ref, _colsum(dxr)), (dwg_ref, dwg),
                                    (dbg_ref, _colsum(dgates)), (dlam_ref, dlam), (sg_ref, _colsum(dg)),
                                    (sr_ref, _colsum(dr))])

        dh_ref[...] = parts[part]

    at = _ahead(heads, bsz)
    strip = _strip(s, blk, 0, at)
    vec = pl.BlockSpec((1, blk), lambda hd, b, p: (0, hd))
    return pl.pallas_call(
        body, name=name, grid=(heads, bsz, 2),
        in_specs=_lru_specs(s, blk, heads, kw, at) + [strip, strip],
        out_specs=[pl.BlockSpec((None, s, blk), lambda hd, b, p: (b, 0, p * heads + hd)),
                   pl.BlockSpec((kw, blk), lambda hd, b, p: (0, hd)), vec,
                   pl.BlockSpec((None, blk, 2 * blk), lambda hd, b, p: (hd, 0, 0)),
                   pl.BlockSpec((None, 1, 2 * blk), lambda hd, b, p: (hd, 0, 0)), vec, vec, vec],
        out_shape=[jax.ShapeDtypeStruct((bsz, s, r2), BF16), jax.ShapeDtypeStruct((kw, rw), F32),
                   jax.ShapeDtypeStruct((1, rw), F32), jax.ShapeDtypeStruct((heads, blk, 2 * blk), F32),
                   jax.ShapeDtypeStruct((heads, 1, 2 * blk), F32), jax.ShapeDtypeStruct((1, rw), F32),
                   jax.ShapeDtypeStruct((1, rw), F32), jax.ShapeDtypeStruct((1, rw), F32)],
        scratch_shapes=[pltpu.VMEM((2, s, blk), BF16)],
        compiler_params=_params(("parallel", "arbitrary", "arbitrary")),
    )(h, h, cw, cb, wg, bg, lam, hs, dy)


HBM_SPEC = pl.BlockSpec(memory_space=pltpu.HBM)
SEM_SPEC = pl.BlockSpec(memory_space=pltpu.SEMAPHORE)
EFFECT = pltpu.SideEffectType.DATAFLOW_SIDE_EFFECTING


def _peer_copies(srcs, lands, gather, send_sem, recv_sem):
    x, y, c = (lax.axis_index(ax) for ax in MESH_AXES)
    me = 4 * x + 2 * y + c
    copies = []
    for i in range(len(srcs)):
        for d in range(1, N_DEV):
            px = 1 - x if d & 4 else x
            py = 1 - y if d & 2 else y
            pc = 1 - c if d & 1 else c
            src = srcs[i] if gather[i] else srcs[i].at[4 * px + 2 * py + pc]
            k = i * (N_DEV - 1) + d - 1
            copies.append(pltpu.make_async_remote_copy(
                src_ref=src, dst_ref=lands[i].at[me], send_sem=send_sem.at[k], recv_sem=recv_sem.at[k],
                device_id=(px, py, pc), device_id_type=pl.DeviceIdType.MESH))
    return copies


def _exchange_start(arrs, gather, *, name):
    n = len(arrs)
    lands = [lax.empty((N_DEV,) + tuple(a.shape if g else a.shape[1:]), a.dtype) for a, g in zip(arrs, gather)]

    def body(*refs):
        srcs, land_refs = refs[:n], refs[n:2 * n]
        send_sem, recv_sem = refs[2 * n], refs[2 * n + 1]
        token = refs[-1]
        for cp in _peer_copies(srcs, land_refs, gather, send_sem, recv_sem):
            cp.start()
        token[...] = jnp.zeros_like(token)

    sems = pltpu.SemaphoreType.DMA((n * (N_DEV - 1),))
    thru = [pltpu.HBM(a.shape, a.dtype) for a in arrs + lands]
    out = pl.pallas_call(
        body, name=name, in_specs=[HBM_SPEC] * (2 * n),
        out_shape=(sems, sems, *thru, jax.ShapeDtypeStruct((SUBLANES, LANES), F32)),
        out_specs=(SEM_SPEC, SEM_SPEC, *([HBM_SPEC] * (2 * n)), pl.BlockSpec(memory_space=pltpu.VMEM)),
        input_output_aliases={i: 2 + i for i in range(2 * n)},
        compiler_params=pltpu.CompilerParams(has_side_effects=EFFECT),
    )(*[pltpu.with_memory_space_constraint(a, pltpu.HBM) for a in arrs + lands])
    return {"send_sem": out[0], "recv_sem": out[1], "srcs": list(out[2:2 + n]), "lands": list(out[2 + n:2 + 2 * n]),
            "token": out[-1], "gather": list(gather)}


def _exchange_wait(handle, after, *, name):
    srcs, lands, gather = handle["srcs"], handle["lands"], handle["gather"]
    n = len(srcs)

    def body(*refs):
        src_refs, land_refs = refs[:n], refs[n:2 * n]
        send_sem, recv_sem = refs[2 * n], refs[2 * n + 1]
        for cp in _peer_copies(src_refs, land_refs, gather, send_sem, recv_sem):
            cp.wait_send()
            cp.wait_recv()

    out = pl.pallas_call(
        body, name=name,
        in_specs=[HBM_SPEC] * (2 * n) + [SEM_SPEC, SEM_SPEC, pl.BlockSpec(memory_space=pl.ANY)],
        out_shape=tuple(pltpu.HBM(a.shape, a.dtype) for a in srcs + lands), out_specs=tuple([HBM_SPEC] * (2 * n)),
        input_output_aliases={i: i for i in range(2 * n)},
        compiler_params=pltpu.CompilerParams(has_side_effects=EFFECT),
    )(*srcs, *lands, handle["send_sem"], handle["recv_sem"], after)
    return list(out[:n]), list(out[n:])


def _adamw(parts, w, m, v, *, name, tr=256):
    r, c = w.shape
    tr = _tile(r, tr, SUBLANES)
    bc1 = 1.0 / (1.0 - ADAM_B1 ** ADAM_STEP)
    bc2 = 1.0 / (1.0 - ADAM_B2 ** ADAM_STEP)

    def body(p_ref, w_ref, m_ref, v_ref, g_ref, d_ref, mo_ref, vo_ref):
        g = p_ref[0].astype(F32)
        for s in range(1, N_DEV):
            g = g + p_ref[s].astype(F32)
        m_new = ADAM_B1 * m_ref[...] + (1.0 - ADAM_B1) * g
        v_new = ADAM_B2 * v_ref[...] + (1.0 - ADAM_B2) * (g * g)
        g_ref[...] = g
        mo_ref[...] = m_new
        vo_ref[...] = v_new
        d_ref[...] = -ADAM_LR * ((m_new * bc1) / (jnp.sqrt(v_new * bc2) + ADAM_EPS) + ADAM_WD * w_ref[...])

    blk = pl.BlockSpec((tr, c), lambda i: (i, 0))
    return pl.pallas_call(
        body, name=name, grid=(r // tr,),
        in_specs=[pl.BlockSpec((N_DEV, tr, c), lambda i: (0, i, 0)), blk, blk, blk],
        out_specs=[blk] * 4, out_shape=[jax.ShapeDtypeStruct((r, c), F32)] * 4,
        compiler_params=_params(("parallel",)),
    )(parts, w, m, v)


def _whole(slabs, axis):
    x = jnp.moveaxis(slabs, 0, axis)
    shp = x.shape
    return x.reshape(shp[:axis] + (shp[axis] * shp[axis + 1],) + shp[axis + 2:])


def _slabs(whole, axis):
    shp = whole.shape
    x = whole.reshape(shp[:axis] + (N_DEV, shp[axis] // N_DEV) + shp[axis + 1:])
    return jnp.moveaxis(x, axis, 0)


def _pack(vecs, rows):
    flat = jnp.concatenate(vecs, axis=-1)
    pad = rows * LANES - flat.shape[-1]
    flat = jnp.pad(flat, [(0, 0)] * (flat.ndim - 1) + [(0, pad)])
    return flat.reshape(flat.shape[:-1] + (rows, LANES))


def _unpack(packed, sizes):
    flat = packed.reshape(packed.shape[:-2] + (-1,))
    out, pos = [], 0
    for n in sizes:
        out.append(flat[..., pos:pos + n])
        pos += n
    return out


def _pack_rows(sizes):
    total = sum(sizes)
    return -(-total // (LANES * SUBLANES)) * SUBLANES


BIG = {"sc_w_in": 2, "sc_w_out": 1, "lru_w_in": 2, "lru_w_gate": 3, "lru_w_out": 1, "ffn_w_up": 2, "ffn_w_down": 1}
SMALL = ["sc_conv_w", "lru_b_in", "lru_conv_w", "lru_conv_b", "lru_b_gate", "lru_lambda", "ffn_conv_w", "ln_g", "ln_b"]
REPL = ["sc_conv_b", "ffn_conv_b"]
WEIGHTS = ["sc_w_in", "sc_conv_w", "sc_conv_b", "sc_w_out", "lru_w_in", "lru_b_in", "lru_conv_w", "lru_conv_b",
           "lru_w_gate", "lru_b_gate", "lru_lambda", "lru_w_out", "ffn_w_up", "ffn_conv_w", "ffn_conv_b", "ffn_w_down",
           "ln_g", "ln_b"]


def _stage_big(g):
    i, j = g // 2, g // 4
    if g % 2:
        return [("ffn_w_up", i), ("ffn_w_down", i)]
    return [("sc_w_in", j), ("sc_w_out", j)] if i % 2 == 0 else [("lru_w_in", j), ("lru_w_gate", j), ("lru_w_out", j)]


def _step(x, loss_target, w, m, v):
    bsz, s, d = x.shape
    t = bsz * s
    depth = w["ffn_w_up"].shape[0]
    alpha = (2.0 * depth) ** 0.25
    heads = w["lru_w_gate"].shape[1]

    small_sizes = [w[k].size for k in SMALL]
    small_rows = _pack_rows(small_sizes)
    small_local = _pack([w[k].reshape(1, -1) for k in SMALL], small_rows)[0]
    me = 4 * lax.axis_index("x") + 2 * lax.axis_index("y") + lax.axis_index("c")

    def with_own(land, own):
        return lax.dynamic_update_slice_in_dim(land, own, me, axis=0)

    stages = 2 * depth
    gathers, tok = [], None
    for g in range(stages):
        arrs = [w[k][l].astype(BF16) for k, l in _stage_big(g)]
        if g == 0:
            arrs.append(small_local)
        if tok is not None:
            arrs[0] = arrs[0] + tok.astype(BF16)
        gathers.append(_exchange_start(arrs, [True] * len(arrs), name=f"gather_start_{g}"))
        tok = gathers[-1]["token"][0, 0]
    full = {k: [None] * w[k].shape[0] for k in BIG}
    full["sc_conv_b"] = w["sc_conv_b"]
    full["ffn_conv_b"] = w["ffn_conv_b"]

    def arrive(g, after):
        srcs, lands = _exchange_wait(gathers[g], after, name=f"gather_wait_{g}")
        for (k, l), src, land in zip(_stage_big(g), srcs, lands):
            full[k][l] = _whole(with_own(land, src[None]), BIG[k] - 1)
        if g == 0:
            for k, seg in zip(SMALL, _unpack(with_own(lands[-1], srcs[-1][None]), small_sizes)):
                full[k] = _whole(seg.reshape((N_DEV,) + w[k].shape), w[k].ndim - 1)

    xt = x.reshape(t, d)
    xb = xt.astype(BF16)
    saved = []
    for i in range(depth):
        j = i // 2
        arrive(2 * i, gathers[-1]["token"] if i == 0 else xb)
        lng, lnb = full["ln_g"][i], full["ln_b"][i]
        sv = {"x0": xb}
        if i % 2 == 0:
            hm = _mm(xb, full["sc_w_in"][j], name="sc_in")
            q = _sc_fwd(hm.reshape(bsz, s, -1), full["sc_conv_w"][j], full["sc_conv_b"][j:j + 1], name="sc_mix")
            w_out = full["sc_w_out"][j]
        else:
            hm = _mm(xb, full["lru_w_in"][j], bias=full["lru_b_in"][j:j + 1], name="lru_in")
            q, hs = _lru_fwd(hm.reshape(bsz, s, -1), full["lru_conv_w"][j], full["lru_conv_b"][j:j + 1],
                             full["lru_w_gate"][j], full["lru_b_gate"][j].reshape(heads, 1, -1),
                             full["lru_lambda"][j:j + 1], name="lru_mix")
            sv["hs"] = hs
            w_out = full["lru_w_out"][j]
        q = q.reshape(t, -1)
        arrive(2 * i + 1, q)
        z1, x1, x1b = _mm_ln(q, w_out, xt, alpha, lng[0:1], lnb[0:1], name="mix_out_ln")
        hf = _mm(x1b, full["ffn_w_up"][i], name="ffn_up")
        a = _ffn_fwd(hf.reshape(bsz, s, -1), full["ffn_conv_w"][i], full["ffn_conv_b"][i:i + 1], name="ffn_act")
        a = a.reshape(t, -1)
        z2, xt, xb = _mm_ln(a, full["ffn_w_down"][i], x1, alpha, lng[1:2], lnb[1:2], name="ffn_down_ln")
        sv.update(hm=hm, q=q, z1=z1, x1=x1b, hf=hf, a=a, z2=z2)
        saved.append(sv)

    sq, dx = _loss_head(xt, loss_target.reshape(t, d), name="loss_head")
    loss = lax.psum((0.5 / d) * sq[0, 0], MESH_AXES)

    grads = {k: [None] * w[k].shape[0] for k in WEIGHTS}
    scatters, tok = [None] * stages, None

    def depart(g):
        send = [_slabs(grads[k][l], BIG[k] - 1).astype(BF16) for k, l in _stage_big(g)]
        scatters[g] = _exchange_start(send, [False] * len(send), name=f"scatter_start_{g}")
        return scatters[g]["token"][0:1, 0:1]

    for i in reversed(range(depth)):
        j = i // 2
        sv = saved[i]
        lng = full["ln_g"][i]
        g2 = lng[1:2] if tok is None else lng[1:2] + tok
        dz2, dz2b, dg2, db2 = _ln_bwd(dx, sv["z2"], g2, name="ln_bwd")
        da = _mm(dz2b, full["ffn_w_down"][i], trans_w=True, name="ffn_down_dx")
        grads["ffn_w_down"][i] = _mm_tn(sv["a"], dz2b, name="ffn_down_dw")
        dhf, dwg, dwv, dbg, dbv = _ffn_bwd(sv["hf"].reshape(bsz, s, -1), da.reshape(bsz, s, -1), full["ffn_conv_w"][i],
                                           full["ffn_conv_b"][i:i + 1], name="ffn_act_bwd")
        dhf = dhf.reshape(t, -1)
        grads["ffn_conv_w"][i] = jnp.concatenate([dwg, dwv], axis=1)
        grads["ffn_conv_b"][i] = jnp.concatenate([dbg, dbv], axis=1)[0]
        grads["ffn_w_up"][i] = _mm_tn(sv["x1"], dhf, name="ffn_up_dw")
        dx1 = _mm(dhf, full["ffn_w_up"][i], trans_w=True, resid=dz2, resid_scale=alpha, name="ffn_up_dx")
        dz1, dz1b, dg1, db1 = _ln_bwd(dx1, sv["z1"], lng[0:1] + depart(2 * i + 1), name="ln_bwd")
        grads["ln_g"][i] = jnp.concatenate([dg1, dg2], axis=0)
        grads["ln_b"][i] = jnp.concatenate([db1, db2], axis=0)
        if i % 2 == 0:
            dq = _mm(dz1b, full["sc_w_out"][j], trans_w=True, name="sc_out_dx")
            grads["sc_w_out"][j] = _mm_tn(sv["q"], dz1b, name="sc_out_dw")
            dhm, dcw, dcb = _sc_bwd(sv["hm"].reshape(bsz, s, -1), dq.reshape(bsz, s, -1), full["sc_conv_w"][j],
                                    full["sc_conv_b"][j:j + 1], name="sc_mix_bwd")
            dhm = dhm.reshape(t, -1)
            grads["sc_conv_w"][j] = dcw
            grads["sc_conv_b"][j] = dcb[0]
            grads["sc_w_in"][j] = _mm_tn(sv["x0"], dhm, name="sc_in_dw")
            dx = _mm(dhm, full["sc_w_in"][j], trans_w=True, resid=dz1, resid_scale=alpha, name="sc_in_dx")
        else:
            dq = _mm(dz1b, full["lru_w_out"][j], trans_w=True, name="lru_out_dx")
            grads["lru_w_out"][j] = _mm_tn(sv["q"], dz1b, name="lru_out_dw")
            dhm, dcw, dcb, dwgt, dbgt, dlam, sgb, srb = _lru_bwd(
                sv["hm"].reshape(bsz, s, -1), sv["hs"], dq.reshape(bsz, s, -1), full["lru_conv_w"][j],
                full["lru_conv_b"][j:j + 1], full["lru_w_gate"][j], full["lru_b_gate"][j].reshape(heads, 1, -1),
                full["lru_lambda"][j:j + 1], name="lru_mix_bwd")
            dhm = dhm.reshape(t, -1)
            grads["lru_conv_w"][j] = dcw
            grads["lru_conv_b"][j] = dcb[0]
            grads["lru_w_gate"][j] = dwgt
            grads["lru_b_gate"][j] = dbgt[:, 0, :]
            grads["lru_lambda"][j] = dlam[0]
            grads["lru_b_in"][j] = jnp.concatenate([sgb, srb], axis=1)[0]
            grads["lru_w_in"][j] = _mm_tn(sv["x0"], dhm, name="lru_in_dw")
            dx = _mm(dhm, full["lru_w_in"][j], trans_w=True, resid=dz1, resid_scale=alpha, name="lru_in_dx")
        tok = depart(2 * i)
    grad_x = dx.reshape(bsz, s, d)

    gsm = {k: jnp.stack(grads[k]) for k in SMALL + REPL}
    small_send = _pack([_slabs(gsm[k], gsm[k].ndim - 1).reshape(N_DEV, -1) for k in SMALL], small_rows)
    repl_sizes = [w[k].size for k in REPL]
    repl_rows = _pack_rows(repl_sizes)
    repl_send = _pack([gsm[k].reshape(1, -1) for k in REPL], repl_rows)[0]
    small_scatter = _exchange_start([small_send, repl_send], [False, True], name="scatter_start_small")

    out = {}

    def update(key, parts, wk, mk, vk):
        g, dl, mn, vn = _adamw(parts, wk, mk, vk, name="adamw_" + key)
        return g, dl, mn, vn

    def own_slab(src):
        return lax.dynamic_slice_in_dim(src, me, 1, axis=0)

    per_layer = {k: [None] * w[k].shape[0] for k in BIG}
    after = dx
    for g in reversed(range(stages)):
        srcs, lands = _exchange_wait(scatters[g], after, name=f"scatter_wait_{g}")
        for (k, l), src, land in zip(_stage_big(g), srcs, lands):
            shp = w[k].shape[1:]
            c2 = shp[-1]
            res = update(f"{k}_{l}", with_own(land, own_slab(src)).reshape(N_DEV, -1, c2), w[k][l].reshape(-1, c2),
                         m[k][l].reshape(-1, c2), v[k][l].reshape(-1, c2))
            per_layer[k][l] = [r.reshape(shp) for r in res]
            after = res[-1]
    for k in BIG:
        out[k] = [jnp.stack([per_layer[k][l][r_i] for l in range(w[k].shape[0])]) for r_i in range(4)]
    srcs, lands = _exchange_wait(small_scatter, after, name="scatter_wait_small")
    got_small = with_own(lands[0], own_slab(srcs[0]))
    got_repl = with_own(lands[1], srcs[1][None])
    pk = lambda src, names, rows: _pack([src[k].reshape(1, -1) for k in names], rows)[0]
    res = update("small", got_small, small_local, pk(m, SMALL, small_rows), pk(v, SMALL, small_rows))
    for r_i, r in enumerate(res):
        for k, seg in zip(SMALL, _unpack(r, small_sizes)):
            out.setdefault(k, [None] * 4)[r_i] = seg.reshape(w[k].shape)
    res = update("repl", got_repl, pk(w, REPL, repl_rows), pk(m, REPL, repl_rows), pk(v, REPL, repl_rows))
    for r_i, r in enumerate(res):
        for k, seg in zip(REPL, _unpack(r, repl_sizes)):
            out.setdefault(k, [None] * 4)[r_i] = seg.reshape(w[k].shape)

    return (loss, grad_x, *[out[k][0] for k in WEIGHTS], *[out[k][1] for k in WEIGHTS],
            *[out[k][2] for k in WEIGHTS], *[out[k][3] for k in WEIGHTS])


def kernel(x, sc_w_in, sc_conv_w, sc_conv_b, sc_w_out, lru_w_in, lru_b_in, lru_conv_w, lru_conv_b, lru_w_gate, lru_b_gate, lru_lambda, lru_w_out, ffn_w_up, ffn_conv_w, ffn_conv_b, ffn_w_down, ln_g, ln_b, loss_target, m_sc_w_in, m_sc_conv_w, m_sc_conv_b, m_sc_w_out, m_lru_w_in, m_lru_b_in, m_lru_conv_w, m_lru_conv_b, m_lru_w_gate, m_lru_b_gate, m_lru_lambda, m_lru_w_out, m_ffn_w_up, m_ffn_conv_w, m_ffn_conv_b, m_ffn_w_down, m_ln_g, m_ln_b, v_sc_w_in, v_sc_conv_w, v_sc_conv_b, v_sc_w_out, v_lru_w_in, v_lru_b_in, v_lru_conv_w, v_lru_conv_b, v_lru_w_gate, v_lru_b_gate, v_lru_lambda, v_lru_w_out, v_ffn_w_up, v_ffn_conv_w, v_ffn_conv_b, v_ffn_w_down, v_ln_g, v_ln_b):
    w = dict(sc_w_in=sc_w_in, sc_conv_w=sc_conv_w, sc_conv_b=sc_conv_b, sc_w_out=sc_w_out, lru_w_in=lru_w_in,
             lru_b_in=lru_b_in, lru_conv_w=lru_conv_w, lru_conv_b=lru_conv_b, lru_w_gate=lru_w_gate,
             lru_b_gate=lru_b_gate, lru_lambda=lru_lambda, lru_w_out=lru_w_out, ffn_w_up=ffn_w_up,
             ffn_conv_w=ffn_conv_w, ffn_conv_b=ffn_conv_b, ffn_w_down=ffn_w_down, ln_g=ln_g, ln_b=ln_b)
    m = dict(sc_w_in=m_sc_w_in, sc_conv_w=m_sc_conv_w, sc_conv_b=m_sc_conv_b, sc_w_out=m_sc_w_out, lru_w_in=m_lru_w_in,
             lru_b_in=m_lru_b_in, lru_conv_w=m_lru_conv_w, lru_conv_b=m_lru_conv_b, lru_w_gate=m_lru_w_gate,
             lru_b_gate=m_lru_b_gate, lru_lambda=m_lru_lambda, lru_w_out=m_lru_w_out, ffn_w_up=m_ffn_w_up,
             ffn_conv_w=m_ffn_conv_w, ffn_conv_b=m_ffn_conv_b, ffn_w_down=m_ffn_w_down, ln_g=m_ln_g, ln_b=m_ln_b)
    v = dict(sc_w_in=v_sc_w_in, sc_conv_w=v_sc_conv_w, sc_conv_b=v_sc_conv_b, sc_w_out=v_sc_w_out, lru_w_in=v_lru_w_in,
             lru_b_in=v_lru_b_in, lru_conv_w=v_lru_conv_w, lru_conv_b=v_lru_conv_b, lru_w_gate=v_lru_w_gate,
             lru_b_gate=v_lru_b_gate, lru_lambda=v_lru_lambda, lru_w_out=v_lru_w_out, ffn_w_up=v_ffn_w_up,
             ffn_conv_w=v_ffn_conv_w, ffn_conv_b=v_ffn_conv_b, ffn_w_down=v_ffn_w_down, ln_g=v_ln_g, ln_b=v_ln_b)
    return _step(x, loss_target, w, m, v)
```

```python
import functools
import math

import jax
import jax.numpy as jnp
from jax import lax
from jax.experimental import pallas as pl
from jax.experimental.pallas import tpu as pltpu

F32 = jnp.float32
BF16 = jnp.bfloat16

N_DEV = 8
MESH_AXES = ("x", "y", "c")
LANES = 128
SUBLANES = 8
VMEM_LIMIT = 56 * 1024 * 1024
MM_LHS_ELEMS = 3 * 1024 * 1024
MM_TN = 1536

LRU_C = 8.0
LN_EPS = 1e-5
ADAM_LR = 0.001
ADAM_B1 = 0.9
ADAM_B2 = 0.999
ADAM_EPS = 1e-08
ADAM_WD = 0.01
ADAM_STEP = 10
GELU_K = math.sqrt(2.0 / math.pi)
GELU_C = 0.044715


def _tile(n, target, align):
    if n <= target:
        return n
    t = (target // align) * align
    while t >= align:
        if n % t == 0:
            return t
        t -= align
    return n


def _params(sem):
    return pltpu.CompilerParams(dimension_semantics=sem, vmem_limit_bytes=VMEM_LIMIT)


def _rows(x):
    return lax.broadcasted_iota(jnp.int32, x.shape, 0)


def _shift_dn(x, k, fill=0.0):
    if k == 0:
        return x
    return jnp.where(_rows(x) >= k, pltpu.roll(x, k, 0), fill)


def _shift_up(x, k, fill=0.0):
    if k == 0:
        return x
    s = x.shape[0]
    return jnp.where(_rows(x) < s - k, pltpu.roll(x, s - k, 0), fill)


def _conv_fwd(x, w, b):
    kw = w.shape[0]
    y = _shift_dn(x, kw - 1) * w[0:1, :] + b
    for k in range(1, kw):
        y = y + _shift_dn(x, kw - 1 - k) * w[k:k + 1, :]
    return y


def _conv_bwd_x(dy, w):
    kw = w.shape[0]
    dx = _shift_up(dy, kw - 1) * w[0:1, :]
    for k in range(1, kw):
        dx = dx + _shift_up(dy, kw - 1 - k) * w[k:k + 1, :]
    return dx


def _conv_bwd_w(dy, x, kw):
    return [jnp.sum(dy * _shift_dn(x, kw - 1 - k), axis=0, keepdims=True) for k in range(kw)]


def _accumulate(first, items):
    flat = []
    for ref, val in items:
        if isinstance(val, list):
            flat += [(ref, (slice(k, k + 1), slice(None)), row) for k, row in enumerate(val)]
        else:
            flat.append((ref, Ellipsis, val))

    @pl.when(first)
    def _():
        for ref, idx, val in flat:
            ref[idx] = val

    @pl.when(jnp.logical_not(first))
    def _():
        for ref, idx, val in flat:
            ref[idx] += val


def _colsum(x):
    return jnp.sum(x, axis=0, keepdims=True)


def _sigmoid(x):
    return 1.0 / (1.0 + jnp.exp(-x))


def _log1p(x):
    u = 1.0 + x
    return jnp.where(u == 1.0, x, jnp.log(u) * (x / (u - 1.0)))


def _softplus(x):
    return jnp.maximum(x, 0.0) + _log1p(jnp.exp(-jnp.abs(x)))


def _expm1(x, ex):
    poly = x * (1.0 + x * (0.5 + x * (1.0 / 6.0 + x * (1.0 / 24.0 + x * (1.0 / 120.0 + x * (1.0 / 720.0))))))
    return jnp.where(jnp.abs(x) < 0.25, poly, ex - 1.0)


def _gelu(x):
    t = jnp.tanh(GELU_K * (x + GELU_C * x * x * x))
    return 0.5 * x * (1.0 + t)


def _gelu_and_grad(x):
    x2 = x * x
    t = jnp.tanh(GELU_K * (x + GELU_C * x * x2))
    g = 0.5 * x * (1.0 + t)
    dg = 0.5 * (1.0 + t) + 0.5 * x * (1.0 - t * t) * (GELU_K * (1.0 + 3.0 * GELU_C * x2))
    return g, dg


def _scan_fwd(a, b):
    s = a.shape[0]
    k = 1
    while k < s:
        b = a * _shift_dn(b, k) + b
        if 2 * k < s:
            a = a * _shift_dn(a, k, 1.0)
        k *= 2
    return b


def _scan_rev(c, v):
    s = c.shape[0]
    k = 1
    while k < s:
        v = c * _shift_up(v, k) + v
        if 2 * k < s:
            c = c * _shift_up(c, k, 1.0)
        k *= 2
    return v


def _mm(a, w, *, name, trans_w=False, bias=None, resid=None, resid_scale=1.0):
    m, k = a.shape
    n = w.shape[0] if trans_w else w.shape[1]
    tm = _tile(m, min(1024, max(256, MM_LHS_ELEMS // k)), SUBLANES)
    tn = _tile(n, MM_TN, LANES)
    has_bias = bias is not None
    has_resid = resid is not None

    def body(*refs):
        a_ref, w_ref = refs[0], refs[1]
        pos = 2
        b_ref = r_ref = None
        if has_bias:
            b_ref = refs[pos]
            pos += 1
        if has_resid:
            r_ref = refs[pos]
            pos += 1
        o_ref = refs[pos]

        cols = pl.ds(pl.multiple_of(pl.program_id(1) * tn, LANES), tn)
        if trans_w:
            acc = lax.dot_general(a_ref[...], w_ref[cols, :], (((1,), (1,)), ((), ())), preferred_element_type=F32)
        else:
            acc = jnp.dot(a_ref[...], w_ref[:, cols], preferred_element_type=F32)
        if has_bias:
            acc = acc + b_ref[...]
        if has_resid:
            acc = acc + resid_scale * r_ref[...]
        o_ref[...] = acc

    in_specs = [pl.BlockSpec((tm, k), lambda i, j: (i, 0)),
                pl.BlockSpec(w.shape, lambda i, j: (0, 0), pipeline_mode=pl.Buffered(1))]
    args = [a, w]
    if has_bias:
        in_specs.append(pl.BlockSpec((1, tn), lambda i, j: (0, j)))
        args.append(bias)
    if has_resid:
        in_specs.append(pl.BlockSpec((tm, tn), lambda i, j: (i, j)))
        args.append(resid)
    return pl.pallas_call(
        body, name=name, grid=(m // tm, n // tn), in_specs=in_specs,
        out_specs=pl.BlockSpec((tm, tn), lambda i, j: (i, j)),
        out_shape=jax.ShapeDtypeStruct((m, n), F32),
        compiler_params=_params(("parallel", "arbitrary")),
    )(*args)


def _mm_ln(a, w, resid, alpha, g, b, *, name, tm=512):
    m, k = a.shape
    d = w.shape[1]
    tm = _tile(m, tm, SUBLANES)

    def body(a_ref, w_ref, r_ref, g_ref, b_ref, z_ref, o_ref, obf_ref):
        y = jnp.dot(a_ref[...], w_ref[...], preferred_element_type=F32)
        z = alpha * r_ref[...] + y
        z_ref[...] = z
        mu = jnp.mean(z, axis=-1, keepdims=True)
        zc = z - mu
        var = jnp.mean(zc * zc, axis=-1, keepdims=True)
        o = zc * lax.rsqrt(var + LN_EPS) * g_ref[...] + b_ref[...]
        o_ref[...] = o
        obf_ref[...] = o.astype(BF16)

    row = pl.BlockSpec((tm, d), lambda i: (i, 0))
    vec = pl.BlockSpec((1, d), lambda i: (0, 0))
    return pl.pallas_call(
        body, name=name, grid=(m // tm,),
        in_specs=[pl.BlockSpec((tm, k), lambda i: (i, 0)),
                  pl.BlockSpec((k, d), lambda i: (0, 0), pipeline_mode=pl.Buffered(1)), row, vec, vec],
        out_specs=[row, row, row],
        out_shape=[jax.ShapeDtypeStruct((m, d), F32), jax.ShapeDtypeStruct((m, d), F32),
                   jax.ShapeDtypeStruct((m, d), BF16)],
        compiler_params=_params(("parallel",)),
    )(a, w, resid, g, b)


def _ln_bwd_math(do, z, g):
    mu = jnp.mean(z, axis=-1, keepdims=True)
    zc = z - mu
    var = jnp.mean(zc * zc, axis=-1, keepdims=True)
    rstd = lax.rsqrt(var + LN_EPS)
    xhat = zc * rstd
    dxh = do * g
    m1 = jnp.mean(dxh, axis=-1, keepdims=True)
    m2 = jnp.mean(dxh * xhat, axis=-1, keepdims=True)
    return rstd * (dxh - m1 - xhat * m2), _colsum(do * xhat), _colsum(do)


def _mm_ln_bwd(a, w, resid, resid_scale, z, g, *, name):
    t, k = a.shape
    d = w.shape[0]
    tm = _tile(t, min(512, max(256, MM_LHS_ELEMS // k)), SUBLANES)

    def body(a_ref, w_ref, r_ref, z_ref, g_ref, dz_ref, dzbf_ref, dg_ref, db_ref):
        @pl.when(pl.program_id(0) == 0)
        def _():
            dg_ref[...] = jnp.zeros_like(dg_ref)
            db_ref[...] = jnp.zeros_like(db_ref)

        dx = lax.dot_general(a_ref[...], w_ref[...], (((1,), (1,)), ((), ())), preferred_element_type=F32)
        dx = dx + resid_scale * r_ref[...]
        dz, dg, db = _ln_bwd_math(dx, z_ref[...], g_ref[...])
        dz_ref[...] = dz
        dzbf_ref[...] = dz.astype(BF16)
        dg_ref[...] += dg
        db_ref[...] += db

    row = pl.BlockSpec((tm, d), lambda i: (i, 0))
    vec = pl.BlockSpec((1, d), lambda i: (0, 0))
    return pl.pallas_call(
        body, name=name, grid=(t // tm,),
        in_specs=[pl.BlockSpec((tm, k), lambda i: (i, 0)),
                  pl.BlockSpec((d, k), lambda i: (0, 0), pipeline_mode=pl.Buffered(1)), row, row, vec],
        out_specs=[row, row, vec, vec],
        out_shape=[jax.ShapeDtypeStruct((t, d), F32), jax.ShapeDtypeStruct((t, d), BF16),
                   jax.ShapeDtypeStruct((1, d), F32), jax.ShapeDtypeStruct((1, d), F32)],
        compiler_params=_params(("arbitrary",)),
    )(a, w, resid, z, g)


def _mm_tn(a, b, *, name, tm=1408, tn=1536, tk=1024):
    t, m = a.shape
    n = b.shape[1]
    tm = _tile(m, tm, LANES)
    tn = _tile(n, tn, LANES)
    tk = _tile(t, tk, SUBLANES)

    def body(a_ref, b_ref, o_ref):
        @pl.when(pl.program_id(2) == 0)
        def _():
            o_ref[...] = jnp.zeros_like(o_ref)

        o_ref[...] += lax.dot_general(a_ref[...], b_ref[...], (((0,), (0,)), ((), ())), preferred_element_type=F32)

    return pl.pallas_call(
        body, name=name, grid=(m // tm, n // tn, t // tk),
        in_specs=[pl.BlockSpec((tk, tm), lambda i, j, l: (l, i)), pl.BlockSpec((tk, tn), lambda i, j, l: (l, j))],
        out_specs=pl.BlockSpec((tm, tn), lambda i, j, l: (i, j)),
        out_shape=jax.ShapeDtypeStruct((m, n), F32),
        compiler_params=_params(("parallel", "parallel", "arbitrary")),
    )(a, b)


def _ln_bwd(dout, z, g, *, name, tm=512):
    t, d = z.shape
    tm = _tile(t, tm, SUBLANES)

    def body(do_ref, z_ref, g_ref, dz_ref, dzbf_ref, dg_ref, db_ref):
        @pl.when(pl.program_id(0) == 0)
        def _():
            dg_ref[...] = jnp.zeros_like(dg_ref)
            db_ref[...] = jnp.zeros_like(db_ref)

        dz, dg, db = _ln_bwd_math(do_ref[...], z_ref[...], g_ref[...])
        dz_ref[...] = dz
        dzbf_ref[...] = dz.astype(BF16)
        dg_ref[...] += dg
        db_ref[...] += db

    row = pl.BlockSpec((tm, d), lambda i: (i, 0))
    vec = pl.BlockSpec((1, d), lambda i: (0, 0))
    return pl.pallas_call(
        body, name=name, grid=(t // tm,), in_specs=[row, row, vec], out_specs=[row, row, vec, vec],
        out_shape=[jax.ShapeDtypeStruct((t, d), F32), jax.ShapeDtypeStruct((t, d), BF16),
                   jax.ShapeDtypeStruct((1, d), F32), jax.ShapeDtypeStruct((1, d), F32)],
        compiler_params=_params(("arbitrary",)),
    )(dout, z, g)


def _loss_head(y, target, *, name, tm=512):
    t, d = y.shape
    tm = _tile(t, tm, SUBLANES)

    def body(y_ref, t_ref, s_ref, dy_ref):
        @pl.when(pl.program_id(0) == 0)
        def _():
            s_ref[...] = jnp.zeros_like(s_ref)

        e = y_ref[...] - t_ref[...]
        dy_ref[...] = e * (1.0 / d)
        s_ref[...] += jnp.sum(_colsum(e * e), axis=-1, keepdims=True)

    row = pl.BlockSpec((tm, d), lambda i: (i, 0))
    return pl.pallas_call(
        body, name=name, grid=(t // tm,), in_specs=[row, row],
        out_specs=[pl.BlockSpec((1, LANES), lambda i: (0, 0)), row],
        out_shape=[jax.ShapeDtypeStruct((1, LANES), F32), jax.ShapeDtypeStruct((t, d), F32)],
        compiler_params=_params(("arbitrary",)),
    )(y, target)


def _own(c, b, *_):
    return c, b


def _ahead(nc, bsz):
    def at(c, b, part):
        b2 = b + jnp.minimum(part, 1)
        return jnp.minimum(c + b2 // bsz, nc - 1), b2 % bsz
    return at


def _strip(s, tc, off, at=_own):
    def index(*ids):
        c, b = at(*ids)
        return b, 0, off + c
    return pl.BlockSpec((None, s, tc), index)


def _cvec(kw, tc, off, at=_own):
    def index(*ids):
        return 0, off + at(*ids)[0]
    return pl.BlockSpec((kw, tc), index)


def _acc(kw, tc):
    return pl.BlockSpec((kw, tc), lambda c, b, *_: (0, c))


def _sc_fwd(h, cw, cb, *, name, tc=256):
    bsz, s, d3 = h.shape
    d = d3 // 3
    tc = _tile(d, tc, LANES)
    nc = d // tc

    def body(gb_ref, gc_ref, v_ref, w_ref, b_ref, q_ref):
        u = _conv_fwd(gc_ref[...] * v_ref[...], w_ref[...], b_ref[...])
        q_ref[...] = (gb_ref[...] * u).astype(BF16)

    return pl.pallas_call(
        body, name=name, grid=(nc, bsz),
        in_specs=[_strip(s, tc, 0), _strip(s, tc, nc), _strip(s, tc, 2 * nc), _cvec(cw.shape[0], tc, 0), _cvec(1, tc, 0)],
        out_specs=_strip(s, tc, 0),
        out_shape=jax.ShapeDtypeStruct((bsz, s, d), BF16),
        compiler_params=_params(("parallel", "parallel")),
    )(h, h, h, cw, cb)


def _sc_bwd(h, dq, cw, cb, *, name, tc=256):
    bsz, s, d3 = h.shape
    d = d3 // 3
    kw = cw.shape[0]
    tc = _tile(d, tc, LANES)
    nc = d // tc

    def body(gb_ref, gc_ref, v_ref, dq_ref, w_ref, b_ref, dh_ref, dw_ref, db_ref, parts):
        b_id, part = pl.program_id(1), pl.program_id(2)

        @pl.when(part == 0)
        def _():
            gb, gc, v, dq_, w = gb_ref[...], gc_ref[...], v_ref[...], dq_ref[...], w_ref[...]
            p = gc * v
            u = _conv_fwd(p, w, b_ref[...])
            du = dq_ * gb
            dp = _conv_bwd_x(du, w)
            parts[0] = (dq_ * u).astype(BF16)
            parts[1] = (dp * v).astype(BF16)
            parts[2] = (dp * gc).astype(BF16)
            _accumulate(b_id == 0, [(dw_ref, _conv_bwd_w(du, p, kw)), (db_ref, _colsum(du))])

        dh_ref[...] = parts[part]

    at = _ahead(nc, bsz)
    return pl.pallas_call(
        body, name=name, grid=(nc, bsz, 3),
        in_specs=[_strip(s, tc, 0, at), _strip(s, tc, nc, at), _strip(s, tc, 2 * nc, at), _strip(s, tc, 0, at),
                  _cvec(kw, tc, 0, at), _cvec(1, tc, 0, at)],
        out_specs=[pl.BlockSpec((None, s, tc), lambda c, b, p: (b, 0, p * nc + c)), _acc(kw, tc), _acc(1, tc)],
        out_shape=[jax.ShapeDtypeStruct((bsz, s, d3), BF16), jax.ShapeDtypeStruct((kw, d), F32),
                   jax.ShapeDtypeStruct((1, d), F32)],
        scratch_shapes=[pltpu.VMEM((3, s, tc), BF16)],
        compiler_params=_params(("parallel", "arbitrary", "arbitrary")),
    )(h, h, h, dq, cw, cb)


def _ffn_fwd(h, cw, cb, *, name, tc=256):
    bsz, s, f2 = h.shape
    f = f2 // 2
    tc = _tile(f, tc, LANES)
    nc = f // tc

    def body(hg_ref, hv_ref, wg_ref, wv_ref, bg_ref, bv_ref, a_ref):
        g = _conv_fwd(hg_ref[...], wg_ref[...], bg_ref[...])
        v = _conv_fwd(hv_ref[...], wv_ref[...], bv_ref[...])
        a_ref[...] = (g * _sigmoid(g) * v).astype(BF16)

    kw = cw.shape[0]
    return pl.pallas_call(
        body, name=name, grid=(nc, bsz),
        in_specs=[_strip(s, tc, 0), _strip(s, tc, nc), _cvec(kw, tc, 0), _cvec(kw, tc, nc), _cvec(1, tc, 0), _cvec(1, tc, nc)],
        out_specs=_strip(s, tc, 0),
        out_shape=jax.ShapeDtypeStruct((bsz, s, f), BF16),
        compiler_params=_params(("parallel", "parallel")),
    )(h, h, cw, cw, cb, cb)


def _ffn_bwd(h, da, cw, cb, *, name, tc=256):
    bsz, s, f2 = h.shape
    f = f2 // 2
    kw = cw.shape[0]
    tc = _tile(f, tc, LANES)
    nc = f // tc

    def body(hg_ref, hv_ref, da_ref, wg_ref, wv_ref, bg_ref, bv_ref,
             dh_ref, dwg_ref, dwv_ref, dbg_ref, dbv_ref, parts):
        b_id, part = pl.program_id(1), pl.program_id(2)

        @pl.when(part == 0)
        def _():
            hg, hv, da_ = hg_ref[...], hv_ref[...], da_ref[...]
            wg, wv = wg_ref[...], wv_ref[...]
            g = _conv_fwd(hg, wg, bg_ref[...])
            v = _conv_fwd(hv, wv, bv_ref[...])
            sg = _sigmoid(g)
            dv = da_ * (g * sg)
            dg = da_ * v * (sg * (1.0 + g * (1.0 - sg)))
            parts[0] = _conv_bwd_x(dg, wg).astype(BF16)
            parts[1] = _conv_bwd_x(dv, wv).astype(BF16)
            _accumulate(b_id == 0, [(dwg_ref, _conv_bwd_w(dg, hg, kw)), (dwv_ref, _conv_bwd_w(dv, hv, kw)),
                                    (dbg_ref, _colsum(dg)), (dbv_ref, _colsum(dv))])

        dh_ref[...] = parts[part]

    at = _ahead(nc, bsz)
    return pl.pallas_call(
        body, name=name, grid=(nc, bsz, 2),
        in_specs=[_strip(s, tc, 0, at), _strip(s, tc, nc, at), _strip(s, tc, 0, at), _cvec(kw, tc, 0, at),
                  _cvec(kw, tc, nc, at), _cvec(1, tc, 0, at), _cvec(1, tc, nc, at)],
        out_specs=[pl.BlockSpec((None, s, tc), lambda c, b, p: (b, 0, p * nc + c)),
                   _acc(kw, tc), _acc(kw, tc), _acc(1, tc), _acc(1, tc)],
        out_shape=[jax.ShapeDtypeStruct((bsz, s, f2), BF16), jax.ShapeDtypeStruct((kw, f), F32),
                   jax.ShapeDtypeStruct((kw, f), F32), jax.ShapeDtypeStruct((1, f), F32), jax.ShapeDtypeStruct((1, f), F32)],
        scratch_shapes=[pltpu.VMEM((2, s, tc), BF16)],
        compiler_params=_params(("parallel", "arbitrary", "arbitrary")),
    )(h, h, da, cw, cw, cb, cb)


def _lru_gates(r, cw, cb, wg, bg, lam):
    blk = r.shape[1]
    xr = _conv_fwd(r, cw, cb)
    gates = jnp.dot(xr.astype(BF16), wg, preferred_element_type=F32) + bg
    rg = _sigmoid(gates[:, :blk])
    ig = _sigmoid(gates[:, blk:])
    sp = _softplus(-lam)
    la = (-LRU_C * sp) * rg
    a = jnp.exp(la)
    mult = jnp.sqrt(-_expm1(2.0 * la, a * a))
    return xr, rg, ig, sp, a, mult


def _lru_specs(s, blk, heads, kw, at=_own):
    def per_head(*ids):
        return at(*ids)[0], 0, 0
    return [_strip(s, blk, 0, at), _strip(s, blk, heads, at), _cvec(kw, blk, 0, at), _cvec(1, blk, 0, at),
            pl.BlockSpec((None, blk, 2 * blk), per_head), pl.BlockSpec((None, 1, 2 * blk), per_head),
            _cvec(1, blk, 0, at)]


def _lru_fwd(h, cw, cb, wg, bg, lam, *, name):
    bsz, s, r2 = h.shape
    heads, blk = wg.shape[0], wg.shape[1]
    kw = cw.shape[0]

    def body(g_ref, r_ref, cw_ref, cb_ref, wg_ref, bg_ref, lam_ref, y_ref, hs_ref):
        xr, _, ig, _, a, mult = _lru_gates(r_ref[...], cw_ref[...], cb_ref[...], wg_ref[...], bg_ref[...], lam_ref[...])
        hs = _scan_fwd(a, mult * (ig * xr))
        hs_ref[...] = hs
        y_ref[...] = (hs * _gelu(g_ref[...])).astype(BF16)

    out = pl.BlockSpec((None, s, blk), lambda hd, b: (b, 0, hd))
    return pl.pallas_call(
        body, name=name, grid=(heads, bsz), in_specs=_lru_specs(s, blk, heads, kw), out_specs=[out, out],
        out_shape=[jax.ShapeDtypeStruct((bsz, s, r2 // 2), BF16), jax.ShapeDtypeStruct((bsz, s, r2 // 2), F32)],
        compiler_params=_params(("parallel", "parallel")),
    )(h, h, cw, cb, wg, bg, lam)


def _lru_bwd(h, hs, dy, cw, cb, wg, bg, lam, *, name):
    bsz, s, r2 = h.shape
    rw = r2 // 2
    heads, blk = wg.shape[0], wg.shape[1]
    kw = cw.shape[0]

    def body(g_ref, r_ref, cw_ref, cb_ref, wg_ref, bg_ref, lam_ref, hs_ref, dy_ref,
             dh_ref, dcw_ref, dcb_ref, dwg_ref, dbg_ref, dlam_ref, sg_ref, sr_ref, parts):
        b_id, part = pl.program_id(1), pl.program_id(2)

        @pl.when(part == 0)
        def _():
            r, cw_, wg_, lam_ = r_ref[...], cw_ref[...], wg_ref[...], lam_ref[...]
            xr, rg, ig, sp, a, mult = _lru_gates(r, cw_, cb_ref[...], wg_, bg_ref[...], lam_)
            hs_, dy_ = hs_ref[...], dy_ref[...]
            gel, dgel = _gelu_and_grad(g_ref[...])
            dg = dy_ * hs_ * dgel
            lmb = _scan_rev(_shift_up(a, 1, 1.0), dy_ * gel)
            da = lmb * _shift_dn(hs_, 1)
            dmult = lmb * (ig * xr)
            dig = lmb * (mult * xr)
            dxr = lmb * (mult * ig)
            dla = da * a - dmult * (a * a / mult)
            drg = dla * (-LRU_C * sp)
            dsp = _colsum(dla * rg) * (-LRU_C)
            dlam = -dsp * _sigmoid(-lam_)
            dgates = jnp.concatenate([drg * (rg * (1.0 - rg)), dig * (ig * (1.0 - ig))], axis=1)
            dgates_bf = dgates.astype(BF16)
            dwg = lax.dot_general(xr.astype(BF16), dgates_bf, (((0,), (0,)), ((), ())), preferred_element_type=F32)
            dxr = dxr + lax.dot_general(dgates_bf, wg_, (((1,), (1,)), ((), ())), preferred_element_type=F32)
            dr = _conv_bwd_x(dxr, cw_)
            parts[0] = dg.astype(BF16)
            parts[1] = dr.astype(BF16)
            _accumulate(b_id == 0, [(dcw_ref, _conv_bwd_w(dxr, r, kw)), (dcb_ref, _colsum(dxr)), (dwg_ref, dwg),
                                    (dbg_ref, _colsum(dgates)), (dlam_ref, dlam), (sg_ref, _colsum(dg)),
                                    (sr_ref, _colsum(dr))])

        dh_ref[...] = parts[part]

    at = _ahead(heads, bsz)
    strip = _strip(s, blk, 0, at)
    vec = pl.BlockSpec((1, blk), lambda hd, b, p: (0, hd))
    return pl.pallas_call(
        body, name=name, grid=(heads, bsz, 2),
        in_specs=_lru_specs(s, blk, heads, kw, at) + [strip, strip],
        out_specs=[pl.BlockSpec((None, s, blk), lambda hd, b, p: (b, 0, p * heads + hd)),
                   pl.BlockSpec((kw, blk), lambda hd, b, p: (0, hd)), vec,
                   pl.BlockSpec((None, blk, 2 * blk), lambda hd, b, p: (hd, 0, 0)),
                   pl.BlockSpec((None, 1, 2 * blk), lambda hd, b, p: (hd, 0, 0)), vec, vec, vec],
        out_shape=[jax.ShapeDtypeStruct((bsz, s, r2), BF16), jax.ShapeDtypeStruct((kw, rw), F32),
                   jax.ShapeDtypeStruct((1, rw), F32), jax.ShapeDtypeStruct((heads, blk, 2 * blk), F32),
                   jax.ShapeDtypeStruct((heads, 1, 2 * blk), F32), jax.ShapeDtypeStruct((1, rw), F32),
                   jax.ShapeDtypeStruct((1, rw), F32), jax.ShapeDtypeStruct((1, rw), F32)],
        scratch_shapes=[pltpu.VMEM((2, s, blk), BF16)],
        compiler_params=_params(("parallel", "arbitrary", "arbitrary")),
    )(h, h, cw, cb, wg, bg, lam, hs, dy)


HBM_SPEC = pl.BlockSpec(memory_space=pltpu.HBM)
SEM_SPEC = pl.BlockSpec(memory_space=pltpu.SEMAPHORE)
EFFECT = pltpu.SideEffectType.DATAFLOW_SIDE_EFFECTING


def _peer_copies(srcs, lands, gather, send_sem, recv_sem):
    x, y, c = (lax.axis_index(ax) for ax in MESH_AXES)
    me = 4 * x + 2 * y + c
    copies = []
    for i in range(len(srcs)):
        for d in range(1, N_DEV):
            px = 1 - x if d & 4 else x
            py = 1 - y if d & 2 else y
            pc = 1 - c if d & 1 else c
            src = srcs[i] if gather[i] else srcs[i].at[4 * px + 2 * py + pc]
            k = i * (N_DEV - 1) + d - 1
            copies.append(pltpu.make_async_remote_copy(
                src_ref=src, dst_ref=lands[i].at[me], send_sem=send_sem.at[k], recv_sem=recv_sem.at[k],
                device_id=(px, py, pc), device_id_type=pl.DeviceIdType.MESH))
    return copies


def _exchange_start(arrs, gather, *, name):
    n = len(arrs)
    lands = [lax.empty((N_DEV,) + tuple(a.shape if g else a.shape[1:]), a.dtype) for a, g in zip(arrs, gather)]

    def body(*refs):
        srcs, land_refs = refs[:n], refs[n:2 * n]
        send_sem, recv_sem = refs[2 * n], refs[2 * n + 1]
        token = refs[-1]
        for cp in _peer_copies(srcs, land_refs, gather, send_sem, recv_sem):
            cp.start()
        token[...] = jnp.zeros_like(token)

    sems = pltpu.SemaphoreType.DMA((n * (N_DEV - 1),))
    thru = [pltpu.HBM(a.shape, a.dtype) for a in arrs + lands]
    out = pl.pallas_call(
        body, name=name, in_specs=[HBM_SPEC] * (2 * n),
        out_shape=(sems, sems, *thru, jax.ShapeDtypeStruct((SUBLANES, LANES), F32)),
        out_specs=(SEM_SPEC, SEM_SPEC, *([HBM_SPEC] * (2 * n)), pl.BlockSpec(memory_space=pltpu.VMEM)),
        input_output_aliases={i: 2 + i for i in range(2 * n)},
        compiler_params=pltpu.CompilerParams(has_side_effects=EFFECT),
    )(*[pltpu.with_memory_space_constraint(a, pltpu.HBM) for a in arrs + lands])
    return {"send_sem": out[0], "recv_sem": out[1], "srcs": list(out[2:2 + n]), "lands": list(out[2 + n:2 + 2 * n]),
            "token": out[-1], "gather": list(gather)}


def _exchange_wait(handle, after, *, name):
    srcs, lands, gather = handle["srcs"], handle["lands"], handle["gather"]
    n = len(srcs)

    def body(*refs):
        src_refs, land_refs = refs[:n], refs[n:2 * n]
        send_sem, recv_sem = refs[2 * n], refs[2 * n + 1]
        for cp in _peer_copies(src_refs, land_refs, gather, send_sem, recv_sem):
            cp.wait_send()
            cp.wait_recv()

    out = pl.pallas_call(
        body, name=name,
        in_specs=[HBM_SPEC] * (2 * n) + [SEM_SPEC, SEM_SPEC, pl.BlockSpec(memory_space=pl.ANY)],
        out_shape=tuple(pltpu.HBM(a.shape, a.dtype) for a in srcs + lands), out_specs=tuple([HBM_SPEC] * (2 * n)),
        input_output_aliases={i: i for i in range(2 * n)},
        compiler_params=pltpu.CompilerParams(has_side_effects=EFFECT),
    )(*srcs, *lands, handle["send_sem"], handle["recv_sem"], after)
    return list(out[:n]), list(out[n:])


def _adamw(parts, w, m, v, *, name, tr=256):
    r, c = w.shape
    tr = _tile(r, tr, SUBLANES)
    bc1 = 1.0 / (1.0 - ADAM_B1 ** ADAM_STEP)
    bc2 = 1.0 / (1.0 - ADAM_B2 ** ADAM_STEP)

    def body(p_ref, w_ref, m_ref, v_ref, g_ref, d_ref, mo_ref, vo_ref):
        g = p_ref[0].astype(F32)
        for s in range(1, N_DEV):
            g = g + p_ref[s].astype(F32)
        m_new = ADAM_B1 * m_ref[...] + (1.0 - ADAM_B1) * g
        v_new = ADAM_B2 * v_ref[...] + (1.0 - ADAM_B2) * (g * g)
        g_ref[...] = g
        mo_ref[...] = m_new
        vo_ref[...] = v_new
        d_ref[...] = -ADAM_LR * ((m_new * bc1) / (jnp.sqrt(v_new * bc2) + ADAM_EPS) + ADAM_WD * w_ref[...])

    blk = pl.BlockSpec((tr, c), lambda i: (i, 0))
    return pl.pallas_call(
        body, name=name, grid=(r // tr,),
        in_specs=[pl.BlockSpec((N_DEV, tr, c), lambda i: (0, i, 0)), blk, blk, blk],
        out_specs=[blk] * 4, out_shape=[jax.ShapeDtypeStruct((r, c), F32)] * 4,
        compiler_params=_params(("parallel",)),
    )(parts, w, m, v)


def _whole(slabs, axis):
    x = jnp.moveaxis(slabs, 0, axis)
    shp = x.shape
    return x.reshape(shp[:axis] + (shp[axis] * shp[axis + 1],) + shp[axis + 2:])


def _slabs(whole, axis):
    shp = whole.shape
    x = whole.reshape(shp[:axis] + (N_DEV, shp[axis] // N_DEV) + shp[axis + 1:])
    return jnp.moveaxis(x, axis, 0)


def _pack(vecs, rows):
    flat = jnp.concatenate(vecs, axis=-1)
    pad = rows * LANES - flat.shape[-1]
    flat = jnp.pad(flat, [(0, 0)] * (flat.ndim - 1) + [(0, pad)])
    return flat.reshape(flat.shape[:-1] + (rows, LANES))


def _unpack(packed, sizes):
    flat = packed.reshape(packed.shape[:-2] + (-1,))
    out, pos = [], 0
    for n in sizes:
        out.append(flat[..., pos:pos + n])
        pos += n
    return out


def _pack_rows(sizes):
    total = sum(sizes)
    return -(-total // (LANES * SUBLANES)) * SUBLANES


BIG = {"sc_w_in": 2, "sc_w_out": 1, "lru_w_in": 2, "lru_w_gate": 3, "lru_w_out": 1, "ffn_w_up": 2, "ffn_w_down": 1}
SMALL = ["sc_conv_w", "lru_b_in", "lru_conv_w", "lru_conv_b", "lru_b_gate", "lru_lambda", "ffn_conv_w", "ln_g", "ln_b"]
REPL = ["sc_conv_b", "ffn_conv_b"]
WEIGHTS = ["sc_w_in", "sc_conv_w", "sc_conv_b", "sc_w_out", "lru_w_in", "lru_b_in", "lru_conv_w", "lru_conv_b",
           "lru_w_gate", "lru_b_gate", "lru_lambda", "lru_w_out", "ffn_w_up", "ffn_conv_w", "ffn_conv_b", "ffn_w_down",
           "ln_g", "ln_b"]


def _stage_big(g):
    i, j = g // 2, g // 4
    if g % 2:
        return [("ffn_w_up", i), ("ffn_w_down", i)]
    return [("sc_w_in", j), ("sc_w_out", j)] if i % 2 == 0 else [("lru_w_in", j), ("lru_w_gate", j), ("lru_w_out", j)]


def _step(x, loss_target, w, m, v):
    bsz, s, d = x.shape
    t = bsz * s
    depth = w["ffn_w_up"].shape[0]
    alpha = (2.0 * depth) ** 0.25
    heads = w["lru_w_gate"].shape[1]

    small_sizes = [w[k].size for k in SMALL]
    small_rows = _pack_rows(small_sizes)
    small_local = _pack([w[k].reshape(1, -1) for k in SMALL], small_rows)[0]
    me = 4 * lax.axis_index("x") + 2 * lax.axis_index("y") + lax.axis_index("c")

    def with_own(land, own):
        return lax.dynamic_update_slice_in_dim(land, own, me, axis=0)

    stages = 2 * depth
    gathers, tok = [], None
    for g in range(stages):
        arrs = [w[k][l].astype(BF16) for k, l in _stage_big(g)]
        if g == 0:
            arrs.append(small_local)
        if tok is not None:
            arrs[0] = arrs[0] + tok.astype(BF16)
        gathers.append(_exchange_start(arrs, [True] * len(arrs), name=f"gather_start_{g}"))
        tok = gathers[-1]["token"][0, 0]
    full = {k: [None] * w[k].shape[0] for k in BIG}
    full["sc_conv_b"] = w["sc_conv_b"]
    full["ffn_conv_b"] = w["ffn_conv_b"]

    def arrive(g, after):
        srcs, lands = _exchange_wait(gathers[g], after, name=f"gather_wait_{g}")
        for (k, l), src, land in zip(_stage_big(g), srcs, lands):
            full[k][l] = _whole(with_own(land, src[None]), BIG[k] - 1)
        if g == 0:
            for k, seg in zip(SMALL, _unpack(with_own(lands[-1], srcs[-1][None]), small_sizes)):
                full[k] = _whole(seg.reshape((N_DEV,) + w[k].shape), w[k].ndim - 1)

    xt = x.reshape(t, d)
    xb = xt.astype(BF16)
    saved = []
    for i in range(depth):
        j = i // 2
        arrive(2 * i, gathers[-1]["token"] if i == 0 else xb)
        lng, lnb = full["ln_g"][i], full["ln_b"][i]
        sv = {"x0": xb}
        if i % 2 == 0:
            hm = _mm(xb, full["sc_w_in"][j], name="sc_in")
            q = _sc_fwd(hm.reshape(bsz, s, -1), full["sc_conv_w"][j], full["sc_conv_b"][j:j + 1], name="sc_mix")
            w_out = full["sc_w_out"][j]
        else:
            hm = _mm(xb, full["lru_w_in"][j], bias=full["lru_b_in"][j:j + 1], name="lru_in")
            q, hs = _lru_fwd(hm.reshape(bsz, s, -1), full["lru_conv_w"][j], full["lru_conv_b"][j:j + 1],
                             full["lru_w_gate"][j], full["lru_b_gate"][j].reshape(heads, 1, -1),
                             full["lru_lambda"][j:j + 1], name="lru_mix")
            sv["hs"] = hs
            w_out = full["lru_w_out"][j]
        q = q.reshape(t, -1)
        arrive(2 * i + 1, q)
        z1, x1, x1b = _mm_ln(q, w_out, xt, alpha, lng[0:1], lnb[0:1], name="mix_out_ln")
        hf = _mm(x1b, full["ffn_w_up"][i], name="ffn_up")
        a = _ffn_fwd(hf.reshape(bsz, s, -1), full["ffn_conv_w"][i], full["ffn_conv_b"][i:i + 1], name="ffn_act")
        a = a.reshape(t, -1)
        z2, xt, xb = _mm_ln(a, full["ffn_w_down"][i], x1, alpha, lng[1:2], lnb[1:2], name="ffn_down_ln")
        sv.update(hm=hm, q=q, z1=z1, x1=x1b, hf=hf, a=a, z2=z2)
        saved.append(sv)

    sq, dx = _loss_head(xt, loss_target.reshape(t, d), name="loss_head")
    loss = lax.psum((0.5 / d) * sq[0, 0], MESH_AXES)

    grads = {k: [None] * w[k].shape[0] for k in WEIGHTS}
    scatters = [None] * stages

    def depart(g):
        send = [_slabs(grads[k][l], BIG[k] - 1).astype(BF16) for k, l in _stage_big(g)]
        scatters[g] = _exchange_start(send, [False] * len(send), name=f"scatter_start_{g}")
        return scatters[g]["token"][0:1, 0:1]

    dz2, dz2b, dg2, db2 = _ln_bwd(dx, saved[-1]["z2"], full["ln_g"][-1][1:2], name="ln_bwd")
    for i in reversed(range(depth)):
        j = i // 2
        sv = saved[i]
        lng = full["ln_g"][i]
        da = _mm(dz2b, full["ffn_w_down"][i], trans_w=True, name="ffn_down_dx")
        grads["ffn_w_down"][i] = _mm_tn(sv["a"], dz2b, name="ffn_down_dw")
        dhf, dwg, dwv, dbg, dbv = _ffn_bwd(sv["hf"].reshape(bsz, s, -1), da.reshape(bsz, s, -1), full["ffn_conv_w"][i],
                                           full["ffn_conv_b"][i:i + 1], name="ffn_act_bwd")
        dhf = dhf.reshape(t, -1)
        grads["ffn_conv_w"][i] = jnp.concatenate([dwg, dwv], axis=1)
        grads["ffn_conv_b"][i] = jnp.concatenate([dbg, dbv], axis=1)[0]
        grads["ffn_w_up"][i] = _mm_tn(sv["x1"], dhf, name="ffn_up_dw")
        dz1, dz1b, dg1, db1 = _mm_ln_bwd(dhf, full["ffn_w_up"][i], dz2, alpha, sv["z1"], lng[0:1] + depart(2 * i + 1),
                                         name="ffn_up_dx_ln")
        grads["ln_g"][i] = jnp.concatenate([dg1, dg2], axis=0)
        grads["ln_b"][i] = jnp.concatenate([db1, db2], axis=0)
        if i % 2 == 0:
            dq = _mm(dz1b, full["sc_w_out"][j], trans_w=True, name="sc_out_dx")
            grads["sc_w_out"][j] = _mm_tn(sv["q"], dz1b, name="sc_out_dw")
            dhm, dcw, dcb = _sc_bwd(sv["hm"].reshape(bsz, s, -1), dq.reshape(bsz, s, -1), full["sc_conv_w"][j],
                                    full["sc_conv_b"][j:j + 1], name="sc_mix_bwd")
            dhm = dhm.reshape(t, -1)
            grads["sc_conv_w"][j] = dcw
            grads["sc_conv_b"][j] = dcb[0]
            grads["sc_w_in"][j] = _mm_tn(sv["x0"], dhm, name="sc_in_dw")
            w_in = full["sc_w_in"][j]
        else:
            dq = _mm(dz1b, full["lru_w_out"][j], trans_w=True, name="lru_out_dx")
            grads["lru_w_out"][j] = _mm_tn(sv["q"], dz1b, name="lru_out_dw")
            dhm, dcw, dcb, dwgt, dbgt, dlam, sgb, srb = _lru_bwd(
                sv["hm"].reshape(bsz, s, -1), sv["hs"], dq.reshape(bsz, s, -1), full["lru_conv_w"][j],
                full["lru_conv_b"][j:j + 1], full["lru_w_gate"][j], full["lru_b_gate"][j].reshape(heads, 1, -1),
                full["lru_lambda"][j:j + 1], name="lru_mix_bwd")
            dhm = dhm.reshape(t, -1)
            grads["lru_conv_w"][j] = dcw
            grads["lru_conv_b"][j] = dcb[0]
            grads["lru_w_gate"][j] = dwgt
            grads["lru_b_gate"][j] = dbgt[:, 0, :]
            grads["lru_lambda"][j] = dlam[0]
            grads["lru_b_in"][j] = jnp.concatenate([sgb, srb], axis=1)[0]
            grads["lru_w_in"][j] = _mm_tn(sv["x0"], dhm, name="lru_in_dw")
            w_in = full["lru_w_in"][j]
        tok = depart(2 * i)
        if i > 0:
            dz2, dz2b, dg2, db2 = _mm_ln_bwd(dhm, w_in, dz1, alpha, saved[i - 1]["z2"], full["ln_g"][i - 1][1:2] + tok,
                                             name="mix_in_dx_ln")
        else:
            dx = _mm(dhm, w_in + tok[0, 0].astype(BF16), trans_w=True, resid=dz1, resid_scale=alpha, name="mix_in_dx")
    grad_x = dx.reshape(bsz, s, d)

    gsm = {k: jnp.stack(grads[k]) for k in SMALL + REPL}
    small_send = _pack([_slabs(gsm[k], gsm[k].ndim - 1).reshape(N_DEV, -1) for k in SMALL], small_rows)
    repl_sizes = [w[k].size for k in REPL]
    repl_rows = _pack_rows(repl_sizes)
    repl_send = _pack([gsm[k].reshape(1, -1) for k in REPL], repl_rows)[0]
    small_scatter = _exchange_start([small_send, repl_send], [False, True], name="scatter_start_small")

    out = {}

    def update(key, parts, wk, mk, vk):
        g, dl, mn, vn = _adamw(parts, wk, mk, vk, name="adamw_" + key)
        return g, dl, mn, vn

    def own_slab(src):
        return lax.dynamic_slice_in_dim(src, me, 1, axis=0)

    per_layer = {k: [None] * w[k].shape[0] for k in BIG}
    after = dx
    for g in reversed(range(stages)):
        srcs, lands = _exchange_wait(scatters[g], after, name=f"scatter_wait_{g}")
        for (k, l), src, land in zip(_stage_big(g), srcs, lands):
            shp = w[k].shape[1:]
            c2 = shp[-1]
            res = update(f"{k}_{l}", with_own(land, own_slab(src)).reshape(N_DEV, -1, c2), w[k][l].reshape(-1, c2),
                         m[k][l].reshape(-1, c2), v[k][l].reshape(-1, c2))
            per_layer[k][l] = [r.reshape(shp) for r in res]
            after = res[-1]
    for k in BIG:
        out[k] = [jnp.stack([per_layer[k][l][r_i] for l in range(w[k].shape[0])]) for r_i in range(4)]
    srcs, lands = _exchange_wait(small_scatter, after, name="scatter_wait_small")
    got_small = with_own(lands[0], own_slab(srcs[0]))
    got_repl = with_own(lands[1], srcs[1][None])
    pk = lambda src, names, rows: _pack([src[k].reshape(1, -1) for k in names], rows)[0]
    res = update("small", got_small, small_local, pk(m, SMALL, small_rows), pk(v, SMALL, small_rows))
    for r_i, r in enumerate(res):
        for k, seg in zip(SMALL, _unpack(r, small_sizes)):
            out.setdefault(k, [None] * 4)[r_i] = seg.reshape(w[k].shape)
    res = update("repl", got_repl, pk(w, REPL, repl_rows), pk(m, REPL, repl_rows), pk(v, REPL, repl_rows))
    for r_i, r in enumerate(res):
        for k, seg in zip(REPL, _unpack(r, repl_sizes)):
            out.setdefault(k, [None] * 4)[r_i] = seg.reshape(w[k].shape)

    return (loss, grad_x, *[out[k][0] for k in WEIGHTS], *[out[k][1] for k in WEIGHTS],
            *[out[k][2] for k in WEIGHTS], *[out[k][3] for k in WEIGHTS])


def kernel(x, sc_w_in, sc_conv_w, sc_conv_b, sc_w_out, lru_w_in, lru_b_in, lru_conv_w, lru_conv_b, lru_w_gate, lru_b_gate, lru_lambda, lru_w_out, ffn_w_up, ffn_conv_w, ffn_conv_b, ffn_w_down, ln_g, ln_b, loss_target, m_sc_w_in, m_sc_conv_w, m_sc_conv_b, m_sc_w_out, m_lru_w_in, m_lru_b_in, m_lru_conv_w, m_lru_conv_b, m_lru_w_gate, m_lru_b_gate, m_lru_lambda, m_lru_w_out, m_ffn_w_up, m_ffn_conv_w, m_ffn_conv_b, m_ffn_w_down, m_ln_g, m_ln_b, v_sc_w_in, v_sc_conv_w, v_sc_conv_b, v_sc_w_out, v_lru_w_in, v_lru_b_in, v_lru_conv_w, v_lru_conv_b, v_lru_w_gate, v_lru_b_gate, v_lru_lambda, v_lru_w_out, v_ffn_w_up, v_ffn_conv_w, v_ffn_conv_b, v_ffn_w_down, v_ln_g, v_ln_b):
    w = dict(sc_w_in=sc_w_in, sc_conv_w=sc_conv_w, sc_conv_b=sc_conv_b, sc_w_out=sc_w_out, lru_w_in=lru_w_in,
             lru_b_in=lru_b_in, lru_conv_w=lru_conv_w, lru_conv_b=lru_conv_b, lru_w_gate=lru_w_gate,
             lru_b_gate=lru_b_gate, lru_lambda=lru_lambda, lru_w_out=lru_w_out, ffn_w_up=ffn_w_up,
             ffn_conv_w=ffn_conv_w, ffn_conv_b=ffn_conv_b, ffn_w_down=ffn_w_down, ln_g=ln_g, ln_b=ln_b)
    m = dict(sc_w_in=m_sc_w_in, sc_conv_w=m_sc_conv_w, sc_conv_b=m_sc_conv_b, sc_w_out=m_sc_w_out, lru_w_in=m_lru_w_in,
             lru_b_in=m_lru_b_in, lru_conv_w=m_lru_conv_w, lru_conv_b=m_lru_conv_b, lru_w_gate=m_lru_w_gate,
             lru_b_gate=m_lru_b_gate, lru_lambda=m_lru_lambda, lru_w_out=m_lru_w_out, ffn_w_up=m_ffn_w_up,
             ffn_conv_w=m_ffn_conv_w, ffn_conv_b=m_ffn_conv_b, ffn_w_down=m_ffn_w_down, ln_g=m_ln_g, ln_b=m_ln_b)
    v = dict(sc_w_in=v_sc_w_in, sc_conv_w=v_sc_conv_w, sc_conv_b=v_sc_conv_b, sc_w_out=v_sc_w_out, lru_w_in=v_lru_w_in,
             lru_b_in=v_lru_b_in, lru_conv_w=v_lru_conv_w, lru_conv_b=v_lru_conv_b, lru_w_gate=v_lru_w_gate,
             lru_b_gate=v_lru_b_gate, lru_lambda=v_lru_lambda, lru_w_out=v_lru_w_out, ffn_w_up=v_ffn_w_up,
             ffn_conv_w=v_ffn_conv_w, ffn_conv_b=v_ffn_conv_b, ffn_w_down=v_ffn_w_down, ln_g=v_ln_g, ln_b=v_ln_b)
    return _step(x, loss_target, w, m, v)
```

```python
import functools
import math

import jax
import jax.numpy as jnp
from jax import lax
from jax.experimental import pallas as pl
from jax.experimental.pallas import tpu as pltpu

F32 = jnp.float32
BF16 = jnp.bfloat16

N_DEV = 8
MESH_AXES = ("x", "y", "c")
LANES = 128
SUBLANES = 8
VMEM_LIMIT = 56 * 1024 * 1024
MM_LHS_ELEMS = 3 * 1024 * 1024
MM_TN = 1536

LRU_C = 8.0
LN_EPS = 1e-5
ADAM_LR = 0.001
ADAM_B1 = 0.9
ADAM_B2 = 0.999
ADAM_EPS = 1e-08
ADAM_WD = 0.01
ADAM_STEP = 10
GELU_K = math.sqrt(2.0 / math.pi)
GELU_C = 0.044715


def _tile(n, target, align):
    if n <= target:
        return n
    t = (target // align) * align
    while t >= align:
        if n % t == 0:
            return t
        t -= align
    return n


def _params(sem):
    return pltpu.CompilerParams(dimension_semantics=sem, vmem_limit_bytes=VMEM_LIMIT)


def _rows(x):
    return lax.broadcasted_iota(jnp.int32, x.shape, 0)


def _shift_dn(x, k, fill=0.0):
    if k == 0:
        return x
    return jnp.where(_rows(x) >= k, pltpu.roll(x, k, 0), fill)


def _shift_up(x, k, fill=0.0):
    if k == 0:
        return x
    s = x.shape[0]
    return jnp.where(_rows(x) < s - k, pltpu.roll(x, s - k, 0), fill)


def _conv_fwd(x, w, b):
    kw = w.shape[0]
    y = _shift_dn(x, kw - 1) * w[0:1, :] + b
    for k in range(1, kw):
        y = y + _shift_dn(x, kw - 1 - k) * w[k:k + 1, :]
    return y


def _conv_bwd_x(dy, w):
    kw = w.shape[0]
    dx = _shift_up(dy, kw - 1) * w[0:1, :]
    for k in range(1, kw):
        dx = dx + _shift_up(dy, kw - 1 - k) * w[k:k + 1, :]
    return dx


def _conv_bwd_w(dy, x, kw):
    return [jnp.sum(dy * _shift_dn(x, kw - 1 - k), axis=0, keepdims=True) for k in range(kw)]


def _accumulate(first, items, cols=slice(None)):
    flat = []
    for ref, val in items:
        if isinstance(val, list):
            flat += [(ref, (slice(k, k + 1), cols), row) for k, row in enumerate(val)]
        else:
            flat.append((ref, Ellipsis, val))

    @pl.when(first)
    def _():
        for ref, idx, val in flat:
            ref[idx] = val

    @pl.when(jnp.logical_not(first))
    def _():
        for ref, idx, val in flat:
            ref[idx] += val


def _colsum(x):
    return jnp.sum(x, axis=0, keepdims=True)


def _sigmoid(x):
    return 1.0 / (1.0 + jnp.exp(-x))


def _log1p(x):
    u = 1.0 + x
    return jnp.where(u == 1.0, x, jnp.log(u) * (x / (u - 1.0)))


def _softplus(x):
    return jnp.maximum(x, 0.0) + _log1p(jnp.exp(-jnp.abs(x)))


def _expm1(x, ex):
    poly = x * (1.0 + x * (0.5 + x * (1.0 / 6.0 + x * (1.0 / 24.0 + x * (1.0 / 120.0 + x * (1.0 / 720.0))))))
    return jnp.where(jnp.abs(x) < 0.25, poly, ex - 1.0)


def _gelu(x):
    t = jnp.tanh(GELU_K * (x + GELU_C * x * x * x))
    return 0.5 * x * (1.0 + t)


def _gelu_and_grad(x):
    x2 = x * x
    t = jnp.tanh(GELU_K * (x + GELU_C * x * x2))
    g = 0.5 * x * (1.0 + t)
    dg = 0.5 * (1.0 + t) + 0.5 * x * (1.0 - t * t) * (GELU_K * (1.0 + 3.0 * GELU_C * x2))
    return g, dg


def _scan_fwd(a, b):
    s = a.shape[0]
    k = 1
    while k < s:
        b = a * _shift_dn(b, k) + b
        if 2 * k < s:
            a = a * _shift_dn(a, k, 1.0)
        k *= 2
    return b


def _scan_rev(c, v):
    s = c.shape[0]
    k = 1
    while k < s:
        v = c * _shift_up(v, k) + v
        if 2 * k < s:
            c = c * _shift_up(c, k, 1.0)
        k *= 2
    return v


def _mm(a, w, *, name, trans_w=False, bias=None, resid=None, resid_scale=1.0):
    m, k = a.shape
    n = w.shape[0] if trans_w else w.shape[1]
    tm = _tile(m, min(1024, max(256, MM_LHS_ELEMS // k)), SUBLANES)
    tn = _tile(n, MM_TN, LANES)
    has_bias = bias is not None
    has_resid = resid is not None

    def body(*refs):
        a_ref, w_ref = refs[0], refs[1]
        pos = 2
        b_ref = r_ref = None
        if has_bias:
            b_ref = refs[pos]
            pos += 1
        if has_resid:
            r_ref = refs[pos]
            pos += 1
        o_ref = refs[pos]

        cols = pl.ds(pl.multiple_of(pl.program_id(1) * tn, LANES), tn)
        if trans_w:
            acc = lax.dot_general(a_ref[...], w_ref[cols, :], (((1,), (1,)), ((), ())), preferred_element_type=F32)
        else:
            acc = jnp.dot(a_ref[...], w_ref[:, cols], preferred_element_type=F32)
        if has_bias:
            acc = acc + b_ref[...]
        if has_resid:
            acc = acc + resid_scale * r_ref[...]
        o_ref[...] = acc

    in_specs = [pl.BlockSpec((tm, k), lambda i, j: (i, 0)),
                pl.BlockSpec(w.shape, lambda i, j: (0, 0), pipeline_mode=pl.Buffered(1))]
    args = [a, w]
    if has_bias:
        in_specs.append(pl.BlockSpec((1, tn), lambda i, j: (0, j)))
        args.append(bias)
    if has_resid:
        in_specs.append(pl.BlockSpec((tm, tn), lambda i, j: (i, j)))
        args.append(resid)
    return pl.pallas_call(
        body, name=name, grid=(m // tm, n // tn), in_specs=in_specs,
        out_specs=pl.BlockSpec((tm, tn), lambda i, j: (i, j)),
        out_shape=jax.ShapeDtypeStruct((m, n), F32),
        compiler_params=_params(("parallel", "arbitrary")),
    )(*args)


def _mm_ln(a, w, resid, alpha, g, b, *, name, tm=512):
    m, k = a.shape
    d = w.shape[1]
    tm = _tile(m, tm, SUBLANES)

    def body(a_ref, w_ref, r_ref, g_ref, b_ref, z_ref, o_ref, obf_ref):
        y = jnp.dot(a_ref[...], w_ref[...], preferred_element_type=F32)
        z = alpha * r_ref[...] + y
        z_ref[...] = z
        mu = jnp.mean(z, axis=-1, keepdims=True)
        zc = z - mu
        var = jnp.mean(zc * zc, axis=-1, keepdims=True)
        o = zc * lax.rsqrt(var + LN_EPS) * g_ref[...] + b_ref[...]
        o_ref[...] = o
        obf_ref[...] = o.astype(BF16)

    row = pl.BlockSpec((tm, d), lambda i: (i, 0))
    vec = pl.BlockSpec((1, d), lambda i: (0, 0))
    return pl.pallas_call(
        body, name=name, grid=(m // tm,),
        in_specs=[pl.BlockSpec((tm, k), lambda i: (i, 0)),
                  pl.BlockSpec((k, d), lambda i: (0, 0), pipeline_mode=pl.Buffered(1)), row, vec, vec],
        out_specs=[row, row, row],
        out_shape=[jax.ShapeDtypeStruct((m, d), F32), jax.ShapeDtypeStruct((m, d), F32),
                   jax.ShapeDtypeStruct((m, d), BF16)],
        compiler_params=_params(("parallel",)),
    )(a, w, resid, g, b)


def _ln_bwd_math(do, z, g):
    mu = jnp.mean(z, axis=-1, keepdims=True)
    zc = z - mu
    var = jnp.mean(zc * zc, axis=-1, keepdims=True)
    rstd = lax.rsqrt(var + LN_EPS)
    xhat = zc * rstd
    dxh = do * g
    m1 = jnp.mean(dxh, axis=-1, keepdims=True)
    m2 = jnp.mean(dxh * xhat, axis=-1, keepdims=True)
    return rstd * (dxh - m1 - xhat * m2), _colsum(do * xhat), _colsum(do)


def _mm_ln_bwd(parts, w, resid, resid_scale, z, g, *, name):
    t, kp = parts[0].shape
    d, k = w.shape
    n = len(parts)
    tm = _tile(t, min(512, max(256, MM_LHS_ELEMS // k)), SUBLANES)

    def body(*refs):
        a_refs = refs[:n]
        w_ref, r_ref, z_ref, g_ref, dz_ref, dzbf_ref, dg_ref, db_ref = refs[n:]

        @pl.when(pl.program_id(0) == 0)
        def _():
            dg_ref[...] = jnp.zeros_like(dg_ref)
            db_ref[...] = jnp.zeros_like(db_ref)

        dx = resid_scale * r_ref[...]
        for p, a_ref in enumerate(a_refs):
            dx = dx + lax.dot_general(a_ref[...], w_ref[:, p * kp:(p + 1) * kp], (((1,), (1,)), ((), ())),
                                      preferred_element_type=F32)
        dz, dg, db = _ln_bwd_math(dx, z_ref[...], g_ref[...])
        dz_ref[...] = dz
        dzbf_ref[...] = dz.astype(BF16)
        dg_ref[...] += dg
        db_ref[...] += db

    row = pl.BlockSpec((tm, d), lambda i: (i, 0))
    vec = pl.BlockSpec((1, d), lambda i: (0, 0))
    return pl.pallas_call(
        body, name=name, grid=(t // tm,),
        in_specs=[pl.BlockSpec((tm, kp), lambda i: (i, 0))] * n
        + [pl.BlockSpec((d, k), lambda i: (0, 0), pipeline_mode=pl.Buffered(1)), row, row, vec],
        out_specs=[row, row, vec, vec],
        out_shape=[jax.ShapeDtypeStruct((t, d), F32), jax.ShapeDtypeStruct((t, d), BF16),
                   jax.ShapeDtypeStruct((1, d), F32), jax.ShapeDtypeStruct((1, d), F32)],
        compiler_params=_params(("arbitrary",)),
    )(*parts, w, resid, z, g)


def _mm_tn(a, b, *, name, tm=1408, tn=1536, tk=1024):
    t, m = a.shape
    n = b.shape[1]
    tm = _tile(m, tm, LANES)
    tn = _tile(n, tn, LANES)
    tk = _tile(t, tk, SUBLANES)

    def body(a_ref, b_ref, o_ref):
        @pl.when(pl.program_id(2) == 0)
        def _():
            o_ref[...] = jnp.zeros_like(o_ref)

        o_ref[...] += lax.dot_general(a_ref[...], b_ref[...], (((0,), (0,)), ((), ())), preferred_element_type=F32)

    return pl.pallas_call(
        body, name=name, grid=(m // tm, n // tn, t // tk),
        in_specs=[pl.BlockSpec((tk, tm), lambda i, j, l: (l, i)), pl.BlockSpec((tk, tn), lambda i, j, l: (l, j))],
        out_specs=pl.BlockSpec((tm, tn), lambda i, j, l: (i, j)),
        out_shape=jax.ShapeDtypeStruct((m, n), F32),
        compiler_params=_params(("parallel", "parallel", "arbitrary")),
    )(a, b)


def _ln_bwd(dout, z, g, *, name, tm=512):
    t, d = z.shape
    tm = _tile(t, tm, SUBLANES)

    def body(do_ref, z_ref, g_ref, dz_ref, dzbf_ref, dg_ref, db_ref):
        @pl.when(pl.program_id(0) == 0)
        def _():
            dg_ref[...] = jnp.zeros_like(dg_ref)
            db_ref[...] = jnp.zeros_like(db_ref)

        dz, dg, db = _ln_bwd_math(do_ref[...], z_ref[...], g_ref[...])
        dz_ref[...] = dz
        dzbf_ref[...] = dz.astype(BF16)
        dg_ref[...] += dg
        db_ref[...] += db

    row = pl.BlockSpec((tm, d), lambda i: (i, 0))
    vec = pl.BlockSpec((1, d), lambda i: (0, 0))
    return pl.pallas_call(
        body, name=name, grid=(t // tm,), in_specs=[row, row, vec], out_specs=[row, row, vec, vec],
        out_shape=[jax.ShapeDtypeStruct((t, d), F32), jax.ShapeDtypeStruct((t, d), BF16),
                   jax.ShapeDtypeStruct((1, d), F32), jax.ShapeDtypeStruct((1, d), F32)],
        compiler_params=_params(("arbitrary",)),
    )(dout, z, g)


def _loss_head(y, target, *, name, tm=512):
    t, d = y.shape
    tm = _tile(t, tm, SUBLANES)

    def body(y_ref, t_ref, s_ref, dy_ref):
        @pl.when(pl.program_id(0) == 0)
        def _():
            s_ref[...] = jnp.zeros_like(s_ref)

        e = y_ref[...] - t_ref[...]
        dy_ref[...] = e * (1.0 / d)
        s_ref[...] += jnp.sum(_colsum(e * e), axis=-1, keepdims=True)

    row = pl.BlockSpec((tm, d), lambda i: (i, 0))
    return pl.pallas_call(
        body, name=name, grid=(t // tm,), in_specs=[row, row],
        out_specs=[pl.BlockSpec((1, LANES), lambda i: (0, 0)), row],
        out_shape=[jax.ShapeDtypeStruct((1, LANES), F32), jax.ShapeDtypeStruct((t, d), F32)],
        compiler_params=_params(("arbitrary",)),
    )(y, target)


def _own(c, b, *_):
    return c, b


def _ahead(nc, bsz):
    def at(c, b, part):
        b2 = b + jnp.minimum(part, 1)
        return jnp.minimum(c + b2 // bsz, nc - 1), b2 % bsz
    return at


def _strip(s, tc, off, at=_own):
    def index(*ids):
        c, b = at(*ids)
        return b, 0, off + c
    return pl.BlockSpec((None, s, tc), index)


def _cvec(kw, tc, off, at=_own):
    def index(*ids):
        return 0, off + at(*ids)[0]
    return pl.BlockSpec((kw, tc), index)


def _acc(kw, tc):
    return pl.BlockSpec((kw, tc), lambda c, b, *_: (0, c))


def _sc_fwd(h, cw, cb, *, name, tc=256):
    bsz, s, d3 = h.shape
    d = d3 // 3
    tc = _tile(d, tc, LANES)
    nc = d // tc

    def body(gb_ref, gc_ref, v_ref, w_ref, b_ref, q_ref):
        u = _conv_fwd(gc_ref[...] * v_ref[...], w_ref[...], b_ref[...])
        q_ref[...] = (gb_ref[...] * u).astype(BF16)

    return pl.pallas_call(
        body, name=name, grid=(nc, bsz),
        in_specs=[_strip(s, tc, 0), _strip(s, tc, nc), _strip(s, tc, 2 * nc), _cvec(cw.shape[0], tc, 0), _cvec(1, tc, 0)],
        out_specs=_strip(s, tc, 0),
        out_shape=jax.ShapeDtypeStruct((bsz, s, d), BF16),
        compiler_params=_params(("parallel", "parallel")),
    )(h, h, h, cw, cb)


def _sc_bwd(h, dq, cw, cb, *, name, tc=256):
    bsz, s, d3 = h.shape
    d = d3 // 3
    kw = cw.shape[0]
    tc = _tile(d, tc, LANES)
    nc = d // tc

    def body(gb_ref, gc_ref, v_ref, dq_ref, w_ref, b_ref, dh_ref, dw_ref, db_ref, parts):
        b_id, part = pl.program_id(1), pl.program_id(2)

        @pl.when(part == 0)
        def _():
            gb, gc, v, dq_, w = gb_ref[...], gc_ref[...], v_ref[...], dq_ref[...], w_ref[...]
            p = gc * v
            u = _conv_fwd(p, w, b_ref[...])
            du = dq_ * gb
            dp = _conv_bwd_x(du, w)
            parts[0] = (dq_ * u).astype(BF16)
            parts[1] = (dp * v).astype(BF16)
            parts[2] = (dp * gc).astype(BF16)
            _accumulate(b_id == 0, [(dw_ref, _conv_bwd_w(du, p, kw)), (db_ref, _colsum(du))])

        dh_ref[...] = parts[part]

    at = _ahead(nc, bsz)
    return pl.pallas_call(
        body, name=name, grid=(nc, bsz, 3),
        in_specs=[_strip(s, tc, 0, at), _strip(s, tc, nc, at), _strip(s, tc, 2 * nc, at), _strip(s, tc, 0, at),
                  _cvec(kw, tc, 0, at), _cvec(1, tc, 0, at)],
        out_specs=[pl.BlockSpec((None, s, tc), lambda c, b, p: (b, 0, p * nc + c)), _acc(kw, tc), _acc(1, tc)],
        out_shape=[jax.ShapeDtypeStruct((bsz, s, d3), BF16), jax.ShapeDtypeStruct((kw, d), F32),
                   jax.ShapeDtypeStruct((1, d), F32)],
        scratch_shapes=[pltpu.VMEM((3, s, tc), BF16)],
        compiler_params=_params(("parallel", "arbitrary", "arbitrary")),
    )(h, h, h, dq, cw, cb)


def _ffn_specs(s, tc, nc, kw):
    strip = pl.BlockSpec((None, s, tc), lambda b, c: (b, 0, c))
    halves = [pl.BlockSpec((kw, tc), lambda b, c: (0, c)), pl.BlockSpec((kw, tc), lambda b, c: (0, nc + c)),
              pl.BlockSpec((1, tc), lambda b, c: (0, c)), pl.BlockSpec((1, tc), lambda b, c: (0, nc + c))]
    return strip, halves


def _ffn_fwd(x, w_up, cw, cb, *, name, tc=256):
    bsz, s, d = x.shape
    f = w_up.shape[1] // 2
    kw = cw.shape[0]
    tc = _tile(f, tc, LANES)
    nc = f // tc

    def body(x_ref, w_ref, wg_ref, wv_ref, bg_ref, bv_ref, hg_ref, hv_ref, a_ref):
        c0 = pl.multiple_of(pl.program_id(1) * tc, LANES)
        xs = x_ref[...]
        hg = jnp.dot(xs, w_ref[:, pl.ds(c0, tc)], preferred_element_type=F32)
        hv = jnp.dot(xs, w_ref[:, pl.ds(f + c0, tc)], preferred_element_type=F32)
        hg_ref[...] = hg
        hv_ref[...] = hv
        g = _conv_fwd(hg, wg_ref[...], bg_ref[...])
        v = _conv_fwd(hv, wv_ref[...], bv_ref[...])
        a_ref[...] = (g * _sigmoid(g) * v).astype(BF16)

    strip, halves = _ffn_specs(s, tc, nc, kw)
    return pl.pallas_call(
        body, name=name, grid=(bsz, nc),
        in_specs=[pl.BlockSpec((None, s, d), lambda b, c: (b, 0, 0)),
                  pl.BlockSpec(w_up.shape, lambda b, c: (0, 0), pipeline_mode=pl.Buffered(1))] + halves,
        out_specs=[strip, strip, strip],
        out_shape=[jax.ShapeDtypeStruct((bsz, s, f), F32), jax.ShapeDtypeStruct((bsz, s, f), F32),
                   jax.ShapeDtypeStruct((bsz, s, f), BF16)],
        compiler_params=_params(("parallel", "arbitrary")),
    )(x, w_up, cw, cw, cb, cb)


def _ffn_bwd(hg, hv, dz, w_down, cw, cb, *, name, tc=256):
    bsz, s, f = hg.shape
    d = dz.shape[2]
    kw = cw.shape[0]
    tc = _tile(f, tc, LANES)
    nc = f // tc

    def body(hg_ref, hv_ref, dz_ref, wd_ref, wg_ref, wv_ref, bg_ref, bv_ref,
             dhg_ref, dhv_ref, dwg_ref, dwv_ref, dbg_ref, dbv_ref):
        c0 = pl.multiple_of(pl.program_id(1) * tc, LANES)
        cols = pl.ds(c0, tc)
        da = lax.dot_general(dz_ref[...], wd_ref[cols, :], (((1,), (1,)), ((), ())), preferred_element_type=F32)
        hg_, hv_ = hg_ref[...], hv_ref[...]
        wg, wv = wg_ref[...], wv_ref[...]
        g = _conv_fwd(hg_, wg, bg_ref[...])
        v = _conv_fwd(hv_, wv, bv_ref[...])
        sg = _sigmoid(g)
        dv = da * (g * sg)
        dg = da * v * (sg * (1.0 + g * (1.0 - sg)))
        dhg_ref[...] = _conv_bwd_x(dg, wg).astype(BF16)
        dhv_ref[...] = _conv_bwd_x(dv, wv).astype(BF16)
        _accumulate(pl.program_id(0) == 0,
                    [(dwg_ref, _conv_bwd_w(dg, hg_, kw)), (dwv_ref, _conv_bwd_w(dv, hv_, kw)),
                     (dbg_ref, [_colsum(dg)]), (dbv_ref, [_colsum(dv)])], cols)

    strip, halves = _ffn_specs(s, tc, nc, kw)
    whole = lambda r: pl.BlockSpec((r, f), lambda b, c: (0, 0))
    return pl.pallas_call(
        body, name=name, grid=(bsz, nc),
        in_specs=[strip, strip, pl.BlockSpec((None, s, d), lambda b, c: (b, 0, 0)),
                  pl.BlockSpec(w_down.shape, lambda b, c: (0, 0), pipeline_mode=pl.Buffered(1))] + halves,
        out_specs=[strip, strip, whole(kw), whole(kw), whole(1), whole(1)],
        out_shape=[jax.ShapeDtypeStruct((bsz, s, f), BF16), jax.ShapeDtypeStruct((bsz, s, f), BF16),
                   jax.ShapeDtypeStruct((kw, f), F32), jax.ShapeDtypeStruct((kw, f), F32),
                   jax.ShapeDtypeStruct((1, f), F32), jax.ShapeDtypeStruct((1, f), F32)],
        compiler_params=_params(("arbitrary", "arbitrary")),
    )(hg, hv, dz, w_down, cw, cw, cb, cb)


def _lru_gates(r, cw, cb, wg, bg, lam):
    blk = r.shape[1]
    xr = _conv_fwd(r, cw, cb)
    gates = jnp.dot(xr.astype(BF16), wg, preferred_element_type=F32) + bg
    rg = _sigmoid(gates[:, :blk])
    ig = _sigmoid(gates[:, blk:])
    sp = _softplus(-lam)
    la = (-LRU_C * sp) * rg
    a = jnp.exp(la)
    mult = jnp.sqrt(-_expm1(2.0 * la, a * a))
    return xr, rg, ig, sp, a, mult


def _lru_specs(s, blk, heads, kw, at=_own):
    def per_head(*ids):
        return at(*ids)[0], 0, 0
    return [_strip(s, blk, 0, at), _strip(s, blk, heads, at), _cvec(kw, blk, 0, at), _cvec(1, blk, 0, at),
            pl.BlockSpec((None, blk, 2 * blk), per_head), pl.BlockSpec((None, 1, 2 * blk), per_head),
            _cvec(1, blk, 0, at)]


def _lru_fwd(h, cw, cb, wg, bg, lam, *, name):
    bsz, s, r2 = h.shape
    heads, blk = wg.shape[0], wg.shape[1]
    kw = cw.shape[0]

    def body(g_ref, r_ref, cw_ref, cb_ref, wg_ref, bg_ref, lam_ref, y_ref, hs_ref):
        xr, _, ig, _, a, mult = _lru_gates(r_ref[...], cw_ref[...], cb_ref[...], wg_ref[...], bg_ref[...], lam_ref[...])
        hs = _scan_fwd(a, mult * (ig * xr))
        hs_ref[...] = hs
        y_ref[...] = (hs * _gelu(g_ref[...])).astype(BF16)

    out = pl.BlockSpec((None, s, blk), lambda hd, b: (b, 0, hd))
    return pl.pallas_call(
        body, name=name, grid=(heads, bsz), in_specs=_lru_specs(s, blk, heads, kw), out_specs=[out, out],
        out_shape=[jax.ShapeDtypeStruct((bsz, s, r2 // 2), BF16), jax.ShapeDtypeStruct((bsz, s, r2 // 2), F32)],
        compiler_params=_params(("parallel", "parallel")),
    )(h, h, cw, cb, wg, bg, lam)


def _lru_bwd(h, hs, dy, cw, cb, wg, bg, lam, *, name):
    bsz, s, r2 = h.shape
    rw = r2 // 2
    heads, blk = wg.shape[0], wg.shape[1]
    kw = cw.shape[0]

    def body(g_ref, r_ref, cw_ref, cb_ref, wg_ref, bg_ref, lam_ref, hs_ref, dy_ref,
             dh_ref, dcw_ref, dcb_ref, dwg_ref, dbg_ref, dlam_ref, sg_ref, sr_ref, parts):
        b_id, part = pl.program_id(1), pl.program_id(2)

        @pl.when(part == 0)
        def _():
            r, cw_, wg_, lam_ = r_ref[...], cw_ref[...], wg_ref[...], lam_ref[...]
            xr, rg, ig, sp, a, mult = _lru_gates(r, cw_, cb_ref[...], wg_, bg_ref[...], lam_)
            hs_, dy_ = hs_ref[...], dy_ref[...]
            gel, dgel = _gelu_and_grad(g_ref[...])
            dg = dy_ * hs_ * dgel
            lmb = _scan_rev(_shift_up(a, 1, 1.0), dy_ * gel)
            da = lmb * _shift_dn(hs_, 1)
            dmult = lmb * (ig * xr)
            dig = lmb * (mult * xr)
            dxr = lmb * (mult * ig)
            dla = da * a - dmult * (a * a / mult)
            drg = dla * (-LRU_C * sp)
            dsp = _colsum(dla * rg) * (-LRU_C)
            dlam = -dsp * _sigmoid(-lam_)
            dgates = jnp.concatenate([drg * (rg * (1.0 - rg)), dig * (ig * (1.0 - ig))], axis=1)
            dgates_bf = dgates.astype(BF16)
            dwg = lax.dot_general(xr.astype(BF16), dgates_bf, (((0,), (0,)), ((), ())), preferred_element_type=F32)
            dxr = dxr + lax.dot_general(dgates_bf, wg_, (((1,), (1,)), ((), ())), preferred_element_type=F32)
            dr = _conv_bwd_x(dxr, cw_)
            parts[0] = dg.astype(BF16)
            parts[1] = dr.astype(BF16)
            _accumulate(b_id == 0, [(dcw_ref, _conv_bwd_w(dxr, r, kw)), (dcb_ref, _colsum(dxr)), (dwg_ref, dwg),
                                    (dbg_ref, _colsum(dgates)), (dlam_ref, dlam), (sg_ref, _colsum(dg)),
                                    (sr_ref, _colsum(dr))])

        dh_ref[...] = parts[part]

    at = _ahead(heads, bsz)
    strip = _strip(s, blk, 0, at)
    vec = pl.BlockSpec((1, blk), lambda hd, b, p: (0, hd))
    return pl.pallas_call(
        body, name=name, grid=(heads, bsz, 2),
        in_specs=_lru_specs(s, blk, heads, kw, at) + [strip, strip],
        out_specs=[pl.BlockSpec((None, s, blk), lambda hd, b, p: (b, 0, p * heads + hd)),
                   pl.BlockSpec((kw, blk), lambda hd, b, p: (0, hd)), vec,
                   pl.BlockSpec((None, blk, 2 * blk), lambda hd, b, p: (hd, 0, 0)),
                   pl.BlockSpec((None, 1, 2 * blk), lambda hd, b, p: (hd, 0, 0)), vec, vec, vec],
        out_shape=[jax.ShapeDtypeStruct((bsz, s, r2), BF16), jax.ShapeDtypeStruct((kw, rw), F32),
                   jax.ShapeDtypeStruct((1, rw), F32), jax.ShapeDtypeStruct((heads, blk, 2 * blk), F32),
                   jax.ShapeDtypeStruct((heads, 1, 2 * blk), F32), jax.ShapeDtypeStruct((1, rw), F32),
                   jax.ShapeDtypeStruct((1, rw), F32), jax.ShapeDtypeStruct((1, rw), F32)],
        scratch_shapes=[pltpu.VMEM((2, s, blk), BF16)],
        compiler_params=_params(("parallel", "arbitrary", "arbitrary")),
    )(h, h, cw, cb, wg, bg, lam, hs, dy)


HBM_SPEC = pl.BlockSpec(memory_space=pltpu.HBM)
SEM_SPEC = pl.BlockSpec(memory_space=pltpu.SEMAPHORE)
EFFECT = pltpu.SideEffectType.DATAFLOW_SIDE_EFFECTING


def _peer_copies(srcs, lands, gather, send_sem, recv_sem):
    x, y, c = (lax.axis_index(ax) for ax in MESH_AXES)
    me = 4 * x + 2 * y + c
    copies = []
    for i in range(len(srcs)):
        for d in range(1, N_DEV):
            px = 1 - x if d & 4 else x
            py = 1 - y if d & 2 else y
            pc = 1 - c if d & 1 else c
            src = srcs[i] if gather[i] else srcs[i].at[4 * px + 2 * py + pc]
            k = i * (N_DEV - 1) + d - 1
            copies.append(pltpu.make_async_remote_copy(
                src_ref=src, dst_ref=lands[i].at[me], send_sem=send_sem.at[k], recv_sem=recv_sem.at[k],
                device_id=(px, py, pc), device_id_type=pl.DeviceIdType.MESH))
    return copies


def _exchange_start(arrs, gather, *, name):
    n = len(arrs)
    lands = [lax.empty((N_DEV,) + tuple(a.shape if g else a.shape[1:]), a.dtype) for a, g in zip(arrs, gather)]

    def body(*refs):
        srcs, land_refs = refs[:n], refs[n:2 * n]
        send_sem, recv_sem = refs[2 * n], refs[2 * n + 1]
        token = refs[-1]
        for cp in _peer_copies(srcs, land_refs, gather, send_sem, recv_sem):
            cp.start()
        token[...] = jnp.zeros_like(token)

    sems = pltpu.SemaphoreType.DMA((n * (N_DEV - 1),))
    thru = [pltpu.HBM(a.shape, a.dtype) for a in arrs + lands]
    out = pl.pallas_call(
        body, name=name, in_specs=[HBM_SPEC] * (2 * n),
        out_shape=(sems, sems, *thru, jax.ShapeDtypeStruct((SUBLANES, LANES), F32)),
        out_specs=(SEM_SPEC, SEM_SPEC, *([HBM_SPEC] * (2 * n)), pl.BlockSpec(memory_space=pltpu.VMEM)),
        input_output_aliases={i: 2 + i for i in range(2 * n)},
        compiler_params=pltpu.CompilerParams(has_side_effects=EFFECT),
    )(*[pltpu.with_memory_space_constraint(a, pltpu.HBM) for a in arrs + lands])
    return {"send_sem": out[0], "recv_sem": out[1], "srcs": list(out[2:2 + n]), "lands": list(out[2 + n:2 + 2 * n]),
            "token": out[-1], "gather": list(gather)}


def _exchange_wait(handle, after, *, name):
    srcs, lands, gather = handle["srcs"], handle["lands"], handle["gather"]
    n = len(srcs)

    def body(*refs):
        src_refs, land_refs = refs[:n], refs[n:2 * n]
        send_sem, recv_sem = refs[2 * n], refs[2 * n + 1]
        for cp in _peer_copies(src_refs, land_refs, gather, send_sem, recv_sem):
            cp.wait_send()
            cp.wait_recv()

    out = pl.pallas_call(
        body, name=name,
        in_specs=[HBM_SPEC] * (2 * n) + [SEM_SPEC, SEM_SPEC, pl.BlockSpec(memory_space=pl.ANY)],
        out_shape=tuple(pltpu.HBM(a.shape, a.dtype) for a in srcs + lands), out_specs=tuple([HBM_SPEC] * (2 * n)),
        input_output_aliases={i: i for i in range(2 * n)},
        compiler_params=pltpu.CompilerParams(has_side_effects=EFFECT),
    )(*srcs, *lands, handle["send_sem"], handle["recv_sem"], after)
    return list(out[:n]), list(out[n:])


def _adamw(parts, w, m, v, *, name, tr=256):
    r, c = w.shape
    tr = _tile(r, tr, SUBLANES)
    bc1 = 1.0 / (1.0 - ADAM_B1 ** ADAM_STEP)
    bc2 = 1.0 / (1.0 - ADAM_B2 ** ADAM_STEP)

    def body(p_ref, w_ref, m_ref, v_ref, g_ref, d_ref, mo_ref, vo_ref):
        g = p_ref[0].astype(F32)
        for s in range(1, N_DEV):
            g = g + p_ref[s].astype(F32)
        m_new = ADAM_B1 * m_ref[...] + (1.0 - ADAM_B1) * g
        v_new = ADAM_B2 * v_ref[...] + (1.0 - ADAM_B2) * (g * g)
        g_ref[...] = g
        mo_ref[...] = m_new
        vo_ref[...] = v_new
        d_ref[...] = -ADAM_LR * ((m_new * bc1) / (jnp.sqrt(v_new * bc2) + ADAM_EPS) + ADAM_WD * w_ref[...])

    blk = pl.BlockSpec((tr, c), lambda i: (i, 0))
    return pl.pallas_call(
        body, name=name, grid=(r // tr,),
        in_specs=[pl.BlockSpec((N_DEV, tr, c), lambda i: (0, i, 0)), blk, blk, blk],
        out_specs=[blk] * 4, out_shape=[jax.ShapeDtypeStruct((r, c), F32)] * 4,
        compiler_params=_params(("parallel",)),
    )(parts, w, m, v)


def _whole(slabs, axis):
    x = jnp.moveaxis(slabs, 0, axis)
    shp = x.shape
    return x.reshape(shp[:axis] + (shp[axis] * shp[axis + 1],) + shp[axis + 2:])


def _slabs(whole, axis):
    shp = whole.shape
    x = whole.reshape(shp[:axis] + (N_DEV, shp[axis] // N_DEV) + shp[axis + 1:])
    return jnp.moveaxis(x, axis, 0)


def _pack(vecs, rows):
    flat = jnp.concatenate(vecs, axis=-1)
    pad = rows * LANES - flat.shape[-1]
    flat = jnp.pad(flat, [(0, 0)] * (flat.ndim - 1) + [(0, pad)])
    return flat.reshape(flat.shape[:-1] + (rows, LANES))


def _unpack(packed, sizes):
    flat = packed.reshape(packed.shape[:-2] + (-1,))
    out, pos = [], 0
    for n in sizes:
        out.append(flat[..., pos:pos + n])
        pos += n
    return out


def _pack_rows(sizes):
    total = sum(sizes)
    return -(-total // (LANES * SUBLANES)) * SUBLANES


BIG = {"sc_w_in": 2, "sc_w_out": 1, "lru_w_in": 2, "lru_w_gate": 3, "lru_w_out": 1, "ffn_w_up": 2, "ffn_w_down": 1}
SMALL = ["sc_conv_w", "lru_b_in", "lru_conv_w", "lru_conv_b", "lru_b_gate", "lru_lambda", "ffn_conv_w", "ln_g", "ln_b"]
REPL = ["sc_conv_b", "ffn_conv_b"]
WEIGHTS = ["sc_w_in", "sc_conv_w", "sc_conv_b", "sc_w_out", "lru_w_in", "lru_b_in", "lru_conv_w", "lru_conv_b",
           "lru_w_gate", "lru_b_gate", "lru_lambda", "lru_w_out", "ffn_w_up", "ffn_conv_w", "ffn_conv_b", "ffn_w_down",
           "ln_g", "ln_b"]


def _stage_big(g):
    i, j = g // 2, g // 4
    if g % 2:
        return [("ffn_w_up", i), ("ffn_w_down", i)]
    return [("sc_w_in", j), ("sc_w_out", j)] if i % 2 == 0 else [("lru_w_in", j), ("lru_w_gate", j), ("lru_w_out", j)]


def _step(x, loss_target, w, m, v):
    bsz, s, d = x.shape
    t = bsz * s
    depth = w["ffn_w_up"].shape[0]
    alpha = (2.0 * depth) ** 0.25
    heads = w["lru_w_gate"].shape[1]

    small_sizes = [w[k].size for k in SMALL]
    small_rows = _pack_rows(small_sizes)
    small_local = _pack([w[k].reshape(1, -1) for k in SMALL], small_rows)[0]
    me = 4 * lax.axis_index("x") + 2 * lax.axis_index("y") + lax.axis_index("c")

    def with_own(land, own):
        return lax.dynamic_update_slice_in_dim(land, own, me, axis=0)

    stages = 2 * depth
    gathers, tok = [], None
    for g in range(stages):
        arrs = [w[k][l].astype(BF16) for k, l in _stage_big(g)]
        if g == 0:
            arrs.append(small_local)
        if tok is not None:
            arrs[0] = arrs[0] + tok.astype(BF16)
        gathers.append(_exchange_start(arrs, [True] * len(arrs), name=f"gather_start_{g}"))
        tok = gathers[-1]["token"][0, 0]
    full = {k: [None] * w[k].shape[0] for k in BIG}
    full["sc_conv_b"] = w["sc_conv_b"]
    full["ffn_conv_b"] = w["ffn_conv_b"]

    def arrive(g, after):
        srcs, lands = _exchange_wait(gathers[g], after, name=f"gather_wait_{g}")
        for (k, l), src, land in zip(_stage_big(g), srcs, lands):
            full[k][l] = _whole(with_own(land, src[None]), BIG[k] - 1)
        if g == 0:
            for k, seg in zip(SMALL, _unpack(with_own(lands[-1], srcs[-1][None]), small_sizes)):
                full[k] = _whole(seg.reshape((N_DEV,) + w[k].shape), w[k].ndim - 1)

    xt = x.reshape(t, d)
    xb = xt.astype(BF16)
    saved = []
    for i in range(depth):
        j = i // 2
        arrive(2 * i, gathers[-1]["token"] if i == 0 else xb)
        lng, lnb = full["ln_g"][i], full["ln_b"][i]
        sv = {"x0": xb}
        if i % 2 == 0:
            hm = _mm(xb, full["sc_w_in"][j], name="sc_in")
            q = _sc_fwd(hm.reshape(bsz, s, -1), full["sc_conv_w"][j], full["sc_conv_b"][j:j + 1], name="sc_mix")
            w_out = full["sc_w_out"][j]
        else:
            hm = _mm(xb, full["lru_w_in"][j], bias=full["lru_b_in"][j:j + 1], name="lru_in")
            q, hs = _lru_fwd(hm.reshape(bsz, s, -1), full["lru_conv_w"][j], full["lru_conv_b"][j:j + 1],
                             full["lru_w_gate"][j], full["lru_b_gate"][j].reshape(heads, 1, -1),
                             full["lru_lambda"][j:j + 1], name="lru_mix")
            sv["hs"] = hs
            w_out = full["lru_w_out"][j]
        q = q.reshape(t, -1)
        arrive(2 * i + 1, q)
        z1, x1, x1b = _mm_ln(q, w_out, xt, alpha, lng[0:1], lnb[0:1], name="mix_out_ln")
        hg, hv, a = _ffn_fwd(x1b.reshape(bsz, s, d), full["ffn_w_up"][i], full["ffn_conv_w"][i],
                             full["ffn_conv_b"][i:i + 1], name="ffn_up_act")
        a = a.reshape(t, -1)
        z2, xt, xb = _mm_ln(a, full["ffn_w_down"][i], x1, alpha, lng[1:2], lnb[1:2], name="ffn_down_ln")
        sv.update(hm=hm, q=q, z1=z1, x1=x1b, hg=hg, hv=hv, a=a, z2=z2)
        saved.append(sv)

    sq, dx = _loss_head(xt, loss_target.reshape(t, d), name="loss_head")
    loss = lax.psum((0.5 / d) * sq[0, 0], MESH_AXES)

    grads = {k: [None] * w[k].shape[0] for k in WEIGHTS}
    scatters = [None] * stages

    def depart(g):
        send = [_slabs(grads[k][l], BIG[k] - 1).astype(BF16) for k, l in _stage_big(g)]
        scatters[g] = _exchange_start(send, [False] * len(send), name=f"scatter_start_{g}")
        return scatters[g]["token"][0:1, 0:1]

    dz2, dz2b, dg2, db2 = _ln_bwd(dx, saved[-1]["z2"], full["ln_g"][-1][1:2], name="ln_bwd")
    for i in reversed(range(depth)):
        j = i // 2
        sv = saved[i]
        lng = full["ln_g"][i]
        grads["ffn_w_down"][i] = _mm_tn(sv["a"], dz2b, name="ffn_down_dw")
        dhg, dhv, dwg, dwv, dbg, dbv = _ffn_bwd(sv["hg"], sv["hv"], dz2b.reshape(bsz, s, d), full["ffn_w_down"][i],
                                                full["ffn_conv_w"][i], full["ffn_conv_b"][i:i + 1], name="ffn_act_bwd")
        dhg, dhv = dhg.reshape(t, -1), dhv.reshape(t, -1)
        grads["ffn_conv_w"][i] = jnp.concatenate([dwg, dwv], axis=1)
        grads["ffn_conv_b"][i] = jnp.concatenate([dbg, dbv], axis=1)[0]
        grads["ffn_w_up"][i] = jnp.concatenate([_mm_tn(sv["x1"], dhg, name="ffn_up_dw_g"),
                                                _mm_tn(sv["x1"], dhv, name="ffn_up_dw_v")], axis=1)
        dz1, dz1b, dg1, db1 = _mm_ln_bwd([dhg, dhv], full["ffn_w_up"][i], dz2, alpha, sv["z1"],
                                         lng[0:1] + depart(2 * i + 1), name="ffn_up_dx_ln")
        grads["ln_g"][i] = jnp.concatenate([dg1, dg2], axis=0)
        grads["ln_b"][i] = jnp.concatenate([db1, db2], axis=0)
        if i % 2 == 0:
            dq = _mm(dz1b, full["sc_w_out"][j], trans_w=True, name="sc_out_dx")
            grads["sc_w_out"][j] = _mm_tn(sv["q"], dz1b, name="sc_out_dw")
            dhm, dcw, dcb = _sc_bwd(sv["hm"].reshape(bsz, s, -1), dq.reshape(bsz, s, -1), full["sc_conv_w"][j],
                                    full["sc_conv_b"][j:j + 1], name="sc_mix_bwd")
            dhm = dhm.reshape(t, -1)
            grads["sc_conv_w"][j] = dcw
            grads["sc_conv_b"][j] = dcb[0]
            grads["sc_w_in"][j] = _mm_tn(sv["x0"], dhm, name="sc_in_dw")
            w_in = full["sc_w_in"][j]
        else:
            dq = _mm(dz1b, full["lru_w_out"][j], trans_w=True, name="lru_out_dx")
            grads["lru_w_out"][j] = _mm_tn(sv["q"], dz1b, name="lru_out_dw")
            dhm, dcw, dcb, dwgt, dbgt, dlam, sgb, srb = _lru_bwd(
                sv["hm"].reshape(bsz, s, -1), sv["hs"], dq.reshape(bsz, s, -1), full["lru_conv_w"][j],
                full["lru_conv_b"][j:j + 1], full["lru_w_gate"][j], full["lru_b_gate"][j].reshape(heads, 1, -1),
                full["lru_lambda"][j:j + 1], name="lru_mix_bwd")
            dhm = dhm.reshape(t, -1)
            grads["lru_conv_w"][j] = dcw
            grads["lru_conv_b"][j] = dcb[0]
            grads["lru_w_gate"][j] = dwgt
            grads["lru_b_gate"][j] = dbgt[:, 0, :]
            grads["lru_lambda"][j] = dlam[0]
            grads["lru_b_in"][j] = jnp.concatenate([sgb, srb], axis=1)[0]
            grads["lru_w_in"][j] = _mm_tn(sv["x0"], dhm, name="lru_in_dw")
            w_in = full["lru_w_in"][j]
        tok = depart(2 * i)
        if i > 0:
            dz2, dz2b, dg2, db2 = _mm_ln_bwd([dhm], w_in, dz1, alpha, saved[i - 1]["z2"], full["ln_g"][i - 1][1:2] + tok,
                                             name="mix_in_dx_ln")
        else:
            dx = _mm(dhm, w_in + tok[0, 0].astype(BF16), trans_w=True, resid=dz1, resid_scale=alpha, name="mix_in_dx")
    grad_x = dx.reshape(bsz, s, d)

    gsm = {k: jnp.stack(grads[k]) for k in SMALL + REPL}
    small_send = _pack([_slabs(gsm[k], gsm[k].ndim - 1).reshape(N_DEV, -1) for k in SMALL], small_rows)
    repl_sizes = [w[k].size for k in REPL]
    repl_rows = _pack_rows(repl_sizes)
    repl_send = _pack([gsm[k].reshape(1, -1) for k in REPL], repl_rows)[0]
    small_scatter = _exchange_start([small_send, repl_send], [False, True], name="scatter_start_small")

    out = {}

    def update(key, parts, wk, mk, vk):
        g, dl, mn, vn = _adamw(parts, wk, mk, vk, name="adamw_" + key)
        return g, dl, mn, vn

    def own_slab(src):
        return lax.dynamic_slice_in_dim(src, me, 1, axis=0)

    per_layer = {k: [None] * w[k].shape[0] for k in BIG}
    after = dx
    for g in reversed(range(stages)):
        srcs, lands = _exchange_wait(scatters[g], after, name=f"scatter_wait_{g}")
        for (k, l), src, land in zip(_stage_big(g), srcs, lands):
            shp = w[k].shape[1:]
            c2 = shp[-1]
            res = update(f"{k}_{l}", with_own(land, own_slab(src)).reshape(N_DEV, -1, c2), w[k][l].reshape(-1, c2),
                         m[k][l].reshape(-1, c2), v[k][l].reshape(-1, c2))
            per_layer[k][l] = [r.reshape(shp) for r in res]
            after = res[-1]
    for k in BIG:
        out[k] = [jnp.stack([per_layer[k][l][r_i] for l in range(w[k].shape[0])]) for r_i in range(4)]
    srcs, lands = _exchange_wait(small_scatter, after, name="scatter_wait_small")
    got_small = with_own(lands[0], own_slab(srcs[0]))
    got_repl = with_own(lands[1], srcs[1][None])
    pk = lambda src, names, rows: _pack([src[k].reshape(1, -1) for k in names], rows)[0]
    res = update("small", got_small, small_local, pk(m, SMALL, small_rows), pk(v, SMALL, small_rows))
    for r_i, r in enumerate(res):
        for k, seg in zip(SMALL, _unpack(r, small_sizes)):
            out.setdefault(k, [None] * 4)[r_i] = seg.reshape(w[k].shape)
    res = update("repl", got_repl, pk(w, REPL, repl_rows), pk(m, REPL, repl_rows), pk(v, REPL, repl_rows))
    for r_i, r in enumerate(res):
        for k, seg in zip(REPL, _unpack(r, repl_sizes)):
            out.setdefault(k, [None] * 4)[r_i] = seg.reshape(w[k].shape)

    return (loss, grad_x, *[out[k][0] for k in WEIGHTS], *[out[k][1] for k in WEIGHTS],
            *[out[k][2] for k in WEIGHTS], *[out[k][3] for k in WEIGHTS])


def kernel(x, sc_w_in, sc_conv_w, sc_conv_b, sc_w_out, lru_w_in, lru_b_in, lru_conv_w, lru_conv_b, lru_w_gate, lru_b_gate, lru_lambda, lru_w_out, ffn_w_up, ffn_conv_w, ffn_conv_b, ffn_w_down, ln_g, ln_b, loss_target, m_sc_w_in, m_sc_conv_w, m_sc_conv_b, m_sc_w_out, m_lru_w_in, m_lru_b_in, m_lru_conv_w, m_lru_conv_b, m_lru_w_gate, m_lru_b_gate, m_lru_lambda, m_lru_w_out, m_ffn_w_up, m_ffn_conv_w, m_ffn_conv_b, m_ffn_w_down, m_ln_g, m_ln_b, v_sc_w_in, v_sc_conv_w, v_sc_conv_b, v_sc_w_out, v_lru_w_in, v_lru_b_in, v_lru_conv_w, v_lru_conv_b, v_lru_w_gate, v_lru_b_gate, v_lru_lambda, v_lru_w_out, v_ffn_w_up, v_ffn_conv_w, v_ffn_conv_b, v_ffn_w_down, v_ln_g, v_ln_b):
    w = dict(sc_w_in=sc_w_in, sc_conv_w=sc_conv_w, sc_conv_b=sc_conv_b, sc_w_out=sc_w_out, lru_w_in=lru_w_in,
             lru_b_in=lru_b_in, lru_conv_w=lru_conv_w, lru_conv_b=lru_conv_b, lru_w_gate=lru_w_gate,
             lru_b_gate=lru_b_gate, lru_lambda=lru_lambda, lru_w_out=lru_w_out, ffn_w_up=ffn_w_up,
             ffn_conv_w=ffn_conv_w, ffn_conv_b=ffn_conv_b, ffn_w_down=ffn_w_down, ln_g=ln_g, ln_b=ln_b)
    m = dict(sc_w_in=m_sc_w_in, sc_conv_w=m_sc_conv_w, sc_conv_b=m_sc_conv_b, sc_w_out=m_sc_w_out, lru_w_in=m_lru_w_in,
             lru_b_in=m_lru_b_in, lru_conv_w=m_lru_conv_w, lru_conv_b=m_lru_conv_b, lru_w_gate=m_lru_w_gate,
             lru_b_gate=m_lru_b_gate, lru_lambda=m_lru_lambda, lru_w_out=m_lru_w_out, ffn_w_up=m_ffn_w_up,
             ffn_conv_w=m_ffn_conv_w, ffn_conv_b=m_ffn_conv_b, ffn_w_down=m_ffn_w_down, ln_g=m_ln_g, ln_b=m_ln_b)
    v = dict(sc_w_in=v_sc_w_in, sc_conv_w=v_sc_conv_w, sc_conv_b=v_sc_conv_b, sc_w_out=v_sc_w_out, lru_w_in=v_lru_w_in,
             lru_b_in=v_lru_b_in, lru_conv_w=v_lru_conv_w, lru_conv_b=v_lru_conv_b, lru_w_gate=v_lru_w_gate,
             lru_b_gate=v_lru_b_gate, lru_lambda=v_lru_lambda, lru_w_out=v_lru_w_out, ffn_w_up=v_ffn_w_up,
             ffn_conv_w=v_ffn_conv_w, ffn_conv_b=v_ffn_conv_b, ffn_w_down=v_ffn_w_down, ln_g=v_ln_g, ln_b=v_ln_b)
    return _step(x, loss_target, w, m, v)
```

```python
import functools
import math

import jax
import jax.numpy as jnp
from jax import lax
from jax.experimental import pallas as pl
from jax.experimental.pallas import tpu as pltpu

F32 = jnp.float32
BF16 = jnp.bfloat16

N_DEV = 8
MESH_AXES = ("x", "y", "c")
LANES = 128
SUBLANES = 8
VMEM_LIMIT = 56 * 1024 * 1024
MM_LHS_ELEMS = 3 * 1024 * 1024
MM_TN = 1536

LRU_C = 8.0
LN_EPS = 1e-5
ADAM_LR = 0.001
ADAM_B1 = 0.9
ADAM_B2 = 0.999
ADAM_EPS = 1e-08
ADAM_WD = 0.01
ADAM_STEP = 10
GELU_K = math.sqrt(2.0 / math.pi)
GELU_C = 0.044715


def _tile(n, target, align):
    if n <= target:
        return n
    t = (target // align) * align
    while t >= align:
        if n % t == 0:
            return t
        t -= align
    return n


def _params(sem):
    return pltpu.CompilerParams(dimension_semantics=sem, vmem_limit_bytes=VMEM_LIMIT)


def _rows(x):
    return lax.broadcasted_iota(jnp.int32, x.shape, 0)


def _shift_dn(x, k, fill=0.0):
    if k == 0:
        return x
    return jnp.where(_rows(x) >= k, pltpu.roll(x, k, 0), fill)


def _shift_up(x, k, fill=0.0):
    if k == 0:
        return x
    s = x.shape[0]
    return jnp.where(_rows(x) < s - k, pltpu.roll(x, s - k, 0), fill)


def _conv_fwd(x, w, b):
    kw = w.shape[0]
    y = _shift_dn(x, kw - 1) * w[0:1, :] + b
    for k in range(1, kw):
        y = y + _shift_dn(x, kw - 1 - k) * w[k:k + 1, :]
    return y


def _conv_bwd_x(dy, w):
    kw = w.shape[0]
    dx = _shift_up(dy, kw - 1) * w[0:1, :]
    for k in range(1, kw):
        dx = dx + _shift_up(dy, kw - 1 - k) * w[k:k + 1, :]
    return dx


def _conv_bwd_w(dy, x, kw):
    return [jnp.sum(dy * _shift_dn(x, kw - 1 - k), axis=0, keepdims=True) for k in range(kw)]


def _accumulate(first, items, cols=slice(None)):
    flat = []
    for ref, val in items:
        if isinstance(val, list):
            flat += [(ref, (slice(k, k + 1), cols), row) for k, row in enumerate(val)]
        else:
            flat.append((ref, Ellipsis, val))

    @pl.when(first)
    def _():
        for ref, idx, val in flat:
            ref[idx] = val

    @pl.when(jnp.logical_not(first))
    def _():
        for ref, idx, val in flat:
            ref[idx] += val


def _colsum(x):
    return jnp.sum(x, axis=0, keepdims=True)


def _sigmoid(x):
    return 1.0 / (1.0 + jnp.exp(-x))


def _log1p(x):
    u = 1.0 + x
    return jnp.where(u == 1.0, x, jnp.log(u) * (x / (u - 1.0)))


def _softplus(x):
    return jnp.maximum(x, 0.0) + _log1p(jnp.exp(-jnp.abs(x)))


def _expm1(x, ex):
    poly = x * (1.0 + x * (0.5 + x * (1.0 / 6.0 + x * (1.0 / 24.0 + x * (1.0 / 120.0 + x * (1.0 / 720.0))))))
    return jnp.where(jnp.abs(x) < 0.25, poly, ex - 1.0)


def _gelu(x):
    t = jnp.tanh(GELU_K * (x + GELU_C * x * x * x))
    return 0.5 * x * (1.0 + t)


def _gelu_and_grad(x):
    x2 = x * x
    t = jnp.tanh(GELU_K * (x + GELU_C * x * x2))
    g = 0.5 * x * (1.0 + t)
    dg = 0.5 * (1.0 + t) + 0.5 * x * (1.0 - t * t) * (GELU_K * (1.0 + 3.0 * GELU_C * x2))
    return g, dg


def _scan_fwd(a, b):
    s = a.shape[0]
    k = 1
    while k < s:
        last = 2 * k >= s
        if k % SUBLANES:
            b = a * _shift_dn(b, k) + b
            if not last:
                a = a * _shift_dn(a, k, 1.0)
        else:
            b = jnp.concatenate([b[:k], a[k:] * b[:s - k] + b[k:]], axis=0)
            if not last:
                a = jnp.concatenate([a[:k], a[k:] * a[:s - k]], axis=0)
        k *= 2
    return b


def _scan_rev(c, v):
    s = c.shape[0]
    k = 1
    while k < s:
        last = 2 * k >= s
        if k % SUBLANES:
            v = c * _shift_up(v, k) + v
            if not last:
                c = c * _shift_up(c, k, 1.0)
        else:
            v = jnp.concatenate([c[:s - k] * v[k:] + v[:s - k], v[s - k:]], axis=0)
            if not last:
                c = jnp.concatenate([c[:s - k] * c[k:], c[s - k:]], axis=0)
        k *= 2
    return v


def _mm(a, w, *, name, trans_w=False, bias=None, resid=None, resid_scale=1.0):
    m, k = a.shape
    n = w.shape[0] if trans_w else w.shape[1]
    tm = _tile(m, min(1024, max(256, MM_LHS_ELEMS // k)), SUBLANES)
    tn = _tile(n, MM_TN, LANES)
    has_bias = bias is not None
    has_resid = resid is not None

    def body(*refs):
        a_ref, w_ref = refs[0], refs[1]
        pos = 2
        b_ref = r_ref = None
        if has_bias:
            b_ref = refs[pos]
            pos += 1
        if has_resid:
            r_ref = refs[pos]
            pos += 1
        o_ref = refs[pos]

        cols = pl.ds(pl.multiple_of(pl.program_id(1) * tn, LANES), tn)
        if trans_w:
            acc = lax.dot_general(a_ref[...], w_ref[cols, :], (((1,), (1,)), ((), ())), preferred_element_type=F32)
        else:
            acc = jnp.dot(a_ref[...], w_ref[:, cols], preferred_element_type=F32)
        if has_bias:
            acc = acc + b_ref[...]
        if has_resid:
            acc = acc + resid_scale * r_ref[...]
        o_ref[...] = acc

    in_specs = [pl.BlockSpec((tm, k), lambda i, j: (i, 0)),
                pl.BlockSpec(w.shape, lambda i, j: (0, 0), pipeline_mode=pl.Buffered(1))]
    args = [a, w]
    if has_bias:
        in_specs.append(pl.BlockSpec((1, tn), lambda i, j: (0, j)))
        args.append(bias)
    if has_resid:
        in_specs.append(pl.BlockSpec((tm, tn), lambda i, j: (i, j)))
        args.append(resid)
    return pl.pallas_call(
        body, name=name, grid=(m // tm, n // tn), in_specs=in_specs,
        out_specs=pl.BlockSpec((tm, tn), lambda i, j: (i, j)),
        out_shape=jax.ShapeDtypeStruct((m, n), F32),
        compiler_params=_params(("parallel", "arbitrary")),
    )(*args)


def _mm_ln(a, w, resid, alpha, g, b, *, name, tm=512):
    m, k = a.shape
    d = w.shape[1]
    tm = _tile(m, tm, SUBLANES)

    def body(a_ref, w_ref, r_ref, g_ref, b_ref, z_ref, o_ref, obf_ref):
        y = jnp.dot(a_ref[...], w_ref[...], preferred_element_type=F32)
        z = alpha * r_ref[...] + y
        z_ref[...] = z
        mu = jnp.mean(z, axis=-1, keepdims=True)
        zc = z - mu
        var = jnp.mean(zc * zc, axis=-1, keepdims=True)
        o = zc * lax.rsqrt(var + LN_EPS) * g_ref[...] + b_ref[...]
        o_ref[...] = o
        obf_ref[...] = o.astype(BF16)

    row = pl.BlockSpec((tm, d), lambda i: (i, 0))
    vec = pl.BlockSpec((1, d), lambda i: (0, 0))
    return pl.pallas_call(
        body, name=name, grid=(m // tm,),
        in_specs=[pl.BlockSpec((tm, k), lambda i: (i, 0)),
                  pl.BlockSpec((k, d), lambda i: (0, 0), pipeline_mode=pl.Buffered(1)), row, vec, vec],
        out_specs=[row, row, row],
        out_shape=[jax.ShapeDtypeStruct((m, d), F32), jax.ShapeDtypeStruct((m, d), F32),
                   jax.ShapeDtypeStruct((m, d), BF16)],
        compiler_params=_params(("parallel",)),
    )(a, w, resid, g, b)


def _ln_bwd_math(do, z, g):
    mu = jnp.mean(z, axis=-1, keepdims=True)
    zc = z - mu
    var = jnp.mean(zc * zc, axis=-1, keepdims=True)
    rstd = lax.rsqrt(var + LN_EPS)
    xhat = zc * rstd
    dxh = do * g
    m1 = jnp.mean(dxh, axis=-1, keepdims=True)
    m2 = jnp.mean(dxh * xhat, axis=-1, keepdims=True)
    return rstd * (dxh - m1 - xhat * m2), _colsum(do * xhat), _colsum(do)


def _mm_ln_bwd(parts, w, resid, resid_scale, z, g, *, name):
    t, kp = parts[0].shape
    d, k = w.shape
    n = len(parts)
    tm = _tile(t, min(512, max(256, MM_LHS_ELEMS // k)), SUBLANES)

    def body(*refs):
        a_refs = refs[:n]
        w_ref, r_ref, z_ref, g_ref, dz_ref, dzbf_ref, dg_ref, db_ref = refs[n:]

        @pl.when(pl.program_id(0) == 0)
        def _():
            dg_ref[...] = jnp.zeros_like(dg_ref)
            db_ref[...] = jnp.zeros_like(db_ref)

        dx = resid_scale * r_ref[...]
        for p, a_ref in enumerate(a_refs):
            dx = dx + lax.dot_general(a_ref[...], w_ref[:, p * kp:(p + 1) * kp], (((1,), (1,)), ((), ())),
                                      preferred_element_type=F32)
        dz, dg, db = _ln_bwd_math(dx, z_ref[...], g_ref[...])
        dz_ref[...] = dz
        dzbf_ref[...] = dz.astype(BF16)
        dg_ref[...] += dg
        db_ref[...] += db

    row = pl.BlockSpec((tm, d), lambda i: (i, 0))
    vec = pl.BlockSpec((1, d), lambda i: (0, 0))
    return pl.pallas_call(
        body, name=name, grid=(t // tm,),
        in_specs=[pl.BlockSpec((tm, kp), lambda i: (i, 0))] * n
        + [pl.BlockSpec((d, k), lambda i: (0, 0), pipeline_mode=pl.Buffered(1)), row, row, vec],
        out_specs=[row, row, vec, vec],
        out_shape=[jax.ShapeDtypeStruct((t, d), F32), jax.ShapeDtypeStruct((t, d), BF16),
                   jax.ShapeDtypeStruct((1, d), F32), jax.ShapeDtypeStruct((1, d), F32)],
        compiler_params=_params(("arbitrary",)),
    )(*parts, w, resid, z, g)


def _mm_tn(a, b, *, name, tm=1408, tn=1536, tk=1024):
    t, m = a.shape
    n = b.shape[1]
    tm = _tile(m, tm, LANES)
    tn = _tile(n, tn, LANES)
    tk = _tile(t, tk, SUBLANES)

    def body(a_ref, b_ref, o_ref):
        @pl.when(pl.program_id(2) == 0)
        def _():
            o_ref[...] = jnp.zeros_like(o_ref)

        o_ref[...] += lax.dot_general(a_ref[...], b_ref[...], (((0,), (0,)), ((), ())), preferred_element_type=F32)

    return pl.pallas_call(
        body, name=name, grid=(m // tm, n // tn, t // tk),
        in_specs=[pl.BlockSpec((tk, tm), lambda i, j, l: (l, i)), pl.BlockSpec((tk, tn), lambda i, j, l: (l, j))],
        out_specs=pl.BlockSpec((tm, tn), lambda i, j, l: (i, j)),
        out_shape=jax.ShapeDtypeStruct((m, n), F32),
        compiler_params=_params(("parallel", "parallel", "arbitrary")),
    )(a, b)


def _ln_bwd(dout, z, g, *, name, tm=512):
    t, d = z.shape
    tm = _tile(t, tm, SUBLANES)

    def body(do_ref, z_ref, g_ref, dz_ref, dzbf_ref, dg_ref, db_ref):
        @pl.when(pl.program_id(0) == 0)
        def _():
            dg_ref[...] = jnp.zeros_like(dg_ref)
            db_ref[...] = jnp.zeros_like(db_ref)

        dz, dg, db = _ln_bwd_math(do_ref[...], z_ref[...], g_ref[...])
        dz_ref[...] = dz
        dzbf_ref[...] = dz.astype(BF16)
        dg_ref[...] += dg
        db_ref[...] += db

    row = pl.BlockSpec((tm, d), lambda i: (i, 0))
    vec = pl.BlockSpec((1, d), lambda i: (0, 0))
    return pl.pallas_call(
        body, name=name, grid=(t // tm,), in_specs=[row, row, vec], out_specs=[row, row, vec, vec],
        out_shape=[jax.ShapeDtypeStruct((t, d), F32), jax.ShapeDtypeStruct((t, d), BF16),
                   jax.ShapeDtypeStruct((1, d), F32), jax.ShapeDtypeStruct((1, d), F32)],
        compiler_params=_params(("arbitrary",)),
    )(dout, z, g)


def _loss_head(y, target, *, name, tm=512):
    t, d = y.shape
    tm = _tile(t, tm, SUBLANES)

    def body(y_ref, t_ref, s_ref, dy_ref):
        @pl.when(pl.program_id(0) == 0)
        def _():
            s_ref[...] = jnp.zeros_like(s_ref)

        e = y_ref[...] - t_ref[...]
        dy_ref[...] = e * (1.0 / d)
        s_ref[...] += jnp.sum(_colsum(e * e), axis=-1, keepdims=True)

    row = pl.BlockSpec((tm, d), lambda i: (i, 0))
    return pl.pallas_call(
        body, name=name, grid=(t // tm,), in_specs=[row, row],
        out_specs=[pl.BlockSpec((1, LANES), lambda i: (0, 0)), row],
        out_shape=[jax.ShapeDtypeStruct((1, LANES), F32), jax.ShapeDtypeStruct((t, d), F32)],
        compiler_params=_params(("arbitrary",)),
    )(y, target)


def _own(c, b, *_):
    return c, b


def _ahead(nc, bsz):
    def at(c, b, part):
        b2 = b + jnp.minimum(part, 1)
        return jnp.minimum(c + b2 // bsz, nc - 1), b2 % bsz
    return at


def _strip(s, tc, off, at=_own):
    def index(*ids):
        c, b = at(*ids)
        return b, 0, off + c
    return pl.BlockSpec((None, s, tc), index)


def _cvec(kw, tc, off, at=_own):
    def index(*ids):
        return 0, off + at(*ids)[0]
    return pl.BlockSpec((kw, tc), index)


def _acc(kw, tc):
    return pl.BlockSpec((kw, tc), lambda c, b, *_: (0, c))


def _sc_fwd(h, cw, cb, *, name, tc=256):
    bsz, s, d3 = h.shape
    d = d3 // 3
    tc = _tile(d, tc, LANES)
    nc = d // tc

    def body(gb_ref, gc_ref, v_ref, w_ref, b_ref, q_ref):
        u = _conv_fwd(gc_ref[...] * v_ref[...], w_ref[...], b_ref[...])
        q_ref[...] = (gb_ref[...] * u).astype(BF16)

    return pl.pallas_call(
        body, name=name, grid=(nc, bsz),
        in_specs=[_strip(s, tc, 0), _strip(s, tc, nc), _strip(s, tc, 2 * nc), _cvec(cw.shape[0], tc, 0), _cvec(1, tc, 0)],
        out_specs=_strip(s, tc, 0),
        out_shape=jax.ShapeDtypeStruct((bsz, s, d), BF16),
        compiler_params=_params(("parallel", "parallel")),
    )(h, h, h, cw, cb)


def _sc_bwd(h, dq, cw, cb, *, name, tc=256):
    bsz, s, d3 = h.shape
    d = d3 // 3
    kw = cw.shape[0]
    tc = _tile(d, tc, LANES)
    nc = d // tc

    def body(gb_ref, gc_ref, v_ref, dq_ref, w_ref, b_ref, dh_ref, dw_ref, db_ref, parts):
        b_id, part = pl.program_id(1), pl.program_id(2)

        @pl.when(part == 0)
        def _():
            gb, gc, v, dq_, w = gb_ref[...], gc_ref[...], v_ref[...], dq_ref[...], w_ref[...]
            p = gc * v
            u = _conv_fwd(p, w, b_ref[...])
            du = dq_ * gb
            dp = _conv_bwd_x(du, w)
            parts[0] = (dq_ * u).astype(BF16)
            parts[1] = (dp * v).astype(BF16)
            parts[2] = (dp * gc).astype(BF16)
            _accumulate(b_id == 0, [(dw_ref, _conv_bwd_w(du, p, kw)), (db_ref, _colsum(du))])

        dh_ref[...] = parts[part]

    at = _ahead(nc, bsz)
    return pl.pallas_call(
        body, name=name, grid=(nc, bsz, 3),
        in_specs=[_strip(s, tc, 0, at), _strip(s, tc, nc, at), _strip(s, tc, 2 * nc, at), _strip(s, tc, 0, at),
                  _cvec(kw, tc, 0, at), _cvec(1, tc, 0, at)],
        out_specs=[pl.BlockSpec((None, s, tc), lambda c, b, p: (b, 0, p * nc + c)), _acc(kw, tc), _acc(1, tc)],
        out_shape=[jax.ShapeDtypeStruct((bsz, s, d3), BF16), jax.ShapeDtypeStruct((kw, d), F32),
                   jax.ShapeDtypeStruct((1, d), F32)],
        scratch_shapes=[pltpu.VMEM((3, s, tc), BF16)],
        compiler_params=_params(("parallel", "arbitrary", "arbitrary")),
    )(h, h, h, dq, cw, cb)


def _ffn_specs(s, tc, nc, kw):
    strip = pl.BlockSpec((None, s, tc), lambda b, c: (b, 0, c))
    halves = [pl.BlockSpec((kw, tc), lambda b, c: (0, c)), pl.BlockSpec((kw, tc), lambda b, c: (0, nc + c)),
              pl.BlockSpec((1, tc), lambda b, c: (0, c)), pl.BlockSpec((1, tc), lambda b, c: (0, nc + c))]
    return strip, halves


def _ffn_fwd(x, w_up, cw, cb, *, name, tc=256):
    bsz, s, d = x.shape
    f = w_up.shape[1] // 2
    kw = cw.shape[0]
    tc = _tile(f, tc, LANES)
    nc = f // tc

    def body(x_ref, w_ref, wg_ref, wv_ref, bg_ref, bv_ref, hg_ref, hv_ref, a_ref):
        c0 = pl.multiple_of(pl.program_id(1) * tc, LANES)
        xs = x_ref[...]
        hg = jnp.dot(xs, w_ref[:, pl.ds(c0, tc)], preferred_element_type=F32)
        hv = jnp.dot(xs, w_ref[:, pl.ds(f + c0, tc)], preferred_element_type=F32)
        hg_ref[...] = hg
        hv_ref[...] = hv
        g = _conv_fwd(hg, wg_ref[...], bg_ref[...])
        v = _conv_fwd(hv, wv_ref[...], bv_ref[...])
        a_ref[...] = (g * _sigmoid(g) * v).astype(BF16)

    strip, halves = _ffn_specs(s, tc, nc, kw)
    return pl.pallas_call(
        body, name=name, grid=(bsz, nc),
        in_specs=[pl.BlockSpec((None, s, d), lambda b, c: (b, 0, 0)),
                  pl.BlockSpec(w_up.shape, lambda b, c: (0, 0), pipeline_mode=pl.Buffered(1))] + halves,
        out_specs=[strip, strip, strip],
        out_shape=[jax.ShapeDtypeStruct((bsz, s, f), F32), jax.ShapeDtypeStruct((bsz, s, f), F32),
                   jax.ShapeDtypeStruct((bsz, s, f), BF16)],
        compiler_params=_params(("parallel", "arbitrary")),
    )(x, w_up, cw, cw, cb, cb)


def _ffn_bwd(hg, hv, dz, w_down, cw, cb, *, name, tc=256):
    bsz, s, f = hg.shape
    d = dz.shape[2]
    kw = cw.shape[0]
    tc = _tile(f, tc, LANES)
    nc = f // tc

    def body(hg_ref, hv_ref, dz_ref, wd_ref, wg_ref, wv_ref, bg_ref, bv_ref,
             dhg_ref, dhv_ref, dwg_ref, dwv_ref, dbg_ref, dbv_ref):
        c0 = pl.multiple_of(pl.program_id(1) * tc, LANES)
        cols = pl.ds(c0, tc)
        da = lax.dot_general(dz_ref[...], wd_ref[cols, :], (((1,), (1,)), ((), ())), preferred_element_type=F32)
        hg_, hv_ = hg_ref[...], hv_ref[...]
        wg, wv = wg_ref[...], wv_ref[...]
        g = _conv_fwd(hg_, wg, bg_ref[...])
        v = _conv_fwd(hv_, wv, bv_ref[...])
        sg = _sigmoid(g)
        dv = da * (g * sg)
        dg = da * v * (sg * (1.0 + g * (1.0 - sg)))
        dhg_ref[...] = _conv_bwd_x(dg, wg).astype(BF16)
        dhv_ref[...] = _conv_bwd_x(dv, wv).astype(BF16)
        _accumulate(pl.program_id(0) == 0,
                    [(dwg_ref, _conv_bwd_w(dg, hg_, kw)), (dwv_ref, _conv_bwd_w(dv, hv_, kw)),
                     (dbg_ref, [_colsum(dg)]), (dbv_ref, [_colsum(dv)])], cols)

    strip, halves = _ffn_specs(s, tc, nc, kw)
    whole = lambda r: pl.BlockSpec((r, f), lambda b, c: (0, 0))
    return pl.pallas_call(
        body, name=name, grid=(bsz, nc),
        in_specs=[strip, strip, pl.BlockSpec((None, s, d), lambda b, c: (b, 0, 0)),
                  pl.BlockSpec(w_down.shape, lambda b, c: (0, 0), pipeline_mode=pl.Buffered(1))] + halves,
        out_specs=[strip, strip, whole(kw), whole(kw), whole(1), whole(1)],
        out_shape=[jax.ShapeDtypeStruct((bsz, s, f), BF16), jax.ShapeDtypeStruct((bsz, s, f), BF16),
                   jax.ShapeDtypeStruct((kw, f), F32), jax.ShapeDtypeStruct((kw, f), F32),
                   jax.ShapeDtypeStruct((1, f), F32), jax.ShapeDtypeStruct((1, f), F32)],
        compiler_params=_params(("arbitrary", "arbitrary")),
    )(hg, hv, dz, w_down, cw, cw, cb, cb)


def _lru_gates(r, cw, cb, wg, bg, lam):
    blk = r.shape[1]
    xr = _conv_fwd(r, cw, cb)
    gates = jnp.dot(xr.astype(BF16), wg, preferred_element_type=F32) + bg
    rg = _sigmoid(gates[:, :blk])
    ig = _sigmoid(gates[:, blk:])
    sp = _softplus(-lam)
    la = (-LRU_C * sp) * rg
    a = jnp.exp(la)
    mult = jnp.sqrt(-_expm1(2.0 * la, a * a))
    return xr, rg, ig, sp, a, mult


def _lru_specs(s, blk, heads, kw, at=_own):
    def per_head(*ids):
        return at(*ids)[0], 0, 0
    return [_strip(s, blk, 0, at), _strip(s, blk, heads, at), _cvec(kw, blk, 0, at), _cvec(1, blk, 0, at),
            pl.BlockSpec((None, blk, 2 * blk), per_head), pl.BlockSpec((None, 1, 2 * blk), per_head),
            _cvec(1, blk, 0, at)]


def _lru_fwd(h, cw, cb, wg, bg, lam, *, name):
    bsz, s, r2 = h.shape
    heads, blk = wg.shape[0], wg.shape[1]
    kw = cw.shape[0]

    def body(g_ref, r_ref, cw_ref, cb_ref, wg_ref, bg_ref, lam_ref, y_ref, hs_ref):
        xr, _, ig, _, a, mult = _lru_gates(r_ref[...], cw_ref[...], cb_ref[...], wg_ref[...], bg_ref[...], lam_ref[...])
        hs = _scan_fwd(a, mult * (ig * xr))
        hs_ref[...] = hs
        y_ref[...] = (hs * _gelu(g_ref[...])).astype(BF16)

    out = pl.BlockSpec((None, s, blk), lambda hd, b: (b, 0, hd))
    return pl.pallas_call(
        body, name=name, grid=(heads, bsz), in_specs=_lru_specs(s, blk, heads, kw), out_specs=[out, out],
        out_shape=[jax.ShapeDtypeStruct((bsz, s, r2 // 2), BF16), jax.ShapeDtypeStruct((bsz, s, r2 // 2), F32)],
        compiler_params=_params(("parallel", "parallel")),
    )(h, h, cw, cb, wg, bg, lam)


def _lru_bwd(h, hs, dy, cw, cb, wg, bg, lam, *, name):
    bsz, s, r2 = h.shape
    rw = r2 // 2
    heads, blk = wg.shape[0], wg.shape[1]
    kw = cw.shape[0]

    def body(g_ref, r_ref, cw_ref, cb_ref, wg_ref, bg_ref, lam_ref, hs_ref, dy_ref,
             dh_ref, dcw_ref, dcb_ref, dwg_ref, dbg_ref, dlam_ref, sg_ref, sr_ref, parts):
        b_id, part = pl.program_id(1), pl.program_id(2)

        @pl.when(part == 0)
        def _():
            r, cw_, wg_, lam_ = r_ref[...], cw_ref[...], wg_ref[...], lam_ref[...]
            xr, rg, ig, sp, a, mult = _lru_gates(r, cw_, cb_ref[...], wg_, bg_ref[...], lam_)
            hs_, dy_ = hs_ref[...], dy_ref[...]
            gel, dgel = _gelu_and_grad(g_ref[...])
            dg = dy_ * hs_ * dgel
            lmb = _scan_rev(_shift_up(a, 1, 1.0), dy_ * gel)
            da = lmb * _shift_dn(hs_, 1)
            dmult = lmb * (ig * xr)
            dig = lmb * (mult * xr)
            dxr = lmb * (mult * ig)
            dla = da * a - dmult * (a * a / mult)
            drg = dla * (-LRU_C * sp)
            dsp = _colsum(dla * rg) * (-LRU_C)
            dlam = -dsp * _sigmoid(-lam_)
            dgates = jnp.concatenate([drg * (rg * (1.0 - rg)), dig * (ig * (1.0 - ig))], axis=1)
            dgates_bf = dgates.astype(BF16)
            dwg = lax.dot_general(xr.astype(BF16), dgates_bf, (((0,), (0,)), ((), ())), preferred_element_type=F32)
            dxr = dxr + lax.dot_general(dgates_bf, wg_, (((1,), (1,)), ((), ())), preferred_element_type=F32)
            dr = _conv_bwd_x(dxr, cw_)
            parts[0] = dg.astype(BF16)
            parts[1] = dr.astype(BF16)
            _accumulate(b_id == 0, [(dcw_ref, _conv_bwd_w(dxr, r, kw)), (dcb_ref, _colsum(dxr)), (dwg_ref, dwg),
                                    (dbg_ref, _colsum(dgates)), (dlam_ref, dlam), (sg_ref, _colsum(dg)),
                                    (sr_ref, _colsum(dr))])

        dh_ref[...] = parts[part]

    at = _ahead(heads, bsz)
    strip = _strip(s, blk, 0, at)
    vec = pl.BlockSpec((1, blk), lambda hd, b, p: (0, hd))
    return pl.pallas_call(
        body, name=name, grid=(heads, bsz, 2),
        in_specs=_lru_specs(s, blk, heads, kw, at) + [strip, strip],
        out_specs=[pl.BlockSpec((None, s, blk), lambda hd, b, p: (b, 0, p * heads + hd)),
                   pl.BlockSpec((kw, blk), lambda hd, b, p: (0, hd)), vec,
                   pl.BlockSpec((None, blk, 2 * blk), lambda hd, b, p: (hd, 0, 0)),
                   pl.BlockSpec((None, 1, 2 * blk), lambda hd, b, p: (hd, 0, 0)), vec, vec, vec],
        out_shape=[jax.ShapeDtypeStruct((bsz, s, r2), BF16), jax.ShapeDtypeStruct((kw, rw), F32),
                   jax.ShapeDtypeStruct((1, rw), F32), jax.ShapeDtypeStruct((heads, blk, 2 * blk), F32),
                   jax.ShapeDtypeStruct((heads, 1, 2 * blk), F32), jax.ShapeDtypeStruct((1, rw), F32),
                   jax.ShapeDtypeStruct((1, rw), F32), jax.ShapeDtypeStruct((1, rw), F32)],
        scratch_shapes=[pltpu.VMEM((2, s, blk), BF16)],
        compiler_params=_params(("parallel", "arbitrary", "arbitrary")),
    )(h, h, cw, cb, wg, bg, lam, hs, dy)


HBM_SPEC = pl.BlockSpec(memory_space=pltpu.HBM)
SEM_SPEC = pl.BlockSpec(memory_space=pltpu.SEMAPHORE)
EFFECT = pltpu.SideEffectType.DATAFLOW_SIDE_EFFECTING


def _peer_copies(srcs, lands, gather, send_sem, recv_sem):
    x, y, c = (lax.axis_index(ax) for ax in MESH_AXES)
    me = 4 * x + 2 * y + c
    copies = []
    for i in range(len(srcs)):
        for d in range(1, N_DEV):
            px = 1 - x if d & 4 else x
            py = 1 - y if d & 2 else y
            pc = 1 - c if d & 1 else c
            src = srcs[i] if gather[i] else srcs[i].at[4 * px + 2 * py + pc]
            k = i * (N_DEV - 1) + d - 1
            copies.append(pltpu.make_async_remote_copy(
                src_ref=src, dst_ref=lands[i].at[me], send_sem=send_sem.at[k], recv_sem=recv_sem.at[k],
                device_id=(px, py, pc), device_id_type=pl.DeviceIdType.MESH))
    return copies


def _exchange_start(arrs, gather, *, name):
    n = len(arrs)
    lands = [lax.empty((N_DEV,) + tuple(a.shape if g else a.shape[1:]), a.dtype) for a, g in zip(arrs, gather)]

    def body(*refs):
        srcs, land_refs = refs[:n], refs[n:2 * n]
        send_sem, recv_sem = refs[2 * n], refs[2 * n + 1]
        token = refs[-1]
        for cp in _peer_copies(srcs, land_refs, gather, send_sem, recv_sem):
            cp.start()
        token[...] = jnp.zeros_like(token)

    sems = pltpu.SemaphoreType.DMA((n * (N_DEV - 1),))
    thru = [pltpu.HBM(a.shape, a.dtype) for a in arrs + lands]
    out = pl.pallas_call(
        body, name=name, in_specs=[HBM_SPEC] * (2 * n),
        out_shape=(sems, sems, *thru, jax.ShapeDtypeStruct((SUBLANES, LANES), F32)),
        out_specs=(SEM_SPEC, SEM_SPEC, *([HBM_SPEC] * (2 * n)), pl.BlockSpec(memory_space=pltpu.VMEM)),
        input_output_aliases={i: 2 + i for i in range(2 * n)},
        compiler_params=pltpu.CompilerParams(has_side_effects=EFFECT),
    )(*[pltpu.with_memory_space_constraint(a, pltpu.HBM) for a in arrs + lands])
    return {"send_sem": out[0], "recv_sem": out[1], "srcs": list(out[2:2 + n]), "lands": list(out[2 + n:2 + 2 * n]),
            "token": out[-1], "gather": list(gather)}


def _exchange_wait(handle, after, *, name):
    srcs, lands, gather = handle["srcs"], handle["lands"], handle["gather"]
    n = len(srcs)

    def body(*refs):
        src_refs, land_refs = refs[:n], refs[n:2 * n]
        send_sem, recv_sem = refs[2 * n], refs[2 * n + 1]
        for cp in _peer_copies(src_refs, land_refs, gather, send_sem, recv_sem):
            cp.wait_send()
            cp.wait_recv()

    out = pl.pallas_call(
        body, name=name,
        in_specs=[HBM_SPEC] * (2 * n) + [SEM_SPEC, SEM_SPEC, pl.BlockSpec(memory_space=pl.ANY)],
        out_shape=tuple(pltpu.HBM(a.shape, a.dtype) for a in srcs + lands), out_specs=tuple([HBM_SPEC] * (2 * n)),
        input_output_aliases={i: i for i in range(2 * n)},
        compiler_params=pltpu.CompilerParams(has_side_effects=EFFECT),
    )(*srcs, *lands, handle["send_sem"], handle["recv_sem"], after)
    return list(out[:n]), list(out[n:])


def _adamw(parts, w, m, v, layer, so_far, *, name, tr=256):
    n_layers, r, c = w.shape
    tr = _tile(r, tr, SUBLANES)
    bc1 = 1.0 / (1.0 - ADAM_B1 ** ADAM_STEP)
    bc2 = 1.0 / (1.0 - ADAM_B2 ** ADAM_STEP)
    if so_far is None:
        so_far = [lax.empty(w.shape, F32) for _ in range(4)]

    def body(p_ref, w_ref, m_ref, v_ref, *rest):
        g_ref, d_ref, mo_ref, vo_ref = rest[4:]
        g = p_ref[0].astype(F32)
        for s in range(1, N_DEV):
            g = g + p_ref[s].astype(F32)
        m_new = ADAM_B1 * m_ref[...] + (1.0 - ADAM_B1) * g
        v_new = ADAM_B2 * v_ref[...] + (1.0 - ADAM_B2) * (g * g)
        g_ref[...] = g
        mo_ref[...] = m_new
        vo_ref[...] = v_new
        d_ref[...] = -ADAM_LR * ((m_new * bc1) / (jnp.sqrt(v_new * bc2) + ADAM_EPS) + ADAM_WD * w_ref[...])

    blk = pl.BlockSpec((None, tr, c), lambda i: (layer, i, 0))
    return pl.pallas_call(
        body, name=name, grid=(r // tr,),
        in_specs=[pl.BlockSpec((N_DEV, tr, c), lambda i: (0, i, 0)), blk, blk, blk]
        + [pl.BlockSpec(memory_space=pl.ANY)] * 4,
        out_specs=[blk] * 4, out_shape=[jax.ShapeDtypeStruct(w.shape, F32)] * 4,
        input_output_aliases={4 + o: o for o in range(4)},
        compiler_params=_params(("parallel",)),
    )(parts, w, m, v, *so_far)


def _whole(slabs, axis):
    x = jnp.moveaxis(slabs, 0, axis)
    shp = x.shape
    return x.reshape(shp[:axis] + (shp[axis] * shp[axis + 1],) + shp[axis + 2:])


def _slabs(whole, axis):
    shp = whole.shape
    x = whole.reshape(shp[:axis] + (N_DEV, shp[axis] // N_DEV) + shp[axis + 1:])
    return jnp.moveaxis(x, axis, 0)


def _pack(vecs, rows):
    flat = jnp.concatenate(vecs, axis=-1)
    pad = rows * LANES - flat.shape[-1]
    flat = jnp.pad(flat, [(0, 0)] * (flat.ndim - 1) + [(0, pad)])
    return flat.reshape(flat.shape[:-1] + (rows, LANES))


def _unpack(packed, sizes):
    flat = packed.reshape(packed.shape[:-2] + (-1,))
    out, pos = [], 0
    for n in sizes:
        out.append(flat[..., pos:pos + n])
        pos += n
    return out


def _pack_rows(sizes):
    total = sum(sizes)
    return -(-total // (LANES * SUBLANES)) * SUBLANES


BIG = {"sc_w_in": 2, "sc_w_out": 1, "lru_w_in": 2, "lru_w_gate": 3, "lru_w_out": 1, "ffn_w_up": 2, "ffn_w_down": 1}
SMALL = ["sc_conv_w", "lru_b_in", "lru_conv_w", "lru_conv_b", "lru_b_gate", "lru_lambda", "ffn_conv_w", "ln_g", "ln_b"]
REPL = ["sc_conv_b", "ffn_conv_b"]
WEIGHTS = ["sc_w_in", "sc_conv_w", "sc_conv_b", "sc_w_out", "lru_w_in", "lru_b_in", "lru_conv_w", "lru_conv_b",
           "lru_w_gate", "lru_b_gate", "lru_lambda", "lru_w_out", "ffn_w_up", "ffn_conv_w", "ffn_conv_b", "ffn_w_down",
           "ln_g", "ln_b"]


def _stage_big(g):
    i, j = g // 2, g // 4
    if g % 2:
        return [("ffn_w_up", i), ("ffn_w_down", i)]
    return [("sc_w_in", j), ("sc_w_out", j)] if i % 2 == 0 else [("lru_w_in", j), ("lru_w_gate", j), ("lru_w_out", j)]


def _step(x, loss_target, w, m, v):
    bsz, s, d = x.shape
    t = bsz * s
    depth = w["ffn_w_up"].shape[0]
    alpha = (2.0 * depth) ** 0.25
    heads = w["lru_w_gate"].shape[1]

    small_sizes = [w[k].size for k in SMALL]
    small_rows = _pack_rows(small_sizes)
    small_local = _pack([w[k].reshape(1, -1) for k in SMALL], small_rows)[0]
    me = 4 * lax.axis_index("x") + 2 * lax.axis_index("y") + lax.axis_index("c")

    def with_own(land, own):
        return lax.dynamic_update_slice_in_dim(land, own, me, axis=0)

    stages = 2 * depth
    gathers, tok = [], None
    for g in range(stages):
        arrs = [w[k][l].astype(BF16) for k, l in _stage_big(g)]
        if g == 0:
            arrs.append(small_local)
        if tok is not None:
            arrs[0] = arrs[0] + tok.astype(BF16)
        gathers.append(_exchange_start(arrs, [True] * len(arrs), name=f"gather_start_{g}"))
        tok = gathers[-1]["token"][0, 0]
    full = {k: [None] * w[k].shape[0] for k in BIG}
    full["sc_conv_b"] = w["sc_conv_b"]
    full["ffn_conv_b"] = w["ffn_conv_b"]

    def arrive(g, after):
        srcs, lands = _exchange_wait(gathers[g], after, name=f"gather_wait_{g}")
        for (k, l), src, land in zip(_stage_big(g), srcs, lands):
            full[k][l] = _whole(with_own(land, src[None]), BIG[k] - 1)
        if g == 0:
            for k, seg in zip(SMALL, _unpack(with_own(lands[-1], srcs[-1][None]), small_sizes)):
                full[k] = _whole(seg.reshape((N_DEV,) + w[k].shape), w[k].ndim - 1)

    xt = x.reshape(t, d)
    xb = xt.astype(BF16)
    saved = []
    for i in range(depth):
        j = i // 2
        arrive(2 * i, gathers[-1]["token"] if i == 0 else xb)
        lng, lnb = full["ln_g"][i], full["ln_b"][i]
        sv = {"x0": xb}
        if i % 2 == 0:
            hm = _mm(xb, full["sc_w_in"][j], name="sc_in")
            q = _sc_fwd(hm.reshape(bsz, s, -1), full["sc_conv_w"][j], full["sc_conv_b"][j:j + 1], name="sc_mix")
            w_out = full["sc_w_out"][j]
        else:
            hm = _mm(xb, full["lru_w_in"][j], bias=full["lru_b_in"][j:j + 1], name="lru_in")
            q, hs = _lru_fwd(hm.reshape(bsz, s, -1), full["lru_conv_w"][j], full["lru_conv_b"][j:j + 1],
                             full["lru_w_gate"][j], full["lru_b_gate"][j].reshape(heads, 1, -1),
                             full["lru_lambda"][j:j + 1], name="lru_mix")
            sv["hs"] = hs
            w_out = full["lru_w_out"][j]
        q = q.reshape(t, -1)
        arrive(2 * i + 1, q)
        z1, x1, x1b = _mm_ln(q, w_out, xt, alpha, lng[0:1], lnb[0:1], name="mix_out_ln")
        hg, hv, a = _ffn_fwd(x1b.reshape(bsz, s, d), full["ffn_w_up"][i], full["ffn_conv_w"][i],
                             full["ffn_conv_b"][i:i + 1], name="ffn_up_act")
        a = a.reshape(t, -1)
        z2, xt, xb = _mm_ln(a, full["ffn_w_down"][i], x1, alpha, lng[1:2], lnb[1:2], name="ffn_down_ln")
        sv.update(hm=hm, q=q, z1=z1, x1=x1b, hg=hg, hv=hv, a=a, z2=z2)
        saved.append(sv)

    sq, dx = _loss_head(xt, loss_target.reshape(t, d), name="loss_head")
    loss = lax.psum((0.5 / d) * sq[0, 0], MESH_AXES)

    grads = {k: [None] * w[k].shape[0] for k in WEIGHTS}
    scatters = [None] * stages

    def depart(g):
        send = [_slabs(grads[k][l], BIG[k] - 1).astype(BF16) for k, l in _stage_big(g)]
        scatters[g] = _exchange_start(send, [False] * len(send), name=f"scatter_start_{g}")
        return scatters[g]["token"][0:1, 0:1]

    dz2, dz2b, dg2, db2 = _ln_bwd(dx, saved[-1]["z2"], full["ln_g"][-1][1:2], name="ln_bwd")
    for i in reversed(range(depth)):
        j = i // 2
        sv = saved[i]
        lng = full["ln_g"][i]
        grads["ffn_w_down"][i] = _mm_tn(sv["a"], dz2b, name="ffn_down_dw")
        dhg, dhv, dwg, dwv, dbg, dbv = _ffn_bwd(sv["hg"], sv["hv"], dz2b.reshape(bsz, s, d), full["ffn_w_down"][i],
                                                full["ffn_conv_w"][i], full["ffn_conv_b"][i:i + 1], name="ffn_act_bwd")
        dhg, dhv = dhg.reshape(t, -1), dhv.reshape(t, -1)
        grads["ffn_conv_w"][i] = jnp.concatenate([dwg, dwv], axis=1)
        grads["ffn_conv_b"][i] = jnp.concatenate([dbg, dbv], axis=1)[0]
        grads["ffn_w_up"][i] = jnp.concatenate([_mm_tn(sv["x1"], dhg, name="ffn_up_dw_g"),
                                                _mm_tn(sv["x1"], dhv, name="ffn_up_dw_v")], axis=1)
        dz1, dz1b, dg1, db1 = _mm_ln_bwd([dhg, dhv], full["ffn_w_up"][i], dz2, alpha, sv["z1"],
                                         lng[0:1] + depart(2 * i + 1), name="ffn_up_dx_ln")
        grads["ln_g"][i] = jnp.concatenate([dg1, dg2], axis=0)
        grads["ln_b"][i] = jnp.concatenate([db1, db2], axis=0)
        if i % 2 == 0:
            dq = _mm(dz1b, full["sc_w_out"][j], trans_w=True, name="sc_out_dx")
            grads["sc_w_out"][j] = _mm_tn(sv["q"], dz1b, name="sc_out_dw")
            dhm, dcw, dcb = _sc_bwd(sv["hm"].reshape(bsz, s, -1), dq.reshape(bsz, s, -1), full["sc_conv_w"][j],
                                    full["sc_conv_b"][j:j + 1], name="sc_mix_bwd")
            dhm = dhm.reshape(t, -1)
            grads["sc_conv_w"][j] = dcw
            grads["sc_conv_b"][j] = dcb[0]
            grads["sc_w_in"][j] = _mm_tn(sv["x0"], dhm, name="sc_in_dw")
            w_in = full["sc_w_in"][j]
        else:
            dq = _mm(dz1b, full["lru_w_out"][j], trans_w=True, name="lru_out_dx")
            grads["lru_w_out"][j] = _mm_tn(sv["q"], dz1b, name="lru_out_dw")
            dhm, dcw, dcb, dwgt, dbgt, dlam, sgb, srb = _lru_bwd(
                sv["hm"].reshape(bsz, s, -1), sv["hs"], dq.reshape(bsz, s, -1), full["lru_conv_w"][j],
                full["lru_conv_b"][j:j + 1], full["lru_w_gate"][j], full["lru_b_gate"][j].reshape(heads, 1, -1),
                full["lru_lambda"][j:j + 1], name="lru_mix_bwd")
            dhm = dhm.reshape(t, -1)
            grads["lru_conv_w"][j] = dcw
            grads["lru_conv_b"][j] = dcb[0]
            grads["lru_w_gate"][j] = dwgt
            grads["lru_b_gate"][j] = dbgt[:, 0, :]
            grads["lru_lambda"][j] = dlam[0]
            grads["lru_b_in"][j] = jnp.concatenate([sgb, srb], axis=1)[0]
            grads["lru_w_in"][j] = _mm_tn(sv["x0"], dhm, name="lru_in_dw")
            w_in = full["lru_w_in"][j]
        tok = depart(2 * i)
        if i > 0:
            dz2, dz2b, dg2, db2 = _mm_ln_bwd([dhm], w_in, dz1, alpha, saved[i - 1]["z2"], full["ln_g"][i - 1][1:2] + tok,
                                             name="mix_in_dx_ln")
        else:
            dx = _mm(dhm, w_in + tok[0, 0].astype(BF16), trans_w=True, resid=dz1, resid_scale=alpha, name="mix_in_dx")
    grad_x = dx.reshape(bsz, s, d)

    gsm = {k: jnp.stack(grads[k]) for k in SMALL + REPL}
    small_send = _pack([_slabs(gsm[k], gsm[k].ndim - 1).reshape(N_DEV, -1) for k in SMALL], small_rows)
    repl_sizes = [w[k].size for k in REPL]
    repl_rows = _pack_rows(repl_sizes)
    repl_send = _pack([gsm[k].reshape(1, -1) for k in REPL], repl_rows)[0]
    small_scatter = _exchange_start([small_send, repl_send], [False, True], name="scatter_start_small")

    out = {}

    def own_slab(src):
        return lax.dynamic_slice_in_dim(src, me, 1, axis=0)

    stacks = {k: None for k in BIG}
    after = dx
    for g in reversed(range(stages)):
        srcs, lands = _exchange_wait(scatters[g], after, name=f"scatter_wait_{g}")
        for (k, l), src, land in zip(_stage_big(g), srcs, lands):
            n_l, c2 = w[k].shape[0], w[k].shape[-1]
            stacks[k] = _adamw(with_own(land, own_slab(src)).reshape(N_DEV, -1, c2), w[k].reshape(n_l, -1, c2),
                               m[k].reshape(n_l, -1, c2), v[k].reshape(n_l, -1, c2), l, stacks[k],
                               name=f"adamw_{k}_{l}")
            after = stacks[k][-1]
    for k in BIG:
        out[k] = [r.reshape(w[k].shape) for r in stacks[k]]
    srcs, lands = _exchange_wait(small_scatter, after, name="scatter_wait_small")
    got_small = with_own(lands[0], own_slab(srcs[0]))
    got_repl = with_own(lands[1], srcs[1][None])
    pk = lambda src, names, rows: _pack([src[k].reshape(1, -1) for k in names], rows)
    res = _adamw(got_small, small_local[None], pk(m, SMALL, small_rows), pk(v, SMALL, small_rows), 0, None,
                 name="adamw_small")
    for r_i, r in enumerate(res):
        for k, seg in zip(SMALL, _unpack(r[0], small_sizes)):
            out.setdefault(k, [None] * 4)[r_i] = seg.reshape(w[k].shape)
    res = _adamw(got_repl, pk(w, REPL, repl_rows), pk(m, REPL, repl_rows), pk(v, REPL, repl_rows), 0, None,
                 name="adamw_repl")
    for r_i, r in enumerate(res):
        for k, seg in zip(REPL, _unpack(r[0], repl_sizes)):
            out.setdefault(k, [None] * 4)[r_i] = seg.reshape(w[k].shape)

    return (loss, grad_x, *[out[k][0] for k in WEIGHTS], *[out[k][1] for k in WEIGHTS],
            *[out[k][2] for k in WEIGHTS], *[out[k][3] for k in WEIGHTS])


def kernel(x, sc_w_in, sc_conv_w, sc_conv_b, sc_w_out, lru_w_in, lru_b_in, lru_conv_w, lru_conv_b, lru_w_gate, lru_b_gate, lru_lambda, lru_w_out, ffn_w_up, ffn_conv_w, ffn_conv_b, ffn_w_down, ln_g, ln_b, loss_target, m_sc_w_in, m_sc_conv_w, m_sc_conv_b, m_sc_w_out, m_lru_w_in, m_lru_b_in, m_lru_conv_w, m_lru_conv_b, m_lru_w_gate, m_lru_b_gate, m_lru_lambda, m_lru_w_out, m_ffn_w_up, m_ffn_conv_w, m_ffn_conv_b, m_ffn_w_down, m_ln_g, m_ln_b, v_sc_w_in, v_sc_conv_w, v_sc_conv_b, v_sc_w_out, v_lru_w_in, v_lru_b_in, v_lru_conv_w, v_lru_conv_b, v_lru_w_gate, v_lru_b_gate, v_lru_lambda, v_lru_w_out, v_ffn_w_up, v_ffn_conv_w, v_ffn_conv_b, v_ffn_w_down, v_ln_g, v_ln_b):
    w = dict(sc_w_in=sc_w_in, sc_conv_w=sc_conv_w, sc_conv_b=sc_conv_b, sc_w_out=sc_w_out, lru_w_in=lru_w_in,
             lru_b_in=lru_b_in, lru_conv_w=lru_conv_w, lru_conv_b=lru_conv_b, lru_w_gate=lru_w_gate,
             lru_b_gate=lru_b_gate, lru_lambda=lru_lambda, lru_w_out=lru_w_out, ffn_w_up=ffn_w_up,
             ffn_conv_w=ffn_conv_w, ffn_conv_b=ffn_conv_b, ffn_w_down=ffn_w_down, ln_g=ln_g, ln_b=ln_b)
    m = dict(sc_w_in=m_sc_w_in, sc_conv_w=m_sc_conv_w, sc_conv_b=m_sc_conv_b, sc_w_out=m_sc_w_out, lru_w_in=m_lru_w_in,
             lru_b_in=m_lru_b_in, lru_conv_w=m_lru_conv_w, lru_conv_b=m_lru_conv_b, lru_w_gate=m_lru_w_gate,
             lru_b_gate=m_lru_b_gate, lru_lambda=m_lru_lambda, lru_w_out=m_lru_w_out, ffn_w_up=m_ffn_w_up,
             ffn_conv_w=m_ffn_conv_w, ffn_conv_b=m_ffn_conv_b, ffn_w_down=m_ffn_w_down, ln_g=m_ln_g, ln_b=m_ln_b)
    v = dict(sc_w_in=v_sc_w_in, sc_conv_w=v_sc_conv_w, sc_conv_b=v_sc_conv_b, sc_w_out=v_sc_w_out, lru_w_in=v_lru_w_in,
             lru_b_in=v_lru_b_in, lru_conv_w=v_lru_conv_w, lru_conv_b=v_lru_conv_b, lru_w_gate=v_lru_w_gate,
             lru_b_gate=v_lru_b_gate, lru_lambda=v_lru_lambda, lru_w_out=v_lru_w_out, ffn_w_up=v_ffn_w_up,
             ffn_conv_w=v_ffn_conv_w, ffn_conv_b=v_ffn_conv_b, ffn_w_down=v_ffn_w_down, ln_g=v_ln_g, ln_b=v_ln_b)
    return _step(x, loss_target, w, m, v)
```

```python
import functools
import math

import jax
import jax.numpy as jnp
from jax import lax
from jax.experimental import pallas as pl
from jax.experimental.pallas import tpu as pltpu

F32 = jnp.float32
BF16 = jnp.bfloat16

N_DEV = 8
MESH_AXES = ("x", "y", "c")
LANES = 128
SUBLANES = 8
VMEM_LIMIT = 56 * 1024 * 1024
MM_LHS_ELEMS = 3 * 1024 * 1024
MM_TN = 1536

LRU_C = 8.0
LN_EPS = 1e-5
ADAM_LR = 0.001
ADAM_B1 = 0.9
ADAM_B2 = 0.999
ADAM_EPS = 1e-08
ADAM_WD = 0.01
ADAM_STEP = 10
GELU_K = math.sqrt(2.0 / math.pi)
GELU_C = 0.044715


def _tile(n, target, align):
    if n <= target:
        return n
    t = (target // align) * align
    while t >= align:
        if n % t == 0:
            return t
        t -= align
    return n


def _params(sem):
    return pltpu.CompilerParams(dimension_semantics=sem, vmem_limit_bytes=VMEM_LIMIT)


def _rows(x):
    return lax.broadcasted_iota(jnp.int32, x.shape, 0)


def _shift_dn(x, k, fill=0.0):
    if k == 0:
        return x
    return jnp.where(_rows(x) >= k, pltpu.roll(x, k, 0), fill)


def _shift_up(x, k, fill=0.0):
    if k == 0:
        return x
    s = x.shape[0]
    return jnp.where(_rows(x) < s - k, pltpu.roll(x, s - k, 0), fill)


def _conv_fwd(x, w, b):
    kw = w.shape[0]
    y = _shift_dn(x, kw - 1) * w[0:1, :] + b
    for k in range(1, kw):
        y = y + _shift_dn(x, kw - 1 - k) * w[k:k + 1, :]
    return y


def _conv_bwd_x(dy, w):
    kw = w.shape[0]
    dx = _shift_up(dy, kw - 1) * w[0:1, :]
    for k in range(1, kw):
        dx = dx + _shift_up(dy, kw - 1 - k) * w[k:k + 1, :]
    return dx


def _conv_bwd_w(dy, x, kw):
    return [jnp.sum(dy * _shift_dn(x, kw - 1 - k), axis=0, keepdims=True) for k in range(kw)]


def _accumulate(first, items, cols=slice(None)):
    flat = []
    for ref, val in items:
        if isinstance(val, list):
            flat += [(ref, (slice(k, k + 1), cols), row) for k, row in enumerate(val)]
        else:
            flat.append((ref, Ellipsis, val))

    @pl.when(first)
    def _():
        for ref, idx, val in flat:
            ref[idx] = val

    @pl.when(jnp.logical_not(first))
    def _():
        for ref, idx, val in flat:
            ref[idx] += val


def _colsum(x):
    return jnp.sum(x, axis=0, keepdims=True)


def _sigmoid(x):
    return 1.0 / (1.0 + jnp.exp(-x))


def _log1p(x):
    u = 1.0 + x
    return jnp.where(u == 1.0, x, jnp.log(u) * (x / (u - 1.0)))


def _softplus(x):
    return jnp.maximum(x, 0.0) + _log1p(jnp.exp(-jnp.abs(x)))


def _expm1(x, ex):
    poly = x * (1.0 + x * (0.5 + x * (1.0 / 6.0 + x * (1.0 / 24.0 + x * (1.0 / 120.0 + x * (1.0 / 720.0))))))
    return jnp.where(jnp.abs(x) < 0.25, poly, ex - 1.0)


def _gelu(x):
    t = jnp.tanh(GELU_K * (x + GELU_C * x * x * x))
    return 0.5 * x * (1.0 + t)


def _gelu_and_grad(x):
    x2 = x * x
    t = jnp.tanh(GELU_K * (x + GELU_C * x * x2))
    g = 0.5 * x * (1.0 + t)
    dg = 0.5 * (1.0 + t) + 0.5 * x * (1.0 - t * t) * (GELU_K * (1.0 + 3.0 * GELU_C * x2))
    return g, dg


def _scan_fwd(a, b):
    s = a.shape[0]
    k = 1
    while k < s:
        last = 2 * k >= s
        if k % SUBLANES:
            b = a * _shift_dn(b, k) + b
            if not last:
                a = a * _shift_dn(a, k, 1.0)
        else:
            b = jnp.concatenate([b[:k], a[k:] * b[:s - k] + b[k:]], axis=0)
            if not last:
                a = jnp.concatenate([a[:k], a[k:] * a[:s - k]], axis=0)
        k *= 2
    return b


def _scan_rev(c, v):
    s = c.shape[0]
    k = 1
    while k < s:
        last = 2 * k >= s
        if k % SUBLANES:
            v = c * _shift_up(v, k) + v
            if not last:
                c = c * _shift_up(c, k, 1.0)
        else:
            v = jnp.concatenate([c[:s - k] * v[k:] + v[:s - k], v[s - k:]], axis=0)
            if not last:
                c = jnp.concatenate([c[:s - k] * c[k:], c[s - k:]], axis=0)
        k *= 2
    return v


def _mm(a, w, *, name, trans_w=False, bias=None, resid=None, resid_scale=1.0):
    m, k = a.shape
    n = w.shape[0] if trans_w else w.shape[1]
    tm = _tile(m, min(1024, max(256, MM_LHS_ELEMS // k)), SUBLANES)
    tn = _tile(n, MM_TN, LANES)
    has_bias = bias is not None
    has_resid = resid is not None

    def body(*refs):
        a_ref, w_ref = refs[0], refs[1]
        pos = 2
        b_ref = r_ref = None
        if has_bias:
            b_ref = refs[pos]
            pos += 1
        if has_resid:
            r_ref = refs[pos]
            pos += 1
        o_ref = refs[pos]

        cols = pl.ds(pl.multiple_of(pl.program_id(1) * tn, LANES), tn)
        if trans_w:
            acc = lax.dot_general(a_ref[...], w_ref[cols, :], (((1,), (1,)), ((), ())), preferred_element_type=F32)
        else:
            acc = jnp.dot(a_ref[...], w_ref[:, cols], preferred_element_type=F32)
        if has_bias:
            acc = acc + b_ref[...]
        if has_resid:
            acc = acc + resid_scale * r_ref[...]
        o_ref[...] = acc

    in_specs = [pl.BlockSpec((tm, k), lambda i, j: (i, 0)),
                pl.BlockSpec(w.shape, lambda i, j: (0, 0), pipeline_mode=pl.Buffered(1))]
    args = [a, w]
    if has_bias:
        in_specs.append(pl.BlockSpec((1, tn), lambda i, j: (0, j)))
        args.append(bias)
    if has_resid:
        in_specs.append(pl.BlockSpec((tm, tn), lambda i, j: (i, j)))
        args.append(resid)
    return pl.pallas_call(
        body, name=name, grid=(m // tm, n // tn), in_specs=in_specs,
        out_specs=pl.BlockSpec((tm, tn), lambda i, j: (i, j)),
        out_shape=jax.ShapeDtypeStruct((m, n), F32),
        compiler_params=_params(("parallel", "arbitrary")),
    )(*args)


def _mm_ln(a, w, resid, alpha, g, b, *, name, tm=512):
    m, k = a.shape
    d = w.shape[1]
    tm = _tile(m, tm, SUBLANES)

    def body(a_ref, w_ref, r_ref, g_ref, b_ref, z_ref, o_ref, obf_ref):
        y = jnp.dot(a_ref[...], w_ref[...], preferred_element_type=F32)
        z = alpha * r_ref[...] + y
        z_ref[...] = z
        mu = jnp.mean(z, axis=-1, keepdims=True)
        zc = z - mu
        var = jnp.mean(zc * zc, axis=-1, keepdims=True)
        o = zc * lax.rsqrt(var + LN_EPS) * g_ref[...] + b_ref[...]
        o_ref[...] = o
        obf_ref[...] = o.astype(BF16)

    row = pl.BlockSpec((tm, d), lambda i: (i, 0))
    vec = pl.BlockSpec((1, d), lambda i: (0, 0))
    return pl.pallas_call(
        body, name=name, grid=(m // tm,),
        in_specs=[pl.BlockSpec((tm, k), lambda i: (i, 0)),
                  pl.BlockSpec((k, d), lambda i: (0, 0), pipeline_mode=pl.Buffered(1)), row, vec, vec],
        out_specs=[row, row, row],
        out_shape=[jax.ShapeDtypeStruct((m, d), F32), jax.ShapeDtypeStruct((m, d), F32),
                   jax.ShapeDtypeStruct((m, d), BF16)],
        compiler_params=_params(("parallel",)),
    )(a, w, resid, g, b)


def _ln_bwd_math(do, z, g):
    mu = jnp.mean(z, axis=-1, keepdims=True)
    zc = z - mu
    var = jnp.mean(zc * zc, axis=-1, keepdims=True)
    rstd = lax.rsqrt(var + LN_EPS)
    xhat = zc * rstd
    dxh = do * g
    m1 = jnp.mean(dxh, axis=-1, keepdims=True)
    m2 = jnp.mean(dxh * xhat, axis=-1, keepdims=True)
    return rstd * (dxh - m1 - xhat * m2), _colsum(do * xhat), _colsum(do)


def _mm_ln_bwd(parts, w, resid, resid_scale, z, g, *, name):
    t, kp = parts[0].shape
    d, k = w.shape
    n = len(parts)
    tm = _tile(t, min(512, max(256, MM_LHS_ELEMS // k)), SUBLANES)

    def body(*refs):
        a_refs = refs[:n]
        w_ref, r_ref, z_ref, g_ref, dz_ref, dzbf_ref, dg_ref, db_ref = refs[n:]

        @pl.when(pl.program_id(0) == 0)
        def _():
            dg_ref[...] = jnp.zeros_like(dg_ref)
            db_ref[...] = jnp.zeros_like(db_ref)

        dx = resid_scale * r_ref[...]
        for p, a_ref in enumerate(a_refs):
            dx = dx + lax.dot_general(a_ref[...], w_ref[:, p * kp:(p + 1) * kp], (((1,), (1,)), ((), ())),
                                      preferred_element_type=F32)
        dz, dg, db = _ln_bwd_math(dx, z_ref[...], g_ref[...])
        dz_ref[...] = dz
        dzbf_ref[...] = dz.astype(BF16)
        dg_ref[...] += dg
        db_ref[...] += db

    row = pl.BlockSpec((tm, d), lambda i: (i, 0))
    vec = pl.BlockSpec((1, d), lambda i: (0, 0))
    return pl.pallas_call(
        body, name=name, grid=(t // tm,),
        in_specs=[pl.BlockSpec((tm, kp), lambda i: (i, 0))] * n
        + [pl.BlockSpec((d, k), lambda i: (0, 0), pipeline_mode=pl.Buffered(1)), row, row, vec],
        out_specs=[row, row, vec, vec],
        out_shape=[jax.ShapeDtypeStruct((t, d), F32), jax.ShapeDtypeStruct((t, d), BF16),
                   jax.ShapeDtypeStruct((1, d), F32), jax.ShapeDtypeStruct((1, d), F32)],
        compiler_params=_params(("arbitrary",)),
    )(*parts, w, resid, z, g)


def _mm_tn(a, b, *, name, tm=1408, tn=1536, tk=1024):
    t, m = a.shape
    n = b.shape[1]
    tm = _tile(m, tm, LANES)
    tn = _tile(n, tn, LANES)
    tk = _tile(t, tk, SUBLANES)

    def body(a_ref, b_ref, o_ref):
        @pl.when(pl.program_id(2) == 0)
        def _():
            o_ref[...] = jnp.zeros_like(o_ref)

        o_ref[...] += lax.dot_general(a_ref[...], b_ref[...], (((0,), (0,)), ((), ())), preferred_element_type=F32)

    return pl.pallas_call(
        body, name=name, grid=(m // tm, n // tn, t // tk),
        in_specs=[pl.BlockSpec((tk, tm), lambda i, j, l: (l, i)), pl.BlockSpec((tk, tn), lambda i, j, l: (l, j))],
        out_specs=pl.BlockSpec((tm, tn), lambda i, j, l: (i, j)),
        out_shape=jax.ShapeDtypeStruct((m, n), F32),
        compiler_params=_params(("parallel", "parallel", "arbitrary")),
    )(a, b)


def _ln_bwd(dout, z, g, *, name, tm=512):
    t, d = z.shape
    tm = _tile(t, tm, SUBLANES)

    def body(do_ref, z_ref, g_ref, dz_ref, dzbf_ref, dg_ref, db_ref):
        @pl.when(pl.program_id(0) == 0)
        def _():
            dg_ref[...] = jnp.zeros_like(dg_ref)
            db_ref[...] = jnp.zeros_like(db_ref)

        dz, dg, db = _ln_bwd_math(do_ref[...], z_ref[...], g_ref[...])
        dz_ref[...] = dz
        dzbf_ref[...] = dz.astype(BF16)
        dg_ref[...] += dg
        db_ref[...] += db

    row = pl.BlockSpec((tm, d), lambda i: (i, 0))
    vec = pl.BlockSpec((1, d), lambda i: (0, 0))
    return pl.pallas_call(
        body, name=name, grid=(t // tm,), in_specs=[row, row, vec], out_specs=[row, row, vec, vec],
        out_shape=[jax.ShapeDtypeStruct((t, d), F32), jax.ShapeDtypeStruct((t, d), BF16),
                   jax.ShapeDtypeStruct((1, d), F32), jax.ShapeDtypeStruct((1, d), F32)],
        compiler_params=_params(("arbitrary",)),
    )(dout, z, g)


def _loss_head(y, target, *, name, tm=512):
    t, d = y.shape
    tm = _tile(t, tm, SUBLANES)

    def body(y_ref, t_ref, s_ref, dy_ref):
        @pl.when(pl.program_id(0) == 0)
        def _():
            s_ref[...] = jnp.zeros_like(s_ref)

        e = y_ref[...] - t_ref[...]
        dy_ref[...] = e * (1.0 / d)
        s_ref[...] += jnp.sum(_colsum(e * e), axis=-1, keepdims=True)

    row = pl.BlockSpec((tm, d), lambda i: (i, 0))
    return pl.pallas_call(
        body, name=name, grid=(t // tm,), in_specs=[row, row],
        out_specs=[pl.BlockSpec((1, LANES), lambda i: (0, 0)), row],
        out_shape=[jax.ShapeDtypeStruct((1, LANES), F32), jax.ShapeDtypeStruct((t, d), F32)],
        compiler_params=_params(("arbitrary",)),
    )(y, target)


def _own(c, b, *_):
    return c, b


def _ahead(nc, bsz):
    def at(c, b, part):
        b2 = b + jnp.minimum(part, 1)
        return jnp.minimum(c + b2 // bsz, nc - 1), b2 % bsz
    return at


def _strip(s, tc, off, at=_own):
    def index(*ids):
        c, b = at(*ids)
        return b, 0, off + c
    return pl.BlockSpec((None, s, tc), index)


def _cvec(kw, tc, off, at=_own):
    def index(*ids):
        return 0, off + at(*ids)[0]
    return pl.BlockSpec((kw, tc), index)


def _acc(kw, tc):
    return pl.BlockSpec((kw, tc), lambda c, b, *_: (0, c))


def _sc_fwd(h, cw, cb, *, name, tc=256):
    bsz, s, d3 = h.shape
    d = d3 // 3
    tc = _tile(d, tc, LANES)
    nc = d // tc

    def body(gb_ref, gc_ref, v_ref, w_ref, b_ref, q_ref):
        u = _conv_fwd(gc_ref[...] * v_ref[...], w_ref[...], b_ref[...])
        q_ref[...] = (gb_ref[...] * u).astype(BF16)

    return pl.pallas_call(
        body, name=name, grid=(nc, bsz),
        in_specs=[_strip(s, tc, 0), _strip(s, tc, nc), _strip(s, tc, 2 * nc), _cvec(cw.shape[0], tc, 0), _cvec(1, tc, 0)],
        out_specs=_strip(s, tc, 0),
        out_shape=jax.ShapeDtypeStruct((bsz, s, d), BF16),
        compiler_params=_params(("parallel", "parallel")),
    )(h, h, h, cw, cb)


def _sc_bwd(h, dq, cw, cb, *, name, tc=256):
    bsz, s, d3 = h.shape
    d = d3 // 3
    kw = cw.shape[0]
    tc = _tile(d, tc, LANES)
    nc = d // tc

    def body(gb_ref, gc_ref, v_ref, dq_ref, w_ref, b_ref, dh_ref, dw_ref, db_ref, parts):
        b_id, part = pl.program_id(1), pl.program_id(2)

        @pl.when(part == 0)
        def _():
            gb, gc, v, dq_, w = gb_ref[...], gc_ref[...], v_ref[...], dq_ref[...], w_ref[...]
            p = gc * v
            u = _conv_fwd(p, w, b_ref[...])
            du = dq_ * gb
            dp = _conv_bwd_x(du, w)
            parts[0] = (dq_ * u).astype(BF16)
            parts[1] = (dp * v).astype(BF16)
            parts[2] = (dp * gc).astype(BF16)
            _accumulate(b_id == 0, [(dw_ref, _conv_bwd_w(du, p, kw)), (db_ref, _colsum(du))])

        dh_ref[...] = parts[part]

    at = _ahead(nc, bsz)
    return pl.pallas_call(
        body, name=name, grid=(nc, bsz, 3),
        in_specs=[_strip(s, tc, 0, at), _strip(s, tc, nc, at), _strip(s, tc, 2 * nc, at), _strip(s, tc, 0, at),
                  _cvec(kw, tc, 0, at), _cvec(1, tc, 0, at)],
        out_specs=[pl.BlockSpec((None, s, tc), lambda c, b, p: (b, 0, p * nc + c)), _acc(kw, tc), _acc(1, tc)],
        out_shape=[jax.ShapeDtypeStruct((bsz, s, d3), BF16), jax.ShapeDtypeStruct((kw, d), F32),
                   jax.ShapeDtypeStruct((1, d), F32)],
        scratch_shapes=[pltpu.VMEM((3, s, tc), BF16)],
        compiler_params=_params(("parallel", "arbitrary", "arbitrary")),
    )(h, h, h, dq, cw, cb)


def _ffn_specs(s, tc, nc, kw):
    strip = pl.BlockSpec((None, s, tc), lambda b, c: (b, 0, c))
    halves = [pl.BlockSpec((kw, tc), lambda b, c: (0, c)), pl.BlockSpec((kw, tc), lambda b, c: (0, nc + c)),
              pl.BlockSpec((1, tc), lambda b, c: (0, c)), pl.BlockSpec((1, tc), lambda b, c: (0, nc + c))]
    return strip, halves


def _ffn_fwd(x, w_up, cw, cb, *, name, tc=256):
    bsz, s, d = x.shape
    f = w_up.shape[1] // 2
    kw = cw.shape[0]
    tc = _tile(f, tc, LANES)
    nc = f // tc

    def body(x_ref, w_ref, wg_ref, wv_ref, bg_ref, bv_ref, hg_ref, hv_ref, g_ref, v_ref, a_ref):
        c0 = pl.multiple_of(pl.program_id(1) * tc, LANES)
        xs = x_ref[...]
        hg = jnp.dot(xs, w_ref[:, pl.ds(c0, tc)], preferred_element_type=F32)
        hv = jnp.dot(xs, w_ref[:, pl.ds(f + c0, tc)], preferred_element_type=F32)
        hg_ref[...] = hg
        hv_ref[...] = hv
        g = _conv_fwd(hg, wg_ref[...], bg_ref[...])
        v = _conv_fwd(hv, wv_ref[...], bv_ref[...])
        g_ref[...] = g
        v_ref[...] = v
        a_ref[...] = (g * _sigmoid(g) * v).astype(BF16)

    strip, halves = _ffn_specs(s, tc, nc, kw)
    return pl.pallas_call(
        body, name=name, grid=(bsz, nc),
        in_specs=[pl.BlockSpec((None, s, d), lambda b, c: (b, 0, 0)),
                  pl.BlockSpec(w_up.shape, lambda b, c: (0, 0), pipeline_mode=pl.Buffered(1))] + halves,
        out_specs=[strip] * 5,
        out_shape=[jax.ShapeDtypeStruct((bsz, s, f), F32)] * 4 + [jax.ShapeDtypeStruct((bsz, s, f), BF16)],
        compiler_params=_params(("parallel", "arbitrary")),
    )(x, w_up, cw, cw, cb, cb)


def _ffn_bwd(hg, hv, g, v, dz, w_down, cw, *, name, tc=256):
    bsz, s, f = hg.shape
    d = dz.shape[2]
    kw = cw.shape[0]
    tc = _tile(f, tc, LANES)
    nc = f // tc

    def body(hg_ref, hv_ref, g_ref, v_ref, dz_ref, wd_ref, wg_ref, wv_ref,
             dhg_ref, dhv_ref, dwg_ref, dwv_ref, dbg_ref, dbv_ref):
        c0 = pl.multiple_of(pl.program_id(1) * tc, LANES)
        cols = pl.ds(c0, tc)
        da = lax.dot_general(dz_ref[...], wd_ref[cols, :], (((1,), (1,)), ((), ())), preferred_element_type=F32)
        g_ = g_ref[...]
        sg = _sigmoid(g_)
        dv = da * (g_ * sg)
        dg = da * v_ref[...] * (sg * (1.0 + g_ * (1.0 - sg)))
        dhg_ref[...] = _conv_bwd_x(dg, wg_ref[...]).astype(BF16)
        dhv_ref[...] = _conv_bwd_x(dv, wv_ref[...]).astype(BF16)
        _accumulate(pl.program_id(0) == 0,
                    [(dwg_ref, _conv_bwd_w(dg, hg_ref[...], kw)), (dwv_ref, _conv_bwd_w(dv, hv_ref[...], kw)),
                     (dbg_ref, [_colsum(dg)]), (dbv_ref, [_colsum(dv)])], cols)

    strip, halves = _ffn_specs(s, tc, nc, kw)
    whole = lambda r: pl.BlockSpec((r, f), lambda b, c: (0, 0))
    return pl.pallas_call(
        body, name=name, grid=(bsz, nc),
        in_specs=[strip] * 4 + [pl.BlockSpec((None, s, d), lambda b, c: (b, 0, 0)),
                                pl.BlockSpec(w_down.shape, lambda b, c: (0, 0), pipeline_mode=pl.Buffered(1))]
        + halves[:2],
        out_specs=[strip, strip, whole(kw), whole(kw), whole(1), whole(1)],
        out_shape=[jax.ShapeDtypeStruct((bsz, s, f), BF16), jax.ShapeDtypeStruct((bsz, s, f), BF16),
                   jax.ShapeDtypeStruct((kw, f), F32), jax.ShapeDtypeStruct((kw, f), F32),
                   jax.ShapeDtypeStruct((1, f), F32), jax.ShapeDtypeStruct((1, f), F32)],
        compiler_params=_params(("arbitrary", "arbitrary")),
    )(hg, hv, g, v, dz, w_down, cw, cw)


def _lru_gates(r, cw, cb, wg, bg, lam):
    blk = r.shape[1]
    xr = _conv_fwd(r, cw, cb)
    gates = jnp.dot(xr.astype(BF16), wg, preferred_element_type=F32) + bg
    rg = _sigmoid(gates[:, :blk])
    ig = _sigmoid(gates[:, blk:])
    sp = _softplus(-lam)
    la = (-LRU_C * sp) * rg
    a = jnp.exp(la)
    mult = jnp.sqrt(-_expm1(2.0 * la, a * a))
    return xr, rg, ig, sp, a, mult


def _lru_fwd(h, cw, cb, wg, bg, lam, *, name):
    bsz, s, r2 = h.shape
    heads, blk = wg.shape[0], wg.shape[1]
    kw = cw.shape[0]

    def body(g_ref, r_ref, cw_ref, cb_ref, wg_ref, bg_ref, lam_ref, y_ref, sv_ref):
        xr, rg, ig, _, a, mult = _lru_gates(r_ref[...], cw_ref[...], cb_ref[...], wg_ref[...], bg_ref[...], lam_ref[...])
        hs = _scan_fwd(a, mult * (ig * xr))
        for n, val in enumerate((hs, xr, rg, ig, a, mult)):
            sv_ref[n] = val
        y_ref[...] = (hs * _gelu(g_ref[...])).astype(BF16)

    per_head = lambda hd, b: (hd, 0, 0)
    return pl.pallas_call(
        body, name=name, grid=(heads, bsz),
        in_specs=[_strip(s, blk, 0), _strip(s, blk, heads), _cvec(kw, blk, 0), _cvec(1, blk, 0),
                  pl.BlockSpec((None, blk, 2 * blk), per_head), pl.BlockSpec((None, 1, 2 * blk), per_head),
                  _cvec(1, blk, 0)],
        out_specs=[_strip(s, blk, 0), pl.BlockSpec((6, None, s, blk), lambda hd, b: (0, b, 0, hd))],
        out_shape=[jax.ShapeDtypeStruct((bsz, s, r2 // 2), BF16), jax.ShapeDtypeStruct((6, bsz, s, r2 // 2), F32)],
        compiler_params=_params(("parallel", "parallel")),
    )(h, h, cw, cb, wg, bg, lam)


def _lru_bwd(h, sv, dy, cw, wg, lam, *, name):
    bsz, s, r2 = h.shape
    rw = r2 // 2
    heads, blk = wg.shape[0], wg.shape[1]
    kw = cw.shape[0]

    def body(g_ref, r_ref, cw_ref, wg_ref, lam_ref, sv_ref, dy_ref,
             dh_ref, dcw_ref, dcb_ref, dwg_ref, dbg_ref, dlam_ref, sg_ref, sr_ref, parts):
        b_id, part = pl.program_id(1), pl.program_id(2)

        @pl.when(part == 0)
        def _():
            r, cw_, wg_, lam_ = r_ref[...], cw_ref[...], wg_ref[...], lam_ref[...]
            hs_, xr, rg, ig, a, mult = (sv_ref[n] for n in range(6))
            sp = _softplus(-lam_)
            dy_ = dy_ref[...]
            gel, dgel = _gelu_and_grad(g_ref[...])
            dg = dy_ * hs_ * dgel
            lmb = _scan_rev(_shift_up(a, 1, 1.0), dy_ * gel)
            da = lmb * _shift_dn(hs_, 1)
            dmult = lmb * (ig * xr)
            dig = lmb * (mult * xr)
            dxr = lmb * (mult * ig)
            dla = da * a - dmult * (a * a / mult)
            drg = dla * (-LRU_C * sp)
            dsp = _colsum(dla * rg) * (-LRU_C)
            dlam = -dsp * _sigmoid(-lam_)
            dgates = jnp.concatenate([drg * (rg * (1.0 - rg)), dig * (ig * (1.0 - ig))], axis=1)
            dgates_bf = dgates.astype(BF16)
            dwg = lax.dot_general(xr.astype(BF16), dgates_bf, (((0,), (0,)), ((), ())), preferred_element_type=F32)
            dxr = dxr + lax.dot_general(dgates_bf, wg_, (((1,), (1,)), ((), ())), preferred_element_type=F32)
            dr = _conv_bwd_x(dxr, cw_)
            parts[0] = dg.astype(BF16)
            parts[1] = dr.astype(BF16)
            _accumulate(b_id == 0, [(dcw_ref, _conv_bwd_w(dxr, r, kw)), (dcb_ref, _colsum(dxr)), (dwg_ref, dwg),
                                    (dbg_ref, _colsum(dgates)), (dlam_ref, dlam), (sg_ref, _colsum(dg)),
                                    (sr_ref, _colsum(dr))])

        dh_ref[...] = parts[part]

    at = _ahead(heads, bsz)

    def saved(*ids):
        hd, b = at(*ids)
        return 0, b, 0, hd

    vec = pl.BlockSpec((1, blk), lambda hd, b, p: (0, hd))
    return pl.pallas_call(
        body, name=name, grid=(heads, bsz, 2),
        in_specs=[_strip(s, blk, 0, at), _strip(s, blk, heads, at), _cvec(kw, blk, 0, at),
                  pl.BlockSpec((None, blk, 2 * blk), lambda *ids: (at(*ids)[0], 0, 0)), _cvec(1, blk, 0, at),
                  pl.BlockSpec((6, None, s, blk), saved), _strip(s, blk, 0, at)],
        out_specs=[pl.BlockSpec((None, s, blk), lambda hd, b, p: (b, 0, p * heads + hd)),
                   pl.BlockSpec((kw, blk), lambda hd, b, p: (0, hd)), vec,
                   pl.BlockSpec((None, blk, 2 * blk), lambda hd, b, p: (hd, 0, 0)),
                   pl.BlockSpec((None, 1, 2 * blk), lambda hd, b, p: (hd, 0, 0)), vec, vec, vec],
        out_shape=[jax.ShapeDtypeStruct((bsz, s, r2), BF16), jax.ShapeDtypeStruct((kw, rw), F32),
                   jax.ShapeDtypeStruct((1, rw), F32), jax.ShapeDtypeStruct((heads, blk, 2 * blk), F32),
                   jax.ShapeDtypeStruct((heads, 1, 2 * blk), F32), jax.ShapeDtypeStruct((1, rw), F32),
                   jax.ShapeDtypeStruct((1, rw), F32), jax.ShapeDtypeStruct((1, rw), F32)],
        scratch_shapes=[pltpu.VMEM((2, s, blk), BF16)],
        compiler_params=_params(("parallel", "arbitrary", "arbitrary")),
    )(h, h, cw, wg, lam, sv, dy)


HBM_SPEC = pl.BlockSpec(memory_space=pltpu.HBM)
SEM_SPEC = pl.BlockSpec(memory_space=pltpu.SEMAPHORE)
EFFECT = pltpu.SideEffectType.DATAFLOW_SIDE_EFFECTING


def _peer_copies(srcs, lands, gather, send_sem, recv_sem):
    x, y, c = (lax.axis_index(ax) for ax in MESH_AXES)
    me = 4 * x + 2 * y + c
    copies = []
    for i in range(len(srcs)):
        for d in range(1, N_DEV):
            px = 1 - x if d & 4 else x
            py = 1 - y if d & 2 else y
            pc = 1 - c if d & 1 else c
            src = srcs[i] if gather[i] else srcs[i].at[4 * px + 2 * py + pc]
            k = i * (N_DEV - 1) + d - 1
            copies.append(pltpu.make_async_remote_copy(
                src_ref=src, dst_ref=lands[i].at[me], send_sem=send_sem.at[k], recv_sem=recv_sem.at[k],
                device_id=(px, py, pc), device_id_type=pl.DeviceIdType.MESH))
    return copies


def _exchange_start(arrs, gather, *, name):
    n = len(arrs)
    lands = [lax.empty((N_DEV,) + tuple(a.shape if g else a.shape[1:]), a.dtype) for a, g in zip(arrs, gather)]

    def body(*refs):
        srcs, land_refs = refs[:n], refs[n:2 * n]
        send_sem, recv_sem = refs[2 * n], refs[2 * n + 1]
        token = refs[-1]
        for cp in _peer_copies(srcs, land_refs, gather, send_sem, recv_sem):
            cp.start()
        token[...] = jnp.zeros_like(token)

    sems = pltpu.SemaphoreType.DMA((n * (N_DEV - 1),))
    thru = [pltpu.HBM(a.shape, a.dtype) for a in arrs + lands]
    out = pl.pallas_call(
        body, name=name, in_specs=[HBM_SPEC] * (2 * n),
        out_shape=(sems, sems, *thru, jax.ShapeDtypeStruct((SUBLANES, LANES), F32)),
        out_specs=(SEM_SPEC, SEM_SPEC, *([HBM_SPEC] * (2 * n)), pl.BlockSpec(memory_space=pltpu.VMEM)),
        input_output_aliases={i: 2 + i for i in range(2 * n)},
        compiler_params=pltpu.CompilerParams(has_side_effects=EFFECT),
    )(*[pltpu.with_memory_space_constraint(a, pltpu.HBM) for a in arrs + lands])
    return {"send_sem": out[0], "recv_sem": out[1], "srcs": list(out[2:2 + n]), "lands": list(out[2 + n:2 + 2 * n]),
            "token": out[-1], "gather": list(gather)}


def _exchange_wait(handle, after, *, name):
    srcs, lands, gather = handle["srcs"], handle["lands"], handle["gather"]
    n = len(srcs)

    def body(*refs):
        src_refs, land_refs = refs[:n], refs[n:2 * n]
        send_sem, recv_sem = refs[2 * n], refs[2 * n + 1]
        for cp in _peer_copies(src_refs, land_refs, gather, send_sem, recv_sem):
            cp.wait_send()
            cp.wait_recv()

    out = pl.pallas_call(
        body, name=name,
        in_specs=[HBM_SPEC] * (2 * n) + [SEM_SPEC, SEM_SPEC, pl.BlockSpec(memory_space=pl.ANY)],
        out_shape=tuple(pltpu.HBM(a.shape, a.dtype) for a in srcs + lands), out_specs=tuple([HBM_SPEC] * (2 * n)),
        input_output_aliases={i: i for i in range(2 * n)},
        compiler_params=pltpu.CompilerParams(has_side_effects=EFFECT),
    )(*srcs, *lands, handle["send_sem"], handle["recv_sem"], after)
    return list(out[:n]), list(out[n:])


def _adamw(parts, w, m, v, layer, so_far, *, name, tr=256):
    n_layers, r, c = w.shape
    tr = _tile(r, tr, SUBLANES)
    bc1 = 1.0 / (1.0 - ADAM_B1 ** ADAM_STEP)
    bc2 = 1.0 / (1.0 - ADAM_B2 ** ADAM_STEP)
    if so_far is None:
        so_far = [lax.empty(w.shape, F32) for _ in range(4)]

    def body(p_ref, w_ref, m_ref, v_ref, *rest):
        g_ref, d_ref, mo_ref, vo_ref = rest[4:]
        g = p_ref[0].astype(F32)
        for s in range(1, N_DEV):
            g = g + p_ref[s].astype(F32)
        m_new = ADAM_B1 * m_ref[...] + (1.0 - ADAM_B1) * g
        v_new = ADAM_B2 * v_ref[...] + (1.0 - ADAM_B2) * (g * g)
        g_ref[...] = g
        mo_ref[...] = m_new
        vo_ref[...] = v_new
        d_ref[...] = -ADAM_LR * ((m_new * bc1) / (jnp.sqrt(v_new * bc2) + ADAM_EPS) + ADAM_WD * w_ref[...])

    blk = pl.BlockSpec((None, tr, c), lambda i: (layer, i, 0))
    return pl.pallas_call(
        body, name=name, grid=(r // tr,),
        in_specs=[pl.BlockSpec((N_DEV, tr, c), lambda i: (0, i, 0)), blk, blk, blk]
        + [pl.BlockSpec(memory_space=pl.ANY)] * 4,
        out_specs=[blk] * 4, out_shape=[jax.ShapeDtypeStruct(w.shape, F32)] * 4,
        input_output_aliases={4 + o: o for o in range(4)},
        compiler_params=_params(("parallel",)),
    )(parts, w, m, v, *so_far)


def _whole(slabs, axis):
    x = jnp.moveaxis(slabs, 0, axis)
    shp = x.shape
    return x.reshape(shp[:axis] + (shp[axis] * shp[axis + 1],) + shp[axis + 2:])


def _slabs(whole, axis):
    shp = whole.shape
    x = whole.reshape(shp[:axis] + (N_DEV, shp[axis] // N_DEV) + shp[axis + 1:])
    return jnp.moveaxis(x, axis, 0)


def _pack(vecs, rows):
    flat = jnp.concatenate(vecs, axis=-1)
    pad = rows * LANES - flat.shape[-1]
    flat = jnp.pad(flat, [(0, 0)] * (flat.ndim - 1) + [(0, pad)])
    return flat.reshape(flat.shape[:-1] + (rows, LANES))


def _unpack(packed, sizes):
    flat = packed.reshape(packed.shape[:-2] + (-1,))
    out, pos = [], 0
    for n in sizes:
        out.append(flat[..., pos:pos + n])
        pos += n
    return out


def _pack_rows(sizes):
    total = sum(sizes)
    return -(-total // (LANES * SUBLANES)) * SUBLANES


BIG = {"sc_w_in": 2, "sc_w_out": 1, "lru_w_in": 2, "lru_w_gate": 3, "lru_w_out": 1, "ffn_w_up": 2, "ffn_w_down": 1}
SMALL = ["sc_conv_w", "lru_b_in", "lru_conv_w", "lru_conv_b", "lru_b_gate", "lru_lambda", "ffn_conv_w", "ln_g", "ln_b"]
REPL = ["sc_conv_b", "ffn_conv_b"]
WEIGHTS = ["sc_w_in", "sc_conv_w", "sc_conv_b", "sc_w_out", "lru_w_in", "lru_b_in", "lru_conv_w", "lru_conv_b",
           "lru_w_gate", "lru_b_gate", "lru_lambda", "lru_w_out", "ffn_w_up", "ffn_conv_w", "ffn_conv_b", "ffn_w_down",
           "ln_g", "ln_b"]


def _stage_big(g):
    i, j = g // 2, g // 4
    if g % 2:
        return [("ffn_w_up", i), ("ffn_w_down", i)]
    return [("sc_w_in", j), ("sc_w_out", j)] if i % 2 == 0 else [("lru_w_in", j), ("lru_w_gate", j), ("lru_w_out", j)]


def _step(x, loss_target, w, m, v):
    bsz, s, d = x.shape
    t = bsz * s
    depth = w["ffn_w_up"].shape[0]
    alpha = (2.0 * depth) ** 0.25
    heads = w["lru_w_gate"].shape[1]

    small_sizes = [w[k].size for k in SMALL]
    small_rows = _pack_rows(small_sizes)
    small_local = _pack([w[k].reshape(1, -1) for k in SMALL], small_rows)[0]
    me = 4 * lax.axis_index("x") + 2 * lax.axis_index("y") + lax.axis_index("c")

    def with_own(land, own):
        return lax.dynamic_update_slice_in_dim(land, own, me, axis=0)

    stages = 2 * depth
    gathers, tok = [], None
    for g in range(stages):
        arrs = [w[k][l].astype(BF16) for k, l in _stage_big(g)]
        if g == 0:
            arrs.append(small_local)
        if tok is not None:
            arrs[0] = arrs[0] + tok.astype(BF16)
        gathers.append(_exchange_start(arrs, [True] * len(arrs), name=f"gather_start_{g}"))
        tok = gathers[-1]["token"][0, 0]
    full = {k: [None] * w[k].shape[0] for k in BIG}
    full["sc_conv_b"] = w["sc_conv_b"]
    full["ffn_conv_b"] = w["ffn_conv_b"]

    def arrive(g, after):
        srcs, lands = _exchange_wait(gathers[g], after, name=f"gather_wait_{g}")
        for (k, l), src, land in zip(_stage_big(g), srcs, lands):
            full[k][l] = _whole(with_own(land, src[None]), BIG[k] - 1)
        if g == 0:
            for k, seg in zip(SMALL, _unpack(with_own(lands[-1], srcs[-1][None]), small_sizes)):
                full[k] = _whole(seg.reshape((N_DEV,) + w[k].shape), w[k].ndim - 1)

    xt = x.reshape(t, d)
    xb = xt.astype(BF16)
    saved = []
    for i in range(depth):
        j = i // 2
        arrive(2 * i, gathers[-1]["token"] if i == 0 else xb)
        lng, lnb = full["ln_g"][i], full["ln_b"][i]
        sv = {"x0": xb}
        if i % 2 == 0:
            hm = _mm(xb, full["sc_w_in"][j], name="sc_in")
            q = _sc_fwd(hm.reshape(bsz, s, -1), full["sc_conv_w"][j], full["sc_conv_b"][j:j + 1], name="sc_mix")
            w_out = full["sc_w_out"][j]
        else:
            hm = _mm(xb, full["lru_w_in"][j], bias=full["lru_b_in"][j:j + 1], name="lru_in")
            q, hs = _lru_fwd(hm.reshape(bsz, s, -1), full["lru_conv_w"][j], full["lru_conv_b"][j:j + 1],
                             full["lru_w_gate"][j], full["lru_b_gate"][j].reshape(heads, 1, -1),
                             full["lru_lambda"][j:j + 1], name="lru_mix")
            sv["hs"] = hs
            w_out = full["lru_w_out"][j]
        q = q.reshape(t, -1)
        arrive(2 * i + 1, q)
        z1, x1, x1b = _mm_ln(q, w_out, xt, alpha, lng[0:1], lnb[0:1], name="mix_out_ln")
        hg, hv, gc, vc, a = _ffn_fwd(x1b.reshape(bsz, s, d), full["ffn_w_up"][i], full["ffn_conv_w"][i],
                                     full["ffn_conv_b"][i:i + 1], name="ffn_up_act")
        a = a.reshape(t, -1)
        z2, xt, xb = _mm_ln(a, full["ffn_w_down"][i], x1, alpha, lng[1:2], lnb[1:2], name="ffn_down_ln")
        sv.update(hm=hm, q=q, z1=z1, x1=x1b, ffn=(hg, hv, gc, vc), a=a, z2=z2)
        saved.append(sv)

    sq, dx = _loss_head(xt, loss_target.reshape(t, d), name="loss_head")
    loss = lax.psum((0.5 / d) * sq[0, 0], MESH_AXES)

    grads = {k: [None] * w[k].shape[0] for k in WEIGHTS}
    scatters = [None] * stages

    def depart(g):
        send = [_slabs(grads[k][l], BIG[k] - 1).astype(BF16) for k, l in _stage_big(g)]
        scatters[g] = _exchange_start(send, [False] * len(send), name=f"scatter_start_{g}")
        return scatters[g]["token"][0:1, 0:1]

    dz2, dz2b, dg2, db2 = _ln_bwd(dx, saved[-1]["z2"], full["ln_g"][-1][1:2], name="ln_bwd")
    for i in reversed(range(depth)):
        j = i // 2
        sv = saved[i]
        lng = full["ln_g"][i]
        grads["ffn_w_down"][i] = _mm_tn(sv["a"], dz2b, name="ffn_down_dw")
        dhg, dhv, dwg, dwv, dbg, dbv = _ffn_bwd(*sv["ffn"], dz2b.reshape(bsz, s, d), full["ffn_w_down"][i],
                                                full["ffn_conv_w"][i], name="ffn_act_bwd")
        dhg, dhv = dhg.reshape(t, -1), dhv.reshape(t, -1)
        grads["ffn_conv_w"][i] = jnp.concatenate([dwg, dwv], axis=1)
        grads["ffn_conv_b"][i] = jnp.concatenate([dbg, dbv], axis=1)[0]
        grads["ffn_w_up"][i] = jnp.concatenate([_mm_tn(sv["x1"], dhg, name="ffn_up_dw_g"),
                                                _mm_tn(sv["x1"], dhv, name="ffn_up_dw_v")], axis=1)
        dz1, dz1b, dg1, db1 = _mm_ln_bwd([dhg, dhv], full["ffn_w_up"][i], dz2, alpha, sv["z1"],
                                         lng[0:1] + depart(2 * i + 1), name="ffn_up_dx_ln")
        grads["ln_g"][i] = jnp.concatenate([dg1, dg2], axis=0)
        grads["ln_b"][i] = jnp.concatenate([db1, db2], axis=0)
        if i % 2 == 0:
            dq = _mm(dz1b, full["sc_w_out"][j], trans_w=True, name="sc_out_dx")
            grads["sc_w_out"][j] = _mm_tn(sv["q"], dz1b, name="sc_out_dw")
            dhm, dcw, dcb = _sc_bwd(sv["hm"].reshape(bsz, s, -1), dq.reshape(bsz, s, -1), full["sc_conv_w"][j],
                                    full["sc_conv_b"][j:j + 1], name="sc_mix_bwd")
            dhm = dhm.reshape(t, -1)
            grads["sc_conv_w"][j] = dcw
            grads["sc_conv_b"][j] = dcb[0]
            grads["sc_w_in"][j] = _mm_tn(sv["x0"], dhm, name="sc_in_dw")
            w_in = full["sc_w_in"][j]
        else:
            dq = _mm(dz1b, full["lru_w_out"][j], trans_w=True, name="lru_out_dx")
            grads["lru_w_out"][j] = _mm_tn(sv["q"], dz1b, name="lru_out_dw")
            dhm, dcw, dcb, dwgt, dbgt, dlam, sgb, srb = _lru_bwd(
                sv["hm"].reshape(bsz, s, -1), sv["hs"], dq.reshape(bsz, s, -1), full["lru_conv_w"][j],
                full["lru_w_gate"][j], full["lru_lambda"][j:j + 1], name="lru_mix_bwd")
            dhm = dhm.reshape(t, -1)
            grads["lru_conv_w"][j] = dcw
            grads["lru_conv_b"][j] = dcb[0]
            grads["lru_w_gate"][j] = dwgt
            grads["lru_b_gate"][j] = dbgt[:, 0, :]
            grads["lru_lambda"][j] = dlam[0]
            grads["lru_b_in"][j] = jnp.concatenate([sgb, srb], axis=1)[0]
            grads["lru_w_in"][j] = _mm_tn(sv["x0"], dhm, name="lru_in_dw")
            w_in = full["lru_w_in"][j]
        tok = depart(2 * i)
        if i > 0:
            dz2, dz2b, dg2, db2 = _mm_ln_bwd([dhm], w_in, dz1, alpha, saved[i - 1]["z2"], full["ln_g"][i - 1][1:2] + tok,
                                             name="mix_in_dx_ln")
        else:
            dx = _mm(dhm, w_in + tok[0, 0].astype(BF16), trans_w=True, resid=dz1, resid_scale=alpha, name="mix_in_dx")
    grad_x = dx.reshape(bsz, s, d)

    gsm = {k: jnp.stack(grads[k]) for k in SMALL + REPL}
    small_send = _pack([_slabs(gsm[k], gsm[k].ndim - 1).reshape(N_DEV, -1) for k in SMALL], small_rows)
    repl_sizes = [w[k].size for k in REPL]
    repl_rows = _pack_rows(repl_sizes)
    repl_send = _pack([gsm[k].reshape(1, -1) for k in REPL], repl_rows)[0]
    small_scatter = _exchange_start([small_send, repl_send], [False, True], name="scatter_start_small")

    out = {}

    def own_slab(src):
        return lax.dynamic_slice_in_dim(src, me, 1, axis=0)

    stacks = {k: None for k in BIG}
    after = dx
    for g in reversed(range(stages)):
        srcs, lands = _exchange_wait(scatters[g], after, name=f"scatter_wait_{g}")
        for (k, l), src, land in zip(_stage_big(g), srcs, lands):
            n_l, c2 = w[k].shape[0], w[k].shape[-1]
            stacks[k] = _adamw(with_own(land, own_slab(src)).reshape(N_DEV, -1, c2), w[k].reshape(n_l, -1, c2),
                               m[k].reshape(n_l, -1, c2), v[k].reshape(n_l, -1, c2), l, stacks[k],
                               name=f"adamw_{k}_{l}")
            after = stacks[k][-1]
    for k in BIG:
        out[k] = [r.reshape(w[k].shape) for r in stacks[k]]
    srcs, lands = _exchange_wait(small_scatter, after, name="scatter_wait_small")
    got_small = with_own(lands[0], own_slab(srcs[0]))
    got_repl = with_own(lands[1], srcs[1][None])
    pk = lambda src, names, rows: _pack([src[k].reshape(1, -1) for k in names], rows)
    res = _adamw(got_small, small_local[None], pk(m, SMALL, small_rows), pk(v, SMALL, small_rows), 0, None,
                 name="adamw_small")
    for r_i, r in enumerate(res):
        for k, seg in zip(SMALL, _unpack(r[0], small_sizes)):
            out.setdefault(k, [None] * 4)[r_i] = seg.reshape(w[k].shape)
    res = _adamw(got_repl, pk(w, REPL, repl_rows), pk(m, REPL, repl_rows), pk(v, REPL, repl_rows), 0, None,
                 name="adamw_repl")
    for r_i, r in enumerate(res):
        for k, seg in zip(REPL, _unpack(r[0], repl_sizes)):
            out.setdefault(k, [None] * 4)[r_i] = seg.reshape(w[k].shape)

    return (loss, grad_x, *[out[k][0] for k in WEIGHTS], *[out[k][1] for k in WEIGHTS],
            *[out[k][2] for k in WEIGHTS], *[out[k][3] for k in WEIGHTS])


def kernel(x, sc_w_in, sc_conv_w, sc_conv_b, sc_w_out, lru_w_in, lru_b_in, lru_conv_w, lru_conv_b, lru_w_gate, lru_b_gate, lru_lambda, lru_w_out, ffn_w_up, ffn_conv_w, ffn_conv_b, ffn_w_down, ln_g, ln_b, loss_target, m_sc_w_in, m_sc_conv_w, m_sc_conv_b, m_sc_w_out, m_lru_w_in, m_lru_b_in, m_lru_conv_w, m_lru_conv_b, m_lru_w_gate, m_lru_b_gate, m_lru_lambda, m_lru_w_out, m_ffn_w_up, m_ffn_conv_w, m_ffn_conv_b, m_ffn_w_down, m_ln_g, m_ln_b, v_sc_w_in, v_sc_conv_w, v_sc_conv_b, v_sc_w_out, v_lru_w_in, v_lru_b_in, v_lru_conv_w, v_lru_conv_b, v_lru_w_gate, v_lru_b_gate, v_lru_lambda, v_lru_w_out, v_ffn_w_up, v_ffn_conv_w, v_ffn_conv_b, v_ffn_w_down, v_ln_g, v_ln_b):
    w = dict(sc_w_in=sc_w_in, sc_conv_w=sc_conv_w, sc_conv_b=sc_conv_b, sc_w_out=sc_w_out, lru_w_in=lru_w_in,
             lru_b_in=lru_b_in, lru_conv_w=lru_conv_w, lru_conv_b=lru_conv_b, lru_w_gate=lru_w_gate,
             lru_b_gate=lru_b_gate, lru_lambda=lru_lambda, lru_w_out=lru_w_out, ffn_w_up=ffn_w_up,
             ffn_conv_w=ffn_conv_w, ffn_conv_b=ffn_conv_b, ffn_w_down=ffn_w_down, ln_g=ln_g, ln_b=ln_b)
    m = dict(sc_w_in=m_sc_w_in, sc_conv_w=m_sc_conv_w, sc_conv_b=m_sc_conv_b, sc_w_out=m_sc_w_out, lru_w_in=m_lru_w_in,
             lru_b_in=m_lru_b_in, lru_conv_w=m_lru_conv_w, lru_conv_b=m_lru_conv_b, lru_w_gate=m_lru_w_gate,
             lru_b_gate=m_lru_b_gate, lru_lambda=m_lru_lambda, lru_w_out=m_lru_w_out, ffn_w_up=m_ffn_w_up,
             ffn_conv_w=m_ffn_conv_w, ffn_conv_b=m_ffn_conv_b, ffn_w_down=m_ffn_w_down, ln_g=m_ln_g, ln_b=m_ln_b)
    v = dict(sc_w_in=v_sc_w_in, sc_conv_w=v_sc_conv_w, sc_conv_b=v_sc_conv_b, sc_w_out=v_sc_w_out, lru_w_in=v_lru_w_in,
             lru_b_in=v_lru_b_in, lru_conv_w=v_lru_conv_w, lru_conv_b=v_lru_conv_b, lru_w_gate=v_lru_w_gate,
             lru_b_gate=v_lru_b_gate, lru_lambda=v_lru_lambda, lru_w_out=v_lru_w_out, ffn_w_up=v_ffn_w_up,
             ffn_conv_w=v_ffn_conv_w, ffn_conv_b=v_ffn_conv_b, ffn_w_down=v_ffn_w_down, ln_g=v_ln_g, ln_b=v_ln_b)
    return _step(x, loss_target, w, m, v)
```

```python
import functools
import math

import jax
import jax.numpy as jnp
from jax import lax
from jax.experimental import pallas as pl
from jax.experimental.pallas import tpu as pltpu

F32 = jnp.float32
BF16 = jnp.bfloat16

N_DEV = 8
MESH_AXES = ("x", "y", "c")
LANES = 128
SUBLANES = 8
VMEM_LIMIT = 56 * 1024 * 1024
MM_LHS_ELEMS = 3 * 1024 * 1024
MM_TN = 1536

LRU_C = 8.0
LN_EPS = 1e-5
ADAM_LR = 0.001
ADAM_B1 = 0.9
ADAM_B2 = 0.999
ADAM_EPS = 1e-08
ADAM_WD = 0.01
ADAM_STEP = 10
GELU_K = math.sqrt(2.0 / math.pi)
GELU_C = 0.044715


def _tile(n, target, align):
    if n <= target:
        return n
    t = (target // align) * align
    while t >= align:
        if n % t == 0:
            return t
        t -= align
    return n


def _params(sem):
    return pltpu.CompilerParams(dimension_semantics=sem, vmem_limit_bytes=VMEM_LIMIT)


def _rows(x):
    return lax.broadcasted_iota(jnp.int32, x.shape, 0)


def _shift_dn(x, k, fill=0.0):
    if k == 0:
        return x
    return jnp.where(_rows(x) >= k, pltpu.roll(x, k, 0), fill)


def _shift_up(x, k, fill=0.0):
    if k == 0:
        return x
    s = x.shape[0]
    return jnp.where(_rows(x) < s - k, pltpu.roll(x, s - k, 0), fill)


def _conv_fwd(x, w, b):
    kw = w.shape[0]
    y = _shift_dn(x, kw - 1) * w[0:1, :] + b
    for k in range(1, kw):
        y = y + _shift_dn(x, kw - 1 - k) * w[k:k + 1, :]
    return y


def _conv_bwd(dy, x, w):
    kw = w.shape[0]
    ahead = [_shift_up(dy, j) for j in range(kw)]
    dx = ahead[kw - 1] * w[0:1, :]
    for k in range(1, kw):
        dx = dx + ahead[kw - 1 - k] * w[k:k + 1, :]
    return dx, [_colsum(ahead[kw - 1 - k] * x) for k in range(kw)]


def _accumulate(first, items, cols=slice(None)):
    flat = []
    for ref, val in items:
        if isinstance(val, list):
            flat += [(ref, (slice(k, k + 1), cols), row) for k, row in enumerate(val)]
        else:
            flat.append((ref, Ellipsis, val))

    @pl.when(first)
    def _():
        for ref, idx, val in flat:
            ref[idx] = val

    @pl.when(jnp.logical_not(first))
    def _():
        for ref, idx, val in flat:
            ref[idx] += val


def _colsum(x):
    return jnp.sum(x, axis=0, keepdims=True)


def _sigmoid(x):
    return 1.0 / (1.0 + jnp.exp(-x))


def _log1p(x):
    u = 1.0 + x
    return jnp.where(u == 1.0, x, jnp.log(u) * (x / (u - 1.0)))


def _softplus(x):
    return jnp.maximum(x, 0.0) + _log1p(jnp.exp(-jnp.abs(x)))


def _expm1(x, ex):
    poly = x * (1.0 + x * (0.5 + x * (1.0 / 6.0 + x * (1.0 / 24.0 + x * (1.0 / 120.0 + x * (1.0 / 720.0))))))
    return jnp.where(jnp.abs(x) < 0.25, poly, ex - 1.0)


def _gelu(x):
    t = jnp.tanh(GELU_K * (x + GELU_C * x * x * x))
    return 0.5 * x * (1.0 + t)


def _gelu_and_grad(x):
    x2 = x * x
    t = jnp.tanh(GELU_K * (x + GELU_C * x * x2))
    g = 0.5 * x * (1.0 + t)
    dg = 0.5 * (1.0 + t) + 0.5 * x * (1.0 - t * t) * (GELU_K * (1.0 + 3.0 * GELU_C * x2))
    return g, dg


def _scan_fwd(a, b):
    s = a.shape[0]
    k = 1
    while k < s:
        last = 2 * k >= s
        if k % SUBLANES:
            b = a * _shift_dn(b, k) + b
            if not last:
                a = a * _shift_dn(a, k, 1.0)
        else:
            b = jnp.concatenate([b[:k], a[k:] * b[:s - k] + b[k:]], axis=0)
            if not last:
                a = jnp.concatenate([a[:k], a[k:] * a[:s - k]], axis=0)
        k *= 2
    return b


def _scan_rev(c, v):
    s = c.shape[0]
    k = 1
    while k < s:
        last = 2 * k >= s
        if k % SUBLANES:
            v = c * _shift_up(v, k) + v
            if not last:
                c = c * _shift_up(c, k, 1.0)
        else:
            v = jnp.concatenate([c[:s - k] * v[k:] + v[:s - k], v[s - k:]], axis=0)
            if not last:
                c = jnp.concatenate([c[:s - k] * c[k:], c[s - k:]], axis=0)
        k *= 2
    return v


def _mm(a, w, *, name, trans_w=False, bias=None, resid=None, resid_scale=1.0):
    m, k = a.shape
    n = w.shape[0] if trans_w else w.shape[1]
    tm = _tile(m, min(1024, max(256, MM_LHS_ELEMS // k)), SUBLANES)
    tn = _tile(n, MM_TN, LANES)
    has_bias = bias is not None
    has_resid = resid is not None

    def body(*refs):
        a_ref, w_ref = refs[0], refs[1]
        pos = 2
        b_ref = r_ref = None
        if has_bias:
            b_ref = refs[pos]
            pos += 1
        if has_resid:
            r_ref = refs[pos]
            pos += 1
        o_ref = refs[pos]

        cols = pl.ds(pl.multiple_of(pl.program_id(1) * tn, LANES), tn)
        if trans_w:
            acc = lax.dot_general(a_ref[...], w_ref[cols, :], (((1,), (1,)), ((), ())), preferred_element_type=F32)
        else:
            acc = jnp.dot(a_ref[...], w_ref[:, cols], preferred_element_type=F32)
        if has_bias:
            acc = acc + b_ref[...]
        if has_resid:
            acc = acc + resid_scale * r_ref[...]
        o_ref[...] = acc

    in_specs = [pl.BlockSpec((tm, k), lambda i, j: (i, 0)),
                pl.BlockSpec(w.shape, lambda i, j: (0, 0), pipeline_mode=pl.Buffered(1))]
    args = [a, w]
    if has_bias:
        in_specs.append(pl.BlockSpec((1, tn), lambda i, j: (0, j)))
        args.append(bias)
    if has_resid:
        in_specs.append(pl.BlockSpec((tm, tn), lambda i, j: (i, j)))
        args.append(resid)
    return pl.pallas_call(
        body, name=name, grid=(m // tm, n // tn), in_specs=in_specs,
        out_specs=pl.BlockSpec((tm, tn), lambda i, j: (i, j)),
        out_shape=jax.ShapeDtypeStruct((m, n), F32),
        compiler_params=_params(("parallel", "arbitrary")),
    )(*args)


def _mm_ln(a, w, resid, alpha, g, b, *, name, tm=512):
    m, k = a.shape
    d = w.shape[1]
    tm = _tile(m, tm, SUBLANES)

    def body(a_ref, w_ref, r_ref, g_ref, b_ref, z_ref, o_ref, obf_ref):
        y = jnp.dot(a_ref[...], w_ref[...], preferred_element_type=F32)
        z = alpha * r_ref[...] + y
        z_ref[...] = z
        mu = jnp.mean(z, axis=-1, keepdims=True)
        zc = z - mu
        var = jnp.mean(zc * zc, axis=-1, keepdims=True)
        o = zc * lax.rsqrt(var + LN_EPS) * g_ref[...] + b_ref[...]
        o_ref[...] = o
        obf_ref[...] = o.astype(BF16)

    row = pl.BlockSpec((tm, d), lambda i: (i, 0))
    vec = pl.BlockSpec((1, d), lambda i: (0, 0))
    return pl.pallas_call(
        body, name=name, grid=(m // tm,),
        in_specs=[pl.BlockSpec((tm, k), lambda i: (i, 0)),
                  pl.BlockSpec((k, d), lambda i: (0, 0), pipeline_mode=pl.Buffered(1)), row, vec, vec],
        out_specs=[row, row, row],
        out_shape=[jax.ShapeDtypeStruct((m, d), F32), jax.ShapeDtypeStruct((m, d), F32),
                   jax.ShapeDtypeStruct((m, d), BF16)],
        compiler_params=_params(("parallel",)),
    )(a, w, resid, g, b)


def _ln_bwd_math(do, z, g):
    mu = jnp.mean(z, axis=-1, keepdims=True)
    zc = z - mu
    var = jnp.mean(zc * zc, axis=-1, keepdims=True)
    rstd = lax.rsqrt(var + LN_EPS)
    xhat = zc * rstd
    dxh = do * g
    m1 = jnp.mean(dxh, axis=-1, keepdims=True)
    m2 = jnp.mean(dxh * xhat, axis=-1, keepdims=True)
    return rstd * (dxh - m1 - xhat * m2), _colsum(do * xhat), _colsum(do)


def _mm_ln_bwd(parts, w, resid, resid_scale, z, g, *, name):
    t, kp = parts[0].shape
    d, k = w.shape
    n = len(parts)
    tm = _tile(t, min(512, max(256, MM_LHS_ELEMS // k)), SUBLANES)

    def body(*refs):
        a_refs = refs[:n]
        w_ref, r_ref, z_ref, g_ref, dz_ref, dzbf_ref, dg_ref, db_ref = refs[n:]

        @pl.when(pl.program_id(0) == 0)
        def _():
            dg_ref[...] = jnp.zeros_like(dg_ref)
            db_ref[...] = jnp.zeros_like(db_ref)

        dx = resid_scale * r_ref[...]
        for p, a_ref in enumerate(a_refs):
            dx = dx + lax.dot_general(a_ref[...], w_ref[:, p * kp:(p + 1) * kp], (((1,), (1,)), ((), ())),
                                      preferred_element_type=F32)
        dz, dg, db = _ln_bwd_math(dx, z_ref[...], g_ref[...])
        dz_ref[...] = dz
        dzbf_ref[...] = dz.astype(BF16)
        dg_ref[...] += dg
        db_ref[...] += db

    row = pl.BlockSpec((tm, d), lambda i: (i, 0))
    vec = pl.BlockSpec((1, d), lambda i: (0, 0))
    return pl.pallas_call(
        body, name=name, grid=(t // tm,),
        in_specs=[pl.BlockSpec((tm, kp), lambda i: (i, 0))] * n
        + [pl.BlockSpec((d, k), lambda i: (0, 0), pipeline_mode=pl.Buffered(1)), row, row, vec],
        out_specs=[row, row, vec, vec],
        out_shape=[jax.ShapeDtypeStruct((t, d), F32), jax.ShapeDtypeStruct((t, d), BF16),
                   jax.ShapeDtypeStruct((1, d), F32), jax.ShapeDtypeStruct((1, d), F32)],
        compiler_params=_params(("arbitrary",)),
    )(*parts, w, resid, z, g)


def _mm_tn(a, b, *, name, tm=1408, tn=1536, tk=1024):
    t, m = a.shape
    n = b.shape[1]
    tm = _tile(m, tm, LANES)
    tn = _tile(n, tn, LANES)
    tk = _tile(t, tk, SUBLANES)

    def body(a_ref, b_ref, o_ref):
        @pl.when(pl.program_id(2) == 0)
        def _():
            o_ref[...] = jnp.zeros_like(o_ref)

        o_ref[...] += lax.dot_general(a_ref[...], b_ref[...], (((0,), (0,)), ((), ())), preferred_element_type=F32)

    return pl.pallas_call(
        body, name=name, grid=(m // tm, n // tn, t // tk),
        in_specs=[pl.BlockSpec((tk, tm), lambda i, j, l: (l, i)), pl.BlockSpec((tk, tn), lambda i, j, l: (l, j))],
        out_specs=pl.BlockSpec((tm, tn), lambda i, j, l: (i, j)),
        out_shape=jax.ShapeDtypeStruct((m, n), F32),
        compiler_params=_params(("parallel", "parallel", "arbitrary")),
    )(a, b)


def _ln_bwd(dout, z, g, *, name, tm=512):
    t, d = z.shape
    tm = _tile(t, tm, SUBLANES)

    def body(do_ref, z_ref, g_ref, dz_ref, dzbf_ref, dg_ref, db_ref):
        @pl.when(pl.program_id(0) == 0)
        def _():
            dg_ref[...] = jnp.zeros_like(dg_ref)
            db_ref[...] = jnp.zeros_like(db_ref)

        dz, dg, db = _ln_bwd_math(do_ref[...], z_ref[...], g_ref[...])
        dz_ref[...] = dz
        dzbf_ref[...] = dz.astype(BF16)
        dg_ref[...] += dg
        db_ref[...] += db

    row = pl.BlockSpec((tm, d), lambda i: (i, 0))
    vec = pl.BlockSpec((1, d), lambda i: (0, 0))
    return pl.pallas_call(
        body, name=name, grid=(t // tm,), in_specs=[row, row, vec], out_specs=[row, row, vec, vec],
        out_shape=[jax.ShapeDtypeStruct((t, d), F32), jax.ShapeDtypeStruct((t, d), BF16),
                   jax.ShapeDtypeStruct((1, d), F32), jax.ShapeDtypeStruct((1, d), F32)],
        compiler_params=_params(("arbitrary",)),
    )(dout, z, g)


def _loss_head(y, target, *, name, tm=512):
    t, d = y.shape
    tm = _tile(t, tm, SUBLANES)

    def body(y_ref, t_ref, s_ref, dy_ref):
        @pl.when(pl.program_id(0) == 0)
        def _():
            s_ref[...] = jnp.zeros_like(s_ref)

        e = y_ref[...] - t_ref[...]
        dy_ref[...] = e * (1.0 / d)
        s_ref[...] += jnp.sum(_colsum(e * e), axis=-1, keepdims=True)

    row = pl.BlockSpec((tm, d), lambda i: (i, 0))
    return pl.pallas_call(
        body, name=name, grid=(t // tm,), in_specs=[row, row],
        out_specs=[pl.BlockSpec((1, LANES), lambda i: (0, 0)), row],
        out_shape=[jax.ShapeDtypeStruct((1, LANES), F32), jax.ShapeDtypeStruct((t, d), F32)],
        compiler_params=_params(("arbitrary",)),
    )(y, target)


def _own(c, b, *_):
    return c, b


def _ahead(nc, bsz):
    def at(c, b, part):
        b2 = b + jnp.minimum(part, 1)
        return jnp.minimum(c + b2 // bsz, nc - 1), b2 % bsz
    return at


def _strip(s, tc, off, at=_own):
    def index(*ids):
        c, b = at(*ids)
        return b, 0, off + c
    return pl.BlockSpec((None, s, tc), index)


def _cvec(kw, tc, off, at=_own):
    def index(*ids):
        return 0, off + at(*ids)[0]
    return pl.BlockSpec((kw, tc), index)


def _acc(kw, tc):
    return pl.BlockSpec((kw, tc), lambda c, b, *_: (0, c))


def _sc_fwd(h, cw, cb, *, name, tc=256):
    bsz, s, d3 = h.shape
    d = d3 // 3
    tc = _tile(d, tc, LANES)
    nc = d // tc

    def body(gb_ref, gc_ref, v_ref, w_ref, b_ref, q_ref):
        u = _conv_fwd(gc_ref[...] * v_ref[...], w_ref[...], b_ref[...])
        q_ref[...] = (gb_ref[...] * u).astype(BF16)

    return pl.pallas_call(
        body, name=name, grid=(nc, bsz),
        in_specs=[_strip(s, tc, 0), _strip(s, tc, nc), _strip(s, tc, 2 * nc), _cvec(cw.shape[0], tc, 0), _cvec(1, tc, 0)],
        out_specs=_strip(s, tc, 0),
        out_shape=jax.ShapeDtypeStruct((bsz, s, d), BF16),
        compiler_params=_params(("parallel", "parallel")),
    )(h, h, h, cw, cb)


def _sc_bwd(h, dq, cw, cb, *, name, tc=256):
    bsz, s, d3 = h.shape
    d = d3 // 3
    kw = cw.shape[0]
    tc = _tile(d, tc, LANES)
    nc = d // tc

    def body(gb_ref, gc_ref, v_ref, dq_ref, w_ref, b_ref, dh_ref, dw_ref, db_ref, parts):
        b_id, part = pl.program_id(1), pl.program_id(2)

        @pl.when(part == 0)
        def _():
            gb, gc, v, dq_, w = gb_ref[...], gc_ref[...], v_ref[...], dq_ref[...], w_ref[...]
            p = gc * v
            u = _conv_fwd(p, w, b_ref[...])
            du = dq_ * gb
            dp, dw_rows = _conv_bwd(du, p, w)
            parts[0] = (dq_ * u).astype(BF16)
            parts[1] = (dp * v).astype(BF16)
            parts[2] = (dp * gc).astype(BF16)
            _accumulate(b_id == 0, [(dw_ref, dw_rows), (db_ref, _colsum(du))])

        dh_ref[...] = parts[part]

    at = _ahead(nc, bsz)
    return pl.pallas_call(
        body, name=name, grid=(nc, bsz, 3),
        in_specs=[_strip(s, tc, 0, at), _strip(s, tc, nc, at), _strip(s, tc, 2 * nc, at), _strip(s, tc, 0, at),
                  _cvec(kw, tc, 0, at), _cvec(1, tc, 0, at)],
        out_specs=[pl.BlockSpec((None, s, tc), lambda c, b, p: (b, 0, p * nc + c)), _acc(kw, tc), _acc(1, tc)],
        out_shape=[jax.ShapeDtypeStruct((bsz, s, d3), BF16), jax.ShapeDtypeStruct((kw, d), F32),
                   jax.ShapeDtypeStruct((1, d), F32)],
        scratch_shapes=[pltpu.VMEM((3, s, tc), BF16)],
        compiler_params=_params(("parallel", "arbitrary", "arbitrary")),
    )(h, h, h, dq, cw, cb)


def _ffn_specs(s, tc, nc, kw):
    strip = pl.BlockSpec((None, s, tc), lambda b, c: (b, 0, c))
    halves = [pl.BlockSpec((kw, tc), lambda b, c: (0, c)), pl.BlockSpec((kw, tc), lambda b, c: (0, nc + c)),
              pl.BlockSpec((1, tc), lambda b, c: (0, c)), pl.BlockSpec((1, tc), lambda b, c: (0, nc + c))]
    return strip, halves


def _ffn_fwd(x, w_up, cw, cb, *, name, tc=256):
    bsz, s, d = x.shape
    f = w_up.shape[1] // 2
    kw = cw.shape[0]
    tc = _tile(f, tc, LANES)
    nc = f // tc

    def body(x_ref, w_ref, wg_ref, wv_ref, bg_ref, bv_ref, hg_ref, hv_ref, g_ref, v_ref, a_ref):
        c0 = pl.multiple_of(pl.program_id(1) * tc, LANES)
        xs = x_ref[...]
        hg = jnp.dot(xs, w_ref[:, pl.ds(c0, tc)], preferred_element_type=F32)
        hv = jnp.dot(xs, w_ref[:, pl.ds(f + c0, tc)], preferred_element_type=F32)
        hg_ref[...] = hg
        hv_ref[...] = hv
        g = _conv_fwd(hg, wg_ref[...], bg_ref[...])
        v = _conv_fwd(hv, wv_ref[...], bv_ref[...])
        g_ref[...] = g
        v_ref[...] = v
        a_ref[...] = (g * _sigmoid(g) * v).astype(BF16)

    strip, halves = _ffn_specs(s, tc, nc, kw)
    return pl.pallas_call(
        body, name=name, grid=(bsz, nc),
        in_specs=[pl.BlockSpec((None, s, d), lambda b, c: (b, 0, 0)),
                  pl.BlockSpec(w_up.shape, lambda b, c: (0, 0), pipeline_mode=pl.Buffered(1))] + halves,
        out_specs=[strip] * 5,
        out_shape=[jax.ShapeDtypeStruct((bsz, s, f), F32)] * 4 + [jax.ShapeDtypeStruct((bsz, s, f), BF16)],
        compiler_params=_params(("parallel", "arbitrary")),
    )(x, w_up, cw, cw, cb, cb)


def _ffn_bwd(hg, hv, g, v, dz, w_down, cw, *, name, tc=256):
    bsz, s, f = hg.shape
    d = dz.shape[2]
    kw = cw.shape[0]
    tc = _tile(f, tc, LANES)
    nc = f // tc

    def body(hg_ref, hv_ref, g_ref, v_ref, dz_ref, wd_ref, wg_ref, wv_ref,
             dhg_ref, dhv_ref, dwg_ref, dwv_ref, dbg_ref, dbv_ref):
        c0 = pl.multiple_of(pl.program_id(1) * tc, LANES)
        cols = pl.ds(c0, tc)
        da = lax.dot_general(dz_ref[...], wd_ref[cols, :], (((1,), (1,)), ((), ())), preferred_element_type=F32)
        g_ = g_ref[...]
        sg = _sigmoid(g_)
        dv = da * (g_ * sg)
        dg = da * v_ref[...] * (sg * (1.0 + g_ * (1.0 - sg)))
        dhg, dwg_rows = _conv_bwd(dg, hg_ref[...], wg_ref[...])
        dhv, dwv_rows = _conv_bwd(dv, hv_ref[...], wv_ref[...])
        dhg_ref[...] = dhg.astype(BF16)
        dhv_ref[...] = dhv.astype(BF16)
        _accumulate(pl.program_id(0) == 0, [(dwg_ref, dwg_rows), (dwv_ref, dwv_rows),
                                            (dbg_ref, [_colsum(dg)]), (dbv_ref, [_colsum(dv)])], cols)

    strip, halves = _ffn_specs(s, tc, nc, kw)
    whole = lambda r: pl.BlockSpec((r, f), lambda b, c: (0, 0))
    return pl.pallas_call(
        body, name=name, grid=(bsz, nc),
        in_specs=[strip] * 4 + [pl.BlockSpec((None, s, d), lambda b, c: (b, 0, 0)),
                                pl.BlockSpec(w_down.shape, lambda b, c: (0, 0), pipeline_mode=pl.Buffered(1))]
        + halves[:2],
        out_specs=[strip, strip, whole(kw), whole(kw), whole(1), whole(1)],
        out_shape=[jax.ShapeDtypeStruct((bsz, s, f), BF16), jax.ShapeDtypeStruct((bsz, s, f), BF16),
                   jax.ShapeDtypeStruct((kw, f), F32), jax.ShapeDtypeStruct((kw, f), F32),
                   jax.ShapeDtypeStruct((1, f), F32), jax.ShapeDtypeStruct((1, f), F32)],
        compiler_params=_params(("arbitrary", "arbitrary")),
    )(hg, hv, g, v, dz, w_down, cw, cw)


def _lru_gates(r, cw, cb, wg, bg, lam):
    blk = r.shape[1]
    xr = _conv_fwd(r, cw, cb)
    gates = jnp.dot(xr.astype(BF16), wg, preferred_element_type=F32) + bg
    rg = _sigmoid(gates[:, :blk])
    ig = _sigmoid(gates[:, blk:])
    sp = _softplus(-lam)
    la = (-LRU_C * sp) * rg
    a = jnp.exp(la)
    mult = jnp.sqrt(-_expm1(2.0 * la, a * a))
    return xr, rg, ig, sp, a, mult


def _lru_fwd(h, cw, cb, wg, bg, lam, *, name):
    bsz, s, r2 = h.shape
    heads, blk = wg.shape[0], wg.shape[1]
    kw = cw.shape[0]

    def body(g_ref, r_ref, cw_ref, cb_ref, wg_ref, bg_ref, lam_ref, y_ref, sv_ref):
        xr, rg, ig, _, a, mult = _lru_gates(r_ref[...], cw_ref[...], cb_ref[...], wg_ref[...], bg_ref[...], lam_ref[...])
        hs = _scan_fwd(a, mult * (ig * xr))
        for n, val in enumerate((hs, xr, rg, ig, a, mult)):
            sv_ref[n] = val
        y_ref[...] = (hs * _gelu(g_ref[...])).astype(BF16)

    per_head = lambda hd, b: (hd, 0, 0)
    return pl.pallas_call(
        body, name=name, grid=(heads, bsz),
        in_specs=[_strip(s, blk, 0), _strip(s, blk, heads), _cvec(kw, blk, 0), _cvec(1, blk, 0),
                  pl.BlockSpec((None, blk, 2 * blk), per_head), pl.BlockSpec((None, 1, 2 * blk), per_head),
                  _cvec(1, blk, 0)],
        out_specs=[_strip(s, blk, 0), pl.BlockSpec((6, None, s, blk), lambda hd, b: (0, b, 0, hd))],
        out_shape=[jax.ShapeDtypeStruct((bsz, s, r2 // 2), BF16), jax.ShapeDtypeStruct((6, bsz, s, r2 // 2), F32)],
        compiler_params=_params(("parallel", "parallel")),
    )(h, h, cw, cb, wg, bg, lam)


def _lru_bwd(h, sv, dy, cw, wg, lam, *, name):
    bsz, s, r2 = h.shape
    rw = r2 // 2
    heads, blk = wg.shape[0], wg.shape[1]
    kw = cw.shape[0]

    def body(g_ref, r_ref, cw_ref, wg_ref, lam_ref, sv_ref, dy_ref,
             dh_ref, dcw_ref, dcb_ref, dwg_ref, dbg_ref, dlam_ref, sg_ref, sr_ref, parts):
        b_id, part = pl.program_id(1), pl.program_id(2)

        @pl.when(part == 0)
        def _():
            r, cw_, wg_, lam_ = r_ref[...], cw_ref[...], wg_ref[...], lam_ref[...]
            hs_, xr, rg, ig, a, mult = (sv_ref[n] for n in range(6))
            sp = _softplus(-lam_)
            dy_ = dy_ref[...]
            gel, dgel = _gelu_and_grad(g_ref[...])
            dg = dy_ * hs_ * dgel
            lmb = _scan_rev(_shift_up(a, 1, 1.0), dy_ * gel)
            da = lmb * _shift_dn(hs_, 1)
            dmult = lmb * (ig * xr)
            dig = lmb * (mult * xr)
            dxr = lmb * (mult * ig)
            dla = da * a - dmult * (a * a / mult)
            drg = dla * (-LRU_C * sp)
            dsp = _colsum(dla * rg) * (-LRU_C)
            dlam = -dsp * _sigmoid(-lam_)
            dgates = jnp.concatenate([drg * (rg * (1.0 - rg)), dig * (ig * (1.0 - ig))], axis=1)
            dgates_bf = dgates.astype(BF16)
            dwg = lax.dot_general(xr.astype(BF16), dgates_bf, (((0,), (0,)), ((), ())), preferred_element_type=F32)
            dxr = dxr + lax.dot_general(dgates_bf, wg_, (((1,), (1,)), ((), ())), preferred_element_type=F32)
            dr, dcw_rows = _conv_bwd(dxr, r, cw_)
            parts[0] = dg.astype(BF16)
            parts[1] = dr.astype(BF16)
            _accumulate(b_id == 0, [(dcw_ref, dcw_rows), (dcb_ref, _colsum(dxr)), (dwg_ref, dwg),
                                    (dbg_ref, _colsum(dgates)), (dlam_ref, dlam), (sg_ref, _colsum(dg)),
                                    (sr_ref, _colsum(dr))])

        dh_ref[...] = parts[part]

    at = _ahead(heads, bsz)

    def saved(*ids):
        hd, b = at(*ids)
        return 0, b, 0, hd

    vec = pl.BlockSpec((1, blk), lambda hd, b, p: (0, hd))
    return pl.pallas_call(
        body, name=name, grid=(heads, bsz, 2),
        in_specs=[_strip(s, blk, 0, at), _strip(s, blk, heads, at), _cvec(kw, blk, 0, at),
                  pl.BlockSpec((None, blk, 2 * blk), lambda *ids: (at(*ids)[0], 0, 0)), _cvec(1, blk, 0, at),
                  pl.BlockSpec((6, None, s, blk), saved), _strip(s, blk, 0, at)],
        out_specs=[pl.BlockSpec((None, s, blk), lambda hd, b, p: (b, 0, p * heads + hd)),
                   pl.BlockSpec((kw, blk), lambda hd, b, p: (0, hd)), vec,
                   pl.BlockSpec((None, blk, 2 * blk), lambda hd, b, p: (hd, 0, 0)),
                   pl.BlockSpec((None, 1, 2 * blk), lambda hd, b, p: (hd, 0, 0)), vec, vec, vec],
        out_shape=[jax.ShapeDtypeStruct((bsz, s, r2), BF16), jax.ShapeDtypeStruct((kw, rw), F32),
                   jax.ShapeDtypeStruct((1, rw), F32), jax.ShapeDtypeStruct((heads, blk, 2 * blk), F32),
                   jax.ShapeDtypeStruct((heads, 1, 2 * blk), F32), jax.ShapeDtypeStruct((1, rw), F32),
                   jax.ShapeDtypeStruct((1, rw), F32), jax.ShapeDtypeStruct((1, rw), F32)],
        scratch_shapes=[pltpu.VMEM((2, s, blk), BF16)],
        compiler_params=_params(("parallel", "arbitrary", "arbitrary")),
    )(h, h, cw, wg, lam, sv, dy)


HBM_SPEC = pl.BlockSpec(memory_space=pltpu.HBM)
SEM_SPEC = pl.BlockSpec(memory_space=pltpu.SEMAPHORE)
EFFECT = pltpu.SideEffectType.DATAFLOW_SIDE_EFFECTING


def _peer_copies(srcs, lands, gather, send_sem, recv_sem):
    x, y, c = (lax.axis_index(ax) for ax in MESH_AXES)
    me = 4 * x + 2 * y + c
    copies = []
    for i in range(len(srcs)):
        for d in range(1, N_DEV):
            px = 1 - x if d & 4 else x
            py = 1 - y if d & 2 else y
            pc = 1 - c if d & 1 else c
            src = srcs[i] if gather[i] else srcs[i].at[4 * px + 2 * py + pc]
            k = i * (N_DEV - 1) + d - 1
            copies.append(pltpu.make_async_remote_copy(
                src_ref=src, dst_ref=lands[i].at[me], send_sem=send_sem.at[k], recv_sem=recv_sem.at[k],
                device_id=(px, py, pc), device_id_type=pl.DeviceIdType.MESH))
    return copies


def _exchange_start(arrs, gather, *, name):
    n = len(arrs)
    lands = [lax.empty((N_DEV,) + tuple(a.shape if g else a.shape[1:]), a.dtype) for a, g in zip(arrs, gather)]

    def body(*refs):
        srcs, land_refs = refs[:n], refs[n:2 * n]
        send_sem, recv_sem = refs[2 * n], refs[2 * n + 1]
        token = refs[-1]
        for cp in _peer_copies(srcs, land_refs, gather, send_sem, recv_sem):
            cp.start()
        token[...] = jnp.zeros_like(token)

    sems = pltpu.SemaphoreType.DMA((n * (N_DEV - 1),))
    thru = [pltpu.HBM(a.shape, a.dtype) for a in arrs + lands]
    out = pl.pallas_call(
        body, name=name, in_specs=[HBM_SPEC] * (2 * n),
        out_shape=(sems, sems, *thru, jax.ShapeDtypeStruct((SUBLANES, LANES), F32)),
        out_specs=(SEM_SPEC, SEM_SPEC, *([HBM_SPEC] * (2 * n)), pl.BlockSpec(memory_space=pltpu.VMEM)),
        input_output_aliases={i: 2 + i for i in range(2 * n)},
        compiler_params=pltpu.CompilerParams(has_side_effects=EFFECT),
    )(*[pltpu.with_memory_space_constraint(a, pltpu.HBM) for a in arrs + lands])
    return {"send_sem": out[0], "recv_sem": out[1], "srcs": list(out[2:2 + n]), "lands": list(out[2 + n:2 + 2 * n]),
            "token": out[-1], "gather": list(gather)}


def _exchange_wait(handle, after, *, name):
    srcs, lands, gather = handle["srcs"], handle["lands"], handle["gather"]
    n = len(srcs)

    def body(*refs):
        src_refs, land_refs = refs[:n], refs[n:2 * n]
        send_sem, recv_sem = refs[2 * n], refs[2 * n + 1]
        for cp in _peer_copies(src_refs, land_refs, gather, send_sem, recv_sem):
            cp.wait_send()
            cp.wait_recv()

    out = pl.pallas_call(
        body, name=name,
        in_specs=[HBM_SPEC] * (2 * n) + [SEM_SPEC, SEM_SPEC, pl.BlockSpec(memory_space=pl.ANY)],
        out_shape=tuple(pltpu.HBM(a.shape, a.dtype) for a in srcs + lands), out_specs=tuple([HBM_SPEC] * (2 * n)),
        input_output_aliases={i: i for i in range(2 * n)},
        compiler_params=pltpu.CompilerParams(has_side_effects=EFFECT),
    )(*srcs, *lands, handle["send_sem"], handle["recv_sem"], after)
    return list(out[:n]), list(out[n:])


def _adamw(parts, w, m, v, layer, so_far, *, name, tr=256):
    n_layers, r, c = w.shape
    tr = _tile(r, tr, SUBLANES)
    bc1 = 1.0 / (1.0 - ADAM_B1 ** ADAM_STEP)
    bc2 = 1.0 / (1.0 - ADAM_B2 ** ADAM_STEP)
    if so_far is None:
        so_far = [lax.empty(w.shape, F32) for _ in range(4)]

    def body(p_ref, w_ref, m_ref, v_ref, *rest):
        g_ref, d_ref, mo_ref, vo_ref = rest[4:]
        g = p_ref[0].astype(F32)
        for s in range(1, N_DEV):
            g = g + p_ref[s].astype(F32)
        m_new = ADAM_B1 * m_ref[...] + (1.0 - ADAM_B1) * g
        v_new = ADAM_B2 * v_ref[...] + (1.0 - ADAM_B2) * (g * g)
        g_ref[...] = g
        mo_ref[...] = m_new
        vo_ref[...] = v_new
        d_ref[...] = -ADAM_LR * ((m_new * bc1) / (jnp.sqrt(v_new * bc2) + ADAM_EPS) + ADAM_WD * w_ref[...])

    blk = pl.BlockSpec((None, tr, c), lambda i: (layer, i, 0))
    return pl.pallas_call(
        body, name=name, grid=(r // tr,),
        in_specs=[pl.BlockSpec((N_DEV, tr, c), lambda i: (0, i, 0)), blk, blk, blk]
        + [pl.BlockSpec(memory_space=pl.ANY)] * 4,
        out_specs=[blk] * 4, out_shape=[jax.ShapeDtypeStruct(w.shape, F32)] * 4,
        input_output_aliases={4 + o: o for o in range(4)},
        compiler_params=_params(("parallel",)),
    )(parts, w, m, v, *so_far)


def _whole(slabs, axis):
    x = jnp.moveaxis(slabs, 0, axis)
    shp = x.shape
    return x.reshape(shp[:axis] + (shp[axis] * shp[axis + 1],) + shp[axis + 2:])


def _slabs(whole, axis):
    shp = whole.shape
    x = whole.reshape(shp[:axis] + (N_DEV, shp[axis] // N_DEV) + shp[axis + 1:])
    return jnp.moveaxis(x, axis, 0)


def _pack(vecs, rows):
    flat = jnp.concatenate(vecs, axis=-1)
    pad = rows * LANES - flat.shape[-1]
    flat = jnp.pad(flat, [(0, 0)] * (flat.ndim - 1) + [(0, pad)])
    return flat.reshape(flat.shape[:-1] + (rows, LANES))


def _unpack(packed, sizes):
    flat = packed.reshape(packed.shape[:-2] + (-1,))
    out, pos = [], 0
    for n in sizes:
        out.append(flat[..., pos:pos + n])
        pos += n
    return out


def _pack_rows(sizes):
    total = sum(sizes)
    return -(-total // (LANES * SUBLANES)) * SUBLANES


BIG = {"sc_w_in": 2, "sc_w_out": 1, "lru_w_in": 2, "lru_w_gate": 3, "lru_w_out": 1, "ffn_w_up": 2, "ffn_w_down": 1}
SWAPPED = ("ffn_w_up", "lru_w_in")
SMALL = ["sc_conv_w", "lru_b_in", "lru_conv_w", "lru_conv_b", "lru_b_gate", "lru_lambda", "ffn_conv_w", "ln_g", "ln_b"]
REPL = ["sc_conv_b", "ffn_conv_b"]
WEIGHTS = ["sc_w_in", "sc_conv_w", "sc_conv_b", "sc_w_out", "lru_w_in", "lru_b_in", "lru_conv_w", "lru_conv_b",
           "lru_w_gate", "lru_b_gate", "lru_lambda", "lru_w_out", "ffn_w_up", "ffn_conv_w", "ffn_conv_b", "ffn_w_down",
           "ln_g", "ln_b"]


STAGES_PER_LAYER = 3


def _stage_big(g):
    i, part = divmod(g, STAGES_PER_LAYER)
    j = i // 2
    if part:
        return [("ffn_w_up" if part == 1 else "ffn_w_down", i)]
    return [("sc_w_in", j), ("sc_w_out", j)] if i % 2 == 0 else [("lru_w_in", j), ("lru_w_gate", j), ("lru_w_out", j)]


def _step(x, loss_target, w, m, v):
    bsz, s, d = x.shape
    t = bsz * s
    depth = w["ffn_w_up"].shape[0]
    alpha = (2.0 * depth) ** 0.25
    heads = w["lru_w_gate"].shape[1]

    small_sizes = [w[k].size for k in SMALL]
    small_rows = _pack_rows(small_sizes)
    small_local = _pack([w[k].reshape(1, -1) for k in SMALL], small_rows)[0]
    me = 4 * lax.axis_index("x") + 2 * lax.axis_index("y") + lax.axis_index("c")

    def with_own(land, own):
        return lax.dynamic_update_slice_in_dim(land, own, me, axis=0)

    stages = STAGES_PER_LAYER * depth
    gathers, tok = [], None
    for g in range(stages):
        arrs = [w[k][l].astype(BF16) for k, l in _stage_big(g)]
        if g == 0:
            arrs.append(small_local)
        if tok is not None:
            arrs[0] = arrs[0] + tok.astype(BF16)
        gathers.append(_exchange_start(arrs, [True] * len(arrs), name=f"gather_start_{g}"))
        tok = gathers[-1]["token"][0, 0]
    full = {k: [None] * w[k].shape[0] for k in BIG}
    full["sc_conv_b"] = w["sc_conv_b"]
    full["ffn_conv_b"] = w["ffn_conv_b"]

    def arrive(g, after):
        srcs, lands = _exchange_wait(gathers[g], after, name=f"gather_wait_{g}")
        for (k, l), src, land in zip(_stage_big(g), srcs, lands):
            full[k][l] = _whole(with_own(land, src[None]), BIG[k] - 1)
        if g == 0:
            for k, seg in zip(SMALL, _unpack(with_own(lands[-1], srcs[-1][None]), small_sizes)):
                full[k] = _whole(seg.reshape((N_DEV,) + w[k].shape), w[k].ndim - 1)

    xt = x.reshape(t, d)
    xb = xt.astype(BF16)
    saved = []
    for i in range(depth):
        j = i // 2
        arrive(3 * i, gathers[-1]["token"] if i == 0 else xb)
        lng, lnb = full["ln_g"][i], full["ln_b"][i]
        sv = {"x0": xb}
        if i % 2 == 0:
            hm = _mm(xb, full["sc_w_in"][j], name="sc_in")
            q = _sc_fwd(hm.reshape(bsz, s, -1), full["sc_conv_w"][j], full["sc_conv_b"][j:j + 1], name="sc_mix")
            w_out = full["sc_w_out"][j]
        else:
            hm = _mm(xb, full["lru_w_in"][j], bias=full["lru_b_in"][j:j + 1], name="lru_in")
            q, hs = _lru_fwd(hm.reshape(bsz, s, -1), full["lru_conv_w"][j], full["lru_conv_b"][j:j + 1],
                             full["lru_w_gate"][j], full["lru_b_gate"][j].reshape(heads, 1, -1),
                             full["lru_lambda"][j:j + 1], name="lru_mix")
            sv["hs"] = hs
            w_out = full["lru_w_out"][j]
        q = q.reshape(t, -1)
        arrive(3 * i + 1, q)
        z1, x1, x1b = _mm_ln(q, w_out, xt, alpha, lng[0:1], lnb[0:1], name="mix_out_ln")
        hg, hv, gc, vc, a = _ffn_fwd(x1b.reshape(bsz, s, d), full["ffn_w_up"][i], full["ffn_conv_w"][i],
                                     full["ffn_conv_b"][i:i + 1], name="ffn_up_act")
        a = a.reshape(t, -1)
        arrive(3 * i + 2, a)
        z2, xt, xb = _mm_ln(a, full["ffn_w_down"][i], x1, alpha, lng[1:2], lnb[1:2], name="ffn_down_ln")
        sv.update(hm=hm, q=q, z1=z1, x1=x1b, ffn=(hg, hv, gc, vc), a=a, z2=z2)
        saved.append(sv)

    sq, dx = _loss_head(xt, loss_target.reshape(t, d), name="loss_head")
    loss = lax.psum((0.5 / d) * sq[0, 0], MESH_AXES)

    grads = {k: [None] * w[k].shape[0] for k in WEIGHTS}
    scatters = [None] * stages

    def as_updated(k, arr):
        return jnp.swapaxes(arr, -1, -2) if k in SWAPPED else arr

    def depart(g):
        send = [as_updated(k, _slabs(grads[k][l], BIG[k] - 1).astype(BF16)) for k, l in _stage_big(g)]
        scatters[g] = _exchange_start(send, [False] * len(send), name=f"scatter_start_{g}")
        return scatters[g]["token"][0:1, 0:1]

    dz2, dz2b, dg2, db2 = _ln_bwd(dx, saved[-1]["z2"], full["ln_g"][-1][1:2], name="ln_bwd")
    for i in reversed(range(depth)):
        j = i // 2
        sv = saved[i]
        lng = full["ln_g"][i]
        grads["ffn_w_down"][i] = _mm_tn(sv["a"], dz2b, name="ffn_down_dw")
        dhg, dhv, dwg, dwv, dbg, dbv = _ffn_bwd(*sv["ffn"], dz2b.reshape(bsz, s, d), full["ffn_w_down"][i],
                                                full["ffn_conv_w"][i] + depart(3 * i + 2), name="ffn_act_bwd")
        dhg, dhv = dhg.reshape(t, -1), dhv.reshape(t, -1)
        grads["ffn_conv_w"][i] = jnp.concatenate([dwg, dwv], axis=1)
        grads["ffn_conv_b"][i] = jnp.concatenate([dbg, dbv], axis=1)[0]
        grads["ffn_w_up"][i] = jnp.concatenate([_mm_tn(sv["x1"], dhg, name="ffn_up_dw_g"),
                                                _mm_tn(sv["x1"], dhv, name="ffn_up_dw_v")], axis=1)
        dz1, dz1b, dg1, db1 = _mm_ln_bwd([dhg, dhv], full["ffn_w_up"][i], dz2, alpha, sv["z1"],
                                         lng[0:1] + depart(3 * i + 1), name="ffn_up_dx_ln")
        grads["ln_g"][i] = jnp.concatenate([dg1, dg2], axis=0)
        grads["ln_b"][i] = jnp.concatenate([db1, db2], axis=0)
        if i % 2 == 0:
            dq = _mm(dz1b, full["sc_w_out"][j], trans_w=True, name="sc_out_dx")
            grads["sc_w_out"][j] = _mm_tn(sv["q"], dz1b, name="sc_out_dw")
            dhm, dcw, dcb = _sc_bwd(sv["hm"].reshape(bsz, s, -1), dq.reshape(bsz, s, -1), full["sc_conv_w"][j],
                                    full["sc_conv_b"][j:j + 1], name="sc_mix_bwd")
            dhm = dhm.reshape(t, -1)
            grads["sc_conv_w"][j] = dcw
            grads["sc_conv_b"][j] = dcb[0]
            grads["sc_w_in"][j] = _mm_tn(sv["x0"], dhm, name="sc_in_dw")
            w_in = full["sc_w_in"][j]
        else:
            dq = _mm(dz1b, full["lru_w_out"][j], trans_w=True, name="lru_out_dx")
            grads["lru_w_out"][j] = _mm_tn(sv["q"], dz1b, name="lru_out_dw")
            dhm, dcw, dcb, dwgt, dbgt, dlam, sgb, srb = _lru_bwd(
                sv["hm"].reshape(bsz, s, -1), sv["hs"], dq.reshape(bsz, s, -1), full["lru_conv_w"][j],
                full["lru_w_gate"][j], full["lru_lambda"][j:j + 1], name="lru_mix_bwd")
            dhm = dhm.reshape(t, -1)
            grads["lru_conv_w"][j] = dcw
            grads["lru_conv_b"][j] = dcb[0]
            grads["lru_w_gate"][j] = dwgt
            grads["lru_b_gate"][j] = dbgt[:, 0, :]
            grads["lru_lambda"][j] = dlam[0]
            grads["lru_b_in"][j] = jnp.concatenate([sgb, srb], axis=1)[0]
            grads["lru_w_in"][j] = _mm_tn(sv["x0"], dhm, name="lru_in_dw")
            w_in = full["lru_w_in"][j]
        tok = depart(3 * i)
        if i > 0:
            dz2, dz2b, dg2, db2 = _mm_ln_bwd([dhm], w_in, dz1, alpha, saved[i - 1]["z2"], full["ln_g"][i - 1][1:2] + tok,
                                             name="mix_in_dx_ln")
        else:
            dx = _mm(dhm, w_in + tok[0, 0].astype(BF16), trans_w=True, resid=dz1, resid_scale=alpha, name="mix_in_dx")
    grad_x = dx.reshape(bsz, s, d)

    gsm = {k: jnp.stack(grads[k]) for k in SMALL + REPL}
    small_send = _pack([_slabs(gsm[k], gsm[k].ndim - 1).reshape(N_DEV, -1) for k in SMALL], small_rows)
    repl_sizes = [w[k].size for k in REPL]
    repl_rows = _pack_rows(repl_sizes)
    repl_send = _pack([gsm[k].reshape(1, -1) for k in REPL], repl_rows)[0]
    small_scatter = _exchange_start([small_send, repl_send], [False, True], name="scatter_start_small")

    out = {}

    def own_slab(src):
        return lax.dynamic_slice_in_dim(src, me, 1, axis=0)

    stacks = {k: None for k in BIG}
    after = dx
    for g in reversed(range(stages)):
        srcs, lands = _exchange_wait(scatters[g], after, name=f"scatter_wait_{g}")
        for (k, l), src, land in zip(_stage_big(g), srcs, lands):
            n_l, c2 = w[k].shape[0], land.shape[-1]
            wk, mk, vk = (as_updated(k, arr[k]).reshape(n_l, -1, c2) for arr in (w, m, v))
            stacks[k] = _adamw(with_own(land, own_slab(src)).reshape(N_DEV, -1, c2), wk, mk, vk, l, stacks[k],
                               name=f"adamw_{k}_{l}")
            after = stacks[k][-1]
    for k in BIG:
        shp = as_updated(k, w[k]).shape
        out[k] = [as_updated(k, r.reshape(shp)) for r in stacks[k]]
    srcs, lands = _exchange_wait(small_scatter, after, name="scatter_wait_small")
    got_small = with_own(lands[0], own_slab(srcs[0]))
    got_repl = with_own(lands[1], srcs[1][None])
    pk = lambda src, names, rows: _pack([src[k].reshape(1, -1) for k in names], rows)
    res = _adamw(got_small, small_local[None], pk(m, SMALL, small_rows), pk(v, SMALL, small_rows), 0, None,
                 name="adamw_small")
    for r_i, r in enumerate(res):
        for k, seg in zip(SMALL, _unpack(r[0], small_sizes)):
            out.setdefault(k, [None] * 4)[r_i] = seg.reshape(w[k].shape)
    res = _adamw(got_repl, pk(w, REPL, repl_rows), pk(m, REPL, repl_rows), pk(v, REPL, repl_rows), 0, None,
                 name="adamw_repl")
    for r_i, r in enumerate(res):
        for k, seg in zip(REPL, _unpack(r[0], repl_sizes)):
            out.setdefault(k, [None] * 4)[r_i] = seg.reshape(w[k].shape)

    return (loss, grad_x, *[out[k][0] for k in WEIGHTS], *[out[k][1] for k in WEIGHTS],
            *[out[k][2] for k in WEIGHTS], *[out[k][3] for k in WEIGHTS])


def kernel(x, sc_w_in, sc_conv_w, sc_conv_b, sc_w_out, lru_w_in, lru_b_in, lru_conv_w, lru_conv_b, lru_w_gate, lru_b_gate, lru_lambda, lru_w_out, ffn_w_up, ffn_conv_w, ffn_conv_b, ffn_w_down, ln_g, ln_b, loss_target, m_sc_w_in, m_sc_conv_w, m_sc_conv_b, m_sc_w_out, m_lru_w_in, m_lru_b_in, m_lru_conv_w, m_lru_conv_b, m_lru_w_gate, m_lru_b_gate, m_lru_lambda, m_lru_w_out, m_ffn_w_up, m_ffn_conv_w, m_ffn_conv_b, m_ffn_w_down, m_ln_g, m_ln_b, v_sc_w_in, v_sc_conv_w, v_sc_conv_b, v_sc_w_out, v_lru_w_in, v_lru_b_in, v_lru_conv_w, v_lru_conv_b, v_lru_w_gate, v_lru_b_gate, v_lru_lambda, v_lru_w_out, v_ffn_w_up, v_ffn_conv_w, v_ffn_conv_b, v_ffn_w_down, v_ln_g, v_ln_b):
    w = dict(sc_w_in=sc_w_in, sc_conv_w=sc_conv_w, sc_conv_b=sc_conv_b, sc_w_out=sc_w_out, lru_w_in=lru_w_in,
             lru_b_in=lru_b_in, lru_conv_w=lru_conv_w, lru_conv_b=lru_conv_b, lru_w_gate=lru_w_gate,
             lru_b_gate=lru_b_gate, lru_lambda=lru_lambda, lru_w_out=lru_w_out, ffn_w_up=ffn_w_up,
             ffn_conv_w=ffn_conv_w, ffn_conv_b=ffn_conv_b, ffn_w_down=ffn_w_down, ln_g=ln_g, ln_b=ln_b)
    m = dict(sc_w_in=m_sc_w_in, sc_conv_w=m_sc_conv_w, sc_conv_b=m_sc_conv_b, sc_w_out=m_sc_w_out, lru_w_in=m_lru_w_in,
             lru_b_in=m_lru_b_in, lru_conv_w=m_lru_conv_w, lru_conv_b=m_lru_conv_b, lru_w_gate=m_lru_w_gate,
             lru_b_gate=m_lru_b_gate, lru_lambda=m_lru_lambda, lru_w_out=m_lru_w_out, ffn_w_up=m_ffn_w_up,
             ffn_conv_w=m_ffn_conv_w, ffn_conv_b=m_ffn_conv_b, ffn_w_down=m_ffn_w_down, ln_g=m_ln_g, ln_b=m_ln_b)
    v = dict(sc_w_in=v_sc_w_in, sc_conv_w=v_sc_conv_w, sc_conv_b=v_sc_conv_b, sc_w_out=v_sc_w_out, lru_w_in=v_lru_w_in,
             lru_b_in=v_lru_b_in, lru_conv_w=v_lru_conv_w, lru_conv_b=v_lru_conv_b, lru_w_gate=v_lru_w_gate,
             lru_b_gate=v_lru_b_gate, lru_lambda=v_lru_lambda, lru_w_out=v_lru_w_out, ffn_w_up=v_ffn_w_up,
             ffn_conv_w=v_ffn_conv_w, ffn_conv_b=v_ffn_conv_b, ffn_w_down=v_ffn_w_down, ln_g=v_ln_g, ln_b=v_ln_b)
    return _step(x, loss_target, w, m, v)
```

```python
import functools
import math

import jax
import jax.numpy as jnp
from jax import lax
from jax.experimental import pallas as pl
from jax.experimental.pallas import tpu as pltpu

F32 = jnp.float32
BF16 = jnp.bfloat16

N_DEV = 8
MESH_AXES = ("x", "y", "c")
LANES = 128
SUBLANES = 8
VMEM_LIMIT = 56 * 1024 * 1024
MM_LHS_ELEMS = 3 * 1024 * 1024
MM_TN = 1536

LRU_C = 8.0
LN_EPS = 1e-5
ADAM_LR = 0.001
ADAM_B1 = 0.9
ADAM_B2 = 0.999
ADAM_EPS = 1e-08
ADAM_WD = 0.01
ADAM_STEP = 10
GELU_K = math.sqrt(2.0 / math.pi)
GELU_C = 0.044715


def _tile(n, target, align):
    if n <= target:
        return n
    t = (target // align) * align
    while t >= align:
        if n % t == 0:
            return t
        t -= align
    return n


def _params(sem):
    return pltpu.CompilerParams(dimension_semantics=sem, vmem_limit_bytes=VMEM_LIMIT)


def _rows(x):
    return lax.broadcasted_iota(jnp.int32, x.shape, 0)


def _shift_dn(x, k, fill=0.0):
    if k == 0:
        return x
    return jnp.where(_rows(x) >= k, pltpu.roll(x, k, 0), fill)


def _shift_up(x, k, fill=0.0):
    if k == 0:
        return x
    s = x.shape[0]
    return jnp.where(_rows(x) < s - k, pltpu.roll(x, s - k, 0), fill)


def _conv_fwd(x, w, b):
    kw = w.shape[0]
    y = _shift_dn(x, kw - 1) * w[0:1, :] + b
    for k in range(1, kw):
        y = y + _shift_dn(x, kw - 1 - k) * w[k:k + 1, :]
    return y


def _conv_bwd(dy, x, w):
    kw = w.shape[0]
    ahead = [_shift_up(dy, j) for j in range(kw)]
    dx = ahead[kw - 1] * w[0:1, :]
    for k in range(1, kw):
        dx = dx + ahead[kw - 1 - k] * w[k:k + 1, :]
    return dx, [_colsum(ahead[kw - 1 - k] * x) for k in range(kw)]


def _accumulate(first, items, cols=slice(None)):
    flat = []
    for ref, val in items:
        if isinstance(val, list):
            flat += [(ref, (slice(k, k + 1), cols), row) for k, row in enumerate(val)]
        else:
            flat.append((ref, Ellipsis, val))

    @pl.when(first)
    def _():
        for ref, idx, val in flat:
            ref[idx] = val

    @pl.when(jnp.logical_not(first))
    def _():
        for ref, idx, val in flat:
            ref[idx] += val


def _colsum(x):
    return jnp.sum(x, axis=0, keepdims=True)


def _sigmoid(x):
    return 1.0 / (1.0 + jnp.exp(-x))


def _log1p(x):
    u = 1.0 + x
    return jnp.where(u == 1.0, x, jnp.log(u) * (x / (u - 1.0)))


def _softplus(x):
    return jnp.maximum(x, 0.0) + _log1p(jnp.exp(-jnp.abs(x)))


def _expm1(x, ex):
    poly = x * (1.0 + x * (0.5 + x * (1.0 / 6.0 + x * (1.0 / 24.0 + x * (1.0 / 120.0 + x * (1.0 / 720.0))))))
    return jnp.where(jnp.abs(x) < 0.25, poly, ex - 1.0)


def _gelu(x):
    t = jnp.tanh(GELU_K * (x + GELU_C * x * x * x))
    return 0.5 * x * (1.0 + t)


def _gelu_and_grad(x):
    x2 = x * x
    t = jnp.tanh(GELU_K * (x + GELU_C * x * x2))
    g = 0.5 * x * (1.0 + t)
    dg = 0.5 * (1.0 + t) + 0.5 * x * (1.0 - t * t) * (GELU_K * (1.0 + 3.0 * GELU_C * x2))
    return g, dg


def _scan_fwd(a, b):
    s = a.shape[0]
    k = 1
    while k < s:
        last = 2 * k >= s
        if k % SUBLANES:
            b = a * _shift_dn(b, k) + b
            if not last:
                a = a * _shift_dn(a, k, 1.0)
        else:
            b = jnp.concatenate([b[:k], a[k:] * b[:s - k] + b[k:]], axis=0)
            if not last:
                a = jnp.concatenate([a[:k], a[k:] * a[:s - k]], axis=0)
        k *= 2
    return b


def _scan_rev(c, v):
    s = c.shape[0]
    k = 1
    while k < s:
        last = 2 * k >= s
        if k % SUBLANES:
            v = c * _shift_up(v, k) + v
            if not last:
                c = c * _shift_up(c, k, 1.0)
        else:
            v = jnp.concatenate([c[:s - k] * v[k:] + v[:s - k], v[s - k:]], axis=0)
            if not last:
                c = jnp.concatenate([c[:s - k] * c[k:], c[s - k:]], axis=0)
        k *= 2
    return v


def _mm(a, w, *, name, trans_w=False, bias=None, resid=None, resid_scale=1.0):
    m, k = a.shape
    n = w.shape[0] if trans_w else w.shape[1]
    tm = _tile(m, min(1024, max(256, MM_LHS_ELEMS // k)), SUBLANES)
    tn = _tile(n, MM_TN, LANES)
    has_bias = bias is not None
    has_resid = resid is not None

    def body(*refs):
        a_ref, w_ref = refs[0], refs[1]
        pos = 2
        b_ref = r_ref = None
        if has_bias:
            b_ref = refs[pos]
            pos += 1
        if has_resid:
            r_ref = refs[pos]
            pos += 1
        o_ref = refs[pos]

        cols = pl.ds(pl.multiple_of(pl.program_id(1) * tn, LANES), tn)
        if trans_w:
            acc = lax.dot_general(a_ref[...], w_ref[cols, :], (((1,), (1,)), ((), ())), preferred_element_type=F32)
        else:
            acc = jnp.dot(a_ref[...], w_ref[:, cols], preferred_element_type=F32)
        if has_bias:
            acc = acc + b_ref[...]
        if has_resid:
            acc = acc + resid_scale * r_ref[...]
        o_ref[...] = acc

    in_specs = [pl.BlockSpec((tm, k), lambda i, j: (i, 0)),
                pl.BlockSpec(w.shape, lambda i, j: (0, 0), pipeline_mode=pl.Buffered(1))]
    args = [a, w]
    if has_bias:
        in_specs.append(pl.BlockSpec((1, tn), lambda i, j: (0, j)))
        args.append(bias)
    if has_resid:
        in_specs.append(pl.BlockSpec((tm, tn), lambda i, j: (i, j)))
        args.append(resid)
    return pl.pallas_call(
        body, name=name, grid=(m // tm, n // tn), in_specs=in_specs,
        out_specs=pl.BlockSpec((tm, tn), lambda i, j: (i, j)),
        out_shape=jax.ShapeDtypeStruct((m, n), F32),
        compiler_params=_params(("parallel", "arbitrary")),
    )(*args)


def _mm_ln(a, w, resid, alpha, g, b, *, name, tm=512):
    m, k = a.shape
    d = w.shape[1]
    tm = _tile(m, tm, SUBLANES)

    def body(a_ref, w_ref, r_ref, g_ref, b_ref, z_ref, o_ref, obf_ref):
        y = jnp.dot(a_ref[...], w_ref[...], preferred_element_type=F32)
        z = alpha * r_ref[...] + y
        z_ref[...] = z
        mu = jnp.mean(z, axis=-1, keepdims=True)
        zc = z - mu
        var = jnp.mean(zc * zc, axis=-1, keepdims=True)
        o = zc * lax.rsqrt(var + LN_EPS) * g_ref[...] + b_ref[...]
        o_ref[...] = o
        obf_ref[...] = o.astype(BF16)

    row = pl.BlockSpec((tm, d), lambda i: (i, 0))
    vec = pl.BlockSpec((1, d), lambda i: (0, 0))
    return pl.pallas_call(
        body, name=name, grid=(m // tm,),
        in_specs=[pl.BlockSpec((tm, k), lambda i: (i, 0)),
                  pl.BlockSpec((k, d), lambda i: (0, 0), pipeline_mode=pl.Buffered(1)), row, vec, vec],
        out_specs=[row, row, row],
        out_shape=[jax.ShapeDtypeStruct((m, d), F32), jax.ShapeDtypeStruct((m, d), F32),
                   jax.ShapeDtypeStruct((m, d), BF16)],
        compiler_params=_params(("parallel",)),
    )(a, w, resid, g, b)


def _ln_bwd_math(do, z, g):
    mu = jnp.mean(z, axis=-1, keepdims=True)
    zc = z - mu
    var = jnp.mean(zc * zc, axis=-1, keepdims=True)
    rstd = lax.rsqrt(var + LN_EPS)
    xhat = zc * rstd
    dxh = do * g
    m1 = jnp.mean(dxh, axis=-1, keepdims=True)
    m2 = jnp.mean(dxh * xhat, axis=-1, keepdims=True)
    return rstd * (dxh - m1 - xhat * m2), _colsum(do * xhat), _colsum(do)


def _mm_ln_bwd(parts, w, resid, resid_scale, z, g, *, name, w_rows_are_k=False):
    t, kp = parts[0].shape
    k, d = w.shape if w_rows_are_k else w.shape[::-1]
    n = len(parts)
    tm = _tile(t, min(512, max(256, MM_LHS_ELEMS // k)), SUBLANES)

    def body(*refs):
        a_refs = refs[:n]
        w_ref, r_ref, z_ref, g_ref, dz_ref, dzbf_ref, dg_ref, db_ref = refs[n:]

        @pl.when(pl.program_id(0) == 0)
        def _():
            dg_ref[...] = jnp.zeros_like(dg_ref)
            db_ref[...] = jnp.zeros_like(db_ref)

        dx = resid_scale * r_ref[...]
        for p, a_ref in enumerate(a_refs):
            if w_rows_are_k:
                dx = dx + jnp.dot(a_ref[...], w_ref[p * kp:(p + 1) * kp, :], preferred_element_type=F32)
            else:
                dx = dx + lax.dot_general(a_ref[...], w_ref[:, p * kp:(p + 1) * kp], (((1,), (1,)), ((), ())),
                                          preferred_element_type=F32)
        dz, dg, db = _ln_bwd_math(dx, z_ref[...], g_ref[...])
        dz_ref[...] = dz
        dzbf_ref[...] = dz.astype(BF16)
        dg_ref[...] += dg
        db_ref[...] += db

    row = pl.BlockSpec((tm, d), lambda i: (i, 0))
    vec = pl.BlockSpec((1, d), lambda i: (0, 0))
    return pl.pallas_call(
        body, name=name, grid=(t // tm,),
        in_specs=[pl.BlockSpec((tm, kp), lambda i: (i, 0))] * n
        + [pl.BlockSpec(w.shape, lambda i: (0, 0), pipeline_mode=pl.Buffered(1)), row, row, vec],
        out_specs=[row, row, vec, vec],
        out_shape=[jax.ShapeDtypeStruct((t, d), F32), jax.ShapeDtypeStruct((t, d), BF16),
                   jax.ShapeDtypeStruct((1, d), F32), jax.ShapeDtypeStruct((1, d), F32)],
        compiler_params=_params(("arbitrary",)),
    )(*parts, w, resid, z, g)


def _mm_tn(a, b, *, name, tm=1408, tn=1536, tk=1024):
    t, m = a.shape
    n = b.shape[1]
    tm = _tile(m, tm, LANES)
    tn = _tile(n, tn, LANES)
    tk = _tile(t, tk, SUBLANES)

    def body(a_ref, b_ref, o_ref):
        @pl.when(pl.program_id(2) == 0)
        def _():
            o_ref[...] = jnp.zeros_like(o_ref)

        o_ref[...] += lax.dot_general(a_ref[...], b_ref[...], (((0,), (0,)), ((), ())), preferred_element_type=F32)

    return pl.pallas_call(
        body, name=name, grid=(m // tm, n // tn, t // tk),
        in_specs=[pl.BlockSpec((tk, tm), lambda i, j, l: (l, i)), pl.BlockSpec((tk, tn), lambda i, j, l: (l, j))],
        out_specs=pl.BlockSpec((tm, tn), lambda i, j, l: (i, j)),
        out_shape=jax.ShapeDtypeStruct((m, n), F32),
        compiler_params=_params(("parallel", "parallel", "arbitrary")),
    )(a, b)


def _ln_bwd(dout, z, g, *, name, tm=512):
    t, d = z.shape
    tm = _tile(t, tm, SUBLANES)

    def body(do_ref, z_ref, g_ref, dz_ref, dzbf_ref, dg_ref, db_ref):
        @pl.when(pl.program_id(0) == 0)
        def _():
            dg_ref[...] = jnp.zeros_like(dg_ref)
            db_ref[...] = jnp.zeros_like(db_ref)

        dz, dg, db = _ln_bwd_math(do_ref[...], z_ref[...], g_ref[...])
        dz_ref[...] = dz
        dzbf_ref[...] = dz.astype(BF16)
        dg_ref[...] += dg
        db_ref[...] += db

    row = pl.BlockSpec((tm, d), lambda i: (i, 0))
    vec = pl.BlockSpec((1, d), lambda i: (0, 0))
    return pl.pallas_call(
        body, name=name, grid=(t // tm,), in_specs=[row, row, vec], out_specs=[row, row, vec, vec],
        out_shape=[jax.ShapeDtypeStruct((t, d), F32), jax.ShapeDtypeStruct((t, d), BF16),
                   jax.ShapeDtypeStruct((1, d), F32), jax.ShapeDtypeStruct((1, d), F32)],
        compiler_params=_params(("arbitrary",)),
    )(dout, z, g)


def _loss_head(y, target, *, name, tm=512):
    t, d = y.shape
    tm = _tile(t, tm, SUBLANES)

    def body(y_ref, t_ref, s_ref, dy_ref):
        @pl.when(pl.program_id(0) == 0)
        def _():
            s_ref[...] = jnp.zeros_like(s_ref)

        e = y_ref[...] - t_ref[...]
        dy_ref[...] = e * (1.0 / d)
        s_ref[...] += jnp.sum(_colsum(e * e), axis=-1, keepdims=True)

    row = pl.BlockSpec((tm, d), lambda i: (i, 0))
    return pl.pallas_call(
        body, name=name, grid=(t // tm,), in_specs=[row, row],
        out_specs=[pl.BlockSpec((1, LANES), lambda i: (0, 0)), row],
        out_shape=[jax.ShapeDtypeStruct((1, LANES), F32), jax.ShapeDtypeStruct((t, d), F32)],
        compiler_params=_params(("arbitrary",)),
    )(y, target)


def _own(c, b, *_):
    return c, b


def _ahead(nc, bsz):
    def at(c, b, part):
        b2 = b + jnp.minimum(part, 1)
        return jnp.minimum(c + b2 // bsz, nc - 1), b2 % bsz
    return at


def _strip(s, tc, off, at=_own):
    def index(*ids):
        c, b = at(*ids)
        return b, 0, off + c
    return pl.BlockSpec((None, s, tc), index)


def _cvec(kw, tc, off, at=_own):
    def index(*ids):
        return 0, off + at(*ids)[0]
    return pl.BlockSpec((kw, tc), index)


def _acc(kw, tc):
    return pl.BlockSpec((kw, tc), lambda c, b, *_: (0, c))


def _sc_fwd(h, cw, cb, *, name, tc=256):
    bsz, s, d3 = h.shape
    d = d3 // 3
    tc = _tile(d, tc, LANES)
    nc = d // tc

    def body(gb_ref, gc_ref, v_ref, w_ref, b_ref, q_ref):
        u = _conv_fwd(gc_ref[...] * v_ref[...], w_ref[...], b_ref[...])
        q_ref[...] = (gb_ref[...] * u).astype(BF16)

    return pl.pallas_call(
        body, name=name, grid=(nc, bsz),
        in_specs=[_strip(s, tc, 0), _strip(s, tc, nc), _strip(s, tc, 2 * nc), _cvec(cw.shape[0], tc, 0), _cvec(1, tc, 0)],
        out_specs=_strip(s, tc, 0),
        out_shape=jax.ShapeDtypeStruct((bsz, s, d), BF16),
        compiler_params=_params(("parallel", "parallel")),
    )(h, h, h, cw, cb)


def _sc_bwd(h, dq, cw, cb, *, name, tc=256):
    bsz, s, d3 = h.shape
    d = d3 // 3
    kw = cw.shape[0]
    tc = _tile(d, tc, LANES)
    nc = d // tc

    def body(gb_ref, gc_ref, v_ref, dq_ref, w_ref, b_ref, dh_ref, dw_ref, db_ref, parts):
        b_id, part = pl.program_id(1), pl.program_id(2)

        @pl.when(part == 0)
        def _():
            gb, gc, v, dq_, w = gb_ref[...], gc_ref[...], v_ref[...], dq_ref[...], w_ref[...]
            p = gc * v
            u = _conv_fwd(p, w, b_ref[...])
            du = dq_ * gb
            dp, dw_rows = _conv_bwd(du, p, w)
            parts[0] = (dq_ * u).astype(BF16)
            parts[1] = (dp * v).astype(BF16)
            parts[2] = (dp * gc).astype(BF16)
            _accumulate(b_id == 0, [(dw_ref, dw_rows), (db_ref, _colsum(du))])

        dh_ref[...] = parts[part]

    at = _ahead(nc, bsz)
    return pl.pallas_call(
        body, name=name, grid=(nc, bsz, 3),
        in_specs=[_strip(s, tc, 0, at), _strip(s, tc, nc, at), _strip(s, tc, 2 * nc, at), _strip(s, tc, 0, at),
                  _cvec(kw, tc, 0, at), _cvec(1, tc, 0, at)],
        out_specs=[pl.BlockSpec((None, s, tc), lambda c, b, p: (b, 0, p * nc + c)), _acc(kw, tc), _acc(1, tc)],
        out_shape=[jax.ShapeDtypeStruct((bsz, s, d3), BF16), jax.ShapeDtypeStruct((kw, d), F32),
                   jax.ShapeDtypeStruct((1, d), F32)],
        scratch_shapes=[pltpu.VMEM((3, s, tc), BF16)],
        compiler_params=_params(("parallel", "arbitrary", "arbitrary")),
    )(h, h, h, dq, cw, cb)


def _ffn_specs(s, tc, nc, kw):
    strip = pl.BlockSpec((None, s, tc), lambda b, c: (b, 0, c))
    halves = [pl.BlockSpec((kw, tc), lambda b, c: (0, c)), pl.BlockSpec((kw, tc), lambda b, c: (0, nc + c)),
              pl.BlockSpec((1, tc), lambda b, c: (0, c)), pl.BlockSpec((1, tc), lambda b, c: (0, nc + c))]
    return strip, halves


def _ffn_fwd(x, w_up, cw, cb, *, name, tc=256):
    bsz, s, d = x.shape
    f = w_up.shape[0] // 2
    kw = cw.shape[0]
    tc = _tile(f, tc, LANES)
    nc = f // tc
    nt = (((1,), (1,)), ((), ()))

    def body(x_ref, w_ref, wg_ref, wv_ref, bg_ref, bv_ref, hg_ref, hv_ref, g_ref, v_ref, a_ref):
        c0 = pl.multiple_of(pl.program_id(1) * tc, LANES)
        xs = x_ref[...]
        hg = lax.dot_general(xs, w_ref[pl.ds(c0, tc), :], nt, preferred_element_type=F32)
        hv = lax.dot_general(xs, w_ref[pl.ds(f + c0, tc), :], nt, preferred_element_type=F32)
        hg_ref[...] = hg
        hv_ref[...] = hv
        g = _conv_fwd(hg, wg_ref[...], bg_ref[...])
        v = _conv_fwd(hv, wv_ref[...], bv_ref[...])
        g_ref[...] = g
        v_ref[...] = v
        a_ref[...] = (g * _sigmoid(g) * v).astype(BF16)

    strip, halves = _ffn_specs(s, tc, nc, kw)
    return pl.pallas_call(
        body, name=name, grid=(bsz, nc),
        in_specs=[pl.BlockSpec((None, s, d), lambda b, c: (b, 0, 0)),
                  pl.BlockSpec(w_up.shape, lambda b, c: (0, 0), pipeline_mode=pl.Buffered(1))] + halves,
        out_specs=[strip] * 5,
        out_shape=[jax.ShapeDtypeStruct((bsz, s, f), F32)] * 4 + [jax.ShapeDtypeStruct((bsz, s, f), BF16)],
        compiler_params=_params(("parallel", "arbitrary")),
    )(x, w_up, cw, cw, cb, cb)


def _ffn_bwd(hg, hv, g, v, dz, w_down, cw, *, name, tc=256):
    bsz, s, f = hg.shape
    d = dz.shape[2]
    kw = cw.shape[0]
    tc = _tile(f, tc, LANES)
    nc = f // tc

    def body(hg_ref, hv_ref, g_ref, v_ref, dz_ref, wd_ref, wg_ref, wv_ref,
             dhg_ref, dhv_ref, dwg_ref, dwv_ref, dbg_ref, dbv_ref):
        c0 = pl.multiple_of(pl.program_id(1) * tc, LANES)
        cols = pl.ds(c0, tc)
        da = lax.dot_general(dz_ref[...], wd_ref[cols, :], (((1,), (1,)), ((), ())), preferred_element_type=F32)
        g_ = g_ref[...]
        sg = _sigmoid(g_)
        dv = da * (g_ * sg)
        dg = da * v_ref[...] * (sg * (1.0 + g_ * (1.0 - sg)))
        dhg, dwg_rows = _conv_bwd(dg, hg_ref[...], wg_ref[...])
        dhv, dwv_rows = _conv_bwd(dv, hv_ref[...], wv_ref[...])
        dhg_ref[...] = dhg.astype(BF16)
        dhv_ref[...] = dhv.astype(BF16)
        _accumulate(pl.program_id(0) == 0, [(dwg_ref, dwg_rows), (dwv_ref, dwv_rows),
                                            (dbg_ref, [_colsum(dg)]), (dbv_ref, [_colsum(dv)])], cols)

    strip, halves = _ffn_specs(s, tc, nc, kw)
    whole = lambda r: pl.BlockSpec((r, f), lambda b, c: (0, 0))
    return pl.pallas_call(
        body, name=name, grid=(bsz, nc),
        in_specs=[strip] * 4 + [pl.BlockSpec((None, s, d), lambda b, c: (b, 0, 0)),
                                pl.BlockSpec(w_down.shape, lambda b, c: (0, 0), pipeline_mode=pl.Buffered(1))]
        + halves[:2],
        out_specs=[strip, strip, whole(kw), whole(kw), whole(1), whole(1)],
        out_shape=[jax.ShapeDtypeStruct((bsz, s, f), BF16), jax.ShapeDtypeStruct((bsz, s, f), BF16),
                   jax.ShapeDtypeStruct((kw, f), F32), jax.ShapeDtypeStruct((kw, f), F32),
                   jax.ShapeDtypeStruct((1, f), F32), jax.ShapeDtypeStruct((1, f), F32)],
        compiler_params=_params(("arbitrary", "arbitrary")),
    )(hg, hv, g, v, dz, w_down, cw, cw)


def _lru_gates(r, cw, cb, wg, bg, lam):
    blk = r.shape[1]
    xr = _conv_fwd(r, cw, cb)
    gates = jnp.dot(xr.astype(BF16), wg, preferred_element_type=F32) + bg
    rg = _sigmoid(gates[:, :blk])
    ig = _sigmoid(gates[:, blk:])
    sp = _softplus(-lam)
    la = (-LRU_C * sp) * rg
    a = jnp.exp(la)
    mult = jnp.sqrt(-_expm1(2.0 * la, a * a))
    return xr, rg, ig, sp, a, mult


def _lru_fwd(h, cw, cb, wg, bg, lam, *, name):
    bsz, s, r2 = h.shape
    heads, blk = wg.shape[0], wg.shape[1]
    kw = cw.shape[0]

    def body(g_ref, r_ref, cw_ref, cb_ref, wg_ref, bg_ref, lam_ref, y_ref, sv_ref):
        xr, rg, ig, _, a, mult = _lru_gates(r_ref[...], cw_ref[...], cb_ref[...], wg_ref[...], bg_ref[...], lam_ref[...])
        hs = _scan_fwd(a, mult * (ig * xr))
        for n, val in enumerate((hs, xr, rg, ig, a, mult)):
            sv_ref[n] = val
        y_ref[...] = (hs * _gelu(g_ref[...])).astype(BF16)

    per_head = lambda hd, b: (hd, 0, 0)
    return pl.pallas_call(
        body, name=name, grid=(heads, bsz),
        in_specs=[_strip(s, blk, 0), _strip(s, blk, heads), _cvec(kw, blk, 0), _cvec(1, blk, 0),
                  pl.BlockSpec((None, blk, 2 * blk), per_head), pl.BlockSpec((None, 1, 2 * blk), per_head),
                  _cvec(1, blk, 0)],
        out_specs=[_strip(s, blk, 0), pl.BlockSpec((6, None, s, blk), lambda hd, b: (0, b, 0, hd))],
        out_shape=[jax.ShapeDtypeStruct((bsz, s, r2 // 2), BF16), jax.ShapeDtypeStruct((6, bsz, s, r2 // 2), F32)],
        compiler_params=_params(("parallel", "parallel")),
    )(h, h, cw, cb, wg, bg, lam)


def _lru_bwd(h, sv, dy, cw, wg, lam, *, name):
    bsz, s, r2 = h.shape
    rw = r2 // 2
    heads, blk = wg.shape[0], wg.shape[1]
    kw = cw.shape[0]

    def body(g_ref, r_ref, cw_ref, wg_ref, lam_ref, sv_ref, dy_ref,
             dh_ref, dcw_ref, dcb_ref, dwg_ref, dbg_ref, dlam_ref, sg_ref, sr_ref, parts):
        b_id, part = pl.program_id(1), pl.program_id(2)

        @pl.when(part == 0)
        def _():
            r, cw_, wg_, lam_ = r_ref[...], cw_ref[...], wg_ref[...], lam_ref[...]
            hs_, xr, rg, ig, a, mult = (sv_ref[n] for n in range(6))
            sp = _softplus(-lam_)
            dy_ = dy_ref[...]
            gel, dgel = _gelu_and_grad(g_ref[...])
            dg = dy_ * hs_ * dgel
            lmb = _scan_rev(_shift_up(a, 1, 1.0), dy_ * gel)
            da = lmb * _shift_dn(hs_, 1)
            dmult = lmb * (ig * xr)
            dig = lmb * (mult * xr)
            dxr = lmb * (mult * ig)
            dla = da * a - dmult * (a * a / mult)
            drg = dla * (-LRU_C * sp)
            dsp = _colsum(dla * rg) * (-LRU_C)
            dlam = -dsp * _sigmoid(-lam_)
            dgates = jnp.concatenate([drg * (rg * (1.0 - rg)), dig * (ig * (1.0 - ig))], axis=1)
            dgates_bf = dgates.astype(BF16)
            dwg = lax.dot_general(xr.astype(BF16), dgates_bf, (((0,), (0,)), ((), ())), preferred_element_type=F32)
            dxr = dxr + lax.dot_general(dgates_bf, wg_, (((1,), (1,)), ((), ())), preferred_element_type=F32)
            dr, dcw_rows = _conv_bwd(dxr, r, cw_)
            parts[0] = dg.astype(BF16)
            parts[1] = dr.astype(BF16)
            _accumulate(b_id == 0, [(dcw_ref, dcw_rows), (dcb_ref, _colsum(dxr)), (dwg_ref, dwg),
                                    (dbg_ref, _colsum(dgates)), (dlam_ref, dlam), (sg_ref, _colsum(dg)),
                                    (sr_ref, _colsum(dr))])

        dh_ref[...] = parts[part]

    at = _ahead(heads, bsz)

    def saved(*ids):
        hd, b = at(*ids)
        return 0, b, 0, hd

    vec = pl.BlockSpec((1, blk), lambda hd, b, p: (0, hd))
    return pl.pallas_call(
        body, name=name, grid=(heads, bsz, 2),
        in_specs=[_strip(s, blk, 0, at), _strip(s, blk, heads, at), _cvec(kw, blk, 0, at),
                  pl.BlockSpec((None, blk, 2 * blk), lambda *ids: (at(*ids)[0], 0, 0)), _cvec(1, blk, 0, at),
                  pl.BlockSpec((6, None, s, blk), saved), _strip(s, blk, 0, at)],
        out_specs=[pl.BlockSpec((None, s, blk), lambda hd, b, p: (b, 0, p * heads + hd)),
                   pl.BlockSpec((kw, blk), lambda hd, b, p: (0, hd)), vec,
                   pl.BlockSpec((None, blk, 2 * blk), lambda hd, b, p: (hd, 0, 0)),
                   pl.BlockSpec((None, 1, 2 * blk), lambda hd, b, p: (hd, 0, 0)), vec, vec, vec],
        out_shape=[jax.ShapeDtypeStruct((bsz, s, r2), BF16), jax.ShapeDtypeStruct((kw, rw), F32),
                   jax.ShapeDtypeStruct((1, rw), F32), jax.ShapeDtypeStruct((heads, blk, 2 * blk), F32),
                   jax.ShapeDtypeStruct((heads, 1, 2 * blk), F32), jax.ShapeDtypeStruct((1, rw), F32),
                   jax.ShapeDtypeStruct((1, rw), F32), jax.ShapeDtypeStruct((1, rw), F32)],
        scratch_shapes=[pltpu.VMEM((2, s, blk), BF16)],
        compiler_params=_params(("parallel", "arbitrary", "arbitrary")),
    )(h, h, cw, wg, lam, sv, dy)


HBM_SPEC = pl.BlockSpec(memory_space=pltpu.HBM)
SEM_SPEC = pl.BlockSpec(memory_space=pltpu.SEMAPHORE)
EFFECT = pltpu.SideEffectType.DATAFLOW_SIDE_EFFECTING


def _peer_copies(srcs, lands, gather, send_sem, recv_sem):
    x, y, c = (lax.axis_index(ax) for ax in MESH_AXES)
    me = 4 * x + 2 * y + c
    copies = []
    for i in range(len(srcs)):
        for d in range(1, N_DEV):
            px = 1 - x if d & 4 else x
            py = 1 - y if d & 2 else y
            pc = 1 - c if d & 1 else c
            src = srcs[i] if gather[i] else srcs[i].at[4 * px + 2 * py + pc]
            k = i * (N_DEV - 1) + d - 1
            copies.append(pltpu.make_async_remote_copy(
                src_ref=src, dst_ref=lands[i].at[me], send_sem=send_sem.at[k], recv_sem=recv_sem.at[k],
                device_id=(px, py, pc), device_id_type=pl.DeviceIdType.MESH))
    return copies


def _exchange_start(arrs, gather, *, name):
    n = len(arrs)
    lands = [lax.empty((N_DEV,) + tuple(a.shape if g else a.shape[1:]), a.dtype) for a, g in zip(arrs, gather)]

    def body(*refs):
        srcs, land_refs = refs[:n], refs[n:2 * n]
        send_sem, recv_sem = refs[2 * n], refs[2 * n + 1]
        token = refs[-1]
        for cp in _peer_copies(srcs, land_refs, gather, send_sem, recv_sem):
            cp.start()
        token[...] = jnp.zeros_like(token)

    sems = pltpu.SemaphoreType.DMA((n * (N_DEV - 1),))
    thru = [pltpu.HBM(a.shape, a.dtype) for a in arrs + lands]
    out = pl.pallas_call(
        body, name=name, in_specs=[HBM_SPEC] * (2 * n),
        out_shape=(sems, sems, *thru, jax.ShapeDtypeStruct((SUBLANES, LANES), F32)),
        out_specs=(SEM_SPEC, SEM_SPEC, *([HBM_SPEC] * (2 * n)), pl.BlockSpec(memory_space=pltpu.VMEM)),
        input_output_aliases={i: 2 + i for i in range(2 * n)},
        compiler_params=pltpu.CompilerParams(has_side_effects=EFFECT),
    )(*[pltpu.with_memory_space_constraint(a, pltpu.HBM) for a in arrs + lands])
    return {"send_sem": out[0], "recv_sem": out[1], "srcs": list(out[2:2 + n]), "lands": list(out[2 + n:2 + 2 * n]),
            "token": out[-1], "gather": list(gather)}


def _exchange_wait(handle, after, *, name):
    srcs, lands, gather = handle["srcs"], handle["lands"], handle["gather"]
    n = len(srcs)

    def body(*refs):
        src_refs, land_refs = refs[:n], refs[n:2 * n]
        send_sem, recv_sem = refs[2 * n], refs[2 * n + 1]
        for cp in _peer_copies(src_refs, land_refs, gather, send_sem, recv_sem):
            cp.wait_send()
            cp.wait_recv()

    out = pl.pallas_call(
        body, name=name,
        in_specs=[HBM_SPEC] * (2 * n) + [SEM_SPEC, SEM_SPEC, pl.BlockSpec(memory_space=pl.ANY)],
        out_shape=tuple(pltpu.HBM(a.shape, a.dtype) for a in srcs + lands), out_specs=tuple([HBM_SPEC] * (2 * n)),
        input_output_aliases={i: i for i in range(2 * n)},
        compiler_params=pltpu.CompilerParams(has_side_effects=EFFECT),
    )(*srcs, *lands, handle["send_sem"], handle["recv_sem"], after)
    return list(out[:n]), list(out[n:])


def _adamw(parts, w, m, v, layer, so_far, *, name, tr=256):
    n_layers, r, c = w.shape
    tr = _tile(r, tr, SUBLANES)
    bc1 = 1.0 / (1.0 - ADAM_B1 ** ADAM_STEP)
    bc2 = 1.0 / (1.0 - ADAM_B2 ** ADAM_STEP)
    if so_far is None:
        so_far = [lax.empty(w.shape, F32) for _ in range(4)]

    def body(p_ref, w_ref, m_ref, v_ref, *rest):
        g_ref, d_ref, mo_ref, vo_ref = rest[4:]
        g = p_ref[0].astype(F32)
        for s in range(1, N_DEV):
            g = g + p_ref[s].astype(F32)
        m_new = ADAM_B1 * m_ref[...] + (1.0 - ADAM_B1) * g
        v_new = ADAM_B2 * v_ref[...] + (1.0 - ADAM_B2) * (g * g)
        g_ref[...] = g
        mo_ref[...] = m_new
        vo_ref[...] = v_new
        d_ref[...] = -ADAM_LR * ((m_new * bc1) / (jnp.sqrt(v_new * bc2) + ADAM_EPS) + ADAM_WD * w_ref[...])

    blk = pl.BlockSpec((None, tr, c), lambda i: (layer, i, 0))
    return pl.pallas_call(
        body, name=name, grid=(r // tr,),
        in_specs=[pl.BlockSpec((N_DEV, tr, c), lambda i: (0, i, 0)), blk, blk, blk]
        + [pl.BlockSpec(memory_space=pl.ANY)] * 4,
        out_specs=[blk] * 4, out_shape=[jax.ShapeDtypeStruct(w.shape, F32)] * 4,
        input_output_aliases={4 + o: o for o in range(4)},
        compiler_params=_params(("parallel",)),
    )(parts, w, m, v, *so_far)


def _whole(slabs, axis):
    x = jnp.moveaxis(slabs, 0, axis)
    shp = x.shape
    return x.reshape(shp[:axis] + (shp[axis] * shp[axis + 1],) + shp[axis + 2:])


def _slabs(whole, axis):
    shp = whole.shape
    x = whole.reshape(shp[:axis] + (N_DEV, shp[axis] // N_DEV) + shp[axis + 1:])
    return jnp.moveaxis(x, axis, 0)


def _pack(vecs, rows):
    flat = jnp.concatenate(vecs, axis=-1)
    pad = rows * LANES - flat.shape[-1]
    flat = jnp.pad(flat, [(0, 0)] * (flat.ndim - 1) + [(0, pad)])
    return flat.reshape(flat.shape[:-1] + (rows, LANES))


def _unpack(packed, sizes):
    flat = packed.reshape(packed.shape[:-2] + (-1,))
    out, pos = [], 0
    for n in sizes:
        out.append(flat[..., pos:pos + n])
        pos += n
    return out


def _pack_rows(sizes):
    total = sum(sizes)
    return -(-total // (LANES * SUBLANES)) * SUBLANES


BIG = {"sc_w_in": 2, "sc_w_out": 1, "lru_w_in": 2, "lru_w_gate": 3, "lru_w_out": 1, "ffn_w_up": 2, "ffn_w_down": 1}
SWAPPED = ("ffn_w_up", "lru_w_in")
TRANSPOSED = ("ffn_w_up",)
SMALL = ["sc_conv_w", "lru_b_in", "lru_conv_w", "lru_conv_b", "lru_b_gate", "lru_lambda", "ffn_conv_w", "ln_g", "ln_b"]
REPL = ["sc_conv_b", "ffn_conv_b"]
WEIGHTS = ["sc_w_in", "sc_conv_w", "sc_conv_b", "sc_w_out", "lru_w_in", "lru_b_in", "lru_conv_w", "lru_conv_b",
           "lru_w_gate", "lru_b_gate", "lru_lambda", "lru_w_out", "ffn_w_up", "ffn_conv_w", "ffn_conv_b", "ffn_w_down",
           "ln_g", "ln_b"]


STAGES_PER_LAYER = 3


def _stage_big(g):
    i, part = divmod(g, STAGES_PER_LAYER)
    j = i // 2
    if part:
        return [("ffn_w_up" if part == 1 else "ffn_w_down", i)]
    return [("sc_w_in", j), ("sc_w_out", j)] if i % 2 == 0 else [("lru_w_in", j), ("lru_w_gate", j), ("lru_w_out", j)]


def _step(x, loss_target, w, m, v):
    bsz, s, d = x.shape
    t = bsz * s
    depth = w["ffn_w_up"].shape[0]
    alpha = (2.0 * depth) ** 0.25
    heads = w["lru_w_gate"].shape[1]

    small_sizes = [w[k].size for k in SMALL]
    small_rows = _pack_rows(small_sizes)
    small_local = _pack([w[k].reshape(1, -1) for k in SMALL], small_rows)[0]
    me = 4 * lax.axis_index("x") + 2 * lax.axis_index("y") + lax.axis_index("c")

    def with_own(land, own):
        return lax.dynamic_update_slice_in_dim(land, own, me, axis=0)

    stages = STAGES_PER_LAYER * depth
    def held(k, arr):
        return jnp.swapaxes(arr, -1, -2) if k in TRANSPOSED else arr

    def split_axis(k):
        return 0 if k in TRANSPOSED else BIG[k] - 1

    gathers, tok = [], None
    for g in range(stages):
        arrs = [held(k, w[k][l]).astype(BF16) for k, l in _stage_big(g)]
        if g == 0:
            arrs.append(small_local)
        if tok is not None:
            arrs[0] = arrs[0] + tok.astype(BF16)
        gathers.append(_exchange_start(arrs, [True] * len(arrs), name=f"gather_start_{g}"))
        tok = gathers[-1]["token"][0, 0]
    full = {k: [None] * w[k].shape[0] for k in BIG}
    full["sc_conv_b"] = w["sc_conv_b"]
    full["ffn_conv_b"] = w["ffn_conv_b"]

    def arrive(g, after):
        srcs, lands = _exchange_wait(gathers[g], after, name=f"gather_wait_{g}")
        for (k, l), src, land in zip(_stage_big(g), srcs, lands):
            full[k][l] = _whole(with_own(land, src[None]), split_axis(k))
        if g == 0:
            for k, seg in zip(SMALL, _unpack(with_own(lands[-1], srcs[-1][None]), small_sizes)):
                full[k] = _whole(seg.reshape((N_DEV,) + w[k].shape), w[k].ndim - 1)

    xt = x.reshape(t, d)
    xb = xt.astype(BF16)
    saved = []
    for i in range(depth):
        j = i // 2
        arrive(3 * i, gathers[-1]["token"] if i == 0 else xb)
        lng, lnb = full["ln_g"][i], full["ln_b"][i]
        sv = {"x0": xb}
        if i % 2 == 0:
            hm = _mm(xb, full["sc_w_in"][j], name="sc_in")
            q = _sc_fwd(hm.reshape(bsz, s, -1), full["sc_conv_w"][j], full["sc_conv_b"][j:j + 1], name="sc_mix")
            w_out = full["sc_w_out"][j]
        else:
            hm = _mm(xb, full["lru_w_in"][j], bias=full["lru_b_in"][j:j + 1], name="lru_in")
            q, hs = _lru_fwd(hm.reshape(bsz, s, -1), full["lru_conv_w"][j], full["lru_conv_b"][j:j + 1],
                             full["lru_w_gate"][j], full["lru_b_gate"][j].reshape(heads, 1, -1),
                             full["lru_lambda"][j:j + 1], name="lru_mix")
            sv["hs"] = hs
            w_out = full["lru_w_out"][j]
        q = q.reshape(t, -1)
        arrive(3 * i + 1, q)
        z1, x1, x1b = _mm_ln(q, w_out, xt, alpha, lng[0:1], lnb[0:1], name="mix_out_ln")
        hg, hv, gc, vc, a = _ffn_fwd(x1b.reshape(bsz, s, d), full["ffn_w_up"][i], full["ffn_conv_w"][i],
                                     full["ffn_conv_b"][i:i + 1], name="ffn_up_act")
        a = a.reshape(t, -1)
        arrive(3 * i + 2, a)
        z2, xt, xb = _mm_ln(a, full["ffn_w_down"][i], x1, alpha, lng[1:2], lnb[1:2], name="ffn_down_ln")
        sv.update(hm=hm, q=q, z1=z1, x1=x1b, ffn=(hg, hv, gc, vc), a=a, z2=z2)
        saved.append(sv)

    sq, dx = _loss_head(xt, loss_target.reshape(t, d), name="loss_head")
    loss = lax.psum((0.5 / d) * sq[0, 0], MESH_AXES)

    grads = {k: [None] * w[k].shape[0] for k in WEIGHTS}
    scatters = [None] * stages

    def as_updated(k, arr):
        return jnp.swapaxes(arr, -1, -2) if k in SWAPPED else arr

    def depart(g):
        send = [_slabs(grads[k][l], split_axis(k)).astype(BF16) for k, l in _stage_big(g)]
        send = [sl if k in TRANSPOSED else as_updated(k, sl) for (k, l), sl in zip(_stage_big(g), send)]
        scatters[g] = _exchange_start(send, [False] * len(send), name=f"scatter_start_{g}")
        return scatters[g]["token"][0:1, 0:1]

    dz2, dz2b, dg2, db2 = _ln_bwd(dx, saved[-1]["z2"], full["ln_g"][-1][1:2], name="ln_bwd")
    for i in reversed(range(depth)):
        j = i // 2
        sv = saved[i]
        lng = full["ln_g"][i]
        grads["ffn_w_down"][i] = _mm_tn(sv["a"], dz2b, name="ffn_down_dw")
        dhg, dhv, dwg, dwv, dbg, dbv = _ffn_bwd(*sv["ffn"], dz2b.reshape(bsz, s, d), full["ffn_w_down"][i],
                                                full["ffn_conv_w"][i] + depart(3 * i + 2), name="ffn_act_bwd")
        dhg, dhv = dhg.reshape(t, -1), dhv.reshape(t, -1)
        grads["ffn_conv_w"][i] = jnp.concatenate([dwg, dwv], axis=1)
        grads["ffn_conv_b"][i] = jnp.concatenate([dbg, dbv], axis=1)[0]
        grads["ffn_w_up"][i] = jnp.concatenate([_mm_tn(dhg, sv["x1"], name="ffn_up_dw_g"),
                                                _mm_tn(dhv, sv["x1"], name="ffn_up_dw_v")], axis=0)
        dz1, dz1b, dg1, db1 = _mm_ln_bwd([dhg, dhv], full["ffn_w_up"][i], dz2, alpha, sv["z1"],
                                         lng[0:1] + depart(3 * i + 1), name="ffn_up_dx_ln", w_rows_are_k=True)
        grads["ln_g"][i] = jnp.concatenate([dg1, dg2], axis=0)
        grads["ln_b"][i] = jnp.concatenate([db1, db2], axis=0)
        if i % 2 == 0:
            dq = _mm(dz1b, full["sc_w_out"][j], trans_w=True, name="sc_out_dx")
            grads["sc_w_out"][j] = _mm_tn(sv["q"], dz1b, name="sc_out_dw")
            dhm, dcw, dcb = _sc_bwd(sv["hm"].reshape(bsz, s, -1), dq.reshape(bsz, s, -1), full["sc_conv_w"][j],
                                    full["sc_conv_b"][j:j + 1], name="sc_mix_bwd")
            dhm = dhm.reshape(t, -1)
            grads["sc_conv_w"][j] = dcw
            grads["sc_conv_b"][j] = dcb[0]
            grads["sc_w_in"][j] = _mm_tn(sv["x0"], dhm, name="sc_in_dw")
            w_in = full["sc_w_in"][j]
        else:
            dq = _mm(dz1b, full["lru_w_out"][j], trans_w=True, name="lru_out_dx")
            grads["lru_w_out"][j] = _mm_tn(sv["q"], dz1b, name="lru_out_dw")
            dhm, dcw, dcb, dwgt, dbgt, dlam, sgb, srb = _lru_bwd(
                sv["hm"].reshape(bsz, s, -1), sv["hs"], dq.reshape(bsz, s, -1), full["lru_conv_w"][j],
                full["lru_w_gate"][j], full["lru_lambda"][j:j + 1], name="lru_mix_bwd")
            dhm = dhm.reshape(t, -1)
            grads["lru_conv_w"][j] = dcw
            grads["lru_conv_b"][j] = dcb[0]
            grads["lru_w_gate"][j] = dwgt
            grads["lru_b_gate"][j] = dbgt[:, 0, :]
            grads["lru_lambda"][j] = dlam[0]
            grads["lru_b_in"][j] = jnp.concatenate([sgb, srb], axis=1)[0]
            grads["lru_w_in"][j] = _mm_tn(sv["x0"], dhm, name="lru_in_dw")
            w_in = full["lru_w_in"][j]
        tok = depart(3 * i)
        if i > 0:
            dz2, dz2b, dg2, db2 = _mm_ln_bwd([dhm], w_in, dz1, alpha, saved[i - 1]["z2"], full["ln_g"][i - 1][1:2] + tok,
                                             name="mix_in_dx_ln")
        else:
            dx = _mm(dhm, w_in + tok[0, 0].astype(BF16), trans_w=True, resid=dz1, resid_scale=alpha, name="mix_in_dx")
    grad_x = dx.reshape(bsz, s, d)

    gsm = {k: jnp.stack(grads[k]) for k in SMALL + REPL}
    small_send = _pack([_slabs(gsm[k], gsm[k].ndim - 1).reshape(N_DEV, -1) for k in SMALL], small_rows)
    repl_sizes = [w[k].size for k in REPL]
    repl_rows = _pack_rows(repl_sizes)
    repl_send = _pack([gsm[k].reshape(1, -1) for k in REPL], repl_rows)[0]
    small_scatter = _exchange_start([small_send, repl_send], [False, True], name="scatter_start_small")

    out = {}

    def own_slab(src):
        return lax.dynamic_slice_in_dim(src, me, 1, axis=0)

    stacks = {k: None for k in BIG}
    after = dx
    for g in reversed(range(stages)):
        srcs, lands = _exchange_wait(scatters[g], after, name=f"scatter_wait_{g}")
        for (k, l), src, land in zip(_stage_big(g), srcs, lands):
            n_l, c2 = w[k].shape[0], land.shape[-1]
            wk, mk, vk = (as_updated(k, arr[k]).reshape(n_l, -1, c2) for arr in (w, m, v))
            stacks[k] = _adamw(with_own(land, own_slab(src)).reshape(N_DEV, -1, c2), wk, mk, vk, l, stacks[k],
                               name=f"adamw_{k}_{l}")
            after = stacks[k][-1]
    for k in BIG:
        shp = as_updated(k, w[k]).shape
        out[k] = [as_updated(k, r.reshape(shp)) for r in stacks[k]]
    srcs, lands = _exchange_wait(small_scatter, after, name="scatter_wait_small")
    got_small = with_own(lands[0], own_slab(srcs[0]))
    got_repl = with_own(lands[1], srcs[1][None])
    pk = lambda src, names, rows: _pack([src[k].reshape(1, -1) for k in names], rows)
    res = _adamw(got_small, small_local[None], pk(m, SMALL, small_rows), pk(v, SMALL, small_rows), 0, None,
                 name="adamw_small")
    for r_i, r in enumerate(res):
        for k, seg in zip(SMALL, _unpack(r[0], small_sizes)):
            out.setdefault(k, [None] * 4)[r_i] = seg.reshape(w[k].shape)
    res = _adamw(got_repl, pk(w, REPL, repl_rows), pk(m, REPL, repl_rows), pk(v, REPL, repl_rows), 0, None,
                 name="adamw_repl")
    for r_i, r in enumerate(res):
        for k, seg in zip(REPL, _unpack(r[0], repl_sizes)):
            out.setdefault(k, [None] * 4)[r_i] = seg.reshape(w[k].shape)

    return (loss, grad_x, *[out[k][0] for k in WEIGHTS], *[out[k][1] for k in WEIGHTS],
            *[out[k][2] for k in WEIGHTS], *[out[k][3] for k in WEIGHTS])


def kernel(x, sc_w_in, sc_conv_w, sc_conv_b, sc_w_out, lru_w_in, lru_b_in, lru_conv_w, lru_conv_b, lru_w_gate, lru_b_gate, lru_lambda, lru_w_out, ffn_w_up, ffn_conv_w, ffn_conv_b, ffn_w_down, ln_g, ln_b, loss_target, m_sc_w_in, m_sc_conv_w, m_sc_conv_b, m_sc_w_out, m_lru_w_in, m_lru_b_in, m_lru_conv_w, m_lru_conv_b, m_lru_w_gate, m_lru_b_gate, m_lru_lambda, m_lru_w_out, m_ffn_w_up, m_ffn_conv_w, m_ffn_conv_b, m_ffn_w_down, m_ln_g, m_ln_b, v_sc_w_in, v_sc_conv_w, v_sc_conv_b, v_sc_w_out, v_lru_w_in, v_lru_b_in, v_lru_conv_w, v_lru_conv_b, v_lru_w_gate, v_lru_b_gate, v_lru_lambda, v_lru_w_out, v_ffn_w_up, v_ffn_conv_w, v_ffn_conv_b, v_ffn_w_down, v_ln_g, v_ln_b):
    w = dict(sc_w_in=sc_w_in, sc_conv_w=sc_conv_w, sc_conv_b=sc_conv_b, sc_w_out=sc_w_out, lru_w_in=lru_w_in,
             lru_b_in=lru_b_in, lru_conv_w=lru_conv_w, lru_conv_b=lru_conv_b, lru_w_gate=lru_w_gate,
             lru_b_gate=lru_b_gate, lru_lambda=lru_lambda, lru_w_out=lru_w_out, ffn_w_up=ffn_w_up,
             ffn_conv_w=ffn_conv_w, ffn_conv_b=ffn_conv_b, ffn_w_down=ffn_w_down, ln_g=ln_g, ln_b=ln_b)
    m = dict(sc_w_in=m_sc_w_in, sc_conv_w=m_sc_conv_w, sc_conv_b=m_sc_conv_b, sc_w_out=m_sc_w_out, lru_w_in=m_lru_w_in,
             lru_b_in=m_lru_b_in, lru_conv_w=m_lru_conv_w, lru_conv_b=m_lru_conv_b, lru_w_gate=m_lru_w_gate,
             lru_b_gate=m_lru_b_gate, lru_lambda=m_lru_lambda, lru_w_out=m_lru_w_out, ffn_w_up=m_ffn_w_up,
             ffn_conv_w=m_ffn_conv_w, ffn_conv_b=m_ffn_conv_b, ffn_w_down=m_ffn_w_down, ln_g=m_ln_g, ln_b=m_ln_b)
    v = dict(sc_w_in=v_sc_w_in, sc_conv_w=v_sc_conv_w, sc_conv_b=v_sc_conv_b, sc_w_out=v_sc_w_out, lru_w_in=v_lru_w_in,
             lru_b_in=v_lru_b_in, lru_conv_w=v_lru_conv_w, lru_conv_b=v_lru_conv_b, lru_w_gate=v_lru_w_gate,
             lru_b_gate=v_lru_b_gate, lru_lambda=v_lru_lambda, lru_w_out=v_lru_w_out, ffn_w_up=v_ffn_w_up,
             ffn_conv_w=v_ffn_conv_w, ffn_conv_b=v_ffn_conv_b, ffn_w_down=v_ffn_w_down, ln_g=v_ln_g, ln_b=v_ln_b)
    return _step(x, loss_target, w, m, v)
```

```python
import functools
import math

import jax
import jax.numpy as jnp
from jax import lax
from jax.experimental import pallas as pl
from jax.experimental.pallas import tpu as pltpu

F32 = jnp.float32
BF16 = jnp.bfloat16

N_DEV = 8
MESH_AXES = ("x", "y", "c")
LANES = 128
SUBLANES = 8
VMEM_LIMIT = 56 * 1024 * 1024
MM_LHS_ELEMS = 3 * 1024 * 1024
MM_TN = 1536

LRU_C = 8.0
LN_EPS = 1e-5
ADAM_LR = 0.001
ADAM_B1 = 0.9
ADAM_B2 = 0.999
ADAM_EPS = 1e-08
ADAM_WD = 0.01
ADAM_STEP = 10
GELU_K = math.sqrt(2.0 / math.pi)
GELU_C = 0.044715


def _tile(n, target, align):
    if n <= target:
        return n
    t = (target // align) * align
    while t >= align:
        if n % t == 0:
            return t
        t -= align
    return n


def _params(sem):
    return pltpu.CompilerParams(dimension_semantics=sem, vmem_limit_bytes=VMEM_LIMIT)


def _rows(x):
    return lax.broadcasted_iota(jnp.int32, x.shape, 0)


def _shift_dn(x, k, fill=0.0):
    if k == 0:
        return x
    return jnp.where(_rows(x) >= k, pltpu.roll(x, k, 0), fill)


def _shift_up(x, k, fill=0.0):
    if k == 0:
        return x
    s = x.shape[0]
    return jnp.where(_rows(x) < s - k, pltpu.roll(x, s - k, 0), fill)


def _conv_fwd(x, w, b):
    kw = w.shape[0]
    y = _shift_dn(x, kw - 1) * w[0:1, :] + b
    for k in range(1, kw):
        y = y + _shift_dn(x, kw - 1 - k) * w[k:k + 1, :]
    return y


def _conv_bwd(dy, x, w):
    kw = w.shape[0]
    ahead = [_shift_up(dy, j) for j in range(kw)]
    dx = ahead[kw - 1] * w[0:1, :]
    for k in range(1, kw):
        dx = dx + ahead[kw - 1 - k] * w[k:k + 1, :]
    return dx, [_colsum(ahead[kw - 1 - k] * x) for k in range(kw)]


def _accumulate(first, items, cols=slice(None)):
    flat = []
    for ref, val in items:
        if isinstance(val, list):
            flat += [(ref, (slice(k, k + 1), cols), row) for k, row in enumerate(val)]
        else:
            flat.append((ref, Ellipsis, val))

    @pl.when(first)
    def _():
        for ref, idx, val in flat:
            ref[idx] = val

    @pl.when(jnp.logical_not(first))
    def _():
        for ref, idx, val in flat:
            ref[idx] += val


def _colsum(x):
    return jnp.sum(x, axis=0, keepdims=True)


def _sigmoid(x):
    return 1.0 / (1.0 + jnp.exp(-x))


def _log1p(x):
    u = 1.0 + x
    return jnp.where(u == 1.0, x, jnp.log(u) * (x / (u - 1.0)))


def _softplus(x):
    return jnp.maximum(x, 0.0) + _log1p(jnp.exp(-jnp.abs(x)))


def _expm1(x, ex):
    poly = x * (1.0 + x * (0.5 + x * (1.0 / 6.0 + x * (1.0 / 24.0 + x * (1.0 / 120.0 + x * (1.0 / 720.0))))))
    return jnp.where(jnp.abs(x) < 0.25, poly, ex - 1.0)


def _gelu(x):
    t = jnp.tanh(GELU_K * (x + GELU_C * x * x * x))
    return 0.5 * x * (1.0 + t)


def _gelu_and_grad(x):
    x2 = x * x
    t = jnp.tanh(GELU_K * (x + GELU_C * x * x2))
    g = 0.5 * x * (1.0 + t)
    dg = 0.5 * (1.0 + t) + 0.5 * x * (1.0 - t * t) * (GELU_K * (1.0 + 3.0 * GELU_C * x2))
    return g, dg


def _scan_fwd(a, b):
    s = a.shape[0]
    k = 1
    while k < s:
        last = 2 * k >= s
        if k % SUBLANES:
            b = a * _shift_dn(b, k) + b
            if not last:
                a = a * _shift_dn(a, k, 1.0)
        else:
            b = jnp.concatenate([b[:k], a[k:] * b[:s - k] + b[k:]], axis=0)
            if not last:
                a = jnp.concatenate([a[:k], a[k:] * a[:s - k]], axis=0)
        k *= 2
    return b


def _scan_rev(c, v):
    s = c.shape[0]
    k = 1
    while k < s:
        last = 2 * k >= s
        if k % SUBLANES:
            v = c * _shift_up(v, k) + v
            if not last:
                c = c * _shift_up(c, k, 1.0)
        else:
            v = jnp.concatenate([c[:s - k] * v[k:] + v[:s - k], v[s - k:]], axis=0)
            if not last:
                c = jnp.concatenate([c[:s - k] * c[k:], c[s - k:]], axis=0)
        k *= 2
    return v


def _mm(a, w, *, name, trans_w=False, bias=None, resid=None, resid_scale=1.0):
    m, k = a.shape
    n = w.shape[0] if trans_w else w.shape[1]
    tm = _tile(m, min(1024, max(256, MM_LHS_ELEMS // k)), SUBLANES)
    tn = _tile(n, MM_TN, LANES)
    has_bias = bias is not None
    has_resid = resid is not None

    def body(*refs):
        a_ref, w_ref = refs[0], refs[1]
        pos = 2
        b_ref = r_ref = None
        if has_bias:
            b_ref = refs[pos]
            pos += 1
        if has_resid:
            r_ref = refs[pos]
            pos += 1
        o_ref = refs[pos]

        cols = pl.ds(pl.multiple_of(pl.program_id(1) * tn, LANES), tn)
        if trans_w:
            acc = lax.dot_general(a_ref[...], w_ref[cols, :], (((1,), (1,)), ((), ())), preferred_element_type=F32)
        else:
            acc = jnp.dot(a_ref[...], w_ref[:, cols], preferred_element_type=F32)
        if has_bias:
            acc = acc + b_ref[...]
        if has_resid:
            acc = acc + resid_scale * r_ref[...]
        o_ref[...] = acc

    in_specs = [pl.BlockSpec((tm, k), lambda i, j: (i, 0)),
                pl.BlockSpec(w.shape, lambda i, j: (0, 0), pipeline_mode=pl.Buffered(1))]
    args = [a, w]
    if has_bias:
        in_specs.append(pl.BlockSpec((1, tn), lambda i, j: (0, j)))
        args.append(bias)
    if has_resid:
        in_specs.append(pl.BlockSpec((tm, tn), lambda i, j: (i, j)))
        args.append(resid)
    return pl.pallas_call(
        body, name=name, grid=(m // tm, n // tn), in_specs=in_specs,
        out_specs=pl.BlockSpec((tm, tn), lambda i, j: (i, j)),
        out_shape=jax.ShapeDtypeStruct((m, n), F32),
        compiler_params=_params(("parallel", "arbitrary")),
    )(*args)


def _ln(z, g, b):
    mu = jnp.mean(z, axis=-1, keepdims=True)
    zc = z - mu
    var = jnp.mean(zc * zc, axis=-1, keepdims=True)
    return zc * lax.rsqrt(var + LN_EPS) * g + b


def _mm_ln(a, w, resid, alpha, g, b, *, name, resid_ln=None, tm=512):
    m, k = a.shape
    d = w.shape[1]
    tm = _tile(m, tm, SUBLANES)
    n_extra = 0 if resid_ln is None else 2

    def body(a_ref, w_ref, r_ref, g_ref, b_ref, *rest):
        z_ref, obf_ref = rest[n_extra:]
        x = r_ref[...]
        if resid_ln is not None:
            x = _ln(x, rest[0][...], rest[1][...])
        z = alpha * x + jnp.dot(a_ref[...], w_ref[...], preferred_element_type=F32)
        z_ref[...] = z
        obf_ref[...] = _ln(z, g_ref[...], b_ref[...]).astype(BF16)

    row = pl.BlockSpec((tm, d), lambda i: (i, 0))
    vec = pl.BlockSpec((1, d), lambda i: (0, 0))
    return pl.pallas_call(
        body, name=name, grid=(m // tm,),
        in_specs=[pl.BlockSpec((tm, k), lambda i: (i, 0)),
                  pl.BlockSpec((k, d), lambda i: (0, 0), pipeline_mode=pl.Buffered(1)), row, vec, vec]
        + [vec] * n_extra,
        out_specs=[row, row],
        out_shape=[jax.ShapeDtypeStruct((m, d), F32), jax.ShapeDtypeStruct((m, d), BF16)],
        compiler_params=_params(("parallel",)),
    )(a, w, resid, g, b, *(resid_ln or ()))


def _ln_bwd_math(do, z, g):
    mu = jnp.mean(z, axis=-1, keepdims=True)
    zc = z - mu
    var = jnp.mean(zc * zc, axis=-1, keepdims=True)
    rstd = lax.rsqrt(var + LN_EPS)
    xhat = zc * rstd
    dxh = do * g
    m1 = jnp.mean(dxh, axis=-1, keepdims=True)
    m2 = jnp.mean(dxh * xhat, axis=-1, keepdims=True)
    return rstd * (dxh - m1 - xhat * m2), _colsum(do * xhat), _colsum(do)


def _mm_ln_bwd(parts, w, resid, resid_scale, z, g, *, name, w_rows_are_k=False):
    t, kp = parts[0].shape
    k, d = w.shape if w_rows_are_k else w.shape[::-1]
    n = len(parts)
    tm = _tile(t, min(512, max(256, MM_LHS_ELEMS // k)), SUBLANES)

    def body(*refs):
        a_refs = refs[:n]
        w_ref, r_ref, z_ref, g_ref, dz_ref, dzbf_ref, dg_ref, db_ref = refs[n:]

        @pl.when(pl.program_id(0) == 0)
        def _():
            dg_ref[...] = jnp.zeros_like(dg_ref)
            db_ref[...] = jnp.zeros_like(db_ref)

        dx = resid_scale * r_ref[...]
        for p, a_ref in enumerate(a_refs):
            if w_rows_are_k:
                dx = dx + jnp.dot(a_ref[...], w_ref[p * kp:(p + 1) * kp, :], preferred_element_type=F32)
            else:
                dx = dx + lax.dot_general(a_ref[...], w_ref[:, p * kp:(p + 1) * kp], (((1,), (1,)), ((), ())),
                                          preferred_element_type=F32)
        dz, dg, db = _ln_bwd_math(dx, z_ref[...], g_ref[...])
        dz_ref[...] = dz
        dzbf_ref[...] = dz.astype(BF16)
        dg_ref[...] += dg
        db_ref[...] += db

    row = pl.BlockSpec((tm, d), lambda i: (i, 0))
    vec = pl.BlockSpec((1, d), lambda i: (0, 0))
    return pl.pallas_call(
        body, name=name, grid=(t // tm,),
        in_specs=[pl.BlockSpec((tm, kp), lambda i: (i, 0))] * n
        + [pl.BlockSpec(w.shape, lambda i: (0, 0), pipeline_mode=pl.Buffered(1)), row, row, vec],
        out_specs=[row, row, vec, vec],
        out_shape=[jax.ShapeDtypeStruct((t, d), F32), jax.ShapeDtypeStruct((t, d), BF16),
                   jax.ShapeDtypeStruct((1, d), F32), jax.ShapeDtypeStruct((1, d), F32)],
        compiler_params=_params(("arbitrary",)),
    )(*parts, w, resid, z, g)


def _mm_tn(a, b, *, name, tm=1408, tn=1536, tk=1024):
    t, m = a.shape
    n = b.shape[1]
    tm = _tile(m, tm, LANES)
    tn = _tile(n, tn, LANES)
    tk = _tile(t, tk, SUBLANES)

    def body(a_ref, b_ref, o_ref):
        @pl.when(pl.program_id(2) == 0)
        def _():
            o_ref[...] = jnp.zeros_like(o_ref)

        o_ref[...] += lax.dot_general(a_ref[...], b_ref[...], (((0,), (0,)), ((), ())), preferred_element_type=F32)

    return pl.pallas_call(
        body, name=name, grid=(m // tm, n // tn, t // tk),
        in_specs=[pl.BlockSpec((tk, tm), lambda i, j, l: (l, i)), pl.BlockSpec((tk, tn), lambda i, j, l: (l, j))],
        out_specs=pl.BlockSpec((tm, tn), lambda i, j, l: (i, j)),
        out_shape=jax.ShapeDtypeStruct((m, n), F32),
        compiler_params=_params(("parallel", "parallel", "arbitrary")),
    )(a, b)


def _ln_bwd(dout, z, g, *, name, tm=512):
    t, d = z.shape
    tm = _tile(t, tm, SUBLANES)

    def body(do_ref, z_ref, g_ref, dz_ref, dzbf_ref, dg_ref, db_ref):
        @pl.when(pl.program_id(0) == 0)
        def _():
            dg_ref[...] = jnp.zeros_like(dg_ref)
            db_ref[...] = jnp.zeros_like(db_ref)

        dz, dg, db = _ln_bwd_math(do_ref[...], z_ref[...], g_ref[...])
        dz_ref[...] = dz
        dzbf_ref[...] = dz.astype(BF16)
        dg_ref[...] += dg
        db_ref[...] += db

    row = pl.BlockSpec((tm, d), lambda i: (i, 0))
    vec = pl.BlockSpec((1, d), lambda i: (0, 0))
    return pl.pallas_call(
        body, name=name, grid=(t // tm,), in_specs=[row, row, vec], out_specs=[row, row, vec, vec],
        out_shape=[jax.ShapeDtypeStruct((t, d), F32), jax.ShapeDtypeStruct((t, d), BF16),
                   jax.ShapeDtypeStruct((1, d), F32), jax.ShapeDtypeStruct((1, d), F32)],
        compiler_params=_params(("arbitrary",)),
    )(dout, z, g)


def _loss_head(z, g, b, target, *, name, tm=512):
    t, d = z.shape
    tm = _tile(t, tm, SUBLANES)

    def body(z_ref, g_ref, b_ref, t_ref, s_ref, dy_ref):
        @pl.when(pl.program_id(0) == 0)
        def _():
            s_ref[...] = jnp.zeros_like(s_ref)

        e = _ln(z_ref[...], g_ref[...], b_ref[...]) - t_ref[...]
        dy_ref[...] = e * (1.0 / d)
        s_ref[...] += jnp.sum(_colsum(e * e), axis=-1, keepdims=True)

    row = pl.BlockSpec((tm, d), lambda i: (i, 0))
    vec = pl.BlockSpec((1, d), lambda i: (0, 0))
    return pl.pallas_call(
        body, name=name, grid=(t // tm,), in_specs=[row, vec, vec, row],
        out_specs=[pl.BlockSpec((1, LANES), lambda i: (0, 0)), row],
        out_shape=[jax.ShapeDtypeStruct((1, LANES), F32), jax.ShapeDtypeStruct((t, d), F32)],
        compiler_params=_params(("arbitrary",)),
    )(z, g, b, target)


def _own(c, b, *_):
    return c, b


def _ahead(nc, bsz):
    def at(c, b, part):
        b2 = b + jnp.minimum(part, 1)
        return jnp.minimum(c + b2 // bsz, nc - 1), b2 % bsz
    return at


def _strip(s, tc, off, at=_own):
    def index(*ids):
        c, b = at(*ids)
        return b, 0, off + c
    return pl.BlockSpec((None, s, tc), index)


def _cvec(kw, tc, off, at=_own):
    def index(*ids):
        return 0, off + at(*ids)[0]
    return pl.BlockSpec((kw, tc), index)


def _acc(kw, tc):
    return pl.BlockSpec((kw, tc), lambda c, b, *_: (0, c))


def _sc_fwd(h, cw, cb, *, name, tc=256):
    bsz, s, d3 = h.shape
    d = d3 // 3
    tc = _tile(d, tc, LANES)
    nc = d // tc

    def body(gb_ref, gc_ref, v_ref, w_ref, b_ref, q_ref):
        u = _conv_fwd(gc_ref[...] * v_ref[...], w_ref[...], b_ref[...])
        q_ref[...] = (gb_ref[...] * u).astype(BF16)

    return pl.pallas_call(
        body, name=name, grid=(nc, bsz),
        in_specs=[_strip(s, tc, 0), _strip(s, tc, nc), _strip(s, tc, 2 * nc), _cvec(cw.shape[0], tc, 0), _cvec(1, tc, 0)],
        out_specs=_strip(s, tc, 0),
        out_shape=jax.ShapeDtypeStruct((bsz, s, d), BF16),
        compiler_params=_params(("parallel", "parallel")),
    )(h, h, h, cw, cb)


def _sc_bwd(h, dq, cw, cb, *, name, tc=256):
    bsz, s, d3 = h.shape
    d = d3 // 3
    kw = cw.shape[0]
    tc = _tile(d, tc, LANES)
    nc = d // tc

    def body(gb_ref, gc_ref, v_ref, dq_ref, w_ref, b_ref, dh_ref, dw_ref, db_ref, parts):
        b_id, part = pl.program_id(1), pl.program_id(2)

        @pl.when(part == 0)
        def _():
            gb, gc, v, dq_, w = gb_ref[...], gc_ref[...], v_ref[...], dq_ref[...], w_ref[...]
            p = gc * v
            u = _conv_fwd(p, w, b_ref[...])
            du = dq_ * gb
            dp, dw_rows = _conv_bwd(du, p, w)
            parts[0] = (dq_ * u).astype(BF16)
            parts[1] = (dp * v).astype(BF16)
            parts[2] = (dp * gc).astype(BF16)
            _accumulate(b_id == 0, [(dw_ref, dw_rows), (db_ref, _colsum(du))])

        dh_ref[...] = parts[part]

    at = _ahead(nc, bsz)
    return pl.pallas_call(
        body, name=name, grid=(nc, bsz, 3),
        in_specs=[_strip(s, tc, 0, at), _strip(s, tc, nc, at), _strip(s, tc, 2 * nc, at), _strip(s, tc, 0, at),
                  _cvec(kw, tc, 0, at), _cvec(1, tc, 0, at)],
        out_specs=[pl.BlockSpec((None, s, tc), lambda c, b, p: (b, 0, p * nc + c)), _acc(kw, tc), _acc(1, tc)],
        out_shape=[jax.ShapeDtypeStruct((bsz, s, d3), BF16), jax.ShapeDtypeStruct((kw, d), F32),
                   jax.ShapeDtypeStruct((1, d), F32)],
        scratch_shapes=[pltpu.VMEM((3, s, tc), BF16)],
        compiler_params=_params(("parallel", "arbitrary", "arbitrary")),
    )(h, h, h, dq, cw, cb)


def _ffn_specs(s, tc, nc, kw):
    strip = pl.BlockSpec((None, s, tc), lambda b, c: (b, 0, c))
    halves = [pl.BlockSpec((kw, tc), lambda b, c: (0, c)), pl.BlockSpec((kw, tc), lambda b, c: (0, nc + c)),
              pl.BlockSpec((1, tc), lambda b, c: (0, c)), pl.BlockSpec((1, tc), lambda b, c: (0, nc + c))]
    return strip, halves


def _ffn_fwd(x, w_up, cw, cb, *, name, tc=256):
    bsz, s, d = x.shape
    f = w_up.shape[0] // 2
    kw = cw.shape[0]
    tc = _tile(f, tc, LANES)
    nc = f // tc
    nt = (((1,), (1,)), ((), ()))

    def body(x_ref, w_ref, wg_ref, wv_ref, bg_ref, bv_ref, hg_ref, hv_ref, g_ref, v_ref, a_ref):
        c0 = pl.multiple_of(pl.program_id(1) * tc, LANES)
        xs = x_ref[...]
        hg = lax.dot_general(xs, w_ref[pl.ds(c0, tc), :], nt, preferred_element_type=F32)
        hv = lax.dot_general(xs, w_ref[pl.ds(f + c0, tc), :], nt, preferred_element_type=F32)
        hg_ref[...] = hg
        hv_ref[...] = hv
        g = _conv_fwd(hg, wg_ref[...], bg_ref[...])
        v = _conv_fwd(hv, wv_ref[...], bv_ref[...])
        g_ref[...] = g
        v_ref[...] = v
        a_ref[...] = (g * _sigmoid(g) * v).astype(BF16)

    strip, halves = _ffn_specs(s, tc, nc, kw)
    return pl.pallas_call(
        body, name=name, grid=(bsz, nc),
        in_specs=[pl.BlockSpec((None, s, d), lambda b, c: (b, 0, 0)),
                  pl.BlockSpec(w_up.shape, lambda b, c: (0, 0), pipeline_mode=pl.Buffered(1))] + halves,
        out_specs=[strip] * 5,
        out_shape=[jax.ShapeDtypeStruct((bsz, s, f), F32)] * 4 + [jax.ShapeDtypeStruct((bsz, s, f), BF16)],
        compiler_params=_params(("parallel", "arbitrary")),
    )(x, w_up, cw, cw, cb, cb)


def _ffn_bwd(hg, hv, g, v, dz, w_down, cw, *, name, tc=256):
    bsz, s, f = hg.shape
    d = dz.shape[2]
    kw = cw.shape[0]
    tc = _tile(f, tc, LANES)
    nc = f // tc

    def body(hg_ref, hv_ref, g_ref, v_ref, dz_ref, wd_ref, wg_ref, wv_ref,
             dhg_ref, dhv_ref, dwg_ref, dwv_ref, dbg_ref, dbv_ref):
        c0 = pl.multiple_of(pl.program_id(1) * tc, LANES)
        cols = pl.ds(c0, tc)
        da = lax.dot_general(dz_ref[...], wd_ref[cols, :], (((1,), (1,)), ((), ())), preferred_element_type=F32)
        g_ = g_ref[...]
        sg = _sigmoid(g_)
        dv = da * (g_ * sg)
        dg = da * v_ref[...] * (sg * (1.0 + g_ * (1.0 - sg)))
        dhg, dwg_rows = _conv_bwd(dg, hg_ref[...], wg_ref[...])
        dhv, dwv_rows = _conv_bwd(dv, hv_ref[...], wv_ref[...])
        dhg_ref[...] = dhg.astype(BF16)
        dhv_ref[...] = dhv.astype(BF16)
        _accumulate(pl.program_id(0) == 0, [(dwg_ref, dwg_rows), (dwv_ref, dwv_rows),
                                            (dbg_ref, [_colsum(dg)]), (dbv_ref, [_colsum(dv)])], cols)

    strip, halves = _ffn_specs(s, tc, nc, kw)
    whole = lambda r: pl.BlockSpec((r, f), lambda b, c: (0, 0))
    return pl.pallas_call(
        body, name=name, grid=(bsz, nc),
        in_specs=[strip] * 4 + [pl.BlockSpec((None, s, d), lambda b, c: (b, 0, 0)),
                                pl.BlockSpec(w_down.shape, lambda b, c: (0, 0), pipeline_mode=pl.Buffered(1))]
        + halves[:2],
        out_specs=[strip, strip, whole(kw), whole(kw), whole(1), whole(1)],
        out_shape=[jax.ShapeDtypeStruct((bsz, s, f), BF16), jax.ShapeDtypeStruct((bsz, s, f), BF16),
                   jax.ShapeDtypeStruct((kw, f), F32), jax.ShapeDtypeStruct((kw, f), F32),
                   jax.ShapeDtypeStruct((1, f), F32), jax.ShapeDtypeStruct((1, f), F32)],
        compiler_params=_params(("arbitrary", "arbitrary")),
    )(hg, hv, g, v, dz, w_down, cw, cw)


def _lru_gates(r, cw, cb, wg, bg, lam):
    blk = r.shape[1]
    xr = _conv_fwd(r, cw, cb)
    gates = jnp.dot(xr.astype(BF16), wg, preferred_element_type=F32) + bg
    rg = _sigmoid(gates[:, :blk])
    ig = _sigmoid(gates[:, blk:])
    sp = _softplus(-lam)
    la = (-LRU_C * sp) * rg
    a = jnp.exp(la)
    mult = jnp.sqrt(-_expm1(2.0 * la, a * a))
    return xr, rg, ig, sp, a, mult


def _lru_fwd(h, cw, cb, wg, bg, lam, *, name):
    bsz, s, r2 = h.shape
    heads, blk = wg.shape[0], wg.shape[1]
    kw = cw.shape[0]

    def body(g_ref, r_ref, cw_ref, cb_ref, wg_ref, bg_ref, lam_ref, y_ref, sv_ref):
        xr, rg, ig, _, a, mult = _lru_gates(r_ref[...], cw_ref[...], cb_ref[...], wg_ref[...], bg_ref[...], lam_ref[...])
        hs = _scan_fwd(a, mult * (ig * xr))
        for n, val in enumerate((hs, xr, rg, ig, a, mult)):
            sv_ref[n] = val
        y_ref[...] = (hs * _gelu(g_ref[...])).astype(BF16)

    per_head = lambda hd, b: (hd, 0, 0)
    return pl.pallas_call(
        body, name=name, grid=(heads, bsz),
        in_specs=[_strip(s, blk, 0), _strip(s, blk, heads), _cvec(kw, blk, 0), _cvec(1, blk, 0),
                  pl.BlockSpec((None, blk, 2 * blk), per_head), pl.BlockSpec((None, 1, 2 * blk), per_head),
                  _cvec(1, blk, 0)],
        out_specs=[_strip(s, blk, 0), pl.BlockSpec((6, None, s, blk), lambda hd, b: (0, b, 0, hd))],
        out_shape=[jax.ShapeDtypeStruct((bsz, s, r2 // 2), BF16), jax.ShapeDtypeStruct((6, bsz, s, r2 // 2), F32)],
        compiler_params=_params(("parallel", "parallel")),
    )(h, h, cw, cb, wg, bg, lam)


def _lru_bwd(h, sv, dy, cw, wg, lam, *, name):
    bsz, s, r2 = h.shape
    rw = r2 // 2
    heads, blk = wg.shape[0], wg.shape[1]
    kw = cw.shape[0]

    def body(g_ref, r_ref, cw_ref, wg_ref, lam_ref, sv_ref, dy_ref,
             dh_ref, dcw_ref, dcb_ref, dwg_ref, dbg_ref, dlam_ref, sg_ref, sr_ref, parts):
        b_id, part = pl.program_id(1), pl.program_id(2)

        @pl.when(part == 0)
        def _():
            r, cw_, wg_, lam_ = r_ref[...], cw_ref[...], wg_ref[...], lam_ref[...]
            hs_, xr, rg, ig, a, mult = (sv_ref[n] for n in range(6))
            sp = _softplus(-lam_)
            dy_ = dy_ref[...]
            gel, dgel = _gelu_and_grad(g_ref[...])
            dg = dy_ * hs_ * dgel
            lmb = _scan_rev(_shift_up(a, 1, 1.0), dy_ * gel)
            da = lmb * _shift_dn(hs_, 1)
            dmult = lmb * (ig * xr)
            dig = lmb * (mult * xr)
            dxr = lmb * (mult * ig)
            dla = da * a - dmult * (a * a / mult)
            drg = dla * (-LRU_C * sp)
            dsp = _colsum(dla * rg) * (-LRU_C)
            dlam = -dsp * _sigmoid(-lam_)
            dgates = jnp.concatenate([drg * (rg * (1.0 - rg)), dig * (ig * (1.0 - ig))], axis=1)
            dgates_bf = dgates.astype(BF16)
            dwg = lax.dot_general(xr.astype(BF16), dgates_bf, (((0,), (0,)), ((), ())), preferred_element_type=F32)
            dxr = dxr + lax.dot_general(dgates_bf, wg_, (((1,), (1,)), ((), ())), preferred_element_type=F32)
            dr, dcw_rows = _conv_bwd(dxr, r, cw_)
            parts[0] = dg.astype(BF16)
            parts[1] = dr.astype(BF16)
            _accumulate(b_id == 0, [(dcw_ref, dcw_rows), (dcb_ref, _colsum(dxr)), (dwg_ref, dwg),
                                    (dbg_ref, _colsum(dgates)), (dlam_ref, dlam), (sg_ref, _colsum(dg)),
                                    (sr_ref, _colsum(dr))])

        dh_ref[...] = parts[part]

    at = _ahead(heads, bsz)

    def saved(*ids):
        hd, b = at(*ids)
        return 0, b, 0, hd

    vec = pl.BlockSpec((1, blk), lambda hd, b, p: (0, hd))
    return pl.pallas_call(
        body, name=name, grid=(heads, bsz, 2),
        in_specs=[_strip(s, blk, 0, at), _strip(s, blk, heads, at), _cvec(kw, blk, 0, at),
                  pl.BlockSpec((None, blk, 2 * blk), lambda *ids: (at(*ids)[0], 0, 0)), _cvec(1, blk, 0, at),
                  pl.BlockSpec((6, None, s, blk), saved), _strip(s, blk, 0, at)],
        out_specs=[pl.BlockSpec((None, s, blk), lambda hd, b, p: (b, 0, p * heads + hd)),
                   pl.BlockSpec((kw, blk), lambda hd, b, p: (0, hd)), vec,
                   pl.BlockSpec((None, blk, 2 * blk), lambda hd, b, p: (hd, 0, 0)),
                   pl.BlockSpec((None, 1, 2 * blk), lambda hd, b, p: (hd, 0, 0)), vec, vec, vec],
        out_shape=[jax.ShapeDtypeStruct((bsz, s, r2), BF16), jax.ShapeDtypeStruct((kw, rw), F32),
                   jax.ShapeDtypeStruct((1, rw), F32), jax.ShapeDtypeStruct((heads, blk, 2 * blk), F32),
                   jax.ShapeDtypeStruct((heads, 1, 2 * blk), F32), jax.ShapeDtypeStruct((1, rw), F32),
                   jax.ShapeDtypeStruct((1, rw), F32), jax.ShapeDtypeStruct((1, rw), F32)],
        scratch_shapes=[pltpu.VMEM((2, s, blk), BF16)],
        compiler_params=_params(("parallel", "arbitrary", "arbitrary")),
    )(h, h, cw, wg, lam, sv, dy)


HBM_SPEC = pl.BlockSpec(memory_space=pltpu.HBM)
SEM_SPEC = pl.BlockSpec(memory_space=pltpu.SEMAPHORE)
EFFECT = pltpu.SideEffectType.DATAFLOW_SIDE_EFFECTING


def _peer_copies(srcs, lands, gather, send_sem, recv_sem):
    x, y, c = (lax.axis_index(ax) for ax in MESH_AXES)
    me = 4 * x + 2 * y + c
    copies = []
    for i in range(len(srcs)):
        for d in range(1, N_DEV):
            px = 1 - x if d & 4 else x
            py = 1 - y if d & 2 else y
            pc = 1 - c if d & 1 else c
            src = srcs[i] if gather[i] else srcs[i].at[4 * px + 2 * py + pc]
            k = i * (N_DEV - 1) + d - 1
            copies.append(pltpu.make_async_remote_copy(
                src_ref=src, dst_ref=lands[i].at[me], send_sem=send_sem.at[k], recv_sem=recv_sem.at[k],
                device_id=(px, py, pc), device_id_type=pl.DeviceIdType.MESH))
    return copies


def _exchange_start(arrs, gather, *, name):
    n = len(arrs)
    lands = [lax.empty((N_DEV,) + tuple(a.shape if g else a.shape[1:]), a.dtype) for a, g in zip(arrs, gather)]

    def body(*refs):
        srcs, land_refs = refs[:n], refs[n:2 * n]
        send_sem, recv_sem = refs[2 * n], refs[2 * n + 1]
        token = refs[-1]
        for cp in _peer_copies(srcs, land_refs, gather, send_sem, recv_sem):
            cp.start()
        token[...] = jnp.zeros_like(token)

    sems = pltpu.SemaphoreType.DMA((n * (N_DEV - 1),))
    thru = [pltpu.HBM(a.shape, a.dtype) for a in arrs + lands]
    out = pl.pallas_call(
        body, name=name, in_specs=[HBM_SPEC] * (2 * n),
        out_shape=(sems, sems, *thru, jax.ShapeDtypeStruct((SUBLANES, LANES), F32)),
        out_specs=(SEM_SPEC, SEM_SPEC, *([HBM_SPEC] * (2 * n)), pl.BlockSpec(memory_space=pltpu.VMEM)),
        input_output_aliases={i: 2 + i for i in range(2 * n)},
        compiler_params=pltpu.CompilerParams(has_side_effects=EFFECT),
    )(*[pltpu.with_memory_space_constraint(a, pltpu.HBM) for a in arrs + lands])
    return {"send_sem": out[0], "recv_sem": out[1], "srcs": list(out[2:2 + n]), "lands": list(out[2 + n:2 + 2 * n]),
            "token": out[-1], "gather": list(gather)}


def _exchange_wait(handle, after, *, name):
    srcs, lands, gather = handle["srcs"], handle["lands"], handle["gather"]
    n = len(srcs)

    def body(*refs):
        src_refs, land_refs = refs[:n], refs[n:2 * n]
        send_sem, recv_sem = refs[2 * n], refs[2 * n + 1]
        for cp in _peer_copies(src_refs, land_refs, gather, send_sem, recv_sem):
            cp.wait_send()
            cp.wait_recv()

    out = pl.pallas_call(
        body, name=name,
        in_specs=[HBM_SPEC] * (2 * n) + [SEM_SPEC, SEM_SPEC, pl.BlockSpec(memory_space=pl.ANY)],
        out_shape=tuple(pltpu.HBM(a.shape, a.dtype) for a in srcs + lands), out_specs=tuple([HBM_SPEC] * (2 * n)),
        input_output_aliases={i: i for i in range(2 * n)},
        compiler_params=pltpu.CompilerParams(has_side_effects=EFFECT),
    )(*srcs, *lands, handle["send_sem"], handle["recv_sem"], after)
    return list(out[:n]), list(out[n:])


def _adamw(parts, w, m, v, layer, so_far, *, name, tr=256):
    n_layers, r, c = w.shape
    tr = _tile(r, tr, SUBLANES)
    bc1 = 1.0 / (1.0 - ADAM_B1 ** ADAM_STEP)
    bc2 = 1.0 / (1.0 - ADAM_B2 ** ADAM_STEP)
    if so_far is None:
        so_far = [lax.empty(w.shape, F32) for _ in range(4)]

    def body(p_ref, w_ref, m_ref, v_ref, *rest):
        g_ref, d_ref, mo_ref, vo_ref = rest[4:]
        g = p_ref[0].astype(F32)
        for s in range(1, N_DEV):
            g = g + p_ref[s].astype(F32)
        m_new = ADAM_B1 * m_ref[...] + (1.0 - ADAM_B1) * g
        v_new = ADAM_B2 * v_ref[...] + (1.0 - ADAM_B2) * (g * g)
        g_ref[...] = g
        mo_ref[...] = m_new
        vo_ref[...] = v_new
        d_ref[...] = -ADAM_LR * ((m_new * bc1) / (jnp.sqrt(v_new * bc2) + ADAM_EPS) + ADAM_WD * w_ref[...])

    blk = pl.BlockSpec((None, tr, c), lambda i: (layer, i, 0))
    return pl.pallas_call(
        body, name=name, grid=(r // tr,),
        in_specs=[pl.BlockSpec((N_DEV, tr, c), lambda i: (0, i, 0)), blk, blk, blk]
        + [pl.BlockSpec(memory_space=pl.ANY)] * 4,
        out_specs=[blk] * 4, out_shape=[jax.ShapeDtypeStruct(w.shape, F32)] * 4,
        input_output_aliases={4 + o: o for o in range(4)},
        compiler_params=_params(("parallel",)),
    )(parts, w, m, v, *so_far)


def _whole(slabs, axis):
    x = jnp.moveaxis(slabs, 0, axis)
    shp = x.shape
    return x.reshape(shp[:axis] + (shp[axis] * shp[axis + 1],) + shp[axis + 2:])


def _slabs(whole, axis):
    shp = whole.shape
    x = whole.reshape(shp[:axis] + (N_DEV, shp[axis] // N_DEV) + shp[axis + 1:])
    return jnp.moveaxis(x, axis, 0)


def _pack(vecs, rows):
    flat = jnp.concatenate(vecs, axis=-1)
    pad = rows * LANES - flat.shape[-1]
    flat = jnp.pad(flat, [(0, 0)] * (flat.ndim - 1) + [(0, pad)])
    return flat.reshape(flat.shape[:-1] + (rows, LANES))


def _unpack(packed, sizes):
    flat = packed.reshape(packed.shape[:-2] + (-1,))
    out, pos = [], 0
    for n in sizes:
        out.append(flat[..., pos:pos + n])
        pos += n
    return out


def _pack_rows(sizes):
    total = sum(sizes)
    return -(-total // (LANES * SUBLANES)) * SUBLANES


BIG = {"sc_w_in": 2, "sc_w_out": 1, "lru_w_in": 2, "lru_w_gate": 3, "lru_w_out": 1, "ffn_w_up": 2, "ffn_w_down": 1}
SWAPPED = ("ffn_w_up", "lru_w_in")
TRANSPOSED = ("ffn_w_up",)
SMALL = ["sc_conv_w", "lru_b_in", "lru_conv_w", "lru_conv_b", "lru_b_gate", "lru_lambda", "ffn_conv_w", "ln_g", "ln_b"]
REPL = ["sc_conv_b", "ffn_conv_b"]
WEIGHTS = ["sc_w_in", "sc_conv_w", "sc_conv_b", "sc_w_out", "lru_w_in", "lru_b_in", "lru_conv_w", "lru_conv_b",
           "lru_w_gate", "lru_b_gate", "lru_lambda", "lru_w_out", "ffn_w_up", "ffn_conv_w", "ffn_conv_b", "ffn_w_down",
           "ln_g", "ln_b"]


STAGES_PER_LAYER = 3


def _stage_big(g):
    i, part = divmod(g, STAGES_PER_LAYER)
    j = i // 2
    if part:
        return [("ffn_w_up" if part == 1 else "ffn_w_down", i)]
    return [("sc_w_in", j), ("sc_w_out", j)] if i % 2 == 0 else [("lru_w_in", j), ("lru_w_gate", j), ("lru_w_out", j)]


def _step(x, loss_target, w, m, v):
    bsz, s, d = x.shape
    t = bsz * s
    depth = w["ffn_w_up"].shape[0]
    alpha = (2.0 * depth) ** 0.25
    heads = w["lru_w_gate"].shape[1]

    small_sizes = [w[k].size for k in SMALL]
    small_rows = _pack_rows(small_sizes)
    small_local = _pack([w[k].reshape(1, -1) for k in SMALL], small_rows)[0]
    me = 4 * lax.axis_index("x") + 2 * lax.axis_index("y") + lax.axis_index("c")

    def with_own(land, own):
        return lax.dynamic_update_slice_in_dim(land, own, me, axis=0)

    stages = STAGES_PER_LAYER * depth
    def held(k, arr):
        return jnp.swapaxes(arr, -1, -2) if k in TRANSPOSED else arr

    def split_axis(k):
        return 0 if k in TRANSPOSED else BIG[k] - 1

    gathers, tok = [], None
    for g in range(stages):
        arrs = [held(k, w[k][l]).astype(BF16) for k, l in _stage_big(g)]
        if g == 0:
            arrs.append(small_local)
        if tok is not None:
            arrs[0] = arrs[0] + tok.astype(BF16)
        gathers.append(_exchange_start(arrs, [True] * len(arrs), name=f"gather_start_{g}"))
        tok = gathers[-1]["token"][0, 0]
    full = {k: [None] * w[k].shape[0] for k in BIG}
    full["sc_conv_b"] = w["sc_conv_b"]
    full["ffn_conv_b"] = w["ffn_conv_b"]

    def arrive(g, after):
        srcs, lands = _exchange_wait(gathers[g], after, name=f"gather_wait_{g}")
        for (k, l), src, land in zip(_stage_big(g), srcs, lands):
            full[k][l] = _whole(with_own(land, src[None]), split_axis(k))
        if g == 0:
            for k, seg in zip(SMALL, _unpack(with_own(lands[-1], srcs[-1][None]), small_sizes)):
                full[k] = _whole(seg.reshape((N_DEV,) + w[k].shape), w[k].ndim - 1)

    stream, stream_ln = x.reshape(t, d), None
    xb = stream.astype(BF16)
    saved = []
    for i in range(depth):
        j = i // 2
        arrive(3 * i, gathers[-1]["token"] if i == 0 else xb)
        lng, lnb = full["ln_g"][i], full["ln_b"][i]
        sv = {"x0": xb}
        if i % 2 == 0:
            hm = _mm(xb, full["sc_w_in"][j], name="sc_in")
            q = _sc_fwd(hm.reshape(bsz, s, -1), full["sc_conv_w"][j], full["sc_conv_b"][j:j + 1], name="sc_mix")
            w_out = full["sc_w_out"][j]
        else:
            hm = _mm(xb, full["lru_w_in"][j], bias=full["lru_b_in"][j:j + 1], name="lru_in")
            q, hs = _lru_fwd(hm.reshape(bsz, s, -1), full["lru_conv_w"][j], full["lru_conv_b"][j:j + 1],
                             full["lru_w_gate"][j], full["lru_b_gate"][j].reshape(heads, 1, -1),
                             full["lru_lambda"][j:j + 1], name="lru_mix")
            sv["hs"] = hs
            w_out = full["lru_w_out"][j]
        q = q.reshape(t, -1)
        arrive(3 * i + 1, q)
        z1, x1b = _mm_ln(q, w_out, stream, alpha, lng[0:1], lnb[0:1], resid_ln=stream_ln, name="mix_out_ln")
        hg, hv, gc, vc, a = _ffn_fwd(x1b.reshape(bsz, s, d), full["ffn_w_up"][i], full["ffn_conv_w"][i],
                                     full["ffn_conv_b"][i:i + 1], name="ffn_up_act")
        a = a.reshape(t, -1)
        arrive(3 * i + 2, a)
        z2, xb = _mm_ln(a, full["ffn_w_down"][i], z1, alpha, lng[1:2], lnb[1:2], resid_ln=(lng[0:1], lnb[0:1]),
                        name="ffn_down_ln")
        stream, stream_ln = z2, (lng[1:2], lnb[1:2])
        sv.update(hm=hm, q=q, z1=z1, x1=x1b, ffn=(hg, hv, gc, vc), a=a, z2=z2)
        saved.append(sv)

    sq, dx = _loss_head(stream, *stream_ln, loss_target.reshape(t, d), name="loss_head")
    loss = lax.psum((0.5 / d) * sq[0, 0], MESH_AXES)

    grads = {k: [None] * w[k].shape[0] for k in WEIGHTS}
    scatters = [None] * stages

    def as_updated(k, arr):
        return jnp.swapaxes(arr, -1, -2) if k in SWAPPED else arr

    def depart(g):
        send = [_slabs(grads[k][l], split_axis(k)).astype(BF16) for k, l in _stage_big(g)]
        send = [sl if k in TRANSPOSED else as_updated(k, sl) for (k, l), sl in zip(_stage_big(g), send)]
        scatters[g] = _exchange_start(send, [False] * len(send), name=f"scatter_start_{g}")
        return scatters[g]["token"][0:1, 0:1]

    dz2, dz2b, dg2, db2 = _ln_bwd(dx, saved[-1]["z2"], full["ln_g"][-1][1:2], name="ln_bwd")
    for i in reversed(range(depth)):
        j = i // 2
        sv = saved[i]
        lng = full["ln_g"][i]
        grads["ffn_w_down"][i] = _mm_tn(sv["a"], dz2b, name="ffn_down_dw")
        dhg, dhv, dwg, dwv, dbg, dbv = _ffn_bwd(*sv["ffn"], dz2b.reshape(bsz, s, d), full["ffn_w_down"][i],
                                                full["ffn_conv_w"][i] + depart(3 * i + 2), name="ffn_act_bwd")
        dhg, dhv = dhg.reshape(t, -1), dhv.reshape(t, -1)
        grads["ffn_conv_w"][i] = jnp.concatenate([dwg, dwv], axis=1)
        grads["ffn_conv_b"][i] = jnp.concatenate([dbg, dbv], axis=1)[0]
        grads["ffn_w_up"][i] = jnp.concatenate([_mm_tn(dhg, sv["x1"], name="ffn_up_dw_g"),
                                                _mm_tn(dhv, sv["x1"], name="ffn_up_dw_v")], axis=0)
        dz1, dz1b, dg1, db1 = _mm_ln_bwd([dhg, dhv], full["ffn_w_up"][i], dz2, alpha, sv["z1"],
                                         lng[0:1] + depart(3 * i + 1), name="ffn_up_dx_ln", w_rows_are_k=True)
        grads["ln_g"][i] = jnp.concatenate([dg1, dg2], axis=0)
        grads["ln_b"][i] = jnp.concatenate([db1, db2], axis=0)
        if i % 2 == 0:
            dq = _mm(dz1b, full["sc_w_out"][j], trans_w=True, name="sc_out_dx")
            grads["sc_w_out"][j] = _mm_tn(sv["q"], dz1b, name="sc_out_dw")
            dhm, dcw, dcb = _sc_bwd(sv["hm"].reshape(bsz, s, -1), dq.reshape(bsz, s, -1), full["sc_conv_w"][j],
                                    full["sc_conv_b"][j:j + 1], name="sc_mix_bwd")
            dhm = dhm.reshape(t, -1)
            grads["sc_conv_w"][j] = dcw
            grads["sc_conv_b"][j] = dcb[0]
            grads["sc_w_in"][j] = _mm_tn(sv["x0"], dhm, name="sc_in_dw")
            w_in = full["sc_w_in"][j]
        else:
            dq = _mm(dz1b, full["lru_w_out"][j], trans_w=True, name="lru_out_dx")
            grads["lru_w_out"][j] = _mm_tn(sv["q"], dz1b, name="lru_out_dw")
            dhm, dcw, dcb, dwgt, dbgt, dlam, sgb, srb = _lru_bwd(
                sv["hm"].reshape(bsz, s, -1), sv["hs"], dq.reshape(bsz, s, -1), full["lru_conv_w"][j],
                full["lru_w_gate"][j], full["lru_lambda"][j:j + 1], name="lru_mix_bwd")
            dhm = dhm.reshape(t, -1)
            grads["lru_conv_w"][j] = dcw
            grads["lru_conv_b"][j] = dcb[0]
            grads["lru_w_gate"][j] = dwgt
            grads["lru_b_gate"][j] = dbgt[:, 0, :]
            grads["lru_lambda"][j] = dlam[0]
            grads["lru_b_in"][j] = jnp.concatenate([sgb, srb], axis=1)[0]
            grads["lru_w_in"][j] = _mm_tn(sv["x0"], dhm, name="lru_in_dw")
            w_in = full["lru_w_in"][j]
        tok = depart(3 * i)
        if i > 0:
            dz2, dz2b, dg2, db2 = _mm_ln_bwd([dhm], w_in, dz1, alpha, saved[i - 1]["z2"], full["ln_g"][i - 1][1:2] + tok,
                                             name="mix_in_dx_ln")
        else:
            dx = _mm(dhm, w_in + tok[0, 0].astype(BF16), trans_w=True, resid=dz1, resid_scale=alpha, name="mix_in_dx")
    grad_x = dx.reshape(bsz, s, d)

    gsm = {k: jnp.stack(grads[k]) for k in SMALL + REPL}
    small_send = _pack([_slabs(gsm[k], gsm[k].ndim - 1).reshape(N_DEV, -1) for k in SMALL], small_rows)
    repl_sizes = [w[k].size for k in REPL]
    repl_rows = _pack_rows(repl_sizes)
    repl_send = _pack([gsm[k].reshape(1, -1) for k in REPL], repl_rows)[0]
    small_scatter = _exchange_start([small_send, repl_send], [False, True], name="scatter_start_small")

    out = {}

    def own_slab(src):
        return lax.dynamic_slice_in_dim(src, me, 1, axis=0)

    stacks = {k: None for k in BIG}
    after = dx
    for g in reversed(range(stages)):
        srcs, lands = _exchange_wait(scatters[g], after, name=f"scatter_wait_{g}")
        for (k, l), src, land in zip(_stage_big(g), srcs, lands):
            n_l, c2 = w[k].shape[0], land.shape[-1]
            wk, mk, vk = (as_updated(k, arr[k]).reshape(n_l, -1, c2) for arr in (w, m, v))
            stacks[k] = _adamw(with_own(land, own_slab(src)).reshape(N_DEV, -1, c2), wk, mk, vk, l, stacks[k],
                               name=f"adamw_{k}_{l}")
            after = stacks[k][-1]
    for k in BIG:
        shp = as_updated(k, w[k]).shape
        out[k] = [as_updated(k, r.reshape(shp)) for r in stacks[k]]
    srcs, lands = _exchange_wait(small_scatter, after, name="scatter_wait_small")
    got_small = with_own(lands[0], own_slab(srcs[0]))
    got_repl = with_own(lands[1], srcs[1][None])
    pk = lambda src, names, rows: _pack([src[k].reshape(1, -1) for k in names], rows)
    res = _adamw(got_small, small_local[None], pk(m, SMALL, small_rows), pk(v, SMALL, small_rows), 0, None,
                 name="adamw_small")
    for r_i, r in enumerate(res):
        for k, seg in zip(SMALL, _unpack(r[0], small_sizes)):
            out.setdefault(k, [None] * 4)[r_i] = seg.reshape(w[k].shape)
    res = _adamw(got_repl, pk(w, REPL, repl_rows), pk(m, REPL, repl_rows), pk(v, REPL, repl_rows), 0, None,
                 name="adamw_repl")
    for r_i, r in enumerate(res):
        for k, seg in zip(REPL, _unpack(r[0], repl_sizes)):
            out.setdefault(k, [None] * 4)[r_i] = seg.reshape(w[k].shape)

    return (loss, grad_x, *[out[k][0] for k in WEIGHTS], *[out[k][1] for k in WEIGHTS],
            *[out[k][2] for k in WEIGHTS], *[out[k][3] for k in WEIGHTS])


def kernel(x, sc_w_in, sc_conv_w, sc_conv_b, sc_w_out, lru_w_in, lru_b_in, lru_conv_w, lru_conv_b, lru_w_gate, lru_b_gate, lru_lambda, lru_w_out, ffn_w_up, ffn_conv_w, ffn_conv_b, ffn_w_down, ln_g, ln_b, loss_target, m_sc_w_in, m_sc_conv_w, m_sc_conv_b, m_sc_w_out, m_lru_w_in, m_lru_b_in, m_lru_conv_w, m_lru_conv_b, m_lru_w_gate, m_lru_b_gate, m_lru_lambda, m_lru_w_out, m_ffn_w_up, m_ffn_conv_w, m_ffn_conv_b, m_ffn_w_down, m_ln_g, m_ln_b, v_sc_w_in, v_sc_conv_w, v_sc_conv_b, v_sc_w_out, v_lru_w_in, v_lru_b_in, v_lru_conv_w, v_lru_conv_b, v_lru_w_gate, v_lru_b_gate, v_lru_lambda, v_lru_w_out, v_ffn_w_up, v_ffn_conv_w, v_ffn_conv_b, v_ffn_w_down, v_ln_g, v_ln_b):
    w = dict(sc_w_in=sc_w_in, sc_conv_w=sc_conv_w, sc_conv_b=sc_conv_b, sc_w_out=sc_w_out, lru_w_in=lru_w_in,
             lru_b_in=lru_b_in, lru_conv_w=lru_conv_w, lru_conv_b=lru_conv_b, lru_w_gate=lru_w_gate,
             lru_b_gate=lru_b_gate, lru_lambda=lru_lambda, lru_w_out=lru_w_out, ffn_w_up=ffn_w_up,
             ffn_conv_w=ffn_conv_w, ffn_conv_b=ffn_conv_b, ffn_w_down=ffn_w_down, ln_g=ln_g, ln_b=ln_b)
    m = dict(sc_w_in=m_sc_w_in, sc_conv_w=m_sc_conv_w, sc_conv_b=m_sc_conv_b, sc_w_out=m_sc_w_out, lru_w_in=m_lru_w_in,
             lru_b_in=m_lru_b_in, lru_conv_w=m_lru_conv_w, lru_conv_b=m_lru_conv_b, lru_w_gate=m_lru_w_gate,
             lru_b_gate=m_lru_b_gate, lru_lambda=m_lru_lambda, lru_w_out=m_lru_w_out, ffn_w_up=m_ffn_w_up,
             ffn_conv_w=m_ffn_conv_w, ffn_conv_b=m_ffn_conv_b, ffn_w_down=m_ffn_w_down, ln_g=m_ln_g, ln_b=m_ln_b)
    v = dict(sc_w_in=v_sc_w_in, sc_conv_w=v_sc_conv_w, sc_conv_b=v_sc_conv_b, sc_w_out=v_sc_w_out, lru_w_in=v_lru_w_in,
             lru_b_in=v_lru_b_in, lru_conv_w=v_lru_conv_w, lru_conv_b=v_lru_conv_b, lru_w_gate=v_lru_w_gate,
             lru_b_gate=v_lru_b_gate, lru_lambda=v_lru_lambda, lru_w_out=v_lru_w_out, ffn_w_up=v_ffn_w_up,
             ffn_conv_w=v_ffn_conv_w, ffn_conv_b=v_ffn_conv_b, ffn_w_down=v_ffn_w_down, ln_g=v_ln_g, ln_b=v_ln_b)
    return _step(x, loss_target, w, m, v)
```

```python
import functools
import math

import jax
import jax.numpy as jnp
from jax import lax
from jax.experimental import pallas as pl
from jax.experimental.pallas import tpu as pltpu

F32 = jnp.float32
BF16 = jnp.bfloat16

N_DEV = 8
MESH_AXES = ("x", "y", "c")
LANES = 128
SUBLANES = 8
VMEM_LIMIT = 56 * 1024 * 1024
MM_LHS_ELEMS = 3 * 1024 * 1024
MM_TN = 1536

LRU_C = 8.0
LN_EPS = 1e-5
ADAM_LR = 0.001
ADAM_B1 = 0.9
ADAM_B2 = 0.999
ADAM_EPS = 1e-08
ADAM_WD = 0.01
ADAM_STEP = 10
GELU_K = math.sqrt(2.0 / math.pi)
GELU_C = 0.044715


def _tile(n, target, align):
    if n <= target:
        return n
    t = (target // align) * align
    while t >= align:
        if n % t == 0:
            return t
        t -= align
    return n


def _params(sem):
    return pltpu.CompilerParams(dimension_semantics=sem, vmem_limit_bytes=VMEM_LIMIT)


def _rows(x):
    return lax.broadcasted_iota(jnp.int32, x.shape, 0)


def _shift_dn(x, k, fill=0.0):
    if k == 0:
        return x
    return jnp.where(_rows(x) >= k, pltpu.roll(x, k, 0), fill)


def _shift_up(x, k, fill=0.0):
    if k == 0:
        return x
    s = x.shape[0]
    return jnp.where(_rows(x) < s - k, pltpu.roll(x, s - k, 0), fill)


def _conv_fwd(x, w, b):
    kw = w.shape[0]
    y = _shift_dn(x, kw - 1) * w[0:1, :] + b
    for k in range(1, kw):
        y = y + _shift_dn(x, kw - 1 - k) * w[k:k + 1, :]
    return y


def _conv_bwd(dy, x, w):
    kw = w.shape[0]
    ahead = [_shift_up(dy, j) for j in range(kw)]
    dx = ahead[kw - 1] * w[0:1, :]
    for k in range(1, kw):
        dx = dx + ahead[kw - 1 - k] * w[k:k + 1, :]
    return dx, [_colsum(ahead[kw - 1 - k] * x) for k in range(kw)]


def _accumulate(first, items, cols=slice(None)):
    flat = []
    for ref, val in items:
        if isinstance(val, list):
            flat += [(ref, (slice(k, k + 1), cols), row) for k, row in enumerate(val)]
        else:
            flat.append((ref, Ellipsis, val))

    @pl.when(first)
    def _():
        for ref, idx, val in flat:
            ref[idx] = val

    @pl.when(jnp.logical_not(first))
    def _():
        for ref, idx, val in flat:
            ref[idx] += val


def _colsum(x):
    return jnp.sum(x, axis=0, keepdims=True)


def _sigmoid(x):
    return 1.0 / (1.0 + jnp.exp(-x))


def _log1p(x):
    u = 1.0 + x
    return jnp.where(u == 1.0, x, jnp.log(u) * (x / (u - 1.0)))


def _softplus(x):
    return jnp.maximum(x, 0.0) + _log1p(jnp.exp(-jnp.abs(x)))


def _expm1(x, ex):
    poly = x * (1.0 + x * (0.5 + x * (1.0 / 6.0 + x * (1.0 / 24.0 + x * (1.0 / 120.0 + x * (1.0 / 720.0))))))
    return jnp.where(jnp.abs(x) < 0.25, poly, ex - 1.0)


def _gelu(x):
    t = jnp.tanh(GELU_K * (x + GELU_C * x * x * x))
    return 0.5 * x * (1.0 + t)


def _gelu_and_grad(x):
    x2 = x * x
    t = jnp.tanh(GELU_K * (x + GELU_C * x * x2))
    g = 0.5 * x * (1.0 + t)
    dg = 0.5 * (1.0 + t) + 0.5 * x * (1.0 - t * t) * (GELU_K * (1.0 + 3.0 * GELU_C * x2))
    return g, dg


def _scan_fwd(a, b):
    s = a.shape[0]
    k = 1
    while k < s:
        last = 2 * k >= s
        if k % SUBLANES:
            b = a * _shift_dn(b, k) + b
            if not last:
                a = a * _shift_dn(a, k, 1.0)
        else:
            b = jnp.concatenate([b[:k], a[k:] * b[:s - k] + b[k:]], axis=0)
            if not last:
                a = jnp.concatenate([a[:k], a[k:] * a[:s - k]], axis=0)
        k *= 2
    return b


def _scan_rev(c, v):
    s = c.shape[0]
    k = 1
    while k < s:
        last = 2 * k >= s
        if k % SUBLANES:
            v = c * _shift_up(v, k) + v
            if not last:
                c = c * _shift_up(c, k, 1.0)
        else:
            v = jnp.concatenate([c[:s - k] * v[k:] + v[:s - k], v[s - k:]], axis=0)
            if not last:
                c = jnp.concatenate([c[:s - k] * c[k:], c[s - k:]], axis=0)
        k *= 2
    return v


def _mm(a, w, *, name, trans_w=False, bias=None, resid=None, resid_scale=1.0):
    m, k = a.shape
    n = w.shape[0] if trans_w else w.shape[1]
    tm = _tile(m, min(1024, max(256, MM_LHS_ELEMS // k)), SUBLANES)
    tn = _tile(n, MM_TN, LANES)
    has_bias = bias is not None
    has_resid = resid is not None

    def body(*refs):
        a_ref, w_ref = refs[0], refs[1]
        pos = 2
        b_ref = r_ref = None
        if has_bias:
            b_ref = refs[pos]
            pos += 1
        if has_resid:
            r_ref = refs[pos]
            pos += 1
        o_ref = refs[pos]

        cols = pl.ds(pl.multiple_of(pl.program_id(1) * tn, LANES), tn)
        if trans_w:
            acc = lax.dot_general(a_ref[...], w_ref[cols, :], (((1,), (1,)), ((), ())), preferred_element_type=F32)
        else:
            acc = jnp.dot(a_ref[...], w_ref[:, cols], preferred_element_type=F32)
        if has_bias:
            acc = acc + b_ref[...]
        if has_resid:
            acc = acc + resid_scale * r_ref[...]
        o_ref[...] = acc

    in_specs = [pl.BlockSpec((tm, k), lambda i, j: (i, 0)),
                pl.BlockSpec(w.shape, lambda i, j: (0, 0), pipeline_mode=pl.Buffered(1))]
    args = [a, w]
    if has_bias:
        in_specs.append(pl.BlockSpec((1, tn), lambda i, j: (0, j)))
        args.append(bias)
    if has_resid:
        in_specs.append(pl.BlockSpec((tm, tn), lambda i, j: (i, j)))
        args.append(resid)
    return pl.pallas_call(
        body, name=name, grid=(m // tm, n // tn), in_specs=in_specs,
        out_specs=pl.BlockSpec((tm, tn), lambda i, j: (i, j)),
        out_shape=jax.ShapeDtypeStruct((m, n), F32),
        compiler_params=_params(("parallel", "arbitrary")),
    )(*args)


def _ln(z, g, b):
    mu = jnp.mean(z, axis=-1, keepdims=True)
    zc = z - mu
    var = jnp.mean(zc * zc, axis=-1, keepdims=True)
    return zc * lax.rsqrt(var + LN_EPS) * g + b


def _mm_ln(a, w, resid, alpha, g, b, *, name, resid_ln=None, tm=512):
    m, k = a.shape
    d = w.shape[1]
    tm = _tile(m, tm, SUBLANES)
    n_extra = 0 if resid_ln is None else 2

    def body(a_ref, w_ref, r_ref, g_ref, b_ref, *rest):
        z_ref, obf_ref = rest[n_extra:]
        x = r_ref[...]
        if resid_ln is not None:
            x = _ln(x, rest[0][...], rest[1][...])
        z = alpha * x + jnp.dot(a_ref[...], w_ref[...], preferred_element_type=F32)
        z_ref[...] = z
        obf_ref[...] = _ln(z, g_ref[...], b_ref[...]).astype(BF16)

    row = pl.BlockSpec((tm, d), lambda i: (i, 0))
    vec = pl.BlockSpec((1, d), lambda i: (0, 0))
    return pl.pallas_call(
        body, name=name, grid=(m // tm,),
        in_specs=[pl.BlockSpec((tm, k), lambda i: (i, 0)),
                  pl.BlockSpec((k, d), lambda i: (0, 0), pipeline_mode=pl.Buffered(1)), row, vec, vec]
        + [vec] * n_extra,
        out_specs=[row, row],
        out_shape=[jax.ShapeDtypeStruct((m, d), F32), jax.ShapeDtypeStruct((m, d), BF16)],
        compiler_params=_params(("parallel",)),
    )(a, w, resid, g, b, *(resid_ln or ()))


def _ln_bwd_math(do, z, g):
    mu = jnp.mean(z, axis=-1, keepdims=True)
    zc = z - mu
    var = jnp.mean(zc * zc, axis=-1, keepdims=True)
    rstd = lax.rsqrt(var + LN_EPS)
    xhat = zc * rstd
    dxh = do * g
    m1 = jnp.mean(dxh, axis=-1, keepdims=True)
    m2 = jnp.mean(dxh * xhat, axis=-1, keepdims=True)
    return rstd * (dxh - m1 - xhat * m2), _colsum(do * xhat), _colsum(do)


def _mm_ln_bwd(parts, w, resid, resid_scale, z, g, *, name, w_rows_are_k=False):
    t, kp = parts[0].shape
    k, d = w.shape if w_rows_are_k else w.shape[::-1]
    n = len(parts)
    tm = _tile(t, min(512, max(256, MM_LHS_ELEMS // k)), SUBLANES)

    def body(*refs):
        a_refs = refs[:n]
        w_ref, r_ref, z_ref, g_ref, dz_ref, dzbf_ref, dg_ref, db_ref = refs[n:]

        @pl.when(pl.program_id(0) == 0)
        def _():
            dg_ref[...] = jnp.zeros_like(dg_ref)
            db_ref[...] = jnp.zeros_like(db_ref)

        dx = resid_scale * r_ref[...]
        for p, a_ref in enumerate(a_refs):
            if w_rows_are_k:
                dx = dx + jnp.dot(a_ref[...], w_ref[p * kp:(p + 1) * kp, :], preferred_element_type=F32)
            else:
                dx = dx + lax.dot_general(a_ref[...], w_ref[:, p * kp:(p + 1) * kp], (((1,), (1,)), ((), ())),
                                          preferred_element_type=F32)
        dz, dg, db = _ln_bwd_math(dx, z_ref[...], g_ref[...])
        dz_ref[...] = dz
        dzbf_ref[...] = dz.astype(BF16)
        dg_ref[...] += dg
        db_ref[...] += db

    row = pl.BlockSpec((tm, d), lambda i: (i, 0))
    vec = pl.BlockSpec((1, d), lambda i: (0, 0))
    return pl.pallas_call(
        body, name=name, grid=(t // tm,),
        in_specs=[pl.BlockSpec((tm, kp), lambda i: (i, 0))] * n
        + [pl.BlockSpec(w.shape, lambda i: (0, 0), pipeline_mode=pl.Buffered(1)), row, row, vec],
        out_specs=[row, row, vec, vec],
        out_shape=[jax.ShapeDtypeStruct((t, d), F32), jax.ShapeDtypeStruct((t, d), BF16),
                   jax.ShapeDtypeStruct((1, d), F32), jax.ShapeDtypeStruct((1, d), F32)],
        compiler_params=_params(("arbitrary",)),
    )(*parts, w, resid, z, g)


def _mm_tn(a, b, *, name, tm=1408, tn=1536, tk=1024):
    t, m = a.shape
    n = b.shape[1]
    tm = _tile(m, tm, LANES)
    tn = _tile(n, tn, LANES)
    tk = _tile(t, tk, SUBLANES)

    def body(a_ref, b_ref, o_ref):
        @pl.when(pl.program_id(2) == 0)
        def _():
            o_ref[...] = jnp.zeros_like(o_ref)

        o_ref[...] += lax.dot_general(a_ref[...], b_ref[...], (((0,), (0,)), ((), ())), preferred_element_type=F32)

    return pl.pallas_call(
        body, name=name, grid=(m // tm, n // tn, t // tk),
        in_specs=[pl.BlockSpec((tk, tm), lambda i, j, l: (l, i)), pl.BlockSpec((tk, tn), lambda i, j, l: (l, j))],
        out_specs=pl.BlockSpec((tm, tn), lambda i, j, l: (i, j)),
        out_shape=jax.ShapeDtypeStruct((m, n), F32),
        compiler_params=_params(("parallel", "parallel", "arbitrary")),
    )(a, b)


def _ln_bwd(dout, z, g, *, name, tm=512):
    t, d = z.shape
    tm = _tile(t, tm, SUBLANES)

    def body(do_ref, z_ref, g_ref, dz_ref, dzbf_ref, dg_ref, db_ref):
        @pl.when(pl.program_id(0) == 0)
        def _():
            dg_ref[...] = jnp.zeros_like(dg_ref)
            db_ref[...] = jnp.zeros_like(db_ref)

        dz, dg, db = _ln_bwd_math(do_ref[...], z_ref[...], g_ref[...])
        dz_ref[...] = dz
        dzbf_ref[...] = dz.astype(BF16)
        dg_ref[...] += dg
        db_ref[...] += db

    row = pl.BlockSpec((tm, d), lambda i: (i, 0))
    vec = pl.BlockSpec((1, d), lambda i: (0, 0))
    return pl.pallas_call(
        body, name=name, grid=(t // tm,), in_specs=[row, row, vec], out_specs=[row, row, vec, vec],
        out_shape=[jax.ShapeDtypeStruct((t, d), F32), jax.ShapeDtypeStruct((t, d), BF16),
                   jax.ShapeDtypeStruct((1, d), F32), jax.ShapeDtypeStruct((1, d), F32)],
        compiler_params=_params(("arbitrary",)),
    )(dout, z, g)


def _loss_head(z, g, b, target, *, name, tm=512):
    t, d = z.shape
    tm = _tile(t, tm, SUBLANES)

    def body(z_ref, g_ref, b_ref, t_ref, s_ref, dy_ref):
        @pl.when(pl.program_id(0) == 0)
        def _():
            s_ref[...] = jnp.zeros_like(s_ref)

        e = _ln(z_ref[...], g_ref[...], b_ref[...]) - t_ref[...]
        dy_ref[...] = e * (1.0 / d)
        s_ref[...] += jnp.sum(_colsum(e * e), axis=-1, keepdims=True)

    row = pl.BlockSpec((tm, d), lambda i: (i, 0))
    vec = pl.BlockSpec((1, d), lambda i: (0, 0))
    return pl.pallas_call(
        body, name=name, grid=(t // tm,), in_specs=[row, vec, vec, row],
        out_specs=[pl.BlockSpec((1, LANES), lambda i: (0, 0)), row],
        out_shape=[jax.ShapeDtypeStruct((1, LANES), F32), jax.ShapeDtypeStruct((t, d), F32)],
        compiler_params=_params(("arbitrary",)),
    )(z, g, b, target)


def _own(c, b, *_):
    return c, b


def _ahead(nc, bsz):
    def at(c, b, part):
        b2 = b + jnp.minimum(part, 1)
        return jnp.minimum(c + b2 // bsz, nc - 1), b2 % bsz
    return at


def _strip(s, tc, off, at=_own):
    def index(*ids):
        c, b = at(*ids)
        return b, 0, off + c
    return pl.BlockSpec((None, s, tc), index)


def _cvec(kw, tc, off, at=_own):
    def index(*ids):
        return 0, off + at(*ids)[0]
    return pl.BlockSpec((kw, tc), index)


def _acc(kw, tc):
    return pl.BlockSpec((kw, tc), lambda c, b, *_: (0, c))


def _sc_fwd(h, cw, cb, *, name, tc=256):
    bsz, s, d3 = h.shape
    d = d3 // 3
    tc = _tile(d, tc, LANES)
    nc = d // tc

    def body(gb_ref, gc_ref, v_ref, w_ref, b_ref, q_ref):
        u = _conv_fwd(gc_ref[...] * v_ref[...], w_ref[...], b_ref[...])
        q_ref[...] = (gb_ref[...] * u).astype(BF16)

    return pl.pallas_call(
        body, name=name, grid=(nc, bsz),
        in_specs=[_strip(s, tc, 0), _strip(s, tc, nc), _strip(s, tc, 2 * nc), _cvec(cw.shape[0], tc, 0), _cvec(1, tc, 0)],
        out_specs=_strip(s, tc, 0),
        out_shape=jax.ShapeDtypeStruct((bsz, s, d), BF16),
        compiler_params=_params(("parallel", "parallel")),
    )(h, h, h, cw, cb)


def _sc_bwd(h, dq, cw, cb, *, name, tc=256):
    bsz, s, d3 = h.shape
    d = d3 // 3
    kw = cw.shape[0]
    tc = _tile(d, tc, LANES)
    nc = d // tc

    def body(gb_ref, gc_ref, v_ref, dq_ref, w_ref, b_ref, dh_ref, dw_ref, db_ref, parts):
        b_id, part = pl.program_id(1), pl.program_id(2)

        @pl.when(part == 0)
        def _():
            gb, gc, v, dq_, w = gb_ref[...], gc_ref[...], v_ref[...], dq_ref[...], w_ref[...]
            p = gc * v
            u = _conv_fwd(p, w, b_ref[...])
            du = dq_ * gb
            dp, dw_rows = _conv_bwd(du, p, w)
            parts[0] = (dq_ * u).astype(BF16)
            parts[1] = (dp * v).astype(BF16)
            parts[2] = (dp * gc).astype(BF16)
            _accumulate(b_id == 0, [(dw_ref, dw_rows), (db_ref, _colsum(du))])

        dh_ref[...] = parts[part]

    at = _ahead(nc, bsz)
    return pl.pallas_call(
        body, name=name, grid=(nc, bsz, 3),
        in_specs=[_strip(s, tc, 0, at), _strip(s, tc, nc, at), _strip(s, tc, 2 * nc, at), _strip(s, tc, 0, at),
                  _cvec(kw, tc, 0, at), _cvec(1, tc, 0, at)],
        out_specs=[pl.BlockSpec((None, s, tc), lambda c, b, p: (b, 0, p * nc + c)), _acc(kw, tc), _acc(1, tc)],
        out_shape=[jax.ShapeDtypeStruct((bsz, s, d3), BF16), jax.ShapeDtypeStruct((kw, d), F32),
                   jax.ShapeDtypeStruct((1, d), F32)],
        scratch_shapes=[pltpu.VMEM((3, s, tc), BF16)],
        compiler_params=_params(("parallel", "arbitrary", "arbitrary")),
    )(h, h, h, dq, cw, cb)


def _ffn_specs(s, tc, nc, kw):
    strip = pl.BlockSpec((None, s, tc), lambda b, c: (b, 0, c))
    halves = [pl.BlockSpec((kw, tc), lambda b, c: (0, c)), pl.BlockSpec((kw, tc), lambda b, c: (0, nc + c)),
              pl.BlockSpec((1, tc), lambda b, c: (0, c)), pl.BlockSpec((1, tc), lambda b, c: (0, nc + c))]
    return strip, halves


def _ffn_fwd(x, w_up, cw, cb, *, name, tc=256):
    bsz, s, d = x.shape
    f = w_up.shape[0] // 2
    kw = cw.shape[0]
    tc = _tile(f, tc, LANES)
    nc = f // tc
    nt = (((1,), (1,)), ((), ()))

    def body(x_ref, w_ref, wg_ref, wv_ref, bg_ref, bv_ref, hg_ref, hv_ref, g_ref, v_ref, a_ref):
        c0 = pl.multiple_of(pl.program_id(1) * tc, LANES)
        xs = x_ref[...]
        hg = lax.dot_general(xs, w_ref[pl.ds(c0, tc), :], nt, preferred_element_type=F32)
        hv = lax.dot_general(xs, w_ref[pl.ds(f + c0, tc), :], nt, preferred_element_type=F32)
        hg_ref[...] = hg
        hv_ref[...] = hv
        g = _conv_fwd(hg, wg_ref[...], bg_ref[...])
        v = _conv_fwd(hv, wv_ref[...], bv_ref[...])
        g_ref[...] = g
        v_ref[...] = v
        a_ref[...] = (g * _sigmoid(g) * v).astype(BF16)

    strip, halves = _ffn_specs(s, tc, nc, kw)
    return pl.pallas_call(
        body, name=name, grid=(bsz, nc),
        in_specs=[pl.BlockSpec((None, s, d), lambda b, c: (b, 0, 0)),
                  pl.BlockSpec(w_up.shape, lambda b, c: (0, 0), pipeline_mode=pl.Buffered(1))] + halves,
        out_specs=[strip] * 5,
        out_shape=[jax.ShapeDtypeStruct((bsz, s, f), F32)] * 4 + [jax.ShapeDtypeStruct((bsz, s, f), BF16)],
        compiler_params=_params(("parallel", "arbitrary")),
    )(x, w_up, cw, cw, cb, cb)


def _ffn_bwd(hg, hv, g, v, dz, w_down, cw, *, name, tc=256):
    bsz, s, f = hg.shape
    d = dz.shape[2]
    kw = cw.shape[0]
    tc = _tile(f, tc, LANES)
    nc = f // tc

    def body(hg_ref, hv_ref, g_ref, v_ref, dz_ref, wd_ref, wg_ref, wv_ref,
             dhg_ref, dhv_ref, dwg_ref, dwv_ref, dbg_ref, dbv_ref):
        c0 = pl.multiple_of(pl.program_id(1) * tc, LANES)
        cols = pl.ds(c0, tc)
        da = lax.dot_general(dz_ref[...], wd_ref[cols, :], (((1,), (1,)), ((), ())), preferred_element_type=F32)
        g_ = g_ref[...]
        sg = _sigmoid(g_)
        dv = da * (g_ * sg)
        dg = da * v_ref[...] * (sg * (1.0 + g_ * (1.0 - sg)))
        dhg, dwg_rows = _conv_bwd(dg, hg_ref[...], wg_ref[...])
        dhv, dwv_rows = _conv_bwd(dv, hv_ref[...], wv_ref[...])
        dhg_ref[...] = dhg.astype(BF16)
        dhv_ref[...] = dhv.astype(BF16)
        _accumulate(pl.program_id(0) == 0, [(dwg_ref, dwg_rows), (dwv_ref, dwv_rows),
                                            (dbg_ref, [_colsum(dg)]), (dbv_ref, [_colsum(dv)])], cols)

    strip, halves = _ffn_specs(s, tc, nc, kw)
    whole = lambda r: pl.BlockSpec((r, f), lambda b, c: (0, 0))
    return pl.pallas_call(
        body, name=name, grid=(bsz, nc),
        in_specs=[strip] * 4 + [pl.BlockSpec((None, s, d), lambda b, c: (b, 0, 0)),
                                pl.BlockSpec(w_down.shape, lambda b, c: (0, 0), pipeline_mode=pl.Buffered(1))]
        + halves[:2],
        out_specs=[strip, strip, whole(kw), whole(kw), whole(1), whole(1)],
        out_shape=[jax.ShapeDtypeStruct((bsz, s, f), BF16), jax.ShapeDtypeStruct((bsz, s, f), BF16),
                   jax.ShapeDtypeStruct((kw, f), F32), jax.ShapeDtypeStruct((kw, f), F32),
                   jax.ShapeDtypeStruct((1, f), F32), jax.ShapeDtypeStruct((1, f), F32)],
        compiler_params=_params(("arbitrary", "arbitrary")),
    )(hg, hv, g, v, dz, w_down, cw, cw)


def _lru_gates(r, cw, cb, wg, bg, lam):
    blk = r.shape[1]
    xr = _conv_fwd(r, cw, cb)
    gates = jnp.dot(xr.astype(BF16), wg, preferred_element_type=F32) + bg
    rg = _sigmoid(gates[:, :blk])
    ig = _sigmoid(gates[:, blk:])
    sp = _softplus(-lam)
    la = (-LRU_C * sp) * rg
    a = jnp.exp(la)
    mult = jnp.sqrt(-_expm1(2.0 * la, a * a))
    return xr, rg, ig, sp, a, mult


def _lru_fwd(h, cw, cb, wg, bg, lam, *, name):
    bsz, s, r2 = h.shape
    heads, blk = wg.shape[0], wg.shape[1]
    kw = cw.shape[0]

    def body(g_ref, r_ref, cw_ref, cb_ref, wg_ref, bg_ref, lam_ref, y_ref, sv_ref):
        xr, rg, ig, _, a, mult = _lru_gates(r_ref[...], cw_ref[...], cb_ref[...], wg_ref[...], bg_ref[...], lam_ref[...])
        hs = _scan_fwd(a, mult * (ig * xr))
        for n, val in enumerate((hs, xr, rg, ig, a, mult)):
            sv_ref[n] = val
        y_ref[...] = (hs * _gelu(g_ref[...])).astype(BF16)

    per_head = lambda hd, b: (hd, 0, 0)
    return pl.pallas_call(
        body, name=name, grid=(heads, bsz),
        in_specs=[_strip(s, blk, 0), _strip(s, blk, heads), _cvec(kw, blk, 0), _cvec(1, blk, 0),
                  pl.BlockSpec((None, blk, 2 * blk), per_head), pl.BlockSpec((None, 1, 2 * blk), per_head),
                  _cvec(1, blk, 0)],
        out_specs=[_strip(s, blk, 0), pl.BlockSpec((6, None, s, blk), lambda hd, b: (0, b, 0, hd))],
        out_shape=[jax.ShapeDtypeStruct((bsz, s, r2 // 2), BF16), jax.ShapeDtypeStruct((6, bsz, s, r2 // 2), F32)],
        compiler_params=_params(("parallel", "parallel")),
    )(h, h, cw, cb, wg, bg, lam)


def _lru_bwd(h, sv, dy, cw, wg, lam, *, name):
    bsz, s, r2 = h.shape
    rw = r2 // 2
    heads, blk = wg.shape[0], wg.shape[1]
    kw = cw.shape[0]

    def body(g_ref, r_ref, cw_ref, wg_ref, lam_ref, sv_ref, dy_ref,
             dh_ref, dcw_ref, dcb_ref, dwg_ref, dbg_ref, dlam_ref, sg_ref, sr_ref, parts):
        b_id, part = pl.program_id(1), pl.program_id(2)

        @pl.when(part == 0)
        def _():
            r, cw_, wg_, lam_ = r_ref[...], cw_ref[...], wg_ref[...], lam_ref[...]
            hs_, xr, rg, ig, a, mult = (sv_ref[n] for n in range(6))
            sp = _softplus(-lam_)
            dy_ = dy_ref[...]
            gel, dgel = _gelu_and_grad(g_ref[...])
            dg = dy_ * hs_ * dgel
            lmb = _scan_rev(_shift_up(a, 1, 1.0), dy_ * gel)
            da = lmb * _shift_dn(hs_, 1)
            dmult = lmb * (ig * xr)
            dig = lmb * (mult * xr)
            dxr = lmb * (mult * ig)
            dla = da * a - dmult * (a * a / mult)
            drg = dla * (-LRU_C * sp)
            dsp = _colsum(dla * rg) * (-LRU_C)
            dlam = -dsp * _sigmoid(-lam_)
            dgates = jnp.concatenate([drg * (rg * (1.0 - rg)), dig * (ig * (1.0 - ig))], axis=1)
            dgates_bf = dgates.astype(BF16)
            dwg = lax.dot_general(xr.astype(BF16), dgates_bf, (((0,), (0,)), ((), ())), preferred_element_type=F32)
            dxr = dxr + lax.dot_general(dgates_bf, wg_, (((1,), (1,)), ((), ())), preferred_element_type=F32)
            dr, dcw_rows = _conv_bwd(dxr, r, cw_)
            parts[0] = dg.astype(BF16)
            parts[1] = dr.astype(BF16)
            _accumulate(b_id == 0, [(dcw_ref, dcw_rows), (dcb_ref, _colsum(dxr)), (dwg_ref, dwg),
                                    (dbg_ref, _colsum(dgates)), (dlam_ref, dlam), (sg_ref, _colsum(dg)),
                                    (sr_ref, _colsum(dr))])

        dh_ref[...] = parts[part]

    at = _ahead(heads, bsz)

    def saved(*ids):
        hd, b = at(*ids)
        return 0, b, 0, hd

    vec = pl.BlockSpec((1, blk), lambda hd, b, p: (0, hd))
    return pl.pallas_call(
        body, name=name, grid=(heads, bsz, 2),
        in_specs=[_strip(s, blk, 0, at), _strip(s, blk, heads, at), _cvec(kw, blk, 0, at),
                  pl.BlockSpec((None, blk, 2 * blk), lambda *ids: (at(*ids)[0], 0, 0)), _cvec(1, blk, 0, at),
                  pl.BlockSpec((6, None, s, blk), saved), _strip(s, blk, 0, at)],
        out_specs=[pl.BlockSpec((None, s, blk), lambda hd, b, p: (b, 0, p * heads + hd)),
                   pl.BlockSpec((kw, blk), lambda hd, b, p: (0, hd)), vec,
                   pl.BlockSpec((None, blk, 2 * blk), lambda hd, b, p: (hd, 0, 0)),
                   pl.BlockSpec((None, 1, 2 * blk), lambda hd, b, p: (hd, 0, 0)), vec, vec, vec],
        out_shape=[jax.ShapeDtypeStruct((bsz, s, r2), BF16), jax.ShapeDtypeStruct((kw, rw), F32),
                   jax.ShapeDtypeStruct((1, rw), F32), jax.ShapeDtypeStruct((heads, blk, 2 * blk), F32),
                   jax.ShapeDtypeStruct((heads, 1, 2 * blk), F32), jax.ShapeDtypeStruct((1, rw), F32),
                   jax.ShapeDtypeStruct((1, rw), F32), jax.ShapeDtypeStruct((1, rw), F32)],
        scratch_shapes=[pltpu.VMEM((2, s, blk), BF16)],
        compiler_params=_params(("parallel", "arbitrary", "arbitrary")),
    )(h, h, cw, wg, lam, sv, dy)


HBM_SPEC = pl.BlockSpec(memory_space=pltpu.HBM)
SEM_SPEC = pl.BlockSpec(memory_space=pltpu.SEMAPHORE)
EFFECT = pltpu.SideEffectType.DATAFLOW_SIDE_EFFECTING


def _peer_copies(srcs, lands, gather, send_sem, recv_sem):
    x, y, c = (lax.axis_index(ax) for ax in MESH_AXES)
    me = 4 * x + 2 * y + c
    copies = []
    for i in range(len(srcs)):
        for d in range(1, N_DEV):
            px = 1 - x if d & 4 else x
            py = 1 - y if d & 2 else y
            pc = 1 - c if d & 1 else c
            src = srcs[i] if gather[i] else srcs[i].at[4 * px + 2 * py + pc]
            k = i * (N_DEV - 1) + d - 1
            copies.append(pltpu.make_async_remote_copy(
                src_ref=src, dst_ref=lands[i].at[me], send_sem=send_sem.at[k], recv_sem=recv_sem.at[k],
                device_id=(px, py, pc), device_id_type=pl.DeviceIdType.MESH))
    return copies


def _exchange_start(arrs, gather, *, name):
    n = len(arrs)
    lands = [lax.empty((N_DEV,) + tuple(a.shape if g else a.shape[1:]), a.dtype) for a, g in zip(arrs, gather)]

    def body(*refs):
        srcs, land_refs = refs[:n], refs[n:2 * n]
        send_sem, recv_sem = refs[2 * n], refs[2 * n + 1]
        token = refs[-1]
        for cp in _peer_copies(srcs, land_refs, gather, send_sem, recv_sem):
            cp.start()
        token[...] = jnp.zeros_like(token)

    sems = pltpu.SemaphoreType.DMA((n * (N_DEV - 1),))
    thru = [pltpu.HBM(a.shape, a.dtype) for a in arrs + lands]
    out = pl.pallas_call(
        body, name=name, in_specs=[HBM_SPEC] * (2 * n),
        out_shape=(sems, sems, *thru, jax.ShapeDtypeStruct((SUBLANES, LANES), F32)),
        out_specs=(SEM_SPEC, SEM_SPEC, *([HBM_SPEC] * (2 * n)), pl.BlockSpec(memory_space=pltpu.VMEM)),
        input_output_aliases={i: 2 + i for i in range(2 * n)},
        compiler_params=pltpu.CompilerParams(has_side_effects=EFFECT),
    )(*[pltpu.with_memory_space_constraint(a, pltpu.HBM) for a in arrs + lands])
    return {"send_sem": out[0], "recv_sem": out[1], "srcs": list(out[2:2 + n]), "lands": list(out[2 + n:2 + 2 * n]),
            "token": out[-1], "gather": list(gather)}


def _exchange_wait(handle, after, *, name):
    srcs, lands, gather = handle["srcs"], handle["lands"], handle["gather"]
    n = len(srcs)

    def body(*refs):
        src_refs, land_refs = refs[:n], refs[n:2 * n]
        send_sem, recv_sem = refs[2 * n], refs[2 * n + 1]
        for cp in _peer_copies(src_refs, land_refs, gather, send_sem, recv_sem):
            cp.wait_send()
            cp.wait_recv()

    out = pl.pallas_call(
        body, name=name,
        in_specs=[HBM_SPEC] * (2 * n) + [SEM_SPEC, SEM_SPEC, pl.BlockSpec(memory_space=pl.ANY)],
        out_shape=tuple(pltpu.HBM(a.shape, a.dtype) for a in srcs + lands), out_specs=tuple([HBM_SPEC] * (2 * n)),
        input_output_aliases={i: i for i in range(2 * n)},
        compiler_params=pltpu.CompilerParams(has_side_effects=EFFECT),
    )(*srcs, *lands, handle["send_sem"], handle["recv_sem"], after)
    return list(out[:n]), list(out[n:])


def _layers_bf16(stacks, *, name):
    counts = [a.shape[0] for a in stacks]

    def body(*refs):
        outs = iter(refs[len(stacks):])
        for i_ref, n_layers in zip(refs, counts):
            for layer in range(n_layers):
                next(outs)[...] = i_ref[layer].astype(BF16)

    flat = pl.pallas_call(
        body, name=name,
        out_shape=[jax.ShapeDtypeStruct(a.shape[1:], BF16) for a in stacks for _ in range(a.shape[0])],
        compiler_params=pltpu.CompilerParams(vmem_limit_bytes=VMEM_LIMIT),
    )(*stacks)
    split, pos = [], 0
    for n_layers in counts:
        split.append(list(flat[pos:pos + n_layers]))
        pos += n_layers
    return split


def _adamw(parts, w, m, v, layer, so_far, *, name, tr=256):
    n_layers, r, c = w.shape
    tr = _tile(r, tr, SUBLANES)
    bc1 = 1.0 / (1.0 - ADAM_B1 ** ADAM_STEP)
    bc2 = 1.0 / (1.0 - ADAM_B2 ** ADAM_STEP)
    if so_far is None:
        so_far = [lax.empty(w.shape, F32) for _ in range(4)]

    def body(p_ref, w_ref, m_ref, v_ref, *rest):
        g_ref, d_ref, mo_ref, vo_ref = rest[4:]
        g = p_ref[0].astype(F32)
        for s in range(1, N_DEV):
            g = g + p_ref[s].astype(F32)
        m_new = ADAM_B1 * m_ref[...] + (1.0 - ADAM_B1) * g
        v_new = ADAM_B2 * v_ref[...] + (1.0 - ADAM_B2) * (g * g)
        g_ref[...] = g
        mo_ref[...] = m_new
        vo_ref[...] = v_new
        d_ref[...] = -ADAM_LR * ((m_new * bc1) / (jnp.sqrt(v_new * bc2) + ADAM_EPS) + ADAM_WD * w_ref[...])

    blk = pl.BlockSpec((None, tr, c), lambda i: (layer, i, 0))
    return pl.pallas_call(
        body, name=name, grid=(r // tr,),
        in_specs=[pl.BlockSpec((N_DEV, tr, c), lambda i: (0, i, 0)), blk, blk, blk]
        + [pl.BlockSpec(memory_space=pl.ANY)] * 4,
        out_specs=[blk] * 4, out_shape=[jax.ShapeDtypeStruct(w.shape, F32)] * 4,
        input_output_aliases={4 + o: o for o in range(4)},
        compiler_params=_params(("parallel",)),
    )(parts, w, m, v, *so_far)


def _whole(slabs, axis):
    x = jnp.moveaxis(slabs, 0, axis)
    shp = x.shape
    return x.reshape(shp[:axis] + (shp[axis] * shp[axis + 1],) + shp[axis + 2:])


def _slabs(whole, axis):
    shp = whole.shape
    x = whole.reshape(shp[:axis] + (N_DEV, shp[axis] // N_DEV) + shp[axis + 1:])
    return jnp.moveaxis(x, axis, 0)


def _pack(vecs, rows):
    flat = jnp.concatenate(vecs, axis=-1)
    pad = rows * LANES - flat.shape[-1]
    flat = jnp.pad(flat, [(0, 0)] * (flat.ndim - 1) + [(0, pad)])
    return flat.reshape(flat.shape[:-1] + (rows, LANES))


def _unpack(packed, sizes):
    flat = packed.reshape(packed.shape[:-2] + (-1,))
    out, pos = [], 0
    for n in sizes:
        out.append(flat[..., pos:pos + n])
        pos += n
    return out


def _pack_rows(sizes):
    total = sum(sizes)
    return -(-total // (LANES * SUBLANES)) * SUBLANES


BIG = {"sc_w_in": 2, "sc_w_out": 1, "lru_w_in": 2, "lru_w_gate": 3, "lru_w_out": 1, "ffn_w_up": 2, "ffn_w_down": 1}
SWAPPED = ("ffn_w_up", "lru_w_in")
TRANSPOSED = SWAPPED
SMALL = ["sc_conv_w", "lru_b_in", "lru_conv_w", "lru_conv_b", "lru_b_gate", "lru_lambda", "ffn_conv_w", "ln_g", "ln_b"]
REPL = ["sc_conv_b", "ffn_conv_b"]
WEIGHTS = ["sc_w_in", "sc_conv_w", "sc_conv_b", "sc_w_out", "lru_w_in", "lru_b_in", "lru_conv_w", "lru_conv_b",
           "lru_w_gate", "lru_b_gate", "lru_lambda", "lru_w_out", "ffn_w_up", "ffn_conv_w", "ffn_conv_b", "ffn_w_down",
           "ln_g", "ln_b"]


STAGES_PER_LAYER = 3


def _stage_big(g):
    i, part = divmod(g, STAGES_PER_LAYER)
    j = i // 2
    if part:
        return [("ffn_w_up" if part == 1 else "ffn_w_down", i)]
    return [("sc_w_in", j), ("sc_w_out", j)] if i % 2 == 0 else [("lru_w_in", j), ("lru_w_gate", j), ("lru_w_out", j)]


def _step(x, loss_target, w, m, v):
    bsz, s, d = x.shape
    t = bsz * s
    depth = w["ffn_w_up"].shape[0]
    alpha = (2.0 * depth) ** 0.25
    heads = w["lru_w_gate"].shape[1]

    small_sizes = [w[k].size for k in SMALL]
    small_rows = _pack_rows(small_sizes)
    small_local = _pack([w[k].reshape(1, -1) for k in SMALL], small_rows)[0]
    me = 4 * lax.axis_index("x") + 2 * lax.axis_index("y") + lax.axis_index("c")

    def with_own(land, own):
        return lax.dynamic_update_slice_in_dim(land, own, me, axis=0)

    stages = STAGES_PER_LAYER * depth
    def held(k, arr):
        return jnp.swapaxes(arr, -1, -2) if k in TRANSPOSED else arr

    def split_axis(k):
        return 0 if k in TRANSPOSED else BIG[k] - 1

    flat_names = [k for k in BIG if w[k].ndim == 3]
    wb = dict(zip(flat_names, _layers_bf16([held(k, w[k]) for k in flat_names], name="weights_bf16")))
    wb.update({k: list(w[k].astype(BF16)) for k in BIG if k not in flat_names})

    gathers, tok = [], None
    for g in range(stages):
        arrs = [wb[k][l] for k, l in _stage_big(g)]
        if g == 0:
            arrs.append(small_local)
        if tok is not None:
            arrs[0] = arrs[0] + tok.astype(BF16)
        gathers.append(_exchange_start(arrs, [True] * len(arrs), name=f"gather_start_{g}"))
        tok = gathers[-1]["token"][0, 0]
    full = {k: [None] * w[k].shape[0] for k in BIG}
    full["sc_conv_b"] = w["sc_conv_b"]
    full["ffn_conv_b"] = w["ffn_conv_b"]

    def arrive(g, after):
        srcs, lands = _exchange_wait(gathers[g], after, name=f"gather_wait_{g}")
        for (k, l), src, land in zip(_stage_big(g), srcs, lands):
            full[k][l] = _whole(with_own(land, src[None]), split_axis(k))
        if g == 0:
            for k, seg in zip(SMALL, _unpack(with_own(lands[-1], srcs[-1][None]), small_sizes)):
                full[k] = _whole(seg.reshape((N_DEV,) + w[k].shape), w[k].ndim - 1)

    stream, stream_ln = x.reshape(t, d), None
    xb = stream.astype(BF16)
    saved = []
    for i in range(depth):
        j = i // 2
        arrive(3 * i, gathers[-1]["token"] if i == 0 else xb)
        lng, lnb = full["ln_g"][i], full["ln_b"][i]
        sv = {"x0": xb}
        if i % 2 == 0:
            hm = _mm(xb, full["sc_w_in"][j], name="sc_in")
            q = _sc_fwd(hm.reshape(bsz, s, -1), full["sc_conv_w"][j], full["sc_conv_b"][j:j + 1], name="sc_mix")
            w_out = full["sc_w_out"][j]
        else:
            hm = _mm(xb, full["lru_w_in"][j], trans_w=True, bias=full["lru_b_in"][j:j + 1], name="lru_in")
            q, hs = _lru_fwd(hm.reshape(bsz, s, -1), full["lru_conv_w"][j], full["lru_conv_b"][j:j + 1],
                             full["lru_w_gate"][j], full["lru_b_gate"][j].reshape(heads, 1, -1),
                             full["lru_lambda"][j:j + 1], name="lru_mix")
            sv["hs"] = hs
            w_out = full["lru_w_out"][j]
        q = q.reshape(t, -1)
        arrive(3 * i + 1, q)
        z1, x1b = _mm_ln(q, w_out, stream, alpha, lng[0:1], lnb[0:1], resid_ln=stream_ln, name="mix_out_ln")
        hg, hv, gc, vc, a = _ffn_fwd(x1b.reshape(bsz, s, d), full["ffn_w_up"][i], full["ffn_conv_w"][i],
                                     full["ffn_conv_b"][i:i + 1], name="ffn_up_act")
        a = a.reshape(t, -1)
        arrive(3 * i + 2, a)
        z2, xb = _mm_ln(a, full["ffn_w_down"][i], z1, alpha, lng[1:2], lnb[1:2], resid_ln=(lng[0:1], lnb[0:1]),
                        name="ffn_down_ln")
        stream, stream_ln = z2, (lng[1:2], lnb[1:2])
        sv.update(hm=hm, q=q, z1=z1, x1=x1b, ffn=(hg, hv, gc, vc), a=a, z2=z2)
        saved.append(sv)

    sq, dx = _loss_head(stream, *stream_ln, loss_target.reshape(t, d), name="loss_head")
    loss = lax.psum((0.5 / d) * sq[0, 0], MESH_AXES)

    grads = {k: [None] * w[k].shape[0] for k in WEIGHTS}
    scatters = [None] * stages

    def as_updated(k, arr):
        return jnp.swapaxes(arr, -1, -2) if k in SWAPPED else arr

    def depart(g):
        send = [_slabs(grads[k][l], split_axis(k)).astype(BF16) for k, l in _stage_big(g)]
        send = [sl if k in TRANSPOSED else as_updated(k, sl) for (k, l), sl in zip(_stage_big(g), send)]
        scatters[g] = _exchange_start(send, [False] * len(send), name=f"scatter_start_{g}")
        return scatters[g]["token"][0:1, 0:1]

    dz2, dz2b, dg2, db2 = _ln_bwd(dx, saved[-1]["z2"], full["ln_g"][-1][1:2], name="ln_bwd")
    for i in reversed(range(depth)):
        j = i // 2
        sv = saved[i]
        lng = full["ln_g"][i]
        grads["ffn_w_down"][i] = _mm_tn(sv["a"], dz2b, name="ffn_down_dw")
        dhg, dhv, dwg, dwv, dbg, dbv = _ffn_bwd(*sv["ffn"], dz2b.reshape(bsz, s, d), full["ffn_w_down"][i],
                                                full["ffn_conv_w"][i] + depart(3 * i + 2), name="ffn_act_bwd")
        dhg, dhv = dhg.reshape(t, -1), dhv.reshape(t, -1)
        grads["ffn_conv_w"][i] = jnp.concatenate([dwg, dwv], axis=1)
        grads["ffn_conv_b"][i] = jnp.concatenate([dbg, dbv], axis=1)[0]
        grads["ffn_w_up"][i] = jnp.concatenate([_mm_tn(dhg, sv["x1"], name="ffn_up_dw_g"),
                                                _mm_tn(dhv, sv["x1"], name="ffn_up_dw_v")], axis=0)
        dz1, dz1b, dg1, db1 = _mm_ln_bwd([dhg, dhv], full["ffn_w_up"][i], dz2, alpha, sv["z1"],
                                         lng[0:1] + depart(3 * i + 1), name="ffn_up_dx_ln", w_rows_are_k=True)
        grads["ln_g"][i] = jnp.concatenate([dg1, dg2], axis=0)
        grads["ln_b"][i] = jnp.concatenate([db1, db2], axis=0)
        if i % 2 == 0:
            dq = _mm(dz1b, full["sc_w_out"][j], trans_w=True, name="sc_out_dx")
            grads["sc_w_out"][j] = _mm_tn(sv["q"], dz1b, name="sc_out_dw")
            dhm, dcw, dcb = _sc_bwd(sv["hm"].reshape(bsz, s, -1), dq.reshape(bsz, s, -1), full["sc_conv_w"][j],
                                    full["sc_conv_b"][j:j + 1], name="sc_mix_bwd")
            dhm = dhm.reshape(t, -1)
            grads["sc_conv_w"][j] = dcw
            grads["sc_conv_b"][j] = dcb[0]
            grads["sc_w_in"][j] = _mm_tn(sv["x0"], dhm, name="sc_in_dw")
            w_in = full["sc_w_in"][j]
        else:
            dq = _mm(dz1b, full["lru_w_out"][j], trans_w=True, name="lru_out_dx")
            grads["lru_w_out"][j] = _mm_tn(sv["q"], dz1b, name="lru_out_dw")
            dhm, dcw, dcb, dwgt, dbgt, dlam, sgb, srb = _lru_bwd(
                sv["hm"].reshape(bsz, s, -1), sv["hs"], dq.reshape(bsz, s, -1), full["lru_conv_w"][j],
                full["lru_w_gate"][j], full["lru_lambda"][j:j + 1], name="lru_mix_bwd")
            dhm = dhm.reshape(t, -1)
            grads["lru_conv_w"][j] = dcw
            grads["lru_conv_b"][j] = dcb[0]
            grads["lru_w_gate"][j] = dwgt
            grads["lru_b_gate"][j] = dbgt[:, 0, :]
            grads["lru_lambda"][j] = dlam[0]
            grads["lru_b_in"][j] = jnp.concatenate([sgb, srb], axis=1)[0]
            grads["lru_w_in"][j] = _mm_tn(dhm, sv["x0"], name="lru_in_dw")
            w_in = full["lru_w_in"][j]
        tok = depart(3 * i)
        if i > 0:
            dz2, dz2b, dg2, db2 = _mm_ln_bwd([dhm], w_in, dz1, alpha, saved[i - 1]["z2"], full["ln_g"][i - 1][1:2] + tok,
                                             name="mix_in_dx_ln", w_rows_are_k=i % 2 == 1)
        else:
            dx = _mm(dhm, w_in + tok[0, 0].astype(BF16), trans_w=True, resid=dz1, resid_scale=alpha, name="mix_in_dx")
    grad_x = dx.reshape(bsz, s, d)

    gsm = {k: jnp.stack(grads[k]) for k in SMALL + REPL}
    small_send = _pack([_slabs(gsm[k], gsm[k].ndim - 1).reshape(N_DEV, -1) for k in SMALL], small_rows)
    repl_sizes = [w[k].size for k in REPL]
    repl_rows = _pack_rows(repl_sizes)
    repl_send = _pack([gsm[k].reshape(1, -1) for k in REPL], repl_rows)[0]
    small_scatter = _exchange_start([small_send, repl_send], [False, True], name="scatter_start_small")

    out = {}

    def own_slab(src):
        return lax.dynamic_slice_in_dim(src, me, 1, axis=0)

    stacks = {k: None for k in BIG}
    after = dx
    for g in reversed(range(stages)):
        srcs, lands = _exchange_wait(scatters[g], after, name=f"scatter_wait_{g}")
        for (k, l), src, land in zip(_stage_big(g), srcs, lands):
            n_l, c2 = w[k].shape[0], land.shape[-1]
            wk, mk, vk = (as_updated(k, arr[k]).reshape(n_l, -1, c2) for arr in (w, m, v))
            stacks[k] = _adamw(with_own(land, own_slab(src)).reshape(N_DEV, -1, c2), wk, mk, vk, l, stacks[k],
                               name=f"adamw_{k}_{l}")
            after = stacks[k][-1]
    for k in BIG:
        shp = as_updated(k, w[k]).shape
        out[k] = [as_updated(k, r.reshape(shp)) for r in stacks[k]]
    srcs, lands = _exchange_wait(small_scatter, after, name="scatter_wait_small")
    got_small = with_own(lands[0], own_slab(srcs[0]))
    got_repl = with_own(lands[1], srcs[1][None])
    pk = lambda src, names, rows: _pack([src[k].reshape(1, -1) for k in names], rows)
    res = _adamw(got_small, small_local[None], pk(m, SMALL, small_rows), pk(v, SMALL, small_rows), 0, None,
                 name="adamw_small")
    for r_i, r in enumerate(res):
        for k, seg in zip(SMALL, _unpack(r[0], small_sizes)):
            out.setdefault(k, [None] * 4)[r_i] = seg.reshape(w[k].shape)
    res = _adamw(got_repl, pk(w, REPL, repl_rows), pk(m, REPL, repl_rows), pk(v, REPL, repl_rows), 0, None,
                 name="adamw_repl")
    for r_i, r in enumerate(res):
        for k, seg in zip(REPL, _unpack(r[0], repl_sizes)):
            out.setdefault(k, [None] * 4)[r_i] = seg.reshape(w[k].shape)

    return (loss, grad_x, *[out[k][0] for k in WEIGHTS], *[out[k][1] for k in WEIGHTS],
            *[out[k][2] for k in WEIGHTS], *[out[k][3] for k in WEIGHTS])


def kernel(x, sc_w_in, sc_conv_w, sc_conv_b, sc_w_out, lru_w_in, lru_b_in, lru_conv_w, lru_conv_b, lru_w_gate, lru_b_gate, lru_lambda, lru_w_out, ffn_w_up, ffn_conv_w, ffn_conv_b, ffn_w_down, ln_g, ln_b, loss_target, m_sc_w_in, m_sc_conv_w, m_sc_conv_b, m_sc_w_out, m_lru_w_in, m_lru_b_in, m_lru_conv_w, m_lru_conv_b, m_lru_w_gate, m_lru_b_gate, m_lru_lambda, m_lru_w_out, m_ffn_w_up, m_ffn_conv_w, m_ffn_conv_b, m_ffn_w_down, m_ln_g, m_ln_b, v_sc_w_in, v_sc_conv_w, v_sc_conv_b, v_sc_w_out, v_lru_w_in, v_lru_b_in, v_lru_conv_w, v_lru_conv_b, v_lru_w_gate, v_lru_b_gate, v_lru_lambda, v_lru_w_out, v_ffn_w_up, v_ffn_conv_w, v_ffn_conv_b, v_ffn_w_down, v_ln_g, v_ln_b):
    w = dict(sc_w_in=sc_w_in, sc_conv_w=sc_conv_w, sc_conv_b=sc_conv_b, sc_w_out=sc_w_out, lru_w_in=lru_w_in,
             lru_b_in=lru_b_in, lru_conv_w=lru_conv_w, lru_conv_b=lru_conv_b, lru_w_gate=lru_w_gate,
             lru_b_gate=lru_b_gate, lru_lambda=lru_lambda, lru_w_out=lru_w_out, ffn_w_up=ffn_w_up,
             ffn_conv_w=ffn_conv_w, ffn_conv_b=ffn_conv_b, ffn_w_down=ffn_w_down, ln_g=ln_g, ln_b=ln_b)
    m = dict(sc_w_in=m_sc_w_in, sc_conv_w=m_sc_conv_w, sc_conv_b=m_sc_conv_b, sc_w_out=m_sc_w_out, lru_w_in=m_lru_w_in,
             lru_b_in=m_lru_b_in, lru_conv_w=m_lru_conv_w, lru_conv_b=m_lru_conv_b, lru_w_gate=m_lru_w_gate,
             lru_b_gate=m_lru_b_gate, lru_lambda=m_lru_lambda, lru_w_out=m_lru_w_out, ffn_w_up=m_ffn_w_up,
             ffn_conv_w=m_ffn_conv_w, ffn_conv_b=m_ffn_conv_b, ffn_w_down=m_ffn_w_down, ln_g=m_ln_g, ln_b=m_ln_b)
    v = dict(sc_w_in=v_sc_w_in, sc_conv_w=v_sc_conv_w, sc_conv_b=v_sc_conv_b, sc_w_out=v_sc_w_out, lru_w_in=v_lru_w_in,
             lru_b_in=v_lru_b_in, lru_conv_w=v_lru_conv_w, lru_conv_b=v_lru_conv_b, lru_w_gate=v_lru_w_gate,
             lru_b_gate=v_lru_b_gate, lru_lambda=v_lru_lambda, lru_w_out=v_lru_w_out, ffn_w_up=v_ffn_w_up,
             ffn_conv_w=v_ffn_conv_w, ffn_conv_b=v_ffn_conv_b, ffn_w_down=v_ffn_w_down, ln_g=v_ln_g, ln_b=v_ln_b)
    return _step(x, loss_target, w, m, v)
```

```python
import functools
import math

import jax
import jax.numpy as jnp
from jax import lax
from jax.experimental import pallas as pl
from jax.experimental.pallas import tpu as pltpu

F32 = jnp.float32
BF16 = jnp.bfloat16

N_DEV = 8
MESH_AXES = ("x", "y", "c")
LANES = 128
SUBLANES = 8
VMEM_LIMIT = 56 * 1024 * 1024
MM_LHS_ELEMS = 3 * 1024 * 1024
MM_TN = 1536

LRU_C = 8.0
LN_EPS = 1e-5
ADAM_LR = 0.001
ADAM_B1 = 0.9
ADAM_B2 = 0.999
ADAM_EPS = 1e-08
ADAM_WD = 0.01
ADAM_STEP = 10
GELU_K = math.sqrt(2.0 / math.pi)
GELU_C = 0.044715


def _tile(n, target, align):
    if n <= target:
        return n
    t = (target // align) * align
    while t >= align:
        if n % t == 0:
            return t
        t -= align
    return n


def _params(sem):
    return pltpu.CompilerParams(dimension_semantics=sem, vmem_limit_bytes=VMEM_LIMIT)


def _rows(x):
    return lax.broadcasted_iota(jnp.int32, x.shape, 0)


def _shift_dn(x, k, fill=0.0):
    if k == 0:
        return x
    return jnp.where(_rows(x) >= k, pltpu.roll(x, k, 0), fill)


def _shift_up(x, k, fill=0.0):
    if k == 0:
        return x
    s = x.shape[0]
    return jnp.where(_rows(x) < s - k, pltpu.roll(x, s - k, 0), fill)


def _conv_fwd(x, w, b):
    kw = w.shape[0]
    y = _shift_dn(x, kw - 1) * w[0:1, :] + b
    for k in range(1, kw):
        y = y + _shift_dn(x, kw - 1 - k) * w[k:k + 1, :]
    return y


def _conv_bwd(dy, x, w):
    kw = w.shape[0]
    ahead = [_shift_up(dy, j) for j in range(kw)]
    dx = ahead[kw - 1] * w[0:1, :]
    for k in range(1, kw):
        dx = dx + ahead[kw - 1 - k] * w[k:k + 1, :]
    return dx, [_colsum(ahead[kw - 1 - k] * x) for k in range(kw)]


def _accumulate(first, items, cols=slice(None)):
    flat = []
    for ref, val in items:
        if isinstance(val, list):
            flat += [(ref, (slice(k, k + 1), cols), row) for k, row in enumerate(val)]
        else:
            flat.append((ref, Ellipsis, val))

    @pl.when(first)
    def _():
        for ref, idx, val in flat:
            ref[idx] = val

    @pl.when(jnp.logical_not(first))
    def _():
        for ref, idx, val in flat:
            ref[idx] += val


def _colsum(x):
    return jnp.sum(x, axis=0, keepdims=True)


def _sigmoid(x):
    return 1.0 / (1.0 + jnp.exp(-x))


def _log1p(x):
    u = 1.0 + x
    return jnp.where(u == 1.0, x, jnp.log(u) * (x / (u - 1.0)))


def _softplus(x):
    return jnp.maximum(x, 0.0) + _log1p(jnp.exp(-jnp.abs(x)))


def _expm1(x, ex):
    poly = x * (1.0 + x * (0.5 + x * (1.0 / 6.0 + x * (1.0 / 24.0 + x * (1.0 / 120.0 + x * (1.0 / 720.0))))))
    return jnp.where(jnp.abs(x) < 0.25, poly, ex - 1.0)


def _gelu(x):
    t = jnp.tanh(GELU_K * (x + GELU_C * x * x * x))
    return 0.5 * x * (1.0 + t)


def _gelu_and_grad(x):
    x2 = x * x
    t = jnp.tanh(GELU_K * (x + GELU_C * x * x2))
    g = 0.5 * x * (1.0 + t)
    dg = 0.5 * (1.0 + t) + 0.5 * x * (1.0 - t * t) * (GELU_K * (1.0 + 3.0 * GELU_C * x2))
    return g, dg


def _scan_fwd(a, b):
    s = a.shape[0]
    k = 1
    while k < s:
        last = 2 * k >= s
        if k % SUBLANES:
            b = a * _shift_dn(b, k) + b
            if not last:
                a = a * _shift_dn(a, k, 1.0)
        else:
            b = jnp.concatenate([b[:k], a[k:] * b[:s - k] + b[k:]], axis=0)
            if not last:
                a = jnp.concatenate([a[:k], a[k:] * a[:s - k]], axis=0)
        k *= 2
    return b


def _scan_rev(c, v):
    s = c.shape[0]
    k = 1
    while k < s:
        last = 2 * k >= s
        if k % SUBLANES:
            v = c * _shift_up(v, k) + v
            if not last:
                c = c * _shift_up(c, k, 1.0)
        else:
            v = jnp.concatenate([c[:s - k] * v[k:] + v[:s - k], v[s - k:]], axis=0)
            if not last:
                c = jnp.concatenate([c[:s - k] * c[k:], c[s - k:]], axis=0)
        k *= 2
    return v


def _mm(a, w, *, name, trans_w=False, bias=None, resid=None, resid_scale=1.0):
    m, k = a.shape
    n = w.shape[0] if trans_w else w.shape[1]
    tm = _tile(m, min(1024, max(256, MM_LHS_ELEMS // k)), SUBLANES)
    tn = _tile(n, MM_TN, LANES)
    has_bias = bias is not None
    has_resid = resid is not None

    def body(*refs):
        a_ref, w_ref = refs[0], refs[1]
        pos = 2
        b_ref = r_ref = None
        if has_bias:
            b_ref = refs[pos]
            pos += 1
        if has_resid:
            r_ref = refs[pos]
            pos += 1
        o_ref = refs[pos]

        cols = pl.ds(pl.multiple_of(pl.program_id(1) * tn, LANES), tn)
        if trans_w:
            acc = lax.dot_general(a_ref[...], w_ref[cols, :], (((1,), (1,)), ((), ())), preferred_element_type=F32)
        else:
            acc = jnp.dot(a_ref[...], w_ref[:, cols], preferred_element_type=F32)
        if has_bias:
            acc = acc + b_ref[...]
        if has_resid:
            acc = acc + resid_scale * r_ref[...]
        o_ref[...] = acc

    in_specs = [pl.BlockSpec((tm, k), lambda i, j: (i, 0)),
                pl.BlockSpec(w.shape, lambda i, j: (0, 0), pipeline_mode=pl.Buffered(1))]
    args = [a, w]
    if has_bias:
        in_specs.append(pl.BlockSpec((1, tn), lambda i, j: (0, j)))
        args.append(bias)
    if has_resid:
        in_specs.append(pl.BlockSpec((tm, tn), lambda i, j: (i, j)))
        args.append(resid)
    return pl.pallas_call(
        body, name=name, grid=(m // tm, n // tn), in_specs=in_specs,
        out_specs=pl.BlockSpec((tm, tn), lambda i, j: (i, j)),
        out_shape=jax.ShapeDtypeStruct((m, n), F32),
        compiler_params=_params(("parallel", "arbitrary")),
    )(*args)


def _ln(z, g, b):
    mu = jnp.mean(z, axis=-1, keepdims=True)
    zc = z - mu
    var = jnp.mean(zc * zc, axis=-1, keepdims=True)
    return zc * lax.rsqrt(var + LN_EPS) * g + b


def _mm_ln(a, w, resid, alpha, g, b, *, name, resid_ln=None, tm=512):
    m, k = a.shape
    d = w.shape[1]
    tm = _tile(m, tm, SUBLANES)
    n_extra = 0 if resid_ln is None else 2

    def body(a_ref, w_ref, r_ref, g_ref, b_ref, *rest):
        z_ref, obf_ref = rest[n_extra:]
        x = r_ref[...]
        if resid_ln is not None:
            x = _ln(x, rest[0][...], rest[1][...])
        z = alpha * x + jnp.dot(a_ref[...], w_ref[...], preferred_element_type=F32)
        z_ref[...] = z
        obf_ref[...] = _ln(z, g_ref[...], b_ref[...]).astype(BF16)

    row = pl.BlockSpec((tm, d), lambda i: (i, 0))
    vec = pl.BlockSpec((1, d), lambda i: (0, 0))
    return pl.pallas_call(
        body, name=name, grid=(m // tm,),
        in_specs=[pl.BlockSpec((tm, k), lambda i: (i, 0)),
                  pl.BlockSpec((k, d), lambda i: (0, 0), pipeline_mode=pl.Buffered(1)), row, vec, vec]
        + [vec] * n_extra,
        out_specs=[row, row],
        out_shape=[jax.ShapeDtypeStruct((m, d), F32), jax.ShapeDtypeStruct((m, d), BF16)],
        compiler_params=_params(("parallel",)),
    )(a, w, resid, g, b, *(resid_ln or ()))


def _ln_bwd_math(do, z, g):
    mu = jnp.mean(z, axis=-1, keepdims=True)
    zc = z - mu
    var = jnp.mean(zc * zc, axis=-1, keepdims=True)
    rstd = lax.rsqrt(var + LN_EPS)
    xhat = zc * rstd
    dxh = do * g
    m1 = jnp.mean(dxh, axis=-1, keepdims=True)
    m2 = jnp.mean(dxh * xhat, axis=-1, keepdims=True)
    return rstd * (dxh - m1 - xhat * m2), _colsum(do * xhat), _colsum(do)


def _mm_ln_bwd(parts, w, resid, resid_scale, z, g, *, name, w_rows_are_k=False):
    t, kp = parts[0].shape
    k, d = w.shape if w_rows_are_k else w.shape[::-1]
    n = len(parts)
    tm = _tile(t, min(512, max(256, MM_LHS_ELEMS // k)), SUBLANES)

    def body(*refs):
        a_refs = refs[:n]
        w_ref, r_ref, z_ref, g_ref, dz_ref, dzbf_ref, dg_ref, db_ref = refs[n:]

        @pl.when(pl.program_id(0) == 0)
        def _():
            dg_ref[...] = jnp.zeros_like(dg_ref)
            db_ref[...] = jnp.zeros_like(db_ref)

        dx = resid_scale * r_ref[...]
        for p, a_ref in enumerate(a_refs):
            if w_rows_are_k:
                dx = dx + jnp.dot(a_ref[...], w_ref[p * kp:(p + 1) * kp, :], preferred_element_type=F32)
            else:
                dx = dx + lax.dot_general(a_ref[...], w_ref[:, p * kp:(p + 1) * kp], (((1,), (1,)), ((), ())),
                                          preferred_element_type=F32)
        dz, dg, db = _ln_bwd_math(dx, z_ref[...], g_ref[...])
        dz_ref[...] = dz
        dzbf_ref[...] = dz.astype(BF16)
        dg_ref[...] += dg
        db_ref[...] += db

    row = pl.BlockSpec((tm, d), lambda i: (i, 0))
    vec = pl.BlockSpec((1, d), lambda i: (0, 0))
    return pl.pallas_call(
        body, name=name, grid=(t // tm,),
        in_specs=[pl.BlockSpec((tm, kp), lambda i: (i, 0))] * n
        + [pl.BlockSpec(w.shape, lambda i: (0, 0), pipeline_mode=pl.Buffered(1)), row, row, vec],
        out_specs=[row, row, vec, vec],
        out_shape=[jax.ShapeDtypeStruct((t, d), F32), jax.ShapeDtypeStruct((t, d), BF16),
                   jax.ShapeDtypeStruct((1, d), F32), jax.ShapeDtypeStruct((1, d), F32)],
        compiler_params=_params(("arbitrary",)),
    )(*parts, w, resid, z, g)


def _mm_tn(a, b, *, name, tm=1408, tn=1536, tk=1024):
    t, m = a.shape
    n = b.shape[1]
    tm = _tile(m, tm, LANES)
    tn = _tile(n, tn, LANES)
    tk = _tile(t, tk, SUBLANES)
    last = t // tk - 1

    def body(a_ref, b_ref, o_ref, acc):
        @pl.when(pl.program_id(2) == 0)
        def _():
            acc[...] = jnp.zeros_like(acc)

        acc[...] += lax.dot_general(a_ref[...], b_ref[...], (((0,), (0,)), ((), ())), preferred_element_type=F32)

        @pl.when(pl.program_id(2) == last)
        def _():
            o_ref[...] = acc[...].astype(BF16)

    return pl.pallas_call(
        body, name=name, grid=(m // tm, n // tn, t // tk),
        in_specs=[pl.BlockSpec((tk, tm), lambda i, j, l: (l, i)), pl.BlockSpec((tk, tn), lambda i, j, l: (l, j))],
        out_specs=pl.BlockSpec((tm, tn), lambda i, j, l: (i, j)),
        out_shape=jax.ShapeDtypeStruct((m, n), BF16),
        scratch_shapes=[pltpu.VMEM((tm, tn), F32)],
        compiler_params=_params(("parallel", "parallel", "arbitrary")),
    )(a, b)


def _ln_bwd(dout, z, g, *, name, tm=512):
    t, d = z.shape
    tm = _tile(t, tm, SUBLANES)

    def body(do_ref, z_ref, g_ref, dz_ref, dzbf_ref, dg_ref, db_ref):
        @pl.when(pl.program_id(0) == 0)
        def _():
            dg_ref[...] = jnp.zeros_like(dg_ref)
            db_ref[...] = jnp.zeros_like(db_ref)

        dz, dg, db = _ln_bwd_math(do_ref[...], z_ref[...], g_ref[...])
        dz_ref[...] = dz
        dzbf_ref[...] = dz.astype(BF16)
        dg_ref[...] += dg
        db_ref[...] += db

    row = pl.BlockSpec((tm, d), lambda i: (i, 0))
    vec = pl.BlockSpec((1, d), lambda i: (0, 0))
    return pl.pallas_call(
        body, name=name, grid=(t // tm,), in_specs=[row, row, vec], out_specs=[row, row, vec, vec],
        out_shape=[jax.ShapeDtypeStruct((t, d), F32), jax.ShapeDtypeStruct((t, d), BF16),
                   jax.ShapeDtypeStruct((1, d), F32), jax.ShapeDtypeStruct((1, d), F32)],
        compiler_params=_params(("arbitrary",)),
    )(dout, z, g)


def _loss_head(z, g, b, target, *, name, tm=512):
    t, d = z.shape
    tm = _tile(t, tm, SUBLANES)

    def body(z_ref, g_ref, b_ref, t_ref, s_ref, dy_ref):
        @pl.when(pl.program_id(0) == 0)
        def _():
            s_ref[...] = jnp.zeros_like(s_ref)

        e = _ln(z_ref[...], g_ref[...], b_ref[...]) - t_ref[...]
        dy_ref[...] = e * (1.0 / d)
        s_ref[...] += jnp.sum(_colsum(e * e), axis=-1, keepdims=True)

    row = pl.BlockSpec((tm, d), lambda i: (i, 0))
    vec = pl.BlockSpec((1, d), lambda i: (0, 0))
    return pl.pallas_call(
        body, name=name, grid=(t // tm,), in_specs=[row, vec, vec, row],
        out_specs=[pl.BlockSpec((1, LANES), lambda i: (0, 0)), row],
        out_shape=[jax.ShapeDtypeStruct((1, LANES), F32), jax.ShapeDtypeStruct((t, d), F32)],
        compiler_params=_params(("arbitrary",)),
    )(z, g, b, target)


def _own(c, b, *_):
    return c, b


def _ahead(nc, bsz):
    def at(c, b, part):
        b2 = b + jnp.minimum(part, 1)
        return jnp.minimum(c + b2 // bsz, nc - 1), b2 % bsz
    return at


def _strip(s, tc, off, at=_own):
    def index(*ids):
        c, b = at(*ids)
        return b, 0, off + c
    return pl.BlockSpec((None, s, tc), index)


def _cvec(kw, tc, off, at=_own):
    def index(*ids):
        return 0, off + at(*ids)[0]
    return pl.BlockSpec((kw, tc), index)


def _acc(kw, tc):
    return pl.BlockSpec((kw, tc), lambda c, b, *_: (0, c))


def _sc_fwd(h, cw, cb, *, name, tc=256):
    bsz, s, d3 = h.shape
    d = d3 // 3
    tc = _tile(d, tc, LANES)
    nc = d // tc

    def body(gb_ref, gc_ref, v_ref, w_ref, b_ref, q_ref):
        u = _conv_fwd(gc_ref[...] * v_ref[...], w_ref[...], b_ref[...])
        q_ref[...] = (gb_ref[...] * u).astype(BF16)

    return pl.pallas_call(
        body, name=name, grid=(nc, bsz),
        in_specs=[_strip(s, tc, 0), _strip(s, tc, nc), _strip(s, tc, 2 * nc), _cvec(cw.shape[0], tc, 0), _cvec(1, tc, 0)],
        out_specs=_strip(s, tc, 0),
        out_shape=jax.ShapeDtypeStruct((bsz, s, d), BF16),
        compiler_params=_params(("parallel", "parallel")),
    )(h, h, h, cw, cb)


def _sc_bwd(h, dq, cw, cb, *, name, tc=256):
    bsz, s, d3 = h.shape
    d = d3 // 3
    kw = cw.shape[0]
    tc = _tile(d, tc, LANES)
    nc = d // tc

    def body(gb_ref, gc_ref, v_ref, dq_ref, w_ref, b_ref, dh_ref, dw_ref, db_ref, parts):
        b_id, part = pl.program_id(1), pl.program_id(2)

        @pl.when(part == 0)
        def _():
            gb, gc, v, dq_, w = gb_ref[...], gc_ref[...], v_ref[...], dq_ref[...], w_ref[...]
            p = gc * v
            u = _conv_fwd(p, w, b_ref[...])
            du = dq_ * gb
            dp, dw_rows = _conv_bwd(du, p, w)
            parts[0] = (dq_ * u).astype(BF16)
            parts[1] = (dp * v).astype(BF16)
            parts[2] = (dp * gc).astype(BF16)
            _accumulate(b_id == 0, [(dw_ref, dw_rows), (db_ref, _colsum(du))])

        dh_ref[...] = parts[part]

    at = _ahead(nc, bsz)
    return pl.pallas_call(
        body, name=name, grid=(nc, bsz, 3),
        in_specs=[_strip(s, tc, 0, at), _strip(s, tc, nc, at), _strip(s, tc, 2 * nc, at), _strip(s, tc, 0, at),
                  _cvec(kw, tc, 0, at), _cvec(1, tc, 0, at)],
        out_specs=[pl.BlockSpec((None, s, tc), lambda c, b, p: (b, 0, p * nc + c)), _acc(kw, tc), _acc(1, tc)],
        out_shape=[jax.ShapeDtypeStruct((bsz, s, d3), BF16), jax.ShapeDtypeStruct((kw, d), F32),
                   jax.ShapeDtypeStruct((1, d), F32)],
        scratch_shapes=[pltpu.VMEM((3, s, tc), BF16)],
        compiler_params=_params(("parallel", "arbitrary", "arbitrary")),
    )(h, h, h, dq, cw, cb)


def _ffn_specs(s, tc, nc, kw):
    strip = pl.BlockSpec((None, s, tc), lambda b, c: (b, 0, c))
    halves = [pl.BlockSpec((kw, tc), lambda b, c: (0, c)), pl.BlockSpec((kw, tc), lambda b, c: (0, nc + c)),
              pl.BlockSpec((1, tc), lambda b, c: (0, c)), pl.BlockSpec((1, tc), lambda b, c: (0, nc + c))]
    return strip, halves


def _ffn_fwd(x, w_up, cw, cb, *, name, tc=256):
    bsz, s, d = x.shape
    f = w_up.shape[0] // 2
    kw = cw.shape[0]
    tc = _tile(f, tc, LANES)
    nc = f // tc
    nt = (((1,), (1,)), ((), ()))

    def body(x_ref, w_ref, wg_ref, wv_ref, bg_ref, bv_ref, hg_ref, hv_ref, g_ref, v_ref, a_ref):
        c0 = pl.multiple_of(pl.program_id(1) * tc, LANES)
        xs = x_ref[...]
        hg = lax.dot_general(xs, w_ref[pl.ds(c0, tc), :], nt, preferred_element_type=F32)
        hv = lax.dot_general(xs, w_ref[pl.ds(f + c0, tc), :], nt, preferred_element_type=F32)
        hg_ref[...] = hg
        hv_ref[...] = hv
        g = _conv_fwd(hg, wg_ref[...], bg_ref[...])
        v = _conv_fwd(hv, wv_ref[...], bv_ref[...])
        g_ref[...] = g
        v_ref[...] = v
        a_ref[...] = (g * _sigmoid(g) * v).astype(BF16)

    strip, halves = _ffn_specs(s, tc, nc, kw)
    return pl.pallas_call(
        body, name=name, grid=(bsz, nc),
        in_specs=[pl.BlockSpec((None, s, d), lambda b, c: (b, 0, 0)),
                  pl.BlockSpec(w_up.shape, lambda b, c: (0, 0), pipeline_mode=pl.Buffered(1))] + halves,
        out_specs=[strip] * 5,
        out_shape=[jax.ShapeDtypeStruct((bsz, s, f), F32)] * 4 + [jax.ShapeDtypeStruct((bsz, s, f), BF16)],
        compiler_params=_params(("parallel", "arbitrary")),
    )(x, w_up, cw, cw, cb, cb)


def _ffn_bwd(hg, hv, g, v, dz, w_down, cw, *, name, tc=256):
    bsz, s, f = hg.shape
    d = dz.shape[2]
    kw = cw.shape[0]
    tc = _tile(f, tc, LANES)
    nc = f // tc

    def body(hg_ref, hv_ref, g_ref, v_ref, dz_ref, wd_ref, wg_ref, wv_ref,
             dhg_ref, dhv_ref, dwg_ref, dwv_ref, dbg_ref, dbv_ref):
        c0 = pl.multiple_of(pl.program_id(1) * tc, LANES)
        cols = pl.ds(c0, tc)
        da = lax.dot_general(dz_ref[...], wd_ref[cols, :], (((1,), (1,)), ((), ())), preferred_element_type=F32)
        g_ = g_ref[...]
        sg = _sigmoid(g_)
        dv = da * (g_ * sg)
        dg = da * v_ref[...] * (sg * (1.0 + g_ * (1.0 - sg)))
        dhg, dwg_rows = _conv_bwd(dg, hg_ref[...], wg_ref[...])
        dhv, dwv_rows = _conv_bwd(dv, hv_ref[...], wv_ref[...])
        dhg_ref[...] = dhg.astype(BF16)
        dhv_ref[...] = dhv.astype(BF16)
        _accumulate(pl.program_id(0) == 0, [(dwg_ref, dwg_rows), (dwv_ref, dwv_rows),
                                            (dbg_ref, [_colsum(dg)]), (dbv_ref, [_colsum(dv)])], cols)

    strip, halves = _ffn_specs(s, tc, nc, kw)
    whole = lambda r: pl.BlockSpec((r, f), lambda b, c: (0, 0))
    return pl.pallas_call(
        body, name=name, grid=(bsz, nc),
        in_specs=[strip] * 4 + [pl.BlockSpec((None, s, d), lambda b, c: (b, 0, 0)),
                                pl.BlockSpec(w_down.shape, lambda b, c: (0, 0), pipeline_mode=pl.Buffered(1))]
        + halves[:2],
        out_specs=[strip, strip, whole(kw), whole(kw), whole(1), whole(1)],
        out_shape=[jax.ShapeDtypeStruct((bsz, s, f), BF16), jax.ShapeDtypeStruct((bsz, s, f), BF16),
                   jax.ShapeDtypeStruct((kw, f), F32), jax.ShapeDtypeStruct((kw, f), F32),
                   jax.ShapeDtypeStruct((1, f), F32), jax.ShapeDtypeStruct((1, f), F32)],
        compiler_params=_params(("arbitrary", "arbitrary")),
    )(hg, hv, g, v, dz, w_down, cw, cw)


def _lru_gates(r, cw, cb, wg, bg, lam):
    blk = r.shape[1]
    xr = _conv_fwd(r, cw, cb)
    gates = jnp.dot(xr.astype(BF16), wg, preferred_element_type=F32) + bg
    rg = _sigmoid(gates[:, :blk])
    ig = _sigmoid(gates[:, blk:])
    sp = _softplus(-lam)
    la = (-LRU_C * sp) * rg
    a = jnp.exp(la)
    mult = jnp.sqrt(-_expm1(2.0 * la, a * a))
    return xr, rg, ig, sp, a, mult


def _lru_fwd(h, cw, cb, wg, bg, lam, *, name):
    bsz, s, r2 = h.shape
    heads, blk = wg.shape[0], wg.shape[1]
    kw = cw.shape[0]

    def body(g_ref, r_ref, cw_ref, cb_ref, wg_ref, bg_ref, lam_ref, y_ref, sv_ref):
        xr, rg, ig, _, a, mult = _lru_gates(r_ref[...], cw_ref[...], cb_ref[...], wg_ref[...], bg_ref[...], lam_ref[...])
        hs = _scan_fwd(a, mult * (ig * xr))
        for n, val in enumerate((hs, xr, rg, ig, a, mult)):
            sv_ref[n] = val
        y_ref[...] = (hs * _gelu(g_ref[...])).astype(BF16)

    per_head = lambda hd, b: (hd, 0, 0)
    return pl.pallas_call(
        body, name=name, grid=(heads, bsz),
        in_specs=[_strip(s, blk, 0), _strip(s, blk, heads), _cvec(kw, blk, 0), _cvec(1, blk, 0),
                  pl.BlockSpec((None, blk, 2 * blk), per_head), pl.BlockSpec((None, 1, 2 * blk), per_head),
                  _cvec(1, blk, 0)],
        out_specs=[_strip(s, blk, 0), pl.BlockSpec((6, None, s, blk), lambda hd, b: (0, b, 0, hd))],
        out_shape=[jax.ShapeDtypeStruct((bsz, s, r2 // 2), BF16), jax.ShapeDtypeStruct((6, bsz, s, r2 // 2), F32)],
        compiler_params=_params(("parallel", "parallel")),
    )(h, h, cw, cb, wg, bg, lam)


def _lru_bwd(h, sv, dy, cw, wg, lam, *, name):
    bsz, s, r2 = h.shape
    rw = r2 // 2
    heads, blk = wg.shape[0], wg.shape[1]
    kw = cw.shape[0]

    def body(g_ref, r_ref, cw_ref, wg_ref, lam_ref, sv_ref, dy_ref,
             dh_ref, dcw_ref, dcb_ref, dwg_ref, dbg_ref, dlam_ref, sg_ref, sr_ref, parts):
        b_id, part = pl.program_id(1), pl.program_id(2)

        @pl.when(part == 0)
        def _():
            r, cw_, wg_, lam_ = r_ref[...], cw_ref[...], wg_ref[...], lam_ref[...]
            hs_, xr, rg, ig, a, mult = (sv_ref[n] for n in range(6))
            sp = _softplus(-lam_)
            dy_ = dy_ref[...]
            gel, dgel = _gelu_and_grad(g_ref[...])
            dg = dy_ * hs_ * dgel
            lmb = _scan_rev(_shift_up(a, 1, 1.0), dy_ * gel)
            da = lmb * _shift_dn(hs_, 1)
            dmult = lmb * (ig * xr)
            dig = lmb * (mult * xr)
            dxr = lmb * (mult * ig)
            dla = da * a - dmult * (a * a / mult)
            drg = dla * (-LRU_C * sp)
            dsp = _colsum(dla * rg) * (-LRU_C)
            dlam = -dsp * _sigmoid(-lam_)
            dgates = jnp.concatenate([drg * (rg * (1.0 - rg)), dig * (ig * (1.0 - ig))], axis=1)
            dgates_bf = dgates.astype(BF16)
            dwg = lax.dot_general(xr.astype(BF16), dgates_bf, (((0,), (0,)), ((), ())), preferred_element_type=F32)
            dxr = dxr + lax.dot_general(dgates_bf, wg_, (((1,), (1,)), ((), ())), preferred_element_type=F32)
            dr, dcw_rows = _conv_bwd(dxr, r, cw_)
            parts[0] = dg.astype(BF16)
            parts[1] = dr.astype(BF16)
            _accumulate(b_id == 0, [(dcw_ref, dcw_rows), (dcb_ref, _colsum(dxr)), (dwg_ref, dwg),
                                    (dbg_ref, _colsum(dgates)), (dlam_ref, dlam), (sg_ref, _colsum(dg)),
                                    (sr_ref, _colsum(dr))])

        dh_ref[...] = parts[part]

    at = _ahead(heads, bsz)

    def saved(*ids):
        hd, b = at(*ids)
        return 0, b, 0, hd

    vec = pl.BlockSpec((1, blk), lambda hd, b, p: (0, hd))
    return pl.pallas_call(
        body, name=name, grid=(heads, bsz, 2),
        in_specs=[_strip(s, blk, 0, at), _strip(s, blk, heads, at), _cvec(kw, blk, 0, at),
                  pl.BlockSpec((None, blk, 2 * blk), lambda *ids: (at(*ids)[0], 0, 0)), _cvec(1, blk, 0, at),
                  pl.BlockSpec((6, None, s, blk), saved), _strip(s, blk, 0, at)],
        out_specs=[pl.BlockSpec((None, s, blk), lambda hd, b, p: (b, 0, p * heads + hd)),
                   pl.BlockSpec((kw, blk), lambda hd, b, p: (0, hd)), vec,
                   pl.BlockSpec((None, blk, 2 * blk), lambda hd, b, p: (hd, 0, 0)),
                   pl.BlockSpec((None, 1, 2 * blk), lambda hd, b, p: (hd, 0, 0)), vec, vec, vec],
        out_shape=[jax.ShapeDtypeStruct((bsz, s, r2), BF16), jax.ShapeDtypeStruct((kw, rw), F32),
                   jax.ShapeDtypeStruct((1, rw), F32), jax.ShapeDtypeStruct((heads, blk, 2 * blk), F32),
                   jax.ShapeDtypeStruct((heads, 1, 2 * blk), F32), jax.ShapeDtypeStruct((1, rw), F32),
                   jax.ShapeDtypeStruct((1, rw), F32), jax.ShapeDtypeStruct((1, rw), F32)],
        scratch_shapes=[pltpu.VMEM((2, s, blk), BF16)],
        compiler_params=_params(("parallel", "arbitrary", "arbitrary")),
    )(h, h, cw, wg, lam, sv, dy)


HBM_SPEC = pl.BlockSpec(memory_space=pltpu.HBM)
SEM_SPEC = pl.BlockSpec(memory_space=pltpu.SEMAPHORE)
EFFECT = pltpu.SideEffectType.DATAFLOW_SIDE_EFFECTING


def _peer_copies(srcs, lands, gather, send_sem, recv_sem):
    x, y, c = (lax.axis_index(ax) for ax in MESH_AXES)
    me = 4 * x + 2 * y + c
    copies = []
    for i in range(len(srcs)):
        for d in range(1, N_DEV):
            px = 1 - x if d & 4 else x
            py = 1 - y if d & 2 else y
            pc = 1 - c if d & 1 else c
            src = srcs[i] if gather[i] else srcs[i].at[4 * px + 2 * py + pc]
            k = i * (N_DEV - 1) + d - 1
            copies.append(pltpu.make_async_remote_copy(
                src_ref=src, dst_ref=lands[i].at[me], send_sem=send_sem.at[k], recv_sem=recv_sem.at[k],
                device_id=(px, py, pc), device_id_type=pl.DeviceIdType.MESH))
    return copies


def _exchange_start(arrs, gather, *, name):
    n = len(arrs)
    lands = [lax.empty((N_DEV,) + tuple(a.shape if g else a.shape[1:]), a.dtype) for a, g in zip(arrs, gather)]

    def body(*refs):
        srcs, land_refs = refs[:n], refs[n:2 * n]
        send_sem, recv_sem = refs[2 * n], refs[2 * n + 1]
        token = refs[-1]
        for cp in _peer_copies(srcs, land_refs, gather, send_sem, recv_sem):
            cp.start()
        token[...] = jnp.zeros_like(token)

    sems = pltpu.SemaphoreType.DMA((n * (N_DEV - 1),))
    thru = [pltpu.HBM(a.shape, a.dtype) for a in arrs + lands]
    out = pl.pallas_call(
        body, name=name, in_specs=[HBM_SPEC] * (2 * n),
        out_shape=(sems, sems, *thru, jax.ShapeDtypeStruct((SUBLANES, LANES), F32)),
        out_specs=(SEM_SPEC, SEM_SPEC, *([HBM_SPEC] * (2 * n)), pl.BlockSpec(memory_space=pltpu.VMEM)),
        input_output_aliases={i: 2 + i for i in range(2 * n)},
        compiler_params=pltpu.CompilerParams(has_side_effects=EFFECT),
    )(*[pltpu.with_memory_space_constraint(a, pltpu.HBM) for a in arrs + lands])
    return {"send_sem": out[0], "recv_sem": out[1], "srcs": list(out[2:2 + n]), "lands": list(out[2 + n:2 + 2 * n]),
            "token": out[-1], "gather": list(gather)}


def _exchange_wait(handle, after, *, name):
    srcs, lands, gather = handle["srcs"], handle["lands"], handle["gather"]
    n = len(srcs)

    def body(*refs):
        src_refs, land_refs = refs[:n], refs[n:2 * n]
        send_sem, recv_sem = refs[2 * n], refs[2 * n + 1]
        for cp in _peer_copies(src_refs, land_refs, gather, send_sem, recv_sem):
            cp.wait_send()
            cp.wait_recv()

    out = pl.pallas_call(
        body, name=name,
        in_specs=[HBM_SPEC] * (2 * n) + [SEM_SPEC, SEM_SPEC, pl.BlockSpec(memory_space=pl.ANY)],
        out_shape=tuple(pltpu.HBM(a.shape, a.dtype) for a in srcs + lands), out_specs=tuple([HBM_SPEC] * (2 * n)),
        input_output_aliases={i: i for i in range(2 * n)},
        compiler_params=pltpu.CompilerParams(has_side_effects=EFFECT),
    )(*srcs, *lands, handle["send_sem"], handle["recv_sem"], after)
    return list(out[:n]), list(out[n:])


def _layers_bf16(stacks, *, name):
    counts = [a.shape[0] for a in stacks]

    def body(*refs):
        outs = iter(refs[len(stacks):])
        for i_ref, n_layers in zip(refs, counts):
            for layer in range(n_layers):
                next(outs)[...] = i_ref[layer].astype(BF16)

    flat = pl.pallas_call(
        body, name=name,
        out_shape=[jax.ShapeDtypeStruct(a.shape[1:], BF16) for a in stacks for _ in range(a.shape[0])],
        compiler_params=pltpu.CompilerParams(vmem_limit_bytes=VMEM_LIMIT),
    )(*stacks)
    split, pos = [], 0
    for n_layers in counts:
        split.append(list(flat[pos:pos + n_layers]))
        pos += n_layers
    return split


def _adamw(parts, w, m, v, layer, so_far, *, name, tr=256):
    n_layers, r, c = w.shape
    tr = _tile(r, tr, SUBLANES)
    bc1 = 1.0 / (1.0 - ADAM_B1 ** ADAM_STEP)
    bc2 = 1.0 / (1.0 - ADAM_B2 ** ADAM_STEP)
    if so_far is None:
        so_far = [lax.empty(w.shape, F32) for _ in range(4)]

    def body(p_ref, w_ref, m_ref, v_ref, *rest):
        g_ref, d_ref, mo_ref, vo_ref = rest[4:]
        g = p_ref[0].astype(F32)
        for s in range(1, N_DEV):
            g = g + p_ref[s].astype(F32)
        m_new = ADAM_B1 * m_ref[...] + (1.0 - ADAM_B1) * g
        v_new = ADAM_B2 * v_ref[...] + (1.0 - ADAM_B2) * (g * g)
        g_ref[...] = g
        mo_ref[...] = m_new
        vo_ref[...] = v_new
        d_ref[...] = -ADAM_LR * ((m_new * bc1) / (jnp.sqrt(v_new * bc2) + ADAM_EPS) + ADAM_WD * w_ref[...])

    blk = pl.BlockSpec((None, tr, c), lambda i: (layer, i, 0))
    return pl.pallas_call(
        body, name=name, grid=(r // tr,),
        in_specs=[pl.BlockSpec((N_DEV, tr, c), lambda i: (0, i, 0)), blk, blk, blk]
        + [pl.BlockSpec(memory_space=pl.ANY)] * 4,
        out_specs=[blk] * 4, out_shape=[jax.ShapeDtypeStruct(w.shape, F32)] * 4,
        input_output_aliases={4 + o: o for o in range(4)},
        compiler_params=_params(("parallel",)),
    )(parts, w, m, v, *so_far)


def _whole(slabs, axis):
    x = jnp.moveaxis(slabs, 0, axis)
    shp = x.shape
    return x.reshape(shp[:axis] + (shp[axis] * shp[axis + 1],) + shp[axis + 2:])


def _slabs(whole, axis):
    shp = whole.shape
    x = whole.reshape(shp[:axis] + (N_DEV, shp[axis] // N_DEV) + shp[axis + 1:])
    return jnp.moveaxis(x, axis, 0)


def _pack(vecs, rows):
    flat = jnp.concatenate(vecs, axis=-1)
    pad = rows * LANES - flat.shape[-1]
    flat = jnp.pad(flat, [(0, 0)] * (flat.ndim - 1) + [(0, pad)])
    return flat.reshape(flat.shape[:-1] + (rows, LANES))


def _unpack(packed, sizes):
    flat = packed.reshape(packed.shape[:-2] + (-1,))
    out, pos = [], 0
    for n in sizes:
        out.append(flat[..., pos:pos + n])
        pos += n
    return out


def _pack_rows(sizes):
    total = sum(sizes)
    return -(-total // (LANES * SUBLANES)) * SUBLANES


BIG = {"sc_w_in": 2, "sc_w_out": 1, "lru_w_in": 2, "lru_w_gate": 3, "lru_w_out": 1, "ffn_w_up": 2, "ffn_w_down": 1}
SWAPPED = ("ffn_w_up", "lru_w_in")
TRANSPOSED = SWAPPED
SMALL = ["sc_conv_w", "lru_b_in", "lru_conv_w", "lru_conv_b", "lru_b_gate", "lru_lambda", "ffn_conv_w", "ln_g", "ln_b"]
REPL = ["sc_conv_b", "ffn_conv_b"]
WEIGHTS = ["sc_w_in", "sc_conv_w", "sc_conv_b", "sc_w_out", "lru_w_in", "lru_b_in", "lru_conv_w", "lru_conv_b",
           "lru_w_gate", "lru_b_gate", "lru_lambda", "lru_w_out", "ffn_w_up", "ffn_conv_w", "ffn_conv_b", "ffn_w_down",
           "ln_g", "ln_b"]


STAGES_PER_LAYER = 3


def _stage_big(g):
    i, part = divmod(g, STAGES_PER_LAYER)
    j = i // 2
    if part:
        return [("ffn_w_up" if part == 1 else "ffn_w_down", i)]
    return [("sc_w_in", j), ("sc_w_out", j)] if i % 2 == 0 else [("lru_w_in", j), ("lru_w_gate", j), ("lru_w_out", j)]


def _step(x, loss_target, w, m, v):
    bsz, s, d = x.shape
    t = bsz * s
    depth = w["ffn_w_up"].shape[0]
    alpha = (2.0 * depth) ** 0.25
    heads = w["lru_w_gate"].shape[1]

    small_sizes = [w[k].size for k in SMALL]
    small_rows = _pack_rows(small_sizes)
    small_local = _pack([w[k].reshape(1, -1) for k in SMALL], small_rows)[0]
    me = 4 * lax.axis_index("x") + 2 * lax.axis_index("y") + lax.axis_index("c")

    def with_own(land, own):
        return lax.dynamic_update_slice_in_dim(land, own, me, axis=0)

    stages = STAGES_PER_LAYER * depth
    def held(k, arr):
        return jnp.swapaxes(arr, -1, -2) if k in TRANSPOSED else arr

    def split_axis(k):
        return 0 if k in TRANSPOSED else BIG[k] - 1

    flat_names = [k for k in BIG if w[k].ndim == 3]
    wb = dict(zip(flat_names, _layers_bf16([held(k, w[k]) for k in flat_names], name="weights_bf16")))
    wb.update({k: list(w[k].astype(BF16)) for k in BIG if k not in flat_names})

    gathers, tok = [], None
    for g in range(stages):
        arrs = [wb[k][l] for k, l in _stage_big(g)]
        if g == 0:
            arrs.append(small_local)
        if tok is not None:
            arrs[0] = arrs[0] + tok.astype(BF16)
        gathers.append(_exchange_start(arrs, [True] * len(arrs), name=f"gather_start_{g}"))
        tok = gathers[-1]["token"][0, 0]
    full = {k: [None] * w[k].shape[0] for k in BIG}
    full["sc_conv_b"] = w["sc_conv_b"]
    full["ffn_conv_b"] = w["ffn_conv_b"]

    def arrive(g, after):
        srcs, lands = _exchange_wait(gathers[g], after, name=f"gather_wait_{g}")
        for (k, l), src, land in zip(_stage_big(g), srcs, lands):
            full[k][l] = _whole(with_own(land, src[None]), split_axis(k))
        if g == 0:
            for k, seg in zip(SMALL, _unpack(with_own(lands[-1], srcs[-1][None]), small_sizes)):
                full[k] = _whole(seg.reshape((N_DEV,) + w[k].shape), w[k].ndim - 1)

    stream, stream_ln = x.reshape(t, d), None
    xb = stream.astype(BF16)
    saved = []
    for i in range(depth):
        j = i // 2
        arrive(3 * i, gathers[-1]["token"] if i == 0 else xb)
        lng, lnb = full["ln_g"][i], full["ln_b"][i]
        sv = {"x0": xb}
        if i % 2 == 0:
            hm = _mm(xb, full["sc_w_in"][j], name="sc_in")
            q = _sc_fwd(hm.reshape(bsz, s, -1), full["sc_conv_w"][j], full["sc_conv_b"][j:j + 1], name="sc_mix")
            w_out = full["sc_w_out"][j]
        else:
            hm = _mm(xb, full["lru_w_in"][j], trans_w=True, bias=full["lru_b_in"][j:j + 1], name="lru_in")
            q, hs = _lru_fwd(hm.reshape(bsz, s, -1), full["lru_conv_w"][j], full["lru_conv_b"][j:j + 1],
                             full["lru_w_gate"][j], full["lru_b_gate"][j].reshape(heads, 1, -1),
                             full["lru_lambda"][j:j + 1], name="lru_mix")
            sv["hs"] = hs
            w_out = full["lru_w_out"][j]
        q = q.reshape(t, -1)
        arrive(3 * i + 1, q)
        z1, x1b = _mm_ln(q, w_out, stream, alpha, lng[0:1], lnb[0:1], resid_ln=stream_ln, name="mix_out_ln")
        hg, hv, gc, vc, a = _ffn_fwd(x1b.reshape(bsz, s, d), full["ffn_w_up"][i], full["ffn_conv_w"][i],
                                     full["ffn_conv_b"][i:i + 1], name="ffn_up_act")
        a = a.reshape(t, -1)
        arrive(3 * i + 2, a)
        z2, xb = _mm_ln(a, full["ffn_w_down"][i], z1, alpha, lng[1:2], lnb[1:2], resid_ln=(lng[0:1], lnb[0:1]),
                        name="ffn_down_ln")
        stream, stream_ln = z2, (lng[1:2], lnb[1:2])
        sv.update(hm=hm, q=q, z1=z1, x1=x1b, ffn=(hg, hv, gc, vc), a=a, z2=z2)
        saved.append(sv)

    sq, dx = _loss_head(stream, *stream_ln, loss_target.reshape(t, d), name="loss_head")
    loss = lax.psum((0.5 / d) * sq[0, 0], MESH_AXES)

    grads = {k: [None] * w[k].shape[0] for k in WEIGHTS}
    scatters = [None] * stages

    def as_updated(k, arr):
        return jnp.swapaxes(arr, -1, -2) if k in SWAPPED else arr

    def depart(g):
        send = [_slabs(grads[k][l], split_axis(k)).astype(BF16) for k, l in _stage_big(g)]
        send = [sl if k in TRANSPOSED else as_updated(k, sl) for (k, l), sl in zip(_stage_big(g), send)]
        scatters[g] = _exchange_start(send, [False] * len(send), name=f"scatter_start_{g}")
        return scatters[g]["token"][0:1, 0:1]

    dz2, dz2b, dg2, db2 = _ln_bwd(dx, saved[-1]["z2"], full["ln_g"][-1][1:2], name="ln_bwd")
    for i in reversed(range(depth)):
        j = i // 2
        sv = saved[i]
        lng = full["ln_g"][i]
        grads["ffn_w_down"][i] = _mm_tn(sv["a"], dz2b, name="ffn_down_dw")
        dhg, dhv, dwg, dwv, dbg, dbv = _ffn_bwd(*sv["ffn"], dz2b.reshape(bsz, s, d), full["ffn_w_down"][i],
                                                full["ffn_conv_w"][i] + depart(3 * i + 2), name="ffn_act_bwd")
        dhg, dhv = dhg.reshape(t, -1), dhv.reshape(t, -1)
        grads["ffn_conv_w"][i] = jnp.concatenate([dwg, dwv], axis=1)
        grads["ffn_conv_b"][i] = jnp.concatenate([dbg, dbv], axis=1)[0]
        grads["ffn_w_up"][i] = jnp.concatenate([_mm_tn(dhg, sv["x1"], name="ffn_up_dw_g"),
                                                _mm_tn(dhv, sv["x1"], name="ffn_up_dw_v")], axis=0)
        dz1, dz1b, dg1, db1 = _mm_ln_bwd([dhg, dhv], full["ffn_w_up"][i], dz2, alpha, sv["z1"],
                                         lng[0:1] + depart(3 * i + 1), name="ffn_up_dx_ln", w_rows_are_k=True)
        grads["ln_g"][i] = jnp.concatenate([dg1, dg2], axis=0)
        grads["ln_b"][i] = jnp.concatenate([db1, db2], axis=0)
        if i % 2 == 0:
            dq = _mm(dz1b, full["sc_w_out"][j], trans_w=True, name="sc_out_dx")
            grads["sc_w_out"][j] = _mm_tn(sv["q"], dz1b, name="sc_out_dw")
            dhm, dcw, dcb = _sc_bwd(sv["hm"].reshape(bsz, s, -1), dq.reshape(bsz, s, -1), full["sc_conv_w"][j],
                                    full["sc_conv_b"][j:j + 1], name="sc_mix_bwd")
            dhm = dhm.reshape(t, -1)
            grads["sc_conv_w"][j] = dcw
            grads["sc_conv_b"][j] = dcb[0]
            grads["sc_w_in"][j] = _mm_tn(sv["x0"], dhm, name="sc_in_dw")
            w_in = full["sc_w_in"][j]
        else:
            dq = _mm(dz1b, full["lru_w_out"][j], trans_w=True, name="lru_out_dx")
            grads["lru_w_out"][j] = _mm_tn(sv["q"], dz1b, name="lru_out_dw")
            dhm, dcw, dcb, dwgt, dbgt, dlam, sgb, srb = _lru_bwd(
                sv["hm"].reshape(bsz, s, -1), sv["hs"], dq.reshape(bsz, s, -1), full["lru_conv_w"][j],
                full["lru_w_gate"][j], full["lru_lambda"][j:j + 1], name="lru_mix_bwd")
            dhm = dhm.reshape(t, -1)
            grads["lru_conv_w"][j] = dcw
            grads["lru_conv_b"][j] = dcb[0]
            grads["lru_w_gate"][j] = dwgt
            grads["lru_b_gate"][j] = dbgt[:, 0, :]
            grads["lru_lambda"][j] = dlam[0]
            grads["lru_b_in"][j] = jnp.concatenate([sgb, srb], axis=1)[0]
            grads["lru_w_in"][j] = _mm_tn(dhm, sv["x0"], name="lru_in_dw")
            w_in = full["lru_w_in"][j]
        tok = depart(3 * i)
        if i > 0:
            dz2, dz2b, dg2, db2 = _mm_ln_bwd([dhm], w_in, dz1, alpha, saved[i - 1]["z2"], full["ln_g"][i - 1][1:2] + tok,
                                             name="mix_in_dx_ln", w_rows_are_k=i % 2 == 1)
        else:
            dx = _mm(dhm, w_in + tok[0, 0].astype(BF16), trans_w=True, resid=dz1, resid_scale=alpha, name="mix_in_dx")
    grad_x = dx.reshape(bsz, s, d)

    gsm = {k: jnp.stack(grads[k]) for k in SMALL + REPL}
    small_send = _pack([_slabs(gsm[k], gsm[k].ndim - 1).reshape(N_DEV, -1) for k in SMALL], small_rows)
    small_scatter = _exchange_start([small_send] + [gsm[k] for k in REPL], [False] + [True] * len(REPL),
                                    name="scatter_start_small")

    out = {}

    def own_slab(src):
        return lax.dynamic_slice_in_dim(src, me, 1, axis=0)

    stacks = {k: None for k in BIG}
    after = dx
    for g in reversed(range(stages)):
        srcs, lands = _exchange_wait(scatters[g], after, name=f"scatter_wait_{g}")
        for (k, l), src, land in zip(_stage_big(g), srcs, lands):
            n_l, c2 = w[k].shape[0], land.shape[-1]
            wk, mk, vk = (as_updated(k, arr[k]).reshape(n_l, -1, c2) for arr in (w, m, v))
            stacks[k] = _adamw(with_own(land, own_slab(src)).reshape(N_DEV, -1, c2), wk, mk, vk, l, stacks[k],
                               name=f"adamw_{k}_{l}")
            after = stacks[k][-1]
    for k in BIG:
        shp = as_updated(k, w[k]).shape
        out[k] = [as_updated(k, r.reshape(shp)) for r in stacks[k]]
    srcs, lands = _exchange_wait(small_scatter, after, name="scatter_wait_small")
    got_small = with_own(lands[0], own_slab(srcs[0]))
    pk = lambda src, names, rows: _pack([src[k].reshape(1, -1) for k in names], rows)
    res = _adamw(got_small, small_local[None], pk(m, SMALL, small_rows), pk(v, SMALL, small_rows), 0, None,
                 name="adamw_small")
    for r_i, r in enumerate(res):
        for k, seg in zip(SMALL, _unpack(r[0], small_sizes)):
            out.setdefault(k, [None] * 4)[r_i] = seg.reshape(w[k].shape)
    for n, k in enumerate(REPL, start=1):
        res = _adamw(with_own(lands[n], srcs[n][None]), w[k][None], m[k][None], v[k][None], 0, None, name="adamw_" + k)
        out[k] = [r[0] for r in res]

    return (loss, grad_x, *[out[k][0] for k in WEIGHTS], *[out[k][1] for k in WEIGHTS],
            *[out[k][2] for k in WEIGHTS], *[out[k][3] for k in WEIGHTS])


def kernel(x, sc_w_in, sc_conv_w, sc_conv_b, sc_w_out, lru_w_in, lru_b_in, lru_conv_w, lru_conv_b, lru_w_gate, lru_b_gate, lru_lambda, lru_w_out, ffn_w_up, ffn_conv_w, ffn_conv_b, ffn_w_down, ln_g, ln_b, loss_target, m_sc_w_in, m_sc_conv_w, m_sc_conv_b, m_sc_w_out, m_lru_w_in, m_lru_b_in, m_lru_conv_w, m_lru_conv_b, m_lru_w_gate, m_lru_b_gate, m_lru_lambda, m_lru_w_out, m_ffn_w_up, m_ffn_conv_w, m_ffn_conv_b, m_ffn_w_down, m_ln_g, m_ln_b, v_sc_w_in, v_sc_conv_w, v_sc_conv_b, v_sc_w_out, v_lru_w_in, v_lru_b_in, v_lru_conv_w, v_lru_conv_b, v_lru_w_gate, v_lru_b_gate, v_lru_lambda, v_lru_w_out, v_ffn_w_up, v_ffn_conv_w, v_ffn_conv_b, v_ffn_w_down, v_ln_g, v_ln_b):
    w = dict(sc_w_in=sc_w_in, sc_conv_w=sc_conv_w, sc_conv_b=sc_conv_b, sc_w_out=sc_w_out, lru_w_in=lru_w_in,
             lru_b_in=lru_b_in, lru_conv_w=lru_conv_w, lru_conv_b=lru_conv_b, lru_w_gate=lru_w_gate,
             lru_b_gate=lru_b_gate, lru_lambda=lru_lambda, lru_w_out=lru_w_out, ffn_w_up=ffn_w_up,
             ffn_conv_w=ffn_conv_w, ffn_conv_b=ffn_conv_b, ffn_w_down=ffn_w_down, ln_g=ln_g, ln_b=ln_b)
    m = dict(sc_w_in=m_sc_w_in, sc_conv_w=m_sc_conv_w, sc_conv_b=m_sc_conv_b, sc_w_out=m_sc_w_out, lru_w_in=m_lru_w_in,
             lru_b_in=m_lru_b_in, lru_conv_w=m_lru_conv_w, lru_conv_b=m_lru_conv_b, lru_w_gate=m_lru_w_gate,
             lru_b_gate=m_lru_b_gate, lru_lambda=m_lru_lambda, lru_w_out=m_lru_w_out, ffn_w_up=m_ffn_w_up,
             ffn_conv_w=m_ffn_conv_w, ffn_conv_b=m_ffn_conv_b, ffn_w_down=m_ffn_w_down, ln_g=m_ln_g, ln_b=m_ln_b)
    v = dict(sc_w_in=v_sc_w_in, sc_conv_w=v_sc_conv_w, sc_conv_b=v_sc_conv_b, sc_w_out=v_sc_w_out, lru_w_in=v_lru_w_in,
             lru_b_in=v_lru_b_in, lru_conv_w=v_lru_conv_w, lru_conv_b=v_lru_conv_b, lru_w_gate=v_lru_w_gate,
             lru_b_gate=v_lru_b_gate, lru_lambda=v_lru_lambda, lru_w_out=v_lru_w_out, ffn_w_up=v_ffn_w_up,
             ffn_conv_w=v_ffn_conv_w, ffn_conv_b=v_ffn_conv_b, ffn_w_down=v_ffn_w_down, ln_g=v_ln_g, ln_b=v_ln_b)
    return _step(x, loss_target, w, m, v)
```

```python
import functools
import math

import jax
import jax.numpy as jnp
from jax import lax
from jax.experimental import pallas as pl
from jax.experimental.pallas import tpu as pltpu

F32 = jnp.float32
BF16 = jnp.bfloat16

N_DEV = 8
MESH_AXES = ("x", "y", "c")
LANES = 128
SUBLANES = 8
VMEM_LIMIT = 56 * 1024 * 1024
MM_LHS_ELEMS = 3 * 1024 * 1024
MM_TN = 1536

LRU_C = 8.0
LN_EPS = 1e-5
ADAM_LR = 0.001
ADAM_B1 = 0.9
ADAM_B2 = 0.999
ADAM_EPS = 1e-08
ADAM_WD = 0.01
ADAM_STEP = 10
GELU_K = math.sqrt(2.0 / math.pi)
GELU_C = 0.044715


def _tile(n, target, align):
    if n <= target:
        return n
    t = (target // align) * align
    while t >= align:
        if n % t == 0:
            return t
        t -= align
    return n


def _params(sem):
    return pltpu.CompilerParams(dimension_semantics=sem, vmem_limit_bytes=VMEM_LIMIT)


def _rows(x):
    return lax.broadcasted_iota(jnp.int32, x.shape, 0)


def _shift_dn(x, k, fill=0.0):
    if k == 0:
        return x
    return jnp.where(_rows(x) >= k, pltpu.roll(x, k, 0), fill)


def _shift_up(x, k, fill=0.0):
    if k == 0:
        return x
    s = x.shape[0]
    return jnp.where(_rows(x) < s - k, pltpu.roll(x, s - k, 0), fill)


def _conv_fwd(x, w, b):
    kw = w.shape[0]
    y = _shift_dn(x, kw - 1) * w[0:1, :] + b
    for k in range(1, kw):
        y = y + _shift_dn(x, kw - 1 - k) * w[k:k + 1, :]
    return y


def _conv_bwd(dy, x, w):
    kw = w.shape[0]
    ahead = [_shift_up(dy, j) for j in range(kw)]
    dx = ahead[kw - 1] * w[0:1, :]
    for k in range(1, kw):
        dx = dx + ahead[kw - 1 - k] * w[k:k + 1, :]
    return dx, [_colsum(ahead[kw - 1 - k] * x) for k in range(kw)]


def _accumulate(first, items, cols=slice(None)):
    flat = []
    for ref, val in items:
        if isinstance(val, list):
            flat += [(ref, (slice(k, k + 1), cols), row) for k, row in enumerate(val)]
        else:
            flat.append((ref, Ellipsis, val))

    @pl.when(first)
    def _():
        for ref, idx, val in flat:
            ref[idx] = val

    @pl.when(jnp.logical_not(first))
    def _():
        for ref, idx, val in flat:
            ref[idx] += val


def _colsum(x):
    return jnp.sum(x, axis=0, keepdims=True)


def _sigmoid(x):
    return 1.0 / (1.0 + jnp.exp(-x))


def _log1p(x):
    u = 1.0 + x
    return jnp.where(u == 1.0, x, jnp.log(u) * (x / (u - 1.0)))


def _softplus(x):
    return jnp.maximum(x, 0.0) + _log1p(jnp.exp(-jnp.abs(x)))


def _expm1(x, ex):
    poly = x * (1.0 + x * (0.5 + x * (1.0 / 6.0 + x * (1.0 / 24.0 + x * (1.0 / 120.0 + x * (1.0 / 720.0))))))
    return jnp.where(jnp.abs(x) < 0.25, poly, ex - 1.0)


def _gelu(x):
    t = jnp.tanh(GELU_K * (x + GELU_C * x * x * x))
    return 0.5 * x * (1.0 + t)


def _gelu_and_grad(x):
    x2 = x * x
    t = jnp.tanh(GELU_K * (x + GELU_C * x * x2))
    g = 0.5 * x * (1.0 + t)
    dg = 0.5 * (1.0 + t) + 0.5 * x * (1.0 - t * t) * (GELU_K * (1.0 + 3.0 * GELU_C * x2))
    return g, dg


def _scan_fwd(a, b):
    s = a.shape[0]
    k = 1
    while k < s:
        last = 2 * k >= s
        if k % SUBLANES:
            b = a * _shift_dn(b, k) + b
            if not last:
                a = a * _shift_dn(a, k, 1.0)
        else:
            b = jnp.concatenate([b[:k], a[k:] * b[:s - k] + b[k:]], axis=0)
            if not last:
                a = jnp.concatenate([a[:k], a[k:] * a[:s - k]], axis=0)
        k *= 2
    return b


def _scan_rev(c, v):
    s = c.shape[0]
    k = 1
    while k < s:
        last = 2 * k >= s
        if k % SUBLANES:
            v = c * _shift_up(v, k) + v
            if not last:
                c = c * _shift_up(c, k, 1.0)
        else:
            v = jnp.concatenate([c[:s - k] * v[k:] + v[:s - k], v[s - k:]], axis=0)
            if not last:
                c = jnp.concatenate([c[:s - k] * c[k:], c[s - k:]], axis=0)
        k *= 2
    return v


def _mm(a, w, *, name, trans_w=False, bias=None, resid=None, resid_scale=1.0):
    m, k = a.shape
    n = w.shape[0] if trans_w else w.shape[1]
    tm = _tile(m, min(1024, max(256, MM_LHS_ELEMS // k)), SUBLANES)
    tn = _tile(n, MM_TN, LANES)
    has_bias = bias is not None
    has_resid = resid is not None

    def body(*refs):
        a_ref, w_ref = refs[0], refs[1]
        pos = 2
        b_ref = r_ref = None
        if has_bias:
            b_ref = refs[pos]
            pos += 1
        if has_resid:
            r_ref = refs[pos]
            pos += 1
        o_ref = refs[pos]

        cols = pl.ds(pl.multiple_of(pl.program_id(1) * tn, LANES), tn)
        if trans_w:
            acc = lax.dot_general(a_ref[...], w_ref[cols, :], (((1,), (1,)), ((), ())), preferred_element_type=F32)
        else:
            acc = jnp.dot(a_ref[...], w_ref[:, cols], preferred_element_type=F32)
        if has_bias:
            acc = acc + b_ref[...]
        if has_resid:
            acc = acc + resid_scale * r_ref[...]
        o_ref[...] = acc

    in_specs = [pl.BlockSpec((tm, k), lambda i, j: (i, 0)),
                pl.BlockSpec(w.shape, lambda i, j: (0, 0), pipeline_mode=pl.Buffered(1))]
    args = [a, w]
    if has_bias:
        in_specs.append(pl.BlockSpec((1, tn), lambda i, j: (0, j)))
        args.append(bias)
    if has_resid:
        in_specs.append(pl.BlockSpec((tm, tn), lambda i, j: (i, j)))
        args.append(resid)
    return pl.pallas_call(
        body, name=name, grid=(m // tm, n // tn), in_specs=in_specs,
        out_specs=pl.BlockSpec((tm, tn), lambda i, j: (i, j)),
        out_shape=jax.ShapeDtypeStruct((m, n), F32),
        compiler_params=_params(("parallel", "arbitrary")),
    )(*args)


def _ln(z, g, b):
    mu = jnp.mean(z, axis=-1, keepdims=True)
    zc = z - mu
    var = jnp.mean(zc * zc, axis=-1, keepdims=True)
    return zc * lax.rsqrt(var + LN_EPS) * g + b


def _mm_ln(a, w, resid, alpha, g, b, *, name, resid_ln=None, tm=512):
    m, k = a.shape
    d = w.shape[1]
    tm = _tile(m, tm, SUBLANES)
    n_extra = 0 if resid_ln is None else 2

    def body(a_ref, w_ref, r_ref, g_ref, b_ref, *rest):
        z_ref, obf_ref = rest[n_extra:]
        x = r_ref[...]
        if resid_ln is not None:
            x = _ln(x, rest[0][...], rest[1][...])
        z = alpha * x + jnp.dot(a_ref[...], w_ref[...], preferred_element_type=F32)
        z_ref[...] = z
        obf_ref[...] = _ln(z, g_ref[...], b_ref[...]).astype(BF16)

    row = pl.BlockSpec((tm, d), lambda i: (i, 0))
    vec = pl.BlockSpec((1, d), lambda i: (0, 0))
    return pl.pallas_call(
        body, name=name, grid=(m // tm,),
        in_specs=[pl.BlockSpec((tm, k), lambda i: (i, 0)),
                  pl.BlockSpec((k, d), lambda i: (0, 0), pipeline_mode=pl.Buffered(1)), row, vec, vec]
        + [vec] * n_extra,
        out_specs=[row, row],
        out_shape=[jax.ShapeDtypeStruct((m, d), F32), jax.ShapeDtypeStruct((m, d), BF16)],
        compiler_params=_params(("parallel",)),
    )(a, w, resid, g, b, *(resid_ln or ()))


def _ln_bwd_math(do, z, g):
    mu = jnp.mean(z, axis=-1, keepdims=True)
    zc = z - mu
    var = jnp.mean(zc * zc, axis=-1, keepdims=True)
    rstd = lax.rsqrt(var + LN_EPS)
    xhat = zc * rstd
    dxh = do * g
    m1 = jnp.mean(dxh, axis=-1, keepdims=True)
    m2 = jnp.mean(dxh * xhat, axis=-1, keepdims=True)
    return rstd * (dxh - m1 - xhat * m2), _colsum(do * xhat), _colsum(do)


def _mm_ln_bwd(parts, w, resid, resid_scale, z, g, *, name, w_rows_are_k=False):
    t, kp = parts[0].shape
    k, d = w.shape if w_rows_are_k else w.shape[::-1]
    n = len(parts)
    tm = _tile(t, min(512, max(256, MM_LHS_ELEMS // k)), SUBLANES)

    def body(*refs):
        a_refs = refs[:n]
        w_ref, r_ref, z_ref, g_ref, dz_ref, dzbf_ref, dg_ref, db_ref = refs[n:]

        @pl.when(pl.program_id(0) == 0)
        def _():
            dg_ref[...] = jnp.zeros_like(dg_ref)
            db_ref[...] = jnp.zeros_like(db_ref)

        dx = resid_scale * r_ref[...]
        for p, a_ref in enumerate(a_refs):
            if w_rows_are_k:
                dx = dx + jnp.dot(a_ref[...], w_ref[p * kp:(p + 1) * kp, :], preferred_element_type=F32)
            else:
                dx = dx + lax.dot_general(a_ref[...], w_ref[:, p * kp:(p + 1) * kp], (((1,), (1,)), ((), ())),
                                          preferred_element_type=F32)
        dz, dg, db = _ln_bwd_math(dx, z_ref[...], g_ref[...])
        dz_ref[...] = dz
        dzbf_ref[...] = dz.astype(BF16)
        dg_ref[...] += dg
        db_ref[...] += db

    row = pl.BlockSpec((tm, d), lambda i: (i, 0))
    vec = pl.BlockSpec((1, d), lambda i: (0, 0))
    return pl.pallas_call(
        body, name=name, grid=(t // tm,),
        in_specs=[pl.BlockSpec((tm, kp), lambda i: (i, 0))] * n
        + [pl.BlockSpec(w.shape, lambda i: (0, 0), pipeline_mode=pl.Buffered(1)), row, row, vec],
        out_specs=[row, row, vec, vec],
        out_shape=[jax.ShapeDtypeStruct((t, d), F32), jax.ShapeDtypeStruct((t, d), BF16),
                   jax.ShapeDtypeStruct((1, d), F32), jax.ShapeDtypeStruct((1, d), F32)],
        compiler_params=_params(("arbitrary",)),
    )(*parts, w, resid, z, g)


def _mm_tn(a, b, *, name, tm=1408, tn=1536, tk=1024):
    t, m = a.shape
    n = b.shape[1]
    tm = _tile(m, tm, LANES)
    tn = _tile(n, tn, LANES)
    tk = _tile(t, tk, SUBLANES)
    last = t // tk - 1

    def body(a_ref, b_ref, o_ref, acc):
        @pl.when(pl.program_id(2) == 0)
        def _():
            acc[...] = jnp.zeros_like(acc)

        acc[...] += lax.dot_general(a_ref[...], b_ref[...], (((0,), (0,)), ((), ())), preferred_element_type=F32)

        @pl.when(pl.program_id(2) == last)
        def _():
            o_ref[...] = acc[...].astype(BF16)

    return pl.pallas_call(
        body, name=name, grid=(m // tm, n // tn, t // tk),
        in_specs=[pl.BlockSpec((tk, tm), lambda i, j, l: (l, i)), pl.BlockSpec((tk, tn), lambda i, j, l: (l, j))],
        out_specs=pl.BlockSpec((tm, tn), lambda i, j, l: (i, j)),
        out_shape=jax.ShapeDtypeStruct((m, n), BF16),
        scratch_shapes=[pltpu.VMEM((tm, tn), F32)],
        compiler_params=_params(("parallel", "parallel", "arbitrary")),
    )(a, b)


def _ln_bwd(dout, z, g, *, name, tm=512):
    t, d = z.shape
    tm = _tile(t, tm, SUBLANES)

    def body(do_ref, z_ref, g_ref, dz_ref, dzbf_ref, dg_ref, db_ref):
        @pl.when(pl.program_id(0) == 0)
        def _():
            dg_ref[...] = jnp.zeros_like(dg_ref)
            db_ref[...] = jnp.zeros_like(db_ref)

        dz, dg, db = _ln_bwd_math(do_ref[...], z_ref[...], g_ref[...])
        dz_ref[...] = dz
        dzbf_ref[...] = dz.astype(BF16)
        dg_ref[...] += dg
        db_ref[...] += db

    row = pl.BlockSpec((tm, d), lambda i: (i, 0))
    vec = pl.BlockSpec((1, d), lambda i: (0, 0))
    return pl.pallas_call(
        body, name=name, grid=(t // tm,), in_specs=[row, row, vec], out_specs=[row, row, vec, vec],
        out_shape=[jax.ShapeDtypeStruct((t, d), F32), jax.ShapeDtypeStruct((t, d), BF16),
                   jax.ShapeDtypeStruct((1, d), F32), jax.ShapeDtypeStruct((1, d), F32)],
        compiler_params=_params(("arbitrary",)),
    )(dout, z, g)


def _loss_head(z, g, b, target, *, name, tm=512):
    t, d = z.shape
    tm = _tile(t, tm, SUBLANES)

    def body(z_ref, g_ref, b_ref, t_ref, s_ref, dy_ref):
        @pl.when(pl.program_id(0) == 0)
        def _():
            s_ref[...] = jnp.zeros_like(s_ref)

        e = _ln(z_ref[...], g_ref[...], b_ref[...]) - t_ref[...]
        dy_ref[...] = e * (1.0 / d)
        s_ref[...] += jnp.sum(_colsum(e * e), axis=-1, keepdims=True)

    row = pl.BlockSpec((tm, d), lambda i: (i, 0))
    vec = pl.BlockSpec((1, d), lambda i: (0, 0))
    return pl.pallas_call(
        body, name=name, grid=(t // tm,), in_specs=[row, vec, vec, row],
        out_specs=[pl.BlockSpec((1, LANES), lambda i: (0, 0)), row],
        out_shape=[jax.ShapeDtypeStruct((1, LANES), F32), jax.ShapeDtypeStruct((t, d), F32)],
        compiler_params=_params(("arbitrary",)),
    )(z, g, b, target)


def _own(c, b, *_):
    return c, b


def _ahead(nc, bsz):
    def at(c, b, part):
        b2 = b + jnp.minimum(part, 1)
        return jnp.minimum(c + b2 // bsz, nc - 1), b2 % bsz
    return at


def _strip(s, tc, off, at=_own):
    def index(*ids):
        c, b = at(*ids)
        return b, 0, off + c
    return pl.BlockSpec((None, s, tc), index)


def _cvec(kw, tc, off, at=_own):
    def index(*ids):
        return 0, off + at(*ids)[0]
    return pl.BlockSpec((kw, tc), index)


def _acc(kw, tc):
    return pl.BlockSpec((kw, tc), lambda c, b, *_: (0, c))


def _sc_fwd(h, cw, cb, *, name, tc=256):
    bsz, s, d3 = h.shape
    d = d3 // 3
    tc = _tile(d, tc, LANES)
    nc = d // tc

    def body(gb_ref, gc_ref, v_ref, w_ref, b_ref, q_ref):
        u = _conv_fwd(gc_ref[...] * v_ref[...], w_ref[...], b_ref[...])
        q_ref[...] = (gb_ref[...] * u).astype(BF16)

    return pl.pallas_call(
        body, name=name, grid=(nc, bsz),
        in_specs=[_strip(s, tc, 0), _strip(s, tc, nc), _strip(s, tc, 2 * nc), _cvec(cw.shape[0], tc, 0), _cvec(1, tc, 0)],
        out_specs=_strip(s, tc, 0),
        out_shape=jax.ShapeDtypeStruct((bsz, s, d), BF16),
        compiler_params=_params(("parallel", "parallel")),
    )(h, h, h, cw, cb)


def _sc_bwd(h, dq, cw, cb, *, name, tc=256):
    bsz, s, d3 = h.shape
    d = d3 // 3
    kw = cw.shape[0]
    tc = _tile(d, tc, LANES)
    nc = d // tc

    def body(gb_ref, gc_ref, v_ref, dq_ref, w_ref, b_ref, dh_ref, dw_ref, db_ref, parts):
        b_id, part = pl.program_id(1), pl.program_id(2)

        @pl.when(part == 0)
        def _():
            gb, gc, v, dq_, w = gb_ref[...], gc_ref[...], v_ref[...], dq_ref[...], w_ref[...]
            p = gc * v
            u = _conv_fwd(p, w, b_ref[...])
            du = dq_ * gb
            dp, dw_rows = _conv_bwd(du, p, w)
            parts[0] = (dq_ * u).astype(BF16)
            parts[1] = (dp * v).astype(BF16)
            parts[2] = (dp * gc).astype(BF16)
            _accumulate(b_id == 0, [(dw_ref, dw_rows), (db_ref, _colsum(du))])

        dh_ref[...] = parts[part]

    at = _ahead(nc, bsz)
    return pl.pallas_call(
        body, name=name, grid=(nc, bsz, 3),
        in_specs=[_strip(s, tc, 0, at), _strip(s, tc, nc, at), _strip(s, tc, 2 * nc, at), _strip(s, tc, 0, at),
                  _cvec(kw, tc, 0, at), _cvec(1, tc, 0, at)],
        out_specs=[pl.BlockSpec((None, s, tc), lambda c, b, p: (b, 0, p * nc + c)), _acc(kw, tc), _acc(1, tc)],
        out_shape=[jax.ShapeDtypeStruct((bsz, s, d3), BF16), jax.ShapeDtypeStruct((kw, d), F32),
                   jax.ShapeDtypeStruct((1, d), F32)],
        scratch_shapes=[pltpu.VMEM((3, s, tc), BF16)],
        compiler_params=_params(("parallel", "arbitrary", "arbitrary")),
    )(h, h, h, dq, cw, cb)


def _ffn_specs(s, tc, nc, kw):
    strip = pl.BlockSpec((None, s, tc), lambda b, c: (b, 0, c))
    halves = [pl.BlockSpec((kw, tc), lambda b, c: (0, c)), pl.BlockSpec((kw, tc), lambda b, c: (0, nc + c)),
              pl.BlockSpec((1, tc), lambda b, c: (0, c)), pl.BlockSpec((1, tc), lambda b, c: (0, nc + c))]
    return strip, halves


def _ffn_fwd(x, w_up, cw, cb, *, name, tc=256):
    bsz, s, d = x.shape
    f = w_up.shape[0] // 2
    kw = cw.shape[0]
    tc = _tile(f, tc, LANES)
    nc = f // tc
    nt = (((1,), (1,)), ((), ()))

    def body(x_ref, w_ref, wg_ref, wv_ref, bg_ref, bv_ref, hg_ref, hv_ref, g_ref, v_ref, a_ref):
        c0 = pl.multiple_of(pl.program_id(1) * tc, LANES)
        xs = x_ref[...]
        hg = lax.dot_general(xs, w_ref[pl.ds(c0, tc), :], nt, preferred_element_type=F32)
        hv = lax.dot_general(xs, w_ref[pl.ds(f + c0, tc), :], nt, preferred_element_type=F32)
        hg_ref[...] = hg
        hv_ref[...] = hv
        g = _conv_fwd(hg, wg_ref[...], bg_ref[...])
        v = _conv_fwd(hv, wv_ref[...], bv_ref[...])
        g_ref[...] = g
        v_ref[...] = v
        a_ref[...] = (g * _sigmoid(g) * v).astype(BF16)

    strip, halves = _ffn_specs(s, tc, nc, kw)
    return pl.pallas_call(
        body, name=name, grid=(bsz, nc),
        in_specs=[pl.BlockSpec((None, s, d), lambda b, c: (b, 0, 0)),
                  pl.BlockSpec(w_up.shape, lambda b, c: (0, 0), pipeline_mode=pl.Buffered(1))] + halves,
        out_specs=[strip] * 5,
        out_shape=[jax.ShapeDtypeStruct((bsz, s, f), F32)] * 4 + [jax.ShapeDtypeStruct((bsz, s, f), BF16)],
        compiler_params=_params(("parallel", "arbitrary")),
    )(x, w_up, cw, cw, cb, cb)


def _ffn_bwd(hg, hv, g, v, dz, w_down, cw, *, name, tc=256):
    bsz, s, f = hg.shape
    d = dz.shape[2]
    kw = cw.shape[0]
    tc = _tile(f, tc, LANES)
    nc = f // tc

    def body(hg_ref, hv_ref, g_ref, v_ref, dz_ref, wd_ref, wg_ref, wv_ref,
             dhg_ref, dhv_ref, dwg_ref, dwv_ref, dbg_ref, dbv_ref):
        c0 = pl.multiple_of(pl.program_id(1) * tc, LANES)
        cols = pl.ds(c0, tc)
        da = lax.dot_general(dz_ref[...], wd_ref[cols, :], (((1,), (1,)), ((), ())), preferred_element_type=F32)
        g_ = g_ref[...]
        sg = _sigmoid(g_)
        dv = da * (g_ * sg)
        dg = da * v_ref[...] * (sg * (1.0 + g_ * (1.0 - sg)))
        dhg, dwg_rows = _conv_bwd(dg, hg_ref[...], wg_ref[...])
        dhv, dwv_rows = _conv_bwd(dv, hv_ref[...], wv_ref[...])
        dhg_ref[...] = dhg.astype(BF16)
        dhv_ref[...] = dhv.astype(BF16)
        _accumulate(pl.program_id(0) == 0, [(dwg_ref, dwg_rows), (dwv_ref, dwv_rows),
                                            (dbg_ref, [_colsum(dg)]), (dbv_ref, [_colsum(dv)])], cols)

    strip, halves = _ffn_specs(s, tc, nc, kw)
    whole = lambda r: pl.BlockSpec((r, f), lambda b, c: (0, 0))
    return pl.pallas_call(
        body, name=name, grid=(bsz, nc),
        in_specs=[strip] * 4 + [pl.BlockSpec((None, s, d), lambda b, c: (b, 0, 0)),
                                pl.BlockSpec(w_down.shape, lambda b, c: (0, 0), pipeline_mode=pl.Buffered(1))]
        + halves[:2],
        out_specs=[strip, strip, whole(kw), whole(kw), whole(1), whole(1)],
        out_shape=[jax.ShapeDtypeStruct((bsz, s, f), BF16), jax.ShapeDtypeStruct((bsz, s, f), BF16),
                   jax.ShapeDtypeStruct((kw, f), F32), jax.ShapeDtypeStruct((kw, f), F32),
                   jax.ShapeDtypeStruct((1, f), F32), jax.ShapeDtypeStruct((1, f), F32)],
        compiler_params=_params(("arbitrary", "arbitrary")),
    )(hg, hv, g, v, dz, w_down, cw, cw)


def _lru_gates(r, cw, cb, wg, bg, lam):
    blk = r.shape[1]
    xr = _conv_fwd(r, cw, cb)
    gates = jnp.dot(xr.astype(BF16), wg, preferred_element_type=F32) + bg
    rg = _sigmoid(gates[:, :blk])
    ig = _sigmoid(gates[:, blk:])
    sp = _softplus(-lam)
    la = (-LRU_C * sp) * rg
    a = jnp.exp(la)
    mult = jnp.sqrt(-_expm1(2.0 * la, a * a))
    return xr, rg, ig, sp, a, mult


def _lru_fwd(h, cw, cb, wg, bg, lam, *, name):
    bsz, s, r2 = h.shape
    heads, blk = wg.shape[0], wg.shape[1]
    kw = cw.shape[0]

    def body(g_ref, r_ref, cw_ref, cb_ref, wg_ref, bg_ref, lam_ref, y_ref, sv_ref):
        xr, rg, ig, _, a, mult = _lru_gates(r_ref[...], cw_ref[...], cb_ref[...], wg_ref[...], bg_ref[...], lam_ref[...])
        hs = _scan_fwd(a, mult * (ig * xr))
        for n, val in enumerate((hs, xr, rg, ig, a, mult)):
            sv_ref[n] = val
        y_ref[...] = (hs * _gelu(g_ref[...])).astype(BF16)

    per_head = lambda hd, b: (hd, 0, 0)
    return pl.pallas_call(
        body, name=name, grid=(heads, bsz),
        in_specs=[_strip(s, blk, 0), _strip(s, blk, heads), _cvec(kw, blk, 0), _cvec(1, blk, 0),
                  pl.BlockSpec((None, blk, 2 * blk), per_head), pl.BlockSpec((None, 1, 2 * blk), per_head),
                  _cvec(1, blk, 0)],
        out_specs=[_strip(s, blk, 0), pl.BlockSpec((6, None, s, blk), lambda hd, b: (0, b, 0, hd))],
        out_shape=[jax.ShapeDtypeStruct((bsz, s, r2 // 2), BF16), jax.ShapeDtypeStruct((6, bsz, s, r2 // 2), F32)],
        compiler_params=_params(("parallel", "parallel")),
    )(h, h, cw, cb, wg, bg, lam)


def _lru_bwd(h, sv, dy, cw, wg, lam, *, name):
    bsz, s, r2 = h.shape
    rw = r2 // 2
    heads, blk = wg.shape[0], wg.shape[1]
    kw = cw.shape[0]

    def body(g_ref, r_ref, cw_ref, wg_ref, lam_ref, sv_ref, dy_ref,
             dh_ref, dcw_ref, dcb_ref, dwg_ref, dbg_ref, dlam_ref, sg_ref, sr_ref, parts):
        b_id, part = pl.program_id(1), pl.program_id(2)

        @pl.when(part == 0)
        def _():
            r, cw_, wg_, lam_ = r_ref[...], cw_ref[...], wg_ref[...], lam_ref[...]
            hs_, xr, rg, ig, a, mult = (sv_ref[n] for n in range(6))
            sp = _softplus(-lam_)
            dy_ = dy_ref[...]
            gel, dgel = _gelu_and_grad(g_ref[...])
            dg = dy_ * hs_ * dgel
            lmb = _scan_rev(_shift_up(a, 1, 1.0), dy_ * gel)
            da = lmb * _shift_dn(hs_, 1)
            dmult = lmb * (ig * xr)
            dig = lmb * (mult * xr)
            dxr = lmb * (mult * ig)
            dla = da * a - dmult * (a * a / mult)
            drg = dla * (-LRU_C * sp)
            dsp = _colsum(dla * rg) * (-LRU_C)
            dlam = -dsp * _sigmoid(-lam_)
            dgates = jnp.concatenate([drg * (rg * (1.0 - rg)), dig * (ig * (1.0 - ig))], axis=1)
            dgates_bf = dgates.astype(BF16)
            dwg = lax.dot_general(xr.astype(BF16), dgates_bf, (((0,), (0,)), ((), ())), preferred_element_type=F32)
            dxr = dxr + lax.dot_general(dgates_bf, wg_, (((1,), (1,)), ((), ())), preferred_element_type=F32)
            dr, dcw_rows = _conv_bwd(dxr, r, cw_)
            parts[0] = dg.astype(BF16)
            parts[1] = dr.astype(BF16)
            _accumulate(b_id == 0, [(dcw_ref, dcw_rows), (dcb_ref, _colsum(dxr)), (dwg_ref, dwg),
                                    (dbg_ref, _colsum(dgates)), (dlam_ref, dlam), (sg_ref, _colsum(dg)),
                                    (sr_ref, _colsum(dr))])

        dh_ref[...] = parts[part]

    at = _ahead(heads, bsz)

    def saved(*ids):
        hd, b = at(*ids)
        return 0, b, 0, hd

    vec = pl.BlockSpec((1, blk), lambda hd, b, p: (0, hd))
    return pl.pallas_call(
        body, name=name, grid=(heads, bsz, 2),
        in_specs=[_strip(s, blk, 0, at), _strip(s, blk, heads, at), _cvec(kw, blk, 0, at),
                  pl.BlockSpec((None, blk, 2 * blk), lambda *ids: (at(*ids)[0], 0, 0)), _cvec(1, blk, 0, at),
                  pl.BlockSpec((6, None, s, blk), saved), _strip(s, blk, 0, at)],
        out_specs=[pl.BlockSpec((None, s, blk), lambda hd, b, p: (b, 0, p * heads + hd)),
                   pl.BlockSpec((kw, blk), lambda hd, b, p: (0, hd)), vec,
                   pl.BlockSpec((None, blk, 2 * blk), lambda hd, b, p: (hd, 0, 0)),
                   pl.BlockSpec((None, 1, 2 * blk), lambda hd, b, p: (hd, 0, 0)), vec, vec, vec],
        out_shape=[jax.ShapeDtypeStruct((bsz, s, r2), BF16), jax.ShapeDtypeStruct((kw, rw), F32),
                   jax.ShapeDtypeStruct((1, rw), F32), jax.ShapeDtypeStruct((heads, blk, 2 * blk), F32),
                   jax.ShapeDtypeStruct((heads, 1, 2 * blk), F32), jax.ShapeDtypeStruct((1, rw), F32),
                   jax.ShapeDtypeStruct((1, rw), F32), jax.ShapeDtypeStruct((1, rw), F32)],
        scratch_shapes=[pltpu.VMEM((2, s, blk), BF16)],
        compiler_params=_params(("parallel", "arbitrary", "arbitrary")),
    )(h, h, cw, wg, lam, sv, dy)


HBM_SPEC = pl.BlockSpec(memory_space=pltpu.HBM)
SEM_SPEC = pl.BlockSpec(memory_space=pltpu.SEMAPHORE)
EFFECT = pltpu.SideEffectType.DATAFLOW_SIDE_EFFECTING


def _peer_copies(srcs, lands, gather, send_sem, recv_sem):
    x, y, c = (lax.axis_index(ax) for ax in MESH_AXES)
    me = 4 * x + 2 * y + c
    copies = []
    for i in range(len(srcs)):
        for d in range(1, N_DEV):
            px = 1 - x if d & 4 else x
            py = 1 - y if d & 2 else y
            pc = 1 - c if d & 1 else c
            src = srcs[i] if gather[i] else srcs[i].at[4 * px + 2 * py + pc]
            k = i * (N_DEV - 1) + d - 1
            copies.append(pltpu.make_async_remote_copy(
                src_ref=src, dst_ref=lands[i].at[me], send_sem=send_sem.at[k], recv_sem=recv_sem.at[k],
                device_id=(px, py, pc), device_id_type=pl.DeviceIdType.MESH))
    return copies


def _exchange_start(arrs, gather, *, name):
    n = len(arrs)
    lands = [lax.empty((N_DEV,) + tuple(a.shape if g else a.shape[1:]), a.dtype) for a, g in zip(arrs, gather)]

    def body(*refs):
        srcs, land_refs = refs[:n], refs[n:2 * n]
        send_sem, recv_sem = refs[2 * n], refs[2 * n + 1]
        token = refs[-1]
        for cp in _peer_copies(srcs, land_refs, gather, send_sem, recv_sem):
            cp.start()
        token[...] = jnp.zeros_like(token)

    sems = pltpu.SemaphoreType.DMA((n * (N_DEV - 1),))
    thru = [pltpu.HBM(a.shape, a.dtype) for a in arrs + lands]
    out = pl.pallas_call(
        body, name=name, in_specs=[HBM_SPEC] * (2 * n),
        out_shape=(sems, sems, *thru, jax.ShapeDtypeStruct((SUBLANES, LANES), F32)),
        out_specs=(SEM_SPEC, SEM_SPEC, *([HBM_SPEC] * (2 * n)), pl.BlockSpec(memory_space=pltpu.VMEM)),
        input_output_aliases={i: 2 + i for i in range(2 * n)},
        compiler_params=pltpu.CompilerParams(has_side_effects=EFFECT),
    )(*[pltpu.with_memory_space_constraint(a, pltpu.HBM) for a in arrs + lands])
    return {"send_sem": out[0], "recv_sem": out[1], "srcs": list(out[2:2 + n]), "lands": list(out[2 + n:2 + 2 * n]),
            "token": out[-1], "gather": list(gather)}


def _exchange_wait(handle, after, *, name):
    srcs, lands, gather = handle["srcs"], handle["lands"], handle["gather"]
    n = len(srcs)

    def body(*refs):
        src_refs, land_refs = refs[:n], refs[n:2 * n]
        send_sem, recv_sem = refs[2 * n], refs[2 * n + 1]
        for cp in _peer_copies(src_refs, land_refs, gather, send_sem, recv_sem):
            cp.wait_send()
            cp.wait_recv()

    out = pl.pallas_call(
        body, name=name,
        in_specs=[HBM_SPEC] * (2 * n) + [SEM_SPEC, SEM_SPEC, pl.BlockSpec(memory_space=pl.ANY)],
        out_shape=tuple(pltpu.HBM(a.shape, a.dtype) for a in srcs + lands), out_specs=tuple([HBM_SPEC] * (2 * n)),
        input_output_aliases={i: i for i in range(2 * n)},
        compiler_params=pltpu.CompilerParams(has_side_effects=EFFECT),
    )(*srcs, *lands, handle["send_sem"], handle["recv_sem"], after)
    return list(out[:n]), list(out[n:])


def _layers_bf16(stacks, *, name):
    counts = [a.shape[0] for a in stacks]

    def body(*refs):
        outs = iter(refs[len(stacks):])
        for i_ref, n_layers in zip(refs, counts):
            for layer in range(n_layers):
                next(outs)[...] = i_ref[layer].astype(BF16)

    flat = pl.pallas_call(
        body, name=name,
        out_shape=[jax.ShapeDtypeStruct(a.shape[1:], BF16) for a in stacks for _ in range(a.shape[0])],
        compiler_params=pltpu.CompilerParams(vmem_limit_bytes=VMEM_LIMIT),
    )(*stacks)
    split, pos = [], 0
    for n_layers in counts:
        split.append(list(flat[pos:pos + n_layers]))
        pos += n_layers
    return split


def _adamw(parts, w, m, v, layer, so_far, *, name, tr=256):
    n_layers, r, c = w.shape
    tr = _tile(r, tr, SUBLANES)
    bc1 = 1.0 / (1.0 - ADAM_B1 ** ADAM_STEP)
    bc2 = 1.0 / (1.0 - ADAM_B2 ** ADAM_STEP)
    if so_far is None:
        so_far = [lax.empty(w.shape, F32) for _ in range(4)]

    def body(p_ref, w_ref, m_ref, v_ref, *rest):
        g_ref, d_ref, mo_ref, vo_ref = rest[4:]
        g = p_ref[0].astype(F32)
        for s in range(1, N_DEV):
            g = g + p_ref[s].astype(F32)
        m_new = ADAM_B1 * m_ref[...] + (1.0 - ADAM_B1) * g
        v_new = ADAM_B2 * v_ref[...] + (1.0 - ADAM_B2) * (g * g)
        g_ref[...] = g
        mo_ref[...] = m_new
        vo_ref[...] = v_new
        d_ref[...] = -ADAM_LR * ((m_new * bc1) / (jnp.sqrt(v_new * bc2) + ADAM_EPS) + ADAM_WD * w_ref[...])

    blk = pl.BlockSpec((None, tr, c), lambda i: (layer, i, 0))
    return pl.pallas_call(
        body, name=name, grid=(r // tr,),
        in_specs=[pl.BlockSpec((N_DEV, tr, c), lambda i: (0, i, 0)), blk, blk, blk]
        + [pl.BlockSpec(memory_space=pl.ANY)] * 4,
        out_specs=[blk] * 4, out_shape=[jax.ShapeDtypeStruct(w.shape, F32)] * 4,
        input_output_aliases={4 + o: o for o in range(4)},
        compiler_params=_params(("parallel",)),
    )(parts, w, m, v, *so_far)


def _whole(slabs, axis):
    x = jnp.moveaxis(slabs, 0, axis)
    shp = x.shape
    return x.reshape(shp[:axis] + (shp[axis] * shp[axis + 1],) + shp[axis + 2:])


def _slabs(whole, axis):
    shp = whole.shape
    x = whole.reshape(shp[:axis] + (N_DEV, shp[axis] // N_DEV) + shp[axis + 1:])
    return jnp.moveaxis(x, axis, 0)


BIG = {"sc_w_in": 2, "sc_w_out": 1, "lru_w_in": 2, "lru_w_gate": 3, "lru_w_out": 1, "ffn_w_up": 2, "ffn_w_down": 1}
SWAPPED = ("ffn_w_up", "lru_w_in")
TRANSPOSED = SWAPPED
SMALL = ["sc_conv_w", "lru_b_in", "lru_conv_w", "lru_conv_b", "lru_b_gate", "lru_lambda", "ffn_conv_w", "ln_g", "ln_b"]
REPL = ["sc_conv_b", "ffn_conv_b"]
WEIGHTS = ["sc_w_in", "sc_conv_w", "sc_conv_b", "sc_w_out", "lru_w_in", "lru_b_in", "lru_conv_w", "lru_conv_b",
           "lru_w_gate", "lru_b_gate", "lru_lambda", "lru_w_out", "ffn_w_up", "ffn_conv_w", "ffn_conv_b", "ffn_w_down",
           "ln_g", "ln_b"]


STAGES_PER_LAYER = 3


def _stage_big(g):
    i, part = divmod(g, STAGES_PER_LAYER)
    j = i // 2
    if part:
        return [("ffn_w_up" if part == 1 else "ffn_w_down", i)]
    return [("sc_w_in", j), ("sc_w_out", j)] if i % 2 == 0 else [("lru_w_in", j), ("lru_w_gate", j), ("lru_w_out", j)]


def _step(x, loss_target, w, m, v):
    bsz, s, d = x.shape
    t = bsz * s
    depth = w["ffn_w_up"].shape[0]
    alpha = (2.0 * depth) ** 0.25
    heads = w["lru_w_gate"].shape[1]

    me = 4 * lax.axis_index("x") + 2 * lax.axis_index("y") + lax.axis_index("c")

    def with_own(land, own):
        return lax.dynamic_update_slice_in_dim(land, own, me, axis=0)

    stages = STAGES_PER_LAYER * depth
    def held(k, arr):
        return jnp.swapaxes(arr, -1, -2) if k in TRANSPOSED else arr

    def split_axis(k):
        return 0 if k in TRANSPOSED else BIG[k] - 1

    flat_names = [k for k in BIG if w[k].ndim == 3]
    wb = dict(zip(flat_names, _layers_bf16([held(k, w[k]) for k in flat_names], name="weights_bf16")))
    wb.update({k: list(w[k].astype(BF16)) for k in BIG if k not in flat_names})

    gathers, tok = [], None
    for g in range(stages):
        arrs = [wb[k][l] for k, l in _stage_big(g)]
        if g == 0:
            arrs += [w[k] for k in SMALL]
        if tok is not None:
            arrs[0] = arrs[0] + tok.astype(BF16)
        gathers.append(_exchange_start(arrs, [True] * len(arrs), name=f"gather_start_{g}"))
        tok = gathers[-1]["token"][0, 0]
    full = {k: [None] * w[k].shape[0] for k in BIG}
    full["sc_conv_b"] = w["sc_conv_b"]
    full["ffn_conv_b"] = w["ffn_conv_b"]

    def arrive(g, after):
        srcs, lands = _exchange_wait(gathers[g], after, name=f"gather_wait_{g}")
        for (k, l), src, land in zip(_stage_big(g), srcs, lands):
            full[k][l] = _whole(with_own(land, src[None]), split_axis(k))
        if g == 0:
            n_big = len(_stage_big(0))
            for k, src, land in zip(SMALL, srcs[n_big:], lands[n_big:]):
                full[k] = _whole(with_own(land, src[None]), w[k].ndim - 1)

    stream, stream_ln = x.reshape(t, d), None
    xb = stream.astype(BF16)
    saved = []
    for i in range(depth):
        j = i // 2
        arrive(3 * i, gathers[-1]["token"] if i == 0 else xb)
        lng, lnb = full["ln_g"][i], full["ln_b"][i]
        sv = {"x0": xb}
        if i % 2 == 0:
            hm = _mm(xb, full["sc_w_in"][j], name="sc_in")
            q = _sc_fwd(hm.reshape(bsz, s, -1), full["sc_conv_w"][j], full["sc_conv_b"][j:j + 1], name="sc_mix")
            w_out = full["sc_w_out"][j]
        else:
            hm = _mm(xb, full["lru_w_in"][j], trans_w=True, bias=full["lru_b_in"][j:j + 1], name="lru_in")
            q, hs = _lru_fwd(hm.reshape(bsz, s, -1), full["lru_conv_w"][j], full["lru_conv_b"][j:j + 1],
                             full["lru_w_gate"][j], full["lru_b_gate"][j].reshape(heads, 1, -1),
                             full["lru_lambda"][j:j + 1], name="lru_mix")
            sv["hs"] = hs
            w_out = full["lru_w_out"][j]
        q = q.reshape(t, -1)
        arrive(3 * i + 1, q)
        z1, x1b = _mm_ln(q, w_out, stream, alpha, lng[0:1], lnb[0:1], resid_ln=stream_ln, name="mix_out_ln")
        hg, hv, gc, vc, a = _ffn_fwd(x1b.reshape(bsz, s, d), full["ffn_w_up"][i], full["ffn_conv_w"][i],
                                     full["ffn_conv_b"][i:i + 1], name="ffn_up_act")
        a = a.reshape(t, -1)
        arrive(3 * i + 2, a)
        z2, xb = _mm_ln(a, full["ffn_w_down"][i], z1, alpha, lng[1:2], lnb[1:2], resid_ln=(lng[0:1], lnb[0:1]),
                        name="ffn_down_ln")
        stream, stream_ln = z2, (lng[1:2], lnb[1:2])
        sv.update(hm=hm, q=q, z1=z1, x1=x1b, ffn=(hg, hv, gc, vc), a=a, z2=z2)
        saved.append(sv)

    sq, dx = _loss_head(stream, *stream_ln, loss_target.reshape(t, d), name="loss_head")
    loss = lax.psum((0.5 / d) * sq[0, 0], MESH_AXES)

    grads = {k: [None] * w[k].shape[0] for k in WEIGHTS}
    scatters = [None] * stages

    def as_updated(k, arr):
        return jnp.swapaxes(arr, -1, -2) if k in SWAPPED else arr

    def depart(g):
        send = [_slabs(grads[k][l], split_axis(k)).astype(BF16) for k, l in _stage_big(g)]
        send = [sl if k in TRANSPOSED else as_updated(k, sl) for (k, l), sl in zip(_stage_big(g), send)]
        scatters[g] = _exchange_start(send, [False] * len(send), name=f"scatter_start_{g}")
        return scatters[g]["token"][0:1, 0:1]

    dz2, dz2b, dg2, db2 = _ln_bwd(dx, saved[-1]["z2"], full["ln_g"][-1][1:2], name="ln_bwd")
    for i in reversed(range(depth)):
        j = i // 2
        sv = saved[i]
        lng = full["ln_g"][i]
        grads["ffn_w_down"][i] = _mm_tn(sv["a"], dz2b, name="ffn_down_dw")
        dhg, dhv, dwg, dwv, dbg, dbv = _ffn_bwd(*sv["ffn"], dz2b.reshape(bsz, s, d), full["ffn_w_down"][i],
                                                full["ffn_conv_w"][i] + depart(3 * i + 2), name="ffn_act_bwd")
        dhg, dhv = dhg.reshape(t, -1), dhv.reshape(t, -1)
        grads["ffn_conv_w"][i] = jnp.concatenate([dwg, dwv], axis=1)
        grads["ffn_conv_b"][i] = jnp.concatenate([dbg, dbv], axis=1)[0]
        grads["ffn_w_up"][i] = jnp.concatenate([_mm_tn(dhg, sv["x1"], name="ffn_up_dw_g"),
                                                _mm_tn(dhv, sv["x1"], name="ffn_up_dw_v")], axis=0)
        dz1, dz1b, dg1, db1 = _mm_ln_bwd([dhg, dhv], full["ffn_w_up"][i], dz2, alpha, sv["z1"],
                                         lng[0:1] + depart(3 * i + 1), name="ffn_up_dx_ln", w_rows_are_k=True)
        grads["ln_g"][i] = jnp.concatenate([dg1, dg2], axis=0)
        grads["ln_b"][i] = jnp.concatenate([db1, db2], axis=0)
        if i % 2 == 0:
            dq = _mm(dz1b, full["sc_w_out"][j], trans_w=True, name="sc_out_dx")
            grads["sc_w_out"][j] = _mm_tn(sv["q"], dz1b, name="sc_out_dw")
            dhm, dcw, dcb = _sc_bwd(sv["hm"].reshape(bsz, s, -1), dq.reshape(bsz, s, -1), full["sc_conv_w"][j],
                                    full["sc_conv_b"][j:j + 1], name="sc_mix_bwd")
            dhm = dhm.reshape(t, -1)
            grads["sc_conv_w"][j] = dcw
            grads["sc_conv_b"][j] = dcb[0]
            grads["sc_w_in"][j] = _mm_tn(sv["x0"], dhm, name="sc_in_dw")
            w_in = full["sc_w_in"][j]
        else:
            dq = _mm(dz1b, full["lru_w_out"][j], trans_w=True, name="lru_out_dx")
            grads["lru_w_out"][j] = _mm_tn(sv["q"], dz1b, name="lru_out_dw")
            dhm, dcw, dcb, dwgt, dbgt, dlam, sgb, srb = _lru_bwd(
                sv["hm"].reshape(bsz, s, -1), sv["hs"], dq.reshape(bsz, s, -1), full["lru_conv_w"][j],
                full["lru_w_gate"][j], full["lru_lambda"][j:j + 1], name="lru_mix_bwd")
            dhm = dhm.reshape(t, -1)
            grads["lru_conv_w"][j] = dcw
            grads["lru_conv_b"][j] = dcb[0]
            grads["lru_w_gate"][j] = dwgt
            grads["lru_b_gate"][j] = dbgt[:, 0, :]
            grads["lru_lambda"][j] = dlam[0]
            grads["lru_b_in"][j] = jnp.concatenate([sgb, srb], axis=1)[0]
            grads["lru_w_in"][j] = _mm_tn(dhm, sv["x0"], name="lru_in_dw")
            w_in = full["lru_w_in"][j]
        tok = depart(3 * i)
        if i > 0:
            dz2, dz2b, dg2, db2 = _mm_ln_bwd([dhm], w_in, dz1, alpha, saved[i - 1]["z2"], full["ln_g"][i - 1][1:2] + tok,
                                             name="mix_in_dx_ln", w_rows_are_k=i % 2 == 1)
        else:
            dx = _mm(dhm, w_in + tok[0, 0].astype(BF16), trans_w=True, resid=dz1, resid_scale=alpha, name="mix_in_dx")
    grad_x = dx.reshape(bsz, s, d)

    gsm = {k: jnp.stack(grads[k]) for k in SMALL + REPL}
    small_scatter = _exchange_start([_slabs(gsm[k], gsm[k].ndim - 1) for k in SMALL] + [gsm[k] for k in REPL],
                                    [False] * len(SMALL) + [True] * len(REPL), name="scatter_start_small")

    out = {}

    def own_slab(src):
        return lax.dynamic_slice_in_dim(src, me, 1, axis=0)

    stacks = {k: None for k in BIG}
    after = dx
    for g in reversed(range(stages)):
        srcs, lands = _exchange_wait(scatters[g], after, name=f"scatter_wait_{g}")
        for (k, l), src, land in zip(_stage_big(g), srcs, lands):
            n_l, c2 = w[k].shape[0], land.shape[-1]
            wk, mk, vk = (as_updated(k, arr[k]).reshape(n_l, -1, c2) for arr in (w, m, v))
            stacks[k] = _adamw(with_own(land, own_slab(src)).reshape(N_DEV, -1, c2), wk, mk, vk, l, stacks[k],
                               name=f"adamw_{k}_{l}")
            after = stacks[k][-1]
    for k in BIG:
        shp = as_updated(k, w[k]).shape
        out[k] = [as_updated(k, r.reshape(shp)) for r in stacks[k]]
    srcs, lands = _exchange_wait(small_scatter, after, name="scatter_wait_small")
    for n, k in enumerate(SMALL + REPL):
        own = srcs[n][None] if k in REPL else own_slab(srcs[n])
        c2 = w[k].shape[-1]
        res = _adamw(with_own(lands[n], own).reshape(N_DEV, -1, c2), w[k].reshape(1, -1, c2), m[k].reshape(1, -1, c2),
                     v[k].reshape(1, -1, c2), 0, None, name="adamw_" + k)
        out[k] = [r.reshape(w[k].shape) for r in res]

    return (loss, grad_x, *[out[k][0] for k in WEIGHTS], *[out[k][1] for k in WEIGHTS],
            *[out[k][2] for k in WEIGHTS], *[out[k][3] for k in WEIGHTS])


def kernel(x, sc_w_in, sc_conv_w, sc_conv_b, sc_w_out, lru_w_in, lru_b_in, lru_conv_w, lru_conv_b, lru_w_gate, lru_b_gate, lru_lambda, lru_w_out, ffn_w_up, ffn_conv_w, ffn_conv_b, ffn_w_down, ln_g, ln_b, loss_target, m_sc_w_in, m_sc_conv_w, m_sc_conv_b, m_sc_w_out, m_lru_w_in, m_lru_b_in, m_lru_conv_w, m_lru_conv_b, m_lru_w_gate, m_lru_b_gate, m_lru_lambda, m_lru_w_out, m_ffn_w_up, m_ffn_conv_w, m_ffn_conv_b, m_ffn_w_down, m_ln_g, m_ln_b, v_sc_w_in, v_sc_conv_w, v_sc_conv_b, v_sc_w_out, v_lru_w_in, v_lru_b_in, v_lru_conv_w, v_lru_conv_b, v_lru_w_gate, v_lru_b_gate, v_lru_lambda, v_lru_w_out, v_ffn_w_up, v_ffn_conv_w, v_ffn_conv_b, v_ffn_w_down, v_ln_g, v_ln_b):
    w = dict(sc_w_in=sc_w_in, sc_conv_w=sc_conv_w, sc_conv_b=sc_conv_b, sc_w_out=sc_w_out, lru_w_in=lru_w_in,
             lru_b_in=lru_b_in, lru_conv_w=lru_conv_w, lru_conv_b=lru_conv_b, lru_w_gate=lru_w_gate,
             lru_b_gate=lru_b_gate, lru_lambda=lru_lambda, lru_w_out=lru_w_out, ffn_w_up=ffn_w_up,
             ffn_conv_w=ffn_conv_w, ffn_conv_b=ffn_conv_b, ffn_w_down=ffn_w_down, ln_g=ln_g, ln_b=ln_b)
    m = dict(sc_w_in=m_sc_w_in, sc_conv_w=m_sc_conv_w, sc_conv_b=m_sc_conv_b, sc_w_out=m_sc_w_out, lru_w_in=m_lru_w_in,
             lru_b_in=m_lru_b_in, lru_conv_w=m_lru_conv_w, lru_conv_b=m_lru_conv_b, lru_w_gate=m_lru_w_gate,
             lru_b_gate=m_lru_b_gate, lru_lambda=m_lru_lambda, lru_w_out=m_lru_w_out, ffn_w_up=m_ffn_w_up,
             ffn_conv_w=m_ffn_conv_w, ffn_conv_b=m_ffn_conv_b, ffn_w_down=m_ffn_w_down, ln_g=m_ln_g, ln_b=m_ln_b)
    v = dict(sc_w_in=v_sc_w_in, sc_conv_w=v_sc_conv_w, sc_conv_b=v_sc_conv_b, sc_w_out=v_sc_w_out, lru_w_in=v_lru_w_in,
             lru_b_in=v_lru_b_in, lru_conv_w=v_lru_conv_w, lru_conv_b=v_lru_conv_b, lru_w_gate=v_lru_w_gate,
             lru_b_gate=v_lru_b_gate, lru_lambda=v_lru_lambda, lru_w_out=v_lru_w_out, ffn_w_up=v_ffn_w_up,
             ffn_conv_w=v_ffn_conv_w, ffn_conv_b=v_ffn_conv_b, ffn_w_down=v_ffn_w_down, ln_g=v_ln_g, ln_b=v_ln_b)
    return _step(x, loss_target, w, m, v)
```

```python
import functools
import math

import jax
import jax.numpy as jnp
from jax import lax
from jax.experimental import pallas as pl
from jax.experimental.pallas import tpu as pltpu

F32 = jnp.float32
BF16 = jnp.bfloat16

N_DEV = 8
MESH_AXES = ("x", "y", "c")
LANES = 128
SUBLANES = 8
VMEM_LIMIT = 56 * 1024 * 1024
MM_LHS_ELEMS = 3 * 1024 * 1024
MM_TN = 1536

LRU_C = 8.0
LN_EPS = 1e-5
ADAM_LR = 0.001
ADAM_B1 = 0.9
ADAM_B2 = 0.999
ADAM_EPS = 1e-08
ADAM_WD = 0.01
ADAM_STEP = 10
GELU_K = math.sqrt(2.0 / math.pi)
GELU_C = 0.044715


def _tile(n, target, align):
    if n <= target:
        return n
    t = (target // align) * align
    while t >= align:
        if n % t == 0:
            return t
        t -= align
    return n


def _params(sem):
    return pltpu.CompilerParams(dimension_semantics=sem, vmem_limit_bytes=VMEM_LIMIT)


def _rows(x):
    return lax.broadcasted_iota(jnp.int32, x.shape, 0)


def _shift_dn(x, k, fill=0.0):
    if k == 0:
        return x
    return jnp.where(_rows(x) >= k, pltpu.roll(x, k, 0), fill)


def _shift_up(x, k, fill=0.0):
    if k == 0:
        return x
    s = x.shape[0]
    return jnp.where(_rows(x) < s - k, pltpu.roll(x, s - k, 0), fill)


def _conv_fwd(x, w, b):
    kw = w.shape[0]
    y = _shift_dn(x, kw - 1) * w[0:1, :] + b
    for k in range(1, kw):
        y = y + _shift_dn(x, kw - 1 - k) * w[k:k + 1, :]
    return y


def _conv_bwd(dy, x, w):
    kw = w.shape[0]
    ahead = [_shift_up(dy, j) for j in range(kw)]
    dx = ahead[kw - 1] * w[0:1, :]
    for k in range(1, kw):
        dx = dx + ahead[kw - 1 - k] * w[k:k + 1, :]
    return dx, [_colsum(ahead[kw - 1 - k] * x) for k in range(kw)]


def _accumulate(first, items, cols=slice(None)):
    flat = []
    for ref, val in items:
        if isinstance(val, list):
            flat += [(ref, (slice(k, k + 1), cols), row) for k, row in enumerate(val)]
        else:
            flat.append((ref, Ellipsis, val))

    @pl.when(first)
    def _():
        for ref, idx, val in flat:
            ref[idx] = val

    @pl.when(jnp.logical_not(first))
    def _():
        for ref, idx, val in flat:
            ref[idx] += val


def _colsum(x):
    return jnp.sum(x, axis=0, keepdims=True)


def _sigmoid(x):
    return 1.0 / (1.0 + jnp.exp(-x))


def _log1p(x):
    u = 1.0 + x
    return jnp.where(u == 1.0, x, jnp.log(u) * (x / (u - 1.0)))


def _softplus(x):
    return jnp.maximum(x, 0.0) + _log1p(jnp.exp(-jnp.abs(x)))


def _expm1(x, ex):
    poly = x * (1.0 + x * (0.5 + x * (1.0 / 6.0 + x * (1.0 / 24.0 + x * (1.0 / 120.0 + x * (1.0 / 720.0))))))
    return jnp.where(jnp.abs(x) < 0.25, poly, ex - 1.0)


def _gelu(x):
    t = jnp.tanh(GELU_K * (x + GELU_C * x * x * x))
    return 0.5 * x * (1.0 + t)


def _gelu_and_grad(x):
    x2 = x * x
    t = jnp.tanh(GELU_K * (x + GELU_C * x * x2))
    g = 0.5 * x * (1.0 + t)
    dg = 0.5 * (1.0 + t) + 0.5 * x * (1.0 - t * t) * (GELU_K * (1.0 + 3.0 * GELU_C * x2))
    return g, dg


def _scan_fwd(a, b):
    s = a.shape[0]
    k = 1
    while k < s:
        last = 2 * k >= s
        if k % SUBLANES:
            b = a * _shift_dn(b, k) + b
            if not last:
                a = a * _shift_dn(a, k, 1.0)
        else:
            b = jnp.concatenate([b[:k], a[k:] * b[:s - k] + b[k:]], axis=0)
            if not last:
                a = jnp.concatenate([a[:k], a[k:] * a[:s - k]], axis=0)
        k *= 2
    return b


def _scan_rev(c, v):
    s = c.shape[0]
    k = 1
    while k < s:
        last = 2 * k >= s
        if k % SUBLANES:
            v = c * _shift_up(v, k) + v
            if not last:
                c = c * _shift_up(c, k, 1.0)
        else:
            v = jnp.concatenate([c[:s - k] * v[k:] + v[:s - k], v[s - k:]], axis=0)
            if not last:
                c = jnp.concatenate([c[:s - k] * c[k:], c[s - k:]], axis=0)
        k *= 2
    return v


def _mm(a, w, *, name, trans_w=False, bias=None, resid=None, resid_scale=1.0):
    m, k = a.shape
    n = w.shape[0] if trans_w else w.shape[1]
    tm = _tile(m, min(1024, max(256, MM_LHS_ELEMS // k)), SUBLANES)
    tn = _tile(n, MM_TN, LANES)
    has_bias = bias is not None
    has_resid = resid is not None

    def body(*refs):
        a_ref, w_ref = refs[0], refs[1]
        pos = 2
        b_ref = r_ref = None
        if has_bias:
            b_ref = refs[pos]
            pos += 1
        if has_resid:
            r_ref = refs[pos]
            pos += 1
        o_ref = refs[pos]

        cols = pl.ds(pl.multiple_of(pl.program_id(1) * tn, LANES), tn)
        if trans_w:
            acc = lax.dot_general(a_ref[...], w_ref[cols, :], (((1,), (1,)), ((), ())), preferred_element_type=F32)
        else:
            acc = jnp.dot(a_ref[...], w_ref[:, cols], preferred_element_type=F32)
        if has_bias:
            acc = acc + b_ref[...]
        if has_resid:
            acc = acc + resid_scale * r_ref[...]
        o_ref[...] = acc

    in_specs = [pl.BlockSpec((tm, k), lambda i, j: (i, 0)),
                pl.BlockSpec(w.shape, lambda i, j: (0, 0), pipeline_mode=pl.Buffered(1))]
    args = [a, w]
    if has_bias:
        in_specs.append(pl.BlockSpec((1, tn), lambda i, j: (0, j)))
        args.append(bias)
    if has_resid:
        in_specs.append(pl.BlockSpec((tm, tn), lambda i, j: (i, j)))
        args.append(resid)
    return pl.pallas_call(
        body, name=name, grid=(m // tm, n // tn), in_specs=in_specs,
        out_specs=pl.BlockSpec((tm, tn), lambda i, j: (i, j)),
        out_shape=jax.ShapeDtypeStruct((m, n), F32),
        compiler_params=_params(("parallel", "arbitrary")),
    )(*args)


def _ln(z, g, b):
    mu = jnp.mean(z, axis=-1, keepdims=True)
    zc = z - mu
    var = jnp.mean(zc * zc, axis=-1, keepdims=True)
    return zc * lax.rsqrt(var + LN_EPS) * g + b


def _mm_ln(a, w, resid, alpha, g, b, *, name, resid_ln=None, tm=512):
    m, k = a.shape
    d = w.shape[1]
    tm = _tile(m, tm, SUBLANES)
    n_extra = 0 if resid_ln is None else 2

    def body(a_ref, w_ref, r_ref, g_ref, b_ref, *rest):
        z_ref, obf_ref = rest[n_extra:]
        x = r_ref[...]
        if resid_ln is not None:
            x = _ln(x, rest[0][...], rest[1][...])
        z = alpha * x + jnp.dot(a_ref[...], w_ref[...], preferred_element_type=F32)
        z_ref[...] = z
        obf_ref[...] = _ln(z, g_ref[...], b_ref[...]).astype(BF16)

    row = pl.BlockSpec((tm, d), lambda i: (i, 0))
    vec = pl.BlockSpec((1, d), lambda i: (0, 0))
    return pl.pallas_call(
        body, name=name, grid=(m // tm,),
        in_specs=[pl.BlockSpec((tm, k), lambda i: (i, 0)),
                  pl.BlockSpec((k, d), lambda i: (0, 0), pipeline_mode=pl.Buffered(1)), row, vec, vec]
        + [vec] * n_extra,
        out_specs=[row, row],
        out_shape=[jax.ShapeDtypeStruct((m, d), F32), jax.ShapeDtypeStruct((m, d), BF16)],
        compiler_params=_params(("parallel",)),
    )(a, w, resid, g, b, *(resid_ln or ()))


def _ln_bwd_math(do, z, g):
    mu = jnp.mean(z, axis=-1, keepdims=True)
    zc = z - mu
    var = jnp.mean(zc * zc, axis=-1, keepdims=True)
    rstd = lax.rsqrt(var + LN_EPS)
    xhat = zc * rstd
    dxh = do * g
    m1 = jnp.mean(dxh, axis=-1, keepdims=True)
    m2 = jnp.mean(dxh * xhat, axis=-1, keepdims=True)
    return rstd * (dxh - m1 - xhat * m2), _colsum(do * xhat), _colsum(do)


def _mm_ln_bwd(parts, w, resid, resid_scale, z, g, *, name, w_rows_are_k=False):
    t, kp = parts[0].shape
    k, d = w.shape if w_rows_are_k else w.shape[::-1]
    n = len(parts)
    tm = _tile(t, min(512, max(256, MM_LHS_ELEMS // k)), SUBLANES)

    def body(*refs):
        a_refs = refs[:n]
        w_ref, r_ref, z_ref, g_ref, dz_ref, dzbf_ref, dg_ref, db_ref = refs[n:]

        @pl.when(pl.program_id(0) == 0)
        def _():
            dg_ref[...] = jnp.zeros_like(dg_ref)
            db_ref[...] = jnp.zeros_like(db_ref)

        dx = resid_scale * r_ref[...]
        for p, a_ref in enumerate(a_refs):
            if w_rows_are_k:
                dx = dx + jnp.dot(a_ref[...], w_ref[p * kp:(p + 1) * kp, :], preferred_element_type=F32)
            else:
                dx = dx + lax.dot_general(a_ref[...], w_ref[:, p * kp:(p + 1) * kp], (((1,), (1,)), ((), ())),
                                          preferred_element_type=F32)
        dz, dg, db = _ln_bwd_math(dx, z_ref[...], g_ref[...])
        dz_ref[...] = dz
        dzbf_ref[...] = dz.astype(BF16)
        dg_ref[...] += dg
        db_ref[...] += db

    row = pl.BlockSpec((tm, d), lambda i: (i, 0))
    vec = pl.BlockSpec((1, d), lambda i: (0, 0))
    return pl.pallas_call(
        body, name=name, grid=(t // tm,),
        in_specs=[pl.BlockSpec((tm, kp), lambda i: (i, 0))] * n
        + [pl.BlockSpec(w.shape, lambda i: (0, 0), pipeline_mode=pl.Buffered(1)), row, row, vec],
        out_specs=[row, row, vec, vec],
        out_shape=[jax.ShapeDtypeStruct((t, d), F32), jax.ShapeDtypeStruct((t, d), BF16),
                   jax.ShapeDtypeStruct((1, d), F32), jax.ShapeDtypeStruct((1, d), F32)],
        compiler_params=_params(("arbitrary",)),
    )(*parts, w, resid, z, g)


def _mm_tn(a, b, *, name, below=None, tm=1408, tn=1536, tk=1024):
    t, m = a.shape
    n = b.shape[1]
    tm = _tile(m, tm, LANES)
    tn = _tile(n, tn, LANES)
    tk = _tile(t, tk, SUBLANES)
    last = t // tk - 1
    rows, earlier = (m, None) if below is None else below
    skip = (rows - m) // tm if earlier is not None else 0

    def body(a_ref, b_ref, *rest):
        o_ref, acc = rest[-2:]

        @pl.when(pl.program_id(2) == 0)
        def _():
            acc[...] = jnp.zeros_like(acc)

        acc[...] += lax.dot_general(a_ref[...], b_ref[...], (((0,), (0,)), ((), ())), preferred_element_type=F32)

        @pl.when(pl.program_id(2) == last)
        def _():
            o_ref[...] = acc[...].astype(BF16)

    in_specs = [pl.BlockSpec((tk, tm), lambda i, j, l: (l, i)), pl.BlockSpec((tk, tn), lambda i, j, l: (l, j))]
    return pl.pallas_call(
        body, name=name, grid=(m // tm, n // tn, t // tk),
        in_specs=in_specs + ([] if earlier is None else [pl.BlockSpec(memory_space=pl.ANY)]),
        out_specs=pl.BlockSpec((tm, tn), lambda i, j, l: (i + skip, j)),
        out_shape=jax.ShapeDtypeStruct((rows, n), BF16),
        input_output_aliases={} if earlier is None else {2: 0},
        scratch_shapes=[pltpu.VMEM((tm, tn), F32)],
        compiler_params=_params(("parallel", "parallel", "arbitrary")),
    )(a, b, *(() if earlier is None else (earlier,)))


def _ln_bwd(dout, z, g, *, name, tm=512):
    t, d = z.shape
    tm = _tile(t, tm, SUBLANES)

    def body(do_ref, z_ref, g_ref, dz_ref, dzbf_ref, dg_ref, db_ref):
        @pl.when(pl.program_id(0) == 0)
        def _():
            dg_ref[...] = jnp.zeros_like(dg_ref)
            db_ref[...] = jnp.zeros_like(db_ref)

        dz, dg, db = _ln_bwd_math(do_ref[...], z_ref[...], g_ref[...])
        dz_ref[...] = dz
        dzbf_ref[...] = dz.astype(BF16)
        dg_ref[...] += dg
        db_ref[...] += db

    row = pl.BlockSpec((tm, d), lambda i: (i, 0))
    vec = pl.BlockSpec((1, d), lambda i: (0, 0))
    return pl.pallas_call(
        body, name=name, grid=(t // tm,), in_specs=[row, row, vec], out_specs=[row, row, vec, vec],
        out_shape=[jax.ShapeDtypeStruct((t, d), F32), jax.ShapeDtypeStruct((t, d), BF16),
                   jax.ShapeDtypeStruct((1, d), F32), jax.ShapeDtypeStruct((1, d), F32)],
        compiler_params=_params(("arbitrary",)),
    )(dout, z, g)


def _loss_head(z, g, b, target, *, name, tm=512):
    t, d = z.shape
    tm = _tile(t, tm, SUBLANES)

    def body(z_ref, g_ref, b_ref, t_ref, s_ref, dy_ref):
        @pl.when(pl.program_id(0) == 0)
        def _():
            s_ref[...] = jnp.zeros_like(s_ref)

        e = _ln(z_ref[...], g_ref[...], b_ref[...]) - t_ref[...]
        dy_ref[...] = e * (1.0 / d)
        s_ref[...] += jnp.sum(_colsum(e * e), axis=-1, keepdims=True)

    row = pl.BlockSpec((tm, d), lambda i: (i, 0))
    vec = pl.BlockSpec((1, d), lambda i: (0, 0))
    return pl.pallas_call(
        body, name=name, grid=(t // tm,), in_specs=[row, vec, vec, row],
        out_specs=[pl.BlockSpec((1, LANES), lambda i: (0, 0)), row],
        out_shape=[jax.ShapeDtypeStruct((1, LANES), F32), jax.ShapeDtypeStruct((t, d), F32)],
        compiler_params=_params(("arbitrary",)),
    )(z, g, b, target)


def _own(c, b, *_):
    return c, b


def _ahead(nc, bsz):
    def at(c, b, part):
        b2 = b + jnp.minimum(part, 1)
        return jnp.minimum(c + b2 // bsz, nc - 1), b2 % bsz
    return at


def _strip(s, tc, off, at=_own):
    def index(*ids):
        c, b = at(*ids)
        return b, 0, off + c
    return pl.BlockSpec((None, s, tc), index)


def _cvec(kw, tc, off, at=_own):
    def index(*ids):
        return 0, off + at(*ids)[0]
    return pl.BlockSpec((kw, tc), index)


def _acc(kw, tc):
    return pl.BlockSpec((kw, tc), lambda c, b, *_: (0, c))


def _sc_fwd(h, cw, cb, *, name, tc=256):
    bsz, s, d3 = h.shape
    d = d3 // 3
    tc = _tile(d, tc, LANES)
    nc = d // tc

    def body(gb_ref, gc_ref, v_ref, w_ref, b_ref, q_ref):
        u = _conv_fwd(gc_ref[...] * v_ref[...], w_ref[...], b_ref[...])
        q_ref[...] = (gb_ref[...] * u).astype(BF16)

    return pl.pallas_call(
        body, name=name, grid=(nc, bsz),
        in_specs=[_strip(s, tc, 0), _strip(s, tc, nc), _strip(s, tc, 2 * nc), _cvec(cw.shape[0], tc, 0), _cvec(1, tc, 0)],
        out_specs=_strip(s, tc, 0),
        out_shape=jax.ShapeDtypeStruct((bsz, s, d), BF16),
        compiler_params=_params(("parallel", "parallel")),
    )(h, h, h, cw, cb)


def _sc_bwd(h, dq, cw, cb, *, name, tc=256):
    bsz, s, d3 = h.shape
    d = d3 // 3
    kw = cw.shape[0]
    tc = _tile(d, tc, LANES)
    nc = d // tc

    def body(gb_ref, gc_ref, v_ref, dq_ref, w_ref, b_ref, dh_ref, dw_ref, db_ref, parts):
        b_id, part = pl.program_id(1), pl.program_id(2)

        @pl.when(part == 0)
        def _():
            gb, gc, v, dq_, w = gb_ref[...], gc_ref[...], v_ref[...], dq_ref[...], w_ref[...]
            p = gc * v
            u = _conv_fwd(p, w, b_ref[...])
            du = dq_ * gb
            dp, dw_rows = _conv_bwd(du, p, w)
            parts[0] = (dq_ * u).astype(BF16)
            parts[1] = (dp * v).astype(BF16)
            parts[2] = (dp * gc).astype(BF16)
            _accumulate(b_id == 0, [(dw_ref, dw_rows), (db_ref, _colsum(du))])

        dh_ref[...] = parts[part]

    at = _ahead(nc, bsz)
    return pl.pallas_call(
        body, name=name, grid=(nc, bsz, 3),
        in_specs=[_strip(s, tc, 0, at), _strip(s, tc, nc, at), _strip(s, tc, 2 * nc, at), _strip(s, tc, 0, at),
                  _cvec(kw, tc, 0, at), _cvec(1, tc, 0, at)],
        out_specs=[pl.BlockSpec((None, s, tc), lambda c, b, p: (b, 0, p * nc + c)), _acc(kw, tc), _acc(1, tc)],
        out_shape=[jax.ShapeDtypeStruct((bsz, s, d3), BF16), jax.ShapeDtypeStruct((kw, d), F32),
                   jax.ShapeDtypeStruct((1, d), F32)],
        scratch_shapes=[pltpu.VMEM((3, s, tc), BF16)],
        compiler_params=_params(("parallel", "arbitrary", "arbitrary")),
    )(h, h, h, dq, cw, cb)


def _ffn_specs(s, tc, nc, kw):
    strip = pl.BlockSpec((None, s, tc), lambda b, c: (b, 0, c))
    halves = [pl.BlockSpec((kw, tc), lambda b, c: (0, c)), pl.BlockSpec((kw, tc), lambda b, c: (0, nc + c)),
              pl.BlockSpec((1, tc), lambda b, c: (0, c)), pl.BlockSpec((1, tc), lambda b, c: (0, nc + c))]
    return strip, halves


def _ffn_fwd(x, w_up, cw, cb, *, name, tc=256):
    bsz, s, d = x.shape
    f = w_up.shape[0] // 2
    kw = cw.shape[0]
    tc = _tile(f, tc, LANES)
    nc = f // tc
    nt = (((1,), (1,)), ((), ()))

    def body(x_ref, w_ref, wg_ref, wv_ref, bg_ref, bv_ref, hg_ref, hv_ref, g_ref, v_ref, a_ref):
        c0 = pl.multiple_of(pl.program_id(1) * tc, LANES)
        xs = x_ref[...]
        hg = lax.dot_general(xs, w_ref[pl.ds(c0, tc), :], nt, preferred_element_type=F32)
        hv = lax.dot_general(xs, w_ref[pl.ds(f + c0, tc), :], nt, preferred_element_type=F32)
        hg_ref[...] = hg
        hv_ref[...] = hv
        g = _conv_fwd(hg, wg_ref[...], bg_ref[...])
        v = _conv_fwd(hv, wv_ref[...], bv_ref[...])
        g_ref[...] = g
        v_ref[...] = v
        a_ref[...] = (g * _sigmoid(g) * v).astype(BF16)

    strip, halves = _ffn_specs(s, tc, nc, kw)
    return pl.pallas_call(
        body, name=name, grid=(bsz, nc),
        in_specs=[pl.BlockSpec((None, s, d), lambda b, c: (b, 0, 0)),
                  pl.BlockSpec(w_up.shape, lambda b, c: (0, 0), pipeline_mode=pl.Buffered(1))] + halves,
        out_specs=[strip] * 5,
        out_shape=[jax.ShapeDtypeStruct((bsz, s, f), F32)] * 4 + [jax.ShapeDtypeStruct((bsz, s, f), BF16)],
        compiler_params=_params(("parallel", "arbitrary")),
    )(x, w_up, cw, cw, cb, cb)


def _ffn_bwd(hg, hv, g, v, dz, w_down, cw, *, name, tc=256):
    bsz, s, f = hg.shape
    d = dz.shape[2]
    kw = cw.shape[0]
    tc = _tile(f, tc, LANES)
    nc = f // tc

    def body(hg_ref, hv_ref, g_ref, v_ref, dz_ref, wd_ref, wg_ref, wv_ref,
             dhg_ref, dhv_ref, dwg_ref, dwv_ref, dbg_ref, dbv_ref):
        c0 = pl.multiple_of(pl.program_id(1) * tc, LANES)
        cols = pl.ds(c0, tc)
        da = lax.dot_general(dz_ref[...], wd_ref[cols, :], (((1,), (1,)), ((), ())), preferred_element_type=F32)
        g_ = g_ref[...]
        sg = _sigmoid(g_)
        dv = da * (g_ * sg)
        dg = da * v_ref[...] * (sg * (1.0 + g_ * (1.0 - sg)))
        dhg, dwg_rows = _conv_bwd(dg, hg_ref[...], wg_ref[...])
        dhv, dwv_rows = _conv_bwd(dv, hv_ref[...], wv_ref[...])
        dhg_ref[...] = dhg.astype(BF16)
        dhv_ref[...] = dhv.astype(BF16)
        _accumulate(pl.program_id(0) == 0, [(dwg_ref, dwg_rows), (dwv_ref, dwv_rows),
                                            (dbg_ref, [_colsum(dg)]), (dbv_ref, [_colsum(dv)])], cols)

    strip, halves = _ffn_specs(s, tc, nc, kw)
    whole = lambda r: pl.BlockSpec((r, f), lambda b, c: (0, 0))
    return pl.pallas_call(
        body, name=name, grid=(bsz, nc),
        in_specs=[strip] * 4 + [pl.BlockSpec((None, s, d), lambda b, c: (b, 0, 0)),
                                pl.BlockSpec(w_down.shape, lambda b, c: (0, 0), pipeline_mode=pl.Buffered(1))]
        + halves[:2],
        out_specs=[strip, strip, whole(kw), whole(kw), whole(1), whole(1)],
        out_shape=[jax.ShapeDtypeStruct((bsz, s, f), BF16), jax.ShapeDtypeStruct((bsz, s, f), BF16),
                   jax.ShapeDtypeStruct((kw, f), F32), jax.ShapeDtypeStruct((kw, f), F32),
                   jax.ShapeDtypeStruct((1, f), F32), jax.ShapeDtypeStruct((1, f), F32)],
        compiler_params=_params(("arbitrary", "arbitrary")),
    )(hg, hv, g, v, dz, w_down, cw, cw)


def _lru_gates(r, cw, cb, wg, bg, lam):
    blk = r.shape[1]
    xr = _conv_fwd(r, cw, cb)
    gates = jnp.dot(xr.astype(BF16), wg, preferred_element_type=F32) + bg
    rg = _sigmoid(gates[:, :blk])
    ig = _sigmoid(gates[:, blk:])
    sp = _softplus(-lam)
    la = (-LRU_C * sp) * rg
    a = jnp.exp(la)
    mult = jnp.sqrt(-_expm1(2.0 * la, a * a))
    return xr, rg, ig, sp, a, mult


def _lru_fwd(h, cw, cb, wg, bg, lam, *, name):
    bsz, s, r2 = h.shape
    heads, blk = wg.shape[0], wg.shape[1]
    kw = cw.shape[0]

    def body(g_ref, r_ref, cw_ref, cb_ref, wg_ref, bg_ref, lam_ref, y_ref, sv_ref):
        xr, rg, ig, _, a, mult = _lru_gates(r_ref[...], cw_ref[...], cb_ref[...], wg_ref[...], bg_ref[...], lam_ref[...])
        hs = _scan_fwd(a, mult * (ig * xr))
        for n, val in enumerate((hs, xr, rg, ig, a, mult)):
            sv_ref[n] = val
        y_ref[...] = (hs * _gelu(g_ref[...])).astype(BF16)

    per_head = lambda hd, b: (hd, 0, 0)
    return pl.pallas_call(
        body, name=name, grid=(heads, bsz),
        in_specs=[_strip(s, blk, 0), _strip(s, blk, heads), _cvec(kw, blk, 0), _cvec(1, blk, 0),
                  pl.BlockSpec((None, blk, 2 * blk), per_head), pl.BlockSpec((None, 1, 2 * blk), per_head),
                  _cvec(1, blk, 0)],
        out_specs=[_strip(s, blk, 0), pl.BlockSpec((6, None, s, blk), lambda hd, b: (0, b, 0, hd))],
        out_shape=[jax.ShapeDtypeStruct((bsz, s, r2 // 2), BF16), jax.ShapeDtypeStruct((6, bsz, s, r2 // 2), F32)],
        compiler_params=_params(("parallel", "parallel")),
    )(h, h, cw, cb, wg, bg, lam)


def _lru_bwd(h, sv, dy, cw, wg, lam, *, name):
    bsz, s, r2 = h.shape
    rw = r2 // 2
    heads, blk = wg.shape[0], wg.shape[1]
    kw = cw.shape[0]

    def body(g_ref, r_ref, cw_ref, wg_ref, lam_ref, sv_ref, dy_ref,
             dh_ref, dcw_ref, dcb_ref, dwg_ref, dbg_ref, dlam_ref, sg_ref, sr_ref, parts):
        b_id, part = pl.program_id(1), pl.program_id(2)

        @pl.when(part == 0)
        def _():
            r, cw_, wg_, lam_ = r_ref[...], cw_ref[...], wg_ref[...], lam_ref[...]
            hs_, xr, rg, ig, a, mult = (sv_ref[n] for n in range(6))
            sp = _softplus(-lam_)
            dy_ = dy_ref[...]
            gel, dgel = _gelu_and_grad(g_ref[...])
            dg = dy_ * hs_ * dgel
            lmb = _scan_rev(_shift_up(a, 1, 1.0), dy_ * gel)
            da = lmb * _shift_dn(hs_, 1)
            dmult = lmb * (ig * xr)
            dig = lmb * (mult * xr)
            dxr = lmb * (mult * ig)
            dla = da * a - dmult * (a * a / mult)
            drg = dla * (-LRU_C * sp)
            dsp = _colsum(dla * rg) * (-LRU_C)
            dlam = -dsp * _sigmoid(-lam_)
            dgates = jnp.concatenate([drg * (rg * (1.0 - rg)), dig * (ig * (1.0 - ig))], axis=1)
            dgates_bf = dgates.astype(BF16)
            dwg = lax.dot_general(xr.astype(BF16), dgates_bf, (((0,), (0,)), ((), ())), preferred_element_type=F32)
            dxr = dxr + lax.dot_general(dgates_bf, wg_, (((1,), (1,)), ((), ())), preferred_element_type=F32)
            dr, dcw_rows = _conv_bwd(dxr, r, cw_)
            parts[0] = dg.astype(BF16)
            parts[1] = dr.astype(BF16)
            _accumulate(b_id == 0, [(dcw_ref, dcw_rows), (dcb_ref, _colsum(dxr)), (dwg_ref, dwg),
                                    (dbg_ref, _colsum(dgates)), (dlam_ref, dlam), (sg_ref, _colsum(dg)),
                                    (sr_ref, _colsum(dr))])

        dh_ref[...] = parts[part]

    at = _ahead(heads, bsz)

    def saved(*ids):
        hd, b = at(*ids)
        return 0, b, 0, hd

    vec = pl.BlockSpec((1, blk), lambda hd, b, p: (0, hd))
    return pl.pallas_call(
        body, name=name, grid=(heads, bsz, 2),
        in_specs=[_strip(s, blk, 0, at), _strip(s, blk, heads, at), _cvec(kw, blk, 0, at),
                  pl.BlockSpec((None, blk, 2 * blk), lambda *ids: (at(*ids)[0], 0, 0)), _cvec(1, blk, 0, at),
                  pl.BlockSpec((6, None, s, blk), saved), _strip(s, blk, 0, at)],
        out_specs=[pl.BlockSpec((None, s, blk), lambda hd, b, p: (b, 0, p * heads + hd)),
                   pl.BlockSpec((kw, blk), lambda hd, b, p: (0, hd)), vec,
                   pl.BlockSpec((None, blk, 2 * blk), lambda hd, b, p: (hd, 0, 0)),
                   pl.BlockSpec((None, 1, 2 * blk), lambda hd, b, p: (hd, 0, 0)), vec, vec, vec],
        out_shape=[jax.ShapeDtypeStruct((bsz, s, r2), BF16), jax.ShapeDtypeStruct((kw, rw), F32),
                   jax.ShapeDtypeStruct((1, rw), F32), jax.ShapeDtypeStruct((heads, blk, 2 * blk), F32),
                   jax.ShapeDtypeStruct((heads, 1, 2 * blk), F32), jax.ShapeDtypeStruct((1, rw), F32),
                   jax.ShapeDtypeStruct((1, rw), F32), jax.ShapeDtypeStruct((1, rw), F32)],
        scratch_shapes=[pltpu.VMEM((2, s, blk), BF16)],
        compiler_params=_params(("parallel", "arbitrary", "arbitrary")),
    )(h, h, cw, wg, lam, sv, dy)


HBM_SPEC = pl.BlockSpec(memory_space=pltpu.HBM)
SEM_SPEC = pl.BlockSpec(memory_space=pltpu.SEMAPHORE)
EFFECT = pltpu.SideEffectType.DATAFLOW_SIDE_EFFECTING


def _peer_copies(srcs, lands, gather, send_sem, recv_sem):
    x, y, c = (lax.axis_index(ax) for ax in MESH_AXES)
    me = 4 * x + 2 * y + c
    copies = []
    for i in range(len(srcs)):
        for d in range(1, N_DEV):
            px = 1 - x if d & 4 else x
            py = 1 - y if d & 2 else y
            pc = 1 - c if d & 1 else c
            src = srcs[i] if gather[i] else srcs[i].at[4 * px + 2 * py + pc]
            k = i * (N_DEV - 1) + d - 1
            copies.append(pltpu.make_async_remote_copy(
                src_ref=src, dst_ref=lands[i].at[me], send_sem=send_sem.at[k], recv_sem=recv_sem.at[k],
                device_id=(px, py, pc), device_id_type=pl.DeviceIdType.MESH))
    return copies


def _exchange_start(arrs, gather, *, name):
    n = len(arrs)
    lands = [lax.empty((N_DEV,) + tuple(a.shape if g else a.shape[1:]), a.dtype) for a, g in zip(arrs, gather)]

    def body(*refs):
        srcs, land_refs = refs[:n], refs[n:2 * n]
        send_sem, recv_sem = refs[2 * n], refs[2 * n + 1]
        token = refs[-1]
        for cp in _peer_copies(srcs, land_refs, gather, send_sem, recv_sem):
            cp.start()
        token[...] = jnp.zeros_like(token)

    sems = pltpu.SemaphoreType.DMA((n * (N_DEV - 1),))
    thru = [pltpu.HBM(a.shape, a.dtype) for a in arrs + lands]
    out = pl.pallas_call(
        body, name=name, in_specs=[HBM_SPEC] * (2 * n),
        out_shape=(sems, sems, *thru, jax.ShapeDtypeStruct((SUBLANES, LANES), F32)),
        out_specs=(SEM_SPEC, SEM_SPEC, *([HBM_SPEC] * (2 * n)), pl.BlockSpec(memory_space=pltpu.VMEM)),
        input_output_aliases={i: 2 + i for i in range(2 * n)},
        compiler_params=pltpu.CompilerParams(has_side_effects=EFFECT),
    )(*[pltpu.with_memory_space_constraint(a, pltpu.HBM) for a in arrs + lands])
    return {"send_sem": out[0], "recv_sem": out[1], "srcs": list(out[2:2 + n]), "lands": list(out[2 + n:2 + 2 * n]),
            "token": out[-1], "gather": list(gather)}


def _exchange_wait(handle, after, *, name):
    srcs, lands, gather = handle["srcs"], handle["lands"], handle["gather"]
    n = len(srcs)

    def body(*refs):
        src_refs, land_refs = refs[:n], refs[n:2 * n]
        send_sem, recv_sem = refs[2 * n], refs[2 * n + 1]
        for cp in _peer_copies(src_refs, land_refs, gather, send_sem, recv_sem):
            cp.wait_send()
            cp.wait_recv()

    out = pl.pallas_call(
        body, name=name,
        in_specs=[HBM_SPEC] * (2 * n) + [SEM_SPEC, SEM_SPEC, pl.BlockSpec(memory_space=pl.ANY)],
        out_shape=tuple(pltpu.HBM(a.shape, a.dtype) for a in srcs + lands), out_specs=tuple([HBM_SPEC] * (2 * n)),
        input_output_aliases={i: i for i in range(2 * n)},
        compiler_params=pltpu.CompilerParams(has_side_effects=EFFECT),
    )(*srcs, *lands, handle["send_sem"], handle["recv_sem"], after)
    return list(out[:n]), list(out[n:])


def _layers_bf16(stacks, *, name):
    counts = [a.shape[0] for a in stacks]

    def body(*refs):
        outs = iter(refs[len(stacks):])
        for i_ref, n_layers in zip(refs, counts):
            for layer in range(n_layers):
                next(outs)[...] = i_ref[layer].astype(BF16)

    flat = pl.pallas_call(
        body, name=name,
        out_shape=[jax.ShapeDtypeStruct(a.shape[1:], BF16) for a in stacks for _ in range(a.shape[0])],
        compiler_params=pltpu.CompilerParams(vmem_limit_bytes=VMEM_LIMIT),
    )(*stacks)
    split, pos = [], 0
    for n_layers in counts:
        split.append(list(flat[pos:pos + n_layers]))
        pos += n_layers
    return split


def _adamw(parts, w, m, v, layer, so_far, *, name, tr=256):
    n_layers, r, c = w.shape
    tr = _tile(r, tr, SUBLANES)
    bc1 = 1.0 / (1.0 - ADAM_B1 ** ADAM_STEP)
    bc2 = 1.0 / (1.0 - ADAM_B2 ** ADAM_STEP)
    if so_far is None:
        so_far = [lax.empty(w.shape, F32) for _ in range(4)]

    def body(p_ref, w_ref, m_ref, v_ref, *rest):
        g_ref, d_ref, mo_ref, vo_ref = rest[4:]
        g = p_ref[0].astype(F32)
        for s in range(1, N_DEV):
            g = g + p_ref[s].astype(F32)
        m_new = ADAM_B1 * m_ref[...] + (1.0 - ADAM_B1) * g
        v_new = ADAM_B2 * v_ref[...] + (1.0 - ADAM_B2) * (g * g)
        g_ref[...] = g
        mo_ref[...] = m_new
        vo_ref[...] = v_new
        d_ref[...] = -ADAM_LR * ((m_new * bc1) / (jnp.sqrt(v_new * bc2) + ADAM_EPS) + ADAM_WD * w_ref[...])

    blk = pl.BlockSpec((None, tr, c), lambda i: (layer, i, 0))
    return pl.pallas_call(
        body, name=name, grid=(r // tr,),
        in_specs=[pl.BlockSpec((N_DEV, tr, c), lambda i: (0, i, 0)), blk, blk, blk]
        + [pl.BlockSpec(memory_space=pl.ANY)] * 4,
        out_specs=[blk] * 4, out_shape=[jax.ShapeDtypeStruct(w.shape, F32)] * 4,
        input_output_aliases={4 + o: o for o in range(4)},
        compiler_params=_params(("parallel",)),
    )(parts, w, m, v, *so_far)


def _whole(slabs, axis):
    x = jnp.moveaxis(slabs, 0, axis)
    shp = x.shape
    return x.reshape(shp[:axis] + (shp[axis] * shp[axis + 1],) + shp[axis + 2:])


def _slabs(whole, axis):
    shp = whole.shape
    x = whole.reshape(shp[:axis] + (N_DEV, shp[axis] // N_DEV) + shp[axis + 1:])
    return jnp.moveaxis(x, axis, 0)


BIG = {"sc_w_in": 2, "sc_w_out": 1, "lru_w_in": 2, "lru_w_gate": 3, "lru_w_out": 1, "ffn_w_up": 2, "ffn_w_down": 1}
SWAPPED = ("ffn_w_up", "lru_w_in")
TRANSPOSED = SWAPPED
SMALL = ["sc_conv_w", "lru_b_in", "lru_conv_w", "lru_conv_b", "lru_b_gate", "lru_lambda", "ffn_conv_w", "ln_g", "ln_b"]
REPL = ["sc_conv_b", "ffn_conv_b"]
WEIGHTS = ["sc_w_in", "sc_conv_w", "sc_conv_b", "sc_w_out", "lru_w_in", "lru_b_in", "lru_conv_w", "lru_conv_b",
           "lru_w_gate", "lru_b_gate", "lru_lambda", "lru_w_out", "ffn_w_up", "ffn_conv_w", "ffn_conv_b", "ffn_w_down",
           "ln_g", "ln_b"]


STAGES_PER_LAYER = 3


def _stage_big(g):
    i, part = divmod(g, STAGES_PER_LAYER)
    j = i // 2
    if part:
        return [("ffn_w_up" if part == 1 else "ffn_w_down", i)]
    return [("sc_w_in", j), ("sc_w_out", j)] if i % 2 == 0 else [("lru_w_in", j), ("lru_w_gate", j), ("lru_w_out", j)]


def _step(x, loss_target, w, m, v):
    bsz, s, d = x.shape
    t = bsz * s
    depth = w["ffn_w_up"].shape[0]
    alpha = (2.0 * depth) ** 0.25
    heads = w["lru_w_gate"].shape[1]

    me = 4 * lax.axis_index("x") + 2 * lax.axis_index("y") + lax.axis_index("c")

    def with_own(land, own):
        return lax.dynamic_update_slice_in_dim(land, own, me, axis=0)

    stages = STAGES_PER_LAYER * depth
    def held(k, arr):
        return jnp.swapaxes(arr, -1, -2) if k in TRANSPOSED else arr

    def split_axis(k):
        return 0 if k in TRANSPOSED else BIG[k] - 1

    flat_names = [k for k in BIG if w[k].ndim == 3]
    wb = dict(zip(flat_names, _layers_bf16([held(k, w[k]) for k in flat_names], name="weights_bf16")))
    wb.update({k: list(w[k].astype(BF16)) for k in BIG if k not in flat_names})

    gathers, tok = [], None
    for g in range(stages):
        arrs = [wb[k][l] for k, l in _stage_big(g)]
        if g == 0:
            arrs += [w[k] for k in SMALL]
        if tok is not None:
            arrs[0] = arrs[0] + tok.astype(BF16)
        gathers.append(_exchange_start(arrs, [True] * len(arrs), name=f"gather_start_{g}"))
        tok = gathers[-1]["token"][0, 0]
    full = {k: [None] * w[k].shape[0] for k in BIG}
    full["sc_conv_b"] = w["sc_conv_b"]
    full["ffn_conv_b"] = w["ffn_conv_b"]

    def arrive(g, after):
        srcs, lands = _exchange_wait(gathers[g], after, name=f"gather_wait_{g}")
        for (k, l), src, land in zip(_stage_big(g), srcs, lands):
            full[k][l] = _whole(with_own(land, src[None]), split_axis(k))
        if g == 0:
            n_big = len(_stage_big(0))
            for k, src, land in zip(SMALL, srcs[n_big:], lands[n_big:]):
                full[k] = _whole(with_own(land, src[None]), w[k].ndim - 1)

    stream, stream_ln = x.reshape(t, d), None
    xb = stream.astype(BF16)
    saved = []
    for i in range(depth):
        j = i // 2
        arrive(3 * i, gathers[-1]["token"] if i == 0 else xb)
        lng, lnb = full["ln_g"][i], full["ln_b"][i]
        sv = {"x0": xb}
        if i % 2 == 0:
            hm = _mm(xb, full["sc_w_in"][j], name="sc_in")
            q = _sc_fwd(hm.reshape(bsz, s, -1), full["sc_conv_w"][j], full["sc_conv_b"][j:j + 1], name="sc_mix")
            w_out = full["sc_w_out"][j]
        else:
            hm = _mm(xb, full["lru_w_in"][j], trans_w=True, bias=full["lru_b_in"][j:j + 1], name="lru_in")
            q, hs = _lru_fwd(hm.reshape(bsz, s, -1), full["lru_conv_w"][j], full["lru_conv_b"][j:j + 1],
                             full["lru_w_gate"][j], full["lru_b_gate"][j].reshape(heads, 1, -1),
                             full["lru_lambda"][j:j + 1], name="lru_mix")
            sv["hs"] = hs
            w_out = full["lru_w_out"][j]
        q = q.reshape(t, -1)
        arrive(3 * i + 1, q)
        z1, x1b = _mm_ln(q, w_out, stream, alpha, lng[0:1], lnb[0:1], resid_ln=stream_ln, name="mix_out_ln")
        hg, hv, gc, vc, a = _ffn_fwd(x1b.reshape(bsz, s, d), full["ffn_w_up"][i], full["ffn_conv_w"][i],
                                     full["ffn_conv_b"][i:i + 1], name="ffn_up_act")
        a = a.reshape(t, -1)
        arrive(3 * i + 2, a)
        z2, xb = _mm_ln(a, full["ffn_w_down"][i], z1, alpha, lng[1:2], lnb[1:2], resid_ln=(lng[0:1], lnb[0:1]),
                        name="ffn_down_ln")
        stream, stream_ln = z2, (lng[1:2], lnb[1:2])
        sv.update(hm=hm, q=q, z1=z1, x1=x1b, ffn=(hg, hv, gc, vc), a=a, z2=z2)
        saved.append(sv)

    sq, dx = _loss_head(stream, *stream_ln, loss_target.reshape(t, d), name="loss_head")
    loss = lax.psum((0.5 / d) * sq[0, 0], MESH_AXES)

    grads = {k: [None] * w[k].shape[0] for k in WEIGHTS}
    scatters = [None] * stages

    def as_updated(k, arr):
        return jnp.swapaxes(arr, -1, -2) if k in SWAPPED else arr

    def depart(g):
        send = [_slabs(grads[k][l], split_axis(k)).astype(BF16) for k, l in _stage_big(g)]
        send = [sl if k in TRANSPOSED else as_updated(k, sl) for (k, l), sl in zip(_stage_big(g), send)]
        scatters[g] = _exchange_start(send, [False] * len(send), name=f"scatter_start_{g}")
        return scatters[g]["token"][0:1, 0:1]

    dz2, dz2b, dg2, db2 = _ln_bwd(dx, saved[-1]["z2"], full["ln_g"][-1][1:2], name="ln_bwd")
    for i in reversed(range(depth)):
        j = i // 2
        sv = saved[i]
        lng = full["ln_g"][i]
        grads["ffn_w_down"][i] = _mm_tn(sv["a"], dz2b, name="ffn_down_dw")
        dhg, dhv, dwg, dwv, dbg, dbv = _ffn_bwd(*sv["ffn"], dz2b.reshape(bsz, s, d), full["ffn_w_down"][i],
                                                full["ffn_conv_w"][i] + depart(3 * i + 2), name="ffn_act_bwd")
        dhg, dhv = dhg.reshape(t, -1), dhv.reshape(t, -1)
        grads["ffn_conv_w"][i] = jnp.concatenate([dwg, dwv], axis=1)
        grads["ffn_conv_b"][i] = jnp.concatenate([dbg, dbv], axis=1)[0]
        rows_up = 2 * dhg.shape[1]
        dw_g = _mm_tn(dhg, sv["x1"], below=(rows_up, None), name="ffn_up_dw_g")
        grads["ffn_w_up"][i] = _mm_tn(dhv, sv["x1"], below=(rows_up, dw_g), name="ffn_up_dw_v")
        dz1, dz1b, dg1, db1 = _mm_ln_bwd([dhg, dhv], full["ffn_w_up"][i], dz2, alpha, sv["z1"],
                                         lng[0:1] + depart(3 * i + 1), name="ffn_up_dx_ln", w_rows_are_k=True)
        grads["ln_g"][i] = jnp.concatenate([dg1, dg2], axis=0)
        grads["ln_b"][i] = jnp.concatenate([db1, db2], axis=0)
        if i % 2 == 0:
            dq = _mm(dz1b, full["sc_w_out"][j], trans_w=True, name="sc_out_dx")
            grads["sc_w_out"][j] = _mm_tn(sv["q"], dz1b, name="sc_out_dw")
            dhm, dcw, dcb = _sc_bwd(sv["hm"].reshape(bsz, s, -1), dq.reshape(bsz, s, -1), full["sc_conv_w"][j],
                                    full["sc_conv_b"][j:j + 1], name="sc_mix_bwd")
            dhm = dhm.reshape(t, -1)
            grads["sc_conv_w"][j] = dcw
            grads["sc_conv_b"][j] = dcb[0]
            grads["sc_w_in"][j] = _mm_tn(sv["x0"], dhm, name="sc_in_dw")
            w_in = full["sc_w_in"][j]
        else:
            dq = _mm(dz1b, full["lru_w_out"][j], trans_w=True, name="lru_out_dx")
            grads["lru_w_out"][j] = _mm_tn(sv["q"], dz1b, name="lru_out_dw")
            dhm, dcw, dcb, dwgt, dbgt, dlam, sgb, srb = _lru_bwd(
                sv["hm"].reshape(bsz, s, -1), sv["hs"], dq.reshape(bsz, s, -1), full["lru_conv_w"][j],
                full["lru_w_gate"][j], full["lru_lambda"][j:j + 1], name="lru_mix_bwd")
            dhm = dhm.reshape(t, -1)
            grads["lru_conv_w"][j] = dcw
            grads["lru_conv_b"][j] = dcb[0]
            grads["lru_w_gate"][j] = dwgt
            grads["lru_b_gate"][j] = dbgt[:, 0, :]
            grads["lru_lambda"][j] = dlam[0]
            grads["lru_b_in"][j] = jnp.concatenate([sgb, srb], axis=1)[0]
            grads["lru_w_in"][j] = _mm_tn(dhm, sv["x0"], name="lru_in_dw")
            w_in = full["lru_w_in"][j]
        tok = depart(3 * i)
        if i > 0:
            dz2, dz2b, dg2, db2 = _mm_ln_bwd([dhm], w_in, dz1, alpha, saved[i - 1]["z2"], full["ln_g"][i - 1][1:2] + tok,
                                             name="mix_in_dx_ln", w_rows_are_k=i % 2 == 1)
        else:
            dx = _mm(dhm, w_in + tok[0, 0].astype(BF16), trans_w=True, resid=dz1, resid_scale=alpha, name="mix_in_dx")
    grad_x = dx.reshape(bsz, s, d)

    gsm = {k: jnp.stack(grads[k]) for k in SMALL + REPL}
    small_scatter = _exchange_start([_slabs(gsm[k], gsm[k].ndim - 1) for k in SMALL] + [gsm[k] for k in REPL],
                                    [False] * len(SMALL) + [True] * len(REPL), name="scatter_start_small")

    out = {}

    def own_slab(src):
        return lax.dynamic_slice_in_dim(src, me, 1, axis=0)

    stacks = {k: None for k in BIG}
    after = dx
    for g in reversed(range(stages)):
        srcs, lands = _exchange_wait(scatters[g], after, name=f"scatter_wait_{g}")
        for (k, l), src, land in zip(_stage_big(g), srcs, lands):
            n_l, c2 = w[k].shape[0], land.shape[-1]
            wk, mk, vk = (as_updated(k, arr[k]).reshape(n_l, -1, c2) for arr in (w, m, v))
            stacks[k] = _adamw(with_own(land, own_slab(src)).reshape(N_DEV, -1, c2), wk, mk, vk, l, stacks[k],
                               name=f"adamw_{k}_{l}")
            after = stacks[k][-1]
    for k in BIG:
        shp = as_updated(k, w[k]).shape
        out[k] = [as_updated(k, r.reshape(shp)) for r in stacks[k]]
    srcs, lands = _exchange_wait(small_scatter, after, name="scatter_wait_small")
    for n, k in enumerate(SMALL + REPL):
        own = srcs[n][None] if k in REPL else own_slab(srcs[n])
        c2 = w[k].shape[-1]
        res = _adamw(with_own(lands[n], own).reshape(N_DEV, -1, c2), w[k].reshape(1, -1, c2), m[k].reshape(1, -1, c2),
                     v[k].reshape(1, -1, c2), 0, None, name="adamw_" + k)
        out[k] = [r.reshape(w[k].shape) for r in res]

    return (loss, grad_x, *[out[k][0] for k in WEIGHTS], *[out[k][1] for k in WEIGHTS],
            *[out[k][2] for k in WEIGHTS], *[out[k][3] for k in WEIGHTS])


def kernel(x, sc_w_in, sc_conv_w, sc_conv_b, sc_w_out, lru_w_in, lru_b_in, lru_conv_w, lru_conv_b, lru_w_gate, lru_b_gate, lru_lambda, lru_w_out, ffn_w_up, ffn_conv_w, ffn_conv_b, ffn_w_down, ln_g, ln_b, loss_target, m_sc_w_in, m_sc_conv_w, m_sc_conv_b, m_sc_w_out, m_lru_w_in, m_lru_b_in, m_lru_conv_w, m_lru_conv_b, m_lru_w_gate, m_lru_b_gate, m_lru_lambda, m_lru_w_out, m_ffn_w_up, m_ffn_conv_w, m_ffn_conv_b, m_ffn_w_down, m_ln_g, m_ln_b, v_sc_w_in, v_sc_conv_w, v_sc_conv_b, v_sc_w_out, v_lru_w_in, v_lru_b_in, v_lru_conv_w, v_lru_conv_b, v_lru_w_gate, v_lru_b_gate, v_lru_lambda, v_lru_w_out, v_ffn_w_up, v_ffn_conv_w, v_ffn_conv_b, v_ffn_w_down, v_ln_g, v_ln_b):
    w = dict(sc_w_in=sc_w_in, sc_conv_w=sc_conv_w, sc_conv_b=sc_conv_b, sc_w_out=sc_w_out, lru_w_in=lru_w_in,
             lru_b_in=lru_b_in, lru_conv_w=lru_conv_w, lru_conv_b=lru_conv_b, lru_w_gate=lru_w_gate,
             lru_b_gate=lru_b_gate, lru_lambda=lru_lambda, lru_w_out=lru_w_out, ffn_w_up=ffn_w_up,
             ffn_conv_w=ffn_conv_w, ffn_conv_b=ffn_conv_b, ffn_w_down=ffn_w_down, ln_g=ln_g, ln_b=ln_b)
    m = dict(sc_w_in=m_sc_w_in, sc_conv_w=m_sc_conv_w, sc_conv_b=m_sc_conv_b, sc_w_out=m_sc_w_out, lru_w_in=m_lru_w_in,
             lru_b_in=m_lru_b_in, lru_conv_w=m_lru_conv_w, lru_conv_b=m_lru_conv_b, lru_w_gate=m_lru_w_gate,
             lru_b_gate=m_lru_b_gate, lru_lambda=m_lru_lambda, lru_w_out=m_lru_w_out, ffn_w_up=m_ffn_w_up,
             ffn_conv_w=m_ffn_conv_w, ffn_conv_b=m_ffn_conv_b, ffn_w_down=m_ffn_w_down, ln_g=m_ln_g, ln_b=m_ln_b)
    v = dict(sc_w_in=v_sc_w_in, sc_conv_w=v_sc_conv_w, sc_conv_b=v_sc_conv_b, sc_w_out=v_sc_w_out, lru_w_in=v_lru_w_in,
             lru_b_in=v_lru_b_in, lru_conv_w=v_lru_conv_w, lru_conv_b=v_lru_conv_b, lru_w_gate=v_lru_w_gate,
             lru_b_gate=v_lru_b_gate, lru_lambda=v_lru_lambda, lru_w_out=v_lru_w_out, ffn_w_up=v_ffn_w_up,
             ffn_conv_w=v_ffn_conv_w, ffn_conv_b=v_ffn_conv_b, ffn_w_down=v_ffn_w_down, ln_g=v_ln_g, ln_b=v_ln_b)
    return _step(x, loss_target, w, m, v)
```

```python
import math

import jax
import jax.numpy as jnp
from jax import lax
from jax.experimental import pallas as pl
from jax.experimental.pallas import tpu as pltpu

F32 = jnp.float32
BF16 = jnp.bfloat16

N_DEV = 8
MESH_AXES = ("x", "y", "c")
LANES = 128
SUBLANES = 8
VMEM_LIMIT = 56 * 1024 * 1024
MM_LHS_ELEMS = 3 * 1024 * 1024
MM_TN = 1536

LRU_C = 8.0
LN_EPS = 1e-5
ADAM_LR = 0.001
ADAM_B1 = 0.9
ADAM_B2 = 0.999
ADAM_EPS = 1e-08
ADAM_WD = 0.01
ADAM_STEP = 10
GELU_K = math.sqrt(2.0 / math.pi)
GELU_C = 0.044715


def _tile(n, target, align):
    if n <= target:
        return n
    t = (target // align) * align
    while t >= align:
        if n % t == 0:
            return t
        t -= align
    return n


def _params(sem):
    return pltpu.CompilerParams(dimension_semantics=sem, vmem_limit_bytes=VMEM_LIMIT)


def _rows(x):
    return lax.broadcasted_iota(jnp.int32, x.shape, 0)


def _shift_dn(x, k, fill=0.0):
    if k == 0:
        return x
    return jnp.where(_rows(x) >= k, pltpu.roll(x, k, 0), fill)


def _shift_up(x, k, fill=0.0):
    if k == 0:
        return x
    s = x.shape[0]
    return jnp.where(_rows(x) < s - k, pltpu.roll(x, s - k, 0), fill)


def _conv_fwd(x, w, b):
    kw = w.shape[0]
    y = _shift_dn(x, kw - 1) * w[0:1, :] + b
    for k in range(1, kw):
        y = y + _shift_dn(x, kw - 1 - k) * w[k:k + 1, :]
    return y


def _conv_bwd(dy, x, w):
    kw = w.shape[0]
    ahead = [_shift_up(dy, j) for j in range(kw)]
    dx = ahead[kw - 1] * w[0:1, :]
    for k in range(1, kw):
        dx = dx + ahead[kw - 1 - k] * w[k:k + 1, :]
    return dx, [_colsum(ahead[kw - 1 - k] * x) for k in range(kw)]


def _accumulate(first, items, cols=slice(None)):
    flat = []
    for ref, val in items:
        if isinstance(val, list):
            flat += [(ref, (slice(k, k + 1), cols), row) for k, row in enumerate(val)]
        else:
            flat.append((ref, Ellipsis, val))

    @pl.when(first)
    def _():
        for ref, idx, val in flat:
            ref[idx] = val

    @pl.when(jnp.logical_not(first))
    def _():
        for ref, idx, val in flat:
            ref[idx] += val


def _colsum(x):
    return jnp.sum(x, axis=0, keepdims=True)


def _sigmoid(x):
    return 1.0 / (1.0 + jnp.exp(-x))


def _log1p(x):
    u = 1.0 + x
    return jnp.where(u == 1.0, x, jnp.log(u) * (x / (u - 1.0)))


def _softplus(x):
    return jnp.maximum(x, 0.0) + _log1p(jnp.exp(-jnp.abs(x)))


def _expm1(x, ex):
    poly = x * (1.0 + x * (0.5 + x * (1.0 / 6.0 + x * (1.0 / 24.0 + x * (1.0 / 120.0 + x * (1.0 / 720.0))))))
    return jnp.where(jnp.abs(x) < 0.25, poly, ex - 1.0)


def _gelu(x):
    t = jnp.tanh(GELU_K * (x + GELU_C * x * x * x))
    return 0.5 * x * (1.0 + t)


def _gelu_and_grad(x):
    x2 = x * x
    t = jnp.tanh(GELU_K * (x + GELU_C * x * x2))
    g = 0.5 * x * (1.0 + t)
    dg = 0.5 * (1.0 + t) + 0.5 * x * (1.0 - t * t) * (GELU_K * (1.0 + 3.0 * GELU_C * x2))
    return g, dg


def _scan_fwd(a, b):
    s = a.shape[0]
    k = 1
    while k < s:
        last = 2 * k >= s
        if k % SUBLANES:
            b = a * _shift_dn(b, k) + b
            if not last:
                a = a * _shift_dn(a, k, 1.0)
        else:
            b = jnp.concatenate([b[:k], a[k:] * b[:s - k] + b[k:]], axis=0)
            if not last:
                a = jnp.concatenate([a[:k], a[k:] * a[:s - k]], axis=0)
        k *= 2
    return b


def _scan_rev(c, v):
    s = c.shape[0]
    k = 1
    while k < s:
        last = 2 * k >= s
        if k % SUBLANES:
            v = c * _shift_up(v, k) + v
            if not last:
                c = c * _shift_up(c, k, 1.0)
        else:
            v = jnp.concatenate([c[:s - k] * v[k:] + v[:s - k], v[s - k:]], axis=0)
            if not last:
                c = jnp.concatenate([c[:s - k] * c[k:], c[s - k:]], axis=0)
        k *= 2
    return v


def _mm(a, w, *, name, trans_w=False, bias=None, resid=None, resid_scale=1.0):
    m, k = a.shape
    n = w.shape[0] if trans_w else w.shape[1]
    tm = _tile(m, min(1024, max(256, MM_LHS_ELEMS // k)), SUBLANES)
    tn = _tile(n, MM_TN, LANES)
    has_bias = bias is not None
    has_resid = resid is not None

    def body(*refs):
        a_ref, w_ref = refs[0], refs[1]
        pos = 2
        b_ref = r_ref = None
        if has_bias:
            b_ref = refs[pos]
            pos += 1
        if has_resid:
            r_ref = refs[pos]
            pos += 1
        o_ref = refs[pos]

        cols = pl.ds(pl.multiple_of(pl.program_id(1) * tn, LANES), tn)
        if trans_w:
            acc = lax.dot_general(a_ref[...], w_ref[cols, :], (((1,), (1,)), ((), ())), preferred_element_type=F32)
        else:
            acc = jnp.dot(a_ref[...], w_ref[:, cols], preferred_element_type=F32)
        if has_bias:
            acc = acc + b_ref[...]
        if has_resid:
            acc = acc + resid_scale * r_ref[...]
        o_ref[...] = acc

    in_specs = [pl.BlockSpec((tm, k), lambda i, j: (i, 0)),
                pl.BlockSpec(w.shape, lambda i, j: (0, 0), pipeline_mode=pl.Buffered(1))]
    args = [a, w]
    if has_bias:
        in_specs.append(pl.BlockSpec((1, tn), lambda i, j: (0, j)))
        args.append(bias)
    if has_resid:
        in_specs.append(pl.BlockSpec((tm, tn), lambda i, j: (i, j)))
        args.append(resid)
    return pl.pallas_call(
        body, name=name, grid=(m // tm, n // tn), in_specs=in_specs,
        out_specs=pl.BlockSpec((tm, tn), lambda i, j: (i, j)),
        out_shape=jax.ShapeDtypeStruct((m, n), F32),
        compiler_params=_params(("parallel", "arbitrary")),
    )(*args)


def _ln(z, g, b):
    mu = jnp.mean(z, axis=-1, keepdims=True)
    zc = z - mu
    var = jnp.mean(zc * zc, axis=-1, keepdims=True)
    return zc * lax.rsqrt(var + LN_EPS) * g + b


def _mm_ln(a, w, resid, alpha, g, b, *, name, resid_ln=None, tm=512):
    m, k = a.shape
    d = w.shape[1]
    tm = _tile(m, tm, SUBLANES)
    n_extra = 0 if resid_ln is None else 2

    def body(a_ref, w_ref, r_ref, g_ref, b_ref, *rest):
        z_ref, obf_ref = rest[n_extra:]
        x = r_ref[...]
        if resid_ln is not None:
            x = _ln(x, rest[0][...], rest[1][...])
        z = alpha * x + jnp.dot(a_ref[...], w_ref[...], preferred_element_type=F32)
        z_ref[...] = z
        obf_ref[...] = _ln(z, g_ref[...], b_ref[...]).astype(BF16)

    row = pl.BlockSpec((tm, d), lambda i: (i, 0))
    vec = pl.BlockSpec((1, d), lambda i: (0, 0))
    return pl.pallas_call(
        body, name=name, grid=(m // tm,),
        in_specs=[pl.BlockSpec((tm, k), lambda i: (i, 0)),
                  pl.BlockSpec((k, d), lambda i: (0, 0), pipeline_mode=pl.Buffered(1)), row, vec, vec]
        + [vec] * n_extra,
        out_specs=[row, row],
        out_shape=[jax.ShapeDtypeStruct((m, d), F32), jax.ShapeDtypeStruct((m, d), BF16)],
        compiler_params=_params(("parallel",)),
    )(a, w, resid, g, b, *(resid_ln or ()))


def _ln_bwd_math(do, z, g):
    mu = jnp.mean(z, axis=-1, keepdims=True)
    zc = z - mu
    var = jnp.mean(zc * zc, axis=-1, keepdims=True)
    rstd = lax.rsqrt(var + LN_EPS)
    xhat = zc * rstd
    dxh = do * g
    m1 = jnp.mean(dxh, axis=-1, keepdims=True)
    m2 = jnp.mean(dxh * xhat, axis=-1, keepdims=True)
    return rstd * (dxh - m1 - xhat * m2), _colsum(do * xhat), _colsum(do)


def _mm_ln_bwd(parts, w, resid, resid_scale, z, g, *, name, w_rows_are_k=False):
    t, kp = parts[0].shape
    k, d = w.shape if w_rows_are_k else w.shape[::-1]
    n = len(parts)
    tm = _tile(t, min(512, max(256, MM_LHS_ELEMS // k)), SUBLANES)

    def body(*refs):
        a_refs = refs[:n]
        w_ref, r_ref, z_ref, g_ref, dz_ref, dzbf_ref, dg_ref, db_ref = refs[n:]

        @pl.when(pl.program_id(0) == 0)
        def _():
            dg_ref[...] = jnp.zeros_like(dg_ref)
            db_ref[...] = jnp.zeros_like(db_ref)

        dx = resid_scale * r_ref[...]
        for p, a_ref in enumerate(a_refs):
            if w_rows_are_k:
                dx = dx + jnp.dot(a_ref[...], w_ref[p * kp:(p + 1) * kp, :], preferred_element_type=F32)
            else:
                dx = dx + lax.dot_general(a_ref[...], w_ref[:, p * kp:(p + 1) * kp], (((1,), (1,)), ((), ())),
                                          preferred_element_type=F32)
        dz, dg, db = _ln_bwd_math(dx, z_ref[...], g_ref[...])
        dz_ref[...] = dz
        dzbf_ref[...] = dz.astype(BF16)
        dg_ref[...] += dg
        db_ref[...] += db

    row = pl.BlockSpec((tm, d), lambda i: (i, 0))
    vec = pl.BlockSpec((1, d), lambda i: (0, 0))
    return pl.pallas_call(
        body, name=name, grid=(t // tm,),
        in_specs=[pl.BlockSpec((tm, kp), lambda i: (i, 0))] * n
        + [pl.BlockSpec(w.shape, lambda i: (0, 0), pipeline_mode=pl.Buffered(1)), row, row, vec],
        out_specs=[row, row, vec, vec],
        out_shape=[jax.ShapeDtypeStruct((t, d), F32), jax.ShapeDtypeStruct((t, d), BF16),
                   jax.ShapeDtypeStruct((1, d), F32), jax.ShapeDtypeStruct((1, d), F32)],
        compiler_params=_params(("arbitrary",)),
    )(*parts, w, resid, z, g)


def _mm_tn(a, b, *, name, below=None, tm=1408, tn=1536, tk=2048):
    t, m = a.shape
    n = b.shape[1]
    tm = _tile(m, tm, LANES)
    tn = _tile(n, tn, LANES)
    tk = _tile(t, tk, SUBLANES)
    last = t // tk - 1
    rows, earlier = (m, None) if below is None else below
    skip = (rows - m) // tm if earlier is not None else 0

    def body(a_ref, b_ref, *rest):
        o_ref, acc = rest[-2:]

        @pl.when(pl.program_id(2) == 0)
        def _():
            acc[...] = jnp.zeros_like(acc)

        acc[...] += lax.dot_general(a_ref[...], b_ref[...], (((0,), (0,)), ((), ())), preferred_element_type=F32)

        @pl.when(pl.program_id(2) == last)
        def _():
            o_ref[...] = acc[...].astype(BF16)

    in_specs = [pl.BlockSpec((tk, tm), lambda i, j, l: (l, i)), pl.BlockSpec((tk, tn), lambda i, j, l: (l, j))]
    return pl.pallas_call(
        body, name=name, grid=(m // tm, n // tn, t // tk),
        in_specs=in_specs + ([] if earlier is None else [pl.BlockSpec(memory_space=pl.ANY)]),
        out_specs=pl.BlockSpec((tm, tn), lambda i, j, l: (i + skip, j)),
        out_shape=jax.ShapeDtypeStruct((rows, n), BF16),
        input_output_aliases={} if earlier is None else {2: 0},
        scratch_shapes=[pltpu.VMEM((tm, tn), F32)],
        compiler_params=_params(("parallel", "parallel", "arbitrary")),
    )(a, b, *(() if earlier is None else (earlier,)))


def _ln_bwd(dout, z, g, *, name, tm=512):
    t, d = z.shape
    tm = _tile(t, tm, SUBLANES)

    def body(do_ref, z_ref, g_ref, dz_ref, dzbf_ref, dg_ref, db_ref):
        @pl.when(pl.program_id(0) == 0)
        def _():
            dg_ref[...] = jnp.zeros_like(dg_ref)
            db_ref[...] = jnp.zeros_like(db_ref)

        dz, dg, db = _ln_bwd_math(do_ref[...], z_ref[...], g_ref[...])
        dz_ref[...] = dz
        dzbf_ref[...] = dz.astype(BF16)
        dg_ref[...] += dg
        db_ref[...] += db

    row = pl.BlockSpec((tm, d), lambda i: (i, 0))
    vec = pl.BlockSpec((1, d), lambda i: (0, 0))
    return pl.pallas_call(
        body, name=name, grid=(t // tm,), in_specs=[row, row, vec], out_specs=[row, row, vec, vec],
        out_shape=[jax.ShapeDtypeStruct((t, d), F32), jax.ShapeDtypeStruct((t, d), BF16),
                   jax.ShapeDtypeStruct((1, d), F32), jax.ShapeDtypeStruct((1, d), F32)],
        compiler_params=_params(("arbitrary",)),
    )(dout, z, g)


def _loss_head(z, g, b, target, *, name, tm=512):
    t, d = z.shape
    tm = _tile(t, tm, SUBLANES)

    def body(z_ref, g_ref, b_ref, t_ref, s_ref, dy_ref):
        @pl.when(pl.program_id(0) == 0)
        def _():
            s_ref[...] = jnp.zeros_like(s_ref)

        e = _ln(z_ref[...], g_ref[...], b_ref[...]) - t_ref[...]
        dy_ref[...] = e * (1.0 / d)
        s_ref[...] += jnp.sum(_colsum(e * e), axis=-1, keepdims=True)

    row = pl.BlockSpec((tm, d), lambda i: (i, 0))
    vec = pl.BlockSpec((1, d), lambda i: (0, 0))
    return pl.pallas_call(
        body, name=name, grid=(t // tm,), in_specs=[row, vec, vec, row],
        out_specs=[pl.BlockSpec((1, LANES), lambda i: (0, 0)), row],
        out_shape=[jax.ShapeDtypeStruct((1, LANES), F32), jax.ShapeDtypeStruct((t, d), F32)],
        compiler_params=_params(("arbitrary",)),
    )(z, g, b, target)


def _own(c, b, *_):
    return c, b


def _ahead(nc, bsz):
    def at(c, b, part):
        b2 = b + jnp.minimum(part, 1)
        return jnp.minimum(c + b2 // bsz, nc - 1), b2 % bsz
    return at


def _strip(s, tc, off, at=_own):
    def index(*ids):
        c, b = at(*ids)
        return b, 0, off + c
    return pl.BlockSpec((None, s, tc), index)


def _cvec(kw, tc, off, at=_own):
    def index(*ids):
        return 0, off + at(*ids)[0]
    return pl.BlockSpec((kw, tc), index)


def _acc(kw, tc):
    return pl.BlockSpec((kw, tc), lambda c, b, *_: (0, c))


def _sc_fwd(h, cw, cb, *, name, tc=256):
    bsz, s, d3 = h.shape
    d = d3 // 3
    tc = _tile(d, tc, LANES)
    nc = d // tc

    def body(gb_ref, gc_ref, v_ref, w_ref, b_ref, q_ref):
        u = _conv_fwd(gc_ref[...] * v_ref[...], w_ref[...], b_ref[...])
        q_ref[...] = (gb_ref[...] * u).astype(BF16)

    return pl.pallas_call(
        body, name=name, grid=(nc, bsz),
        in_specs=[_strip(s, tc, 0), _strip(s, tc, nc), _strip(s, tc, 2 * nc), _cvec(cw.shape[0], tc, 0), _cvec(1, tc, 0)],
        out_specs=_strip(s, tc, 0),
        out_shape=jax.ShapeDtypeStruct((bsz, s, d), BF16),
        compiler_params=_params(("parallel", "parallel")),
    )(h, h, h, cw, cb)


def _sc_bwd(h, dq, cw, cb, *, name, tc=256):
    bsz, s, d3 = h.shape
    d = d3 // 3
    kw = cw.shape[0]
    tc = _tile(d, tc, LANES)
    nc = d // tc

    def body(gb_ref, gc_ref, v_ref, dq_ref, w_ref, b_ref, dh_ref, dw_ref, db_ref, parts):
        b_id, part = pl.program_id(1), pl.program_id(2)

        @pl.when(part == 0)
        def _():
            gb, gc, v, dq_, w = gb_ref[...], gc_ref[...], v_ref[...], dq_ref[...], w_ref[...]
            p = gc * v
            u = _conv_fwd(p, w, b_ref[...])
            du = dq_ * gb
            dp, dw_rows = _conv_bwd(du, p, w)
            parts[0] = (dq_ * u).astype(BF16)
            parts[1] = (dp * v).astype(BF16)
            parts[2] = (dp * gc).astype(BF16)
            _accumulate(b_id == 0, [(dw_ref, dw_rows), (db_ref, _colsum(du))])

        dh_ref[...] = parts[part]

    at = _ahead(nc, bsz)
    return pl.pallas_call(
        body, name=name, grid=(nc, bsz, 3),
        in_specs=[_strip(s, tc, 0, at), _strip(s, tc, nc, at), _strip(s, tc, 2 * nc, at), _strip(s, tc, 0, at),
                  _cvec(kw, tc, 0, at), _cvec(1, tc, 0, at)],
        out_specs=[pl.BlockSpec((None, s, tc), lambda c, b, p: (b, 0, p * nc + c)), _acc(kw, tc), _acc(1, tc)],
        out_shape=[jax.ShapeDtypeStruct((bsz, s, d3), BF16), jax.ShapeDtypeStruct((kw, d), F32),
                   jax.ShapeDtypeStruct((1, d), F32)],
        scratch_shapes=[pltpu.VMEM((3, s, tc), BF16)],
        compiler_params=_params(("parallel", "arbitrary", "arbitrary")),
    )(h, h, h, dq, cw, cb)


def _ffn_specs(s, tc, nc, kw):
    strip = pl.BlockSpec((None, s, tc), lambda b, c: (b, 0, c))
    halves = [pl.BlockSpec((kw, tc), lambda b, c: (0, c)), pl.BlockSpec((kw, tc), lambda b, c: (0, nc + c)),
              pl.BlockSpec((1, tc), lambda b, c: (0, c)), pl.BlockSpec((1, tc), lambda b, c: (0, nc + c))]
    return strip, halves


def _ffn_fwd(x, w_up, cw, cb, *, name, tc=256):
    bsz, s, d = x.shape
    f = w_up.shape[0] // 2
    kw = cw.shape[0]
    tc = _tile(f, tc, LANES)
    nc = f // tc
    nt = (((1,), (1,)), ((), ()))

    def body(x_ref, w_ref, wg_ref, wv_ref, bg_ref, bv_ref, hg_ref, hv_ref, g_ref, v_ref, a_ref):
        c0 = pl.multiple_of(pl.program_id(1) * tc, LANES)
        xs = x_ref[...]
        hg = lax.dot_general(xs, w_ref[pl.ds(c0, tc), :], nt, preferred_element_type=F32)
        hv = lax.dot_general(xs, w_ref[pl.ds(f + c0, tc), :], nt, preferred_element_type=F32)
        hg_ref[...] = hg
        hv_ref[...] = hv
        g = _conv_fwd(hg, wg_ref[...], bg_ref[...])
        v = _conv_fwd(hv, wv_ref[...], bv_ref[...])
        g_ref[...] = g
        v_ref[...] = v
        a_ref[...] = (g * _sigmoid(g) * v).astype(BF16)

    strip, halves = _ffn_specs(s, tc, nc, kw)
    return pl.pallas_call(
        body, name=name, grid=(bsz, nc),
        in_specs=[pl.BlockSpec((None, s, d), lambda b, c: (b, 0, 0)),
                  pl.BlockSpec(w_up.shape, lambda b, c: (0, 0), pipeline_mode=pl.Buffered(1))] + halves,
        out_specs=[strip] * 5,
        out_shape=[jax.ShapeDtypeStruct((bsz, s, f), F32)] * 4 + [jax.ShapeDtypeStruct((bsz, s, f), BF16)],
        compiler_params=_params(("parallel", "arbitrary")),
    )(x, w_up, cw, cw, cb, cb)


def _ffn_bwd(hg, hv, g, v, dz, w_down, cw, *, name, tc=256):
    bsz, s, f = hg.shape
    d = dz.shape[2]
    kw = cw.shape[0]
    tc = _tile(f, tc, LANES)
    nc = f // tc

    def body(hg_ref, hv_ref, g_ref, v_ref, dz_ref, wd_ref, wg_ref, wv_ref,
             dhg_ref, dhv_ref, dwg_ref, dwv_ref, dbg_ref, dbv_ref):
        c0 = pl.multiple_of(pl.program_id(1) * tc, LANES)
        cols = pl.ds(c0, tc)
        da = lax.dot_general(dz_ref[...], wd_ref[cols, :], (((1,), (1,)), ((), ())), preferred_element_type=F32)
        g_ = g_ref[...]
        sg = _sigmoid(g_)
        dv = da * (g_ * sg)
        dg = da * v_ref[...] * (sg * (1.0 + g_ * (1.0 - sg)))
        dhg, dwg_rows = _conv_bwd(dg, hg_ref[...], wg_ref[...])
        dhv, dwv_rows = _conv_bwd(dv, hv_ref[...], wv_ref[...])
        dhg_ref[...] = dhg.astype(BF16)
        dhv_ref[...] = dhv.astype(BF16)
        _accumulate(pl.program_id(0) == 0, [(dwg_ref, dwg_rows), (dwv_ref, dwv_rows),
                                            (dbg_ref, [_colsum(dg)]), (dbv_ref, [_colsum(dv)])], cols)

    strip, halves = _ffn_specs(s, tc, nc, kw)
    whole = lambda r: pl.BlockSpec((r, f), lambda b, c: (0, 0))
    return pl.pallas_call(
        body, name=name, grid=(bsz, nc),
        in_specs=[strip] * 4 + [pl.BlockSpec((None, s, d), lambda b, c: (b, 0, 0)),
                                pl.BlockSpec(w_down.shape, lambda b, c: (0, 0), pipeline_mode=pl.Buffered(1))]
        + halves[:2],
        out_specs=[strip, strip, whole(kw), whole(kw), whole(1), whole(1)],
        out_shape=[jax.ShapeDtypeStruct((bsz, s, f), BF16), jax.ShapeDtypeStruct((bsz, s, f), BF16),
                   jax.ShapeDtypeStruct((kw, f), F32), jax.ShapeDtypeStruct((kw, f), F32),
                   jax.ShapeDtypeStruct((1, f), F32), jax.ShapeDtypeStruct((1, f), F32)],
        compiler_params=_params(("arbitrary", "arbitrary")),
    )(hg, hv, g, v, dz, w_down, cw, cw)


def _lru_gates(r, cw, cb, wg, bg, lam):
    blk = r.shape[1]
    xr = _conv_fwd(r, cw, cb)
    gates = jnp.dot(xr.astype(BF16), wg, preferred_element_type=F32) + bg
    rg = _sigmoid(gates[:, :blk])
    ig = _sigmoid(gates[:, blk:])
    sp = _softplus(-lam)
    la = (-LRU_C * sp) * rg
    a = jnp.exp(la)
    mult = jnp.sqrt(-_expm1(2.0 * la, a * a))
    return xr, rg, ig, sp, a, mult


def _lru_fwd(h, cw, cb, wg, bg, lam, *, name):
    bsz, s, r2 = h.shape
    heads, blk = wg.shape[0], wg.shape[1]
    kw = cw.shape[0]

    def body(g_ref, r_ref, cw_ref, cb_ref, wg_ref, bg_ref, lam_ref, y_ref, sv_ref):
        xr, rg, ig, _, a, mult = _lru_gates(r_ref[...], cw_ref[...], cb_ref[...], wg_ref[...], bg_ref[...], lam_ref[...])
        hs = _scan_fwd(a, mult * (ig * xr))
        for n, val in enumerate((hs, xr, rg, ig, a, mult)):
            sv_ref[n] = val
        y_ref[...] = (hs * _gelu(g_ref[...])).astype(BF16)

    per_head = lambda hd, b: (hd, 0, 0)
    return pl.pallas_call(
        body, name=name, grid=(heads, bsz),
        in_specs=[_strip(s, blk, 0), _strip(s, blk, heads), _cvec(kw, blk, 0), _cvec(1, blk, 0),
                  pl.BlockSpec((None, blk, 2 * blk), per_head), pl.BlockSpec((None, 1, 2 * blk), per_head),
                  _cvec(1, blk, 0)],
        out_specs=[_strip(s, blk, 0), pl.BlockSpec((6, None, s, blk), lambda hd, b: (0, b, 0, hd))],
        out_shape=[jax.ShapeDtypeStruct((bsz, s, r2 // 2), BF16), jax.ShapeDtypeStruct((6, bsz, s, r2 // 2), F32)],
        compiler_params=_params(("parallel", "parallel")),
    )(h, h, cw, cb, wg, bg, lam)


def _lru_bwd(h, sv, dy, cw, wg, lam, *, name):
    bsz, s, r2 = h.shape
    rw = r2 // 2
    heads, blk = wg.shape[0], wg.shape[1]
    kw = cw.shape[0]

    def body(g_ref, r_ref, cw_ref, wg_ref, lam_ref, sv_ref, dy_ref,
             dh_ref, dcw_ref, dcb_ref, dwg_ref, dbg_ref, dlam_ref, sg_ref, sr_ref, parts):
        b_id, part = pl.program_id(1), pl.program_id(2)

        @pl.when(part == 0)
        def _():
            r, cw_, wg_, lam_ = r_ref[...], cw_ref[...], wg_ref[...], lam_ref[...]
            hs_, xr, rg, ig, a, mult = (sv_ref[n] for n in range(6))
            sp = _softplus(-lam_)
            dy_ = dy_ref[...]
            gel, dgel = _gelu_and_grad(g_ref[...])
            dg = dy_ * hs_ * dgel
            lmb = _scan_rev(_shift_up(a, 1, 1.0), dy_ * gel)
            da = lmb * _shift_dn(hs_, 1)
            dmult = lmb * (ig * xr)
            dig = lmb * (mult * xr)
            dxr = lmb * (mult * ig)
            dla = da * a - dmult * (a * a / mult)
            drg = dla * (-LRU_C * sp)
            dsp = _colsum(dla * rg) * (-LRU_C)
            dlam = -dsp * _sigmoid(-lam_)
            dgates = jnp.concatenate([drg * (rg * (1.0 - rg)), dig * (ig * (1.0 - ig))], axis=1)
            dgates_bf = dgates.astype(BF16)
            dwg = lax.dot_general(xr.astype(BF16), dgates_bf, (((0,), (0,)), ((), ())), preferred_element_type=F32)
            dxr = dxr + lax.dot_general(dgates_bf, wg_, (((1,), (1,)), ((), ())), preferred_element_type=F32)
            dr, dcw_rows = _conv_bwd(dxr, r, cw_)
            parts[0] = dg.astype(BF16)
            parts[1] = dr.astype(BF16)
            _accumulate(b_id == 0, [(dcw_ref, dcw_rows), (dcb_ref, _colsum(dxr)), (dwg_ref, dwg),
                                    (dbg_ref, _colsum(dgates)), (dlam_ref, dlam), (sg_ref, _colsum(dg)),
                                    (sr_ref, _colsum(dr))])

        dh_ref[...] = parts[part]

    at = _ahead(heads, bsz)

    def saved(*ids):
        hd, b = at(*ids)
        return 0, b, 0, hd

    vec = pl.BlockSpec((1, blk), lambda hd, b, p: (0, hd))
    return pl.pallas_call(
        body, name=name, grid=(heads, bsz, 2),
        in_specs=[_strip(s, blk, 0, at), _strip(s, blk, heads, at), _cvec(kw, blk, 0, at),
                  pl.BlockSpec((None, blk, 2 * blk), lambda *ids: (at(*ids)[0], 0, 0)), _cvec(1, blk, 0, at),
                  pl.BlockSpec((6, None, s, blk), saved), _strip(s, blk, 0, at)],
        out_specs=[pl.BlockSpec((None, s, blk), lambda hd, b, p: (b, 0, p * heads + hd)),
                   pl.BlockSpec((kw, blk), lambda hd, b, p: (0, hd)), vec,
                   pl.BlockSpec((None, blk, 2 * blk), lambda hd, b, p: (hd, 0, 0)),
                   pl.BlockSpec((None, 1, 2 * blk), lambda hd, b, p: (hd, 0, 0)), vec, vec, vec],
        out_shape=[jax.ShapeDtypeStruct((bsz, s, r2), BF16), jax.ShapeDtypeStruct((kw, rw), F32),
                   jax.ShapeDtypeStruct((1, rw), F32), jax.ShapeDtypeStruct((heads, blk, 2 * blk), F32),
                   jax.ShapeDtypeStruct((heads, 1, 2 * blk), F32), jax.ShapeDtypeStruct((1, rw), F32),
                   jax.ShapeDtypeStruct((1, rw), F32), jax.ShapeDtypeStruct((1, rw), F32)],
        scratch_shapes=[pltpu.VMEM((2, s, blk), BF16)],
        compiler_params=_params(("parallel", "arbitrary", "arbitrary")),
    )(h, h, cw, wg, lam, sv, dy)


HBM_SPEC = pl.BlockSpec(memory_space=pltpu.HBM)
SEM_SPEC = pl.BlockSpec(memory_space=pltpu.SEMAPHORE)
EFFECT = pltpu.SideEffectType.DATAFLOW_SIDE_EFFECTING


def _peer_copies(srcs, lands, gather, send_sem, recv_sem):
    x, y, c = (lax.axis_index(ax) for ax in MESH_AXES)
    me = 4 * x + 2 * y + c
    copies = []
    for i in range(len(srcs)):
        for d in range(1, N_DEV):
            px = 1 - x if d & 4 else x
            py = 1 - y if d & 2 else y
            pc = 1 - c if d & 1 else c
            src = srcs[i] if gather[i] else srcs[i].at[4 * px + 2 * py + pc]
            k = i * (N_DEV - 1) + d - 1
            copies.append(pltpu.make_async_remote_copy(
                src_ref=src, dst_ref=lands[i].at[me], send_sem=send_sem.at[k], recv_sem=recv_sem.at[k],
                device_id=(px, py, pc), device_id_type=pl.DeviceIdType.MESH))
    return copies


def _exchange_start(arrs, gather, *, name):
    n = len(arrs)
    lands = [lax.empty((N_DEV,) + tuple(a.shape if g else a.shape[1:]), a.dtype) for a, g in zip(arrs, gather)]

    def body(*refs):
        srcs, land_refs = refs[:n], refs[n:2 * n]
        send_sem, recv_sem = refs[2 * n], refs[2 * n + 1]
        token = refs[-1]
        for cp in _peer_copies(srcs, land_refs, gather, send_sem, recv_sem):
            cp.start()
        token[...] = jnp.zeros_like(token)

    sems = pltpu.SemaphoreType.DMA((n * (N_DEV - 1),))
    thru = [pltpu.HBM(a.shape, a.dtype) for a in arrs + lands]
    out = pl.pallas_call(
        body, name=name, in_specs=[HBM_SPEC] * (2 * n),
        out_shape=(sems, sems, *thru, jax.ShapeDtypeStruct((SUBLANES, LANES), F32)),
        out_specs=(SEM_SPEC, SEM_SPEC, *([HBM_SPEC] * (2 * n)), pl.BlockSpec(memory_space=pltpu.VMEM)),
        input_output_aliases={i: 2 + i for i in range(2 * n)},
        compiler_params=pltpu.CompilerParams(has_side_effects=EFFECT),
    )(*[pltpu.with_memory_space_constraint(a, pltpu.HBM) for a in arrs + lands])
    return {"send_sem": out[0], "recv_sem": out[1], "srcs": list(out[2:2 + n]), "lands": list(out[2 + n:2 + 2 * n]),
            "token": out[-1], "gather": list(gather)}


def _exchange_wait(handle, after, *, name):
    srcs, lands, gather = handle["srcs"], handle["lands"], handle["gather"]
    n = len(srcs)

    def body(*refs):
        src_refs, land_refs = refs[:n], refs[n:2 * n]
        send_sem, recv_sem = refs[2 * n], refs[2 * n + 1]
        for cp in _peer_copies(src_refs, land_refs, gather, send_sem, recv_sem):
            cp.wait_send()
            cp.wait_recv()

    out = pl.pallas_call(
        body, name=name,
        in_specs=[HBM_SPEC] * (2 * n) + [SEM_SPEC, SEM_SPEC, pl.BlockSpec(memory_space=pl.ANY)],
        out_shape=tuple(pltpu.HBM(a.shape, a.dtype) for a in srcs + lands), out_specs=tuple([HBM_SPEC] * (2 * n)),
        input_output_aliases={i: i for i in range(2 * n)},
        compiler_params=pltpu.CompilerParams(has_side_effects=EFFECT),
    )(*srcs, *lands, handle["send_sem"], handle["recv_sem"], after)
    return list(out[:n]), list(out[n:])


def _layers_bf16(stacks, *, name):
    counts = [a.shape[0] for a in stacks]

    def body(*refs):
        outs = iter(refs[len(stacks):])
        for i_ref, n_layers in zip(refs, counts):
            for layer in range(n_layers):
                next(outs)[...] = i_ref[layer].astype(BF16)

    flat = pl.pallas_call(
        body, name=name,
        out_shape=[jax.ShapeDtypeStruct(a.shape[1:], BF16) for a in stacks for _ in range(a.shape[0])],
        compiler_params=pltpu.CompilerParams(vmem_limit_bytes=VMEM_LIMIT),
    )(*stacks)
    split, pos = [], 0
    for n_layers in counts:
        split.append(list(flat[pos:pos + n_layers]))
        pos += n_layers
    return split


def _adamw(parts, w, m, v, layer, so_far, *, name, tr=256):
    n_layers, r, c = w.shape
    tr = _tile(r, tr, SUBLANES)
    bc1 = 1.0 / (1.0 - ADAM_B1 ** ADAM_STEP)
    bc2 = 1.0 / (1.0 - ADAM_B2 ** ADAM_STEP)
    if so_far is None:
        so_far = [lax.empty(w.shape, F32) for _ in range(4)]

    def body(p_ref, w_ref, m_ref, v_ref, *rest):
        g_ref, d_ref, mo_ref, vo_ref = rest[4:]
        g = p_ref[0].astype(F32)
        for s in range(1, N_DEV):
            g = g + p_ref[s].astype(F32)
        m_new = ADAM_B1 * m_ref[...] + (1.0 - ADAM_B1) * g
        v_new = ADAM_B2 * v_ref[...] + (1.0 - ADAM_B2) * (g * g)
        g_ref[...] = g
        mo_ref[...] = m_new
        vo_ref[...] = v_new
        d_ref[...] = -ADAM_LR * ((m_new * bc1) / (jnp.sqrt(v_new * bc2) + ADAM_EPS) + ADAM_WD * w_ref[...])

    blk = pl.BlockSpec((None, tr, c), lambda i: (layer, i, 0))
    return pl.pallas_call(
        body, name=name, grid=(r // tr,),
        in_specs=[pl.BlockSpec((N_DEV, tr, c), lambda i: (0, i, 0)), blk, blk, blk]
        + [pl.BlockSpec(memory_space=pl.ANY)] * 4,
        out_specs=[blk] * 4, out_shape=[jax.ShapeDtypeStruct(w.shape, F32)] * 4,
        input_output_aliases={4 + o: o for o in range(4)},
        compiler_params=_params(("parallel",)),
    )(parts, w, m, v, *so_far)


def _whole(slabs, axis):
    x = jnp.moveaxis(slabs, 0, axis)
    shp = x.shape
    return x.reshape(shp[:axis] + (shp[axis] * shp[axis + 1],) + shp[axis + 2:])


def _slabs(whole, axis):
    shp = whole.shape
    x = whole.reshape(shp[:axis] + (N_DEV, shp[axis] // N_DEV) + shp[axis + 1:])
    return jnp.moveaxis(x, axis, 0)


BIG = {"sc_w_in": 2, "sc_w_out": 1, "lru_w_in": 2, "lru_w_gate": 3, "lru_w_out": 1, "ffn_w_up": 2, "ffn_w_down": 1}
TRANSPOSED = ("ffn_w_up", "lru_w_in")
SMALL = ["sc_conv_w", "lru_b_in", "lru_conv_w", "lru_conv_b", "lru_b_gate", "lru_lambda", "ffn_conv_w", "ln_g", "ln_b"]
REPL = ["sc_conv_b", "ffn_conv_b"]
WEIGHTS = ["sc_w_in", "sc_conv_w", "sc_conv_b", "sc_w_out", "lru_w_in", "lru_b_in", "lru_conv_w", "lru_conv_b",
           "lru_w_gate", "lru_b_gate", "lru_lambda", "lru_w_out", "ffn_w_up", "ffn_conv_w", "ffn_conv_b", "ffn_w_down",
           "ln_g", "ln_b"]


STAGES_PER_LAYER = 3


def _stage_big(g):
    i, part = divmod(g, STAGES_PER_LAYER)
    j = i // 2
    if part:
        return [("ffn_w_up" if part == 1 else "ffn_w_down", i)]
    return [("sc_w_in", j), ("sc_w_out", j)] if i % 2 == 0 else [("lru_w_in", j), ("lru_w_gate", j), ("lru_w_out", j)]


def _step(x, loss_target, w, m, v):
    bsz, s, d = x.shape
    t = bsz * s
    depth = w["ffn_w_up"].shape[0]
    alpha = (2.0 * depth) ** 0.25
    heads = w["lru_w_gate"].shape[1]

    me = 4 * lax.axis_index("x") + 2 * lax.axis_index("y") + lax.axis_index("c")

    def with_own(land, own):
        return lax.dynamic_update_slice_in_dim(land, own, me, axis=0)

    stages = STAGES_PER_LAYER * depth
    def held(k, arr):
        return jnp.swapaxes(arr, -1, -2) if k in TRANSPOSED else arr

    def split_axis(k):
        return 0 if k in TRANSPOSED else BIG[k] - 1

    flat_names = [k for k in BIG if w[k].ndim == 3]
    wb = dict(zip(flat_names, _layers_bf16([held(k, w[k]) for k in flat_names], name="weights_bf16")))
    wb.update({k: list(w[k].astype(BF16)) for k in BIG if k not in flat_names})

    gathers, tok = [], None
    for g in range(stages):
        arrs = [wb[k][l] for k, l in _stage_big(g)]
        if g == 0:
            arrs += [w[k] for k in SMALL]
        if tok is not None:
            arrs[0] = arrs[0] + tok.astype(BF16)
        gathers.append(_exchange_start(arrs, [True] * len(arrs), name=f"gather_start_{g}"))
        tok = gathers[-1]["token"][0, 0]
    full = {k: [None] * w[k].shape[0] for k in BIG}
    full["sc_conv_b"] = w["sc_conv_b"]
    full["ffn_conv_b"] = w["ffn_conv_b"]

    def arrive(g, after):
        srcs, lands = _exchange_wait(gathers[g], after, name=f"gather_wait_{g}")
        for (k, l), src, land in zip(_stage_big(g), srcs, lands):
            full[k][l] = _whole(with_own(land, src[None]), split_axis(k))
        if g == 0:
            n_big = len(_stage_big(0))
            for k, src, land in zip(SMALL, srcs[n_big:], lands[n_big:]):
                full[k] = _whole(with_own(land, src[None]), w[k].ndim - 1)

    stream, stream_ln = x.reshape(t, d), None
    xb = stream.astype(BF16)
    saved = []
    for i in range(depth):
        j = i // 2
        arrive(3 * i, gathers[-1]["token"] if i == 0 else xb)
        lng, lnb = full["ln_g"][i], full["ln_b"][i]
        sv = {"x0": xb}
        if i % 2 == 0:
            hm = _mm(xb, full["sc_w_in"][j], name="sc_in")
            q = _sc_fwd(hm.reshape(bsz, s, -1), full["sc_conv_w"][j], full["sc_conv_b"][j:j + 1], name="sc_mix")
            w_out = full["sc_w_out"][j]
        else:
            hm = _mm(xb, full["lru_w_in"][j], trans_w=True, bias=full["lru_b_in"][j:j + 1], name="lru_in")
            q, hs = _lru_fwd(hm.reshape(bsz, s, -1), full["lru_conv_w"][j], full["lru_conv_b"][j:j + 1],
                             full["lru_w_gate"][j], full["lru_b_gate"][j].reshape(heads, 1, -1),
                             full["lru_lambda"][j:j + 1], name="lru_mix")
            sv["hs"] = hs
            w_out = full["lru_w_out"][j]
        q = q.reshape(t, -1)
        arrive(3 * i + 1, q)
        z1, x1b = _mm_ln(q, w_out, stream, alpha, lng[0:1], lnb[0:1], resid_ln=stream_ln, name="mix_out_ln")
        hg, hv, gc, vc, a = _ffn_fwd(x1b.reshape(bsz, s, d), full["ffn_w_up"][i], full["ffn_conv_w"][i],
                                     full["ffn_conv_b"][i:i + 1], name="ffn_up_act")
        a = a.reshape(t, -1)
        arrive(3 * i + 2, a)
        z2, xb = _mm_ln(a, full["ffn_w_down"][i], z1, alpha, lng[1:2], lnb[1:2], resid_ln=(lng[0:1], lnb[0:1]),
                        name="ffn_down_ln")
        stream, stream_ln = z2, (lng[1:2], lnb[1:2])
        sv.update(hm=hm, q=q, z1=z1, x1=x1b, ffn=(hg, hv, gc, vc), a=a, z2=z2)
        saved.append(sv)

    sq, dx = _loss_head(stream, *stream_ln, loss_target.reshape(t, d), name="loss_head")
    loss = lax.psum((0.5 / d) * sq[0, 0], MESH_AXES)

    grads = {k: [None] * w[k].shape[0] for k in WEIGHTS}
    scatters = [None] * stages

    def depart(g):
        send = [_slabs(grads[k][l], split_axis(k)).astype(BF16) for k, l in _stage_big(g)]
        scatters[g] = _exchange_start(send, [False] * len(send), name=f"scatter_start_{g}")
        return scatters[g]["token"][0:1, 0:1]

    dz2, dz2b, dg2, db2 = _ln_bwd(dx, saved[-1]["z2"], full["ln_g"][-1][1:2], name="ln_bwd")
    for i in reversed(range(depth)):
        j = i // 2
        sv = saved[i]
        lng = full["ln_g"][i]
        grads["ffn_w_down"][i] = _mm_tn(sv["a"], dz2b, name="ffn_down_dw")
        dhg, dhv, dwg, dwv, dbg, dbv = _ffn_bwd(*sv["ffn"], dz2b.reshape(bsz, s, d), full["ffn_w_down"][i],
                                                full["ffn_conv_w"][i] + depart(3 * i + 2), name="ffn_act_bwd")
        dhg, dhv = dhg.reshape(t, -1), dhv.reshape(t, -1)
        grads["ffn_conv_w"][i] = jnp.concatenate([dwg, dwv], axis=1)
        grads["ffn_conv_b"][i] = jnp.concatenate([dbg, dbv], axis=1)[0]
        rows_up = 2 * dhg.shape[1]
        dw_g = _mm_tn(dhg, sv["x1"], below=(rows_up, None), name="ffn_up_dw_g")
        grads["ffn_w_up"][i] = _mm_tn(dhv, sv["x1"], below=(rows_up, dw_g), name="ffn_up_dw_v")
        dz1, dz1b, dg1, db1 = _mm_ln_bwd([dhg, dhv], full["ffn_w_up"][i], dz2, alpha, sv["z1"],
                                         lng[0:1] + depart(3 * i + 1), name="ffn_up_dx_ln", w_rows_are_k=True)
        grads["ln_g"][i] = jnp.concatenate([dg1, dg2], axis=0)
        grads["ln_b"][i] = jnp.concatenate([db1, db2], axis=0)
        if i % 2 == 0:
            dq = _mm(dz1b, full["sc_w_out"][j], trans_w=True, name="sc_out_dx")
            grads["sc_w_out"][j] = _mm_tn(sv["q"], dz1b, name="sc_out_dw")
            dhm, dcw, dcb = _sc_bwd(sv["hm"].reshape(bsz, s, -1), dq.reshape(bsz, s, -1), full["sc_conv_w"][j],
                                    full["sc_conv_b"][j:j + 1], name="sc_mix_bwd")
            dhm = dhm.reshape(t, -1)
            grads["sc_conv_w"][j] = dcw
            grads["sc_conv_b"][j] = dcb[0]
            grads["sc_w_in"][j] = _mm_tn(sv["x0"], dhm, name="sc_in_dw")
            w_in = full["sc_w_in"][j]
        else:
            dq = _mm(dz1b, full["lru_w_out"][j], trans_w=True, name="lru_out_dx")
            grads["lru_w_out"][j] = _mm_tn(sv["q"], dz1b, name="lru_out_dw")
            dhm, dcw, dcb, dwgt, dbgt, dlam, sgb, srb = _lru_bwd(
                sv["hm"].reshape(bsz, s, -1), sv["hs"], dq.reshape(bsz, s, -1), full["lru_conv_w"][j],
                full["lru_w_gate"][j], full["lru_lambda"][j:j + 1], name="lru_mix_bwd")
            dhm = dhm.reshape(t, -1)
            grads["lru_conv_w"][j] = dcw
            grads["lru_conv_b"][j] = dcb[0]
            grads["lru_w_gate"][j] = dwgt
            grads["lru_b_gate"][j] = dbgt[:, 0, :]
            grads["lru_lambda"][j] = dlam[0]
            grads["lru_b_in"][j] = jnp.concatenate([sgb, srb], axis=1)[0]
            grads["lru_w_in"][j] = _mm_tn(dhm, sv["x0"], name="lru_in_dw")
            w_in = full["lru_w_in"][j]
        tok = depart(3 * i)
        if i > 0:
            dz2, dz2b, dg2, db2 = _mm_ln_bwd([dhm], w_in, dz1, alpha, saved[i - 1]["z2"], full["ln_g"][i - 1][1:2] + tok,
                                             name="mix_in_dx_ln", w_rows_are_k=i % 2 == 1)
        else:
            dx = _mm(dhm, w_in + tok[0, 0].astype(BF16), trans_w=True, resid=dz1, resid_scale=alpha, name="mix_in_dx")
    grad_x = dx.reshape(bsz, s, d)

    gsm = {k: jnp.stack(grads[k]) for k in SMALL + REPL}
    small_scatter = _exchange_start([_slabs(gsm[k], gsm[k].ndim - 1) for k in SMALL] + [gsm[k] for k in REPL],
                                    [False] * len(SMALL) + [True] * len(REPL), name="scatter_start_small")

    out = {}

    def own_slab(src):
        return lax.dynamic_slice_in_dim(src, me, 1, axis=0)

    stacks = {k: None for k in BIG}
    after = dx
    for g in reversed(range(stages)):
        srcs, lands = _exchange_wait(scatters[g], after, name=f"scatter_wait_{g}")
        for (k, l), src, land in zip(_stage_big(g), srcs, lands):
            n_l, c2 = w[k].shape[0], land.shape[-1]
            wk, mk, vk = (held(k, arr[k]).reshape(n_l, -1, c2) for arr in (w, m, v))
            stacks[k] = _adamw(with_own(land, own_slab(src)).reshape(N_DEV, -1, c2), wk, mk, vk, l, stacks[k],
                               name=f"adamw_{k}_{l}")
            after = stacks[k][-1]
    for k in BIG:
        shp = held(k, w[k]).shape
        out[k] = [held(k, r.reshape(shp)) for r in stacks[k]]
    srcs, lands = _exchange_wait(small_scatter, after, name="scatter_wait_small")
    for n, k in enumerate(SMALL + REPL):
        own = srcs[n][None] if k in REPL else own_slab(srcs[n])
        c2 = w[k].shape[-1]
        res = _adamw(with_own(lands[n], own).reshape(N_DEV, -1, c2), w[k].reshape(1, -1, c2), m[k].reshape(1, -1, c2),
                     v[k].reshape(1, -1, c2), 0, None, name="adamw_" + k)
        out[k] = [r.reshape(w[k].shape) for r in res]

    return (loss, grad_x, *[out[k][0] for k in WEIGHTS], *[out[k][1] for k in WEIGHTS],
            *[out[k][2] for k in WEIGHTS], *[out[k][3] for k in WEIGHTS])


def kernel(x, sc_w_in, sc_conv_w, sc_conv_b, sc_w_out, lru_w_in, lru_b_in, lru_conv_w, lru_conv_b, lru_w_gate, lru_b_gate, lru_lambda, lru_w_out, ffn_w_up, ffn_conv_w, ffn_conv_b, ffn_w_down, ln_g, ln_b, loss_target, m_sc_w_in, m_sc_conv_w, m_sc_conv_b, m_sc_w_out, m_lru_w_in, m_lru_b_in, m_lru_conv_w, m_lru_conv_b, m_lru_w_gate, m_lru_b_gate, m_lru_lambda, m_lru_w_out, m_ffn_w_up, m_ffn_conv_w, m_ffn_conv_b, m_ffn_w_down, m_ln_g, m_ln_b, v_sc_w_in, v_sc_conv_w, v_sc_conv_b, v_sc_w_out, v_lru_w_in, v_lru_b_in, v_lru_conv_w, v_lru_conv_b, v_lru_w_gate, v_lru_b_gate, v_lru_lambda, v_lru_w_out, v_ffn_w_up, v_ffn_conv_w, v_ffn_conv_b, v_ffn_w_down, v_ln_g, v_ln_b):
    w = dict(sc_w_in=sc_w_in, sc_conv_w=sc_conv_w, sc_conv_b=sc_conv_b, sc_w_out=sc_w_out, lru_w_in=lru_w_in,
             lru_b_in=lru_b_in, lru_conv_w=lru_conv_w, lru_conv_b=lru_conv_b, lru_w_gate=lru_w_gate,
             lru_b_gate=lru_b_gate, lru_lambda=lru_lambda, lru_w_out=lru_w_out, ffn_w_up=ffn_w_up,
             ffn_conv_w=ffn_conv_w, ffn_conv_b=ffn_conv_b, ffn_w_down=ffn_w_down, ln_g=ln_g, ln_b=ln_b)
    m = dict(sc_w_in=m_sc_w_in, sc_conv_w=m_sc_conv_w, sc_conv_b=m_sc_conv_b, sc_w_out=m_sc_w_out, lru_w_in=m_lru_w_in,
             lru_b_in=m_lru_b_in, lru_conv_w=m_lru_conv_w, lru_conv_b=m_lru_conv_b, lru_w_gate=m_lru_w_gate,
             lru_b_gate=m_lru_b_gate, lru_lambda=m_lru_lambda, lru_w_out=m_lru_w_out, ffn_w_up=m_ffn_w_up,
             ffn_conv_w=m_ffn_conv_w, ffn_conv_b=m_ffn_conv_b, ffn_w_down=m_ffn_w_down, ln_g=m_ln_g, ln_b=m_ln_b)
    v = dict(sc_w_in=v_sc_w_in, sc_conv_w=v_sc_conv_w, sc_conv_b=v_sc_conv_b, sc_w_out=v_sc_w_out, lru_w_in=v_lru_w_in,
             lru_b_in=v_lru_b_in, lru_conv_w=v_lru_conv_w, lru_conv_b=v_lru_conv_b, lru_w_gate=v_lru_w_gate,
             lru_b_gate=v_lru_b_gate, lru_lambda=v_lru_lambda, lru_w_out=v_lru_w_out, ffn_w_up=v_ffn_w_up,
             ffn_conv_w=v_ffn_conv_w, ffn_conv_b=v_ffn_conv_b, ffn_w_down=v_ffn_w_down, ln_g=v_ln_g, ln_b=v_ln_b)
    return _step(x, loss_target, w, m, v)
```

```python
import math

import jax
import jax.numpy as jnp
from jax import lax
from jax.experimental import pallas as pl
from jax.experimental.pallas import tpu as pltpu

F32 = jnp.float32
BF16 = jnp.bfloat16

N_DEV = 8
MESH_AXES = ("x", "y", "c")
LANES = 128
SUBLANES = 8
VMEM_LIMIT = 56 * 1024 * 1024
MM_LHS_ELEMS = 3 * 1024 * 1024
MM_TN = 1536

LRU_C = 8.0
LN_EPS = 1e-5
ADAM_LR = 0.001
ADAM_B1 = 0.9
ADAM_B2 = 0.999
ADAM_EPS = 1e-08
ADAM_WD = 0.01
ADAM_STEP = 10
GELU_K = math.sqrt(2.0 / math.pi)
GELU_C = 0.044715


def _tile(n, target, align):
    if n <= target:
        return n
    t = (target // align) * align
    while t >= align:
        if n % t == 0:
            return t
        t -= align
    return n


def _params(sem):
    return pltpu.CompilerParams(dimension_semantics=sem, vmem_limit_bytes=VMEM_LIMIT)


def _rows(x):
    return lax.broadcasted_iota(jnp.int32, x.shape, 0)


def _shift_dn(x, k, fill=0.0):
    if k == 0:
        return x
    return jnp.where(_rows(x) >= k, pltpu.roll(x, k, 0), fill)


def _shift_up(x, k, fill=0.0):
    if k == 0:
        return x
    s = x.shape[0]
    return jnp.where(_rows(x) < s - k, pltpu.roll(x, s - k, 0), fill)


def _conv_fwd(x, w, b):
    kw = w.shape[0]
    y = _shift_dn(x, kw - 1) * w[0:1, :] + b
    for k in range(1, kw):
        y = y + _shift_dn(x, kw - 1 - k) * w[k:k + 1, :]
    return y


def _conv_bwd(dy, x, w):
    kw = w.shape[0]
    ahead = [_shift_up(dy, j) for j in range(kw)]
    dx = ahead[kw - 1] * w[0:1, :]
    for k in range(1, kw):
        dx = dx + ahead[kw - 1 - k] * w[k:k + 1, :]
    return dx, [_colsum(ahead[kw - 1 - k] * x) for k in range(kw)]


def _accumulate(first, items, cols=slice(None)):
    flat = []
    for ref, val in items:
        if isinstance(val, list):
            flat += [(ref, (slice(k, k + 1), cols), row) for k, row in enumerate(val)]
        else:
            flat.append((ref, Ellipsis, val))

    @pl.when(first)
    def _():
        for ref, idx, val in flat:
            ref[idx] = val

    @pl.when(jnp.logical_not(first))
    def _():
        for ref, idx, val in flat:
            ref[idx] += val


def _colsum(x):
    return jnp.sum(x, axis=0, keepdims=True)


def _sigmoid(x):
    return 1.0 / (1.0 + jnp.exp(-x))


def _log1p(x):
    u = 1.0 + x
    return jnp.where(u == 1.0, x, jnp.log(u) * (x / (u - 1.0)))


def _softplus(x):
    return jnp.maximum(x, 0.0) + _log1p(jnp.exp(-jnp.abs(x)))


def _expm1(x, ex):
    poly = x * (1.0 + x * (0.5 + x * (1.0 / 6.0 + x * (1.0 / 24.0 + x * (1.0 / 120.0 + x * (1.0 / 720.0))))))
    return jnp.where(jnp.abs(x) < 0.25, poly, ex - 1.0)


def _gelu(x):
    t = jnp.tanh(GELU_K * (x + GELU_C * x * x * x))
    return 0.5 * x * (1.0 + t)


def _gelu_and_grad(x):
    x2 = x * x
    t = jnp.tanh(GELU_K * (x + GELU_C * x * x2))
    g = 0.5 * x * (1.0 + t)
    dg = 0.5 * (1.0 + t) + 0.5 * x * (1.0 - t * t) * (GELU_K * (1.0 + 3.0 * GELU_C * x2))
    return g, dg


def _scan_fwd(a, b):
    s = a.shape[0]
    k = 1
    while k < s:
        last = 2 * k >= s
        if k % SUBLANES:
            b = a * _shift_dn(b, k) + b
            if not last:
                a = a * _shift_dn(a, k, 1.0)
        else:
            b = jnp.concatenate([b[:k], a[k:] * b[:s - k] + b[k:]], axis=0)
            if not last:
                a = jnp.concatenate([a[:k], a[k:] * a[:s - k]], axis=0)
        k *= 2
    return b


def _scan_rev(c, v):
    s = c.shape[0]
    k = 1
    while k < s:
        last = 2 * k >= s
        if k % SUBLANES:
            v = c * _shift_up(v, k) + v
            if not last:
                c = c * _shift_up(c, k, 1.0)
        else:
            v = jnp.concatenate([c[:s - k] * v[k:] + v[:s - k], v[s - k:]], axis=0)
            if not last:
                c = jnp.concatenate([c[:s - k] * c[k:], c[s - k:]], axis=0)
        k *= 2
    return v


def _mm(a, w, *, name, trans_w=False, bias=None, resid=None, resid_scale=1.0):
    m, k = a.shape
    n = w.shape[0] if trans_w else w.shape[1]
    tm = _tile(m, min(1024, max(256, MM_LHS_ELEMS // k)), SUBLANES)
    tn = _tile(n, MM_TN, LANES)
    has_bias = bias is not None
    has_resid = resid is not None

    def body(*refs):
        a_ref, w_ref = refs[0], refs[1]
        pos = 2
        b_ref = r_ref = None
        if has_bias:
            b_ref = refs[pos]
            pos += 1
        if has_resid:
            r_ref = refs[pos]
            pos += 1
        o_ref = refs[pos]

        cols = pl.ds(pl.multiple_of(pl.program_id(1) * tn, LANES), tn)
        if trans_w:
            acc = lax.dot_general(a_ref[...], w_ref[cols, :], (((1,), (1,)), ((), ())), preferred_element_type=F32)
        else:
            acc = jnp.dot(a_ref[...], w_ref[:, cols], preferred_element_type=F32)
        if has_bias:
            acc = acc + b_ref[...]
        if has_resid:
            acc = acc + resid_scale * r_ref[...]
        o_ref[...] = acc

    in_specs = [pl.BlockSpec((tm, k), lambda i, j: (i, 0)),
                pl.BlockSpec(w.shape, lambda i, j: (0, 0), pipeline_mode=pl.Buffered(1))]
    args = [a, w]
    if has_bias:
        in_specs.append(pl.BlockSpec((1, tn), lambda i, j: (0, j)))
        args.append(bias)
    if has_resid:
        in_specs.append(pl.BlockSpec((tm, tn), lambda i, j: (i, j)))
        args.append(resid)
    return pl.pallas_call(
        body, name=name, grid=(m // tm, n // tn), in_specs=in_specs,
        out_specs=pl.BlockSpec((tm, tn), lambda i, j: (i, j)),
        out_shape=jax.ShapeDtypeStruct((m, n), F32),
        compiler_params=_params(("parallel", "arbitrary")),
    )(*args)


def _ln(z, g, b):
    mu = jnp.mean(z, axis=-1, keepdims=True)
    zc = z - mu
    var = jnp.mean(zc * zc, axis=-1, keepdims=True)
    return zc * lax.rsqrt(var + LN_EPS) * g + b


def _mm_ln(a, w, resid, alpha, g, b, *, name, resid_ln=None, tm=1024):
    m, k = a.shape
    d = w.shape[1]
    tm = _tile(m, tm, SUBLANES)
    n_extra = 0 if resid_ln is None else 2

    def body(a_ref, w_ref, r_ref, g_ref, b_ref, *rest):
        z_ref, obf_ref = rest[n_extra:]
        x = r_ref[...]
        if resid_ln is not None:
            x = _ln(x, rest[0][...], rest[1][...])
        z = alpha * x + jnp.dot(a_ref[...], w_ref[...], preferred_element_type=F32)
        z_ref[...] = z
        obf_ref[...] = _ln(z, g_ref[...], b_ref[...]).astype(BF16)

    row = pl.BlockSpec((tm, d), lambda i: (i, 0))
    vec = pl.BlockSpec((1, d), lambda i: (0, 0))
    return pl.pallas_call(
        body, name=name, grid=(m // tm,),
        in_specs=[pl.BlockSpec((tm, k), lambda i: (i, 0)),
                  pl.BlockSpec((k, d), lambda i: (0, 0), pipeline_mode=pl.Buffered(1)), row, vec, vec]
        + [vec] * n_extra,
        out_specs=[row, row],
        out_shape=[jax.ShapeDtypeStruct((m, d), F32), jax.ShapeDtypeStruct((m, d), BF16)],
        compiler_params=_params(("parallel",)),
    )(a, w, resid, g, b, *(resid_ln or ()))


def _ln_bwd_math(do, z, g):
    mu = jnp.mean(z, axis=-1, keepdims=True)
    zc = z - mu
    var = jnp.mean(zc * zc, axis=-1, keepdims=True)
    rstd = lax.rsqrt(var + LN_EPS)
    xhat = zc * rstd
    dxh = do * g
    m1 = jnp.mean(dxh, axis=-1, keepdims=True)
    m2 = jnp.mean(dxh * xhat, axis=-1, keepdims=True)
    return rstd * (dxh - m1 - xhat * m2), _colsum(do * xhat), _colsum(do)


def _mm_ln_bwd(parts, w, resid, resid_scale, z, g, *, name, w_rows_are_k=False):
    t, kp = parts[0].shape
    k, d = w.shape if w_rows_are_k else w.shape[::-1]
    n = len(parts)
    tm = _tile(t, min(512, max(256, MM_LHS_ELEMS // k)), SUBLANES)

    def body(*refs):
        a_refs = refs[:n]
        w_ref, r_ref, z_ref, g_ref, dz_ref, dzbf_ref, dg_ref, db_ref = refs[n:]

        @pl.when(pl.program_id(0) == 0)
        def _():
            dg_ref[...] = jnp.zeros_like(dg_ref)
            db_ref[...] = jnp.zeros_like(db_ref)

        dx = resid_scale * r_ref[...]
        for p, a_ref in enumerate(a_refs):
            if w_rows_are_k:
                dx = dx + jnp.dot(a_ref[...], w_ref[p * kp:(p + 1) * kp, :], preferred_element_type=F32)
            else:
                dx = dx + lax.dot_general(a_ref[...], w_ref[:, p * kp:(p + 1) * kp], (((1,), (1,)), ((), ())),
                                          preferred_element_type=F32)
        dz, dg, db = _ln_bwd_math(dx, z_ref[...], g_ref[...])
        dz_ref[...] = dz
        dzbf_ref[...] = dz.astype(BF16)
        dg_ref[...] += dg
        db_ref[...] += db

    row = pl.BlockSpec((tm, d), lambda i: (i, 0))
    vec = pl.BlockSpec((1, d), lambda i: (0, 0))
    return pl.pallas_call(
        body, name=name, grid=(t // tm,),
        in_specs=[pl.BlockSpec((tm, kp), lambda i: (i, 0))] * n
        + [pl.BlockSpec(w.shape, lambda i: (0, 0), pipeline_mode=pl.Buffered(1)), row, row, vec],
        out_specs=[row, row, vec, vec],
        out_shape=[jax.ShapeDtypeStruct((t, d), F32), jax.ShapeDtypeStruct((t, d), BF16),
                   jax.ShapeDtypeStruct((1, d), F32), jax.ShapeDtypeStruct((1, d), F32)],
        compiler_params=_params(("arbitrary",)),
    )(*parts, w, resid, z, g)


def _mm_tn(a, b, *, name, below=None, tm=1408, tn=1536, tk=2048):
    t, m = a.shape
    n = b.shape[1]
    tm = _tile(m, tm, LANES)
    tn = _tile(n, tn, LANES)
    tk = _tile(t, tk, SUBLANES)
    last = t // tk - 1
    rows, earlier = (m, None) if below is None else below
    skip = (rows - m) // tm if earlier is not None else 0

    def body(a_ref, b_ref, *rest):
        o_ref, acc = rest[-2:]

        @pl.when(pl.program_id(2) == 0)
        def _():
            acc[...] = jnp.zeros_like(acc)

        acc[...] += lax.dot_general(a_ref[...], b_ref[...], (((0,), (0,)), ((), ())), preferred_element_type=F32)

        @pl.when(pl.program_id(2) == last)
        def _():
            o_ref[...] = acc[...].astype(BF16)

    in_specs = [pl.BlockSpec((tk, tm), lambda i, j, l: (l, i)), pl.BlockSpec((tk, tn), lambda i, j, l: (l, j))]
    return pl.pallas_call(
        body, name=name, grid=(m // tm, n // tn, t // tk),
        in_specs=in_specs + ([] if earlier is None else [pl.BlockSpec(memory_space=pl.ANY)]),
        out_specs=pl.BlockSpec((tm, tn), lambda i, j, l: (i + skip, j)),
        out_shape=jax.ShapeDtypeStruct((rows, n), BF16),
        input_output_aliases={} if earlier is None else {2: 0},
        scratch_shapes=[pltpu.VMEM((tm, tn), F32)],
        compiler_params=_params(("parallel", "parallel", "arbitrary")),
    )(a, b, *(() if earlier is None else (earlier,)))


def _ln_bwd(dout, z, g, *, name, tm=512):
    t, d = z.shape
    tm = _tile(t, tm, SUBLANES)

    def body(do_ref, z_ref, g_ref, dz_ref, dzbf_ref, dg_ref, db_ref):
        @pl.when(pl.program_id(0) == 0)
        def _():
            dg_ref[...] = jnp.zeros_like(dg_ref)
            db_ref[...] = jnp.zeros_like(db_ref)

        dz, dg, db = _ln_bwd_math(do_ref[...], z_ref[...], g_ref[...])
        dz_ref[...] = dz
        dzbf_ref[...] = dz.astype(BF16)
        dg_ref[...] += dg
        db_ref[...] += db

    row = pl.BlockSpec((tm, d), lambda i: (i, 0))
    vec = pl.BlockSpec((1, d), lambda i: (0, 0))
    return pl.pallas_call(
        body, name=name, grid=(t // tm,), in_specs=[row, row, vec], out_specs=[row, row, vec, vec],
        out_shape=[jax.ShapeDtypeStruct((t, d), F32), jax.ShapeDtypeStruct((t, d), BF16),
                   jax.ShapeDtypeStruct((1, d), F32), jax.ShapeDtypeStruct((1, d), F32)],
        compiler_params=_params(("arbitrary",)),
    )(dout, z, g)


def _loss_head(z, g, b, target, *, name, tm=512):
    t, d = z.shape
    tm = _tile(t, tm, SUBLANES)

    def body(z_ref, g_ref, b_ref, t_ref, s_ref, dy_ref):
        @pl.when(pl.program_id(0) == 0)
        def _():
            s_ref[...] = jnp.zeros_like(s_ref)

        e = _ln(z_ref[...], g_ref[...], b_ref[...]) - t_ref[...]
        dy_ref[...] = e * (1.0 / d)
        s_ref[...] += jnp.sum(_colsum(e * e), axis=-1, keepdims=True)

    row = pl.BlockSpec((tm, d), lambda i: (i, 0))
    vec = pl.BlockSpec((1, d), lambda i: (0, 0))
    return pl.pallas_call(
        body, name=name, grid=(t // tm,), in_specs=[row, vec, vec, row],
        out_specs=[pl.BlockSpec((1, LANES), lambda i: (0, 0)), row],
        out_shape=[jax.ShapeDtypeStruct((1, LANES), F32), jax.ShapeDtypeStruct((t, d), F32)],
        compiler_params=_params(("arbitrary",)),
    )(z, g, b, target)


def _own(c, b, *_):
    return c, b


def _ahead(nc, bsz):
    def at(c, b, part):
        b2 = b + jnp.minimum(part, 1)
        return jnp.minimum(c + b2 // bsz, nc - 1), b2 % bsz
    return at


def _strip(s, tc, off, at=_own):
    def index(*ids):
        c, b = at(*ids)
        return b, 0, off + c
    return pl.BlockSpec((None, s, tc), index)


def _cvec(kw, tc, off, at=_own):
    def index(*ids):
        return 0, off + at(*ids)[0]
    return pl.BlockSpec((kw, tc), index)


def _acc(kw, tc):
    return pl.BlockSpec((kw, tc), lambda c, b, *_: (0, c))


def _sc_fwd(h, cw, cb, *, name, tc=256):
    bsz, s, d3 = h.shape
    d = d3 // 3
    tc = _tile(d, tc, LANES)
    nc = d // tc

    def body(gb_ref, gc_ref, v_ref, w_ref, b_ref, q_ref):
        u = _conv_fwd(gc_ref[...] * v_ref[...], w_ref[...], b_ref[...])
        q_ref[...] = (gb_ref[...] * u).astype(BF16)

    return pl.pallas_call(
        body, name=name, grid=(nc, bsz),
        in_specs=[_strip(s, tc, 0), _strip(s, tc, nc), _strip(s, tc, 2 * nc), _cvec(cw.shape[0], tc, 0), _cvec(1, tc, 0)],
        out_specs=_strip(s, tc, 0),
        out_shape=jax.ShapeDtypeStruct((bsz, s, d), BF16),
        compiler_params=_params(("parallel", "parallel")),
    )(h, h, h, cw, cb)


def _sc_bwd(h, dq, cw, cb, *, name, tc=256):
    bsz, s, d3 = h.shape
    d = d3 // 3
    kw = cw.shape[0]
    tc = _tile(d, tc, LANES)
    nc = d // tc

    def body(gb_ref, gc_ref, v_ref, dq_ref, w_ref, b_ref, dh_ref, dw_ref, db_ref, parts):
        b_id, part = pl.program_id(1), pl.program_id(2)

        @pl.when(part == 0)
        def _():
            gb, gc, v, dq_, w = gb_ref[...], gc_ref[...], v_ref[...], dq_ref[...], w_ref[...]
            p = gc * v
            u = _conv_fwd(p, w, b_ref[...])
            du = dq_ * gb
            dp, dw_rows = _conv_bwd(du, p, w)
            parts[0] = (dq_ * u).astype(BF16)
            parts[1] = (dp * v).astype(BF16)
            parts[2] = (dp * gc).astype(BF16)
            _accumulate(b_id == 0, [(dw_ref, dw_rows), (db_ref, _colsum(du))])

        dh_ref[...] = parts[part]

    at = _ahead(nc, bsz)
    return pl.pallas_call(
        body, name=name, grid=(nc, bsz, 3),
        in_specs=[_strip(s, tc, 0, at), _strip(s, tc, nc, at), _strip(s, tc, 2 * nc, at), _strip(s, tc, 0, at),
                  _cvec(kw, tc, 0, at), _cvec(1, tc, 0, at)],
        out_specs=[pl.BlockSpec((None, s, tc), lambda c, b, p: (b, 0, p * nc + c)), _acc(kw, tc), _acc(1, tc)],
        out_shape=[jax.ShapeDtypeStruct((bsz, s, d3), BF16), jax.ShapeDtypeStruct((kw, d), F32),
                   jax.ShapeDtypeStruct((1, d), F32)],
        scratch_shapes=[pltpu.VMEM((3, s, tc), BF16)],
        compiler_params=_params(("parallel", "arbitrary", "arbitrary")),
    )(h, h, h, dq, cw, cb)


def _ffn_specs(s, tc, nc, kw):
    strip = pl.BlockSpec((None, s, tc), lambda b, c: (b, 0, c))
    halves = [pl.BlockSpec((kw, tc), lambda b, c: (0, c)), pl.BlockSpec((kw, tc), lambda b, c: (0, nc + c)),
              pl.BlockSpec((1, tc), lambda b, c: (0, c)), pl.BlockSpec((1, tc), lambda b, c: (0, nc + c))]
    return strip, halves


def _ffn_fwd(x, w_up, cw, cb, *, name, tc=256):
    bsz, s, d = x.shape
    f = w_up.shape[0] // 2
    kw = cw.shape[0]
    tc = _tile(f, tc, LANES)
    nc = f // tc
    nt = (((1,), (1,)), ((), ()))

    def body(x_ref, w_ref, wg_ref, wv_ref, bg_ref, bv_ref, hg_ref, hv_ref, g_ref, v_ref, a_ref):
        c0 = pl.multiple_of(pl.program_id(1) * tc, LANES)
        xs = x_ref[...]
        hg = lax.dot_general(xs, w_ref[pl.ds(c0, tc), :], nt, preferred_element_type=F32)
        hv = lax.dot_general(xs, w_ref[pl.ds(f + c0, tc), :], nt, preferred_element_type=F32)
        hg_ref[...] = hg
        hv_ref[...] = hv
        g = _conv_fwd(hg, wg_ref[...], bg_ref[...])
        v = _conv_fwd(hv, wv_ref[...], bv_ref[...])
        g_ref[...] = g
        v_ref[...] = v
        a_ref[...] = (g * _sigmoid(g) * v).astype(BF16)

    strip, halves = _ffn_specs(s, tc, nc, kw)
    return pl.pallas_call(
        body, name=name, grid=(bsz, nc),
        in_specs=[pl.BlockSpec((None, s, d), lambda b, c: (b, 0, 0)),
                  pl.BlockSpec(w_up.shape, lambda b, c: (0, 0), pipeline_mode=pl.Buffered(1))] + halves,
        out_specs=[strip] * 5,
        out_shape=[jax.ShapeDtypeStruct((bsz, s, f), F32)] * 4 + [jax.ShapeDtypeStruct((bsz, s, f), BF16)],
        compiler_params=_params(("parallel", "arbitrary")),
    )(x, w_up, cw, cw, cb, cb)


def _ffn_bwd(hg, hv, g, v, dz, w_down, cw, *, name, tc=256):
    bsz, s, f = hg.shape
    d = dz.shape[2]
    kw = cw.shape[0]
    tc = _tile(f, tc, LANES)
    nc = f // tc

    def body(hg_ref, hv_ref, g_ref, v_ref, dz_ref, wd_ref, wg_ref, wv_ref,
             dhg_ref, dhv_ref, dwg_ref, dwv_ref, dbg_ref, dbv_ref):
        c0 = pl.multiple_of(pl.program_id(1) * tc, LANES)
        cols = pl.ds(c0, tc)
        da = lax.dot_general(dz_ref[...], wd_ref[cols, :], (((1,), (1,)), ((), ())), preferred_element_type=F32)
        g_ = g_ref[...]
        sg = _sigmoid(g_)
        dv = da * (g_ * sg)
        dg = da * v_ref[...] * (sg * (1.0 + g_ * (1.0 - sg)))
        dhg, dwg_rows = _conv_bwd(dg, hg_ref[...], wg_ref[...])
        dhv, dwv_rows = _conv_bwd(dv, hv_ref[...], wv_ref[...])
        dhg_ref[...] = dhg.astype(BF16)
        dhv_ref[...] = dhv.astype(BF16)
        _accumulate(pl.program_id(0) == 0, [(dwg_ref, dwg_rows), (dwv_ref, dwv_rows),
                                            (dbg_ref, [_colsum(dg)]), (dbv_ref, [_colsum(dv)])], cols)

    strip, halves = _ffn_specs(s, tc, nc, kw)
    whole = lambda r: pl.BlockSpec((r, f), lambda b, c: (0, 0))
    return pl.pallas_call(
        body, name=name, grid=(bsz, nc),
        in_specs=[strip] * 4 + [pl.BlockSpec((None, s, d), lambda b, c: (b, 0, 0)),
                                pl.BlockSpec(w_down.shape, lambda b, c: (0, 0), pipeline_mode=pl.Buffered(1))]
        + halves[:2],
        out_specs=[strip, strip, whole(kw), whole(kw), whole(1), whole(1)],
        out_shape=[jax.ShapeDtypeStruct((bsz, s, f), BF16), jax.ShapeDtypeStruct((bsz, s, f), BF16),
                   jax.ShapeDtypeStruct((kw, f), F32), jax.ShapeDtypeStruct((kw, f), F32),
                   jax.ShapeDtypeStruct((1, f), F32), jax.ShapeDtypeStruct((1, f), F32)],
        compiler_params=_params(("arbitrary", "arbitrary")),
    )(hg, hv, g, v, dz, w_down, cw, cw)


def _lru_gates(r, cw, cb, wg, bg, lam):
    blk = r.shape[1]
    xr = _conv_fwd(r, cw, cb)
    gates = jnp.dot(xr.astype(BF16), wg, preferred_element_type=F32) + bg
    rg = _sigmoid(gates[:, :blk])
    ig = _sigmoid(gates[:, blk:])
    sp = _softplus(-lam)
    la = (-LRU_C * sp) * rg
    a = jnp.exp(la)
    mult = jnp.sqrt(-_expm1(2.0 * la, a * a))
    return xr, rg, ig, sp, a, mult


def _lru_fwd(h, cw, cb, wg, bg, lam, *, name):
    bsz, s, r2 = h.shape
    heads, blk = wg.shape[0], wg.shape[1]
    kw = cw.shape[0]

    def body(g_ref, r_ref, cw_ref, cb_ref, wg_ref, bg_ref, lam_ref, y_ref, sv_ref):
        xr, rg, ig, _, a, mult = _lru_gates(r_ref[...], cw_ref[...], cb_ref[...], wg_ref[...], bg_ref[...], lam_ref[...])
        hs = _scan_fwd(a, mult * (ig * xr))
        for n, val in enumerate((hs, xr, rg, ig, a, mult)):
            sv_ref[n] = val
        y_ref[...] = (hs * _gelu(g_ref[...])).astype(BF16)

    per_head = lambda hd, b: (hd, 0, 0)
    return pl.pallas_call(
        body, name=name, grid=(heads, bsz),
        in_specs=[_strip(s, blk, 0), _strip(s, blk, heads), _cvec(kw, blk, 0), _cvec(1, blk, 0),
                  pl.BlockSpec((None, blk, 2 * blk), per_head), pl.BlockSpec((None, 1, 2 * blk), per_head),
                  _cvec(1, blk, 0)],
        out_specs=[_strip(s, blk, 0), pl.BlockSpec((6, None, s, blk), lambda hd, b: (0, b, 0, hd))],
        out_shape=[jax.ShapeDtypeStruct((bsz, s, r2 // 2), BF16), jax.ShapeDtypeStruct((6, bsz, s, r2 // 2), F32)],
        compiler_params=_params(("parallel", "parallel")),
    )(h, h, cw, cb, wg, bg, lam)


def _lru_bwd(h, sv, dy, cw, wg, lam, *, name):
    bsz, s, r2 = h.shape
    rw = r2 // 2
    heads, blk = wg.shape[0], wg.shape[1]
    kw = cw.shape[0]

    def body(g_ref, r_ref, cw_ref, wg_ref, lam_ref, sv_ref, dy_ref,
             dh_ref, dcw_ref, dcb_ref, dwg_ref, dbg_ref, dlam_ref, sg_ref, sr_ref, parts):
        b_id, part = pl.program_id(1), pl.program_id(2)

        @pl.when(part == 0)
        def _():
            r, cw_, wg_, lam_ = r_ref[...], cw_ref[...], wg_ref[...], lam_ref[...]
            hs_, xr, rg, ig, a, mult = (sv_ref[n] for n in range(6))
            sp = _softplus(-lam_)
            dy_ = dy_ref[...]
            gel, dgel = _gelu_and_grad(g_ref[...])
            dg = dy_ * hs_ * dgel
            lmb = _scan_rev(_shift_up(a, 1, 1.0), dy_ * gel)
            da = lmb * _shift_dn(hs_, 1)
            dmult = lmb * (ig * xr)
            dig = lmb * (mult * xr)
            dxr = lmb * (mult * ig)
            dla = da * a - dmult * (a * a / mult)
            drg = dla * (-LRU_C * sp)
            dsp = _colsum(dla * rg) * (-LRU_C)
            dlam = -dsp * _sigmoid(-lam_)
            dgates = jnp.concatenate([drg * (rg * (1.0 - rg)), dig * (ig * (1.0 - ig))], axis=1)
            dgates_bf = dgates.astype(BF16)
            dwg = lax.dot_general(xr.astype(BF16), dgates_bf, (((0,), (0,)), ((), ())), preferred_element_type=F32)
            dxr = dxr + lax.dot_general(dgates_bf, wg_, (((1,), (1,)), ((), ())), preferred_element_type=F32)
            dr, dcw_rows = _conv_bwd(dxr, r, cw_)
            parts[0] = dg.astype(BF16)
            parts[1] = dr.astype(BF16)
            _accumulate(b_id == 0, [(dcw_ref, dcw_rows), (dcb_ref, _colsum(dxr)), (dwg_ref, dwg),
                                    (dbg_ref, _colsum(dgates)), (dlam_ref, dlam), (sg_ref, _colsum(dg)),
                                    (sr_ref, _colsum(dr))])

        dh_ref[...] = parts[part]

    at = _ahead(heads, bsz)

    def saved(*ids):
        hd, b = at(*ids)
        return 0, b, 0, hd

    vec = pl.BlockSpec((1, blk), lambda hd, b, p: (0, hd))
    return pl.pallas_call(
        body, name=name, grid=(heads, bsz, 2),
        in_specs=[_strip(s, blk, 0, at), _strip(s, blk, heads, at), _cvec(kw, blk, 0, at),
                  pl.BlockSpec((None, blk, 2 * blk), lambda *ids: (at(*ids)[0], 0, 0)), _cvec(1, blk, 0, at),
                  pl.BlockSpec((6, None, s, blk), saved), _strip(s, blk, 0, at)],
        out_specs=[pl.BlockSpec((None, s, blk), lambda hd, b, p: (b, 0, p * heads + hd)),
                   pl.BlockSpec((kw, blk), lambda hd, b, p: (0, hd)), vec,
                   pl.BlockSpec((None, blk, 2 * blk), lambda hd, b, p: (hd, 0, 0)),
                   pl.BlockSpec((None, 1, 2 * blk), lambda hd, b, p: (hd, 0, 0)), vec, vec, vec],
        out_shape=[jax.ShapeDtypeStruct((bsz, s, r2), BF16), jax.ShapeDtypeStruct((kw, rw), F32),
                   jax.ShapeDtypeStruct((1, rw), F32), jax.ShapeDtypeStruct((heads, blk, 2 * blk), F32),
                   jax.ShapeDtypeStruct((heads, 1, 2 * blk), F32), jax.ShapeDtypeStruct((1, rw), F32),
                   jax.ShapeDtypeStruct((1, rw), F32), jax.ShapeDtypeStruct((1, rw), F32)],
        scratch_shapes=[pltpu.VMEM((2, s, blk), BF16)],
        compiler_params=_params(("parallel", "arbitrary", "arbitrary")),
    )(h, h, cw, wg, lam, sv, dy)


HBM_SPEC = pl.BlockSpec(memory_space=pltpu.HBM)
SEM_SPEC = pl.BlockSpec(memory_space=pltpu.SEMAPHORE)
EFFECT = pltpu.SideEffectType.DATAFLOW_SIDE_EFFECTING


def _peer_copies(srcs, lands, gather, send_sem, recv_sem):
    x, y, c = (lax.axis_index(ax) for ax in MESH_AXES)
    me = 4 * x + 2 * y + c
    copies = []
    for i in range(len(srcs)):
        for d in range(1, N_DEV):
            px = 1 - x if d & 4 else x
            py = 1 - y if d & 2 else y
            pc = 1 - c if d & 1 else c
            src = srcs[i] if gather[i] else srcs[i].at[4 * px + 2 * py + pc]
            k = i * (N_DEV - 1) + d - 1
            copies.append(pltpu.make_async_remote_copy(
                src_ref=src, dst_ref=lands[i].at[me], send_sem=send_sem.at[k], recv_sem=recv_sem.at[k],
                device_id=(px, py, pc), device_id_type=pl.DeviceIdType.MESH))
    return copies


def _exchange_start(arrs, gather, *, name):
    n = len(arrs)
    lands = [lax.empty((N_DEV,) + tuple(a.shape if g else a.shape[1:]), a.dtype) for a, g in zip(arrs, gather)]

    def body(*refs):
        srcs, land_refs = refs[:n], refs[n:2 * n]
        send_sem, recv_sem = refs[2 * n], refs[2 * n + 1]
        token = refs[-1]
        for cp in _peer_copies(srcs, land_refs, gather, send_sem, recv_sem):
            cp.start()
        token[...] = jnp.zeros_like(token)

    sems = pltpu.SemaphoreType.DMA((n * (N_DEV - 1),))
    thru = [pltpu.HBM(a.shape, a.dtype) for a in arrs + lands]
    out = pl.pallas_call(
        body, name=name, in_specs=[HBM_SPEC] * (2 * n),
        out_shape=(sems, sems, *thru, jax.ShapeDtypeStruct((SUBLANES, LANES), F32)),
        out_specs=(SEM_SPEC, SEM_SPEC, *([HBM_SPEC] * (2 * n)), pl.BlockSpec(memory_space=pltpu.VMEM)),
        input_output_aliases={i: 2 + i for i in range(2 * n)},
        compiler_params=pltpu.CompilerParams(has_side_effects=EFFECT),
    )(*[pltpu.with_memory_space_constraint(a, pltpu.HBM) for a in arrs + lands])
    return {"send_sem": out[0], "recv_sem": out[1], "srcs": list(out[2:2 + n]), "lands": list(out[2 + n:2 + 2 * n]),
            "token": out[-1], "gather": list(gather)}


def _exchange_wait(handle, after, *, name):
    srcs, lands, gather = handle["srcs"], handle["lands"], handle["gather"]
    n = len(srcs)

    def body(*refs):
        src_refs, land_refs = refs[:n], refs[n:2 * n]
        send_sem, recv_sem = refs[2 * n], refs[2 * n + 1]
        for cp in _peer_copies(src_refs, land_refs, gather, send_sem, recv_sem):
            cp.wait_send()
            cp.wait_recv()

    out = pl.pallas_call(
        body, name=name,
        in_specs=[HBM_SPEC] * (2 * n) + [SEM_SPEC, SEM_SPEC, pl.BlockSpec(memory_space=pl.ANY)],
        out_shape=tuple(pltpu.HBM(a.shape, a.dtype) for a in srcs + lands), out_specs=tuple([HBM_SPEC] * (2 * n)),
        input_output_aliases={i: i for i in range(2 * n)},
        compiler_params=pltpu.CompilerParams(has_side_effects=EFFECT),
    )(*srcs, *lands, handle["send_sem"], handle["recv_sem"], after)
    return list(out[:n]), list(out[n:])


def _layers_bf16(stacks, *, name):
    counts = [a.shape[0] for a in stacks]

    def body(*refs):
        outs = iter(refs[len(stacks):])
        for i_ref, n_layers in zip(refs, counts):
            for layer in range(n_layers):
                next(outs)[...] = i_ref[layer].astype(BF16)

    flat = pl.pallas_call(
        body, name=name,
        out_shape=[jax.ShapeDtypeStruct(a.shape[1:], BF16) for a in stacks for _ in range(a.shape[0])],
        compiler_params=pltpu.CompilerParams(vmem_limit_bytes=VMEM_LIMIT),
    )(*stacks)
    split, pos = [], 0
    for n_layers in counts:
        split.append(list(flat[pos:pos + n_layers]))
        pos += n_layers
    return split


def _adamw(parts, w, m, v, layer, so_far, *, name, tr=256):
    n_layers, r, c = w.shape
    tr = _tile(r, tr, SUBLANES)
    bc1 = 1.0 / (1.0 - ADAM_B1 ** ADAM_STEP)
    bc2 = 1.0 / (1.0 - ADAM_B2 ** ADAM_STEP)
    if so_far is None:
        so_far = [lax.empty(w.shape, F32) for _ in range(4)]

    def body(p_ref, w_ref, m_ref, v_ref, *rest):
        g_ref, d_ref, mo_ref, vo_ref = rest[4:]
        g = p_ref[0].astype(F32)
        for s in range(1, N_DEV):
            g = g + p_ref[s].astype(F32)
        m_new = ADAM_B1 * m_ref[...] + (1.0 - ADAM_B1) * g
        v_new = ADAM_B2 * v_ref[...] + (1.0 - ADAM_B2) * (g * g)
        g_ref[...] = g
        mo_ref[...] = m_new
        vo_ref[...] = v_new
        d_ref[...] = -ADAM_LR * ((m_new * bc1) / (jnp.sqrt(v_new * bc2) + ADAM_EPS) + ADAM_WD * w_ref[...])

    blk = pl.BlockSpec((None, tr, c), lambda i: (layer, i, 0))
    return pl.pallas_call(
        body, name=name, grid=(r // tr,),
        in_specs=[pl.BlockSpec((N_DEV, tr, c), lambda i: (0, i, 0)), blk, blk, blk]
        + [pl.BlockSpec(memory_space=pl.ANY)] * 4,
        out_specs=[blk] * 4, out_shape=[jax.ShapeDtypeStruct(w.shape, F32)] * 4,
        input_output_aliases={4 + o: o for o in range(4)},
        compiler_params=_params(("parallel",)),
    )(parts, w, m, v, *so_far)


def _whole(slabs, axis):
    x = jnp.moveaxis(slabs, 0, axis)
    shp = x.shape
    return x.reshape(shp[:axis] + (shp[axis] * shp[axis + 1],) + shp[axis + 2:])


def _slabs(whole, axis):
    shp = whole.shape
    x = whole.reshape(shp[:axis] + (N_DEV, shp[axis] // N_DEV) + shp[axis + 1:])
    return jnp.moveaxis(x, axis, 0)


BIG = {"sc_w_in": 2, "sc_w_out": 1, "lru_w_in": 2, "lru_w_gate": 3, "lru_w_out": 1, "ffn_w_up": 2, "ffn_w_down": 1}
TRANSPOSED = ("ffn_w_up", "lru_w_in")
SMALL = ["sc_conv_w", "lru_b_in", "lru_conv_w", "lru_conv_b", "lru_b_gate", "lru_lambda", "ffn_conv_w", "ln_g", "ln_b"]
REPL = ["sc_conv_b", "ffn_conv_b"]
WEIGHTS = ["sc_w_in", "sc_conv_w", "sc_conv_b", "sc_w_out", "lru_w_in", "lru_b_in", "lru_conv_w", "lru_conv_b",
           "lru_w_gate", "lru_b_gate", "lru_lambda", "lru_w_out", "ffn_w_up", "ffn_conv_w", "ffn_conv_b", "ffn_w_down",
           "ln_g", "ln_b"]


STAGES_PER_LAYER = 3


def _stage_big(g):
    i, part = divmod(g, STAGES_PER_LAYER)
    j = i // 2
    if part:
        return [("ffn_w_up" if part == 1 else "ffn_w_down", i)]
    return [("sc_w_in", j), ("sc_w_out", j)] if i % 2 == 0 else [("lru_w_in", j), ("lru_w_gate", j), ("lru_w_out", j)]


def _step(x, loss_target, w, m, v):
    bsz, s, d = x.shape
    t = bsz * s
    depth = w["ffn_w_up"].shape[0]
    alpha = (2.0 * depth) ** 0.25
    heads = w["lru_w_gate"].shape[1]

    me = 4 * lax.axis_index("x") + 2 * lax.axis_index("y") + lax.axis_index("c")

    def with_own(land, own):
        return lax.dynamic_update_slice_in_dim(land, own, me, axis=0)

    stages = STAGES_PER_LAYER * depth
    def held(k, arr):
        return jnp.swapaxes(arr, -1, -2) if k in TRANSPOSED else arr

    def split_axis(k):
        return 0 if k in TRANSPOSED else BIG[k] - 1

    flat_names = [k for k in BIG if w[k].ndim == 3]
    wb = dict(zip(flat_names, _layers_bf16([held(k, w[k]) for k in flat_names], name="weights_bf16")))
    wb.update({k: list(w[k].astype(BF16)) for k in BIG if k not in flat_names})

    gathers, tok = [], None
    for g in range(stages):
        arrs = [wb[k][l] for k, l in _stage_big(g)]
        if g == 0:
            arrs += [w[k] for k in SMALL]
        if tok is not None:
            arrs[0] = arrs[0] + tok.astype(BF16)
        gathers.append(_exchange_start(arrs, [True] * len(arrs), name=f"gather_start_{g}"))
        tok = gathers[-1]["token"][0, 0]
    full = {k: [None] * w[k].shape[0] for k in BIG}
    full["sc_conv_b"] = w["sc_conv_b"]
    full["ffn_conv_b"] = w["ffn_conv_b"]

    def arrive(g, after):
        srcs, lands = _exchange_wait(gathers[g], after, name=f"gather_wait_{g}")
        for (k, l), src, land in zip(_stage_big(g), srcs, lands):
            full[k][l] = _whole(with_own(land, src[None]), split_axis(k))
        if g == 0:
            n_big = len(_stage_big(0))
            for k, src, land in zip(SMALL, srcs[n_big:], lands[n_big:]):
                full[k] = _whole(with_own(land, src[None]), w[k].ndim - 1)

    stream, stream_ln = x.reshape(t, d), None
    xb = stream.astype(BF16)
    saved = []
    for i in range(depth):
        j = i // 2
        arrive(3 * i, gathers[-1]["token"] if i == 0 else xb)
        lng, lnb = full["ln_g"][i], full["ln_b"][i]
        sv = {"x0": xb}
        if i % 2 == 0:
            hm = _mm(xb, full["sc_w_in"][j], name="sc_in")
            q = _sc_fwd(hm.reshape(bsz, s, -1), full["sc_conv_w"][j], full["sc_conv_b"][j:j + 1], name="sc_mix")
            w_out = full["sc_w_out"][j]
        else:
            hm = _mm(xb, full["lru_w_in"][j], trans_w=True, bias=full["lru_b_in"][j:j + 1], name="lru_in")
            q, hs = _lru_fwd(hm.reshape(bsz, s, -1), full["lru_conv_w"][j], full["lru_conv_b"][j:j + 1],
                             full["lru_w_gate"][j], full["lru_b_gate"][j].reshape(heads, 1, -1),
                             full["lru_lambda"][j:j + 1], name="lru_mix")
            sv["hs"] = hs
            w_out = full["lru_w_out"][j]
        q = q.reshape(t, -1)
        arrive(3 * i + 1, q)
        z1, x1b = _mm_ln(q, w_out, stream, alpha, lng[0:1], lnb[0:1], resid_ln=stream_ln, name="mix_out_ln")
        hg, hv, gc, vc, a = _ffn_fwd(x1b.reshape(bsz, s, d), full["ffn_w_up"][i], full["ffn_conv_w"][i],
                                     full["ffn_conv_b"][i:i + 1], name="ffn_up_act")
        a = a.reshape(t, -1)
        arrive(3 * i + 2, a)
        z2, xb = _mm_ln(a, full["ffn_w_down"][i], z1, alpha, lng[1:2], lnb[1:2], resid_ln=(lng[0:1], lnb[0:1]),
                        name="ffn_down_ln")
        stream, stream_ln = z2, (lng[1:2], lnb[1:2])
        sv.update(hm=hm, q=q, z1=z1, x1=x1b, ffn=(hg, hv, gc, vc), a=a, z2=z2)
        saved.append(sv)

    sq, dx = _loss_head(stream, *stream_ln, loss_target.reshape(t, d), name="loss_head")
    loss = lax.psum((0.5 / d) * sq[0, 0], MESH_AXES)

    grads = {k: [None] * w[k].shape[0] for k in WEIGHTS}
    scatters = [None] * stages

    def depart(g):
        send = [_slabs(grads[k][l], split_axis(k)).astype(BF16) for k, l in _stage_big(g)]
        scatters[g] = _exchange_start(send, [False] * len(send), name=f"scatter_start_{g}")
        return scatters[g]["token"][0:1, 0:1]

    dz2, dz2b, dg2, db2 = _ln_bwd(dx, saved[-1]["z2"], full["ln_g"][-1][1:2], name="ln_bwd")
    for i in reversed(range(depth)):
        j = i // 2
        sv = saved[i]
        lng = full["ln_g"][i]
        grads["ffn_w_down"][i] = _mm_tn(sv["a"], dz2b, name="ffn_down_dw")
        dhg, dhv, dwg, dwv, dbg, dbv = _ffn_bwd(*sv["ffn"], dz2b.reshape(bsz, s, d), full["ffn_w_down"][i],
                                                full["ffn_conv_w"][i] + depart(3 * i + 2), name="ffn_act_bwd")
        dhg, dhv = dhg.reshape(t, -1), dhv.reshape(t, -1)
        grads["ffn_conv_w"][i] = jnp.concatenate([dwg, dwv], axis=1)
        grads["ffn_conv_b"][i] = jnp.concatenate([dbg, dbv], axis=1)[0]
        rows_up = 2 * dhg.shape[1]
        dw_g = _mm_tn(dhg, sv["x1"], below=(rows_up, None), name="ffn_up_dw_g")
        grads["ffn_w_up"][i] = _mm_tn(dhv, sv["x1"], below=(rows_up, dw_g), name="ffn_up_dw_v")
        dz1, dz1b, dg1, db1 = _mm_ln_bwd([dhg, dhv], full["ffn_w_up"][i], dz2, alpha, sv["z1"],
                                         lng[0:1] + depart(3 * i + 1), name="ffn_up_dx_ln", w_rows_are_k=True)
        grads["ln_g"][i] = jnp.concatenate([dg1, dg2], axis=0)
        grads["ln_b"][i] = jnp.concatenate([db1, db2], axis=0)
        if i % 2 == 0:
            dq = _mm(dz1b, full["sc_w_out"][j], trans_w=True, name="sc_out_dx")
            grads["sc_w_out"][j] = _mm_tn(sv["q"], dz1b, name="sc_out_dw")
            dhm, dcw, dcb = _sc_bwd(sv["hm"].reshape(bsz, s, -1), dq.reshape(bsz, s, -1), full["sc_conv_w"][j],
                                    full["sc_conv_b"][j:j + 1], name="sc_mix_bwd")
            dhm = dhm.reshape(t, -1)
            grads["sc_conv_w"][j] = dcw
            grads["sc_conv_b"][j] = dcb[0]
            grads["sc_w_in"][j] = _mm_tn(sv["x0"], dhm, name="sc_in_dw")
            w_in = full["sc_w_in"][j]
        else:
            dq = _mm(dz1b, full["lru_w_out"][j], trans_w=True, name="lru_out_dx")
            grads["lru_w_out"][j] = _mm_tn(sv["q"], dz1b, name="lru_out_dw")
            dhm, dcw, dcb, dwgt, dbgt, dlam, sgb, srb = _lru_bwd(
                sv["hm"].reshape(bsz, s, -1), sv["hs"], dq.reshape(bsz, s, -1), full["lru_conv_w"][j],
                full["lru_w_gate"][j], full["lru_lambda"][j:j + 1], name="lru_mix_bwd")
            dhm = dhm.reshape(t, -1)
            grads["lru_conv_w"][j] = dcw
            grads["lru_conv_b"][j] = dcb[0]
            grads["lru_w_gate"][j] = dwgt
            grads["lru_b_gate"][j] = dbgt[:, 0, :]
            grads["lru_lambda"][j] = dlam[0]
            grads["lru_b_in"][j] = jnp.concatenate([sgb, srb], axis=1)[0]
            grads["lru_w_in"][j] = _mm_tn(dhm, sv["x0"], name="lru_in_dw")
            w_in = full["lru_w_in"][j]
        tok = depart(3 * i)
        if i > 0:
            dz2, dz2b, dg2, db2 = _mm_ln_bwd([dhm], w_in, dz1, alpha, saved[i - 1]["z2"], full["ln_g"][i - 1][1:2] + tok,
                                             name="mix_in_dx_ln", w_rows_are_k=i % 2 == 1)
        else:
            dx = _mm(dhm, w_in + tok[0, 0].astype(BF16), trans_w=True, resid=dz1, resid_scale=alpha, name="mix_in_dx")
    grad_x = dx.reshape(bsz, s, d)

    gsm = {k: jnp.stack(grads[k]) for k in SMALL + REPL}
    small_scatter = _exchange_start([_slabs(gsm[k], gsm[k].ndim - 1) for k in SMALL] + [gsm[k] for k in REPL],
                                    [False] * len(SMALL) + [True] * len(REPL), name="scatter_start_small")

    out = {}

    def own_slab(src):
        return lax.dynamic_slice_in_dim(src, me, 1, axis=0)

    stacks = {k: None for k in BIG}
    after = dx
    for g in reversed(range(stages)):
        srcs, lands = _exchange_wait(scatters[g], after, name=f"scatter_wait_{g}")
        for (k, l), src, land in zip(_stage_big(g), srcs, lands):
            n_l, c2 = w[k].shape[0], land.shape[-1]
            wk, mk, vk = (held(k, arr[k]).reshape(n_l, -1, c2) for arr in (w, m, v))
            stacks[k] = _adamw(with_own(land, own_slab(src)).reshape(N_DEV, -1, c2), wk, mk, vk, l, stacks[k],
                               name=f"adamw_{k}_{l}")
            after = stacks[k][-1]
    for k in BIG:
        shp = held(k, w[k]).shape
        out[k] = [held(k, r.reshape(shp)) for r in stacks[k]]
    srcs, lands = _exchange_wait(small_scatter, after, name="scatter_wait_small")
    for n, k in enumerate(SMALL + REPL):
        own = srcs[n][None] if k in REPL else own_slab(srcs[n])
        c2 = w[k].shape[-1]
        res = _adamw(with_own(lands[n], own).reshape(N_DEV, -1, c2), w[k].reshape(1, -1, c2), m[k].reshape(1, -1, c2),
                     v[k].reshape(1, -1, c2), 0, None, name="adamw_" + k)
        out[k] = [r.reshape(w[k].shape) for r in res]

    return (loss, grad_x, *[out[k][0] for k in WEIGHTS], *[out[k][1] for k in WEIGHTS],
            *[out[k][2] for k in WEIGHTS], *[out[k][3] for k in WEIGHTS])


def kernel(x, sc_w_in, sc_conv_w, sc_conv_b, sc_w_out, lru_w_in, lru_b_in, lru_conv_w, lru_conv_b, lru_w_gate, lru_b_gate, lru_lambda, lru_w_out, ffn_w_up, ffn_conv_w, ffn_conv_b, ffn_w_down, ln_g, ln_b, loss_target, m_sc_w_in, m_sc_conv_w, m_sc_conv_b, m_sc_w_out, m_lru_w_in, m_lru_b_in, m_lru_conv_w, m_lru_conv_b, m_lru_w_gate, m_lru_b_gate, m_lru_lambda, m_lru_w_out, m_ffn_w_up, m_ffn_conv_w, m_ffn_conv_b, m_ffn_w_down, m_ln_g, m_ln_b, v_sc_w_in, v_sc_conv_w, v_sc_conv_b, v_sc_w_out, v_lru_w_in, v_lru_b_in, v_lru_conv_w, v_lru_conv_b, v_lru_w_gate, v_lru_b_gate, v_lru_lambda, v_lru_w_out, v_ffn_w_up, v_ffn_conv_w, v_ffn_conv_b, v_ffn_w_down, v_ln_g, v_ln_b):
    w = dict(sc_w_in=sc_w_in, sc_conv_w=sc_conv_w, sc_conv_b=sc_conv_b, sc_w_out=sc_w_out, lru_w_in=lru_w_in,
             lru_b_in=lru_b_in, lru_conv_w=lru_conv_w, lru_conv_b=lru_conv_b, lru_w_gate=lru_w_gate,
             lru_b_gate=lru_b_gate, lru_lambda=lru_lambda, lru_w_out=lru_w_out, ffn_w_up=ffn_w_up,
             ffn_conv_w=ffn_conv_w, ffn_conv_b=ffn_conv_b, ffn_w_down=ffn_w_down, ln_g=ln_g, ln_b=ln_b)
    m = dict(sc_w_in=m_sc_w_in, sc_conv_w=m_sc_conv_w, sc_conv_b=m_sc_conv_b, sc_w_out=m_sc_w_out, lru_w_in=m_lru_w_in,
             lru_b_in=m_lru_b_in, lru_conv_w=m_lru_conv_w, lru_conv_b=m_lru_conv_b, lru_w_gate=m_lru_w_gate,
             lru_b_gate=m_lru_b_gate, lru_lambda=m_lru_lambda, lru_w_out=m_lru_w_out, ffn_w_up=m_ffn_w_up,
             ffn_conv_w=m_ffn_conv_w, ffn_conv_b=m_ffn_conv_b, ffn_w_down=m_ffn_w_down, ln_g=m_ln_g, ln_b=m_ln_b)
    v = dict(sc_w_in=v_sc_w_in, sc_conv_w=v_sc_conv_w, sc_conv_b=v_sc_conv_b, sc_w_out=v_sc_w_out, lru_w_in=v_lru_w_in,
             lru_b_in=v_lru_b_in, lru_conv_w=v_lru_conv_w, lru_conv_b=v_lru_conv_b, lru_w_gate=v_lru_w_gate,
             lru_b_gate=v_lru_b_gate, lru_lambda=v_lru_lambda, lru_w_out=v_lru_w_out, ffn_w_up=v_ffn_w_up,
             ffn_conv_w=v_ffn_conv_w, ffn_conv_b=v_ffn_conv_b, ffn_w_down=v_ffn_w_down, ln_g=v_ln_g, ln_b=v_ln_b)
    return _step(x, loss_target, w, m, v)
```

```python
import math

import jax
import jax.numpy as jnp
from jax import lax
from jax.experimental import pallas as pl
from jax.experimental.pallas import tpu as pltpu

F32 = jnp.float32
BF16 = jnp.bfloat16

N_DEV = 8
MESH_AXES = ("x", "y", "c")
LANES = 128
SUBLANES = 8
VMEM_LIMIT = 56 * 1024 * 1024
MM_LHS_ELEMS = 3 * 1024 * 1024
MM_TN = 1536

LRU_C = 8.0
LN_EPS = 1e-5
ADAM_LR = 0.001
ADAM_B1 = 0.9
ADAM_B2 = 0.999
ADAM_EPS = 1e-08
ADAM_WD = 0.01
ADAM_STEP = 10
GELU_K = math.sqrt(2.0 / math.pi)
GELU_C = 0.044715


def _tile(n, target, align):
    if n <= target:
        return n
    t = (target // align) * align
    while t >= align:
        if n % t == 0:
            return t
        t -= align
    return n


def _params(sem):
    return pltpu.CompilerParams(dimension_semantics=sem, vmem_limit_bytes=VMEM_LIMIT)


def _rows(x):
    return lax.broadcasted_iota(jnp.int32, x.shape, 0)


def _shift_dn(x, k, fill=0.0):
    if k == 0:
        return x
    return jnp.where(_rows(x) >= k, pltpu.roll(x, k, 0), fill)


def _shift_up(x, k, fill=0.0):
    if k == 0:
        return x
    s = x.shape[0]
    return jnp.where(_rows(x) < s - k, pltpu.roll(x, s - k, 0), fill)


def _conv_fwd(x, w, b):
    kw = w.shape[0]
    y = _shift_dn(x, kw - 1) * w[0:1, :] + b
    for k in range(1, kw):
        y = y + _shift_dn(x, kw - 1 - k) * w[k:k + 1, :]
    return y


def _conv_bwd(dy, x, w):
    kw = w.shape[0]
    ahead = [_shift_up(dy, j) for j in range(kw)]
    dx = ahead[kw - 1] * w[0:1, :]
    for k in range(1, kw):
        dx = dx + ahead[kw - 1 - k] * w[k:k + 1, :]
    return dx, [_colsum(ahead[kw - 1 - k] * x) for k in range(kw)]


def _accumulate(first, items, cols=slice(None)):
    flat = []
    for ref, val in items:
        if isinstance(val, list):
            flat += [(ref, (slice(k, k + 1), cols), row) for k, row in enumerate(val)]
        else:
            flat.append((ref, Ellipsis, val))

    @pl.when(first)
    def _():
        for ref, idx, val in flat:
            ref[idx] = val

    @pl.when(jnp.logical_not(first))
    def _():
        for ref, idx, val in flat:
            ref[idx] += val


def _colsum(x):
    return jnp.sum(x, axis=0, keepdims=True)


def _sigmoid(x):
    return 1.0 / (1.0 + jnp.exp(-x))


def _log1p(x):
    u = 1.0 + x
    return jnp.where(u == 1.0, x, jnp.log(u) * (x / (u - 1.0)))


def _softplus(x):
    return jnp.maximum(x, 0.0) + _log1p(jnp.exp(-jnp.abs(x)))


def _expm1(x, ex):
    poly = x * (1.0 + x * (0.5 + x * (1.0 / 6.0 + x * (1.0 / 24.0 + x * (1.0 / 120.0 + x * (1.0 / 720.0))))))
    return jnp.where(jnp.abs(x) < 0.25, poly, ex - 1.0)


def _gelu(x):
    t = jnp.tanh(GELU_K * (x + GELU_C * x * x * x))
    return 0.5 * x * (1.0 + t)


def _gelu_and_grad(x):
    x2 = x * x
    t = jnp.tanh(GELU_K * (x + GELU_C * x * x2))
    g = 0.5 * x * (1.0 + t)
    dg = 0.5 * (1.0 + t) + 0.5 * x * (1.0 - t * t) * (GELU_K * (1.0 + 3.0 * GELU_C * x2))
    return g, dg


def _scan_fwd(a, b):
    s = a.shape[0]
    k = 1
    while k < s:
        last = 2 * k >= s
        if k % SUBLANES:
            b = a * _shift_dn(b, k) + b
            if not last:
                a = a * _shift_dn(a, k, 1.0)
        else:
            b = jnp.concatenate([b[:k], a[k:] * b[:s - k] + b[k:]], axis=0)
            if not last:
                a = jnp.concatenate([a[:k], a[k:] * a[:s - k]], axis=0)
        k *= 2
    return b


def _scan_rev(c, v):
    s = c.shape[0]
    k = 1
    while k < s:
        last = 2 * k >= s
        if k % SUBLANES:
            v = c * _shift_up(v, k) + v
            if not last:
                c = c * _shift_up(c, k, 1.0)
        else:
            v = jnp.concatenate([c[:s - k] * v[k:] + v[:s - k], v[s - k:]], axis=0)
            if not last:
                c = jnp.concatenate([c[:s - k] * c[k:], c[s - k:]], axis=0)
        k *= 2
    return v


def _mm(a, w, *, name, trans_w=False, bias=None, resid=None, resid_scale=1.0):
    m, k = a.shape
    n = w.shape[0] if trans_w else w.shape[1]
    tm = _tile(m, min(1024, max(256, MM_LHS_ELEMS // k)), SUBLANES)
    tn = _tile(n, MM_TN, LANES)
    has_bias = bias is not None
    has_resid = resid is not None

    def body(*refs):
        a_ref, w_ref = refs[0], refs[1]
        pos = 2
        b_ref = r_ref = None
        if has_bias:
            b_ref = refs[pos]
            pos += 1
        if has_resid:
            r_ref = refs[pos]
            pos += 1
        o_ref = refs[pos]

        cols = pl.ds(pl.multiple_of(pl.program_id(1) * tn, LANES), tn)
        if trans_w:
            acc = lax.dot_general(a_ref[...], w_ref[cols, :], (((1,), (1,)), ((), ())), preferred_element_type=F32)
        else:
            acc = jnp.dot(a_ref[...], w_ref[:, cols], preferred_element_type=F32)
        if has_bias:
            acc = acc + b_ref[...]
        if has_resid:
            acc = acc + resid_scale * r_ref[...]
        o_ref[...] = acc

    in_specs = [pl.BlockSpec((tm, k), lambda i, j: (i, 0)),
                pl.BlockSpec(w.shape, lambda i, j: (0, 0), pipeline_mode=pl.Buffered(1))]
    args = [a, w]
    if has_bias:
        in_specs.append(pl.BlockSpec((1, tn), lambda i, j: (0, j)))
        args.append(bias)
    if has_resid:
        in_specs.append(pl.BlockSpec((tm, tn), lambda i, j: (i, j)))
        args.append(resid)
    return pl.pallas_call(
        body, name=name, grid=(m // tm, n // tn), in_specs=in_specs,
        out_specs=pl.BlockSpec((tm, tn), lambda i, j: (i, j)),
        out_shape=jax.ShapeDtypeStruct((m, n), F32),
        compiler_params=_params(("parallel", "arbitrary")),
    )(*args)


def _ln(z, g, b):
    mu = jnp.mean(z, axis=-1, keepdims=True)
    zc = z - mu
    var = jnp.mean(zc * zc, axis=-1, keepdims=True)
    return zc * lax.rsqrt(var + LN_EPS) * g + b


def _mm_ln(a, w, resid, alpha, g, b, *, name, resid_ln=None, tm=1024):
    m, k = a.shape
    d = w.shape[1]
    tm = _tile(m, tm, SUBLANES)
    n_extra = 0 if resid_ln is None else 2

    def body(a_ref, w_ref, r_ref, g_ref, b_ref, *rest):
        z_ref, obf_ref = rest[n_extra:]
        x = r_ref[...]
        if resid_ln is not None:
            x = _ln(x, rest[0][...], rest[1][...])
        z = alpha * x + jnp.dot(a_ref[...], w_ref[...], preferred_element_type=F32)
        z_ref[...] = z
        obf_ref[...] = _ln(z, g_ref[...], b_ref[...]).astype(BF16)

    row = pl.BlockSpec((tm, d), lambda i: (i, 0))
    vec = pl.BlockSpec((1, d), lambda i: (0, 0))
    return pl.pallas_call(
        body, name=name, grid=(m // tm,),
        in_specs=[pl.BlockSpec((tm, k), lambda i: (i, 0)),
                  pl.BlockSpec((k, d), lambda i: (0, 0), pipeline_mode=pl.Buffered(1)), row, vec, vec]
        + [vec] * n_extra,
        out_specs=[row, row],
        out_shape=[jax.ShapeDtypeStruct((m, d), F32), jax.ShapeDtypeStruct((m, d), BF16)],
        compiler_params=_params(("parallel",)),
    )(a, w, resid, g, b, *(resid_ln or ()))


def _ln_bwd_math(do, z, g):
    mu = jnp.mean(z, axis=-1, keepdims=True)
    zc = z - mu
    var = jnp.mean(zc * zc, axis=-1, keepdims=True)
    rstd = lax.rsqrt(var + LN_EPS)
    xhat = zc * rstd
    dxh = do * g
    m1 = jnp.mean(dxh, axis=-1, keepdims=True)
    m2 = jnp.mean(dxh * xhat, axis=-1, keepdims=True)
    return rstd * (dxh - m1 - xhat * m2), _colsum(do * xhat), _colsum(do)


def _mm_ln_bwd(parts, w, resid, resid_scale, z, g, *, name, w_rows_are_k=False):
    t, kp = parts[0].shape
    k, d = w.shape if w_rows_are_k else w.shape[::-1]
    n = len(parts)
    tm = _tile(t, min(512, max(256, MM_LHS_ELEMS // k)), SUBLANES)

    def body(*refs):
        a_refs = refs[:n]
        w_ref, r_ref, z_ref, g_ref, dz_ref, dzbf_ref, dg_ref, db_ref = refs[n:]

        @pl.when(pl.program_id(0) == 0)
        def _():
            dg_ref[...] = jnp.zeros_like(dg_ref)
            db_ref[...] = jnp.zeros_like(db_ref)

        dx = resid_scale * r_ref[...]
        for p, a_ref in enumerate(a_refs):
            if w_rows_are_k:
                dx = dx + jnp.dot(a_ref[...], w_ref[p * kp:(p + 1) * kp, :], preferred_element_type=F32)
            else:
                dx = dx + lax.dot_general(a_ref[...], w_ref[:, p * kp:(p + 1) * kp], (((1,), (1,)), ((), ())),
                                          preferred_element_type=F32)
        dz, dg, db = _ln_bwd_math(dx, z_ref[...], g_ref[...])
        dz_ref[...] = dz
        dzbf_ref[...] = dz.astype(BF16)
        dg_ref[...] += dg
        db_ref[...] += db

    row = pl.BlockSpec((tm, d), lambda i: (i, 0))
    vec = pl.BlockSpec((1, d), lambda i: (0, 0))
    return pl.pallas_call(
        body, name=name, grid=(t // tm,),
        in_specs=[pl.BlockSpec((tm, kp), lambda i: (i, 0))] * n
        + [pl.BlockSpec(w.shape, lambda i: (0, 0), pipeline_mode=pl.Buffered(1)), row, row, vec],
        out_specs=[row, row, vec, vec],
        out_shape=[jax.ShapeDtypeStruct((t, d), F32), jax.ShapeDtypeStruct((t, d), BF16),
                   jax.ShapeDtypeStruct((1, d), F32), jax.ShapeDtypeStruct((1, d), F32)],
        compiler_params=_params(("arbitrary",)),
    )(*parts, w, resid, z, g)


def _mm_tn(a, b, *, name, below=None, tm=1408, tn=1536, tk=2048):
    t, m = a.shape
    n = b.shape[1]
    tm = _tile(m, tm, LANES)
    tn = _tile(n, tn, LANES)
    tk = _tile(t, tk, SUBLANES)
    last = t // tk - 1
    rows, earlier = (m, None) if below is None else below
    skip = (rows - m) // tm if earlier is not None else 0

    def body(a_ref, b_ref, *rest):
        o_ref, acc = rest[-2:]

        @pl.when(pl.program_id(2) == 0)
        def _():
            acc[...] = jnp.zeros_like(acc)

        acc[...] += lax.dot_general(a_ref[...], b_ref[...], (((0,), (0,)), ((), ())), preferred_element_type=F32)

        @pl.when(pl.program_id(2) == last)
        def _():
            o_ref[...] = acc[...].astype(BF16)

    in_specs = [pl.BlockSpec((tk, tm), lambda i, j, l: (l, i)), pl.BlockSpec((tk, tn), lambda i, j, l: (l, j))]
    return pl.pallas_call(
        body, name=name, grid=(m // tm, n // tn, t // tk),
        in_specs=in_specs + ([] if earlier is None else [pl.BlockSpec(memory_space=pl.ANY)]),
        out_specs=pl.BlockSpec((tm, tn), lambda i, j, l: (i + skip, j)),
        out_shape=jax.ShapeDtypeStruct((rows, n), BF16),
        input_output_aliases={} if earlier is None else {2: 0},
        scratch_shapes=[pltpu.VMEM((tm, tn), F32)],
        compiler_params=_params(("parallel", "parallel", "arbitrary")),
    )(a, b, *(() if earlier is None else (earlier,)))


def _ln_bwd(dout, z, g, *, name, tm=1024):
    t, d = z.shape
    tm = _tile(t, tm, SUBLANES)

    def body(do_ref, z_ref, g_ref, dz_ref, dzbf_ref, dg_ref, db_ref):
        @pl.when(pl.program_id(0) == 0)
        def _():
            dg_ref[...] = jnp.zeros_like(dg_ref)
            db_ref[...] = jnp.zeros_like(db_ref)

        dz, dg, db = _ln_bwd_math(do_ref[...], z_ref[...], g_ref[...])
        dz_ref[...] = dz
        dzbf_ref[...] = dz.astype(BF16)
        dg_ref[...] += dg
        db_ref[...] += db

    row = pl.BlockSpec((tm, d), lambda i: (i, 0))
    vec = pl.BlockSpec((1, d), lambda i: (0, 0))
    return pl.pallas_call(
        body, name=name, grid=(t // tm,), in_specs=[row, row, vec], out_specs=[row, row, vec, vec],
        out_shape=[jax.ShapeDtypeStruct((t, d), F32), jax.ShapeDtypeStruct((t, d), BF16),
                   jax.ShapeDtypeStruct((1, d), F32), jax.ShapeDtypeStruct((1, d), F32)],
        compiler_params=_params(("arbitrary",)),
    )(dout, z, g)


def _loss_head(z, g, b, target, *, name, tm=1024):
    t, d = z.shape
    tm = _tile(t, tm, SUBLANES)

    def body(z_ref, g_ref, b_ref, t_ref, s_ref, dy_ref):
        @pl.when(pl.program_id(0) == 0)
        def _():
            s_ref[...] = jnp.zeros_like(s_ref)

        e = _ln(z_ref[...], g_ref[...], b_ref[...]) - t_ref[...]
        dy_ref[...] = e * (1.0 / d)
        s_ref[...] += jnp.sum(_colsum(e * e), axis=-1, keepdims=True)

    row = pl.BlockSpec((tm, d), lambda i: (i, 0))
    vec = pl.BlockSpec((1, d), lambda i: (0, 0))
    return pl.pallas_call(
        body, name=name, grid=(t // tm,), in_specs=[row, vec, vec, row],
        out_specs=[pl.BlockSpec((1, LANES), lambda i: (0, 0)), row],
        out_shape=[jax.ShapeDtypeStruct((1, LANES), F32), jax.ShapeDtypeStruct((t, d), F32)],
        compiler_params=_params(("arbitrary",)),
    )(z, g, b, target)


def _own(c, b, *_):
    return c, b


def _ahead(nc, bsz):
    def at(c, b, part):
        b2 = b + jnp.minimum(part, 1)
        return jnp.minimum(c + b2 // bsz, nc - 1), b2 % bsz
    return at


def _strip(s, tc, off, at=_own):
    def index(*ids):
        c, b = at(*ids)
        return b, 0, off + c
    return pl.BlockSpec((None, s, tc), index)


def _cvec(kw, tc, off, at=_own):
    def index(*ids):
        return 0, off + at(*ids)[0]
    return pl.BlockSpec((kw, tc), index)


def _acc(kw, tc):
    return pl.BlockSpec((kw, tc), lambda c, b, *_: (0, c))


def _sc_fwd(h, cw, cb, *, name, tc=256):
    bsz, s, d3 = h.shape
    d = d3 // 3
    tc = _tile(d, tc, LANES)
    nc = d // tc

    def body(gb_ref, gc_ref, v_ref, w_ref, b_ref, q_ref):
        u = _conv_fwd(gc_ref[...] * v_ref[...], w_ref[...], b_ref[...])
        q_ref[...] = (gb_ref[...] * u).astype(BF16)

    return pl.pallas_call(
        body, name=name, grid=(nc, bsz),
        in_specs=[_strip(s, tc, 0), _strip(s, tc, nc), _strip(s, tc, 2 * nc), _cvec(cw.shape[0], tc, 0), _cvec(1, tc, 0)],
        out_specs=_strip(s, tc, 0),
        out_shape=jax.ShapeDtypeStruct((bsz, s, d), BF16),
        compiler_params=_params(("parallel", "parallel")),
    )(h, h, h, cw, cb)


def _sc_bwd(h, dq, cw, cb, *, name, tc=256):
    bsz, s, d3 = h.shape
    d = d3 // 3
    kw = cw.shape[0]
    tc = _tile(d, tc, LANES)
    nc = d // tc

    def body(gb_ref, gc_ref, v_ref, dq_ref, w_ref, b_ref, dh_ref, dw_ref, db_ref, parts):
        b_id, part = pl.program_id(1), pl.program_id(2)

        @pl.when(part == 0)
        def _():
            gb, gc, v, dq_, w = gb_ref[...], gc_ref[...], v_ref[...], dq_ref[...], w_ref[...]
            p = gc * v
            u = _conv_fwd(p, w, b_ref[...])
            du = dq_ * gb
            dp, dw_rows = _conv_bwd(du, p, w)
            parts[0] = (dq_ * u).astype(BF16)
            parts[1] = (dp * v).astype(BF16)
            parts[2] = (dp * gc).astype(BF16)
            _accumulate(b_id == 0, [(dw_ref, dw_rows), (db_ref, _colsum(du))])

        dh_ref[...] = parts[part]

    at = _ahead(nc, bsz)
    return pl.pallas_call(
        body, name=name, grid=(nc, bsz, 3),
        in_specs=[_strip(s, tc, 0, at), _strip(s, tc, nc, at), _strip(s, tc, 2 * nc, at), _strip(s, tc, 0, at),
                  _cvec(kw, tc, 0, at), _cvec(1, tc, 0, at)],
        out_specs=[pl.BlockSpec((None, s, tc), lambda c, b, p: (b, 0, p * nc + c)), _acc(kw, tc), _acc(1, tc)],
        out_shape=[jax.ShapeDtypeStruct((bsz, s, d3), BF16), jax.ShapeDtypeStruct((kw, d), F32),
                   jax.ShapeDtypeStruct((1, d), F32)],
        scratch_shapes=[pltpu.VMEM((3, s, tc), BF16)],
        compiler_params=_params(("parallel", "arbitrary", "arbitrary")),
    )(h, h, h, dq, cw, cb)


def _ffn_specs(s, tc, nc, kw):
    strip = pl.BlockSpec((None, s, tc), lambda b, c: (b, 0, c))
    halves = [pl.BlockSpec((kw, tc), lambda b, c: (0, c)), pl.BlockSpec((kw, tc), lambda b, c: (0, nc + c)),
              pl.BlockSpec((1, tc), lambda b, c: (0, c)), pl.BlockSpec((1, tc), lambda b, c: (0, nc + c))]
    return strip, halves


def _ffn_fwd(x, w_up, cw, cb, *, name, tc=256):
    bsz, s, d = x.shape
    f = w_up.shape[0] // 2
    kw = cw.shape[0]
    tc = _tile(f, tc, LANES)
    nc = f // tc
    nt = (((1,), (1,)), ((), ()))

    def body(x_ref, w_ref, wg_ref, wv_ref, bg_ref, bv_ref, hg_ref, hv_ref, g_ref, v_ref, a_ref):
        c0 = pl.multiple_of(pl.program_id(1) * tc, LANES)
        xs = x_ref[...]
        hg = lax.dot_general(xs, w_ref[pl.ds(c0, tc), :], nt, preferred_element_type=F32)
        hv = lax.dot_general(xs, w_ref[pl.ds(f + c0, tc), :], nt, preferred_element_type=F32)
        hg_ref[...] = hg
        hv_ref[...] = hv
        g = _conv_fwd(hg, wg_ref[...], bg_ref[...])
        v = _conv_fwd(hv, wv_ref[...], bv_ref[...])
        g_ref[...] = g
        v_ref[...] = v
        a_ref[...] = (g * _sigmoid(g) * v).astype(BF16)

    strip, halves = _ffn_specs(s, tc, nc, kw)
    return pl.pallas_call(
        body, name=name, grid=(bsz, nc),
        in_specs=[pl.BlockSpec((None, s, d), lambda b, c: (b, 0, 0)),
                  pl.BlockSpec(w_up.shape, lambda b, c: (0, 0), pipeline_mode=pl.Buffered(1))] + halves,
        out_specs=[strip] * 5,
        out_shape=[jax.ShapeDtypeStruct((bsz, s, f), F32)] * 4 + [jax.ShapeDtypeStruct((bsz, s, f), BF16)],
        compiler_params=_params(("parallel", "arbitrary")),
    )(x, w_up, cw, cw, cb, cb)


def _ffn_bwd(hg, hv, g, v, dz, w_down, cw, *, name, tc=256):
    bsz, s, f = hg.shape
    d = dz.shape[2]
    kw = cw.shape[0]
    tc = _tile(f, tc, LANES)
    nc = f // tc

    def body(hg_ref, hv_ref, g_ref, v_ref, dz_ref, wd_ref, wg_ref, wv_ref,
             dhg_ref, dhv_ref, dwg_ref, dwv_ref, dbg_ref, dbv_ref):
        c0 = pl.multiple_of(pl.program_id(1) * tc, LANES)
        cols = pl.ds(c0, tc)
        da = lax.dot_general(dz_ref[...], wd_ref[cols, :], (((1,), (1,)), ((), ())), preferred_element_type=F32)
        g_ = g_ref[...]
        sg = _sigmoid(g_)
        dv = da * (g_ * sg)
        dg = da * v_ref[...] * (sg * (1.0 + g_ * (1.0 - sg)))
        dhg, dwg_rows = _conv_bwd(dg, hg_ref[...], wg_ref[...])
        dhv, dwv_rows = _conv_bwd(dv, hv_ref[...], wv_ref[...])
        dhg_ref[...] = dhg.astype(BF16)
        dhv_ref[...] = dhv.astype(BF16)
        _accumulate(pl.program_id(0) == 0, [(dwg_ref, dwg_rows), (dwv_ref, dwv_rows),
                                            (dbg_ref, [_colsum(dg)]), (dbv_ref, [_colsum(dv)])], cols)

    strip, halves = _ffn_specs(s, tc, nc, kw)
    whole = lambda r: pl.BlockSpec((r, f), lambda b, c: (0, 0))
    return pl.pallas_call(
        body, name=name, grid=(bsz, nc),
        in_specs=[strip] * 4 + [pl.BlockSpec((None, s, d), lambda b, c: (b, 0, 0)),
                                pl.BlockSpec(w_down.shape, lambda b, c: (0, 0), pipeline_mode=pl.Buffered(1))]
        + halves[:2],
        out_specs=[strip, strip, whole(kw), whole(kw), whole(1), whole(1)],
        out_shape=[jax.ShapeDtypeStruct((bsz, s, f), BF16), jax.ShapeDtypeStruct((bsz, s, f), BF16),
                   jax.ShapeDtypeStruct((kw, f), F32), jax.ShapeDtypeStruct((kw, f), F32),
                   jax.ShapeDtypeStruct((1, f), F32), jax.ShapeDtypeStruct((1, f), F32)],
        compiler_params=_params(("arbitrary", "arbitrary")),
    )(hg, hv, g, v, dz, w_down, cw, cw)


def _lru_gates(r, cw, cb, wg, bg, lam):
    blk = r.shape[1]
    xr = _conv_fwd(r, cw, cb)
    gates = jnp.dot(xr.astype(BF16), wg, preferred_element_type=F32) + bg
    rg = _sigmoid(gates[:, :blk])
    ig = _sigmoid(gates[:, blk:])
    sp = _softplus(-lam)
    la = (-LRU_C * sp) * rg
    a = jnp.exp(la)
    mult = jnp.sqrt(-_expm1(2.0 * la, a * a))
    return xr, rg, ig, sp, a, mult


def _lru_fwd(h, cw, cb, wg, bg, lam, *, name):
    bsz, s, r2 = h.shape
    heads, blk = wg.shape[0], wg.shape[1]
    kw = cw.shape[0]

    def body(g_ref, r_ref, cw_ref, cb_ref, wg_ref, bg_ref, lam_ref, y_ref, sv_ref):
        xr, rg, ig, _, a, mult = _lru_gates(r_ref[...], cw_ref[...], cb_ref[...], wg_ref[...], bg_ref[...], lam_ref[...])
        hs = _scan_fwd(a, mult * (ig * xr))
        for n, val in enumerate((hs, xr, rg, ig, a, mult)):
            sv_ref[n] = val
        y_ref[...] = (hs * _gelu(g_ref[...])).astype(BF16)

    per_head = lambda hd, b: (hd, 0, 0)
    return pl.pallas_call(
        body, name=name, grid=(heads, bsz),
        in_specs=[_strip(s, blk, 0), _strip(s, blk, heads), _cvec(kw, blk, 0), _cvec(1, blk, 0),
                  pl.BlockSpec((None, blk, 2 * blk), per_head), pl.BlockSpec((None, 1, 2 * blk), per_head),
                  _cvec(1, blk, 0)],
        out_specs=[_strip(s, blk, 0), pl.BlockSpec((6, None, s, blk), lambda hd, b: (0, b, 0, hd))],
        out_shape=[jax.ShapeDtypeStruct((bsz, s, r2 // 2), BF16), jax.ShapeDtypeStruct((6, bsz, s, r2 // 2), F32)],
        compiler_params=_params(("parallel", "parallel")),
    )(h, h, cw, cb, wg, bg, lam)


def _lru_bwd(h, sv, dy, cw, wg, lam, *, name):
    bsz, s, r2 = h.shape
    rw = r2 // 2
    heads, blk = wg.shape[0], wg.shape[1]
    kw = cw.shape[0]

    def body(g_ref, r_ref, cw_ref, wg_ref, lam_ref, sv_ref, dy_ref,
             dh_ref, dcw_ref, dcb_ref, dwg_ref, dbg_ref, dlam_ref, sg_ref, sr_ref, parts):
        b_id, part = pl.program_id(1), pl.program_id(2)

        @pl.when(part == 0)
        def _():
            r, cw_, wg_, lam_ = r_ref[...], cw_ref[...], wg_ref[...], lam_ref[...]
            hs_, xr, rg, ig, a, mult = (sv_ref[n] for n in range(6))
            sp = _softplus(-lam_)
            dy_ = dy_ref[...]
            gel, dgel = _gelu_and_grad(g_ref[...])
            dg = dy_ * hs_ * dgel
            lmb = _scan_rev(_shift_up(a, 1, 1.0), dy_ * gel)
            da = lmb * _shift_dn(hs_, 1)
            dmult = lmb * (ig * xr)
            dig = lmb * (mult * xr)
            dxr = lmb * (mult * ig)
            dla = da * a - dmult * (a * a / mult)
            drg = dla * (-LRU_C * sp)
            dsp = _colsum(dla * rg) * (-LRU_C)
            dlam = -dsp * _sigmoid(-lam_)
            dgates = jnp.concatenate([drg * (rg * (1.0 - rg)), dig * (ig * (1.0 - ig))], axis=1)
            dgates_bf = dgates.astype(BF16)
            dwg = lax.dot_general(xr.astype(BF16), dgates_bf, (((0,), (0,)), ((), ())), preferred_element_type=F32)
            dxr = dxr + lax.dot_general(dgates_bf, wg_, (((1,), (1,)), ((), ())), preferred_element_type=F32)
            dr, dcw_rows = _conv_bwd(dxr, r, cw_)
            parts[0] = dg.astype(BF16)
            parts[1] = dr.astype(BF16)
            _accumulate(b_id == 0, [(dcw_ref, dcw_rows), (dcb_ref, _colsum(dxr)), (dwg_ref, dwg),
                                    (dbg_ref, _colsum(dgates)), (dlam_ref, dlam), (sg_ref, _colsum(dg)),
                                    (sr_ref, _colsum(dr))])

        dh_ref[...] = parts[part]

    at = _ahead(heads, bsz)

    def saved(*ids):
        hd, b = at(*ids)
        return 0, b, 0, hd

    vec = pl.BlockSpec((1, blk), lambda hd, b, p: (0, hd))
    return pl.pallas_call(
        body, name=name, grid=(heads, bsz, 2),
        in_specs=[_strip(s, blk, 0, at), _strip(s, blk, heads, at), _cvec(kw, blk, 0, at),
                  pl.BlockSpec((None, blk, 2 * blk), lambda *ids: (at(*ids)[0], 0, 0)), _cvec(1, blk, 0, at),
                  pl.BlockSpec((6, None, s, blk), saved), _strip(s, blk, 0, at)],
        out_specs=[pl.BlockSpec((None, s, blk), lambda hd, b, p: (b, 0, p * heads + hd)),
                   pl.BlockSpec((kw, blk), lambda hd, b, p: (0, hd)), vec,
                   pl.BlockSpec((None, blk, 2 * blk), lambda hd, b, p: (hd, 0, 0)),
                   pl.BlockSpec((None, 1, 2 * blk), lambda hd, b, p: (hd, 0, 0)), vec, vec, vec],
        out_shape=[jax.ShapeDtypeStruct((bsz, s, r2), BF16), jax.ShapeDtypeStruct((kw, rw), F32),
                   jax.ShapeDtypeStruct((1, rw), F32), jax.ShapeDtypeStruct((heads, blk, 2 * blk), F32),
                   jax.ShapeDtypeStruct((heads, 1, 2 * blk), F32), jax.ShapeDtypeStruct((1, rw), F32),
                   jax.ShapeDtypeStruct((1, rw), F32), jax.ShapeDtypeStruct((1, rw), F32)],
        scratch_shapes=[pltpu.VMEM((2, s, blk), BF16)],
        compiler_params=_params(("parallel", "arbitrary", "arbitrary")),
    )(h, h, cw, wg, lam, sv, dy)


HBM_SPEC = pl.BlockSpec(memory_space=pltpu.HBM)
SEM_SPEC = pl.BlockSpec(memory_space=pltpu.SEMAPHORE)
EFFECT = pltpu.SideEffectType.DATAFLOW_SIDE_EFFECTING


def _peer_copies(srcs, lands, gather, send_sem, recv_sem):
    x, y, c = (lax.axis_index(ax) for ax in MESH_AXES)
    me = 4 * x + 2 * y + c
    copies = []
    for i in range(len(srcs)):
        for d in range(1, N_DEV):
            px = 1 - x if d & 4 else x
            py = 1 - y if d & 2 else y
            pc = 1 - c if d & 1 else c
            src = srcs[i] if gather[i] else srcs[i].at[4 * px + 2 * py + pc]
            k = i * (N_DEV - 1) + d - 1
            copies.append(pltpu.make_async_remote_copy(
                src_ref=src, dst_ref=lands[i].at[me], send_sem=send_sem.at[k], recv_sem=recv_sem.at[k],
                device_id=(px, py, pc), device_id_type=pl.DeviceIdType.MESH))
    return copies


def _exchange_start(arrs, gather, *, name):
    n = len(arrs)
    lands = [lax.empty((N_DEV,) + tuple(a.shape if g else a.shape[1:]), a.dtype) for a, g in zip(arrs, gather)]

    def body(*refs):
        srcs, land_refs = refs[:n], refs[n:2 * n]
        send_sem, recv_sem = refs[2 * n], refs[2 * n + 1]
        token = refs[-1]
        for cp in _peer_copies(srcs, land_refs, gather, send_sem, recv_sem):
            cp.start()
        token[...] = jnp.zeros_like(token)

    sems = pltpu.SemaphoreType.DMA((n * (N_DEV - 1),))
    thru = [pltpu.HBM(a.shape, a.dtype) for a in arrs + lands]
    out = pl.pallas_call(
        body, name=name, in_specs=[HBM_SPEC] * (2 * n),
        out_shape=(sems, sems, *thru, jax.ShapeDtypeStruct((SUBLANES, LANES), F32)),
        out_specs=(SEM_SPEC, SEM_SPEC, *([HBM_SPEC] * (2 * n)), pl.BlockSpec(memory_space=pltpu.VMEM)),
        input_output_aliases={i: 2 + i for i in range(2 * n)},
        compiler_params=pltpu.CompilerParams(has_side_effects=EFFECT),
    )(*[pltpu.with_memory_space_constraint(a, pltpu.HBM) for a in arrs + lands])
    return {"send_sem": out[0], "recv_sem": out[1], "srcs": list(out[2:2 + n]), "lands": list(out[2 + n:2 + 2 * n]),
            "token": out[-1], "gather": list(gather)}


def _exchange_wait(handle, after, *, name):
    srcs, lands, gather = handle["srcs"], handle["lands"], handle["gather"]
    n = len(srcs)

    def body(*refs):
        src_refs, land_refs = refs[:n], refs[n:2 * n]
        send_sem, recv_sem = refs[2 * n], refs[2 * n + 1]
        for cp in _peer_copies(src_refs, land_refs, gather, send_sem, recv_sem):
            cp.wait_send()
            cp.wait_recv()

    out = pl.pallas_call(
        body, name=name,
        in_specs=[HBM_SPEC] * (2 * n) + [SEM_SPEC, SEM_SPEC, pl.BlockSpec(memory_space=pl.ANY)],
        out_shape=tuple(pltpu.HBM(a.shape, a.dtype) for a in srcs + lands), out_specs=tuple([HBM_SPEC] * (2 * n)),
        input_output_aliases={i: i for i in range(2 * n)},
        compiler_params=pltpu.CompilerParams(has_side_effects=EFFECT),
    )(*srcs, *lands, handle["send_sem"], handle["recv_sem"], after)
    return list(out[:n]), list(out[n:])


def _layers_bf16(stacks, *, name):
    counts = [a.shape[0] for a in stacks]

    def body(*refs):
        outs = iter(refs[len(stacks):])
        for i_ref, n_layers in zip(refs, counts):
            for layer in range(n_layers):
                next(outs)[...] = i_ref[layer].astype(BF16)

    flat = pl.pallas_call(
        body, name=name,
        out_shape=[jax.ShapeDtypeStruct(a.shape[1:], BF16) for a in stacks for _ in range(a.shape[0])],
        compiler_params=pltpu.CompilerParams(vmem_limit_bytes=VMEM_LIMIT),
    )(*stacks)
    split, pos = [], 0
    for n_layers in counts:
        split.append(list(flat[pos:pos + n_layers]))
        pos += n_layers
    return split


def _adamw(parts, w, m, v, layer, so_far, *, name, tr=512):
    n_layers, r, c = w.shape
    tr = _tile(r, tr, SUBLANES)
    bc1 = 1.0 / (1.0 - ADAM_B1 ** ADAM_STEP)
    bc2 = 1.0 / (1.0 - ADAM_B2 ** ADAM_STEP)
    if so_far is None:
        so_far = [lax.empty(w.shape, F32) for _ in range(4)]

    def body(p_ref, w_ref, m_ref, v_ref, *rest):
        g_ref, d_ref, mo_ref, vo_ref = rest[4:]
        g = p_ref[0].astype(F32)
        for s in range(1, N_DEV):
            g = g + p_ref[s].astype(F32)
        m_new = ADAM_B1 * m_ref[...] + (1.0 - ADAM_B1) * g
        v_new = ADAM_B2 * v_ref[...] + (1.0 - ADAM_B2) * (g * g)
        g_ref[...] = g
        mo_ref[...] = m_new
        vo_ref[...] = v_new
        d_ref[...] = -ADAM_LR * ((m_new * bc1) / (jnp.sqrt(v_new * bc2) + ADAM_EPS) + ADAM_WD * w_ref[...])

    blk = pl.BlockSpec((None, tr, c), lambda i: (layer, i, 0))
    return pl.pallas_call(
        body, name=name, grid=(r // tr,),
        in_specs=[pl.BlockSpec((N_DEV, tr, c), lambda i: (0, i, 0)), blk, blk, blk]
        + [pl.BlockSpec(memory_space=pl.ANY)] * 4,
        out_specs=[blk] * 4, out_shape=[jax.ShapeDtypeStruct(w.shape, F32)] * 4,
        input_output_aliases={4 + o: o for o in range(4)},
        compiler_params=_params(("parallel",)),
    )(parts, w, m, v, *so_far)


def _whole(slabs, axis):
    x = jnp.moveaxis(slabs, 0, axis)
    shp = x.shape
    return x.reshape(shp[:axis] + (shp[axis] * shp[axis + 1],) + shp[axis + 2:])


def _slabs(whole, axis):
    shp = whole.shape
    x = whole.reshape(shp[:axis] + (N_DEV, shp[axis] // N_DEV) + shp[axis + 1:])
    return jnp.moveaxis(x, axis, 0)


BIG = {"sc_w_in": 2, "sc_w_out": 1, "lru_w_in": 2, "lru_w_gate": 3, "lru_w_out": 1, "ffn_w_up": 2, "ffn_w_down": 1}
TRANSPOSED = ("ffn_w_up", "lru_w_in")
SMALL = ["sc_conv_w", "lru_b_in", "lru_conv_w", "lru_conv_b", "lru_b_gate", "lru_lambda", "ffn_conv_w", "ln_g", "ln_b"]
REPL = ["sc_conv_b", "ffn_conv_b"]
WEIGHTS = ["sc_w_in", "sc_conv_w", "sc_conv_b", "sc_w_out", "lru_w_in", "lru_b_in", "lru_conv_w", "lru_conv_b",
           "lru_w_gate", "lru_b_gate", "lru_lambda", "lru_w_out", "ffn_w_up", "ffn_conv_w", "ffn_conv_b", "ffn_w_down",
           "ln_g", "ln_b"]


STAGES_PER_LAYER = 3


def _stage_big(g):
    i, part = divmod(g, STAGES_PER_LAYER)
    j = i // 2
    if part:
        return [("ffn_w_up" if part == 1 else "ffn_w_down", i)]
    return [("sc_w_in", j), ("sc_w_out", j)] if i % 2 == 0 else [("lru_w_in", j), ("lru_w_gate", j), ("lru_w_out", j)]


def _step(x, loss_target, w, m, v):
    bsz, s, d = x.shape
    t = bsz * s
    depth = w["ffn_w_up"].shape[0]
    alpha = (2.0 * depth) ** 0.25
    heads = w["lru_w_gate"].shape[1]

    me = 4 * lax.axis_index("x") + 2 * lax.axis_index("y") + lax.axis_index("c")

    def with_own(land, own):
        return lax.dynamic_update_slice_in_dim(land, own, me, axis=0)

    stages = STAGES_PER_LAYER * depth
    def held(k, arr):
        return jnp.swapaxes(arr, -1, -2) if k in TRANSPOSED else arr

    def split_axis(k):
        return 0 if k in TRANSPOSED else BIG[k] - 1

    flat_names = [k for k in BIG if w[k].ndim == 3]
    wb = dict(zip(flat_names, _layers_bf16([held(k, w[k]) for k in flat_names], name="weights_bf16")))
    wb.update({k: list(w[k].astype(BF16)) for k in BIG if k not in flat_names})

    gathers, tok = [], None
    for g in range(stages):
        arrs = [wb[k][l] for k, l in _stage_big(g)]
        if g == 0:
            arrs += [w[k] for k in SMALL]
        if tok is not None:
            arrs[0] = arrs[0] + tok.astype(BF16)
        gathers.append(_exchange_start(arrs, [True] * len(arrs), name=f"gather_start_{g}"))
        tok = gathers[-1]["token"][0, 0]
    full = {k: [None] * w[k].shape[0] for k in BIG}
    full["sc_conv_b"] = w["sc_conv_b"]
    full["ffn_conv_b"] = w["ffn_conv_b"]

    def arrive(g, after):
        srcs, lands = _exchange_wait(gathers[g], after, name=f"gather_wait_{g}")
        for (k, l), src, land in zip(_stage_big(g), srcs, lands):
            full[k][l] = _whole(with_own(land, src[None]), split_axis(k))
        if g == 0:
            n_big = len(_stage_big(0))
            for k, src, land in zip(SMALL, srcs[n_big:], lands[n_big:]):
                full[k] = _whole(with_own(land, src[None]), w[k].ndim - 1)

    stream, stream_ln = x.reshape(t, d), None
    xb = stream.astype(BF16)
    saved = []
    for i in range(depth):
        j = i // 2
        arrive(3 * i, gathers[-1]["token"] if i == 0 else xb)
        lng, lnb = full["ln_g"][i], full["ln_b"][i]
        sv = {"x0": xb}
        if i % 2 == 0:
            hm = _mm(xb, full["sc_w_in"][j], name="sc_in")
            q = _sc_fwd(hm.reshape(bsz, s, -1), full["sc_conv_w"][j], full["sc_conv_b"][j:j + 1], name="sc_mix")
            w_out = full["sc_w_out"][j]
        else:
            hm = _mm(xb, full["lru_w_in"][j], trans_w=True, bias=full["lru_b_in"][j:j + 1], name="lru_in")
            q, hs = _lru_fwd(hm.reshape(bsz, s, -1), full["lru_conv_w"][j], full["lru_conv_b"][j:j + 1],
                             full["lru_w_gate"][j], full["lru_b_gate"][j].reshape(heads, 1, -1),
                             full["lru_lambda"][j:j + 1], name="lru_mix")
            sv["hs"] = hs
            w_out = full["lru_w_out"][j]
        q = q.reshape(t, -1)
        arrive(3 * i + 1, q)
        z1, x1b = _mm_ln(q, w_out, stream, alpha, lng[0:1], lnb[0:1], resid_ln=stream_ln, name="mix_out_ln")
        hg, hv, gc, vc, a = _ffn_fwd(x1b.reshape(bsz, s, d), full["ffn_w_up"][i], full["ffn_conv_w"][i],
                                     full["ffn_conv_b"][i:i + 1], name="ffn_up_act")
        a = a.reshape(t, -1)
        arrive(3 * i + 2, a)
        z2, xb = _mm_ln(a, full["ffn_w_down"][i], z1, alpha, lng[1:2], lnb[1:2], resid_ln=(lng[0:1], lnb[0:1]),
                        name="ffn_down_ln")
        stream, stream_ln = z2, (lng[1:2], lnb[1:2])
        sv.update(hm=hm, q=q, z1=z1, x1=x1b, ffn=(hg, hv, gc, vc), a=a, z2=z2)
        saved.append(sv)

    sq, dx = _loss_head(stream, *stream_ln, loss_target.reshape(t, d), name="loss_head")
    loss = lax.psum((0.5 / d) * sq[0, 0], MESH_AXES)

    grads = {k: [None] * w[k].shape[0] for k in WEIGHTS}
    scatters = [None] * stages

    def depart(g):
        send = [_slabs(grads[k][l], split_axis(k)).astype(BF16) for k, l in _stage_big(g)]
        scatters[g] = _exchange_start(send, [False] * len(send), name=f"scatter_start_{g}")
        return scatters[g]["token"][0:1, 0:1]

    dz2, dz2b, dg2, db2 = _ln_bwd(dx, saved[-1]["z2"], full["ln_g"][-1][1:2], name="ln_bwd")
    for i in reversed(range(depth)):
        j = i // 2
        sv = saved[i]
        lng = full["ln_g"][i]
        grads["ffn_w_down"][i] = _mm_tn(sv["a"], dz2b, name="ffn_down_dw")
        dhg, dhv, dwg, dwv, dbg, dbv = _ffn_bwd(*sv["ffn"], dz2b.reshape(bsz, s, d), full["ffn_w_down"][i],
                                                full["ffn_conv_w"][i] + depart(3 * i + 2), name="ffn_act_bwd")
        dhg, dhv = dhg.reshape(t, -1), dhv.reshape(t, -1)
        grads["ffn_conv_w"][i] = jnp.concatenate([dwg, dwv], axis=1)
        grads["ffn_conv_b"][i] = jnp.concatenate([dbg, dbv], axis=1)[0]
        rows_up = 2 * dhg.shape[1]
        dw_g = _mm_tn(dhg, sv["x1"], below=(rows_up, None), name="ffn_up_dw_g")
        grads["ffn_w_up"][i] = _mm_tn(dhv, sv["x1"], below=(rows_up, dw_g), name="ffn_up_dw_v")
        dz1, dz1b, dg1, db1 = _mm_ln_bwd([dhg, dhv], full["ffn_w_up"][i], dz2, alpha, sv["z1"],
                                         lng[0:1] + depart(3 * i + 1), name="ffn_up_dx_ln", w_rows_are_k=True)
        grads["ln_g"][i] = jnp.concatenate([dg1, dg2], axis=0)
        grads["ln_b"][i] = jnp.concatenate([db1, db2], axis=0)
        if i % 2 == 0:
            dq = _mm(dz1b, full["sc_w_out"][j], trans_w=True, name="sc_out_dx")
            grads["sc_w_out"][j] = _mm_tn(sv["q"], dz1b, name="sc_out_dw")
            dhm, dcw, dcb = _sc_bwd(sv["hm"].reshape(bsz, s, -1), dq.reshape(bsz, s, -1), full["sc_conv_w"][j],
                                    full["sc_conv_b"][j:j + 1], name="sc_mix_bwd")
            dhm = dhm.reshape(t, -1)
            grads["sc_conv_w"][j] = dcw
            grads["sc_conv_b"][j] = dcb[0]
            grads["sc_w_in"][j] = _mm_tn(sv["x0"], dhm, name="sc_in_dw")
            w_in = full["sc_w_in"][j]
        else:
            dq = _mm(dz1b, full["lru_w_out"][j], trans_w=True, name="lru_out_dx")
            grads["lru_w_out"][j] = _mm_tn(sv["q"], dz1b, name="lru_out_dw")
            dhm, dcw, dcb, dwgt, dbgt, dlam, sgb, srb = _lru_bwd(
                sv["hm"].reshape(bsz, s, -1), sv["hs"], dq.reshape(bsz, s, -1), full["lru_conv_w"][j],
                full["lru_w_gate"][j], full["lru_lambda"][j:j + 1], name="lru_mix_bwd")
            dhm = dhm.reshape(t, -1)
            grads["lru_conv_w"][j] = dcw
            grads["lru_conv_b"][j] = dcb[0]
            grads["lru_w_gate"][j] = dwgt
            grads["lru_b_gate"][j] = dbgt[:, 0, :]
            grads["lru_lambda"][j] = dlam[0]
            grads["lru_b_in"][j] = jnp.concatenate([sgb, srb], axis=1)[0]
            grads["lru_w_in"][j] = _mm_tn(dhm, sv["x0"], name="lru_in_dw")
            w_in = full["lru_w_in"][j]
        tok = depart(3 * i)
        if i > 0:
            dz2, dz2b, dg2, db2 = _mm_ln_bwd([dhm], w_in, dz1, alpha, saved[i - 1]["z2"], full["ln_g"][i - 1][1:2] + tok,
                                             name="mix_in_dx_ln", w_rows_are_k=i % 2 == 1)
        else:
            dx = _mm(dhm, w_in + tok[0, 0].astype(BF16), trans_w=True, resid=dz1, resid_scale=alpha, name="mix_in_dx")
    grad_x = dx.reshape(bsz, s, d)

    gsm = {k: jnp.stack(grads[k]) for k in SMALL + REPL}
    small_scatter = _exchange_start([_slabs(gsm[k], gsm[k].ndim - 1) for k in SMALL] + [gsm[k] for k in REPL],
                                    [False] * len(SMALL) + [True] * len(REPL), name="scatter_start_small")

    out = {}

    def own_slab(src):
        return lax.dynamic_slice_in_dim(src, me, 1, axis=0)

    stacks = {k: None for k in BIG}
    after = dx
    for g in reversed(range(stages)):
        srcs, lands = _exchange_wait(scatters[g], after, name=f"scatter_wait_{g}")
        for (k, l), src, land in zip(_stage_big(g), srcs, lands):
            n_l, c2 = w[k].shape[0], land.shape[-1]
            wk, mk, vk = (held(k, arr[k]).reshape(n_l, -1, c2) for arr in (w, m, v))
            stacks[k] = _adamw(with_own(land, own_slab(src)).reshape(N_DEV, -1, c2), wk, mk, vk, l, stacks[k],
                               name=f"adamw_{k}_{l}")
            after = stacks[k][-1]
    for k in BIG:
        shp = held(k, w[k]).shape
        out[k] = [held(k, r.reshape(shp)) for r in stacks[k]]
    srcs, lands = _exchange_wait(small_scatter, after, name="scatter_wait_small")
    for n, k in enumerate(SMALL + REPL):
        own = srcs[n][None] if k in REPL else own_slab(srcs[n])
        c2 = w[k].shape[-1]
        res = _adamw(with_own(lands[n], own).reshape(N_DEV, -1, c2), w[k].reshape(1, -1, c2), m[k].reshape(1, -1, c2),
                     v[k].reshape(1, -1, c2), 0, None, name="adamw_" + k)
        out[k] = [r.reshape(w[k].shape) for r in res]

    return (loss, grad_x, *[out[k][0] for k in WEIGHTS], *[out[k][1] for k in WEIGHTS],
            *[out[k][2] for k in WEIGHTS], *[out[k][3] for k in WEIGHTS])


def kernel(x, sc_w_in, sc_conv_w, sc_conv_b, sc_w_out, lru_w_in, lru_b_in, lru_conv_w, lru_conv_b, lru_w_gate, lru_b_gate, lru_lambda, lru_w_out, ffn_w_up, ffn_conv_w, ffn_conv_b, ffn_w_down, ln_g, ln_b, loss_target, m_sc_w_in, m_sc_conv_w, m_sc_conv_b, m_sc_w_out, m_lru_w_in, m_lru_b_in, m_lru_conv_w, m_lru_conv_b, m_lru_w_gate, m_lru_b_gate, m_lru_lambda, m_lru_w_out, m_ffn_w_up, m_ffn_conv_w, m_ffn_conv_b, m_ffn_w_down, m_ln_g, m_ln_b, v_sc_w_in, v_sc_conv_w, v_sc_conv_b, v_sc_w_out, v_lru_w_in, v_lru_b_in, v_lru_conv_w, v_lru_conv_b, v_lru_w_gate, v_lru_b_gate, v_lru_lambda, v_lru_w_out, v_ffn_w_up, v_ffn_conv_w, v_ffn_conv_b, v_ffn_w_down, v_ln_g, v_ln_b):
    w = dict(sc_w_in=sc_w_in, sc_conv_w=sc_conv_w, sc_conv_b=sc_conv_b, sc_w_out=sc_w_out, lru_w_in=lru_w_in,
             lru_b_in=lru_b_in, lru_conv_w=lru_conv_w, lru_conv_b=lru_conv_b, lru_w_gate=lru_w_gate,
             lru_b_gate=lru_b_gate, lru_lambda=lru_lambda, lru_w_out=lru_w_out, ffn_w_up=ffn_w_up,
             ffn_conv_w=ffn_conv_w, ffn_conv_b=ffn_conv_b, ffn_w_down=ffn_w_down, ln_g=ln_g, ln_b=ln_b)
    m = dict(sc_w_in=m_sc_w_in, sc_conv_w=m_sc_conv_w, sc_conv_b=m_sc_conv_b, sc_w_out=m_sc_w_out, lru_w_in=m_lru_w_in,
             lru_b_in=m_lru_b_in, lru_conv_w=m_lru_conv_w, lru_conv_b=m_lru_conv_b, lru_w_gate=m_lru_w_gate,
             lru_b_gate=m_lru_b_gate, lru_lambda=m_lru_lambda, lru_w_out=m_lru_w_out, ffn_w_up=m_ffn_w_up,
             ffn_conv_w=m_ffn_conv_w, ffn_conv_b=m_ffn_conv_b, ffn_w_down=m_ffn_w_down, ln_g=m_ln_g, ln_b=m_ln_b)
    v = dict(sc_w_in=v_sc_w_in, sc_conv_w=v_sc_conv_w, sc_conv_b=v_sc_conv_b, sc_w_out=v_sc_w_out, lru_w_in=v_lru_w_in,
             lru_b_in=v_lru_b_in, lru_conv_w=v_lru_conv_w, lru_conv_b=v_lru_conv_b, lru_w_gate=v_lru_w_gate,
             lru_b_gate=v_lru_b_gate, lru_lambda=v_lru_lambda, lru_w_out=v_lru_w_out, ffn_w_up=v_ffn_w_up,
             ffn_conv_w=v_ffn_conv_w, ffn_conv_b=v_ffn_conv_b, ffn_w_down=v_ffn_w_down, ln_g=v_ln_g, ln_b=v_ln_b)
    return _step(x, loss_target, w, m, v)
```

```python
import math

import jax
import jax.numpy as jnp
from jax import lax
from jax.experimental import pallas as pl
from jax.experimental.pallas import tpu as pltpu

F32 = jnp.float32
BF16 = jnp.bfloat16

N_DEV = 8
MESH_AXES = ("x", "y", "c")
LANES = 128
SUBLANES = 8
VMEM_LIMIT = 56 * 1024 * 1024
MM_LHS_ELEMS = 3 * 1024 * 1024
MM_TN = 1536

LRU_C = 8.0
LN_EPS = 1e-5
ADAM_LR = 0.001
ADAM_B1 = 0.9
ADAM_B2 = 0.999
ADAM_EPS = 1e-08
ADAM_WD = 0.01
ADAM_STEP = 10
GELU_K = math.sqrt(2.0 / math.pi)
GELU_C = 0.044715


def _tile(n, target, align):
    if n <= target:
        return n
    t = (target // align) * align
    while t >= align:
        if n % t == 0:
            return t
        t -= align
    return n


def _params(sem):
    return pltpu.CompilerParams(dimension_semantics=sem, vmem_limit_bytes=VMEM_LIMIT)


def _rows(x):
    return lax.broadcasted_iota(jnp.int32, x.shape, 0)


def _shift_dn(x, k, fill=0.0):
    if k == 0:
        return x
    return jnp.where(_rows(x) >= k, pltpu.roll(x, k, 0), fill)


def _shift_up(x, k, fill=0.0):
    if k == 0:
        return x
    s = x.shape[0]
    return jnp.where(_rows(x) < s - k, pltpu.roll(x, s - k, 0), fill)


def _conv_fwd(x, w, b):
    kw = w.shape[0]
    y = _shift_dn(x, kw - 1) * w[0:1, :] + b
    for k in range(1, kw):
        y = y + _shift_dn(x, kw - 1 - k) * w[k:k + 1, :]
    return y


def _conv_bwd(dy, x, w):
    kw = w.shape[0]
    ahead = [_shift_up(dy, j) for j in range(kw)]
    dx = ahead[kw - 1] * w[0:1, :]
    for k in range(1, kw):
        dx = dx + ahead[kw - 1 - k] * w[k:k + 1, :]
    return dx, [_colsum(ahead[kw - 1 - k] * x) for k in range(kw)]


def _accumulate(first, items, cols=slice(None)):
    flat = []
    for ref, val in items:
        if isinstance(val, list):
            flat += [(ref, (slice(k, k + 1), cols), row) for k, row in enumerate(val)]
        else:
            flat.append((ref, Ellipsis, val))

    @pl.when(first)
    def _():
        for ref, idx, val in flat:
            ref[idx] = val

    @pl.when(jnp.logical_not(first))
    def _():
        for ref, idx, val in flat:
            ref[idx] += val


def _colsum(x):
    return jnp.sum(x, axis=0, keepdims=True)


def _sigmoid(x):
    return 1.0 / (1.0 + jnp.exp(-x))


def _log1p(x):
    u = 1.0 + x
    return jnp.where(u == 1.0, x, jnp.log(u) * (x / (u - 1.0)))


def _softplus(x):
    return jnp.maximum(x, 0.0) + _log1p(jnp.exp(-jnp.abs(x)))


def _expm1(x, ex):
    poly = x * (1.0 + x * (0.5 + x * (1.0 / 6.0 + x * (1.0 / 24.0 + x * (1.0 / 120.0 + x * (1.0 / 720.0))))))
    return jnp.where(jnp.abs(x) < 0.25, poly, ex - 1.0)


def _gelu(x):
    t = jnp.tanh(GELU_K * (x + GELU_C * x * x * x))
    return 0.5 * x * (1.0 + t)


def _gelu_and_grad(x):
    x2 = x * x
    t = jnp.tanh(GELU_K * (x + GELU_C * x * x2))
    g = 0.5 * x * (1.0 + t)
    dg = 0.5 * (1.0 + t) + 0.5 * x * (1.0 - t * t) * (GELU_K * (1.0 + 3.0 * GELU_C * x2))
    return g, dg


def _scan_fwd(a, b):
    s = a.shape[0]
    k = 1
    while k < s:
        last = 2 * k >= s
        if k % SUBLANES:
            b = a * _shift_dn(b, k) + b
            if not last:
                a = a * _shift_dn(a, k, 1.0)
        else:
            b = jnp.concatenate([b[:k], a[k:] * b[:s - k] + b[k:]], axis=0)
            if not last:
                a = jnp.concatenate([a[:k], a[k:] * a[:s - k]], axis=0)
        k *= 2
    return b


def _scan_rev(c, v):
    s = c.shape[0]
    k = 1
    while k < s:
        last = 2 * k >= s
        if k % SUBLANES:
            v = c * _shift_up(v, k) + v
            if not last:
                c = c * _shift_up(c, k, 1.0)
        else:
            v = jnp.concatenate([c[:s - k] * v[k:] + v[:s - k], v[s - k:]], axis=0)
            if not last:
                c = jnp.concatenate([c[:s - k] * c[k:], c[s - k:]], axis=0)
        k *= 2
    return v


def _mm(a, w, *, name, trans_w=False, bias=None, resid=None, resid_scale=1.0):
    m, k = a.shape
    n = w.shape[0] if trans_w else w.shape[1]
    tm = _tile(m, min(1024, max(256, MM_LHS_ELEMS // k)), SUBLANES)
    tn = _tile(n, MM_TN, LANES)
    has_bias = bias is not None
    has_resid = resid is not None

    def body(*refs):
        a_ref, w_ref = refs[0], refs[1]
        pos = 2
        b_ref = r_ref = None
        if has_bias:
            b_ref = refs[pos]
            pos += 1
        if has_resid:
            r_ref = refs[pos]
            pos += 1
        o_ref = refs[pos]

        cols = pl.ds(pl.multiple_of(pl.program_id(1) * tn, LANES), tn)
        if trans_w:
            acc = lax.dot_general(a_ref[...], w_ref[cols, :], (((1,), (1,)), ((), ())), preferred_element_type=F32)
        else:
            acc = jnp.dot(a_ref[...], w_ref[:, cols], preferred_element_type=F32)
        if has_bias:
            acc = acc + b_ref[...]
        if has_resid:
            acc = acc + resid_scale * r_ref[...]
        o_ref[...] = acc

    in_specs = [pl.BlockSpec((tm, k), lambda i, j: (i, 0)),
                pl.BlockSpec(w.shape, lambda i, j: (0, 0), pipeline_mode=pl.Buffered(1))]
    args = [a, w]
    if has_bias:
        in_specs.append(pl.BlockSpec((1, tn), lambda i, j: (0, j)))
        args.append(bias)
    if has_resid:
        in_specs.append(pl.BlockSpec((tm, tn), lambda i, j: (i, j)))
        args.append(resid)
    return pl.pallas_call(
        body, name=name, grid=(m // tm, n // tn), in_specs=in_specs,
        out_specs=pl.BlockSpec((tm, tn), lambda i, j: (i, j)),
        out_shape=jax.ShapeDtypeStruct((m, n), F32),
        compiler_params=_params(("parallel", "arbitrary")),
    )(*args)


def _ln(z, g, b):
    mu = jnp.mean(z, axis=-1, keepdims=True)
    zc = z - mu
    var = jnp.mean(zc * zc, axis=-1, keepdims=True)
    return zc * lax.rsqrt(var + LN_EPS) * g + b


def _mm_ln(a, w, resid, alpha, g, b, *, name, resid_ln=None, tm=1024):
    m, k = a.shape
    d = w.shape[1]
    tm = _tile(m, tm, SUBLANES)
    n_extra = 0 if resid_ln is None else 2

    def body(a_ref, w_ref, r_ref, g_ref, b_ref, *rest):
        z_ref, obf_ref = rest[n_extra:]
        x = r_ref[...]
        if resid_ln is not None:
            x = _ln(x, rest[0][...], rest[1][...])
        z = alpha * x + jnp.dot(a_ref[...], w_ref[...], preferred_element_type=F32)
        z_ref[...] = z
        obf_ref[...] = _ln(z, g_ref[...], b_ref[...]).astype(BF16)

    row = pl.BlockSpec((tm, d), lambda i: (i, 0))
    vec = pl.BlockSpec((1, d), lambda i: (0, 0))
    return pl.pallas_call(
        body, name=name, grid=(m // tm,),
        in_specs=[pl.BlockSpec((tm, k), lambda i: (i, 0)),
                  pl.BlockSpec((k, d), lambda i: (0, 0), pipeline_mode=pl.Buffered(1)), row, vec, vec]
        + [vec] * n_extra,
        out_specs=[row, row],
        out_shape=[jax.ShapeDtypeStruct((m, d), F32), jax.ShapeDtypeStruct((m, d), BF16)],
        compiler_params=_params(("parallel",)),
    )(a, w, resid, g, b, *(resid_ln or ()))


def _ln_bwd_math(do, z, g):
    mu = jnp.mean(z, axis=-1, keepdims=True)
    zc = z - mu
    var = jnp.mean(zc * zc, axis=-1, keepdims=True)
    rstd = lax.rsqrt(var + LN_EPS)
    xhat = zc * rstd
    dxh = do * g
    m1 = jnp.mean(dxh, axis=-1, keepdims=True)
    m2 = jnp.mean(dxh * xhat, axis=-1, keepdims=True)
    return rstd * (dxh - m1 - xhat * m2), _colsum(do * xhat), _colsum(do)


def _mm_ln_bwd(parts, w, resid, resid_scale, z, g, *, name, w_rows_are_k=False):
    t, kp = parts[0].shape
    k, d = w.shape if w_rows_are_k else w.shape[::-1]
    n = len(parts)
    tm = _tile(t, min(512, max(256, MM_LHS_ELEMS // k)), SUBLANES)

    def body(*refs):
        a_refs = refs[:n]
        w_ref, r_ref, z_ref, g_ref, dz_ref, dzbf_ref, dg_ref, db_ref = refs[n:]

        @pl.when(pl.program_id(0) == 0)
        def _():
            dg_ref[...] = jnp.zeros_like(dg_ref)
            db_ref[...] = jnp.zeros_like(db_ref)

        dx = resid_scale * r_ref[...]
        for p, a_ref in enumerate(a_refs):
            if w_rows_are_k:
                dx = dx + jnp.dot(a_ref[...], w_ref[p * kp:(p + 1) * kp, :], preferred_element_type=F32)
            else:
                dx = dx + lax.dot_general(a_ref[...], w_ref[:, p * kp:(p + 1) * kp], (((1,), (1,)), ((), ())),
                                          preferred_element_type=F32)
        dz, dg, db = _ln_bwd_math(dx, z_ref[...], g_ref[...])
        dz_ref[...] = dz
        dzbf_ref[...] = dz.astype(BF16)
        dg_ref[...] += dg
        db_ref[...] += db

    row = pl.BlockSpec((tm, d), lambda i: (i, 0))
    vec = pl.BlockSpec((1, d), lambda i: (0, 0))
    return pl.pallas_call(
        body, name=name, grid=(t // tm,),
        in_specs=[pl.BlockSpec((tm, kp), lambda i: (i, 0))] * n
        + [pl.BlockSpec(w.shape, lambda i: (0, 0), pipeline_mode=pl.Buffered(1)), row, row, vec],
        out_specs=[row, row, vec, vec],
        out_shape=[jax.ShapeDtypeStruct((t, d), F32), jax.ShapeDtypeStruct((t, d), BF16),
                   jax.ShapeDtypeStruct((1, d), F32), jax.ShapeDtypeStruct((1, d), F32)],
        compiler_params=_params(("arbitrary",)),
    )(*parts, w, resid, z, g)


def _mm_tn(a, b, *, name, below=None, tm=1408, tn=1536, tk=2048):
    t, m = a.shape
    n = b.shape[1]
    tm = _tile(m, tm, LANES)
    tn = _tile(n, tn, LANES)
    tk = _tile(t, tk, SUBLANES)
    last = t // tk - 1
    rows, earlier = (m, None) if below is None else below
    skip = (rows - m) // tm if earlier is not None else 0

    def body(a_ref, b_ref, *rest):
        o_ref, acc = rest[-2:]

        @pl.when(pl.program_id(2) == 0)
        def _():
            acc[...] = jnp.zeros_like(acc)

        acc[...] += lax.dot_general(a_ref[...], b_ref[...], (((0,), (0,)), ((), ())), preferred_element_type=F32)

        @pl.when(pl.program_id(2) == last)
        def _():
            o_ref[...] = acc[...].astype(BF16)

    in_specs = [pl.BlockSpec((tk, tm), lambda i, j, l: (l, i)), pl.BlockSpec((tk, tn), lambda i, j, l: (l, j))]
    return pl.pallas_call(
        body, name=name, grid=(m // tm, n // tn, t // tk),
        in_specs=in_specs + ([] if earlier is None else [pl.BlockSpec(memory_space=pl.ANY)]),
        out_specs=pl.BlockSpec((tm, tn), lambda i, j, l: (i + skip, j)),
        out_shape=jax.ShapeDtypeStruct((rows, n), BF16),
        input_output_aliases={} if earlier is None else {2: 0},
        scratch_shapes=[pltpu.VMEM((tm, tn), F32)],
        compiler_params=_params(("parallel", "parallel", "arbitrary")),
    )(a, b, *(() if earlier is None else (earlier,)))


def _ln_bwd(dout, z, g, *, name, tm=1024):
    t, d = z.shape
    tm = _tile(t, tm, SUBLANES)

    def body(do_ref, z_ref, g_ref, dz_ref, dzbf_ref, dg_ref, db_ref):
        @pl.when(pl.program_id(0) == 0)
        def _():
            dg_ref[...] = jnp.zeros_like(dg_ref)
            db_ref[...] = jnp.zeros_like(db_ref)

        dz, dg, db = _ln_bwd_math(do_ref[...], z_ref[...], g_ref[...])
        dz_ref[...] = dz
        dzbf_ref[...] = dz.astype(BF16)
        dg_ref[...] += dg
        db_ref[...] += db

    row = pl.BlockSpec((tm, d), lambda i: (i, 0))
    vec = pl.BlockSpec((1, d), lambda i: (0, 0))
    return pl.pallas_call(
        body, name=name, grid=(t // tm,), in_specs=[row, row, vec], out_specs=[row, row, vec, vec],
        out_shape=[jax.ShapeDtypeStruct((t, d), F32), jax.ShapeDtypeStruct((t, d), BF16),
                   jax.ShapeDtypeStruct((1, d), F32), jax.ShapeDtypeStruct((1, d), F32)],
        compiler_params=_params(("arbitrary",)),
    )(dout, z, g)


def _loss_head(z, g, b, target, *, name, tm=1024):
    t, d = z.shape
    tm = _tile(t, tm, SUBLANES)

    def body(z_ref, g_ref, b_ref, t_ref, s_ref, dy_ref):
        @pl.when(pl.program_id(0) == 0)
        def _():
            s_ref[...] = jnp.zeros_like(s_ref)

        e = _ln(z_ref[...], g_ref[...], b_ref[...]) - t_ref[...]
        dy_ref[...] = e * (1.0 / d)
        s_ref[...] += jnp.sum(_colsum(e * e), axis=-1, keepdims=True)

    row = pl.BlockSpec((tm, d), lambda i: (i, 0))
    vec = pl.BlockSpec((1, d), lambda i: (0, 0))
    return pl.pallas_call(
        body, name=name, grid=(t // tm,), in_specs=[row, vec, vec, row],
        out_specs=[pl.BlockSpec((1, LANES), lambda i: (0, 0)), row],
        out_shape=[jax.ShapeDtypeStruct((1, LANES), F32), jax.ShapeDtypeStruct((t, d), F32)],
        compiler_params=_params(("arbitrary",)),
    )(z, g, b, target)


def _own(c, b, *_):
    return c, b


def _ahead(nc, bsz):
    def at(c, b, part):
        b2 = b + jnp.minimum(part, 1)
        return jnp.minimum(c + b2 // bsz, nc - 1), b2 % bsz
    return at


def _strip(s, tc, off, at=_own):
    def index(*ids):
        c, b = at(*ids)
        return b, 0, off + c
    return pl.BlockSpec((None, s, tc), index)


def _cvec(kw, tc, off, at=_own):
    def index(*ids):
        return 0, off + at(*ids)[0]
    return pl.BlockSpec((kw, tc), index)


def _acc(kw, tc):
    return pl.BlockSpec((kw, tc), lambda c, b, *_: (0, c))


def _sc_fwd(h, cw, cb, *, name, tc=256):
    bsz, s, d3 = h.shape
    d = d3 // 3
    tc = _tile(d, tc, LANES)
    nc = d // tc

    def body(gb_ref, gc_ref, v_ref, w_ref, b_ref, q_ref):
        u = _conv_fwd(gc_ref[...] * v_ref[...], w_ref[...], b_ref[...])
        q_ref[...] = (gb_ref[...] * u).astype(BF16)

    return pl.pallas_call(
        body, name=name, grid=(nc, bsz),
        in_specs=[_strip(s, tc, 0), _strip(s, tc, nc), _strip(s, tc, 2 * nc), _cvec(cw.shape[0], tc, 0), _cvec(1, tc, 0)],
        out_specs=_strip(s, tc, 0),
        out_shape=jax.ShapeDtypeStruct((bsz, s, d), BF16),
        compiler_params=_params(("parallel", "parallel")),
    )(h, h, h, cw, cb)


def _sc_bwd(h, dq, cw, cb, *, name, tc=256):
    bsz, s, d3 = h.shape
    d = d3 // 3
    kw = cw.shape[0]
    tc = _tile(d, tc, LANES)
    nc = d // tc

    def body(gb_ref, gc_ref, v_ref, dq_ref, w_ref, b_ref, dh_ref, dw_ref, db_ref, parts):
        b_id, part = pl.program_id(1), pl.program_id(2)

        @pl.when(part == 0)
        def _():
            gb, gc, v, dq_, w = gb_ref[...], gc_ref[...], v_ref[...], dq_ref[...], w_ref[...]
            p = gc * v
            u = _conv_fwd(p, w, b_ref[...])
            du = dq_ * gb
            dp, dw_rows = _conv_bwd(du, p, w)
            parts[0] = (dq_ * u).astype(BF16)
            parts[1] = (dp * v).astype(BF16)
            parts[2] = (dp * gc).astype(BF16)
            _accumulate(b_id == 0, [(dw_ref, dw_rows), (db_ref, _colsum(du))])

        dh_ref[...] = parts[part]

    at = _ahead(nc, bsz)
    return pl.pallas_call(
        body, name=name, grid=(nc, bsz, 3),
        in_specs=[_strip(s, tc, 0, at), _strip(s, tc, nc, at), _strip(s, tc, 2 * nc, at), _strip(s, tc, 0, at),
                  _cvec(kw, tc, 0, at), _cvec(1, tc, 0, at)],
        out_specs=[pl.BlockSpec((None, s, tc), lambda c, b, p: (b, 0, p * nc + c)), _acc(kw, tc), _acc(1, tc)],
        out_shape=[jax.ShapeDtypeStruct((bsz, s, d3), BF16), jax.ShapeDtypeStruct((kw, d), F32),
                   jax.ShapeDtypeStruct((1, d), F32)],
        scratch_shapes=[pltpu.VMEM((3, s, tc), BF16)],
        compiler_params=_params(("parallel", "arbitrary", "arbitrary")),
    )(h, h, h, dq, cw, cb)


def _ffn_specs(s, tc, nc, kw):
    strip = pl.BlockSpec((None, s, tc), lambda b, c: (b, 0, c))
    halves = [pl.BlockSpec((kw, tc), lambda b, c: (0, c)), pl.BlockSpec((kw, tc), lambda b, c: (0, nc + c)),
              pl.BlockSpec((1, tc), lambda b, c: (0, c)), pl.BlockSpec((1, tc), lambda b, c: (0, nc + c))]
    return strip, halves


def _ffn_fwd(x, w_up, cw, cb, *, name, tc=256):
    bsz, s, d = x.shape
    f = w_up.shape[0] // 2
    kw = cw.shape[0]
    tc = _tile(f, tc, LANES)
    nc = f // tc
    nt = (((1,), (1,)), ((), ()))

    def body(x_ref, w_ref, wg_ref, wv_ref, bg_ref, bv_ref, hg_ref, hv_ref, g_ref, v_ref, a_ref):
        c0 = pl.multiple_of(pl.program_id(1) * tc, LANES)
        xs = x_ref[...]
        hg = lax.dot_general(xs, w_ref[pl.ds(c0, tc), :], nt, preferred_element_type=F32)
        hv = lax.dot_general(xs, w_ref[pl.ds(f + c0, tc), :], nt, preferred_element_type=F32)
        hg_ref[...] = hg
        hv_ref[...] = hv
        for lo in range(0, tc, LANES):
            sl = slice(lo, lo + LANES)
            g = _conv_fwd(hg[:, sl], wg_ref[:, sl], bg_ref[:, sl])
            v = _conv_fwd(hv[:, sl], wv_ref[:, sl], bv_ref[:, sl])
            g_ref[:, sl] = g
            v_ref[:, sl] = v
            a_ref[:, sl] = (g * _sigmoid(g) * v).astype(BF16)

    strip, halves = _ffn_specs(s, tc, nc, kw)
    return pl.pallas_call(
        body, name=name, grid=(bsz, nc),
        in_specs=[pl.BlockSpec((None, s, d), lambda b, c: (b, 0, 0)),
                  pl.BlockSpec(w_up.shape, lambda b, c: (0, 0), pipeline_mode=pl.Buffered(1))] + halves,
        out_specs=[strip] * 5,
        out_shape=[jax.ShapeDtypeStruct((bsz, s, f), F32)] * 4 + [jax.ShapeDtypeStruct((bsz, s, f), BF16)],
        compiler_params=_params(("parallel", "arbitrary")),
    )(x, w_up, cw, cw, cb, cb)


def _ffn_bwd(hg, hv, g, v, dz, w_down, cw, *, name, tc=256):
    bsz, s, f = hg.shape
    d = dz.shape[2]
    kw = cw.shape[0]
    tc = _tile(f, tc, LANES)
    nc = f // tc

    def body(hg_ref, hv_ref, g_ref, v_ref, dz_ref, wd_ref, wg_ref, wv_ref,
             dhg_ref, dhv_ref, dwg_ref, dwv_ref, dbg_ref, dbv_ref):
        c0 = pl.multiple_of(pl.program_id(1) * tc, LANES)
        cols = pl.ds(c0, tc)
        da = lax.dot_general(dz_ref[...], wd_ref[cols, :], (((1,), (1,)), ((), ())), preferred_element_type=F32)
        for lo in range(0, tc, LANES):
            sl = slice(lo, lo + LANES)
            g_, da_ = g_ref[:, sl], da[:, sl]
            sg = _sigmoid(g_)
            dv = da_ * (g_ * sg)
            dg = da_ * v_ref[:, sl] * (sg * (1.0 + g_ * (1.0 - sg)))
            dhg, dwg_rows = _conv_bwd(dg, hg_ref[:, sl], wg_ref[:, sl])
            dhv, dwv_rows = _conv_bwd(dv, hv_ref[:, sl], wv_ref[:, sl])
            dhg_ref[:, sl] = dhg.astype(BF16)
            dhv_ref[:, sl] = dhv.astype(BF16)
            _accumulate(pl.program_id(0) == 0, [(dwg_ref, dwg_rows), (dwv_ref, dwv_rows),
                                                (dbg_ref, [_colsum(dg)]), (dbv_ref, [_colsum(dv)])],
                        pl.ds(pl.multiple_of(c0 + lo, LANES), LANES))

    strip, halves = _ffn_specs(s, tc, nc, kw)
    whole = lambda r: pl.BlockSpec((r, f), lambda b, c: (0, 0))
    return pl.pallas_call(
        body, name=name, grid=(bsz, nc),
        in_specs=[strip] * 4 + [pl.BlockSpec((None, s, d), lambda b, c: (b, 0, 0)),
                                pl.BlockSpec(w_down.shape, lambda b, c: (0, 0), pipeline_mode=pl.Buffered(1))]
        + halves[:2],
        out_specs=[strip, strip, whole(kw), whole(kw), whole(1), whole(1)],
        out_shape=[jax.ShapeDtypeStruct((bsz, s, f), BF16), jax.ShapeDtypeStruct((bsz, s, f), BF16),
                   jax.ShapeDtypeStruct((kw, f), F32), jax.ShapeDtypeStruct((kw, f), F32),
                   jax.ShapeDtypeStruct((1, f), F32), jax.ShapeDtypeStruct((1, f), F32)],
        compiler_params=_params(("arbitrary", "arbitrary")),
    )(hg, hv, g, v, dz, w_down, cw, cw)


def _lru_gates(r, cw, cb, wg, bg, lam):
    blk = r.shape[1]
    xr = _conv_fwd(r, cw, cb)
    gates = jnp.dot(xr.astype(BF16), wg, preferred_element_type=F32) + bg
    rg = _sigmoid(gates[:, :blk])
    ig = _sigmoid(gates[:, blk:])
    sp = _softplus(-lam)
    la = (-LRU_C * sp) * rg
    a = jnp.exp(la)
    mult = jnp.sqrt(-_expm1(2.0 * la, a * a))
    return xr, rg, ig, sp, a, mult


def _lru_fwd(h, cw, cb, wg, bg, lam, *, name):
    bsz, s, r2 = h.shape
    heads, blk = wg.shape[0], wg.shape[1]
    kw = cw.shape[0]

    def body(g_ref, r_ref, cw_ref, cb_ref, wg_ref, bg_ref, lam_ref, y_ref, sv_ref):
        xr, rg, ig, _, a, mult = _lru_gates(r_ref[...], cw_ref[...], cb_ref[...], wg_ref[...], bg_ref[...], lam_ref[...])
        hs = _scan_fwd(a, mult * (ig * xr))
        for n, val in enumerate((hs, xr, rg, ig, a, mult)):
            sv_ref[n] = val
        y_ref[...] = (hs * _gelu(g_ref[...])).astype(BF16)

    per_head = lambda hd, b: (hd, 0, 0)
    return pl.pallas_call(
        body, name=name, grid=(heads, bsz),
        in_specs=[_strip(s, blk, 0), _strip(s, blk, heads), _cvec(kw, blk, 0), _cvec(1, blk, 0),
                  pl.BlockSpec((None, blk, 2 * blk), per_head), pl.BlockSpec((None, 1, 2 * blk), per_head),
                  _cvec(1, blk, 0)],
        out_specs=[_strip(s, blk, 0), pl.BlockSpec((6, None, s, blk), lambda hd, b: (0, b, 0, hd))],
        out_shape=[jax.ShapeDtypeStruct((bsz, s, r2 // 2), BF16), jax.ShapeDtypeStruct((6, bsz, s, r2 // 2), F32)],
        compiler_params=_params(("parallel", "parallel")),
    )(h, h, cw, cb, wg, bg, lam)


def _lru_bwd(h, sv, dy, cw, wg, lam, *, name):
    bsz, s, r2 = h.shape
    rw = r2 // 2
    heads, blk = wg.shape[0], wg.shape[1]
    kw = cw.shape[0]

    def body(g_ref, r_ref, cw_ref, wg_ref, lam_ref, sv_ref, dy_ref,
             dh_ref, dcw_ref, dcb_ref, dwg_ref, dbg_ref, dlam_ref, sg_ref, sr_ref, parts):
        b_id, part = pl.program_id(1), pl.program_id(2)

        @pl.when(part == 0)
        def _():
            r, cw_, wg_, lam_ = r_ref[...], cw_ref[...], wg_ref[...], lam_ref[...]
            hs_, xr, rg, ig, a, mult = (sv_ref[n] for n in range(6))
            sp = _softplus(-lam_)
            dy_ = dy_ref[...]
            gel, dgel = _gelu_and_grad(g_ref[...])
            dg = dy_ * hs_ * dgel
            lmb = _scan_rev(_shift_up(a, 1, 1.0), dy_ * gel)
            da = lmb * _shift_dn(hs_, 1)
            dmult = lmb * (ig * xr)
            dig = lmb * (mult * xr)
            dxr = lmb * (mult * ig)
            dla = da * a - dmult * (a * a / mult)
            drg = dla * (-LRU_C * sp)
            dsp = _colsum(dla * rg) * (-LRU_C)
            dlam = -dsp * _sigmoid(-lam_)
            dgates = jnp.concatenate([drg * (rg * (1.0 - rg)), dig * (ig * (1.0 - ig))], axis=1)
            dgates_bf = dgates.astype(BF16)
            dwg = lax.dot_general(xr.astype(BF16), dgates_bf, (((0,), (0,)), ((), ())), preferred_element_type=F32)
            dxr = dxr + lax.dot_general(dgates_bf, wg_, (((1,), (1,)), ((), ())), preferred_element_type=F32)
            dr, dcw_rows = _conv_bwd(dxr, r, cw_)
            parts[0] = dg.astype(BF16)
            parts[1] = dr.astype(BF16)
            _accumulate(b_id == 0, [(dcw_ref, dcw_rows), (dcb_ref, _colsum(dxr)), (dwg_ref, dwg),
                                    (dbg_ref, _colsum(dgates)), (dlam_ref, dlam), (sg_ref, _colsum(dg)),
                                    (sr_ref, _colsum(dr))])

        dh_ref[...] = parts[part]

    at = _ahead(heads, bsz)

    def saved(*ids):
        hd, b = at(*ids)
        return 0, b, 0, hd

    vec = pl.BlockSpec((1, blk), lambda hd, b, p: (0, hd))
    return pl.pallas_call(
        body, name=name, grid=(heads, bsz, 2),
        in_specs=[_strip(s, blk, 0, at), _strip(s, blk, heads, at), _cvec(kw, blk, 0, at),
                  pl.BlockSpec((None, blk, 2 * blk), lambda *ids: (at(*ids)[0], 0, 0)), _cvec(1, blk, 0, at),
                  pl.BlockSpec((6, None, s, blk), saved), _strip(s, blk, 0, at)],
        out_specs=[pl.BlockSpec((None, s, blk), lambda hd, b, p: (b, 0, p * heads + hd)),
                   pl.BlockSpec((kw, blk), lambda hd, b, p: (0, hd)), vec,
                   pl.BlockSpec((None, blk, 2 * blk), lambda hd, b, p: (hd, 0, 0)),
                   pl.BlockSpec((None, 1, 2 * blk), lambda hd, b, p: (hd, 0, 0)), vec, vec, vec],
        out_shape=[jax.ShapeDtypeStruct((bsz, s, r2), BF16), jax.ShapeDtypeStruct((kw, rw), F32),
                   jax.ShapeDtypeStruct((1, rw), F32), jax.ShapeDtypeStruct((heads, blk, 2 * blk), F32),
                   jax.ShapeDtypeStruct((heads, 1, 2 * blk), F32), jax.ShapeDtypeStruct((1, rw), F32),
                   jax.ShapeDtypeStruct((1, rw), F32), jax.ShapeDtypeStruct((1, rw), F32)],
        scratch_shapes=[pltpu.VMEM((2, s, blk), BF16)],
        compiler_params=_params(("parallel", "arbitrary", "arbitrary")),
    )(h, h, cw, wg, lam, sv, dy)


HBM_SPEC = pl.BlockSpec(memory_space=pltpu.HBM)
SEM_SPEC = pl.BlockSpec(memory_space=pltpu.SEMAPHORE)
EFFECT = pltpu.SideEffectType.DATAFLOW_SIDE_EFFECTING


def _peer_copies(srcs, lands, gather, send_sem, recv_sem):
    x, y, c = (lax.axis_index(ax) for ax in MESH_AXES)
    me = 4 * x + 2 * y + c
    copies = []
    for i in range(len(srcs)):
        for d in range(1, N_DEV):
            px = 1 - x if d & 4 else x
            py = 1 - y if d & 2 else y
            pc = 1 - c if d & 1 else c
            src = srcs[i] if gather[i] else srcs[i].at[4 * px + 2 * py + pc]
            k = i * (N_DEV - 1) + d - 1
            copies.append(pltpu.make_async_remote_copy(
                src_ref=src, dst_ref=lands[i].at[me], send_sem=send_sem.at[k], recv_sem=recv_sem.at[k],
                device_id=(px, py, pc), device_id_type=pl.DeviceIdType.MESH))
    return copies


def _exchange_start(arrs, gather, *, name):
    n = len(arrs)
    lands = [lax.empty((N_DEV,) + tuple(a.shape if g else a.shape[1:]), a.dtype) for a, g in zip(arrs, gather)]

    def body(*refs):
        srcs, land_refs = refs[:n], refs[n:2 * n]
        send_sem, recv_sem = refs[2 * n], refs[2 * n + 1]
        token = refs[-1]
        for cp in _peer_copies(srcs, land_refs, gather, send_sem, recv_sem):
            cp.start()
        token[...] = jnp.zeros_like(token)

    sems = pltpu.SemaphoreType.DMA((n * (N_DEV - 1),))
    thru = [pltpu.HBM(a.shape, a.dtype) for a in arrs + lands]
    out = pl.pallas_call(
        body, name=name, in_specs=[HBM_SPEC] * (2 * n),
        out_shape=(sems, sems, *thru, jax.ShapeDtypeStruct((SUBLANES, LANES), F32)),
        out_specs=(SEM_SPEC, SEM_SPEC, *([HBM_SPEC] * (2 * n)), pl.BlockSpec(memory_space=pltpu.VMEM)),
        input_output_aliases={i: 2 + i for i in range(2 * n)},
        compiler_params=pltpu.CompilerParams(has_side_effects=EFFECT),
    )(*[pltpu.with_memory_space_constraint(a, pltpu.HBM) for a in arrs + lands])
    return {"send_sem": out[0], "recv_sem": out[1], "srcs": list(out[2:2 + n]), "lands": list(out[2 + n:2 + 2 * n]),
            "token": out[-1], "gather": list(gather)}


def _exchange_wait(handle, after, *, name):
    srcs, lands, gather = handle["srcs"], handle["lands"], handle["gather"]
    n = len(srcs)

    def body(*refs):
        src_refs, land_refs = refs[:n], refs[n:2 * n]
        send_sem, recv_sem = refs[2 * n], refs[2 * n + 1]
        for cp in _peer_copies(src_refs, land_refs, gather, send_sem, recv_sem):
            cp.wait_send()
            cp.wait_recv()

    out = pl.pallas_call(
        body, name=name,
        in_specs=[HBM_SPEC] * (2 * n) + [SEM_SPEC, SEM_SPEC, pl.BlockSpec(memory_space=pl.ANY)],
        out_shape=tuple(pltpu.HBM(a.shape, a.dtype) for a in srcs + lands), out_specs=tuple([HBM_SPEC] * (2 * n)),
        input_output_aliases={i: i for i in range(2 * n)},
        compiler_params=pltpu.CompilerParams(has_side_effects=EFFECT),
    )(*srcs, *lands, handle["send_sem"], handle["recv_sem"], after)
    return list(out[:n]), list(out[n:])


def _layers_bf16(stacks, *, name):
    counts = [a.shape[0] for a in stacks]

    def body(*refs):
        outs = iter(refs[len(stacks):])
        for i_ref, n_layers in zip(refs, counts):
            for layer in range(n_layers):
                next(outs)[...] = i_ref[layer].astype(BF16)

    flat = pl.pallas_call(
        body, name=name,
        out_shape=[jax.ShapeDtypeStruct(a.shape[1:], BF16) for a in stacks for _ in range(a.shape[0])],
        compiler_params=pltpu.CompilerParams(vmem_limit_bytes=VMEM_LIMIT),
    )(*stacks)
    split, pos = [], 0
    for n_layers in counts:
        split.append(list(flat[pos:pos + n_layers]))
        pos += n_layers
    return split


def _adamw(parts, w, m, v, layer, so_far, *, name, tr=512):
    n_layers, r, c = w.shape
    tr = _tile(r, tr, SUBLANES)
    bc1 = 1.0 / (1.0 - ADAM_B1 ** ADAM_STEP)
    bc2 = 1.0 / (1.0 - ADAM_B2 ** ADAM_STEP)
    if so_far is None:
        so_far = [lax.empty(w.shape, F32) for _ in range(4)]

    def body(p_ref, w_ref, m_ref, v_ref, *rest):
        g_ref, d_ref, mo_ref, vo_ref = rest[4:]
        g = p_ref[0].astype(F32)
        for s in range(1, N_DEV):
            g = g + p_ref[s].astype(F32)
        m_new = ADAM_B1 * m_ref[...] + (1.0 - ADAM_B1) * g
        v_new = ADAM_B2 * v_ref[...] + (1.0 - ADAM_B2) * (g * g)
        g_ref[...] = g
        mo_ref[...] = m_new
        vo_ref[...] = v_new
        d_ref[...] = -ADAM_LR * ((m_new * bc1) / (jnp.sqrt(v_new * bc2) + ADAM_EPS) + ADAM_WD * w_ref[...])

    blk = pl.BlockSpec((None, tr, c), lambda i: (layer, i, 0))
    return pl.pallas_call(
        body, name=name, grid=(r // tr,),
        in_specs=[pl.BlockSpec((N_DEV, tr, c), lambda i: (0, i, 0)), blk, blk, blk]
        + [pl.BlockSpec(memory_space=pl.ANY)] * 4,
        out_specs=[blk] * 4, out_shape=[jax.ShapeDtypeStruct(w.shape, F32)] * 4,
        input_output_aliases={4 + o: o for o in range(4)},
        compiler_params=_params(("parallel",)),
    )(parts, w, m, v, *so_far)


def _whole(slabs, axis):
    x = jnp.moveaxis(slabs, 0, axis)
    shp = x.shape
    return x.reshape(shp[:axis] + (shp[axis] * shp[axis + 1],) + shp[axis + 2:])


def _slabs(whole, axis):
    shp = whole.shape
    x = whole.reshape(shp[:axis] + (N_DEV, shp[axis] // N_DEV) + shp[axis + 1:])
    return jnp.moveaxis(x, axis, 0)


BIG = {"sc_w_in": 2, "sc_w_out": 1, "lru_w_in": 2, "lru_w_gate": 3, "lru_w_out": 1, "ffn_w_up": 2, "ffn_w_down": 1}
TRANSPOSED = ("ffn_w_up", "lru_w_in")
SMALL = ["sc_conv_w", "lru_b_in", "lru_conv_w", "lru_conv_b", "lru_b_gate", "lru_lambda", "ffn_conv_w", "ln_g", "ln_b"]
REPL = ["sc_conv_b", "ffn_conv_b"]
WEIGHTS = ["sc_w_in", "sc_conv_w", "sc_conv_b", "sc_w_out", "lru_w_in", "lru_b_in", "lru_conv_w", "lru_conv_b",
           "lru_w_gate", "lru_b_gate", "lru_lambda", "lru_w_out", "ffn_w_up", "ffn_conv_w", "ffn_conv_b", "ffn_w_down",
           "ln_g", "ln_b"]


STAGES_PER_LAYER = 3


def _stage_big(g):
    i, part = divmod(g, STAGES_PER_LAYER)
    j = i // 2
    if part:
        return [("ffn_w_up" if part == 1 else "ffn_w_down", i)]
    return [("sc_w_in", j), ("sc_w_out", j)] if i % 2 == 0 else [("lru_w_in", j), ("lru_w_gate", j), ("lru_w_out", j)]


def _step(x, loss_target, w, m, v):
    bsz, s, d = x.shape
    t = bsz * s
    depth = w["ffn_w_up"].shape[0]
    alpha = (2.0 * depth) ** 0.25
    heads = w["lru_w_gate"].shape[1]

    me = 4 * lax.axis_index("x") + 2 * lax.axis_index("y") + lax.axis_index("c")

    def with_own(land, own):
        return lax.dynamic_update_slice_in_dim(land, own, me, axis=0)

    stages = STAGES_PER_LAYER * depth
    def held(k, arr):
        return jnp.swapaxes(arr, -1, -2) if k in TRANSPOSED else arr

    def split_axis(k):
        return 0 if k in TRANSPOSED else BIG[k] - 1

    flat_names = [k for k in BIG if w[k].ndim == 3]
    wb = dict(zip(flat_names, _layers_bf16([held(k, w[k]) for k in flat_names], name="weights_bf16")))
    wb.update({k: list(w[k].astype(BF16)) for k in BIG if k not in flat_names})

    gathers, tok = [], None
    for g in range(stages):
        arrs = [wb[k][l] for k, l in _stage_big(g)]
        if g == 0:
            arrs += [w[k] for k in SMALL]
        if tok is not None:
            arrs[0] = arrs[0] + tok.astype(BF16)
        gathers.append(_exchange_start(arrs, [True] * len(arrs), name=f"gather_start_{g}"))
        tok = gathers[-1]["token"][0, 0]
    full = {k: [None] * w[k].shape[0] for k in BIG}
    full["sc_conv_b"] = w["sc_conv_b"]
    full["ffn_conv_b"] = w["ffn_conv_b"]

    def arrive(g, after):
        srcs, lands = _exchange_wait(gathers[g], after, name=f"gather_wait_{g}")
        for (k, l), src, land in zip(_stage_big(g), srcs, lands):
            full[k][l] = _whole(with_own(land, src[None]), split_axis(k))
        if g == 0:
            n_big = len(_stage_big(0))
            for k, src, land in zip(SMALL, srcs[n_big:], lands[n_big:]):
                full[k] = _whole(with_own(land, src[None]), w[k].ndim - 1)

    stream, stream_ln = x.reshape(t, d), None
    xb = stream.astype(BF16)
    saved = []
    for i in range(depth):
        j = i // 2
        arrive(3 * i, gathers[-1]["token"] if i == 0 else xb)
        lng, lnb = full["ln_g"][i], full["ln_b"][i]
        sv = {"x0": xb}
        if i % 2 == 0:
            hm = _mm(xb, full["sc_w_in"][j], name="sc_in")
            q = _sc_fwd(hm.reshape(bsz, s, -1), full["sc_conv_w"][j], full["sc_conv_b"][j:j + 1], name="sc_mix")
            w_out = full["sc_w_out"][j]
        else:
            hm = _mm(xb, full["lru_w_in"][j], trans_w=True, bias=full["lru_b_in"][j:j + 1], name="lru_in")
            q, hs = _lru_fwd(hm.reshape(bsz, s, -1), full["lru_conv_w"][j], full["lru_conv_b"][j:j + 1],
                             full["lru_w_gate"][j], full["lru_b_gate"][j].reshape(heads, 1, -1),
                             full["lru_lambda"][j:j + 1], name="lru_mix")
            sv["hs"] = hs
            w_out = full["lru_w_out"][j]
        q = q.reshape(t, -1)
        arrive(3 * i + 1, q)
        z1, x1b = _mm_ln(q, w_out, stream, alpha, lng[0:1], lnb[0:1], resid_ln=stream_ln, name="mix_out_ln")
        hg, hv, gc, vc, a = _ffn_fwd(x1b.reshape(bsz, s, d), full["ffn_w_up"][i], full["ffn_conv_w"][i],
                                     full["ffn_conv_b"][i:i + 1], name="ffn_up_act")
        a = a.reshape(t, -1)
        arrive(3 * i + 2, a)
        z2, xb = _mm_ln(a, full["ffn_w_down"][i], z1, alpha, lng[1:2], lnb[1:2], resid_ln=(lng[0:1], lnb[0:1]),
                        name="ffn_down_ln")
        stream, stream_ln = z2, (lng[1:2], lnb[1:2])
        sv.update(hm=hm, q=q, z1=z1, x1=x1b, ffn=(hg, hv, gc, vc), a=a, z2=z2)
        saved.append(sv)

    sq, dx = _loss_head(stream, *stream_ln, loss_target.reshape(t, d), name="loss_head")
    loss = lax.psum((0.5 / d) * sq[0, 0], MESH_AXES)

    grads = {k: [None] * w[k].shape[0] for k in WEIGHTS}
    scatters = [None] * stages

    def depart(g):
        send = [_slabs(grads[k][l], split_axis(k)).astype(BF16) for k, l in _stage_big(g)]
        scatters[g] = _exchange_start(send, [False] * len(send), name=f"scatter_start_{g}")
        return scatters[g]["token"][0:1, 0:1]

    dz2, dz2b, dg2, db2 = _ln_bwd(dx, saved[-1]["z2"], full["ln_g"][-1][1:2], name="ln_bwd")
    for i in reversed(range(depth)):
        j = i // 2
        sv = saved[i]
        lng = full["ln_g"][i]
        grads["ffn_w_down"][i] = _mm_tn(sv["a"], dz2b, name="ffn_down_dw")
        dhg, dhv, dwg, dwv, dbg, dbv = _ffn_bwd(*sv["ffn"], dz2b.reshape(bsz, s, d), full["ffn_w_down"][i],
                                                full["ffn_conv_w"][i] + depart(3 * i + 2), name="ffn_act_bwd")
        dhg, dhv = dhg.reshape(t, -1), dhv.reshape(t, -1)
        grads["ffn_conv_w"][i] = jnp.concatenate([dwg, dwv], axis=1)
        grads["ffn_conv_b"][i] = jnp.concatenate([dbg, dbv], axis=1)[0]
        rows_up = 2 * dhg.shape[1]
        dw_g = _mm_tn(dhg, sv["x1"], below=(rows_up, None), name="ffn_up_dw_g")
        grads["ffn_w_up"][i] = _mm_tn(dhv, sv["x1"], below=(rows_up, dw_g), name="ffn_up_dw_v")
        dz1, dz1b, dg1, db1 = _mm_ln_bwd([dhg, dhv], full["ffn_w_up"][i], dz2, alpha, sv["z1"],
                                         lng[0:1] + depart(3 * i + 1), name="ffn_up_dx_ln", w_rows_are_k=True)
        grads["ln_g"][i] = jnp.concatenate([dg1, dg2], axis=0)
        grads["ln_b"][i] = jnp.concatenate([db1, db2], axis=0)
        if i % 2 == 0:
            dq = _mm(dz1b, full["sc_w_out"][j], trans_w=True, name="sc_out_dx")
            grads["sc_w_out"][j] = _mm_tn(sv["q"], dz1b, name="sc_out_dw")
            dhm, dcw, dcb = _sc_bwd(sv["hm"].reshape(bsz, s, -1), dq.reshape(bsz, s, -1), full["sc_conv_w"][j],
                                    full["sc_conv_b"][j:j + 1], name="sc_mix_bwd")
            dhm = dhm.reshape(t, -1)
            grads["sc_conv_w"][j] = dcw
            grads["sc_conv_b"][j] = dcb[0]
            grads["sc_w_in"][j] = _mm_tn(sv["x0"], dhm, name="sc_in_dw")
            w_in = full["sc_w_in"][j]
        else:
            dq = _mm(dz1b, full["lru_w_out"][j], trans_w=True, name="lru_out_dx")
            grads["lru_w_out"][j] = _mm_tn(sv["q"], dz1b, name="lru_out_dw")
            dhm, dcw, dcb, dwgt, dbgt, dlam, sgb, srb = _lru_bwd(
                sv["hm"].reshape(bsz, s, -1), sv["hs"], dq.reshape(bsz, s, -1), full["lru_conv_w"][j],
                full["lru_w_gate"][j], full["lru_lambda"][j:j + 1], name="lru_mix_bwd")
            dhm = dhm.reshape(t, -1)
            grads["lru_conv_w"][j] = dcw
            grads["lru_conv_b"][j] = dcb[0]
            grads["lru_w_gate"][j] = dwgt
            grads["lru_b_gate"][j] = dbgt[:, 0, :]
            grads["lru_lambda"][j] = dlam[0]
            grads["lru_b_in"][j] = jnp.concatenate([sgb, srb], axis=1)[0]
            grads["lru_w_in"][j] = _mm_tn(dhm, sv["x0"], name="lru_in_dw")
            w_in = full["lru_w_in"][j]
        tok = depart(3 * i)
        if i > 0:
            dz2, dz2b, dg2, db2 = _mm_ln_bwd([dhm], w_in, dz1, alpha, saved[i - 1]["z2"], full["ln_g"][i - 1][1:2] + tok,
                                             name="mix_in_dx_ln", w_rows_are_k=i % 2 == 1)
        else:
            dx = _mm(dhm, w_in + tok[0, 0].astype(BF16), trans_w=True, resid=dz1, resid_scale=alpha, name="mix_in_dx")
    grad_x = dx.reshape(bsz, s, d)

    gsm = {k: jnp.stack(grads[k]) for k in SMALL + REPL}
    small_scatter = _exchange_start([_slabs(gsm[k], gsm[k].ndim - 1) for k in SMALL] + [gsm[k] for k in REPL],
                                    [False] * len(SMALL) + [True] * len(REPL), name="scatter_start_small")

    out = {}

    def own_slab(src):
        return lax.dynamic_slice_in_dim(src, me, 1, axis=0)

    stacks = {k: None for k in BIG}
    after = dx
    for g in reversed(range(stages)):
        srcs, lands = _exchange_wait(scatters[g], after, name=f"scatter_wait_{g}")
        for (k, l), src, land in zip(_stage_big(g), srcs, lands):
            n_l, c2 = w[k].shape[0], land.shape[-1]
            wk, mk, vk = (held(k, arr[k]).reshape(n_l, -1, c2) for arr in (w, m, v))
            stacks[k] = _adamw(with_own(land, own_slab(src)).reshape(N_DEV, -1, c2), wk, mk, vk, l, stacks[k],
                               name=f"adamw_{k}_{l}")
            after = stacks[k][-1]
    for k in BIG:
        shp = held(k, w[k]).shape
        out[k] = [held(k, r.reshape(shp)) for r in stacks[k]]
    srcs, lands = _exchange_wait(small_scatter, after, name="scatter_wait_small")
    for n, k in enumerate(SMALL + REPL):
        own = srcs[n][None] if k in REPL else own_slab(srcs[n])
        c2 = w[k].shape[-1]
        res = _adamw(with_own(lands[n], own).reshape(N_DEV, -1, c2), w[k].reshape(1, -1, c2), m[k].reshape(1, -1, c2),
                     v[k].reshape(1, -1, c2), 0, None, name="adamw_" + k)
        out[k] = [r.reshape(w[k].shape) for r in res]

    return (loss, grad_x, *[out[k][0] for k in WEIGHTS], *[out[k][1] for k in WEIGHTS],
            *[out[k][2] for k in WEIGHTS], *[out[k][3] for k in WEIGHTS])


def kernel(x, sc_w_in, sc_conv_w, sc_conv_b, sc_w_out, lru_w_in, lru_b_in, lru_conv_w, lru_conv_b, lru_w_gate, lru_b_gate, lru_lambda, lru_w_out, ffn_w_up, ffn_conv_w, ffn_conv_b, ffn_w_down, ln_g, ln_b, loss_target, m_sc_w_in, m_sc_conv_w, m_sc_conv_b, m_sc_w_out, m_lru_w_in, m_lru_b_in, m_lru_conv_w, m_lru_conv_b, m_lru_w_gate, m_lru_b_gate, m_lru_lambda, m_lru_w_out, m_ffn_w_up, m_ffn_conv_w, m_ffn_conv_b, m_ffn_w_down, m_ln_g, m_ln_b, v_sc_w_in, v_sc_conv_w, v_sc_conv_b, v_sc_w_out, v_lru_w_in, v_lru_b_in, v_lru_conv_w, v_lru_conv_b, v_lru_w_gate, v_lru_b_gate, v_lru_lambda, v_lru_w_out, v_ffn_w_up, v_ffn_conv_w, v_ffn_conv_b, v_ffn_w_down, v_ln_g, v_ln_b):
    w = dict(sc_w_in=sc_w_in, sc_conv_w=sc_conv_w, sc_conv_b=sc_conv_b, sc_w_out=sc_w_out, lru_w_in=lru_w_in,
             lru_b_in=lru_b_in, lru_conv_w=lru_conv_w, lru_conv_b=lru_conv_b, lru_w_gate=lru_w_gate,
             lru_b_gate=lru_b_gate, lru_lambda=lru_lambda, lru_w_out=lru_w_out, ffn_w_up=ffn_w_up,
             ffn_conv_w=ffn_conv_w, ffn_conv_b=ffn_conv_b, ffn_w_down=ffn_w_down, ln_g=ln_g, ln_b=ln_b)
    m = dict(sc_w_in=m_sc_w_in, sc_conv_w=m_sc_conv_w, sc_conv_b=m_sc_conv_b, sc_w_out=m_sc_w_out, lru_w_in=m_lru_w_in,
             lru_b_in=m_lru_b_in, lru_conv_w=m_lru_conv_w, lru_conv_b=m_lru_conv_b, lru_w_gate=m_lru_w_gate,
             lru_b_gate=m_lru_b_gate, lru_lambda=m_lru_lambda, lru_w_out=m_lru_w_out, ffn_w_up=m_ffn_w_up,
             ffn_conv_w=m_ffn_conv_w, ffn_conv_b=m_ffn_conv_b, ffn_w_down=m_ffn_w_down, ln_g=m_ln_g, ln_b=m_ln_b)
    v = dict(sc_w_in=v_sc_w_in, sc_conv_w=v_sc_conv_w, sc_conv_b=v_sc_conv_b, sc_w_out=v_sc_w_out, lru_w_in=v_lru_w_in,
             lru_b_in=v_lru_b_in, lru_conv_w=v_lru_conv_w, lru_conv_b=v_lru_conv_b, lru_w_gate=v_lru_w_gate,
             lru_b_gate=v_lru_b_gate, lru_lambda=v_lru_lambda, lru_w_out=v_lru_w_out, ffn_w_up=v_ffn_w_up,
             ffn_conv_w=v_ffn_conv_w, ffn_conv_b=v_ffn_conv_b, ffn_w_down=v_ffn_w_down, ln_g=v_ln_g, ln_b=v_ln_b)
    return _step(x, loss_target, w, m, v)
```

```python
import math

import jax
import jax.numpy as jnp
from jax import lax
from jax.experimental import pallas as pl
from jax.experimental.pallas import tpu as pltpu

F32 = jnp.float32
BF16 = jnp.bfloat16

N_DEV = 8
MESH_AXES = ("x", "y", "c")
LANES = 128
SUBLANES = 8
VMEM_LIMIT = 56 * 1024 * 1024
MM_LHS_ELEMS = 3 * 1024 * 1024
MM_TN = 1536

LRU_C = 8.0
LN_EPS = 1e-5
ADAM_LR = 0.001
ADAM_B1 = 0.9
ADAM_B2 = 0.999
ADAM_EPS = 1e-08
ADAM_WD = 0.01
ADAM_STEP = 10
GELU_K = math.sqrt(2.0 / math.pi)
GELU_C = 0.044715


def _tile(n, target, align):
    if n <= target:
        return n
    t = (target // align) * align
    while t >= align:
        if n % t == 0:
            return t
        t -= align
    return n


def _params(sem):
    return pltpu.CompilerParams(dimension_semantics=sem, vmem_limit_bytes=VMEM_LIMIT)


def _rows(x):
    return lax.broadcasted_iota(jnp.int32, x.shape, 0)


def _shift_dn(x, k, fill=0.0):
    if k == 0:
        return x
    return jnp.where(_rows(x) >= k, pltpu.roll(x, k, 0), fill)


def _shift_up(x, k, fill=0.0):
    if k == 0:
        return x
    s = x.shape[0]
    return jnp.where(_rows(x) < s - k, pltpu.roll(x, s - k, 0), fill)


def _conv_fwd(x, w, b):
    kw = w.shape[0]
    y = _shift_dn(x, kw - 1) * w[0:1, :] + b
    for k in range(1, kw):
        y = y + _shift_dn(x, kw - 1 - k) * w[k:k + 1, :]
    return y


def _conv_bwd(dy, x, w):
    kw = w.shape[0]
    ahead = [_shift_up(dy, j) for j in range(kw)]
    dx = ahead[kw - 1] * w[0:1, :]
    for k in range(1, kw):
        dx = dx + ahead[kw - 1 - k] * w[k:k + 1, :]
    return dx, [_colsum(ahead[kw - 1 - k] * x) for k in range(kw)]


def _accumulate(first, items, cols=slice(None)):
    flat = []
    for ref, val in items:
        if isinstance(val, list):
            flat += [(ref, (slice(k, k + 1), cols), row) for k, row in enumerate(val)]
        else:
            flat.append((ref, Ellipsis, val))

    @pl.when(first)
    def _():
        for ref, idx, val in flat:
            ref[idx] = val

    @pl.when(jnp.logical_not(first))
    def _():
        for ref, idx, val in flat:
            ref[idx] += val


def _colsum(x):
    return jnp.sum(x, axis=0, keepdims=True)


def _sigmoid(x):
    return 1.0 / (1.0 + jnp.exp(-x))


def _log1p(x):
    u = 1.0 + x
    return jnp.where(u == 1.0, x, jnp.log(u) * (x / (u - 1.0)))


def _softplus(x):
    return jnp.maximum(x, 0.0) + _log1p(jnp.exp(-jnp.abs(x)))


def _expm1(x, ex):
    poly = x * (1.0 + x * (0.5 + x * (1.0 / 6.0 + x * (1.0 / 24.0 + x * (1.0 / 120.0 + x * (1.0 / 720.0))))))
    return jnp.where(jnp.abs(x) < 0.25, poly, ex - 1.0)


def _gelu(x):
    t = jnp.tanh(GELU_K * (x + GELU_C * x * x * x))
    return 0.5 * x * (1.0 + t)


def _gelu_and_grad(x):
    x2 = x * x
    t = jnp.tanh(GELU_K * (x + GELU_C * x * x2))
    g = 0.5 * x * (1.0 + t)
    dg = 0.5 * (1.0 + t) + 0.5 * x * (1.0 - t * t) * (GELU_K * (1.0 + 3.0 * GELU_C * x2))
    return g, dg


def _scan_fwd(a, b):
    s = a.shape[0]
    k = 1
    while k < s:
        last = 2 * k >= s
        if k % SUBLANES:
            b = a * _shift_dn(b, k) + b
            if not last:
                a = a * _shift_dn(a, k, 1.0)
        else:
            b = jnp.concatenate([b[:k], a[k:] * b[:s - k] + b[k:]], axis=0)
            if not last:
                a = jnp.concatenate([a[:k], a[k:] * a[:s - k]], axis=0)
        k *= 2
    return b


def _scan_rev(c, v):
    s = c.shape[0]
    k = 1
    while k < s:
        last = 2 * k >= s
        if k % SUBLANES:
            v = c * _shift_up(v, k) + v
            if not last:
                c = c * _shift_up(c, k, 1.0)
        else:
            v = jnp.concatenate([c[:s - k] * v[k:] + v[:s - k], v[s - k:]], axis=0)
            if not last:
                c = jnp.concatenate([c[:s - k] * c[k:], c[s - k:]], axis=0)
        k *= 2
    return v


def _mm(a, w, *, name, trans_w=False, bias=None, resid=None, resid_scale=1.0):
    m, k = a.shape
    n = w.shape[0] if trans_w else w.shape[1]
    tm = _tile(m, min(1024, max(256, MM_LHS_ELEMS // k)), SUBLANES)
    tn = _tile(n, MM_TN, LANES)
    has_bias = bias is not None
    has_resid = resid is not None

    def body(*refs):
        a_ref, w_ref = refs[0], refs[1]
        pos = 2
        b_ref = r_ref = None
        if has_bias:
            b_ref = refs[pos]
            pos += 1
        if has_resid:
            r_ref = refs[pos]
            pos += 1
        o_ref = refs[pos]

        cols = pl.ds(pl.multiple_of(pl.program_id(1) * tn, LANES), tn)
        if trans_w:
            acc = lax.dot_general(a_ref[...], w_ref[cols, :], (((1,), (1,)), ((), ())), preferred_element_type=F32)
        else:
            acc = jnp.dot(a_ref[...], w_ref[:, cols], preferred_element_type=F32)
        if has_bias:
            acc = acc + b_ref[...]
        if has_resid:
            acc = acc + resid_scale * r_ref[...]
        o_ref[...] = acc

    in_specs = [pl.BlockSpec((tm, k), lambda i, j: (i, 0)),
                pl.BlockSpec(w.shape, lambda i, j: (0, 0), pipeline_mode=pl.Buffered(1))]
    args = [a, w]
    if has_bias:
        in_specs.append(pl.BlockSpec((1, tn), lambda i, j: (0, j)))
        args.append(bias)
    if has_resid:
        in_specs.append(pl.BlockSpec((tm, tn), lambda i, j: (i, j)))
        args.append(resid)
    return pl.pallas_call(
        body, name=name, grid=(m // tm, n // tn), in_specs=in_specs,
        out_specs=pl.BlockSpec((tm, tn), lambda i, j: (i, j)),
        out_shape=jax.ShapeDtypeStruct((m, n), F32),
        compiler_params=_params(("parallel", "arbitrary")),
    )(*args)


def _ln(z, g, b):
    mu = jnp.mean(z, axis=-1, keepdims=True)
    zc = z - mu
    var = jnp.mean(zc * zc, axis=-1, keepdims=True)
    return zc * lax.rsqrt(var + LN_EPS) * g + b


def _mm_ln(a, w, resid, alpha, g, b, *, name, resid_ln=None, tm=1024):
    m, k = a.shape
    d = w.shape[1]
    tm = _tile(m, tm, SUBLANES)
    n_extra = 0 if resid_ln is None else 2

    def body(a_ref, w_ref, r_ref, g_ref, b_ref, *rest):
        z_ref, obf_ref = rest[n_extra:]
        x = r_ref[...]
        if resid_ln is not None:
            x = _ln(x, rest[0][...], rest[1][...])
        z = alpha * x + jnp.dot(a_ref[...], w_ref[...], preferred_element_type=F32)
        z_ref[...] = z
        obf_ref[...] = _ln(z, g_ref[...], b_ref[...]).astype(BF16)

    row = pl.BlockSpec((tm, d), lambda i: (i, 0))
    vec = pl.BlockSpec((1, d), lambda i: (0, 0))
    return pl.pallas_call(
        body, name=name, grid=(m // tm,),
        in_specs=[pl.BlockSpec((tm, k), lambda i: (i, 0)),
                  pl.BlockSpec((k, d), lambda i: (0, 0), pipeline_mode=pl.Buffered(1)), row, vec, vec]
        + [vec] * n_extra,
        out_specs=[row, row],
        out_shape=[jax.ShapeDtypeStruct((m, d), F32), jax.ShapeDtypeStruct((m, d), BF16)],
        compiler_params=_params(("parallel",)),
    )(a, w, resid, g, b, *(resid_ln or ()))


def _ln_bwd_math(do, z, g):
    mu = jnp.mean(z, axis=-1, keepdims=True)
    zc = z - mu
    var = jnp.mean(zc * zc, axis=-1, keepdims=True)
    rstd = lax.rsqrt(var + LN_EPS)
    xhat = zc * rstd
    dxh = do * g
    m1 = jnp.mean(dxh, axis=-1, keepdims=True)
    m2 = jnp.mean(dxh * xhat, axis=-1, keepdims=True)
    return rstd * (dxh - m1 - xhat * m2), _colsum(do * xhat), _colsum(do)


def _mm_ln_bwd(parts, w, resid, resid_scale, z, g, *, name, w_rows_are_k=False):
    t, kp = parts[0].shape
    k, d = w.shape if w_rows_are_k else w.shape[::-1]
    n = len(parts)
    tm = _tile(t, min(512, max(256, MM_LHS_ELEMS // k)), SUBLANES)

    def body(*refs):
        a_refs = refs[:n]
        w_ref, r_ref, z_ref, g_ref, dz_ref, dzbf_ref, dg_ref, db_ref = refs[n:]

        @pl.when(pl.program_id(0) == 0)
        def _():
            dg_ref[...] = jnp.zeros_like(dg_ref)
            db_ref[...] = jnp.zeros_like(db_ref)

        dx = resid_scale * r_ref[...]
        for p, a_ref in enumerate(a_refs):
            if w_rows_are_k:
                dx = dx + jnp.dot(a_ref[...], w_ref[p * kp:(p + 1) * kp, :], preferred_element_type=F32)
            else:
                dx = dx + lax.dot_general(a_ref[...], w_ref[:, p * kp:(p + 1) * kp], (((1,), (1,)), ((), ())),
                                          preferred_element_type=F32)
        dz, dg, db = _ln_bwd_math(dx, z_ref[...], g_ref[...])
        dz_ref[...] = dz
        dzbf_ref[...] = dz.astype(BF16)
        dg_ref[...] += dg
        db_ref[...] += db

    row = pl.BlockSpec((tm, d), lambda i: (i, 0))
    vec = pl.BlockSpec((1, d), lambda i: (0, 0))
    return pl.pallas_call(
        body, name=name, grid=(t // tm,),
        in_specs=[pl.BlockSpec((tm, kp), lambda i: (i, 0))] * n
        + [pl.BlockSpec(w.shape, lambda i: (0, 0), pipeline_mode=pl.Buffered(1)), row, row, vec],
        out_specs=[row, row, vec, vec],
        out_shape=[jax.ShapeDtypeStruct((t, d), F32), jax.ShapeDtypeStruct((t, d), BF16),
                   jax.ShapeDtypeStruct((1, d), F32), jax.ShapeDtypeStruct((1, d), F32)],
        compiler_params=_params(("arbitrary",)),
    )(*parts, w, resid, z, g)


def _mm_tn(a, b, *, name, below=None, tm=1408, tn=1536, tk=2048):
    t, m = a.shape
    n = b.shape[1]
    tm = _tile(m, tm, LANES)
    tn = _tile(n, tn, LANES)
    tk = _tile(t, tk, SUBLANES)
    last = t // tk - 1
    rows, earlier = (m, None) if below is None else below
    skip = (rows - m) // tm if earlier is not None else 0

    def body(a_ref, b_ref, *rest):
        o_ref, acc = rest[-2:]

        @pl.when(pl.program_id(2) == 0)
        def _():
            acc[...] = jnp.zeros_like(acc)

        acc[...] += lax.dot_general(a_ref[...], b_ref[...], (((0,), (0,)), ((), ())), preferred_element_type=F32)

        @pl.when(pl.program_id(2) == last)
        def _():
            o_ref[...] = acc[...].astype(BF16)

    in_specs = [pl.BlockSpec((tk, tm), lambda i, j, l: (l, i)), pl.BlockSpec((tk, tn), lambda i, j, l: (l, j))]
    return pl.pallas_call(
        body, name=name, grid=(m // tm, n // tn, t // tk),
        in_specs=in_specs + ([] if earlier is None else [pl.BlockSpec(memory_space=pl.ANY)]),
        out_specs=pl.BlockSpec((tm, tn), lambda i, j, l: (i + skip, j)),
        out_shape=jax.ShapeDtypeStruct((rows, n), BF16),
        input_output_aliases={} if earlier is None else {2: 0},
        scratch_shapes=[pltpu.VMEM((tm, tn), F32)],
        compiler_params=_params(("parallel", "parallel", "arbitrary")),
    )(a, b, *(() if earlier is None else (earlier,)))


def _ln_bwd(dout, z, g, *, name, tm=1024):
    t, d = z.shape
    tm = _tile(t, tm, SUBLANES)

    def body(do_ref, z_ref, g_ref, dz_ref, dzbf_ref, dg_ref, db_ref):
        @pl.when(pl.program_id(0) == 0)
        def _():
            dg_ref[...] = jnp.zeros_like(dg_ref)
            db_ref[...] = jnp.zeros_like(db_ref)

        dz, dg, db = _ln_bwd_math(do_ref[...], z_ref[...], g_ref[...])
        dz_ref[...] = dz
        dzbf_ref[...] = dz.astype(BF16)
        dg_ref[...] += dg
        db_ref[...] += db

    row = pl.BlockSpec((tm, d), lambda i: (i, 0))
    vec = pl.BlockSpec((1, d), lambda i: (0, 0))
    return pl.pallas_call(
        body, name=name, grid=(t // tm,), in_specs=[row, row, vec], out_specs=[row, row, vec, vec],
        out_shape=[jax.ShapeDtypeStruct((t, d), F32), jax.ShapeDtypeStruct((t, d), BF16),
                   jax.ShapeDtypeStruct((1, d), F32), jax.ShapeDtypeStruct((1, d), F32)],
        compiler_params=_params(("arbitrary",)),
    )(dout, z, g)


def _loss_head(z, g, b, target, *, name, tm=1024):
    t, d = z.shape
    tm = _tile(t, tm, SUBLANES)

    def body(z_ref, g_ref, b_ref, t_ref, s_ref, dy_ref):
        @pl.when(pl.program_id(0) == 0)
        def _():
            s_ref[...] = jnp.zeros_like(s_ref)

        e = _ln(z_ref[...], g_ref[...], b_ref[...]) - t_ref[...]
        dy_ref[...] = e * (1.0 / d)
        s_ref[...] += jnp.sum(_colsum(e * e), axis=-1, keepdims=True)

    row = pl.BlockSpec((tm, d), lambda i: (i, 0))
    vec = pl.BlockSpec((1, d), lambda i: (0, 0))
    return pl.pallas_call(
        body, name=name, grid=(t // tm,), in_specs=[row, vec, vec, row],
        out_specs=[pl.BlockSpec((1, LANES), lambda i: (0, 0)), row],
        out_shape=[jax.ShapeDtypeStruct((1, LANES), F32), jax.ShapeDtypeStruct((t, d), F32)],
        compiler_params=_params(("arbitrary",)),
    )(z, g, b, target)


def _own(c, b, *_):
    return c, b


def _ahead(nc, bsz):
    def at(c, b, part):
        b2 = b + jnp.minimum(part, 1)
        return jnp.minimum(c + b2 // bsz, nc - 1), b2 % bsz
    return at


def _strip(s, tc, off, at=_own):
    def index(*ids):
        c, b = at(*ids)
        return b, 0, off + c
    return pl.BlockSpec((None, s, tc), index)


def _cvec(kw, tc, off, at=_own):
    def index(*ids):
        return 0, off + at(*ids)[0]
    return pl.BlockSpec((kw, tc), index)


def _acc(kw, tc):
    return pl.BlockSpec((kw, tc), lambda c, b, *_: (0, c))


def _sc_fwd(h, cw, cb, *, name, tc=256):
    bsz, s, d3 = h.shape
    d = d3 // 3
    tc = _tile(d, tc, LANES)
    nc = d // tc

    def body(gb_ref, gc_ref, v_ref, w_ref, b_ref, q_ref):
        u = _conv_fwd(gc_ref[...] * v_ref[...], w_ref[...], b_ref[...])
        q_ref[...] = (gb_ref[...] * u).astype(BF16)

    return pl.pallas_call(
        body, name=name, grid=(nc, bsz),
        in_specs=[_strip(s, tc, 0), _strip(s, tc, nc), _strip(s, tc, 2 * nc), _cvec(cw.shape[0], tc, 0), _cvec(1, tc, 0)],
        out_specs=_strip(s, tc, 0),
        out_shape=jax.ShapeDtypeStruct((bsz, s, d), BF16),
        compiler_params=_params(("parallel", "parallel")),
    )(h, h, h, cw, cb)


def _sc_bwd(h, dq, cw, cb, *, name, tc=256):
    bsz, s, d3 = h.shape
    d = d3 // 3
    kw = cw.shape[0]
    tc = _tile(d, tc, LANES)
    nc = d // tc

    def body(gb_ref, gc_ref, v_ref, dq_ref, w_ref, b_ref, dh_ref, dw_ref, db_ref, parts):
        b_id, part = pl.program_id(1), pl.program_id(2)

        @pl.when(part == 0)
        def _():
            for lo in range(0, tc, LANES):
                sl = slice(lo, lo + LANES)
                gb, gc, v, dq_, w = gb_ref[:, sl], gc_ref[:, sl], v_ref[:, sl], dq_ref[:, sl], w_ref[:, sl]
                p = gc * v
                u = _conv_fwd(p, w, b_ref[:, sl])
                du = dq_ * gb
                dp, dw_rows = _conv_bwd(du, p, w)
                parts[0, :, sl] = (dq_ * u).astype(BF16)
                parts[1, :, sl] = (dp * v).astype(BF16)
                parts[2, :, sl] = (dp * gc).astype(BF16)
                _accumulate(b_id == 0, [(dw_ref, dw_rows), (db_ref, [_colsum(du)])], sl)

        dh_ref[...] = parts[part]

    at = _ahead(nc, bsz)
    return pl.pallas_call(
        body, name=name, grid=(nc, bsz, 3),
        in_specs=[_strip(s, tc, 0, at), _strip(s, tc, nc, at), _strip(s, tc, 2 * nc, at), _strip(s, tc, 0, at),
                  _cvec(kw, tc, 0, at), _cvec(1, tc, 0, at)],
        out_specs=[pl.BlockSpec((None, s, tc), lambda c, b, p: (b, 0, p * nc + c)), _acc(kw, tc), _acc(1, tc)],
        out_shape=[jax.ShapeDtypeStruct((bsz, s, d3), BF16), jax.ShapeDtypeStruct((kw, d), F32),
                   jax.ShapeDtypeStruct((1, d), F32)],
        scratch_shapes=[pltpu.VMEM((3, s, tc), BF16)],
        compiler_params=_params(("parallel", "arbitrary", "arbitrary")),
    )(h, h, h, dq, cw, cb)


def _ffn_specs(s, tc, nc, kw):
    strip = pl.BlockSpec((None, s, tc), lambda b, c: (b, 0, c))
    halves = [pl.BlockSpec((kw, tc), lambda b, c: (0, c)), pl.BlockSpec((kw, tc), lambda b, c: (0, nc + c)),
              pl.BlockSpec((1, tc), lambda b, c: (0, c)), pl.BlockSpec((1, tc), lambda b, c: (0, nc + c))]
    return strip, halves


def _ffn_fwd(x, w_up, cw, cb, *, name, tc=256):
    bsz, s, d = x.shape
    f = w_up.shape[0] // 2
    kw = cw.shape[0]
    tc = _tile(f, tc, LANES)
    nc = f // tc
    nt = (((1,), (1,)), ((), ()))

    def body(x_ref, w_ref, wg_ref, wv_ref, bg_ref, bv_ref, hg_ref, hv_ref, g_ref, v_ref, a_ref):
        c0 = pl.multiple_of(pl.program_id(1) * tc, LANES)
        xs = x_ref[...]
        hg = lax.dot_general(xs, w_ref[pl.ds(c0, tc), :], nt, preferred_element_type=F32)
        hv = lax.dot_general(xs, w_ref[pl.ds(f + c0, tc), :], nt, preferred_element_type=F32)
        hg_ref[...] = hg
        hv_ref[...] = hv
        for lo in range(0, tc, LANES):
            sl = slice(lo, lo + LANES)
            g = _conv_fwd(hg[:, sl], wg_ref[:, sl], bg_ref[:, sl])
            v = _conv_fwd(hv[:, sl], wv_ref[:, sl], bv_ref[:, sl])
            g_ref[:, sl] = g
            v_ref[:, sl] = v
            a_ref[:, sl] = (g * _sigmoid(g) * v).astype(BF16)

    strip, halves = _ffn_specs(s, tc, nc, kw)
    return pl.pallas_call(
        body, name=name, grid=(bsz, nc),
        in_specs=[pl.BlockSpec((None, s, d), lambda b, c: (b, 0, 0)),
                  pl.BlockSpec(w_up.shape, lambda b, c: (0, 0), pipeline_mode=pl.Buffered(1))] + halves,
        out_specs=[strip] * 5,
        out_shape=[jax.ShapeDtypeStruct((bsz, s, f), F32)] * 4 + [jax.ShapeDtypeStruct((bsz, s, f), BF16)],
        compiler_params=_params(("parallel", "arbitrary")),
    )(x, w_up, cw, cw, cb, cb)


def _ffn_bwd(hg, hv, g, v, dz, w_down, cw, *, name, tc=256):
    bsz, s, f = hg.shape
    d = dz.shape[2]
    kw = cw.shape[0]
    tc = _tile(f, tc, LANES)
    nc = f // tc

    def body(hg_ref, hv_ref, g_ref, v_ref, dz_ref, wd_ref, wg_ref, wv_ref,
             dhg_ref, dhv_ref, dwg_ref, dwv_ref, dbg_ref, dbv_ref):
        c0 = pl.multiple_of(pl.program_id(1) * tc, LANES)
        cols = pl.ds(c0, tc)
        da = lax.dot_general(dz_ref[...], wd_ref[cols, :], (((1,), (1,)), ((), ())), preferred_element_type=F32)
        for lo in range(0, tc, LANES):
            sl = slice(lo, lo + LANES)
            g_, da_ = g_ref[:, sl], da[:, sl]
            sg = _sigmoid(g_)
            dv = da_ * (g_ * sg)
            dg = da_ * v_ref[:, sl] * (sg * (1.0 + g_ * (1.0 - sg)))
            dhg, dwg_rows = _conv_bwd(dg, hg_ref[:, sl], wg_ref[:, sl])
            dhv, dwv_rows = _conv_bwd(dv, hv_ref[:, sl], wv_ref[:, sl])
            dhg_ref[:, sl] = dhg.astype(BF16)
            dhv_ref[:, sl] = dhv.astype(BF16)
            _accumulate(pl.program_id(0) == 0, [(dwg_ref, dwg_rows), (dwv_ref, dwv_rows),
                                                (dbg_ref, [_colsum(dg)]), (dbv_ref, [_colsum(dv)])],
                        pl.ds(pl.multiple_of(c0 + lo, LANES), LANES))

    strip, halves = _ffn_specs(s, tc, nc, kw)
    whole = lambda r: pl.BlockSpec((r, f), lambda b, c: (0, 0))
    return pl.pallas_call(
        body, name=name, grid=(bsz, nc),
        in_specs=[strip] * 4 + [pl.BlockSpec((None, s, d), lambda b, c: (b, 0, 0)),
                                pl.BlockSpec(w_down.shape, lambda b, c: (0, 0), pipeline_mode=pl.Buffered(1))]
        + halves[:2],
        out_specs=[strip, strip, whole(kw), whole(kw), whole(1), whole(1)],
        out_shape=[jax.ShapeDtypeStruct((bsz, s, f), BF16), jax.ShapeDtypeStruct((bsz, s, f), BF16),
                   jax.ShapeDtypeStruct((kw, f), F32), jax.ShapeDtypeStruct((kw, f), F32),
                   jax.ShapeDtypeStruct((1, f), F32), jax.ShapeDtypeStruct((1, f), F32)],
        compiler_params=_params(("arbitrary", "arbitrary")),
    )(hg, hv, g, v, dz, w_down, cw, cw)


def _lru_gates(r, cw, cb, wg, bg, lam):
    blk = r.shape[1]
    xr = _conv_fwd(r, cw, cb)
    gates = jnp.dot(xr.astype(BF16), wg, preferred_element_type=F32) + bg
    rg = _sigmoid(gates[:, :blk])
    ig = _sigmoid(gates[:, blk:])
    sp = _softplus(-lam)
    la = (-LRU_C * sp) * rg
    a = jnp.exp(la)
    mult = jnp.sqrt(-_expm1(2.0 * la, a * a))
    return xr, rg, ig, sp, a, mult


def _lru_fwd(h, cw, cb, wg, bg, lam, *, name):
    bsz, s, r2 = h.shape
    heads, blk = wg.shape[0], wg.shape[1]
    kw = cw.shape[0]

    def body(g_ref, r_ref, cw_ref, cb_ref, wg_ref, bg_ref, lam_ref, y_ref, sv_ref):
        xr, rg, ig, _, a, mult = _lru_gates(r_ref[...], cw_ref[...], cb_ref[...], wg_ref[...], bg_ref[...], lam_ref[...])
        hs = _scan_fwd(a, mult * (ig * xr))
        for n, val in enumerate((hs, xr, rg, ig, a, mult)):
            sv_ref[n] = val
        y_ref[...] = (hs * _gelu(g_ref[...])).astype(BF16)

    per_head = lambda hd, b: (hd, 0, 0)
    return pl.pallas_call(
        body, name=name, grid=(heads, bsz),
        in_specs=[_strip(s, blk, 0), _strip(s, blk, heads), _cvec(kw, blk, 0), _cvec(1, blk, 0),
                  pl.BlockSpec((None, blk, 2 * blk), per_head), pl.BlockSpec((None, 1, 2 * blk), per_head),
                  _cvec(1, blk, 0)],
        out_specs=[_strip(s, blk, 0), pl.BlockSpec((6, None, s, blk), lambda hd, b: (0, b, 0, hd))],
        out_shape=[jax.ShapeDtypeStruct((bsz, s, r2 // 2), BF16), jax.ShapeDtypeStruct((6, bsz, s, r2 // 2), F32)],
        compiler_params=_params(("parallel", "parallel")),
    )(h, h, cw, cb, wg, bg, lam)


def _lru_bwd(h, sv, dy, cw, wg, lam, *, name):
    bsz, s, r2 = h.shape
    rw = r2 // 2
    heads, blk = wg.shape[0], wg.shape[1]
    kw = cw.shape[0]

    def body(g_ref, r_ref, cw_ref, wg_ref, lam_ref, sv_ref, dy_ref,
             dh_ref, dcw_ref, dcb_ref, dwg_ref, dbg_ref, dlam_ref, sg_ref, sr_ref, parts):
        b_id, part = pl.program_id(1), pl.program_id(2)

        @pl.when(part == 0)
        def _():
            r, cw_, wg_, lam_ = r_ref[...], cw_ref[...], wg_ref[...], lam_ref[...]
            hs_, xr, rg, ig, a, mult = (sv_ref[n] for n in range(6))
            sp = _softplus(-lam_)
            dy_ = dy_ref[...]
            gel, dgel = _gelu_and_grad(g_ref[...])
            dg = dy_ * hs_ * dgel
            lmb = _scan_rev(_shift_up(a, 1, 1.0), dy_ * gel)
            da = lmb * _shift_dn(hs_, 1)
            dmult = lmb * (ig * xr)
            dig = lmb * (mult * xr)
            dxr = lmb * (mult * ig)
            dla = da * a - dmult * (a * a / mult)
            drg = dla * (-LRU_C * sp)
            dsp = _colsum(dla * rg) * (-LRU_C)
            dlam = -dsp * _sigmoid(-lam_)
            dgates = jnp.concatenate([drg * (rg * (1.0 - rg)), dig * (ig * (1.0 - ig))], axis=1)
            dgates_bf = dgates.astype(BF16)
            dwg = lax.dot_general(xr.astype(BF16), dgates_bf, (((0,), (0,)), ((), ())), preferred_element_type=F32)
            dxr = dxr + lax.dot_general(dgates_bf, wg_, (((1,), (1,)), ((), ())), preferred_element_type=F32)
            dr, dcw_rows = _conv_bwd(dxr, r, cw_)
            parts[0] = dg.astype(BF16)
            parts[1] = dr.astype(BF16)
            _accumulate(b_id == 0, [(dcw_ref, dcw_rows), (dcb_ref, _colsum(dxr)), (dwg_ref, dwg),
                                    (dbg_ref, _colsum(dgates)), (dlam_ref, dlam), (sg_ref, _colsum(dg)),
                                    (sr_ref, _colsum(dr))])

        dh_ref[...] = parts[part]

    at = _ahead(heads, bsz)

    def saved(*ids):
        hd, b = at(*ids)
        return 0, b, 0, hd

    vec = pl.BlockSpec((1, blk), lambda hd, b, p: (0, hd))
    return pl.pallas_call(
        body, name=name, grid=(heads, bsz, 2),
        in_specs=[_strip(s, blk, 0, at), _strip(s, blk, heads, at), _cvec(kw, blk, 0, at),
                  pl.BlockSpec((None, blk, 2 * blk), lambda *ids: (at(*ids)[0], 0, 0)), _cvec(1, blk, 0, at),
                  pl.BlockSpec((6, None, s, blk), saved), _strip(s, blk, 0, at)],
        out_specs=[pl.BlockSpec((None, s, blk), lambda hd, b, p: (b, 0, p * heads + hd)),
                   pl.BlockSpec((kw, blk), lambda hd, b, p: (0, hd)), vec,
                   pl.BlockSpec((None, blk, 2 * blk), lambda hd, b, p: (hd, 0, 0)),
                   pl.BlockSpec((None, 1, 2 * blk), lambda hd, b, p: (hd, 0, 0)), vec, vec, vec],
        out_shape=[jax.ShapeDtypeStruct((bsz, s, r2), BF16), jax.ShapeDtypeStruct((kw, rw), F32),
                   jax.ShapeDtypeStruct((1, rw), F32), jax.ShapeDtypeStruct((heads, blk, 2 * blk), F32),
                   jax.ShapeDtypeStruct((heads, 1, 2 * blk), F32), jax.ShapeDtypeStruct((1, rw), F32),
                   jax.ShapeDtypeStruct((1, rw), F32), jax.ShapeDtypeStruct((1, rw), F32)],
        scratch_shapes=[pltpu.VMEM((2, s, blk), BF16)],
        compiler_params=_params(("parallel", "arbitrary", "arbitrary")),
    )(h, h, cw, wg, lam, sv, dy)


HBM_SPEC = pl.BlockSpec(memory_space=pltpu.HBM)
SEM_SPEC = pl.BlockSpec(memory_space=pltpu.SEMAPHORE)
EFFECT = pltpu.SideEffectType.DATAFLOW_SIDE_EFFECTING


def _peer_copies(srcs, lands, gather, send_sem, recv_sem):
    x, y, c = (lax.axis_index(ax) for ax in MESH_AXES)
    me = 4 * x + 2 * y + c
    copies = []
    for i in range(len(srcs)):
        for d in range(1, N_DEV):
            px = 1 - x if d & 4 else x
            py = 1 - y if d & 2 else y
            pc = 1 - c if d & 1 else c
            src = srcs[i] if gather[i] else srcs[i].at[4 * px + 2 * py + pc]
            k = i * (N_DEV - 1) + d - 1
            copies.append(pltpu.make_async_remote_copy(
                src_ref=src, dst_ref=lands[i].at[me], send_sem=send_sem.at[k], recv_sem=recv_sem.at[k],
                device_id=(px, py, pc), device_id_type=pl.DeviceIdType.MESH))
    return copies


def _exchange_start(arrs, gather, *, name):
    n = len(arrs)
    lands = [lax.empty((N_DEV,) + tuple(a.shape if g else a.shape[1:]), a.dtype) for a, g in zip(arrs, gather)]

    def body(*refs):
        srcs, land_refs = refs[:n], refs[n:2 * n]
        send_sem, recv_sem = refs[2 * n], refs[2 * n + 1]
        token = refs[-1]
        for cp in _peer_copies(srcs, land_refs, gather, send_sem, recv_sem):
            cp.start()
        token[...] = jnp.zeros_like(token)

    sems = pltpu.SemaphoreType.DMA((n * (N_DEV - 1),))
    thru = [pltpu.HBM(a.shape, a.dtype) for a in arrs + lands]
    out = pl.pallas_call(
        body, name=name, in_specs=[HBM_SPEC] * (2 * n),
        out_shape=(sems, sems, *thru, jax.ShapeDtypeStruct((SUBLANES, LANES), F32)),
        out_specs=(SEM_SPEC, SEM_SPEC, *([HBM_SPEC] * (2 * n)), pl.BlockSpec(memory_space=pltpu.VMEM)),
        input_output_aliases={i: 2 + i for i in range(2 * n)},
        compiler_params=pltpu.CompilerParams(has_side_effects=EFFECT),
    )(*[pltpu.with_memory_space_constraint(a, pltpu.HBM) for a in arrs + lands])
    return {"send_sem": out[0], "recv_sem": out[1], "srcs": list(out[2:2 + n]), "lands": list(out[2 + n:2 + 2 * n]),
            "token": out[-1], "gather": list(gather)}


def _exchange_wait(handle, after, *, name):
    srcs, lands, gather = handle["srcs"], handle["lands"], handle["gather"]
    n = len(srcs)

    def body(*refs):
        src_refs, land_refs = refs[:n], refs[n:2 * n]
        send_sem, recv_sem = refs[2 * n], refs[2 * n + 1]
        for cp in _peer_copies(src_refs, land_refs, gather, send_sem, recv_sem):
            cp.wait_send()
            cp.wait_recv()

    out = pl.pallas_call(
        body, name=name,
        in_specs=[HBM_SPEC] * (2 * n) + [SEM_SPEC, SEM_SPEC, pl.BlockSpec(memory_space=pl.ANY)],
        out_shape=tuple(pltpu.HBM(a.shape, a.dtype) for a in srcs + lands), out_specs=tuple([HBM_SPEC] * (2 * n)),
        input_output_aliases={i: i for i in range(2 * n)},
        compiler_params=pltpu.CompilerParams(has_side_effects=EFFECT),
    )(*srcs, *lands, handle["send_sem"], handle["recv_sem"], after)
    return list(out[:n]), list(out[n:])


def _layers_bf16(stacks, *, name):
    counts = [a.shape[0] for a in stacks]

    def body(*refs):
        outs = iter(refs[len(stacks):])
        for i_ref, n_layers in zip(refs, counts):
            for layer in range(n_layers):
                next(outs)[...] = i_ref[layer].astype(BF16)

    flat = pl.pallas_call(
        body, name=name,
        out_shape=[jax.ShapeDtypeStruct(a.shape[1:], BF16) for a in stacks for _ in range(a.shape[0])],
        compiler_params=pltpu.CompilerParams(vmem_limit_bytes=VMEM_LIMIT),
    )(*stacks)
    split, pos = [], 0
    for n_layers in counts:
        split.append(list(flat[pos:pos + n_layers]))
        pos += n_layers
    return split


def _adamw(parts, w, m, v, layer, so_far, *, name, tr=512):
    n_layers, r, c = w.shape
    tr = _tile(r, tr, SUBLANES)
    bc1 = 1.0 / (1.0 - ADAM_B1 ** ADAM_STEP)
    bc2 = 1.0 / (1.0 - ADAM_B2 ** ADAM_STEP)
    if so_far is None:
        so_far = [lax.empty(w.shape, F32) for _ in range(4)]

    def body(p_ref, w_ref, m_ref, v_ref, *rest):
        g_ref, d_ref, mo_ref, vo_ref = rest[4:]
        g = p_ref[0].astype(F32)
        for s in range(1, N_DEV):
            g = g + p_ref[s].astype(F32)
        m_new = ADAM_B1 * m_ref[...] + (1.0 - ADAM_B1) * g
        v_new = ADAM_B2 * v_ref[...] + (1.0 - ADAM_B2) * (g * g)
        g_ref[...] = g
        mo_ref[...] = m_new
        vo_ref[...] = v_new
        d_ref[...] = -ADAM_LR * ((m_new * bc1) / (jnp.sqrt(v_new * bc2) + ADAM_EPS) + ADAM_WD * w_ref[...])

    blk = pl.BlockSpec((None, tr, c), lambda i: (layer, i, 0))
    return pl.pallas_call(
        body, name=name, grid=(r // tr,),
        in_specs=[pl.BlockSpec((N_DEV, tr, c), lambda i: (0, i, 0)), blk, blk, blk]
        + [pl.BlockSpec(memory_space=pl.ANY)] * 4,
        out_specs=[blk] * 4, out_shape=[jax.ShapeDtypeStruct(w.shape, F32)] * 4,
        input_output_aliases={4 + o: o for o in range(4)},
        compiler_params=_params(("parallel",)),
    )(parts, w, m, v, *so_far)


def _whole(slabs, axis):
    x = jnp.moveaxis(slabs, 0, axis)
    shp = x.shape
    return x.reshape(shp[:axis] + (shp[axis] * shp[axis + 1],) + shp[axis + 2:])


def _slabs(whole, axis):
    shp = whole.shape
    x = whole.reshape(shp[:axis] + (N_DEV, shp[axis] // N_DEV) + shp[axis + 1:])
    return jnp.moveaxis(x, axis, 0)


BIG = {"sc_w_in": 2, "sc_w_out": 1, "lru_w_in": 2, "lru_w_gate": 3, "lru_w_out": 1, "ffn_w_up": 2, "ffn_w_down": 1}
TRANSPOSED = ("ffn_w_up", "lru_w_in")
SMALL = ["sc_conv_w", "lru_b_in", "lru_conv_w", "lru_conv_b", "lru_b_gate", "lru_lambda", "ffn_conv_w", "ln_g", "ln_b"]
REPL = ["sc_conv_b", "ffn_conv_b"]
WEIGHTS = ["sc_w_in", "sc_conv_w", "sc_conv_b", "sc_w_out", "lru_w_in", "lru_b_in", "lru_conv_w", "lru_conv_b",
           "lru_w_gate", "lru_b_gate", "lru_lambda", "lru_w_out", "ffn_w_up", "ffn_conv_w", "ffn_conv_b", "ffn_w_down",
           "ln_g", "ln_b"]


STAGES_PER_LAYER = 3


def _stage_big(g):
    i, part = divmod(g, STAGES_PER_LAYER)
    j = i // 2
    if part:
        return [("ffn_w_up" if part == 1 else "ffn_w_down", i)]
    return [("sc_w_in", j), ("sc_w_out", j)] if i % 2 == 0 else [("lru_w_in", j), ("lru_w_gate", j), ("lru_w_out", j)]


def _step(x, loss_target, w, m, v):
    bsz, s, d = x.shape
    t = bsz * s
    depth = w["ffn_w_up"].shape[0]
    alpha = (2.0 * depth) ** 0.25
    heads = w["lru_w_gate"].shape[1]

    me = 4 * lax.axis_index("x") + 2 * lax.axis_index("y") + lax.axis_index("c")

    def with_own(land, own):
        return lax.dynamic_update_slice_in_dim(land, own, me, axis=0)

    stages = STAGES_PER_LAYER * depth
    def held(k, arr):
        return jnp.swapaxes(arr, -1, -2) if k in TRANSPOSED else arr

    def split_axis(k):
        return 0 if k in TRANSPOSED else BIG[k] - 1

    flat_names = [k for k in BIG if w[k].ndim == 3]
    wb = dict(zip(flat_names, _layers_bf16([held(k, w[k]) for k in flat_names], name="weights_bf16")))
    wb.update({k: list(w[k].astype(BF16)) for k in BIG if k not in flat_names})

    gathers, tok = [], None
    for g in range(stages):
        arrs = [wb[k][l] for k, l in _stage_big(g)]
        if g == 0:
            arrs += [w[k] for k in SMALL]
        if tok is not None:
            arrs[0] = arrs[0] + tok.astype(BF16)
        gathers.append(_exchange_start(arrs, [True] * len(arrs), name=f"gather_start_{g}"))
        tok = gathers[-1]["token"][0, 0]
    full = {k: [None] * w[k].shape[0] for k in BIG}
    full["sc_conv_b"] = w["sc_conv_b"]
    full["ffn_conv_b"] = w["ffn_conv_b"]

    def arrive(g, after):
        srcs, lands = _exchange_wait(gathers[g], after, name=f"gather_wait_{g}")
        for (k, l), src, land in zip(_stage_big(g), srcs, lands):
            full[k][l] = _whole(with_own(land, src[None]), split_axis(k))
        if g == 0:
            n_big = len(_stage_big(0))
            for k, src, land in zip(SMALL, srcs[n_big:], lands[n_big:]):
                full[k] = _whole(with_own(land, src[None]), w[k].ndim - 1)

    stream, stream_ln = x.reshape(t, d), None
    xb = stream.astype(BF16)
    saved = []
    for i in range(depth):
        j = i // 2
        arrive(3 * i, gathers[-1]["token"] if i == 0 else xb)
        lng, lnb = full["ln_g"][i], full["ln_b"][i]
        sv = {"x0": xb}
        if i % 2 == 0:
            hm = _mm(xb, full["sc_w_in"][j], name="sc_in")
            q = _sc_fwd(hm.reshape(bsz, s, -1), full["sc_conv_w"][j], full["sc_conv_b"][j:j + 1], name="sc_mix")
            w_out = full["sc_w_out"][j]
        else:
            hm = _mm(xb, full["lru_w_in"][j], trans_w=True, bias=full["lru_b_in"][j:j + 1], name="lru_in")
            q, hs = _lru_fwd(hm.reshape(bsz, s, -1), full["lru_conv_w"][j], full["lru_conv_b"][j:j + 1],
                             full["lru_w_gate"][j], full["lru_b_gate"][j].reshape(heads, 1, -1),
                             full["lru_lambda"][j:j + 1], name="lru_mix")
            sv["hs"] = hs
            w_out = full["lru_w_out"][j]
        q = q.reshape(t, -1)
        arrive(3 * i + 1, q)
        z1, x1b = _mm_ln(q, w_out, stream, alpha, lng[0:1], lnb[0:1], resid_ln=stream_ln, name="mix_out_ln")
        hg, hv, gc, vc, a = _ffn_fwd(x1b.reshape(bsz, s, d), full["ffn_w_up"][i], full["ffn_conv_w"][i],
                                     full["ffn_conv_b"][i:i + 1], name="ffn_up_act")
        a = a.reshape(t, -1)
        arrive(3 * i + 2, a)
        z2, xb = _mm_ln(a, full["ffn_w_down"][i], z1, alpha, lng[1:2], lnb[1:2], resid_ln=(lng[0:1], lnb[0:1]),
                        name="ffn_down_ln")
        stream, stream_ln = z2, (lng[1:2], lnb[1:2])
        sv.update(hm=hm, q=q, z1=z1, x1=x1b, ffn=(hg, hv, gc, vc), a=a, z2=z2)
        saved.append(sv)

    sq, dx = _loss_head(stream, *stream_ln, loss_target.reshape(t, d), name="loss_head")
    loss = lax.psum((0.5 / d) * sq[0, 0], MESH_AXES)

    grads = {k: [None] * w[k].shape[0] for k in WEIGHTS}
    scatters = [None] * stages

    def depart(g):
        send = [_slabs(grads[k][l], split_axis(k)).astype(BF16) for k, l in _stage_big(g)]
        scatters[g] = _exchange_start(send, [False] * len(send), name=f"scatter_start_{g}")
        return scatters[g]["token"][0:1, 0:1]

    dz2, dz2b, dg2, db2 = _ln_bwd(dx, saved[-1]["z2"], full["ln_g"][-1][1:2], name="ln_bwd")
    for i in reversed(range(depth)):
        j = i // 2
        sv = saved[i]
        lng = full["ln_g"][i]
        grads["ffn_w_down"][i] = _mm_tn(sv["a"], dz2b, name="ffn_down_dw")
        dhg, dhv, dwg, dwv, dbg, dbv = _ffn_bwd(*sv["ffn"], dz2b.reshape(bsz, s, d), full["ffn_w_down"][i],
                                                full["ffn_conv_w"][i] + depart(3 * i + 2), name="ffn_act_bwd")
        dhg, dhv = dhg.reshape(t, -1), dhv.reshape(t, -1)
        grads["ffn_conv_w"][i] = jnp.concatenate([dwg, dwv], axis=1)
        grads["ffn_conv_b"][i] = jnp.concatenate([dbg, dbv], axis=1)[0]
        rows_up = 2 * dhg.shape[1]
        dw_g = _mm_tn(dhg, sv["x1"], below=(rows_up, None), name="ffn_up_dw_g")
        grads["ffn_w_up"][i] = _mm_tn(dhv, sv["x1"], below=(rows_up, dw_g), name="ffn_up_dw_v")
        dz1, dz1b, dg1, db1 = _mm_ln_bwd([dhg, dhv], full["ffn_w_up"][i], dz2, alpha, sv["z1"],
                                         lng[0:1] + depart(3 * i + 1), name="ffn_up_dx_ln", w_rows_are_k=True)
        grads["ln_g"][i] = jnp.concatenate([dg1, dg2], axis=0)
        grads["ln_b"][i] = jnp.concatenate([db1, db2], axis=0)
        if i % 2 == 0:
            dq = _mm(dz1b, full["sc_w_out"][j], trans_w=True, name="sc_out_dx")
            grads["sc_w_out"][j] = _mm_tn(sv["q"], dz1b, name="sc_out_dw")
            dhm, dcw, dcb = _sc_bwd(sv["hm"].reshape(bsz, s, -1), dq.reshape(bsz, s, -1), full["sc_conv_w"][j],
                                    full["sc_conv_b"][j:j + 1], name="sc_mix_bwd")
            dhm = dhm.reshape(t, -1)
            grads["sc_conv_w"][j] = dcw
            grads["sc_conv_b"][j] = dcb[0]
            grads["sc_w_in"][j] = _mm_tn(sv["x0"], dhm, name="sc_in_dw")
            w_in = full["sc_w_in"][j]
        else:
            dq = _mm(dz1b, full["lru_w_out"][j], trans_w=True, name="lru_out_dx")
            grads["lru_w_out"][j] = _mm_tn(sv["q"], dz1b, name="lru_out_dw")
            dhm, dcw, dcb, dwgt, dbgt, dlam, sgb, srb = _lru_bwd(
                sv["hm"].reshape(bsz, s, -1), sv["hs"], dq.reshape(bsz, s, -1), full["lru_conv_w"][j],
                full["lru_w_gate"][j], full["lru_lambda"][j:j + 1], name="lru_mix_bwd")
            dhm = dhm.reshape(t, -1)
            grads["lru_conv_w"][j] = dcw
            grads["lru_conv_b"][j] = dcb[0]
            grads["lru_w_gate"][j] = dwgt
            grads["lru_b_gate"][j] = dbgt[:, 0, :]
            grads["lru_lambda"][j] = dlam[0]
            grads["lru_b_in"][j] = jnp.concatenate([sgb, srb], axis=1)[0]
            grads["lru_w_in"][j] = _mm_tn(dhm, sv["x0"], name="lru_in_dw")
            w_in = full["lru_w_in"][j]
        tok = depart(3 * i)
        if i > 0:
            dz2, dz2b, dg2, db2 = _mm_ln_bwd([dhm], w_in, dz1, alpha, saved[i - 1]["z2"], full["ln_g"][i - 1][1:2] + tok,
                                             name="mix_in_dx_ln", w_rows_are_k=i % 2 == 1)
        else:
            dx = _mm(dhm, w_in + tok[0, 0].astype(BF16), trans_w=True, resid=dz1, resid_scale=alpha, name="mix_in_dx")
    grad_x = dx.reshape(bsz, s, d)

    gsm = {k: jnp.stack(grads[k]) for k in SMALL + REPL}
    small_scatter = _exchange_start([_slabs(gsm[k], gsm[k].ndim - 1) for k in SMALL] + [gsm[k] for k in REPL],
                                    [False] * len(SMALL) + [True] * len(REPL), name="scatter_start_small")

    out = {}

    def own_slab(src):
        return lax.dynamic_slice_in_dim(src, me, 1, axis=0)

    stacks = {k: None for k in BIG}
    after = dx
    for g in reversed(range(stages)):
        srcs, lands = _exchange_wait(scatters[g], after, name=f"scatter_wait_{g}")
        for (k, l), src, land in zip(_stage_big(g), srcs, lands):
            n_l, c2 = w[k].shape[0], land.shape[-1]
            wk, mk, vk = (held(k, arr[k]).reshape(n_l, -1, c2) for arr in (w, m, v))
            stacks[k] = _adamw(with_own(land, own_slab(src)).reshape(N_DEV, -1, c2), wk, mk, vk, l, stacks[k],
                               name=f"adamw_{k}_{l}")
            after = stacks[k][-1]
    for k in BIG:
        shp = held(k, w[k]).shape
        out[k] = [held(k, r.reshape(shp)) for r in stacks[k]]
    srcs, lands = _exchange_wait(small_scatter, after, name="scatter_wait_small")
    for n, k in enumerate(SMALL + REPL):
        own = srcs[n][None] if k in REPL else own_slab(srcs[n])
        c2 = w[k].shape[-1]
        res = _adamw(with_own(lands[n], own).reshape(N_DEV, -1, c2), w[k].reshape(1, -1, c2), m[k].reshape(1, -1, c2),
                     v[k].reshape(1, -1, c2), 0, None, name="adamw_" + k)
        out[k] = [r.reshape(w[k].shape) for r in res]

    return (loss, grad_x, *[out[k][0] for k in WEIGHTS], *[out[k][1] for k in WEIGHTS],
            *[out[k][2] for k in WEIGHTS], *[out[k][3] for k in WEIGHTS])


def kernel(x, sc_w_in, sc_conv_w, sc_conv_b, sc_w_out, lru_w_in, lru_b_in, lru_conv_w, lru_conv_b, lru_w_gate, lru_b_gate, lru_lambda, lru_w_out, ffn_w_up, ffn_conv_w, ffn_conv_b, ffn_w_down, ln_g, ln_b, loss_target, m_sc_w_in, m_sc_conv_w, m_sc_conv_b, m_sc_w_out, m_lru_w_in, m_lru_b_in, m_lru_conv_w, m_lru_conv_b, m_lru_w_gate, m_lru_b_gate, m_lru_lambda, m_lru_w_out, m_ffn_w_up, m_ffn_conv_w, m_ffn_conv_b, m_ffn_w_down, m_ln_g, m_ln_b, v_sc_w_in, v_sc_conv_w, v_sc_conv_b, v_sc_w_out, v_lru_w_in, v_lru_b_in, v_lru_conv_w, v_lru_conv_b, v_lru_w_gate, v_lru_b_gate, v_lru_lambda, v_lru_w_out, v_ffn_w_up, v_ffn_conv_w, v_ffn_conv_b, v_ffn_w_down, v_ln_g, v_ln_b):
    w = dict(sc_w_in=sc_w_in, sc_conv_w=sc_conv_w, sc_conv_b=sc_conv_b, sc_w_out=sc_w_out, lru_w_in=lru_w_in,
             lru_b_in=lru_b_in, lru_conv_w=lru_conv_w, lru_conv_b=lru_conv_b, lru_w_gate=lru_w_gate,
             lru_b_gate=lru_b_gate, lru_lambda=lru_lambda, lru_w_out=lru_w_out, ffn_w_up=ffn_w_up,
             ffn_conv_w=ffn_conv_w, ffn_conv_b=ffn_conv_b, ffn_w_down=ffn_w_down, ln_g=ln_g, ln_b=ln_b)
    m = dict(sc_w_in=m_sc_w_in, sc_conv_w=m_sc_conv_w, sc_conv_b=m_sc_conv_b, sc_w_out=m_sc_w_out, lru_w_in=m_lru_w_in,
             lru_b_in=m_lru_b_in, lru_conv_w=m_lru_conv_w, lru_conv_b=m_lru_conv_b, lru_w_gate=m_lru_w_gate,
             lru_b_gate=m_lru_b_gate, lru_lambda=m_lru_lambda, lru_w_out=m_lru_w_out, ffn_w_up=m_ffn_w_up,
             ffn_conv_w=m_ffn_conv_w, ffn_conv_b=m_ffn_conv_b, ffn_w_down=m_ffn_w_down, ln_g=m_ln_g, ln_b=m_ln_b)
    v = dict(sc_w_in=v_sc_w_in, sc_conv_w=v_sc_conv_w, sc_conv_b=v_sc_conv_b, sc_w_out=v_sc_w_out, lru_w_in=v_lru_w_in,
             lru_b_in=v_lru_b_in, lru_conv_w=v_lru_conv_w, lru_conv_b=v_lru_conv_b, lru_w_gate=v_lru_w_gate,
             lru_b_gate=v_lru_b_gate, lru_lambda=v_lru_lambda, lru_w_out=v_lru_w_out, ffn_w_up=v_ffn_w_up,
             ffn_conv_w=v_ffn_conv_w, ffn_conv_b=v_ffn_conv_b, ffn_w_down=v_ffn_w_down, ln_g=v_ln_g, ln_b=v_ln_b)
    return _step(x, loss_target, w, m, v)
```
